```python
import math
import jax, jax.numpy as jnp
from jax import lax
import numpy as np

D_MODEL = 1024
BATCH = 4
SEQ = 4096
DEPTH = 2

GRID_W = 64
CTX_LEN = 256
N_EVEN = (DEPTH + 1) // 2
N_ODD = DEPTH // 2

S5_WIDTH = D_MODEL // 2
S5_GROUP = 16
S5_GROUPS = S5_WIDTH // S5_GROUP
S5_STATE = 64
SGU_WIDTH = D_MODEL // 2
SGU_HEADS = 8
SGU_HEAD_DIM = SGU_WIDTH // SGU_HEADS
CHUNK = 128
EVEN_IN = 2 * S5_WIDTH + 3 * SGU_WIDTH
EVEN_SPLITS = [S5_WIDTH, 2 * S5_WIDTH, 2 * S5_WIDTH + SGU_WIDTH, 2 * S5_WIDTH + 2 * SGU_WIDTH]
EVEN_OUT = S5_WIDTH + SGU_WIDTH
CONV_WIDTH = D_MODEL
CONV_K = 31
ODD_IN = 3 * CONV_WIDTH
DN_ALPHA = (2 * DEPTH) ** 0.25
DN_BETA = (8 * DEPTH) ** -0.25
LN_EPS = 1e-5

kernel_name = "hybrid_s5_sgu_conformer_prefix_dit"


def layer_norm(x, g, b):
    xf = x.astype(jnp.float32)
    mu = jnp.mean(xf, axis=-1, keepdims=True)
    var = jnp.mean(jnp.square(xf - mu), axis=-1, keepdims=True)
    y = (xf - mu) * lax.rsqrt(var + LN_EPS) * g.astype(jnp.float32) + b.astype(jnp.float32)
    return y.astype(x.dtype)


def adaln(cond, mod_w, mod_b):
    m = jax.nn.silu(cond) @ mod_w + mod_b
    return jnp.split(m, 3, axis=-1)


def modulate(x, shift, scale):
    return x * (1 + scale) + shift


def s5_discretise(lam_re, lam_im, log_dt, b_re, b_im):
    lam = lax.complex(lam_re.astype(jnp.float32), lam_im.astype(jnp.float32))
    dt = jnp.exp(log_dt.astype(jnp.float32))[:, None]
    lam_bar = jnp.exp(lam * dt)
    b = lax.complex(b_re.astype(jnp.float32), b_im.astype(jnp.float32))
    b_bar = ((lam_bar - 1) / lam)[..., None] * b
    return lam_bar, b_bar


def s5_scan(lam_bar, bu):
    a = jnp.broadcast_to(lam_bar, bu.shape)

    def combine(e1, e2):
        a1, b1 = e1
        a2, b2 = e2
        return a1 * a2, a2 * b1 + b2

    _, h = lax.associative_scan(combine, (a, bu), axis=1)
    return h


def s5_mixer(u_lat, u_ctx, ctx_out, lam_re, lam_im, log_dt, b_re, b_im, c_re, c_im, d_skip):
    bsz, n, _ = u_lat.shape
    n_c = u_ctx.shape[1]
    ul = u_lat.astype(jnp.float32).reshape(bsz, n, S5_GROUPS, S5_GROUP)
    uc = u_ctx.astype(jnp.float32).reshape(bsz, n_c, S5_GROUPS, S5_GROUP)
    d = d_skip.astype(jnp.float32).reshape(S5_GROUPS, S5_GROUP)
    y_lat = d * ul
    y_ctx = d * uc if ctx_out else None
    for dirn in range(2):
        lam_bar, b_bar = s5_discretise(lam_re[dirn], lam_im[dirn], log_dt[dirn], b_re[dirn], b_im[dirn])
        c_mat = lax.complex(c_re[dirn].astype(jnp.float32), c_im[dirn].astype(jnp.float32))
        bu_l = jnp.einsum('blgh,gph->blgp', ul, b_bar)
        bu_c = jnp.einsum('blgh,gph->blgp', uc, b_bar)
        if dirn == 1:
            bu_l, bu_c = bu_l[:, ::-1], bu_c[:, ::-1]
        h_c = s5_scan(lam_bar, bu_c)
        h_l = s5_scan(lam_bar, bu_l.at[:, 0].add(lam_bar * h_c[:, -1]))
        if dirn == 1:
            h_l, h_c = h_l[:, ::-1], h_c[:, ::-1]
        y_lat = y_lat + jnp.einsum('blgp,ghp->blgh', h_l, c_mat).real
        if ctx_out:
            y_ctx = y_ctx + jnp.einsum('blgp,ghp->blgh', h_c, c_mat).real
    y_lat = y_lat.reshape(bsz, n, S5_WIDTH).astype(u_lat.dtype)
    if not ctx_out:
        return y_lat, None
    return y_lat, y_ctx.reshape(bsz, n_c, S5_WIDTH).astype(u_ctx.dtype)


def s5_glu(y, glu_w, glu_b):
    y = jax.nn.gelu(y)
    return y * jax.nn.sigmoid(y @ glu_w + glu_b)


def sgu(u, v, ln_g, ln_b, w_s, b_s):
    bsz, n, _ = u.shape
    u = jax.nn.gelu(u)
    v = layer_norm(jax.nn.gelu(v), ln_g, ln_b)
    vc = v.reshape(bsz, n // CHUNK, CHUNK, SGU_HEADS, SGU_HEAD_DIM)
    s = jnp.einsum('hqk,bckhd->bcqhd', w_s, vc) + b_s.T[:, :, None]
    return u * s.reshape(bsz, n, SGU_WIDTH)


def depthwise2d(x, k):
    return lax.conv_general_dilated(x, k[:, :, None, :], (1, 1), 'SAME',
                                    dimension_numbers=('NHWC', 'HWIO', 'NHWC'),
                                    feature_group_count=x.shape[-1])


def conformer_conv(a, g, dw_w, dw_b, ln_g, ln_b, on_grid):
    h = a * jax.nn.sigmoid(g)
    bsz, n, cw = h.shape
    k = dw_w.astype(h.dtype)
    if on_grid:
        rows = n // GRID_W
        h4 = h.reshape(bsz, rows, GRID_W, cw)
        half = cw // 2
        along_row = depthwise2d(h4[..., :half], k[None, :, :half])
        along_col = depthwise2d(h4[..., half:], k[:, None, half:])
        h = jnp.concatenate([along_row, along_col], axis=-1).reshape(bsz, n, cw)
    else:
        h = depthwise2d(h[:, None], k[None])[:, 0]
    h = layer_norm(h + dw_b, ln_g, ln_b)
    return jax.nn.silu(h)


def even_layer(x, xc, c, c_ctx, need_ctx, mod_w, mod_b, norm_g, norm_b, w_in, w_out,
               lam_re, lam_im, log_dt, b_re, b_im, c_re, c_im, d_skip, glu_w, glu_b,
               sgu_ln_g, sgu_ln_b, sgu_w, sgu_b):
    shift, scale, gate = adaln(c, mod_w, mod_b)
    h = modulate(x, shift[:, None], scale[:, None])
    ua, za, ub, vb, zb = jnp.split(h @ w_in, EVEN_SPLITS, axis=-1)
    shift_c, scale_c, gate_c = adaln(c_ctx, mod_w, mod_b)
    hc = modulate(xc, shift_c, scale_c)
    if need_ctx:
        ua_c, za_c, ub_c, vb_c, zb_c = jnp.split(hc @ w_in, EVEN_SPLITS, axis=-1)
    else:
        ua_c = hc @ w_in[:, :S5_WIDTH]
    s_lat, s_ctx = s5_mixer(ua, ua_c, need_ctx, lam_re, lam_im, log_dt, b_re, b_im, c_re, c_im, d_skip)
    y = jnp.concatenate([s5_glu(s_lat, glu_w, glu_b) * jax.nn.silu(za),
                         sgu(ub, vb, sgu_ln_g, sgu_ln_b, sgu_w, sgu_b) * jax.nn.silu(zb)], axis=-1) @ w_out
    x_new = layer_norm(DN_ALPHA * x + gate[:, None] * y, norm_g, norm_b)
    if not need_ctx:
        return x_new, None
    yc = jnp.concatenate([s5_glu(s_ctx, glu_w, glu_b) * jax.nn.silu(za_c),
                          sgu(ub_c, vb_c, sgu_ln_g, sgu_ln_b, sgu_w, sgu_b) * jax.nn.silu(zb_c)], axis=-1) @ w_out
    xc_new = layer_norm(DN_ALPHA * xc + gate_c * yc, norm_g, norm_b)
    return x_new, xc_new


def odd_layer(x, xc, c, c_ctx, need_ctx, mod_w, mod_b, norm_g, norm_b, w_in, w_out,
              dw_w, dw_b, ln_g, ln_b):
    def branch(h, on_grid):
        a, g, z = jnp.split(h @ w_in, 3, axis=-1)
        return (conformer_conv(a, g, dw_w, dw_b, ln_g, ln_b, on_grid) * jax.nn.silu(z)) @ w_out

    shift, scale, gate = adaln(c, mod_w, mod_b)
    y = branch(modulate(x, shift[:, None], scale[:, None]), True)
    x_new = layer_norm(DN_ALPHA * x + gate[:, None] * y, norm_g, norm_b)
    if not need_ctx:
        return x_new, None
    shift_c, scale_c, gate_c = adaln(c_ctx, mod_w, mod_b)
    yc = branch(modulate(xc, shift_c, scale_c), False)
    xc_new = layer_norm(DN_ALPHA * xc + gate_c * yc, norm_g, norm_b)
    return x_new, xc_new


def setup_inputs(seed: int = 0) -> dict:
    key = jax.random.key(seed)
    ks = iter(jax.random.split(key, 40))
    nrm = lambda shape: jax.random.normal(next(ks), shape, jnp.float32)
    D = D_MODEL
    G, P, H = S5_GROUPS, S5_STATE, S5_GROUP
    n_idx = jnp.arange(P, dtype=jnp.float32)
    return {
        "x": nrm((BATCH, SEQ, D)),
        "c": nrm((BATCH, D)),
        "ctx": nrm((BATCH, CTX_LEN, D)),
        "c_ctx": nrm((D,)),
        "mod_w": nrm((DEPTH, D, 3 * D)) * (0.5 * D ** -0.5),
        "mod_b": nrm((DEPTH, 3 * D)) * 0.01,
        "norm_g": 1.0 + 0.01 * nrm((DEPTH, D)),
        "norm_b": 0.01 * nrm((DEPTH, D)),
        "ev_w_in": nrm((N_EVEN, D, EVEN_IN)) * D ** -0.5,
        "ev_w_out": nrm((N_EVEN, EVEN_OUT, D)) * (EVEN_OUT ** -0.5 * DN_BETA),
        "s5_lam_re": -0.5 + 0.01 * nrm((N_EVEN, 2, G, P)),
        "s5_lam_im": math.pi * n_idx + 0.01 * nrm((N_EVEN, 2, G, P)),
        "s5_log_dt": jax.random.uniform(next(ks), (N_EVEN, 2, G), jnp.float32, math.log(1e-3), math.log(1e-1)),
        "s5_b_re": nrm((N_EVEN, 2, G, P, H)) * (2 * H) ** -0.5,
        "s5_b_im": nrm((N_EVEN, 2, G, P, H)) * (2 * H) ** -0.5,
        "s5_c_re": nrm((N_EVEN, 2, G, H, P)) * (2 * P) ** -0.5,
        "s5_c_im": nrm((N_EVEN, 2, G, H, P)) * (2 * P) ** -0.5,
        "s5_d": nrm((N_EVEN, S5_WIDTH)),
        "glu_w": nrm((N_EVEN, S5_WIDTH, S5_WIDTH)) * S5_WIDTH ** -0.5,
        "glu_b": 0.01 * nrm((N_EVEN, S5_WIDTH)),
        "sgu_ln_g": 1.0 + 0.01 * nrm((N_EVEN, SGU_WIDTH)),
        "sgu_ln_b": 0.01 * nrm((N_EVEN, SGU_WIDTH)),
        "sgu_w": nrm((N_EVEN, SGU_HEADS, CHUNK, CHUNK)) * CHUNK ** -0.5,
        "sgu_b": 1.0 + 0.01 * nrm((N_EVEN, SGU_HEADS, CHUNK)),
        "od_w_in": nrm((N_ODD, D, ODD_IN)) * D ** -0.5,
        "od_w_out": nrm((N_ODD, CONV_WIDTH, D)) * (CONV_WIDTH ** -0.5 * DN_BETA),
        "dw_w": nrm((N_ODD, CONV_K, CONV_WIDTH)) * CONV_K ** -0.5,
        "dw_b": 0.01 * nrm((N_ODD, CONV_WIDTH)),
        "conv_ln_g": 1.0 + 0.01 * nrm((N_ODD, CONV_WIDTH)),
        "conv_ln_b": 0.01 * nrm((N_ODD, CONV_WIDTH)),
    }


def reference(x, c, ctx, c_ctx, mod_w, mod_b, norm_g, norm_b, ev_w_in, ev_w_out,
              s5_lam_re, s5_lam_im, s5_log_dt, s5_b_re, s5_b_im, s5_c_re, s5_c_im, s5_d,
              glu_w, glu_b, sgu_ln_g, sgu_ln_b, sgu_w, sgu_b,
              od_w_in, od_w_out, dw_w, dw_b, conv_ln_g, conv_ln_b):
    xc = ctx
    for l in range(DEPTH):
        need_ctx = any(j % 2 == 0 for j in range(l + 1, DEPTH))
        i = l // 2
        if l % 2 == 0:
            x, xc = even_layer(x, xc, c, c_ctx, need_ctx, mod_w[l], mod_b[l], norm_g[l], norm_b[l],
                               ev_w_in[i], ev_w_out[i], s5_lam_re[i], s5_lam_im[i], s5_log_dt[i],
                               s5_b_re[i], s5_b_im[i], s5_c_re[i], s5_c_im[i], s5_d[i],
                               glu_w[i], glu_b[i], sgu_ln_g[i], sgu_ln_b[i], sgu_w[i], sgu_b[i])
        else:
            x, xc = odd_layer(x, xc, c, c_ctx, need_ctx, mod_w[l], mod_b[l], norm_g[l], norm_b[l],
                              od_w_in[i], od_w_out[i], dw_w[i], dw_b[i], conv_ln_g[i], conv_ln_b[i])
    return x
```

```python
import functools
import math

import jax
import jax.numpy as jnp
from jax import lax
from jax.experimental import pallas as pl
from jax.experimental.pallas import tpu as pltpu

D = 1024
B = 4
L = 4096
CTX = 256
GRID_W = 64
S5_W = 512
S5_G = 32
S5_H = 16
H_SHIFT = 4
S5_P = 64
S5_T = 16
SGU_W = 512
SGU_HEADS = 8
SGU_HD = 64
SGU_CHUNK = 128
CONV_K = 31
CONV_HALF = CONV_K // 2
EVEN_IN = 2560
ODD_IN = 3072
DEPTH = 2
DN_ALPHA = (2 * DEPTH) ** 0.25
LN_EPS = 1e-5
N_CHUNK = L // S5_T
N_CCHUNK = CTX // S5_T
VMEM_LIMIT_V7X = 56 * 1024 * 1024

F32 = jnp.float32
BF16 = jnp.bfloat16


def _gelu(x):
    return 0.5 * x * (1.0 + jnp.tanh(math.sqrt(2.0 / math.pi) * (x + 0.044715 * (x * x * x))))


def _sigmoid(x):
    return 0.5 * (1.0 + jnp.tanh(0.5 * x))


def _silu(x):
    return x * _sigmoid(x)


def _layer_norm(x, g, b):
    mu = jnp.mean(x, axis=-1, keepdims=True)
    xc = x - mu
    var = jnp.mean(xc * xc, axis=-1, keepdims=True)
    return xc * lax.rsqrt(var + LN_EPS) * g + b


def _params(*sem):
    return pltpu.CompilerParams(dimension_semantics=sem, vmem_limit_bytes=VMEM_LIMIT_V7X)


def _adaln_kernel(c_ref, w_ref, b_ref, o_ref):
    a = _silu(c_ref[...])
    o_ref[...] = jnp.dot(a, w_ref[...], preferred_element_type=F32,
                         precision=lax.Precision.HIGHEST) + b_ref[...]


def _adaln(cond8, mod_w, mod_b):
    tn = 512
    return pl.pallas_call(
        _adaln_kernel,
        out_shape=jax.ShapeDtypeStruct((DEPTH, 8, 3 * D), F32),
        grid=(DEPTH, 3 * D // tn),
        in_specs=[pl.BlockSpec((8, D), lambda l, j: (0, 0)),
                  pl.BlockSpec((None, D, tn), lambda l, j: (l, 0, j)),
                  pl.BlockSpec((None, 1, tn), lambda l, j: (l, 0, j))],
        out_specs=pl.BlockSpec((None, 8, tn), lambda l, j: (l, 0, j)),
        compiler_params=_params("arbitrary", "arbitrary"),
        name="adaln",
    )(cond8, mod_w, mod_b.reshape(DEPTH, 1, 3 * D))


def _disc_kernel(lr_ref, li_ref, ldt_ref, obr_ref, obi_ref, ocr_ref, oci_ref):
    lr = lr_ref[...]
    li = li_ref[...]
    dt = jnp.exp(ldt_ref[...])
    mag = jnp.exp(lr * dt)
    br = mag * jnp.cos(li * dt)
    bi = mag * jnp.sin(li * dt)
    inv = 1.0 / (lr * lr + li * li)
    nr = br - 1.0
    obr_ref[...] = br
    obi_ref[...] = bi
    ocr_ref[...] = (nr * lr + bi * li) * inv
    oci_ref[...] = (bi * lr - nr * li) * inv


def _discretise(lam_re, lam_im, log_dt):
    shp = jax.ShapeDtypeStruct((2 * S5_G, S5_P), F32)
    ldt = jnp.broadcast_to(log_dt.reshape(2 * S5_G, 1), (2 * S5_G, S5_P))
    outs = pl.pallas_call(
        _disc_kernel, out_shape=(shp, shp, shp, shp), name="s5_discretise",
    )(lam_re.reshape(2 * S5_G, S5_P), lam_im.reshape(2 * S5_G, S5_P), ldt)
    return [o.reshape(2, S5_G, S5_P) for o in outs]


def _cpow(base_pows, j):
    re = None
    im = None
    for k, (pr, pi) in enumerate(base_pows):
        bit = ((j >> k) & 1) == 1
        mr = jnp.where(bit, pr, 1.0)
        mi = jnp.where(bit, pi, 0.0)
        if re is None:
            re, im = mr, mi
        else:
            re, im = re * mr - im * mi, re * mi + im * mr
    return re, im


def _squarings(pr, pi, n):
    out = [(pr, pi)]
    for _ in range(n - 1):
        pr, pi = pr * pr - pi * pi, 2.0 * pr * pi
        out.append((pr, pi))
    return out


def _shift_lanes(x, n):
    lane = lax.broadcasted_iota(jnp.int32, (S5_H, 128), 1)
    lo, hi = x[:, :128], x[:, 128:]
    if n == 0:
        return x
    if n < 128:
        rlo = pltpu.roll(lo, n, axis=1)
        rhi = pltpu.roll(hi, n, axis=1)
        return jnp.concatenate([jnp.where(lane >= n, rlo, 0.0), jnp.where(lane >= n, rhi, rlo)], axis=1)
    m = n - 128
    rlo = lo if m == 0 else pltpu.roll(lo, m, axis=1)
    return jnp.concatenate([jnp.zeros_like(lo), jnp.where(lane >= m, rlo, 0.0)], axis=1)


def _unshift_lanes(x, n):
    lane = lax.broadcasted_iota(jnp.int32, (S5_H, 128), 1)
    lo, hi = x[:, :128], x[:, 128:]
    if n == 0:
        return x
    if n < 128:
        rlo = pltpu.roll(lo, 128 - n, axis=1)
        rhi = pltpu.roll(hi, 128 - n, axis=1)
        keep = lane < 128 - n
        return jnp.concatenate([jnp.where(keep, rlo, rhi), jnp.where(keep, rhi, 0.0)], axis=1)
    m = n - 128
    rhi = hi if m == 0 else pltpu.roll(hi, 128 - m, axis=1)
    return jnp.concatenate([jnp.where(lane < 128 - m, rhi, 0.0), jnp.zeros_like(lo)], axis=1)


def _s5w_kernel(lrow_re, lrow_im, crow_re, crow_im, bt_re, bt_im,
                lcol_re, lcol_im, ct_re, ct_im, d_ref,
                win_ref, wout_ref, mix_ref, l16_ref):
    TH = S5_T * S5_H
    lr = lrow_re[...]
    li = lrow_im[...]
    pows_row = _squarings(lr, li, 5)
    l16_ref[0:1, :] = pows_row[4][0]
    l16_ref[1:2, :] = pows_row[4][1]
    l16_ref[2:8, :] = jnp.zeros((6, 128), F32)
    cr = crow_re[...]
    ci = crow_im[...]
    btr = bt_re[...]
    bti = bt_im[...]
    bbr = cr * btr - ci * bti
    bbi = cr * bti + ci * btr
    s_idx = lax.broadcasted_iota(jnp.int32, (TH, 128), 0) >> H_SHIFT
    lane = lax.broadcasted_iota(jnp.int32, (TH, 128), 1)
    jw = jnp.where(lane < S5_P, S5_T - 1 - s_idx, s_idx)
    pr, pi = _cpow(pows_row[:4], jw)
    tbr = jnp.broadcast_to(bbr[None], (S5_T, S5_H, 128)).reshape(TH, 128)
    tbi = jnp.broadcast_to(bbi[None], (S5_T, S5_H, 128)).reshape(TH, 128)
    win_ref[:, 0:128] = (pr * tbr - pi * tbi).astype(BF16)
    win_ref[:, 128:256] = (pr * tbi + pi * tbr).astype(BF16)

    cpows = _squarings(lcol_re[...], lcol_im[...], 5)
    row = lax.broadcasted_iota(jnp.int32, (2 * S5_P, TH), 0)
    t_idx = lax.broadcasted_iota(jnp.int32, (2 * S5_P, TH), 1) >> H_SHIFT
    is_f = row < S5_P
    ctr = ct_re[...]
    cti = ct_im[...]
    er, ei = _cpow(cpows, jnp.where(is_f, t_idx + 1, S5_T - t_idx))
    wr = ctr * er - cti * ei
    wi = ctr * ei + cti * er
    wout_ref[0:128, :] = wr.astype(BF16)
    wout_ref[128:256, :] = (-wi).astype(BF16)
    kr, ki = _cpow(cpows[:4], jnp.where(is_f, t_idx, S5_T - 1 - t_idx))
    ekr = ctr * kr - cti * ki
    eki = ctr * ki + cti * kr
    lane16 = lax.broadcasted_iota(jnp.int32, (S5_H, 128), 1)
    mf = lane16 < S5_P
    hp = lax.Precision.HIGHEST
    dot = functools.partial(jnp.dot, preferred_element_type=F32, precision=hp)
    kkf = dot(jnp.where(mf, bbr, 0.0), ekr) - dot(jnp.where(mf, bbi, 0.0), eki)
    kkb = dot(jnp.where(mf, 0.0, bbr), ekr) - dot(jnp.where(mf, 0.0, bbi), eki)
    dl = d_ref[...]
    r16 = lax.broadcasted_iota(jnp.int32, (S5_H, TH), 0)
    l256 = lax.broadcasted_iota(jnp.int32, (S5_H, TH), 1)
    for s in range(S5_T):
        blk = _shift_lanes(kkf, S5_H * s) + _unshift_lanes(kkb, S5_H * (S5_T - 1 - s))
        blk = blk + jnp.where(l256 == r16 + S5_H * s, dl, 0.0)
        mix_ref[S5_H * s:S5_H * (s + 1), :] = blk.astype(BF16)


def _s5_weights(lrow_re, lrow_im, crow_re, crow_im, bt_re, bt_im, lcol_re, lcol_im, ct_re, ct_im, d_row):
    TH = S5_T * S5_H
    g3 = lambda r, c: pl.BlockSpec((None, r, c), lambda g: (g, 0, 0))
    wshape = jax.ShapeDtypeStruct((S5_G, TH, TH), BF16)
    return pl.pallas_call(
        _s5w_kernel,
        out_shape=(wshape, wshape, wshape, jax.ShapeDtypeStruct((S5_G, 8, 128), F32)),
        grid=(S5_G,),
        in_specs=[g3(1, 128)] * 4 + [g3(S5_H, 128)] * 2 + [g3(128, TH)] * 4 + [g3(1, TH)],
        out_specs=(g3(TH, TH), g3(TH, TH), g3(TH, TH), g3(8, 128)),
        compiler_params=_params("arbitrary"),
        name="s5_weights",
    )(lrow_re, lrow_im, crow_re, crow_im, bt_re, bt_im, lcol_re, lcol_im, ct_re, ct_im, d_row)


def _inproj0_kernel(x_ref, mod_ref, w_ref, lng_ref, lnb_ref, ua_ref, sza_ref, guz_ref, vln_ref):
    shift = mod_ref[:, 0:D]
    scale = mod_ref[:, D:2 * D]
    h = (x_ref[...] * (1.0 + scale) + shift).astype(BF16)
    dot = lambda lo: jnp.dot(h, w_ref[:, lo:lo + 512], preferred_element_type=F32)
    ua_ref[...] = dot(0).astype(BF16)
    sza_ref[...] = _silu(dot(512)).astype(BF16)
    guz_ref[...] = (_gelu(dot(1024)) * _silu(dot(2048))).astype(BF16)
    vln_ref[...] = _layer_norm(_gelu(dot(1536)), lng_ref[...], lnb_ref[...]).astype(BF16)


def _inproj0(x, mod, w_in, ln_g, ln_b, tm=512):
    o = jax.ShapeDtypeStruct((B, L, 512), BF16)
    ospec = pl.BlockSpec((None, tm, 512), lambda b, i: (b, i, 0))
    return pl.pallas_call(
        _inproj0_kernel,
        out_shape=(o, o, o, o),
        grid=(B, L // tm),
        in_specs=[pl.BlockSpec((None, tm, D), lambda b, i: (b, i, 0)),
                  pl.BlockSpec((None, 1, 3 * D), lambda b, i: (b, 0, 0)),
                  pl.BlockSpec((D, EVEN_IN), lambda b, i: (0, 0)),
                  pl.BlockSpec((1, 512), lambda b, i: (0, 0)),
                  pl.BlockSpec((1, 512), lambda b, i: (0, 0))],
        out_specs=(ospec, ospec, ospec, ospec),
        compiler_params=_params("arbitrary", "arbitrary"),
        name="inproj0",
    )(x, mod, w_in, ln_g, ln_b)


def _inproj0_ctx_kernel(x_ref, mod_ref, w_ref, ua_ref):
    shift = mod_ref[:, 0:D]
    scale = mod_ref[:, D:2 * D]
    h = (x_ref[...] * (1.0 + scale) + shift).astype(BF16)
    ua_ref[...] = jnp.dot(h, w_ref[...], preferred_element_type=F32).astype(BF16)


def _inproj0_ctx(ctx, mod_c, w_ua):
    return pl.pallas_call(
        _inproj0_ctx_kernel,
        out_shape=jax.ShapeDtypeStruct((B, CTX, 512), BF16),
        grid=(B,),
        in_specs=[pl.BlockSpec((None, CTX, D), lambda b: (b, 0, 0)),
                  pl.BlockSpec((1, 3 * D), lambda b: (0, 0)),
                  pl.BlockSpec((D, 512), lambda b: (0, 0))],
        out_specs=pl.BlockSpec((None, CTX, 512), lambda b: (b, 0, 0)),
        compiler_params=_params("arbitrary"),
        name="inproj0_ctx",
    )(ctx, mod_c, w_ua)


def _scan_tiles(s_ref, hf_ref, hb_ref, n_tiles, carry, lam, store):
    lre, lim = lam
    row = lax.broadcasted_iota(jnp.int32, (8, 128), 0)
    first = row < B

    def cmul_add(hr, hi, sr, si):
        return lre * hr - lim * hi + sr, lre * hi + lim * hr + si

    def body(k, c):
        fr, fi, br, bi = c
        kb = n_tiles - 1 - k
        of = pl.multiple_of(k * 8, 8)
        ob = pl.multiple_of(kb * 8, 8)
        sr = s_ref[pl.ds(of, 8), 0:128]
        si = s_ref[pl.ds(of, 8), 128:256]
        h1r, h1i = cmul_add(fr, fi, sr, si)
        r1r = pltpu.roll(h1r, B, axis=0)
        r1i = pltpu.roll(h1i, B, axis=0)
        if store:
            hf_ref[pl.ds(of, 8), 0:128] = jnp.where(first, fr, r1r)
            hf_ref[pl.ds(of, 8), 128:256] = jnp.where(first, fi, r1i)
        h2r, h2i = cmul_add(r1r, r1i, sr, si)
        nfr = jnp.where(first, pltpu.roll(h2r, B, axis=0), h2r)
        nfi = jnp.where(first, pltpu.roll(h2i, B, axis=0), h2i)
        tr = s_ref[pl.ds(ob, 8), 0:128]
        ti = s_ref[pl.ds(ob, 8), 128:256]
        g1r, g1i = cmul_add(br, bi, tr, ti)
        q1r = pltpu.roll(g1r, B, axis=0)
        q1i = pltpu.roll(g1i, B, axis=0)
        if store:
            hb_ref[pl.ds(ob, 8), 0:128] = jnp.where(first, q1r, br)
            hb_ref[pl.ds(ob, 8), 128:256] = jnp.where(first, q1i, bi)
        g2r, g2i = cmul_add(q1r, q1i, tr, ti)
        nbr = jnp.where(first, g2r, pltpu.roll(g2r, B, axis=0))
        nbi = jnp.where(first, g2i, pltpu.roll(g2i, B, axis=0))
        return nfr, nfi, nbr, nbi

    return lax.fori_loop(0, n_tiles, body, carry)


def _s5core_kernel(ul_ref, uc_ref, win_ref, wout_ref, mix_ref, l16_ref, y_ref, sl_ref, sc_ref, hf_ref, hb_ref):
    win = win_ref[...]
    sc_ref[...] = jnp.dot(uc_ref[...], win, preferred_element_type=F32)
    sl_ref[...] = jnp.dot(ul_ref[...], win, preferred_element_type=F32)
    lam = (jnp.broadcast_to(l16_ref[0:1, :], (8, 128)), jnp.broadcast_to(l16_ref[1:2, :], (8, 128)))
    z = jnp.zeros((8, 128), F32)
    carry = _scan_tiles(sc_ref, None, None, N_CCHUNK * B // 8, (z, z, z, z), lam, store=False)
    _scan_tiles(sl_ref, hf_ref, hb_ref, N_CHUNK * B // 8, carry, lam, store=True)
    lane = lax.broadcasted_iota(jnp.int32, (N_CHUNK * B, 2 * 128), 1)
    hcat = jnp.where((lane & 127) < S5_P, hf_ref[...], hb_ref[...]).astype(BF16)
    y = jnp.dot(ul_ref[...], mix_ref[...], preferred_element_type=F32)
    y = y + jnp.dot(hcat, wout_ref[...], preferred_element_type=F32)
    y_ref[...] = y.astype(BF16)


def _s5core(ul, uc, win, wout, mix, l16):
    TH = S5_T * S5_H
    nl = N_CHUNK * B
    ncx = N_CCHUNK * B
    g3 = lambda r, c: pl.BlockSpec((None, r, c), lambda g: (g, 0, 0))
    return pl.pallas_call(
        _s5core_kernel,
        out_shape=jax.ShapeDtypeStruct((S5_G, nl, TH), BF16),
        grid=(S5_G,),
        in_specs=[g3(nl, TH), g3(ncx, TH), g3(TH, TH), g3(TH, TH), g3(TH, TH), g3(8, 128)],
        out_specs=g3(nl, TH),
        scratch_shapes=[pltpu.VMEM((nl, TH), F32), pltpu.VMEM((ncx, TH), F32),
                        pltpu.VMEM((nl, TH), F32), pltpu.VMEM((nl, TH), F32)],
        compiler_params=_params("arbitrary"),
        name="s5core",
    )(ul, uc, win, wout, mix, l16)


def _tail0_kernel(x_ref, slat_ref, sza_ref, guz_ref, vln_ref, mod_ref, gluw_ref, glub_ref,
                  sguw_ref, sgub_ref, wout_ref, ng_ref, nb_ref, o_ref):
    tm = x_ref.shape[0]
    g = _gelu(slat_ref[...].astype(F32))
    gate = _sigmoid(jnp.dot(g.astype(BF16), gluw_ref[...], preferred_element_type=F32) + glub_ref[...])
    a = (g * gate * sza_ref[...].astype(F32)).astype(BF16)
    lane = lax.broadcasted_iota(jnp.int32, (SGU_CHUNK, 128), 1)
    lo = lane < SGU_HD
    zero = jnp.zeros((SGU_CHUNK, 128), BF16)
    chunks = []
    for ci in range(tm // SGU_CHUNK):
        v = vln_ref[ci * SGU_CHUNK:(ci + 1) * SGU_CHUNK, :]
        cols = []
        for pi in range(SGU_HEADS // 2):
            vp = v[:, 128 * pi:128 * (pi + 1)]
            bm = jnp.concatenate([jnp.where(lo, vp, zero), jnp.where(lo, zero, vp)], axis=0)
            cols.append(jnp.dot(sguw_ref[pi], bm, preferred_element_type=F32))
        chunks.append(jnp.concatenate(cols, axis=1) + sgub_ref[...])
    s = jnp.concatenate(chunks, axis=0)
    bsg = (guz_ref[...].astype(F32) * s).astype(BF16)
    y = jnp.dot(a, wout_ref[0:512, :], preferred_element_type=F32)
    y = y + jnp.dot(bsg, wout_ref[512:1024, :], preferred_element_type=F32)
    gmod = mod_ref[:, 2 * D:3 * D]
    o_ref[...] = _layer_norm(DN_ALPHA * x_ref[...] + gmod * y, ng_ref[...], nb_ref[...])


def _tail0(x, slat, sza, guz, vln, mod, glu_w, glu_b, sguw, sgub, w_out, ng, nb, tm=256):
    t512 = pl.BlockSpec((None, tm, 512), lambda b, i: (b, i, 0))
    full = lambda *s: pl.BlockSpec(s, lambda b, i: (0,) * len(s))
    return pl.pallas_call(
        _tail0_kernel,
        out_shape=jax.ShapeDtypeStruct((B, L, D), F32),
        grid=(B, L // tm),
        in_specs=[pl.BlockSpec((None, tm, D), lambda b, i: (b, i, 0)), t512, t512, t512, t512,
                  pl.BlockSpec((None, 1, 3 * D), lambda b, i: (b, 0, 0)),
                  full(512, 512), full(1, 512), full(SGU_HEADS // 2, SGU_CHUNK, 256), full(SGU_CHUNK, 512),
                  full(D, D), full(1, D), full(1, D)],
        out_specs=pl.BlockSpec((None, tm, D), lambda b, i: (b, i, 0)),
        compiler_params=_params("arbitrary", "arbitrary"),
        name="tail0",
    )(x, slat, sza, guz, vln, mod, glu_w, glu_b, sguw, sgub, w_out, ng, nb)


def _inproj1_kernel(x_ref, mod_ref, w_ref, hg_ref, sz_ref):
    shift = mod_ref[:, 0:D]
    scale = mod_ref[:, D:2 * D]
    h = (x_ref[...] * (1.0 + scale) + shift).astype(BF16)
    dot = lambda lo: jnp.dot(h, w_ref[:, lo:lo + D], preferred_element_type=F32)
    hg_ref[...] = (dot(0) * _sigmoid(dot(D))).astype(BF16)
    sz_ref[...] = _silu(dot(2 * D)).astype(BF16)


def _inproj1(x, mod, w_in, tm=512):
    o = jax.ShapeDtypeStruct((B, L, D), BF16)
    ospec = pl.BlockSpec((None, tm, D), lambda b, i: (b, i, 0))
    return pl.pallas_call(
        _inproj1_kernel,
        out_shape=(o, o),
        grid=(B, L // tm),
        in_specs=[pl.BlockSpec((None, tm, D), lambda b, i: (b, i, 0)),
                  pl.BlockSpec((None, 1, 3 * D), lambda b, i: (b, 0, 0)),
                  pl.BlockSpec((D, ODD_IN), lambda b, i: (0, 0))],
        out_specs=(ospec, ospec),
        compiler_params=_params("arbitrary", "arbitrary"),
        name="inproj1",
    )(x, mod, w_in)


ROW_PAD = 16
COL_PAD = CONV_HALF * GRID_W
CONV_ROWS = 128


def _conv_kernel(h_ref, w_ref, b_ref, o_ref, prow_ref, pcol_ref):
    j = pl.program_id(1)
    bias = b_ref[...]

    @pl.when(j < (D // 2) // 128)
    def _():
        zpad = jnp.zeros((GRID_W, ROW_PAD, 128), F32)
        prow_ref[:, 0:ROW_PAD, :] = zpad
        prow_ref[:, ROW_PAD + GRID_W:, :] = zpad
        prow_ref[:, ROW_PAD:ROW_PAD + GRID_W, :] = h_ref[...].astype(F32).reshape(GRID_W, GRID_W, 128)

        def body(r, carry):
            acc = jnp.zeros((GRID_W, 128), F32) + bias
            for k in range(CONV_K):
                off = ROW_PAD - CONV_HALF + k
                acc = acc + w_ref[k:k + 1, :] * prow_ref[r, off:off + GRID_W, :]
            o_ref[pl.ds(pl.multiple_of(r * GRID_W, GRID_W), GRID_W), :] = acc.astype(BF16)
            return carry

        lax.fori_loop(0, GRID_W, body, 0)

    @pl.when(j >= (D // 2) // 128)
    def _():
        zpad = jnp.zeros((COL_PAD, 128), F32)
        pcol_ref[0:COL_PAD, :] = zpad
        pcol_ref[COL_PAD + L:, :] = zpad
        pcol_ref[COL_PAD:COL_PAD + L, :] = h_ref[...].astype(F32)

        def body(i, carry):
            base = pl.multiple_of(i * CONV_ROWS, CONV_ROWS)
            acc = jnp.zeros((CONV_ROWS, 128), F32) + bias
            for k in range(CONV_K):
                acc = acc + w_ref[k:k + 1, :] * pcol_ref[pl.ds(base + k * GRID_W, CONV_ROWS), :]
            o_ref[pl.ds(base, CONV_ROWS), :] = acc.astype(BF16)
            return carry

        lax.fori_loop(0, L // CONV_ROWS, body, 0)


def _conv(hg, dw_w, dw_b):
    return pl.pallas_call(
        _conv_kernel,
        out_shape=jax.ShapeDtypeStruct((B, L, D), BF16),
        grid=(B, D // 128),
        in_specs=[pl.BlockSpec((None, L, 128), lambda b, j: (b, 0, j)),
                  pl.BlockSpec((CONV_K, 128), lambda b, j: (0, j)),
                  pl.BlockSpec((1, 128), lambda b, j: (0, j))],
        out_specs=pl.BlockSpec((None, L, 128), lambda b, j: (b, 0, j)),
        scratch_shapes=[pltpu.VMEM((GRID_W, GRID_W + 2 * ROW_PAD, 128), F32),
                        pltpu.VMEM((L + 2 * COL_PAD, 128), F32)],
        compiler_params=_params("arbitrary", "arbitrary"),
        name="dwconv",
    )(hg, dw_w, dw_b)


def _tail1_kernel(x_ref, hc_ref, sz_ref, mod_ref, lng_ref, lnb_ref, wout_ref, ng_ref, nb_ref, o_ref):
    m = _silu(_layer_norm(hc_ref[...].astype(F32), lng_ref[...], lnb_ref[...])) * sz_ref[...].astype(F32)
    y = jnp.dot(m.astype(BF16), wout_ref[...], preferred_element_type=F32)
    gmod = mod_ref[:, 2 * D:3 * D]
    o_ref[...] = _layer_norm(DN_ALPHA * x_ref[...] + gmod * y, ng_ref[...], nb_ref[...])


def _tail1(x, hc, sz, mod, ln_g, ln_b, w_out, ng, nb, tm=512):
    tD = pl.BlockSpec((None, tm, D), lambda b, i: (b, i, 0))
    full = lambda *s: pl.BlockSpec(s, lambda b, i: (0,) * len(s))
    return pl.pallas_call(
        _tail1_kernel,
        out_shape=jax.ShapeDtypeStruct((B, L, D), F32),
        grid=(B, L // tm),
        in_specs=[tD, tD, tD, pl.BlockSpec((None, 1, 3 * D), lambda b, i: (b, 0, 0)),
                  full(1, D), full(1, D), full(D, D), full(1, D), full(1, D)],
        out_specs=tD,
        compiler_params=_params("arbitrary", "arbitrary"),
        name="tail1",
    )(x, hc, sz, mod, ln_g, ln_b, w_out, ng, nb)


def _to_group_chunks(u, n_chunks):
    u = u.reshape(B, n_chunks, S5_T, S5_G, S5_H)
    return jnp.transpose(u, (3, 1, 0, 2, 4)).reshape(S5_G, n_chunks * B, S5_T * S5_H)


def _from_group_chunks(y):
    y = y.reshape(S5_G, N_CHUNK, B, S5_T, S5_H)
    return jnp.transpose(y, (2, 1, 3, 0, 4)).reshape(B, L, S5_W)


def kernel(x, c, ctx, c_ctx, mod_w, mod_b, norm_g, norm_b, ev_w_in, ev_w_out, s5_lam_re, s5_lam_im, s5_log_dt, s5_b_re, s5_b_im, s5_c_re, s5_c_im, s5_d, glu_w, glu_b, sgu_ln_g, sgu_ln_b, sgu_w, sgu_b, od_w_in, od_w_out, dw_w, dw_b, conv_ln_g, conv_ln_b):
    TH = S5_T * S5_H
    row = lambda v: v.reshape(1, -1)

    cond8 = jnp.concatenate([c, c_ctx[None], jnp.zeros((3, D), F32)], axis=0)
    mods = _adaln(cond8, mod_w, mod_b)
    mod0 = mods[0, :B].reshape(B, 1, 3 * D)
    mod0c = mods[0, B:B + 1]
    mod1 = mods[1, :B].reshape(B, 1, 3 * D)

    lbr, lbi, cfr, cfi = _discretise(s5_lam_re[0], s5_lam_im[0], s5_log_dt[0])
    rowcat = lambda a: jnp.concatenate([a[0], a[1]], axis=-1).reshape(S5_G, 1, 2 * S5_P)
    colcat = lambda a: jnp.broadcast_to(jnp.concatenate([a[0], a[1]], axis=-1)[:, :, None], (S5_G, 2 * S5_P, TH))
    bt = lambda a: jnp.concatenate([jnp.swapaxes(a[0], 1, 2), jnp.swapaxes(a[1], 1, 2)], axis=-1)
    ct = lambda a: jnp.tile(jnp.concatenate([jnp.swapaxes(a[0], 1, 2), jnp.swapaxes(a[1], 1, 2)], axis=1), (1, 1, S5_T))
    d_row = jnp.tile(s5_d[0].reshape(S5_G, 1, S5_H), (1, 1, S5_T))
    win, wout, mix, l16 = _s5_weights(rowcat(lbr), rowcat(lbi), rowcat(cfr), rowcat(cfi),
                                      bt(s5_b_re[0]), bt(s5_b_im[0]), colcat(lbr), colcat(lbi),
                                      ct(s5_c_re[0]), ct(s5_c_im[0]), d_row)

    w_in0 = ev_w_in[0].astype(BF16)
    ua, sza, guz, vln = _inproj0(x, mod0, w_in0, row(sgu_ln_g[0]), row(sgu_ln_b[0]))
    ua_c = _inproj0_ctx(ctx, mod0c, w_in0[:, :S5_W])
    y_g = _s5core(_to_group_chunks(ua, N_CHUNK), _to_group_chunks(ua_c, N_CCHUNK), win, wout, mix, l16)
    s_lat = _from_group_chunks(y_g)
    sguw = sgu_w[0].reshape(SGU_HEADS // 2, 2, SGU_CHUNK, SGU_CHUNK)
    sguw = jnp.transpose(sguw, (0, 2, 1, 3)).reshape(SGU_HEADS // 2, SGU_CHUNK, 2 * SGU_CHUNK).astype(BF16)
    sgub = jnp.repeat(sgu_b[0].T, SGU_HD, axis=1)
    x1 = _tail0(x, s_lat, sza, guz, vln, mod0, glu_w[0].astype(BF16), row(glu_b[0]), sguw, sgub,
                ev_w_out[0].astype(BF16), row(norm_g[0]), row(norm_b[0]))

    hg, sz = _inproj1(x1, mod1, od_w_in[0].astype(BF16))
    hc = _conv(hg, dw_w[0], row(dw_b[0]))
    return _tail1(x1, hc, sz, mod1, row(conv_ln_g[0]), row(conv_ln_b[0]), od_w_out[0].astype(BF16),
                  row(norm_g[1]), row(norm_b[1]))
```

```python
import functools
import math

import jax
import jax.numpy as jnp
from jax import lax
from jax.experimental import pallas as pl
from jax.experimental.pallas import tpu as pltpu

D = 1024
B = 4
L = 4096
CTX = 256
GRID_W = 64
S5_W = 512
S5_G = 32
S5_H = 16
H_SHIFT = 4
BLK = 128 // S5_H
S5_P = 64
S5_T = 16
SGU_W = 512
SGU_HEADS = 8
SGU_HD = 64
SGU_CHUNK = 128
CONV_K = 31
CONV_HALF = CONV_K // 2
EVEN_IN = 2560
ODD_IN = 3072
DEPTH = 2
DN_ALPHA = (2 * DEPTH) ** 0.25
LN_EPS = 1e-5
N_CHUNK = L // S5_T
N_CCHUNK = CTX // S5_T
VMEM_LIMIT_V7X = 56 * 1024 * 1024

F32 = jnp.float32
BF16 = jnp.bfloat16


def _gelu(x):
    return 0.5 * x * (1.0 + jnp.tanh(math.sqrt(2.0 / math.pi) * (x + 0.044715 * (x * x * x))))


def _sigmoid(x):
    return 0.5 * (1.0 + jnp.tanh(0.5 * x))


def _silu(x):
    return x * _sigmoid(x)


def _layer_norm(x, g, b):
    mu = jnp.mean(x, axis=-1, keepdims=True)
    xc = x - mu
    var = jnp.mean(xc * xc, axis=-1, keepdims=True)
    return xc * lax.rsqrt(var + LN_EPS) * g + b


def _params(*sem):
    return pltpu.CompilerParams(dimension_semantics=sem, vmem_limit_bytes=VMEM_LIMIT_V7X)


def _adaln_kernel(c_ref, w_ref, b_ref, o_ref):
    a = _silu(c_ref[...])
    o_ref[...] = jnp.dot(a, w_ref[...], preferred_element_type=F32,
                         precision=lax.Precision.HIGHEST) + b_ref[...]


def _adaln(cond8, mod_w, mod_b):
    tn = 512
    return pl.pallas_call(
        _adaln_kernel,
        out_shape=jax.ShapeDtypeStruct((DEPTH, 8, 3 * D), F32),
        grid=(DEPTH, 3 * D // tn),
        in_specs=[pl.BlockSpec((8, D), lambda l, j: (0, 0)),
                  pl.BlockSpec((None, D, tn), lambda l, j: (l, 0, j)),
                  pl.BlockSpec((None, 1, tn), lambda l, j: (l, 0, j))],
        out_specs=pl.BlockSpec((None, 8, tn), lambda l, j: (l, 0, j)),
        compiler_params=_params("arbitrary", "arbitrary"),
        name="adaln",
    )(cond8, mod_w, mod_b.reshape(DEPTH, 1, 3 * D))


def _disc_kernel(lr_ref, li_ref, ldt_ref, obr_ref, obi_ref, ocr_ref, oci_ref):
    lr = lr_ref[...]
    li = li_ref[...]
    dt = jnp.exp(ldt_ref[...])
    mag = jnp.exp(lr * dt)
    br = mag * jnp.cos(li * dt)
    bi = mag * jnp.sin(li * dt)
    inv = 1.0 / (lr * lr + li * li)
    nr = br - 1.0
    obr_ref[...] = br
    obi_ref[...] = bi
    ocr_ref[...] = (nr * lr + bi * li) * inv
    oci_ref[...] = (bi * lr - nr * li) * inv


def _discretise(lam_re, lam_im, log_dt):
    shp = jax.ShapeDtypeStruct((2 * S5_G, S5_P), F32)
    ldt = jnp.broadcast_to(log_dt.reshape(2 * S5_G, 1), (2 * S5_G, S5_P))
    outs = pl.pallas_call(
        _disc_kernel, out_shape=(shp, shp, shp, shp), name="s5_discretise",
    )(lam_re.reshape(2 * S5_G, S5_P), lam_im.reshape(2 * S5_G, S5_P), ldt)
    return [o.reshape(2, S5_G, S5_P) for o in outs]


def _cpow(base_pows, j):
    re = None
    im = None
    for k, (pr, pi) in enumerate(base_pows):
        bit = ((j >> k) & 1) == 1
        mr = jnp.where(bit, pr, 1.0)
        mi = jnp.where(bit, pi, 0.0)
        if re is None:
            re, im = mr, mi
        else:
            re, im = re * mr - im * mi, re * mi + im * mr
    return re, im


def _squarings(pr, pi, n):
    out = [(pr, pi)]
    for _ in range(n - 1):
        pr, pi = pr * pr - pi * pi, 2.0 * pr * pi
        out.append((pr, pi))
    return out


def _shift_lanes(x, n):
    lane = lax.broadcasted_iota(jnp.int32, (S5_H, 128), 1)
    lo, hi = x[:, :128], x[:, 128:]
    if n == 0:
        return x
    if n < 128:
        rlo = pltpu.roll(lo, n, axis=1)
        rhi = pltpu.roll(hi, n, axis=1)
        return jnp.concatenate([jnp.where(lane >= n, rlo, 0.0), jnp.where(lane >= n, rhi, rlo)], axis=1)
    m = n - 128
    rlo = lo if m == 0 else pltpu.roll(lo, m, axis=1)
    return jnp.concatenate([jnp.zeros_like(lo), jnp.where(lane >= m, rlo, 0.0)], axis=1)


def _unshift_lanes(x, n):
    lane = lax.broadcasted_iota(jnp.int32, (S5_H, 128), 1)
    lo, hi = x[:, :128], x[:, 128:]
    if n == 0:
        return x
    if n < 128:
        rlo = pltpu.roll(lo, 128 - n, axis=1)
        rhi = pltpu.roll(hi, 128 - n, axis=1)
        keep = lane < 128 - n
        return jnp.concatenate([jnp.where(keep, rlo, rhi), jnp.where(keep, rhi, 0.0)], axis=1)
    m = n - 128
    rhi = hi if m == 0 else pltpu.roll(hi, 128 - m, axis=1)
    return jnp.concatenate([jnp.where(lane < 128 - m, rhi, 0.0), jnp.zeros_like(lo)], axis=1)


def _s5w_kernel(lrow_re, lrow_im, crow_re, crow_im, bt_re, bt_im,
                lcol_re, lcol_im, ct_re, ct_im, d_ref,
                win_ref, wout_ref, mix_ref, l16_ref):
    TH = S5_T * S5_H
    bg = pl.program_id(0) & (BLK - 1)

    def chunk_pos(idx):
        return (((idx >> H_SHIFT) - bg) & (BLK - 1)) + ((idx >> 7) << 3)

    lr = lrow_re[...]
    li = lrow_im[...]
    pows_row = _squarings(lr, li, 5)
    l16_ref[0:1, :] = pows_row[4][0]
    l16_ref[1:2, :] = pows_row[4][1]
    l16_ref[2:8, :] = jnp.zeros((6, 128), F32)
    cr = crow_re[...]
    ci = crow_im[...]
    btr = bt_re[...]
    bti = bt_im[...]
    bbr = cr * btr - ci * bti
    bbi = cr * bti + ci * btr
    s_idx = chunk_pos(lax.broadcasted_iota(jnp.int32, (TH, 128), 0))
    lane = lax.broadcasted_iota(jnp.int32, (TH, 128), 1)
    jw = jnp.where(lane < S5_P, S5_T - 1 - s_idx, s_idx)
    pr, pi = _cpow(pows_row[:4], jw)
    tbr = jnp.broadcast_to(bbr[None], (S5_T, S5_H, 128)).reshape(TH, 128)
    tbi = jnp.broadcast_to(bbi[None], (S5_T, S5_H, 128)).reshape(TH, 128)
    win_ref[:, 0:128] = (pr * tbr - pi * tbi).astype(BF16)
    win_ref[:, 128:256] = (pr * tbi + pi * tbr).astype(BF16)

    cpows = _squarings(lcol_re[...], lcol_im[...], 5)
    row = lax.broadcasted_iota(jnp.int32, (2 * S5_P, TH), 0)
    lane_w = lax.broadcasted_iota(jnp.int32, (2 * S5_P, TH), 1)
    t_idx = chunk_pos(lane_w)
    j_idx = lane_w >> H_SHIFT
    is_f = row < S5_P
    ctr = ct_re[...]
    cti = ct_im[...]
    er, ei = _cpow(cpows, jnp.where(is_f, t_idx + 1, S5_T - t_idx))
    wr = ctr * er - cti * ei
    wi = ctr * ei + cti * er
    wout_ref[0:128, :] = wr.astype(BF16)
    wout_ref[128:256, :] = (-wi).astype(BF16)
    kr, ki = _cpow(cpows[:4], jnp.where(is_f, j_idx, S5_T - 1 - j_idx))
    ekr = ctr * kr - cti * ki
    eki = ctr * ki + cti * kr
    lane16 = lax.broadcasted_iota(jnp.int32, (S5_H, 128), 1)
    mf = lane16 < S5_P
    hp = lax.Precision.HIGHEST
    dot = functools.partial(jnp.dot, preferred_element_type=F32, precision=hp)
    kkf = dot(jnp.where(mf, bbr, 0.0), ekr) - dot(jnp.where(mf, bbi, 0.0), eki)
    kkb = dot(jnp.where(mf, 0.0, bbr), ekr) - dot(jnp.where(mf, 0.0, bbi), eki)
    dl = d_ref[...]
    r16 = lax.broadcasted_iota(jnp.int32, (S5_H, TH), 0)
    l256 = lax.broadcasted_iota(jnp.int32, (S5_H, TH), 1)
    rot = bg * S5_H
    for s in range(S5_T):
        blk = _shift_lanes(kkf, S5_H * s) + _unshift_lanes(kkb, S5_H * (S5_T - 1 - s))
        blk = blk + jnp.where(l256 == r16 + S5_H * s, dl, 0.0)
        blk = jnp.concatenate([pltpu.roll(blk[:, :128], rot, axis=1), pltpu.roll(blk[:, 128:], rot, axis=1)], axis=1)
        rho = ((s + bg) & (BLK - 1)) + (s & BLK)
        mix_ref[pl.ds(pl.multiple_of(rho * S5_H, S5_H), S5_H), :] = blk.astype(BF16)


def _s5_weights(lrow_re, lrow_im, crow_re, crow_im, bt_re, bt_im, lcol_re, lcol_im, ct_re, ct_im, d_row):
    TH = S5_T * S5_H
    g3 = lambda r, c: pl.BlockSpec((None, r, c), lambda g: (g, 0, 0))
    wshape = jax.ShapeDtypeStruct((S5_G, TH, TH), BF16)
    return pl.pallas_call(
        _s5w_kernel,
        out_shape=(wshape, wshape, wshape, jax.ShapeDtypeStruct((S5_G, 8, 128), F32)),
        grid=(S5_G,),
        in_specs=[g3(1, 128)] * 4 + [g3(S5_H, 128)] * 2 + [g3(128, TH)] * 4 + [g3(1, TH)],
        out_specs=(g3(TH, TH), g3(TH, TH), g3(TH, TH), g3(8, 128)),
        compiler_params=_params("arbitrary"),
        name="s5_weights",
    )(lrow_re, lrow_im, crow_re, crow_im, bt_re, bt_im, lcol_re, lcol_im, ct_re, ct_im, d_row)


def _rot_blocks(w, r, axis):
    shp = w.shape
    n = shp[axis]
    w = w.reshape(shp[:axis] + (n // 128, 128) + shp[axis + 1:])
    return jnp.roll(w, S5_H * r, axis=axis + 1).reshape(shp)


def _slabs_of(h, hs_ref):
    h3 = h.reshape(h.shape[0] // S5_T, S5_T, h.shape[1])
    for s in range(S5_T):
        hs_ref[s] = h3[:, s, :].astype(BF16)


def _inproj0n_kernel(x_ref, mod_ref, w_ref, lng_ref, lnb_ref, guz_ref, vln_ref, hs_ref):
    shift = mod_ref[:, 0:D]
    scale = mod_ref[:, D:2 * D]
    h = x_ref[...] * (1.0 + scale) + shift
    _slabs_of(h, hs_ref)
    hb = h.astype(BF16)
    dot = lambda lo: jnp.dot(hb, w_ref[:, lo:lo + 512], preferred_element_type=F32)
    guz_ref[...] = (_gelu(dot(0)) * _silu(dot(1024))).astype(BF16)
    vln_ref[...] = _layer_norm(_gelu(dot(512)), lng_ref[...], lnb_ref[...]).astype(BF16)


def _inproj0n(x, mod, w_nat, ln_g, ln_b, tm=512):
    nct = tm // S5_T
    o = jax.ShapeDtypeStruct((B, L, 512), BF16)
    ospec = pl.BlockSpec((None, tm, 512), lambda b, i: (b, i, 0))
    full = lambda *s: pl.BlockSpec(s, lambda b, i: (0,) * len(s))
    return pl.pallas_call(
        _inproj0n_kernel,
        out_shape=(o, o, jax.ShapeDtypeStruct((S5_T, B * N_CHUNK, D), BF16)),
        grid=(B, L // tm),
        in_specs=[pl.BlockSpec((None, tm, D), lambda b, i: (b, i, 0)),
                  pl.BlockSpec((None, 1, 3 * D), lambda b, i: (b, 0, 0)),
                  full(D, 1536), full(1, 512), full(1, 512)],
        out_specs=(ospec, ospec,
                   pl.BlockSpec((S5_T, nct, D), lambda b, i: (0, b * (N_CHUNK // nct) + i, 0))),
        compiler_params=_params("arbitrary", "arbitrary"),
        name="inproj0n",
    )(x, mod, w_nat, ln_g, ln_b)


def _ctx_slabs_kernel(x_ref, mod_ref, hs_ref):
    h = x_ref[...] * (1.0 + mod_ref[:, D:2 * D]) + mod_ref[:, 0:D]
    _slabs_of(h, hs_ref)


def _ctx_slabs(ctx, mod_c):
    return pl.pallas_call(
        _ctx_slabs_kernel,
        out_shape=jax.ShapeDtypeStruct((S5_T, B * N_CCHUNK, D), BF16),
        grid=(B,),
        in_specs=[pl.BlockSpec((None, CTX, D), lambda b: (b, 0, 0)),
                  pl.BlockSpec((1, 3 * D), lambda b: (0, 0))],
        out_specs=pl.BlockSpec((S5_T, N_CCHUNK, D), lambda b: (0, b, 0)),
        compiler_params=_params("arbitrary"),
        name="ctx_slabs",
    )(ctx, mod_c)


def _inproj0a_kernel(hs_ref, hc_ref, wrot_ref, ua_ref, sza_ref, uc_ref):
    h = hs_ref[...]
    ua_ref[...] = jnp.dot(h, wrot_ref[:, 0:512], preferred_element_type=F32).astype(BF16)
    sza_ref[...] = _silu(jnp.dot(h, wrot_ref[:, 512:1024], preferred_element_type=F32)).astype(BF16)
    uc_ref[...] = jnp.dot(hc_ref[...], wrot_ref[:, 0:512], preferred_element_type=F32).astype(BF16)


def _inproj0a(hs, hcs, w_rot):
    slab = lambda r, h: r + BLK * h
    sspec = lambda n, w: pl.BlockSpec((None, n, w), lambda r, h: (slab(r, h), 0, 0))
    so = lambda n: jax.ShapeDtypeStruct((S5_T, n, 512), BF16)
    nl, ncx = B * N_CHUNK, B * N_CCHUNK
    return pl.pallas_call(
        _inproj0a_kernel,
        out_shape=(so(nl), so(nl), so(ncx)),
        grid=(BLK, S5_T // BLK),
        in_specs=[sspec(nl, D), sspec(ncx, D), pl.BlockSpec((None, D, 1024), lambda r, h: (r, 0, 0))],
        out_specs=(sspec(nl, 512), sspec(nl, 512), sspec(ncx, 512)),
        compiler_params=_params("arbitrary", "arbitrary"),
        name="inproj0a",
    )(hs, hcs, w_rot)


SCAN_GROUPS = 4


def _scan_tiles(sre_ref, sim_ref, hre_ref, him_ref, n_tiles, carry, lams):
    row = lax.broadcasted_iota(jnp.int32, (8, 128), 0)
    lane = lax.broadcasted_iota(jnp.int32, (8, 128), 1)
    first = row < B
    fwd = lane < S5_P

    def body(k, c):
        of = pl.multiple_of(k * 8, 8)
        ob = pl.multiple_of((n_tiles - 1 - k) * 8, 8)
        out = []
        for gi in range(SCAN_GROUPS):
            lre, lim = lams[gi]
            hr, hi = c[2 * gi], c[2 * gi + 1]
            sr = jnp.where(fwd, sre_ref[gi, pl.ds(of, 8), :], pltpu.roll(sre_ref[gi, pl.ds(ob, 8), :], B, axis=0))
            si = jnp.where(fwd, sim_ref[gi, pl.ds(of, 8), :], pltpu.roll(sim_ref[gi, pl.ds(ob, 8), :], B, axis=0))
            h1r = lre * hr - lim * hi + sr
            h1i = lre * hi + lim * hr + si
            r1r = pltpu.roll(h1r, B, axis=0)
            r1i = pltpu.roll(h1i, B, axis=0)
            if hre_ref is not None:
                hre_ref[gi, pl.ds(of, 8), :] = jnp.where(first, hr, r1r)
                him_ref[gi, pl.ds(of, 8), :] = jnp.where(first, hi, r1i)
            h2r = lre * r1r - lim * r1i + sr
            h2i = lre * r1i + lim * r1r + si
            out.append(jnp.where(first, pltpu.roll(h2r, B, axis=0), h2r))
            out.append(jnp.where(first, pltpu.roll(h2i, B, axis=0), h2i))
        return tuple(out)

    return lax.fori_loop(0, n_tiles, body, carry)


def _unreverse_tiles(h_ref, n_tiles):
    lane = lax.broadcasted_iota(jnp.int32, (8, 128), 1)
    fwd = lane < S5_P

    def body(k, carry):
        of = pl.multiple_of(k * 8, 8)
        ob = pl.multiple_of((n_tiles - 1 - k) * 8, 8)
        for gi in range(SCAN_GROUPS):
            a = h_ref[gi, pl.ds(of, 8), :]
            b = h_ref[gi, pl.ds(ob, 8), :]
            h_ref[gi, pl.ds(of, 8), :] = jnp.where(fwd, a, pltpu.roll(b, B, axis=0))
            h_ref[gi, pl.ds(ob, 8), :] = jnp.where(fwd, b, pltpu.roll(a, B, axis=0))
        return carry

    lax.fori_loop(0, n_tiles // 2, body, 0)


def _gather_group(slab_ref, src):
    halves = []
    for h in range(S5_T // BLK):
        acc = slab_ref[BLK * h]
        for s in range(1, BLK):
            acc = jnp.where(src == s, slab_ref[BLK * h + s], acc)
        halves.append(acc)
    return jnp.concatenate(halves, axis=1)


def _s5core_kernel(ul_ref, uc_ref, win_ref, wout_ref, mix_ref, l16_ref, o_ref,
                   u_ref, sre_ref, sim_ref, cre_ref, cim_ref, hre_ref, him_ref, y_ref):
    nl = N_CHUNK * B
    ncx = N_CCHUNK * B
    blk_l = lax.broadcasted_iota(jnp.int32, (nl, 128), 1) >> H_SHIFT
    blk_c = lax.broadcasted_iota(jnp.int32, (ncx, 128), 1) >> H_SHIFT
    for g0 in range(0, BLK, SCAN_GROUPS):
        for gi in range(SCAN_GROUPS):
            bg = g0 + gi
            win = win_ref[bg]
            src_l = ((blk_l - bg) & (BLK - 1)).astype(F32).astype(BF16)
            src_c = ((blk_c - bg) & (BLK - 1)).astype(F32).astype(BF16)
            u = _gather_group(ul_ref, src_l)
            u_ref[gi] = u
            sl = jnp.dot(u, win, preferred_element_type=F32)
            sc = jnp.dot(_gather_group(uc_ref, src_c), win, preferred_element_type=F32)
            for b in range(B):
                sre_ref[gi, pl.ds(b, N_CHUNK, stride=B), :] = sl[N_CHUNK * b:N_CHUNK * (b + 1), 0:128]
                sim_ref[gi, pl.ds(b, N_CHUNK, stride=B), :] = sl[N_CHUNK * b:N_CHUNK * (b + 1), 128:256]
                cre_ref[gi, pl.ds(b, N_CCHUNK, stride=B), :] = sc[N_CCHUNK * b:N_CCHUNK * (b + 1), 0:128]
                cim_ref[gi, pl.ds(b, N_CCHUNK, stride=B), :] = sc[N_CCHUNK * b:N_CCHUNK * (b + 1), 128:256]
        lams = [(jnp.broadcast_to(l16_ref[g0 + gi, 0:1, :], (8, 128)),
                 jnp.broadcast_to(l16_ref[g0 + gi, 1:2, :], (8, 128))) for gi in range(SCAN_GROUPS)]
        zero = tuple(jnp.zeros((8, 128), F32) for _ in range(2 * SCAN_GROUPS))
        carry = _scan_tiles(cre_ref, cim_ref, None, None, ncx // 8, zero, lams)
        _scan_tiles(sre_ref, sim_ref, hre_ref, him_ref, nl // 8, carry, lams)
        _unreverse_tiles(hre_ref, nl // 8)
        _unreverse_tiles(him_ref, nl // 8)
        for gi in range(SCAN_GROUPS):
            bg = g0 + gi
            y = jnp.dot(u_ref[gi], mix_ref[bg], preferred_element_type=F32)
            hs = []
            for b in range(B):
                hs.append(jnp.concatenate([hre_ref[gi, pl.ds(b, N_CHUNK, stride=B), :],
                                           him_ref[gi, pl.ds(b, N_CHUNK, stride=B), :]], axis=1))
            hcat = jnp.concatenate(hs, axis=0).astype(BF16)
            y = y + jnp.dot(hcat, wout_ref[bg], preferred_element_type=F32)
            y_ref[bg] = y.astype(BF16)

    blk = blk_l.astype(F32).astype(BF16)
    for s in range(S5_T):
        h, r = s // BLK, s % BLK
        acc = None
        for j in range(BLK):
            piece = y_ref[(j - r) % BLK, :, 128 * h:128 * (h + 1)]
            acc = piece if acc is None else jnp.where(blk == j, piece, acc)
        o_ref[s] = acc


def _s5core(ul, uc, win, wout, mix, l16):
    TH = S5_T * S5_H
    nl = N_CHUNK * B
    ncx = N_CCHUNK * B
    g4 = lambda r, c: pl.BlockSpec((BLK, r, c), lambda q: (q, 0, 0))
    col = lambda n: pl.BlockSpec((S5_T, n, 128), lambda q: (0, 0, q))
    f32s = lambda n: pltpu.VMEM((SCAN_GROUPS, n, 128), F32)
    return pl.pallas_call(
        _s5core_kernel,
        out_shape=jax.ShapeDtypeStruct((S5_T, nl, S5_W), BF16),
        grid=(S5_G // BLK,),
        in_specs=[col(nl), col(ncx), g4(TH, TH), g4(TH, TH), g4(TH, TH), g4(8, 128)],
        out_specs=col(nl),
        scratch_shapes=[pltpu.VMEM((SCAN_GROUPS, nl, TH), BF16),
                        f32s(nl), f32s(nl), f32s(ncx), f32s(ncx), f32s(nl), f32s(nl),
                        pltpu.VMEM((BLK, nl, TH), BF16)],
        compiler_params=_params("arbitrary"),
        name="s5core",
    )(ul, uc, win, wout, mix, l16)


def _s5tail_kernel(slat_ref, sza_ref, gluw_ref, glub_ref, wtop_ref, y_ref):
    for b in range(B):
        rows = slice(N_CHUNK * b, N_CHUNK * (b + 1))
        g = _gelu(slat_ref[rows, :].astype(F32))
        gate = _sigmoid(jnp.dot(g.astype(BF16), gluw_ref[...], preferred_element_type=F32) + glub_ref[...])
        a = (g * gate * sza_ref[rows, :].astype(F32)).astype(BF16)
        y_ref[rows, :] = jnp.dot(a, wtop_ref[...], preferred_element_type=F32).astype(BF16)


def _s5tail(slat, sza, gluw_rot, glub_rot, wtop_rot):
    slab = lambda r, h: r + BLK * h
    sspec = lambda w: pl.BlockSpec((None, N_CHUNK * B, w), lambda r, h: (slab(r, h), 0, 0))
    rspec = lambda *s: pl.BlockSpec((None,) + s, lambda r, h: (r,) + (0,) * len(s))
    return pl.pallas_call(
        _s5tail_kernel,
        out_shape=jax.ShapeDtypeStruct((S5_T, N_CHUNK * B, D), BF16),
        grid=(BLK, S5_T // BLK),
        in_specs=[sspec(512), sspec(512), rspec(512, 512), rspec(1, 512), rspec(512, D)],
        out_specs=sspec(D),
        compiler_params=_params("arbitrary", "arbitrary"),
        name="s5tail",
    )(slat, sza, gluw_rot, glub_rot, wtop_rot)


def _tail0_kernel(x_ref, ys5_ref, guz_ref, vln_ref, mod_ref, sguw_ref, sgub_ref, wbot_ref, ng_ref, nb_ref, o_ref):
    tm = x_ref.shape[0]
    lane = lax.broadcasted_iota(jnp.int32, (SGU_CHUNK, 128), 1)
    lo = lane < SGU_HD
    zero = jnp.zeros((SGU_CHUNK, 128), BF16)
    chunks = []
    for ci in range(tm // SGU_CHUNK):
        v = vln_ref[ci * SGU_CHUNK:(ci + 1) * SGU_CHUNK, :]
        cols = []
        for pi in range(SGU_HEADS // 2):
            vp = v[:, 128 * pi:128 * (pi + 1)]
            bm = jnp.concatenate([jnp.where(lo, vp, zero), jnp.where(lo, zero, vp)], axis=0)
            cols.append(jnp.dot(sguw_ref[pi], bm, preferred_element_type=F32))
        chunks.append(jnp.concatenate(cols, axis=1) + sgub_ref[...])
    s = jnp.concatenate(chunks, axis=0)
    bsg = (guz_ref[...].astype(F32) * s).astype(BF16)
    nct = tm // S5_T
    ri = lax.broadcasted_iota(jnp.int32, (tm, tm), 0)
    ci = lax.broadcasted_iota(jnp.int32, (tm, tm), 1)
    perm = jnp.where(((ri >> H_SHIFT) == (ci & (nct - 1))) & ((ri & (S5_T - 1)) == (ci >> H_SHIFT)), 1.0, 0.0)
    ys5 = jnp.dot(perm.astype(BF16), ys5_ref[...].reshape(tm, D), preferred_element_type=F32)
    y = ys5 + jnp.dot(bsg, wbot_ref[...], preferred_element_type=F32)
    gmod = mod_ref[:, 2 * D:3 * D]
    o_ref[...] = _layer_norm(DN_ALPHA * x_ref[...] + gmod * y, ng_ref[...], nb_ref[...])


def _tail0(x, ys5, guz, vln, mod, sguw, sgub, w_bot, ng, nb):
    tm = S5_T * S5_T
    nct = tm // S5_T
    t512 = pl.BlockSpec((None, tm, 512), lambda b, i: (b, i, 0))
    tD = pl.BlockSpec((None, tm, D), lambda b, i: (b, i, 0))
    full = lambda *s: pl.BlockSpec(s, lambda b, i: (0,) * len(s))
    return pl.pallas_call(
        _tail0_kernel,
        out_shape=jax.ShapeDtypeStruct((B, L, D), F32),
        grid=(B, L // tm),
        in_specs=[tD, pl.BlockSpec((S5_T, nct, D), lambda b, i: (0, b * (N_CHUNK // nct) + i, 0)), t512, t512,
                  pl.BlockSpec((None, 1, 3 * D), lambda b, i: (b, 0, 0)),
                  full(SGU_HEADS // 2, SGU_CHUNK, 256), full(SGU_CHUNK, 512),
                  full(512, D), full(1, D), full(1, D)],
        out_specs=tD,
        compiler_params=_params("arbitrary", "arbitrary"),
        name="tail0",
    )(x, ys5, guz, vln, mod, sguw, sgub, w_bot, ng, nb)


def _inproj1_kernel(x_ref, mod_ref, w_ref, hg_ref, sz_ref):
    shift = mod_ref[:, 0:D]
    scale = mod_ref[:, D:2 * D]
    h = (x_ref[...] * (1.0 + scale) + shift).astype(BF16)
    dot = lambda lo: jnp.dot(h, w_ref[:, lo:lo + D], preferred_element_type=F32)
    hg_ref[...] = (dot(0) * _sigmoid(dot(D))).astype(BF16)
    sz_ref[...] = _silu(dot(2 * D)).astype(BF16)


def _inproj1(x, mod, w_in, tm=512):
    o = jax.ShapeDtypeStruct((B, L, D), BF16)
    ospec = pl.BlockSpec((None, tm, D), lambda b, i: (b, i, 0))
    return pl.pallas_call(
        _inproj1_kernel,
        out_shape=(o, o),
        grid=(B, L // tm),
        in_specs=[pl.BlockSpec((None, tm, D), lambda b, i: (b, i, 0)),
                  pl.BlockSpec((None, 1, 3 * D), lambda b, i: (b, 0, 0)),
                  pl.BlockSpec((D, ODD_IN), lambda b, i: (0, 0))],
        out_specs=(ospec, ospec),
        compiler_params=_params("arbitrary", "arbitrary"),
        name="inproj1",
    )(x, mod, w_in)


ROW_PAD = 16
COL_PAD = CONV_HALF * GRID_W
CONV_ROWS = 128


def _conv_kernel(h_ref, w_ref, b_ref, o_ref, prow_ref, pcol_ref):
    j = pl.program_id(1)
    bias = b_ref[...]

    @pl.when(j < (D // 2) // 128)
    def _():
        zpad = jnp.zeros((GRID_W, ROW_PAD, 128), F32)
        prow_ref[:, 0:ROW_PAD, :] = zpad
        prow_ref[:, ROW_PAD + GRID_W:, :] = zpad
        prow_ref[:, ROW_PAD:ROW_PAD + GRID_W, :] = h_ref[...].astype(F32).reshape(GRID_W, GRID_W, 128)

        def body(r, carry):
            acc = jnp.zeros((GRID_W, 128), F32) + bias
            for k in range(CONV_K):
                off = ROW_PAD - CONV_HALF + k
                acc = acc + w_ref[k:k + 1, :] * prow_ref[r, off:off + GRID_W, :]
            o_ref[pl.ds(pl.multiple_of(r * GRID_W, GRID_W), GRID_W), :] = acc.astype(BF16)
            return carry

        lax.fori_loop(0, GRID_W, body, 0)

    @pl.when(j >= (D // 2) // 128)
    def _():
        zpad = jnp.zeros((COL_PAD, 128), F32)
        pcol_ref[0:COL_PAD, :] = zpad
        pcol_ref[COL_PAD + L:, :] = zpad
        pcol_ref[COL_PAD:COL_PAD + L, :] = h_ref[...].astype(F32)

        def body(i, carry):
            base = pl.multiple_of(i * CONV_ROWS, CONV_ROWS)
            acc = jnp.zeros((CONV_ROWS, 128), F32) + bias
            for k in range(CONV_K):
                acc = acc + w_ref[k:k + 1, :] * pcol_ref[pl.ds(base + k * GRID_W, CONV_ROWS), :]
            o_ref[pl.ds(base, CONV_ROWS), :] = acc.astype(BF16)
            return carry

        lax.fori_loop(0, L // CONV_ROWS, body, 0)


def _conv(hg, dw_w, dw_b):
    return pl.pallas_call(
        _conv_kernel,
        out_shape=jax.ShapeDtypeStruct((B, L, D), BF16),
        grid=(B, D // 128),
        in_specs=[pl.BlockSpec((None, L, 128), lambda b, j: (b, 0, j)),
                  pl.BlockSpec((CONV_K, 128), lambda b, j: (0, j)),
                  pl.BlockSpec((1, 128), lambda b, j: (0, j))],
        out_specs=pl.BlockSpec((None, L, 128), lambda b, j: (b, 0, j)),
        scratch_shapes=[pltpu.VMEM((GRID_W, GRID_W + 2 * ROW_PAD, 128), F32),
                        pltpu.VMEM((L + 2 * COL_PAD, 128), F32)],
        compiler_params=_params("arbitrary", "arbitrary"),
        name="dwconv",
    )(hg, dw_w, dw_b)


def _tail1_kernel(x_ref, hc_ref, sz_ref, mod_ref, lng_ref, lnb_ref, wout_ref, ng_ref, nb_ref, o_ref):
    m = _silu(_layer_norm(hc_ref[...].astype(F32), lng_ref[...], lnb_ref[...])) * sz_ref[...].astype(F32)
    y = jnp.dot(m.astype(BF16), wout_ref[...], preferred_element_type=F32)
    gmod = mod_ref[:, 2 * D:3 * D]
    o_ref[...] = _layer_norm(DN_ALPHA * x_ref[...] + gmod * y, ng_ref[...], nb_ref[...])


def _tail1(x, hc, sz, mod, ln_g, ln_b, w_out, ng, nb, tm=512):
    tD = pl.BlockSpec((None, tm, D), lambda b, i: (b, i, 0))
    full = lambda *s: pl.BlockSpec(s, lambda b, i: (0,) * len(s))
    return pl.pallas_call(
        _tail1_kernel,
        out_shape=jax.ShapeDtypeStruct((B, L, D), F32),
        grid=(B, L // tm),
        in_specs=[tD, tD, tD, pl.BlockSpec((None, 1, 3 * D), lambda b, i: (b, 0, 0)),
                  full(1, D), full(1, D), full(D, D), full(1, D), full(1, D)],
        out_specs=tD,
        compiler_params=_params("arbitrary", "arbitrary"),
        name="tail1",
    )(x, hc, sz, mod, ln_g, ln_b, w_out, ng, nb)


def kernel(x, c, ctx, c_ctx, mod_w, mod_b, norm_g, norm_b, ev_w_in, ev_w_out, s5_lam_re, s5_lam_im, s5_log_dt, s5_b_re, s5_b_im, s5_c_re, s5_c_im, s5_d, glu_w, glu_b, sgu_ln_g, sgu_ln_b, sgu_w, sgu_b, od_w_in, od_w_out, dw_w, dw_b, conv_ln_g, conv_ln_b):
    TH = S5_T * S5_H
    row = lambda v: v.reshape(1, -1)

    cond8 = jnp.concatenate([c, c_ctx[None], jnp.zeros((3, D), F32)], axis=0)
    mods = _adaln(cond8, mod_w, mod_b)
    mod0 = mods[0, :B].reshape(B, 1, 3 * D)
    mod0c = mods[0, B:B + 1]
    mod1 = mods[1, :B].reshape(B, 1, 3 * D)

    lbr, lbi, cfr, cfi = _discretise(s5_lam_re[0], s5_lam_im[0], s5_log_dt[0])
    rowcat = lambda a: jnp.concatenate([a[0], a[1]], axis=-1).reshape(S5_G, 1, 2 * S5_P)
    colcat = lambda a: jnp.broadcast_to(jnp.concatenate([a[0], a[1]], axis=-1)[:, :, None], (S5_G, 2 * S5_P, TH))
    bt = lambda a: jnp.concatenate([jnp.swapaxes(a[0], 1, 2), jnp.swapaxes(a[1], 1, 2)], axis=-1)
    ct = lambda a: jnp.tile(jnp.concatenate([jnp.swapaxes(a[0], 1, 2), jnp.swapaxes(a[1], 1, 2)], axis=1), (1, 1, S5_T))
    d_row = jnp.tile(s5_d[0].reshape(S5_G, 1, S5_H), (1, 1, S5_T))
    win, wout, mix, l16 = _s5_weights(rowcat(lbr), rowcat(lbi), rowcat(cfr), rowcat(cfi),
                                      bt(s5_b_re[0]), bt(s5_b_im[0]), colcat(lbr), colcat(lbi),
                                      ct(s5_c_re[0]), ct(s5_c_im[0]), d_row)

    w_in0 = ev_w_in[0].astype(BF16)
    w_out0 = ev_w_out[0].astype(BF16)
    glu_w0 = glu_w[0].astype(BF16)
    rots = range(BLK)
    w_rot = jnp.stack([jnp.concatenate([_rot_blocks(w_in0[:, :S5_W], r, 1),
                                        _rot_blocks(w_in0[:, S5_W:2 * S5_W], r, 1)], axis=1) for r in rots])
    gluw_rot = jnp.stack([_rot_blocks(_rot_blocks(glu_w0, r, 0), r, 1) for r in rots])
    glub_rot = jnp.stack([_rot_blocks(row(glu_b[0]), r, 1) for r in rots])
    wtop_rot = jnp.stack([_rot_blocks(w_out0[:S5_W], r, 0) for r in rots])
    guz, vln, hs = _inproj0n(x, mod0, w_in0[:, 2 * S5_W:], row(sgu_ln_g[0]), row(sgu_ln_b[0]))
    ua, sza, ua_c = _inproj0a(hs, _ctx_slabs(ctx, mod0c), w_rot)
    s_lat = _s5core(ua, ua_c, win, wout, mix, l16)
    y_s5 = _s5tail(s_lat, sza, gluw_rot, glub_rot, wtop_rot)
    sguw = sgu_w[0].reshape(SGU_HEADS // 2, 2, SGU_CHUNK, SGU_CHUNK)
    sguw = jnp.transpose(sguw, (0, 2, 1, 3)).reshape(SGU_HEADS // 2, SGU_CHUNK, 2 * SGU_CHUNK).astype(BF16)
    sgub = jnp.repeat(sgu_b[0].T, SGU_HD, axis=1)
    x1 = _tail0(x, y_s5, guz, vln, mod0, sguw, sgub, w_out0[S5_W:], row(norm_g[0]), row(norm_b[0]))

    hg, sz = _inproj1(x1, mod1, od_w_in[0].astype(BF16))
    hc = _conv(hg, dw_w[0], row(dw_b[0]))
    return _tail1(x1, hc, sz, mod1, row(conv_ln_g[0]), row(conv_ln_b[0]), od_w_out[0].astype(BF16),
                  row(norm_g[1]), row(norm_b[1]))
```

```python
import functools
import math

import jax
import jax.numpy as jnp
from jax import lax
from jax.experimental import pallas as pl
from jax.experimental.pallas import tpu as pltpu

D = 1024
B = 4
L = 4096
CTX = 256
GRID_W = 64
S5_W = 512
S5_G = 32
S5_H = 16
H_SHIFT = 4
BLK = 128 // S5_H
S5_P = 64
S5_T = 16
SGU_W = 512
SGU_HEADS = 8
SGU_HD = 64
SGU_CHUNK = 128
CONV_K = 31
CONV_HALF = CONV_K // 2
EVEN_IN = 2560
ODD_IN = 3072
DEPTH = 2
DN_ALPHA = (2 * DEPTH) ** 0.25
LN_EPS = 1e-5
N_CHUNK = L // S5_T
N_CCHUNK = CTX // S5_T
VMEM_LIMIT_V7X = 56 * 1024 * 1024

F32 = jnp.float32
BF16 = jnp.bfloat16


GELU_C = math.sqrt(2.0 / math.pi)


def _gelu(x):
    hx = 0.5 * x
    return hx * jnp.tanh(x * ((x * x) * (0.044715 * GELU_C) + GELU_C)) + hx


def _sigmoid(x):
    return 0.5 * jnp.tanh(0.5 * x) + 0.5


def _silu(x):
    hx = 0.5 * x
    return hx * jnp.tanh(hx) + hx


def _layer_norm(x, g, b):
    mu = jnp.mean(x, axis=-1, keepdims=True)
    xc = x - mu
    var = jnp.mean(xc * xc, axis=-1, keepdims=True)
    return xc * lax.rsqrt(var + LN_EPS) * g + b


def _params(*sem):
    return pltpu.CompilerParams(dimension_semantics=sem, vmem_limit_bytes=VMEM_LIMIT_V7X)


def _adaln_kernel(c_ref, w_ref, b_ref, o_ref):
    a = _silu(c_ref[...])
    o_ref[...] = jnp.dot(a, w_ref[...], preferred_element_type=F32,
                         precision=lax.Precision.HIGHEST) + b_ref[...]


def _adaln(cond8, mod_w, mod_b):
    tn = 512
    return pl.pallas_call(
        _adaln_kernel,
        out_shape=jax.ShapeDtypeStruct((DEPTH, 8, 3 * D), F32),
        grid=(DEPTH, 3 * D // tn),
        in_specs=[pl.BlockSpec((8, D), lambda l, j: (0, 0)),
                  pl.BlockSpec((None, D, tn), lambda l, j: (l, 0, j)),
                  pl.BlockSpec((None, 1, tn), lambda l, j: (l, 0, j))],
        out_specs=pl.BlockSpec((None, 8, tn), lambda l, j: (l, 0, j)),
        compiler_params=_params("arbitrary", "arbitrary"),
        name="adaln",
    )(cond8, mod_w, mod_b.reshape(DEPTH, 1, 3 * D))


def _disc_kernel(lr_ref, li_ref, ldt_ref, obr_ref, obi_ref, ocr_ref, oci_ref):
    lr = lr_ref[...]
    li = li_ref[...]
    dt = jnp.exp(ldt_ref[...])
    mag = jnp.exp(lr * dt)
    br = mag * jnp.cos(li * dt)
    bi = mag * jnp.sin(li * dt)
    inv = 1.0 / (lr * lr + li * li)
    nr = br - 1.0
    obr_ref[...] = br
    obi_ref[...] = bi
    ocr_ref[...] = (nr * lr + bi * li) * inv
    oci_ref[...] = (bi * lr - nr * li) * inv


def _discretise(lam_re, lam_im, log_dt):
    shp = jax.ShapeDtypeStruct((2 * S5_G, S5_P), F32)
    ldt = jnp.broadcast_to(log_dt.reshape(2 * S5_G, 1), (2 * S5_G, S5_P))
    outs = pl.pallas_call(
        _disc_kernel, out_shape=(shp, shp, shp, shp), name="s5_discretise",
    )(lam_re.reshape(2 * S5_G, S5_P), lam_im.reshape(2 * S5_G, S5_P), ldt)
    return [o.reshape(2, S5_G, S5_P) for o in outs]


def _cpow(base_pows, j):
    re = None
    im = None
    for k, (pr, pi) in enumerate(base_pows):
        bit = ((j >> k) & 1) == 1
        mr = jnp.where(bit, pr, 1.0)
        mi = jnp.where(bit, pi, 0.0)
        if re is None:
            re, im = mr, mi
        else:
            re, im = re * mr - im * mi, re * mi + im * mr
    return re, im


def _squarings(pr, pi, n):
    out = [(pr, pi)]
    for _ in range(n - 1):
        pr, pi = pr * pr - pi * pi, 2.0 * pr * pi
        out.append((pr, pi))
    return out


def _shift_lanes(x, n):
    lane = lax.broadcasted_iota(jnp.int32, (S5_H, 128), 1)
    lo, hi = x[:, :128], x[:, 128:]
    if n == 0:
        return x
    if n < 128:
        rlo = pltpu.roll(lo, n, axis=1)
        rhi = pltpu.roll(hi, n, axis=1)
        return jnp.concatenate([jnp.where(lane >= n, rlo, 0.0), jnp.where(lane >= n, rhi, rlo)], axis=1)
    m = n - 128
    rlo = lo if m == 0 else pltpu.roll(lo, m, axis=1)
    return jnp.concatenate([jnp.zeros_like(lo), jnp.where(lane >= m, rlo, 0.0)], axis=1)


def _unshift_lanes(x, n):
    lane = lax.broadcasted_iota(jnp.int32, (S5_H, 128), 1)
    lo, hi = x[:, :128], x[:, 128:]
    if n == 0:
        return x
    if n < 128:
        rlo = pltpu.roll(lo, 128 - n, axis=1)
        rhi = pltpu.roll(hi, 128 - n, axis=1)
        keep = lane < 128 - n
        return jnp.concatenate([jnp.where(keep, rlo, rhi), jnp.where(keep, rhi, 0.0)], axis=1)
    m = n - 128
    rhi = hi if m == 0 else pltpu.roll(hi, 128 - m, axis=1)
    return jnp.concatenate([jnp.where(lane < 128 - m, rhi, 0.0), jnp.zeros_like(lo)], axis=1)


def _s5w_kernel(lrow_re, lrow_im, crow_re, crow_im, bt_re, bt_im,
                lcol_re, lcol_im, ct_re, ct_im, d_ref,
                win_ref, wout_ref, mix_ref, l16_ref):
    TH = S5_T * S5_H
    bg = pl.program_id(0) & (BLK - 1)

    def chunk_pos(idx):
        return (((idx >> H_SHIFT) - bg) & (BLK - 1)) + ((idx >> 7) << 3)

    lr = lrow_re[...]
    li = lrow_im[...]
    pows_row = _squarings(lr, li, 5)
    l16_ref[0:1, :] = pows_row[4][0]
    l16_ref[1:2, :] = pows_row[4][1]
    l16_ref[2:8, :] = jnp.zeros((6, 128), F32)
    cr = crow_re[...]
    ci = crow_im[...]
    btr = bt_re[...]
    bti = bt_im[...]
    bbr = cr * btr - ci * bti
    bbi = cr * bti + ci * btr
    s_idx = chunk_pos(lax.broadcasted_iota(jnp.int32, (TH, 128), 0))
    lane = lax.broadcasted_iota(jnp.int32, (TH, 128), 1)
    jw = jnp.where(lane < S5_P, S5_T - 1 - s_idx, s_idx)
    pr, pi = _cpow(pows_row[:4], jw)
    tbr = jnp.broadcast_to(bbr[None], (S5_T, S5_H, 128)).reshape(TH, 128)
    tbi = jnp.broadcast_to(bbi[None], (S5_T, S5_H, 128)).reshape(TH, 128)
    win_ref[:, 0:128] = (pr * tbr - pi * tbi).astype(BF16)
    win_ref[:, 128:256] = (pr * tbi + pi * tbr).astype(BF16)

    cpows = _squarings(lcol_re[...], lcol_im[...], 5)
    row = lax.broadcasted_iota(jnp.int32, (2 * S5_P, TH), 0)
    lane_w = lax.broadcasted_iota(jnp.int32, (2 * S5_P, TH), 1)
    t_idx = chunk_pos(lane_w)
    j_idx = lane_w >> H_SHIFT
    is_f = row < S5_P
    ctr = ct_re[...]
    cti = ct_im[...]
    er, ei = _cpow(cpows, jnp.where(is_f, t_idx + 1, S5_T - t_idx))
    wr = ctr * er - cti * ei
    wi = ctr * ei + cti * er
    wout_ref[0:128, :] = wr.astype(BF16)
    wout_ref[128:256, :] = (-wi).astype(BF16)
    kr, ki = _cpow(cpows[:4], jnp.where(is_f, j_idx, S5_T - 1 - j_idx))
    ekr = ctr * kr - cti * ki
    eki = ctr * ki + cti * kr
    lane16 = lax.broadcasted_iota(jnp.int32, (S5_H, 128), 1)
    mf = lane16 < S5_P
    hp = lax.Precision.HIGHEST
    dot = functools.partial(jnp.dot, preferred_element_type=F32, precision=hp)
    kkf = dot(jnp.where(mf, bbr, 0.0), ekr) - dot(jnp.where(mf, bbi, 0.0), eki)
    kkb = dot(jnp.where(mf, 0.0, bbr), ekr) - dot(jnp.where(mf, 0.0, bbi), eki)
    dl = d_ref[...]
    r16 = lax.broadcasted_iota(jnp.int32, (S5_H, TH), 0)
    l256 = lax.broadcasted_iota(jnp.int32, (S5_H, TH), 1)
    rot = bg * S5_H
    for s in range(S5_T):
        blk = _shift_lanes(kkf, S5_H * s) + _unshift_lanes(kkb, S5_H * (S5_T - 1 - s))
        blk = blk + jnp.where(l256 == r16 + S5_H * s, dl, 0.0)
        blk = jnp.concatenate([pltpu.roll(blk[:, :128], rot, axis=1), pltpu.roll(blk[:, 128:], rot, axis=1)], axis=1)
        rho = ((s + bg) & (BLK - 1)) + (s & BLK)
        mix_ref[pl.ds(pl.multiple_of(rho * S5_H, S5_H), S5_H), :] = blk.astype(BF16)


def _s5_weights(lrow_re, lrow_im, crow_re, crow_im, bt_re, bt_im, lcol_re, lcol_im, ct_re, ct_im, d_row):
    TH = S5_T * S5_H
    g3 = lambda r, c: pl.BlockSpec((None, r, c), lambda g: (g, 0, 0))
    wshape = jax.ShapeDtypeStruct((S5_G, TH, TH), BF16)
    return pl.pallas_call(
        _s5w_kernel,
        out_shape=(wshape, wshape, wshape, jax.ShapeDtypeStruct((S5_G, 8, 128), F32)),
        grid=(S5_G,),
        in_specs=[g3(1, 128)] * 4 + [g3(S5_H, 128)] * 2 + [g3(128, TH)] * 4 + [g3(1, TH)],
        out_specs=(g3(TH, TH), g3(TH, TH), g3(TH, TH), g3(8, 128)),
        compiler_params=_params("arbitrary"),
        name="s5_weights",
    )(lrow_re, lrow_im, crow_re, crow_im, bt_re, bt_im, lcol_re, lcol_im, ct_re, ct_im, d_row)


def _rot_blocks(v, r):
    cols = [pltpu.roll(v[:, 128 * q:128 * (q + 1)], S5_H * r, axis=1) for q in range(v.shape[1] // 128)]
    return jnp.concatenate(cols, axis=1)


def _slabs_of(h, hs_ref):
    h3 = h.reshape(h.shape[0] // S5_T, S5_T, h.shape[1])
    for s in range(S5_T):
        hs_ref[s] = h3[:, s, :].astype(BF16)


def _inproj0n_kernel(x_ref, mod_ref, w_ref, lng_ref, lnb_ref, guz_ref, vln_ref, hs_ref):
    shift = mod_ref[:, 0:D]
    scale = mod_ref[:, D:2 * D]
    h = x_ref[...] * (1.0 + scale) + shift
    _slabs_of(h, hs_ref)
    hb = h.astype(BF16)
    dot = lambda lo: jnp.dot(hb, w_ref[:, lo:lo + 512], preferred_element_type=F32)
    guz_ref[...] = (_gelu(dot(0)) * _silu(dot(1024))).astype(BF16)
    vln_ref[...] = _layer_norm(_gelu(dot(512)), lng_ref[...], lnb_ref[...]).astype(BF16)


def _inproj0n(x, mod, w_nat, ln_g, ln_b, tm=512):
    nct = tm // S5_T
    o = jax.ShapeDtypeStruct((B, L, 512), BF16)
    ospec = pl.BlockSpec((None, tm, 512), lambda b, i: (b, i, 0))
    full = lambda *s: pl.BlockSpec(s, lambda b, i: (0,) * len(s))
    return pl.pallas_call(
        _inproj0n_kernel,
        out_shape=(o, o, jax.ShapeDtypeStruct((S5_T, B * N_CHUNK, D), BF16)),
        grid=(B, L // tm),
        in_specs=[pl.BlockSpec((None, tm, D), lambda b, i: (b, i, 0)),
                  pl.BlockSpec((None, 1, 3 * D), lambda b, i: (b, 0, 0)),
                  full(D, 1536), full(1, 512), full(1, 512)],
        out_specs=(ospec, ospec,
                   pl.BlockSpec((S5_T, nct, D), lambda b, i: (0, b * (N_CHUNK // nct) + i, 0))),
        compiler_params=_params("arbitrary", "arbitrary"),
        name="inproj0n",
    )(x, mod, w_nat, ln_g, ln_b)


def _ctx_slabs_kernel(x_ref, mod_ref, hs_ref):
    h = x_ref[...] * (1.0 + mod_ref[:, D:2 * D]) + mod_ref[:, 0:D]
    _slabs_of(h, hs_ref)


def _ctx_slabs(ctx, mod_c):
    return pl.pallas_call(
        _ctx_slabs_kernel,
        out_shape=jax.ShapeDtypeStruct((S5_T, B * N_CCHUNK, D), BF16),
        grid=(B,),
        in_specs=[pl.BlockSpec((None, CTX, D), lambda b: (b, 0, 0)),
                  pl.BlockSpec((1, 3 * D), lambda b: (0, 0))],
        out_specs=pl.BlockSpec((S5_T, N_CCHUNK, D), lambda b: (0, b, 0)),
        compiler_params=_params("arbitrary"),
        name="ctx_slabs",
    )(ctx, mod_c)


def _inproj0a_kernel(hs_ref, hc_ref, w_ref, ua_ref, sza_ref, uc_ref):
    r = pl.program_id(0)
    h = hs_ref[...]
    ua_ref[...] = _rot_blocks(jnp.dot(h, w_ref[:, 0:512], preferred_element_type=F32), r).astype(BF16)
    sza_ref[...] = _silu(jnp.dot(h, w_ref[:, 512:1024], preferred_element_type=F32)).astype(BF16)
    uc_ref[...] = _rot_blocks(jnp.dot(hc_ref[...], w_ref[:, 0:512], preferred_element_type=F32), r).astype(BF16)


def _inproj0a(hs, hcs, w_s5):
    slab = lambda r, h: r + BLK * h
    sspec = lambda n, w: pl.BlockSpec((None, n, w), lambda r, h: (slab(r, h), 0, 0))
    so = lambda n: jax.ShapeDtypeStruct((S5_T, n, 512), BF16)
    nl, ncx = B * N_CHUNK, B * N_CCHUNK
    return pl.pallas_call(
        _inproj0a_kernel,
        out_shape=(so(nl), so(nl), so(ncx)),
        grid=(BLK, S5_T // BLK),
        in_specs=[sspec(nl, D), sspec(ncx, D), pl.BlockSpec((D, 1024), lambda r, h: (0, 0))],
        out_specs=(sspec(nl, 512), sspec(nl, 512), sspec(ncx, 512)),
        compiler_params=_params("arbitrary", "arbitrary"),
        name="inproj0a",
    )(hs, hcs, w_s5)


SCAN_GROUPS = 4


def _scan_tiles(sre_ref, sim_ref, hre_ref, him_ref, n_tiles, carry, lams):
    row = lax.broadcasted_iota(jnp.int32, (8, 128), 0)
    lane = lax.broadcasted_iota(jnp.int32, (8, 128), 1)
    first = row < B
    fwd = lane < S5_P

    def body(k, c):
        of = pl.multiple_of(k * 8, 8)
        ob = pl.multiple_of((n_tiles - 1 - k) * 8, 8)
        out = []
        for gi in range(SCAN_GROUPS):
            lre, lim = lams[gi]
            hr, hi = c[2 * gi], c[2 * gi + 1]
            sr = jnp.where(fwd, sre_ref[gi, pl.ds(of, 8), :], pltpu.roll(sre_ref[gi, pl.ds(ob, 8), :], B, axis=0))
            si = jnp.where(fwd, sim_ref[gi, pl.ds(of, 8), :], pltpu.roll(sim_ref[gi, pl.ds(ob, 8), :], B, axis=0))
            h1r = lre * hr - lim * hi + sr
            h1i = lre * hi + lim * hr + si
            r1r = pltpu.roll(h1r, B, axis=0)
            r1i = pltpu.roll(h1i, B, axis=0)
            if hre_ref is not None:
                hre_ref[gi, pl.ds(of, 8), :] = jnp.where(first, hr, r1r)
                him_ref[gi, pl.ds(of, 8), :] = jnp.where(first, hi, r1i)
            h2r = lre * r1r - lim * r1i + sr
            h2i = lre * r1i + lim * r1r + si
            out.append(jnp.where(first, pltpu.roll(h2r, B, axis=0), h2r))
            out.append(jnp.where(first, pltpu.roll(h2i, B, axis=0), h2i))
        return tuple(out)

    return lax.fori_loop(0, n_tiles, body, carry)


def _unreverse_tiles(h_ref, n_tiles):
    lane = lax.broadcasted_iota(jnp.int32, (8, 128), 1)
    fwd = lane < S5_P

    def body(k, carry):
        of = pl.multiple_of(k * 8, 8)
        ob = pl.multiple_of((n_tiles - 1 - k) * 8, 8)
        for gi in range(SCAN_GROUPS):
            a = h_ref[gi, pl.ds(of, 8), :]
            b = h_ref[gi, pl.ds(ob, 8), :]
            h_ref[gi, pl.ds(of, 8), :] = jnp.where(fwd, a, pltpu.roll(b, B, axis=0))
            h_ref[gi, pl.ds(ob, 8), :] = jnp.where(fwd, b, pltpu.roll(a, B, axis=0))
        return carry

    lax.fori_loop(0, n_tiles // 2, body, 0)


def _gather_group(slab_ref, src):
    halves = []
    for h in range(S5_T // BLK):
        acc = slab_ref[BLK * h]
        for s in range(1, BLK):
            acc = jnp.where(src == s, slab_ref[BLK * h + s], acc)
        halves.append(acc)
    return jnp.concatenate(halves, axis=1)


def _s5core_kernel(ul_ref, uc_ref, win_ref, wout_ref, mix_ref, l16_ref, o_ref,
                   u_ref, sre_ref, sim_ref, cre_ref, cim_ref, hre_ref, him_ref, y_ref):
    nl = N_CHUNK * B
    ncx = N_CCHUNK * B
    blk_l = lax.broadcasted_iota(jnp.int32, (nl, 128), 1) >> H_SHIFT
    blk_c = lax.broadcasted_iota(jnp.int32, (ncx, 128), 1) >> H_SHIFT
    for g0 in range(0, BLK, SCAN_GROUPS):
        for gi in range(SCAN_GROUPS):
            bg = g0 + gi
            win = win_ref[bg]
            src_l = ((blk_l - bg) & (BLK - 1)).astype(F32).astype(BF16)
            src_c = ((blk_c - bg) & (BLK - 1)).astype(F32).astype(BF16)
            u = _gather_group(ul_ref, src_l)
            u_ref[gi] = u
            sl = jnp.dot(u, win, preferred_element_type=F32)
            sc = jnp.dot(_gather_group(uc_ref, src_c), win, preferred_element_type=F32)
            for b in range(B):
                sre_ref[gi, pl.ds(b, N_CHUNK, stride=B), :] = sl[N_CHUNK * b:N_CHUNK * (b + 1), 0:128]
                sim_ref[gi, pl.ds(b, N_CHUNK, stride=B), :] = sl[N_CHUNK * b:N_CHUNK * (b + 1), 128:256]
                cre_ref[gi, pl.ds(b, N_CCHUNK, stride=B), :] = sc[N_CCHUNK * b:N_CCHUNK * (b + 1), 0:128]
                cim_ref[gi, pl.ds(b, N_CCHUNK, stride=B), :] = sc[N_CCHUNK * b:N_CCHUNK * (b + 1), 128:256]
        lams = [(jnp.broadcast_to(l16_ref[g0 + gi, 0:1, :], (8, 128)),
                 jnp.broadcast_to(l16_ref[g0 + gi, 1:2, :], (8, 128))) for gi in range(SCAN_GROUPS)]
        zero = tuple(jnp.zeros((8, 128), F32) for _ in range(2 * SCAN_GROUPS))
        carry = _scan_tiles(cre_ref, cim_ref, None, None, ncx // 8, zero, lams)
        _scan_tiles(sre_ref, sim_ref, hre_ref, him_ref, nl // 8, carry, lams)
        _unreverse_tiles(hre_ref, nl // 8)
        _unreverse_tiles(him_ref, nl // 8)
        for gi in range(SCAN_GROUPS):
            bg = g0 + gi
            y = jnp.dot(u_ref[gi], mix_ref[bg], preferred_element_type=F32)
            hs = []
            for b in range(B):
                hs.append(jnp.concatenate([hre_ref[gi, pl.ds(b, N_CHUNK, stride=B), :],
                                           him_ref[gi, pl.ds(b, N_CHUNK, stride=B), :]], axis=1))
            hcat = jnp.concatenate(hs, axis=0).astype(BF16)
            y = y + jnp.dot(hcat, wout_ref[bg], preferred_element_type=F32)
            y_ref[bg] = y.astype(BF16)

    blk = blk_l.astype(F32).astype(BF16)
    for s in range(S5_T):
        h, r = s // BLK, s % BLK
        acc = None
        for j in range(BLK):
            piece = y_ref[(j - r) % BLK, :, 128 * h:128 * (h + 1)]
            acc = piece if acc is None else jnp.where(blk == j, piece, acc)
        o_ref[s] = acc


def _s5core(ul, uc, win, wout, mix, l16):
    TH = S5_T * S5_H
    nl = N_CHUNK * B
    ncx = N_CCHUNK * B
    g4 = lambda r, c: pl.BlockSpec((BLK, r, c), lambda q: (q, 0, 0))
    col = lambda n: pl.BlockSpec((S5_T, n, 128), lambda q: (0, 0, q))
    f32s = lambda n: pltpu.VMEM((SCAN_GROUPS, n, 128), F32)
    return pl.pallas_call(
        _s5core_kernel,
        out_shape=jax.ShapeDtypeStruct((S5_T, nl, S5_W), BF16),
        grid=(S5_G // BLK,),
        in_specs=[col(nl), col(ncx), g4(TH, TH), g4(TH, TH), g4(TH, TH), g4(8, 128)],
        out_specs=col(nl),
        scratch_shapes=[pltpu.VMEM((SCAN_GROUPS, nl, TH), BF16),
                        f32s(nl), f32s(nl), f32s(ncx), f32s(ncx), f32s(nl), f32s(nl),
                        pltpu.VMEM((BLK, nl, TH), BF16)],
        compiler_params=_params("arbitrary"),
        name="s5core",
    )(ul, uc, win, wout, mix, l16)


def _s5tail_kernel(slat_ref, sza_ref, gluw_ref, glub_ref, wtop_ref, y_ref):
    unrot = (BLK - pl.program_id(0)) & (BLK - 1)
    for b in range(B):
        rows = slice(N_CHUNK * b, N_CHUNK * (b + 1))
        g = _gelu(_rot_blocks(slat_ref[rows, :].astype(F32), unrot))
        gate = _sigmoid(jnp.dot(g.astype(BF16), gluw_ref[...], preferred_element_type=F32) + glub_ref[...])
        a = (g * gate * sza_ref[rows, :].astype(F32)).astype(BF16)
        y_ref[rows, :] = jnp.dot(a, wtop_ref[...], preferred_element_type=F32).astype(BF16)


def _s5tail(slat, sza, glu_w, glu_b, w_top):
    slab = lambda r, h: r + BLK * h
    sspec = lambda w: pl.BlockSpec((None, N_CHUNK * B, w), lambda r, h: (slab(r, h), 0, 0))
    full = lambda *s: pl.BlockSpec(s, lambda r, h: (0,) * len(s))
    return pl.pallas_call(
        _s5tail_kernel,
        out_shape=jax.ShapeDtypeStruct((S5_T, N_CHUNK * B, D), BF16),
        grid=(BLK, S5_T // BLK),
        in_specs=[sspec(512), sspec(512), full(512, 512), full(1, 512), full(512, D)],
        out_specs=sspec(D),
        compiler_params=_params("arbitrary", "arbitrary"),
        name="s5tail",
    )(slat, sza, glu_w, glu_b, w_top)


def _tail0_kernel(x_ref, ys5_ref, guz_ref, vln_ref, mod_ref, sguw_ref, sgub_ref, wbot_ref, ng_ref, nb_ref, o_ref):
    tm = x_ref.shape[0]
    lane = lax.broadcasted_iota(jnp.int32, (SGU_CHUNK, 128), 1)
    lo = lane < SGU_HD
    zero = jnp.zeros((SGU_CHUNK, 128), BF16)
    chunks = []
    for ci in range(tm // SGU_CHUNK):
        v = vln_ref[ci * SGU_CHUNK:(ci + 1) * SGU_CHUNK, :]
        cols = []
        for pi in range(SGU_HEADS // 2):
            vp = v[:, 128 * pi:128 * (pi + 1)]
            bm = jnp.concatenate([jnp.where(lo, vp, zero), jnp.where(lo, zero, vp)], axis=0)
            cols.append(jnp.dot(sguw_ref[pi], bm, preferred_element_type=F32))
        chunks.append(jnp.concatenate(cols, axis=1) + sgub_ref[...])
    s = jnp.concatenate(chunks, axis=0)
    bsg = (guz_ref[...].astype(F32) * s).astype(BF16)
    nct = tm // S5_T
    ri = lax.broadcasted_iota(jnp.int32, (tm, tm), 0)
    ci = lax.broadcasted_iota(jnp.int32, (tm, tm), 1)
    perm = jnp.where(((ri >> H_SHIFT) == (ci & (nct - 1))) & ((ri & (S5_T - 1)) == (ci >> H_SHIFT)), 1.0, 0.0)
    ys5 = jnp.dot(perm.astype(BF16), ys5_ref[...].reshape(tm, D), preferred_element_type=F32)
    y = ys5 + jnp.dot(bsg, wbot_ref[...], preferred_element_type=F32)
    gmod = mod_ref[:, 2 * D:3 * D]
    o_ref[...] = _layer_norm(DN_ALPHA * x_ref[...] + gmod * y, ng_ref[...], nb_ref[...])


def _tail0(x, ys5, guz, vln, mod, sguw, sgub, w_bot, ng, nb):
    tm = S5_T * S5_T
    nct = tm // S5_T
    t512 = pl.BlockSpec((None, tm, 512), lambda b, i: (b, i, 0))
    tD = pl.BlockSpec((None, tm, D), lambda b, i: (b, i, 0))
    full = lambda *s: pl.BlockSpec(s, lambda b, i: (0,) * len(s))
    return pl.pallas_call(
        _tail0_kernel,
        out_shape=jax.ShapeDtypeStruct((B, L, D), F32),
        grid=(B, L // tm),
        in_specs=[tD, pl.BlockSpec((S5_T, nct, D), lambda b, i: (0, b * (N_CHUNK // nct) + i, 0)), t512, t512,
                  pl.BlockSpec((None, 1, 3 * D), lambda b, i: (b, 0, 0)),
                  full(SGU_HEADS // 2, SGU_CHUNK, 256), full(SGU_CHUNK, 512),
                  full(512, D), full(1, D), full(1, D)],
        out_specs=tD,
        compiler_params=_params("arbitrary", "arbitrary"),
        name="tail0",
    )(x, ys5, guz, vln, mod, sguw, sgub, w_bot, ng, nb)


def _inproj1_kernel(x_ref, mod_ref, w_ref, hg_ref, sz_ref):
    shift = mod_ref[:, 0:D]
    scale = mod_ref[:, D:2 * D]
    h = (x_ref[...] * (1.0 + scale) + shift).astype(BF16)
    dot = lambda lo: jnp.dot(h, w_ref[:, lo:lo + D], preferred_element_type=F32)
    hg_ref[...] = (dot(0) * _sigmoid(dot(D))).astype(BF16)
    sz_ref[...] = _silu(dot(2 * D)).astype(BF16)


def _inproj1(x, mod, w_in, tm=512):
    o = jax.ShapeDtypeStruct((B, L, D), BF16)
    ospec = pl.BlockSpec((None, tm, D), lambda b, i: (b, i, 0))
    return pl.pallas_call(
        _inproj1_kernel,
        out_shape=(o, o),
        grid=(B, L // tm),
        in_specs=[pl.BlockSpec((None, tm, D), lambda b, i: (b, i, 0)),
                  pl.BlockSpec((None, 1, 3 * D), lambda b, i: (b, 0, 0)),
                  pl.BlockSpec((D, ODD_IN), lambda b, i: (0, 0))],
        out_specs=(ospec, ospec),
        compiler_params=_params("arbitrary", "arbitrary"),
        name="inproj1",
    )(x, mod, w_in)


ROW_PAD = 16
COL_PAD = CONV_HALF * GRID_W
CONV_ROWS = 128


def _conv_kernel(h_ref, w_ref, b_ref, o_ref, prow_ref, pcol_ref):
    j = pl.program_id(1)
    bias = b_ref[...]

    @pl.when(j < (D // 2) // 128)
    def _():
        zpad = jnp.zeros((GRID_W, ROW_PAD, 128), F32)
        prow_ref[:, 0:ROW_PAD, :] = zpad
        prow_ref[:, ROW_PAD + GRID_W:, :] = zpad
        prow_ref[:, ROW_PAD:ROW_PAD + GRID_W, :] = h_ref[...].astype(F32).reshape(GRID_W, GRID_W, 128)

        def body(r, carry):
            acc = jnp.zeros((GRID_W, 128), F32) + bias
            for k in range(CONV_K):
                off = ROW_PAD - CONV_HALF + k
                acc = acc + w_ref[k:k + 1, :] * prow_ref[r, off:off + GRID_W, :]
            o_ref[pl.ds(pl.multiple_of(r * GRID_W, GRID_W), GRID_W), :] = acc.astype(BF16)
            return carry

        lax.fori_loop(0, GRID_W, body, 0)

    @pl.when(j >= (D // 2) // 128)
    def _():
        zpad = jnp.zeros((COL_PAD, 128), F32)
        pcol_ref[0:COL_PAD, :] = zpad
        pcol_ref[COL_PAD + L:, :] = zpad
        pcol_ref[COL_PAD:COL_PAD + L, :] = h_ref[...].astype(F32)

        def body(i, carry):
            base = pl.multiple_of(i * CONV_ROWS, CONV_ROWS)
            acc = jnp.zeros((CONV_ROWS, 128), F32) + bias
            for k in range(CONV_K):
                acc = acc + w_ref[k:k + 1, :] * pcol_ref[pl.ds(base + k * GRID_W, CONV_ROWS), :]
            o_ref[pl.ds(base, CONV_ROWS), :] = acc.astype(BF16)
            return carry

        lax.fori_loop(0, L // CONV_ROWS, body, 0)


def _conv(hg, dw_w, dw_b):
    return pl.pallas_call(
        _conv_kernel,
        out_shape=jax.ShapeDtypeStruct((B, L, D), BF16),
        grid=(B, D // 128),
        in_specs=[pl.BlockSpec((None, L, 128), lambda b, j: (b, 0, j)),
                  pl.BlockSpec((CONV_K, 128), lambda b, j: (0, j)),
                  pl.BlockSpec((1, 128), lambda b, j: (0, j))],
        out_specs=pl.BlockSpec((None, L, 128), lambda b, j: (b, 0, j)),
        scratch_shapes=[pltpu.VMEM((GRID_W, GRID_W + 2 * ROW_PAD, 128), F32),
                        pltpu.VMEM((L + 2 * COL_PAD, 128), F32)],
        compiler_params=_params("arbitrary", "arbitrary"),
        name="dwconv",
    )(hg, dw_w, dw_b)


def _tail1_kernel(x_ref, hc_ref, sz_ref, mod_ref, lng_ref, lnb_ref, wout_ref, ng_ref, nb_ref, o_ref):
    m = _silu(_layer_norm(hc_ref[...].astype(F32), lng_ref[...], lnb_ref[...])) * sz_ref[...].astype(F32)
    y = jnp.dot(m.astype(BF16), wout_ref[...], preferred_element_type=F32)
    gmod = mod_ref[:, 2 * D:3 * D]
    o_ref[...] = _layer_norm(DN_ALPHA * x_ref[...] + gmod * y, ng_ref[...], nb_ref[...])


def _tail1(x, hc, sz, mod, ln_g, ln_b, w_out, ng, nb, tm=512):
    tD = pl.BlockSpec((None, tm, D), lambda b, i: (b, i, 0))
    full = lambda *s: pl.BlockSpec(s, lambda b, i: (0,) * len(s))
    return pl.pallas_call(
        _tail1_kernel,
        out_shape=jax.ShapeDtypeStruct((B, L, D), F32),
        grid=(B, L // tm),
        in_specs=[tD, tD, tD, pl.BlockSpec((None, 1, 3 * D), lambda b, i: (b, 0, 0)),
                  full(1, D), full(1, D), full(D, D), full(1, D), full(1, D)],
        out_specs=tD,
        compiler_params=_params("arbitrary", "arbitrary"),
        name="tail1",
    )(x, hc, sz, mod, ln_g, ln_b, w_out, ng, nb)


def kernel(x, c, ctx, c_ctx, mod_w, mod_b, norm_g, norm_b, ev_w_in, ev_w_out, s5_lam_re, s5_lam_im, s5_log_dt, s5_b_re, s5_b_im, s5_c_re, s5_c_im, s5_d, glu_w, glu_b, sgu_ln_g, sgu_ln_b, sgu_w, sgu_b, od_w_in, od_w_out, dw_w, dw_b, conv_ln_g, conv_ln_b):
    TH = S5_T * S5_H
    row = lambda v: v.reshape(1, -1)

    cond8 = jnp.concatenate([c, c_ctx[None], jnp.zeros((3, D), F32)], axis=0)
    mods = _adaln(cond8, mod_w, mod_b)
    mod0 = mods[0, :B].reshape(B, 1, 3 * D)
    mod0c = mods[0, B:B + 1]
    mod1 = mods[1, :B].reshape(B, 1, 3 * D)

    lbr, lbi, cfr, cfi = _discretise(s5_lam_re[0], s5_lam_im[0], s5_log_dt[0])
    rowcat = lambda a: jnp.concatenate([a[0], a[1]], axis=-1).reshape(S5_G, 1, 2 * S5_P)
    colcat = lambda a: jnp.broadcast_to(jnp.concatenate([a[0], a[1]], axis=-1)[:, :, None], (S5_G, 2 * S5_P, TH))
    bt = lambda a: jnp.concatenate([jnp.swapaxes(a[0], 1, 2), jnp.swapaxes(a[1], 1, 2)], axis=-1)
    ct = lambda a: jnp.tile(jnp.concatenate([jnp.swapaxes(a[0], 1, 2), jnp.swapaxes(a[1], 1, 2)], axis=1), (1, 1, S5_T))
    d_row = jnp.tile(s5_d[0].reshape(S5_G, 1, S5_H), (1, 1, S5_T))
    win, wout, mix, l16 = _s5_weights(rowcat(lbr), rowcat(lbi), rowcat(cfr), rowcat(cfi),
                                      bt(s5_b_re[0]), bt(s5_b_im[0]), colcat(lbr), colcat(lbi),
                                      ct(s5_c_re[0]), ct(s5_c_im[0]), d_row)

    w_in0 = ev_w_in[0].astype(BF16)
    w_out0 = ev_w_out[0].astype(BF16)
    glu_w0 = glu_w[0].astype(BF16)
    guz, vln, hs = _inproj0n(x, mod0, w_in0[:, 2 * S5_W:], row(sgu_ln_g[0]), row(sgu_ln_b[0]))
    ua, sza, ua_c = _inproj0a(hs, _ctx_slabs(ctx, mod0c), w_in0[:, :2 * S5_W])
    s_lat = _s5core(ua, ua_c, win, wout, mix, l16)
    y_s5 = _s5tail(s_lat, sza, glu_w0, row(glu_b[0]), w_out0[:S5_W])
    sguw = sgu_w[0].reshape(SGU_HEADS // 2, 2, SGU_CHUNK, SGU_CHUNK)
    sguw = jnp.transpose(sguw, (0, 2, 1, 3)).reshape(SGU_HEADS // 2, SGU_CHUNK, 2 * SGU_CHUNK).astype(BF16)
    sgub = jnp.repeat(sgu_b[0].T, SGU_HD, axis=1)
    x1 = _tail0(x, y_s5, guz, vln, mod0, sguw, sgub, w_out0[S5_W:], row(norm_g[0]), row(norm_b[0]))

    hg, sz = _inproj1(x1, mod1, od_w_in[0].astype(BF16))
    hc = _conv(hg, dw_w[0], row(dw_b[0]))
    return _tail1(x1, hc, sz, mod1, row(conv_ln_g[0]), row(conv_ln_b[0]), od_w_out[0].astype(BF16),
                  row(norm_g[1]), row(norm_b[1]))
```

```python
import functools
import math

import jax
import jax.numpy as jnp
from jax import lax
from jax.experimental import pallas as pl
from jax.experimental.pallas import tpu as pltpu

D = 1024
B = 4
L = 4096
CTX = 256
GRID_W = 64
S5_W = 512
S5_G = 32
S5_H = 16
H_SHIFT = 4
BLK = 128 // S5_H
S5_P = 64
S5_T = 16
SGU_W = 512
SGU_HEADS = 8
SGU_HD = 64
SGU_CHUNK = 128
CONV_K = 31
CONV_HALF = CONV_K // 2
EVEN_IN = 2560
ODD_IN = 3072
DEPTH = 2
DN_ALPHA = (2 * DEPTH) ** 0.25
LN_EPS = 1e-5
N_CHUNK = L // S5_T
N_CCHUNK = CTX // S5_T
VMEM_LIMIT_V7X = 56 * 1024 * 1024

F32 = jnp.float32
BF16 = jnp.bfloat16


GELU_C = math.sqrt(2.0 / math.pi)


def _gelu(x):
    hx = 0.5 * x
    return hx * jnp.tanh(x * ((x * x) * (0.044715 * GELU_C) + GELU_C)) + hx


def _sigmoid(x):
    return 0.5 * jnp.tanh(0.5 * x) + 0.5


def _silu(x):
    hx = 0.5 * x
    return hx * jnp.tanh(hx) + hx


def _layer_norm(x, g, b):
    mu = jnp.mean(x, axis=-1, keepdims=True)
    xc = x - mu
    var = jnp.mean(xc * xc, axis=-1, keepdims=True)
    return xc * lax.rsqrt(var + LN_EPS) * g + b


def _params(*sem):
    return pltpu.CompilerParams(dimension_semantics=sem, vmem_limit_bytes=VMEM_LIMIT_V7X)


def _adaln_kernel(c_ref, w_ref, b_ref, o_ref):
    def split(v):
        hi = v.astype(BF16)
        return hi, (v - hi.astype(F32)).astype(BF16)

    a_hi, a_lo = split(_silu(c_ref[...]))
    w_hi, w_lo = split(w_ref[...])
    dot = functools.partial(jnp.dot, preferred_element_type=F32)
    o_ref[...] = dot(a_hi, w_hi) + dot(a_lo, w_hi) + dot(a_hi, w_lo) + b_ref[...]


def _adaln(cond8, mod_w, mod_b):
    tn = 512
    return pl.pallas_call(
        _adaln_kernel,
        out_shape=jax.ShapeDtypeStruct((DEPTH, 8, 3 * D), F32),
        grid=(DEPTH, 3 * D // tn),
        in_specs=[pl.BlockSpec((8, D), lambda l, j: (0, 0)),
                  pl.BlockSpec((None, D, tn), lambda l, j: (l, 0, j)),
                  pl.BlockSpec((None, 1, tn), lambda l, j: (l, 0, j))],
        out_specs=pl.BlockSpec((None, 8, tn), lambda l, j: (l, 0, j)),
        compiler_params=_params("arbitrary", "arbitrary"),
        name="adaln",
    )(cond8, mod_w, mod_b.reshape(DEPTH, 1, 3 * D))


def _disc_kernel(lr_ref, li_ref, ldt_ref, obr_ref, obi_ref, ocr_ref, oci_ref):
    lr = lr_ref[...]
    li = li_ref[...]
    dt = jnp.exp(ldt_ref[...])
    mag = jnp.exp(lr * dt)
    br = mag * jnp.cos(li * dt)
    bi = mag * jnp.sin(li * dt)
    inv = 1.0 / (lr * lr + li * li)
    nr = br - 1.0
    obr_ref[...] = br
    obi_ref[...] = bi
    ocr_ref[...] = (nr * lr + bi * li) * inv
    oci_ref[...] = (bi * lr - nr * li) * inv


def _discretise(lam_re, lam_im, log_dt):
    shp = jax.ShapeDtypeStruct((2 * S5_G, S5_P), F32)
    ldt = jnp.broadcast_to(log_dt.reshape(2 * S5_G, 1), (2 * S5_G, S5_P))
    outs = pl.pallas_call(
        _disc_kernel, out_shape=(shp, shp, shp, shp), name="s5_discretise",
    )(lam_re.reshape(2 * S5_G, S5_P), lam_im.reshape(2 * S5_G, S5_P), ldt)
    return [o.reshape(2, S5_G, S5_P) for o in outs]


def _cpow(base_pows, j):
    re = None
    im = None
    for k, (pr, pi) in enumerate(base_pows):
        bit = ((j >> k) & 1) == 1
        mr = jnp.where(bit, pr, 1.0)
        mi = jnp.where(bit, pi, 0.0)
        if re is None:
            re, im = mr, mi
        else:
            re, im = re * mr - im * mi, re * mi + im * mr
    return re, im


def _squarings(pr, pi, n):
    out = [(pr, pi)]
    for _ in range(n - 1):
        pr, pi = pr * pr - pi * pi, 2.0 * pr * pi
        out.append((pr, pi))
    return out


def _shift_lanes(x, n):
    lane = lax.broadcasted_iota(jnp.int32, (S5_H, 128), 1)
    lo, hi = x[:, :128], x[:, 128:]
    if n == 0:
        return x
    if n < 128:
        rlo = pltpu.roll(lo, n, axis=1)
        rhi = pltpu.roll(hi, n, axis=1)
        return jnp.concatenate([jnp.where(lane >= n, rlo, 0.0), jnp.where(lane >= n, rhi, rlo)], axis=1)
    m = n - 128
    rlo = lo if m == 0 else pltpu.roll(lo, m, axis=1)
    return jnp.concatenate([jnp.zeros_like(lo), jnp.where(lane >= m, rlo, 0.0)], axis=1)


def _unshift_lanes(x, n):
    lane = lax.broadcasted_iota(jnp.int32, (S5_H, 128), 1)
    lo, hi = x[:, :128], x[:, 128:]
    if n == 0:
        return x
    if n < 128:
        rlo = pltpu.roll(lo, 128 - n, axis=1)
        rhi = pltpu.roll(hi, 128 - n, axis=1)
        keep = lane < 128 - n
        return jnp.concatenate([jnp.where(keep, rlo, rhi), jnp.where(keep, rhi, 0.0)], axis=1)
    m = n - 128
    rhi = hi if m == 0 else pltpu.roll(hi, 128 - m, axis=1)
    return jnp.concatenate([jnp.where(lane < 128 - m, rhi, 0.0), jnp.zeros_like(lo)], axis=1)


def _s5w_kernel(lrow_re, lrow_im, crow_re, crow_im, bt_re, bt_im,
                lcol_re, lcol_im, ct_re, ct_im, d_ref,
                win_ref, wout_ref, mix_ref, l16_ref):
    TH = S5_T * S5_H
    bg = pl.program_id(0) & (BLK - 1)

    def chunk_pos(idx):
        return (((idx >> H_SHIFT) - bg) & (BLK - 1)) + ((idx >> 7) << 3)

    lr = lrow_re[...]
    li = lrow_im[...]
    pows_row = _squarings(lr, li, 5)
    l16_ref[0:1, :] = pows_row[4][0]
    l16_ref[1:2, :] = pows_row[4][1]
    l16_ref[2:8, :] = jnp.zeros((6, 128), F32)
    cr = crow_re[...]
    ci = crow_im[...]
    btr = bt_re[...]
    bti = bt_im[...]
    bbr = cr * btr - ci * bti
    bbi = cr * bti + ci * btr
    s_idx = chunk_pos(lax.broadcasted_iota(jnp.int32, (TH, 128), 0))
    lane = lax.broadcasted_iota(jnp.int32, (TH, 128), 1)
    jw = jnp.where(lane < S5_P, S5_T - 1 - s_idx, s_idx)
    pr, pi = _cpow(pows_row[:4], jw)
    tbr = jnp.broadcast_to(bbr[None], (S5_T, S5_H, 128)).reshape(TH, 128)
    tbi = jnp.broadcast_to(bbi[None], (S5_T, S5_H, 128)).reshape(TH, 128)
    win_ref[:, 0:128] = (pr * tbr - pi * tbi).astype(BF16)
    win_ref[:, 128:256] = (pr * tbi + pi * tbr).astype(BF16)

    cpows = _squarings(lcol_re[...], lcol_im[...], 5)
    row = lax.broadcasted_iota(jnp.int32, (2 * S5_P, TH), 0)
    lane_w = lax.broadcasted_iota(jnp.int32, (2 * S5_P, TH), 1)
    t_idx = chunk_pos(lane_w)
    j_idx = lane_w >> H_SHIFT
    is_f = row < S5_P
    ctr = ct_re[...]
    cti = ct_im[...]
    er, ei = _cpow(cpows[:4], jnp.where(is_f, t_idx, S5_T - 1 - t_idx))
    er, ei = er * cpows[0][0] - ei * cpows[0][1], er * cpows[0][1] + ei * cpows[0][0]
    wr = ctr * er - cti * ei
    wi = ctr * ei + cti * er
    wout_ref[0:128, :] = wr.astype(BF16)
    wout_ref[128:256, :] = (-wi).astype(BF16)
    kr, ki = _cpow(cpows[:4], jnp.where(is_f, j_idx, S5_T - 1 - j_idx))
    ekr = ctr * kr - cti * ki
    eki = ctr * ki + cti * kr
    lane16 = lax.broadcasted_iota(jnp.int32, (S5_H, 128), 1)
    mf = lane16 < S5_P
    hp = lax.Precision.HIGHEST
    dot = functools.partial(jnp.dot, preferred_element_type=F32, precision=hp)
    kkf = dot(jnp.where(mf, bbr, 0.0), ekr) - dot(jnp.where(mf, bbi, 0.0), eki)
    kkb = dot(jnp.where(mf, 0.0, bbr), ekr) - dot(jnp.where(mf, 0.0, bbi), eki)
    dl = d_ref[...]
    r16 = lax.broadcasted_iota(jnp.int32, (S5_H, TH), 0)
    l256 = lax.broadcasted_iota(jnp.int32, (S5_H, TH), 1)
    rot = bg * S5_H
    for s in range(S5_T):
        blk = _shift_lanes(kkf, S5_H * s) + _unshift_lanes(kkb, S5_H * (S5_T - 1 - s))
        blk = blk + jnp.where(l256 == r16 + S5_H * s, dl, 0.0)
        blk = jnp.concatenate([pltpu.roll(blk[:, :128], rot, axis=1), pltpu.roll(blk[:, 128:], rot, axis=1)], axis=1)
        rho = ((s + bg) & (BLK - 1)) + (s & BLK)
        mix_ref[pl.ds(pl.multiple_of(rho * S5_H, S5_H), S5_H), :] = blk.astype(BF16)


def _s5_weights(lrow_re, lrow_im, crow_re, crow_im, bt_re, bt_im, lcol_re, lcol_im, ct_re, ct_im, d_row):
    TH = S5_T * S5_H
    g3 = lambda r, c: pl.BlockSpec((None, r, c), lambda g: (g, 0, 0))
    wshape = jax.ShapeDtypeStruct((S5_G, TH, TH), BF16)
    return pl.pallas_call(
        _s5w_kernel,
        out_shape=(wshape, wshape, wshape, jax.ShapeDtypeStruct((S5_G, 8, 128), F32)),
        grid=(S5_G,),
        in_specs=[g3(1, 128)] * 4 + [g3(S5_H, 128)] * 2 + [g3(128, TH)] * 4 + [g3(1, TH)],
        out_specs=(g3(TH, TH), g3(TH, TH), g3(TH, TH), g3(8, 128)),
        compiler_params=_params("arbitrary"),
        name="s5_weights",
    )(lrow_re, lrow_im, crow_re, crow_im, bt_re, bt_im, lcol_re, lcol_im, ct_re, ct_im, d_row)


def _rot_blocks(v, r):
    cols = [pltpu.roll(v[:, 128 * q:128 * (q + 1)], S5_H * r, axis=1) for q in range(v.shape[1] // 128)]
    return jnp.concatenate(cols, axis=1)


def _slabs_of(h, hs_ref):
    h3 = h.reshape(h.shape[0] // S5_T, S5_T, h.shape[1])
    for s in range(S5_T):
        hs_ref[s] = h3[:, s, :].astype(BF16)


def _inproj0n_kernel(x_ref, mod_ref, w_ref, lng_ref, lnb_ref, guz_ref, vln_ref, hs_ref):
    shift = mod_ref[:, 0:D]
    scale = mod_ref[:, D:2 * D]
    h = x_ref[...] * (1.0 + scale) + shift
    _slabs_of(h, hs_ref)
    hb = h.astype(BF16)
    dot = lambda lo: jnp.dot(hb, w_ref[:, lo:lo + 512], preferred_element_type=F32)
    guz_ref[...] = (_gelu(dot(0)) * _silu(dot(1024))).astype(BF16)
    vln_ref[...] = _layer_norm(_gelu(dot(512)), lng_ref[...], lnb_ref[...]).astype(BF16)


def _inproj0n(x, mod, w_nat, ln_g, ln_b, tm=512):
    nct = tm // S5_T
    o = jax.ShapeDtypeStruct((B, L, 512), BF16)
    ospec = pl.BlockSpec((None, tm, 512), lambda b, i: (b, i, 0))
    full = lambda *s: pl.BlockSpec(s, lambda b, i: (0,) * len(s))
    return pl.pallas_call(
        _inproj0n_kernel,
        out_shape=(o, o, jax.ShapeDtypeStruct((S5_T, B * N_CHUNK, D), BF16)),
        grid=(B, L // tm),
        in_specs=[pl.BlockSpec((None, tm, D), lambda b, i: (b, i, 0)),
                  pl.BlockSpec((None, 1, 3 * D), lambda b, i: (b, 0, 0)),
                  full(D, 1536), full(1, 512), full(1, 512)],
        out_specs=(ospec, ospec,
                   pl.BlockSpec((S5_T, nct, D), lambda b, i: (0, b * (N_CHUNK // nct) + i, 0))),
        compiler_params=_params("arbitrary", "arbitrary"),
        name="inproj0n",
    )(x, mod, w_nat, ln_g, ln_b)


def _ctx_slabs_kernel(x_ref, mod_ref, hs_ref):
    h = x_ref[...] * (1.0 + mod_ref[:, D:2 * D]) + mod_ref[:, 0:D]
    _slabs_of(h, hs_ref)


def _ctx_slabs(ctx, mod_c):
    return pl.pallas_call(
        _ctx_slabs_kernel,
        out_shape=jax.ShapeDtypeStruct((S5_T, B * N_CCHUNK, D), BF16),
        grid=(B,),
        in_specs=[pl.BlockSpec((None, CTX, D), lambda b: (b, 0, 0)),
                  pl.BlockSpec((1, 3 * D), lambda b: (0, 0))],
        out_specs=pl.BlockSpec((S5_T, N_CCHUNK, D), lambda b: (0, b, 0)),
        compiler_params=_params("arbitrary"),
        name="ctx_slabs",
    )(ctx, mod_c)


def _inproj0a_kernel(hs_ref, hc_ref, w_ref, ua_ref, sza_ref, uc_ref):
    r = pl.program_id(0)
    h = hs_ref[...]
    ua_ref[...] = _rot_blocks(jnp.dot(h, w_ref[:, 0:512], preferred_element_type=F32), r).astype(BF16)
    sza_ref[...] = _silu(jnp.dot(h, w_ref[:, 512:1024], preferred_element_type=F32)).astype(BF16)
    uc_ref[...] = _rot_blocks(jnp.dot(hc_ref[...], w_ref[:, 0:512], preferred_element_type=F32), r).astype(BF16)


def _inproj0a(hs, hcs, w_s5):
    slab = lambda r, h: r + BLK * h
    sspec = lambda n, w: pl.BlockSpec((None, n, w), lambda r, h: (slab(r, h), 0, 0))
    so = lambda n: jax.ShapeDtypeStruct((S5_T, n, 512), BF16)
    nl, ncx = B * N_CHUNK, B * N_CCHUNK
    return pl.pallas_call(
        _inproj0a_kernel,
        out_shape=(so(nl), so(nl), so(ncx)),
        grid=(BLK, S5_T // BLK),
        in_specs=[sspec(nl, D), sspec(ncx, D), pl.BlockSpec((D, 1024), lambda r, h: (0, 0))],
        out_specs=(sspec(nl, 512), sspec(nl, 512), sspec(ncx, 512)),
        compiler_params=_params("arbitrary", "arbitrary"),
        name="inproj0a",
    )(hs, hcs, w_s5)


SCAN_GROUPS = 4


def _scan_tiles(sre_ref, sim_ref, hre_ref, him_ref, n_tiles, carry, lams):
    row = lax.broadcasted_iota(jnp.int32, (8, 128), 0)
    lane = lax.broadcasted_iota(jnp.int32, (8, 128), 1)
    first = row < B
    fwd = lane < S5_P

    def body(k, c):
        of = pl.multiple_of(k * 8, 8)
        ob = pl.multiple_of((n_tiles - 1 - k) * 8, 8)
        out = []
        for gi in range(SCAN_GROUPS):
            lre, lim = lams[gi]
            hr, hi = c[2 * gi], c[2 * gi + 1]
            sr = jnp.where(fwd, sre_ref[gi, pl.ds(of, 8), :], pltpu.roll(sre_ref[gi, pl.ds(ob, 8), :], B, axis=0))
            si = jnp.where(fwd, sim_ref[gi, pl.ds(of, 8), :], pltpu.roll(sim_ref[gi, pl.ds(ob, 8), :], B, axis=0))
            h1r = lre * hr - lim * hi + sr
            h1i = lre * hi + lim * hr + si
            r1r = pltpu.roll(h1r, B, axis=0)
            r1i = pltpu.roll(h1i, B, axis=0)
            if hre_ref is not None:
                hre_ref[gi, pl.ds(of, 8), :] = jnp.where(first, hr, r1r)
                him_ref[gi, pl.ds(of, 8), :] = jnp.where(first, hi, r1i)
            h2r = lre * r1r - lim * r1i + sr
            h2i = lre * r1i + lim * r1r + si
            out.append(jnp.where(first, pltpu.roll(h2r, B, axis=0), h2r))
            out.append(jnp.where(first, pltpu.roll(h2i, B, axis=0), h2i))
        return tuple(out)

    return lax.fori_loop(0, n_tiles, body, carry)


def _unreverse_tiles(h_ref, n_tiles):
    lane = lax.broadcasted_iota(jnp.int32, (8, 128), 1)
    fwd = lane < S5_P

    def body(k, carry):
        of = pl.multiple_of(k * 8, 8)
        ob = pl.multiple_of((n_tiles - 1 - k) * 8, 8)
        for gi in range(SCAN_GROUPS):
            a = h_ref[gi, pl.ds(of, 8), :]
            b = h_ref[gi, pl.ds(ob, 8), :]
            h_ref[gi, pl.ds(of, 8), :] = jnp.where(fwd, a, pltpu.roll(b, B, axis=0))
            h_ref[gi, pl.ds(ob, 8), :] = jnp.where(fwd, b, pltpu.roll(a, B, axis=0))
        return carry

    lax.fori_loop(0, n_tiles // 2, body, 0, unroll=4)


def _gather_group(slab_ref, src):
    halves = []
    for h in range(S5_T // BLK):
        acc = slab_ref[BLK * h]
        for s in range(1, BLK):
            acc = jnp.where(src == s, slab_ref[BLK * h + s], acc)
        halves.append(acc)
    return jnp.concatenate(halves, axis=1)


def _s5core_kernel(ul_ref, uc_ref, win_ref, wout_ref, mix_ref, l16_ref, o_ref,
                   u_ref, sre_ref, sim_ref, cre_ref, cim_ref, hre_ref, him_ref, y_ref):
    nl = N_CHUNK * B
    ncx = N_CCHUNK * B
    blk_l = lax.broadcasted_iota(jnp.int32, (nl, 128), 1) >> H_SHIFT
    blk_c = lax.broadcasted_iota(jnp.int32, (ncx, 128), 1) >> H_SHIFT
    for g0 in range(0, BLK, SCAN_GROUPS):
        for gi in range(SCAN_GROUPS):
            bg = g0 + gi
            win = win_ref[bg]
            src_l = ((blk_l - bg) & (BLK - 1)).astype(F32).astype(BF16)
            src_c = ((blk_c - bg) & (BLK - 1)).astype(F32).astype(BF16)
            u = _gather_group(ul_ref, src_l)
            u_ref[gi] = u
            sl = jnp.dot(u, win, preferred_element_type=F32)
            sc = jnp.dot(_gather_group(uc_ref, src_c), win, preferred_element_type=F32)
            for b in range(B):
                sre_ref[gi, pl.ds(b, N_CHUNK, stride=B), :] = sl[N_CHUNK * b:N_CHUNK * (b + 1), 0:128]
                sim_ref[gi, pl.ds(b, N_CHUNK, stride=B), :] = sl[N_CHUNK * b:N_CHUNK * (b + 1), 128:256]
                cre_ref[gi, pl.ds(b, N_CCHUNK, stride=B), :] = sc[N_CCHUNK * b:N_CCHUNK * (b + 1), 0:128]
                cim_ref[gi, pl.ds(b, N_CCHUNK, stride=B), :] = sc[N_CCHUNK * b:N_CCHUNK * (b + 1), 128:256]
        lams = [(jnp.broadcast_to(l16_ref[g0 + gi, 0:1, :], (8, 128)),
                 jnp.broadcast_to(l16_ref[g0 + gi, 1:2, :], (8, 128))) for gi in range(SCAN_GROUPS)]
        zero = tuple(jnp.zeros((8, 128), F32) for _ in range(2 * SCAN_GROUPS))
        carry = _scan_tiles(cre_ref, cim_ref, None, None, ncx // 8, zero, lams)
        _scan_tiles(sre_ref, sim_ref, hre_ref, him_ref, nl // 8, carry, lams)
        _unreverse_tiles(hre_ref, nl // 8)
        _unreverse_tiles(him_ref, nl // 8)
        for gi in range(SCAN_GROUPS):
            bg = g0 + gi
            y = jnp.dot(u_ref[gi], mix_ref[bg], preferred_element_type=F32)
            hs = []
            for b in range(B):
                hs.append(jnp.concatenate([hre_ref[gi, pl.ds(b, N_CHUNK, stride=B), :],
                                           him_ref[gi, pl.ds(b, N_CHUNK, stride=B), :]], axis=1))
            hcat = jnp.concatenate(hs, axis=0).astype(BF16)
            y = y + jnp.dot(hcat, wout_ref[bg], preferred_element_type=F32)
            y_ref[bg] = y.astype(BF16)

    blk = blk_l.astype(F32).astype(BF16)
    for s in range(S5_T):
        h, r = s // BLK, s % BLK
        acc = None
        for j in range(BLK):
            piece = y_ref[(j - r) % BLK, :, 128 * h:128 * (h + 1)]
            acc = piece if acc is None else jnp.where(blk == j, piece, acc)
        o_ref[s] = acc


def _s5core(ul, uc, win, wout, mix, l16):
    TH = S5_T * S5_H
    nl = N_CHUNK * B
    ncx = N_CCHUNK * B
    g4 = lambda r, c: pl.BlockSpec((BLK, r, c), lambda q: (q, 0, 0))
    col = lambda n: pl.BlockSpec((S5_T, n, 128), lambda q: (0, 0, q))
    f32s = lambda n: pltpu.VMEM((SCAN_GROUPS, n, 128), F32)
    return pl.pallas_call(
        _s5core_kernel,
        out_shape=jax.ShapeDtypeStruct((S5_T, nl, S5_W), BF16),
        grid=(S5_G // BLK,),
        in_specs=[col(nl), col(ncx), g4(TH, TH), g4(TH, TH), g4(TH, TH), g4(8, 128)],
        out_specs=col(nl),
        scratch_shapes=[pltpu.VMEM((SCAN_GROUPS, nl, TH), BF16),
                        f32s(nl), f32s(nl), f32s(ncx), f32s(ncx), f32s(nl), f32s(nl),
                        pltpu.VMEM((BLK, nl, TH), BF16)],
        compiler_params=_params("arbitrary"),
        name="s5core",
    )(ul, uc, win, wout, mix, l16)


def _s5tail_kernel(slat_ref, sza_ref, gluw_ref, glub_ref, wtop_ref, y_ref):
    unrot = (BLK - pl.program_id(0)) & (BLK - 1)
    for b in range(B):
        rows = slice(N_CHUNK * b, N_CHUNK * (b + 1))
        g = _gelu(_rot_blocks(slat_ref[rows, :].astype(F32), unrot))
        gate = _sigmoid(jnp.dot(g.astype(BF16), gluw_ref[...], preferred_element_type=F32) + glub_ref[...])
        a = (g * gate * sza_ref[rows, :].astype(F32)).astype(BF16)
        y_ref[rows, :] = jnp.dot(a, wtop_ref[...], preferred_element_type=F32).astype(BF16)


def _s5tail(slat, sza, glu_w, glu_b, w_top):
    slab = lambda r, h: r + BLK * h
    sspec = lambda w: pl.BlockSpec((None, N_CHUNK * B, w), lambda r, h: (slab(r, h), 0, 0))
    full = lambda *s: pl.BlockSpec(s, lambda r, h: (0,) * len(s))
    return pl.pallas_call(
        _s5tail_kernel,
        out_shape=jax.ShapeDtypeStruct((S5_T, N_CHUNK * B, D), BF16),
        grid=(BLK, S5_T // BLK),
        in_specs=[sspec(512), sspec(512), full(512, 512), full(1, 512), full(512, D)],
        out_specs=sspec(D),
        compiler_params=_params("arbitrary", "arbitrary"),
        name="s5tail",
    )(slat, sza, glu_w, glu_b, w_top)


PERM_ROWS = S5_T * S5_T


def _tail0_kernel(x_ref, ys5_ref, guz_ref, vln_ref, mod_ref, sguw_ref, sgub_ref, wbot_ref, ng_ref, nb_ref, o_ref):
    tm = x_ref.shape[0]
    lane = lax.broadcasted_iota(jnp.int32, (SGU_CHUNK, 128), 1)
    lo = lane < SGU_HD
    zero = jnp.zeros((SGU_CHUNK, 128), BF16)
    chunks = []
    for ci in range(tm // SGU_CHUNK):
        v = vln_ref[ci * SGU_CHUNK:(ci + 1) * SGU_CHUNK, :]
        cols = []
        for pi in range(SGU_HEADS // 2):
            vp = v[:, 128 * pi:128 * (pi + 1)]
            bm = jnp.concatenate([jnp.where(lo, vp, zero), jnp.where(lo, zero, vp)], axis=0)
            cols.append(jnp.dot(sguw_ref[pi], bm, preferred_element_type=F32))
        chunks.append(jnp.concatenate(cols, axis=1) + sgub_ref[...])
    s = jnp.concatenate(chunks, axis=0)
    bsg = (guz_ref[...].astype(F32) * s).astype(BF16)
    ri = lax.broadcasted_iota(jnp.int32, (PERM_ROWS, PERM_ROWS), 0)
    ci = lax.broadcasted_iota(jnp.int32, (PERM_ROWS, PERM_ROWS), 1)
    perm = jnp.where(((ri >> H_SHIFT) == (ci & (S5_T - 1))) & ((ri & (S5_T - 1)) == (ci >> H_SHIFT)), 1.0, 0.0)
    perm = perm.astype(BF16)
    ys5 = jnp.concatenate(
        [jnp.dot(perm, ys5_ref[:, S5_T * j:S5_T * (j + 1), :].reshape(PERM_ROWS, D), preferred_element_type=F32)
         for j in range(tm // PERM_ROWS)], axis=0)
    y = ys5 + jnp.dot(bsg, wbot_ref[...], preferred_element_type=F32)
    gmod = mod_ref[:, 2 * D:3 * D]
    o_ref[...] = _layer_norm(DN_ALPHA * x_ref[...] + gmod * y, ng_ref[...], nb_ref[...])


def _tail0(x, ys5, guz, vln, mod, sguw, sgub, w_bot, ng, nb, tm=512):
    nct = tm // S5_T
    t512 = pl.BlockSpec((None, tm, 512), lambda b, i: (b, i, 0))
    tD = pl.BlockSpec((None, tm, D), lambda b, i: (b, i, 0))
    full = lambda *s: pl.BlockSpec(s, lambda b, i: (0,) * len(s))
    return pl.pallas_call(
        _tail0_kernel,
        out_shape=jax.ShapeDtypeStruct((B, L, D), F32),
        grid=(B, L // tm),
        in_specs=[tD, pl.BlockSpec((S5_T, nct, D), lambda b, i: (0, b * (N_CHUNK // nct) + i, 0)), t512, t512,
                  pl.BlockSpec((None, 1, 3 * D), lambda b, i: (b, 0, 0)),
                  full(SGU_HEADS // 2, SGU_CHUNK, 256), full(SGU_CHUNK, 512),
                  full(512, D), full(1, D), full(1, D)],
        out_specs=tD,
        compiler_params=_params("arbitrary", "arbitrary"),
        name="tail0",
    )(x, ys5, guz, vln, mod, sguw, sgub, w_bot, ng, nb)


def _inproj1_kernel(x_ref, mod_ref, w_ref, hg_ref, sz_ref):
    shift = mod_ref[:, 0:D]
    scale = mod_ref[:, D:2 * D]
    h = (x_ref[...] * (1.0 + scale) + shift).astype(BF16)
    dot = lambda lo: jnp.dot(h, w_ref[:, lo:lo + D], preferred_element_type=F32)
    hg_ref[...] = (dot(0) * _sigmoid(dot(D))).astype(BF16)
    sz_ref[...] = _silu(dot(2 * D)).astype(BF16)


def _inproj1(x, mod, w_in, tm=512):
    o = jax.ShapeDtypeStruct((B, L, D), BF16)
    ospec = pl.BlockSpec((None, tm, D), lambda b, i: (b, i, 0))
    return pl.pallas_call(
        _inproj1_kernel,
        out_shape=(o, o),
        grid=(B, L // tm),
        in_specs=[pl.BlockSpec((None, tm, D), lambda b, i: (b, i, 0)),
                  pl.BlockSpec((None, 1, 3 * D), lambda b, i: (b, 0, 0)),
                  pl.BlockSpec((D, ODD_IN), lambda b, i: (0, 0))],
        out_specs=(ospec, ospec),
        compiler_params=_params("arbitrary", "arbitrary"),
        name="inproj1",
    )(x, mod, w_in)


ROW_PAD = 16
COL_PAD = CONV_HALF * GRID_W
CONV_ROWS = 128


def _conv_kernel(h_ref, w_ref, b_ref, o_ref, prow_ref, pcol_ref):
    j = pl.program_id(1)
    bias = b_ref[...]

    @pl.when(j < (D // 2) // 128)
    def _():
        zpad = jnp.zeros((GRID_W, ROW_PAD, 128), F32)
        prow_ref[:, 0:ROW_PAD, :] = zpad
        prow_ref[:, ROW_PAD + GRID_W:, :] = zpad
        prow_ref[:, ROW_PAD:ROW_PAD + GRID_W, :] = h_ref[...].astype(F32).reshape(GRID_W, GRID_W, 128)

        def body(r, carry):
            acc = jnp.zeros((GRID_W, 128), F32) + bias
            for k in range(CONV_K):
                off = ROW_PAD - CONV_HALF + k
                acc = acc + w_ref[k:k + 1, :] * prow_ref[r, off:off + GRID_W, :]
            o_ref[pl.ds(pl.multiple_of(r * GRID_W, GRID_W), GRID_W), :] = acc.astype(BF16)
            return carry

        lax.fori_loop(0, GRID_W, body, 0)

    @pl.when(j >= (D // 2) // 128)
    def _():
        zpad = jnp.zeros((COL_PAD, 128), F32)
        pcol_ref[0:COL_PAD, :] = zpad
        pcol_ref[COL_PAD + L:, :] = zpad
        pcol_ref[COL_PAD:COL_PAD + L, :] = h_ref[...].astype(F32)

        def body(i, carry):
            base = pl.multiple_of(i * CONV_ROWS, CONV_ROWS)
            acc = jnp.zeros((CONV_ROWS, 128), F32) + bias
            for k in range(CONV_K):
                acc = acc + w_ref[k:k + 1, :] * pcol_ref[pl.ds(base + k * GRID_W, CONV_ROWS), :]
            o_ref[pl.ds(base, CONV_ROWS), :] = acc.astype(BF16)
            return carry

        lax.fori_loop(0, L // CONV_ROWS, body, 0)


def _conv(hg, dw_w, dw_b):
    return pl.pallas_call(
        _conv_kernel,
        out_shape=jax.ShapeDtypeStruct((B, L, D), BF16),
        grid=(B, D // 128),
        in_specs=[pl.BlockSpec((None, L, 128), lambda b, j: (b, 0, j)),
                  pl.BlockSpec((CONV_K, 128), lambda b, j: (0, j)),
                  pl.BlockSpec((1, 128), lambda b, j: (0, j))],
        out_specs=pl.BlockSpec((None, L, 128), lambda b, j: (b, 0, j)),
        scratch_shapes=[pltpu.VMEM((GRID_W, GRID_W + 2 * ROW_PAD, 128), F32),
                        pltpu.VMEM((L + 2 * COL_PAD, 128), F32)],
        compiler_params=_params("arbitrary", "arbitrary"),
        name="dwconv",
    )(hg, dw_w, dw_b)


def _tail1_kernel(x_ref, hc_ref, sz_ref, mod_ref, lng_ref, lnb_ref, wout_ref, ng_ref, nb_ref, o_ref):
    m = _silu(_layer_norm(hc_ref[...].astype(F32), lng_ref[...], lnb_ref[...])) * sz_ref[...].astype(F32)
    y = jnp.dot(m.astype(BF16), wout_ref[...], preferred_element_type=F32)
    gmod = mod_ref[:, 2 * D:3 * D]
    o_ref[...] = _layer_norm(DN_ALPHA * x_ref[...] + gmod * y, ng_ref[...], nb_ref[...])


def _tail1(x, hc, sz, mod, ln_g, ln_b, w_out, ng, nb, tm=512):
    tD = pl.BlockSpec((None, tm, D), lambda b, i: (b, i, 0))
    full = lambda *s: pl.BlockSpec(s, lambda b, i: (0,) * len(s))
    return pl.pallas_call(
        _tail1_kernel,
        out_shape=jax.ShapeDtypeStruct((B, L, D), F32),
        grid=(B, L // tm),
        in_specs=[tD, tD, tD, pl.BlockSpec((None, 1, 3 * D), lambda b, i: (b, 0, 0)),
                  full(1, D), full(1, D), full(D, D), full(1, D), full(1, D)],
        out_specs=tD,
        compiler_params=_params("arbitrary", "arbitrary"),
        name="tail1",
    )(x, hc, sz, mod, ln_g, ln_b, w_out, ng, nb)


def kernel(x, c, ctx, c_ctx, mod_w, mod_b, norm_g, norm_b, ev_w_in, ev_w_out, s5_lam_re, s5_lam_im, s5_log_dt, s5_b_re, s5_b_im, s5_c_re, s5_c_im, s5_d, glu_w, glu_b, sgu_ln_g, sgu_ln_b, sgu_w, sgu_b, od_w_in, od_w_out, dw_w, dw_b, conv_ln_g, conv_ln_b):
    TH = S5_T * S5_H
    row = lambda v: v.reshape(1, -1)

    cond8 = jnp.concatenate([c, c_ctx[None], jnp.zeros((3, D), F32)], axis=0)
    mods = _adaln(cond8, mod_w, mod_b)
    mod0 = mods[0, :B].reshape(B, 1, 3 * D)
    mod0c = mods[0, B:B + 1]
    mod1 = mods[1, :B].reshape(B, 1, 3 * D)

    lbr, lbi, cfr, cfi = _discretise(s5_lam_re[0], s5_lam_im[0], s5_log_dt[0])
    rowcat = lambda a: jnp.concatenate([a[0], a[1]], axis=-1).reshape(S5_G, 1, 2 * S5_P)
    colcat = lambda a: jnp.broadcast_to(jnp.concatenate([a[0], a[1]], axis=-1)[:, :, None], (S5_G, 2 * S5_P, TH))
    bt = lambda a: jnp.concatenate([jnp.swapaxes(a[0], 1, 2), jnp.swapaxes(a[1], 1, 2)], axis=-1)
    ct = lambda a: jnp.tile(jnp.concatenate([jnp.swapaxes(a[0], 1, 2), jnp.swapaxes(a[1], 1, 2)], axis=1), (1, 1, S5_T))
    d_row = jnp.tile(s5_d[0].reshape(S5_G, 1, S5_H), (1, 1, S5_T))
    win, wout, mix, l16 = _s5_weights(rowcat(lbr), rowcat(lbi), rowcat(cfr), rowcat(cfi),
                                      bt(s5_b_re[0]), bt(s5_b_im[0]), colcat(lbr), colcat(lbi),
                                      ct(s5_c_re[0]), ct(s5_c_im[0]), d_row)

    w_in0 = ev_w_in[0].astype(BF16)
    w_out0 = ev_w_out[0].astype(BF16)
    glu_w0 = glu_w[0].astype(BF16)
    guz, vln, hs = _inproj0n(x, mod0, w_in0[:, 2 * S5_W:], row(sgu_ln_g[0]), row(sgu_ln_b[0]))
    ua, sza, ua_c = _inproj0a(hs, _ctx_slabs(ctx, mod0c), w_in0[:, :2 * S5_W])
    s_lat = _s5core(ua, ua_c, win, wout, mix, l16)
    y_s5 = _s5tail(s_lat, sza, glu_w0, row(glu_b[0]), w_out0[:S5_W])
    sguw = sgu_w[0].reshape(SGU_HEADS // 2, 2, SGU_CHUNK, SGU_CHUNK)
    sguw = jnp.transpose(sguw, (0, 2, 1, 3)).reshape(SGU_HEADS // 2, SGU_CHUNK, 2 * SGU_CHUNK).astype(BF16)
    sgub = jnp.repeat(sgu_b[0].T, SGU_HD, axis=1)
    x1 = _tail0(x, y_s5, guz, vln, mod0, sguw, sgub, w_out0[S5_W:], row(norm_g[0]), row(norm_b[0]))

    hg, sz = _inproj1(x1, mod1, od_w_in[0].astype(BF16))
    hc = _conv(hg, dw_w[0], row(dw_b[0]))
    return _tail1(x1, hc, sz, mod1, row(conv_ln_g[0]), row(conv_ln_b[0]), od_w_out[0].astype(BF16),
                  row(norm_g[1]), row(norm_b[1]))
```

```python
import functools
import math

import jax
import jax.numpy as jnp
from jax import lax
from jax.experimental import pallas as pl
from jax.experimental.pallas import tpu as pltpu

D = 1024
B = 4
L = 4096
CTX = 256
GRID_W = 64
S5_W = 512
S5_G = 32
S5_H = 16
H_SHIFT = 4
BLK = 128 // S5_H
S5_P = 64
S5_T = 16
SGU_W = 512
SGU_HEADS = 8
SGU_HD = 64
SGU_CHUNK = 128
CONV_K = 31
CONV_HALF = CONV_K // 2
EVEN_IN = 2560
ODD_IN = 3072
DEPTH = 2
DN_ALPHA = (2 * DEPTH) ** 0.25
LN_EPS = 1e-5
N_CHUNK = L // S5_T
N_CCHUNK = CTX // S5_T
VMEM_LIMIT_V7X = 56 * 1024 * 1024

F32 = jnp.float32
BF16 = jnp.bfloat16


GELU_C = math.sqrt(2.0 / math.pi)


def _gelu(x):
    hx = 0.5 * x
    return hx * jnp.tanh(x * ((x * x) * (0.044715 * GELU_C) + GELU_C)) + hx


def _sigmoid(x):
    return 0.5 * jnp.tanh(0.5 * x) + 0.5


def _silu(x):
    hx = 0.5 * x
    return hx * jnp.tanh(hx) + hx


def _layer_norm(x, g, b):
    mu = jnp.mean(x, axis=-1, keepdims=True)
    xc = x - mu
    var = jnp.mean(xc * xc, axis=-1, keepdims=True)
    return xc * lax.rsqrt(var + LN_EPS) * g + b


def _params(*sem):
    return pltpu.CompilerParams(dimension_semantics=sem, vmem_limit_bytes=VMEM_LIMIT_V7X)


def _adaln_kernel(c_ref, w_ref, b_ref, o_ref):
    def split(v):
        hi = v.astype(BF16)
        return hi, (v - hi.astype(F32)).astype(BF16)

    a_hi, a_lo = split(_silu(c_ref[...]))
    w_hi, w_lo = split(w_ref[...])
    dot = functools.partial(jnp.dot, preferred_element_type=F32)
    o_ref[...] = dot(a_hi, w_hi) + dot(a_lo, w_hi) + dot(a_hi, w_lo) + b_ref[...]


def _adaln(cond8, mod_w, mod_b):
    tn = 512
    return pl.pallas_call(
        _adaln_kernel,
        out_shape=jax.ShapeDtypeStruct((DEPTH, 8, 3 * D), F32),
        grid=(DEPTH, 3 * D // tn),
        in_specs=[pl.BlockSpec((8, D), lambda l, j: (0, 0)),
                  pl.BlockSpec((None, D, tn), lambda l, j: (l, 0, j)),
                  pl.BlockSpec((None, 1, tn), lambda l, j: (l, 0, j))],
        out_specs=pl.BlockSpec((None, 8, tn), lambda l, j: (l, 0, j)),
        compiler_params=_params("arbitrary", "arbitrary"),
        name="adaln",
    )(cond8, mod_w, mod_b.reshape(DEPTH, 1, 3 * D))


def _disc_kernel(lr_ref, li_ref, ldt_ref, obr_ref, obi_ref, ocr_ref, oci_ref):
    lr = lr_ref[...]
    li = li_ref[...]
    dt = jnp.exp(ldt_ref[...])
    mag = jnp.exp(lr * dt)
    br = mag * jnp.cos(li * dt)
    bi = mag * jnp.sin(li * dt)
    inv = 1.0 / (lr * lr + li * li)
    nr = br - 1.0
    obr_ref[...] = br
    obi_ref[...] = bi
    ocr_ref[...] = (nr * lr + bi * li) * inv
    oci_ref[...] = (bi * lr - nr * li) * inv


def _discretise(lam_re, lam_im, log_dt):
    shp = jax.ShapeDtypeStruct((2 * S5_G, S5_P), F32)
    ldt = jnp.broadcast_to(log_dt.reshape(2 * S5_G, 1), (2 * S5_G, S5_P))
    outs = pl.pallas_call(
        _disc_kernel, out_shape=(shp, shp, shp, shp), name="s5_discretise",
    )(lam_re.reshape(2 * S5_G, S5_P), lam_im.reshape(2 * S5_G, S5_P), ldt)
    return [o.reshape(2, S5_G, S5_P) for o in outs]


def _cpow(base_pows, j):
    re = None
    im = None
    for k, (pr, pi) in enumerate(base_pows):
        bit = ((j >> k) & 1) == 1
        mr = jnp.where(bit, pr, 1.0)
        mi = jnp.where(bit, pi, 0.0)
        if re is None:
            re, im = mr, mi
        else:
            re, im = re * mr - im * mi, re * mi + im * mr
    return re, im


def _squarings(pr, pi, n):
    out = [(pr, pi)]
    for _ in range(n - 1):
        pr, pi = pr * pr - pi * pi, 2.0 * pr * pi
        out.append((pr, pi))
    return out


def _shift_lanes(x, n):
    lane = lax.broadcasted_iota(jnp.int32, (S5_H, 128), 1)
    lo, hi = x[:, :128], x[:, 128:]
    if n == 0:
        return x
    if n < 128:
        rlo = pltpu.roll(lo, n, axis=1)
        rhi = pltpu.roll(hi, n, axis=1)
        return jnp.concatenate([jnp.where(lane >= n, rlo, 0.0), jnp.where(lane >= n, rhi, rlo)], axis=1)
    m = n - 128
    rlo = lo if m == 0 else pltpu.roll(lo, m, axis=1)
    return jnp.concatenate([jnp.zeros_like(lo), jnp.where(lane >= m, rlo, 0.0)], axis=1)


def _unshift_lanes(x, n):
    lane = lax.broadcasted_iota(jnp.int32, (S5_H, 128), 1)
    lo, hi = x[:, :128], x[:, 128:]
    if n == 0:
        return x
    if n < 128:
        rlo = pltpu.roll(lo, 128 - n, axis=1)
        rhi = pltpu.roll(hi, 128 - n, axis=1)
        keep = lane < 128 - n
        return jnp.concatenate([jnp.where(keep, rlo, rhi), jnp.where(keep, rhi, 0.0)], axis=1)
    m = n - 128
    rhi = hi if m == 0 else pltpu.roll(hi, 128 - m, axis=1)
    return jnp.concatenate([jnp.where(lane < 128 - m, rhi, 0.0), jnp.zeros_like(lo)], axis=1)


def _s5w_kernel(lrow_re, lrow_im, crow_re, crow_im, bt_re, bt_im,
                lcol_re, lcol_im, ct_re, ct_im, d_ref,
                win_ref, wout_ref, mix_ref, l16_ref):
    TH = S5_T * S5_H
    bg = pl.program_id(0) & (BLK - 1)

    def chunk_pos(idx):
        return (((idx >> H_SHIFT) - bg) & (BLK - 1)) + ((idx >> 7) << 3)

    lr = lrow_re[...]
    li = lrow_im[...]
    pows_row = _squarings(lr, li, 5)
    l16_ref[0:1, :] = pows_row[4][0]
    l16_ref[1:2, :] = pows_row[4][1]
    l16_ref[2:8, :] = jnp.zeros((6, 128), F32)
    cr = crow_re[...]
    ci = crow_im[...]
    btr = bt_re[...]
    bti = bt_im[...]
    bbr = cr * btr - ci * bti
    bbi = cr * bti + ci * btr
    s_idx = chunk_pos(lax.broadcasted_iota(jnp.int32, (TH, 128), 0))
    lane = lax.broadcasted_iota(jnp.int32, (TH, 128), 1)
    jw = jnp.where(lane < S5_P, S5_T - 1 - s_idx, s_idx)
    pr, pi = _cpow(pows_row[:4], jw)
    tbr = jnp.broadcast_to(bbr[None], (S5_T, S5_H, 128)).reshape(TH, 128)
    tbi = jnp.broadcast_to(bbi[None], (S5_T, S5_H, 128)).reshape(TH, 128)
    win_ref[:, 0:128] = (pr * tbr - pi * tbi).astype(BF16)
    win_ref[:, 128:256] = (pr * tbi + pi * tbr).astype(BF16)

    cpows = _squarings(lcol_re[...], lcol_im[...], 5)
    row = lax.broadcasted_iota(jnp.int32, (2 * S5_P, TH), 0)
    lane_w = lax.broadcasted_iota(jnp.int32, (2 * S5_P, TH), 1)
    t_idx = chunk_pos(lane_w)
    j_idx = lane_w >> H_SHIFT
    is_f = row < S5_P
    ctr = ct_re[...]
    cti = ct_im[...]
    er, ei = _cpow(cpows[:4], jnp.where(is_f, t_idx, S5_T - 1 - t_idx))
    er, ei = er * cpows[0][0] - ei * cpows[0][1], er * cpows[0][1] + ei * cpows[0][0]
    wr = ctr * er - cti * ei
    wi = ctr * ei + cti * er
    wout_ref[0:128, :] = wr.astype(BF16)
    wout_ref[128:256, :] = (-wi).astype(BF16)
    kr, ki = _cpow(cpows[:4], jnp.where(is_f, j_idx, S5_T - 1 - j_idx))
    ekr = ctr * kr - cti * ki
    eki = ctr * ki + cti * kr
    lane16 = lax.broadcasted_iota(jnp.int32, (S5_H, 128), 1)
    mf = lane16 < S5_P
    hp = lax.Precision.HIGHEST
    dot = functools.partial(jnp.dot, preferred_element_type=F32, precision=hp)
    kkf = dot(jnp.where(mf, bbr, 0.0), ekr) - dot(jnp.where(mf, bbi, 0.0), eki)
    kkb = dot(jnp.where(mf, 0.0, bbr), ekr) - dot(jnp.where(mf, 0.0, bbi), eki)
    dl = d_ref[...]
    r16 = lax.broadcasted_iota(jnp.int32, (S5_H, TH), 0)
    l256 = lax.broadcasted_iota(jnp.int32, (S5_H, TH), 1)
    rot = bg * S5_H
    for s in range(S5_T):
        blk = _shift_lanes(kkf, S5_H * s) + _unshift_lanes(kkb, S5_H * (S5_T - 1 - s))
        blk = blk + jnp.where(l256 == r16 + S5_H * s, dl, 0.0)
        blk = jnp.concatenate([pltpu.roll(blk[:, :128], rot, axis=1), pltpu.roll(blk[:, 128:], rot, axis=1)], axis=1)
        rho = ((s + bg) & (BLK - 1)) + (s & BLK)
        mix_ref[pl.ds(pl.multiple_of(rho * S5_H, S5_H), S5_H), :] = blk.astype(BF16)


def _s5_weights(lrow_re, lrow_im, crow_re, crow_im, bt_re, bt_im, lcol_re, lcol_im, ct_re, ct_im, d_row):
    TH = S5_T * S5_H
    g3 = lambda r, c: pl.BlockSpec((None, r, c), lambda g: (g, 0, 0))
    wshape = jax.ShapeDtypeStruct((S5_G, TH, TH), BF16)
    return pl.pallas_call(
        _s5w_kernel,
        out_shape=(wshape, wshape, wshape, jax.ShapeDtypeStruct((S5_G, 8, 128), F32)),
        grid=(S5_G,),
        in_specs=[g3(1, 128)] * 4 + [g3(S5_H, 128)] * 2 + [g3(128, TH)] * 4 + [g3(1, TH)],
        out_specs=(g3(TH, TH), g3(TH, TH), g3(TH, TH), g3(8, 128)),
        compiler_params=_params("arbitrary"),
        name="s5_weights",
    )(lrow_re, lrow_im, crow_re, crow_im, bt_re, bt_im, lcol_re, lcol_im, ct_re, ct_im, d_row)


def _rot_blocks(v, r):
    cols = [pltpu.roll(v[:, 128 * q:128 * (q + 1)], S5_H * r, axis=1) for q in range(v.shape[1] // 128)]
    return jnp.concatenate(cols, axis=1)


def _slabs_of(h, hs_ref):
    h3 = h.reshape(h.shape[0] // S5_T, S5_T, h.shape[1])
    for s in range(S5_T):
        hs_ref[s] = h3[:, s, :].astype(BF16)


def _inproj0n_kernel(x_ref, mod_ref, w_ref, lng_ref, lnb_ref, guz_ref, vln_ref, hs_ref):
    shift = mod_ref[:, 0:D]
    scale = mod_ref[:, D:2 * D]
    h = x_ref[...] * (1.0 + scale) + shift
    _slabs_of(h, hs_ref)
    hb = h.astype(BF16)
    dot = lambda lo: jnp.dot(hb, w_ref[:, lo:lo + 512], preferred_element_type=F32)
    guz_ref[...] = (_gelu(dot(0)) * _silu(dot(1024))).astype(BF16)
    vln_ref[...] = _layer_norm(_gelu(dot(512)), lng_ref[...], lnb_ref[...]).astype(BF16)


def _inproj0n(x, mod, w_nat, ln_g, ln_b, tm=512):
    nct = tm // S5_T
    o = jax.ShapeDtypeStruct((B, L, 512), BF16)
    ospec = pl.BlockSpec((None, tm, 512), lambda b, i: (b, i, 0))
    full = lambda *s: pl.BlockSpec(s, lambda b, i: (0,) * len(s))
    return pl.pallas_call(
        _inproj0n_kernel,
        out_shape=(o, o, jax.ShapeDtypeStruct((S5_T, B * N_CHUNK, D), BF16)),
        grid=(B, L // tm),
        in_specs=[pl.BlockSpec((None, tm, D), lambda b, i: (b, i, 0)),
                  pl.BlockSpec((None, 1, 3 * D), lambda b, i: (b, 0, 0)),
                  full(D, 1536), full(1, 512), full(1, 512)],
        out_specs=(ospec, ospec,
                   pl.BlockSpec((S5_T, nct, D), lambda b, i: (0, b * (N_CHUNK // nct) + i, 0))),
        compiler_params=_params("arbitrary", "arbitrary"),
        name="inproj0n",
    )(x, mod, w_nat, ln_g, ln_b)


def _ctx_slabs_kernel(x_ref, mod_ref, hs_ref):
    h = x_ref[...] * (1.0 + mod_ref[:, D:2 * D]) + mod_ref[:, 0:D]
    _slabs_of(h, hs_ref)


def _ctx_slabs(ctx, mod_c):
    return pl.pallas_call(
        _ctx_slabs_kernel,
        out_shape=jax.ShapeDtypeStruct((S5_T, B * N_CCHUNK, D), BF16),
        grid=(B,),
        in_specs=[pl.BlockSpec((None, CTX, D), lambda b: (b, 0, 0)),
                  pl.BlockSpec((1, 3 * D), lambda b: (0, 0))],
        out_specs=pl.BlockSpec((S5_T, N_CCHUNK, D), lambda b: (0, b, 0)),
        compiler_params=_params("arbitrary"),
        name="ctx_slabs",
    )(ctx, mod_c)


def _inproj0a_kernel(hs_ref, hc_ref, w_ref, ua_ref, sza_ref, uc_ref):
    r = pl.program_id(0)
    h = hs_ref[...]
    ua_ref[...] = _rot_blocks(jnp.dot(h, w_ref[:, 0:512], preferred_element_type=F32), r).astype(BF16)
    sza_ref[...] = _silu(jnp.dot(h, w_ref[:, 512:1024], preferred_element_type=F32)).astype(BF16)
    uc_ref[...] = _rot_blocks(jnp.dot(hc_ref[...], w_ref[:, 0:512], preferred_element_type=F32), r).astype(BF16)


def _inproj0a(hs, hcs, w_s5):
    slab = lambda r, h: r + BLK * h
    sspec = lambda n, w: pl.BlockSpec((None, n, w), lambda r, h: (slab(r, h), 0, 0))
    so = lambda n: jax.ShapeDtypeStruct((S5_T, n, 512), BF16)
    nl, ncx = B * N_CHUNK, B * N_CCHUNK
    return pl.pallas_call(
        _inproj0a_kernel,
        out_shape=(so(nl), so(nl), so(ncx)),
        grid=(BLK, S5_T // BLK),
        in_specs=[sspec(nl, D), sspec(ncx, D), pl.BlockSpec((D, 1024), lambda r, h: (0, 0))],
        out_specs=(sspec(nl, 512), sspec(nl, 512), sspec(ncx, 512)),
        compiler_params=_params("arbitrary", "arbitrary"),
        name="inproj0a",
    )(hs, hcs, w_s5)


SCAN_GROUPS = 4


def _scan_tiles(sre_ref, sim_ref, hre_ref, him_ref, n_tiles, carry, lams):
    row = lax.broadcasted_iota(jnp.int32, (8, 128), 0)
    lane = lax.broadcasted_iota(jnp.int32, (8, 128), 1)
    first = row < B
    fwd = lane < S5_P

    def body(k, c):
        of = pl.multiple_of(k * 8, 8)
        ob = pl.multiple_of((n_tiles - 1 - k) * 8, 8)
        out = []
        for gi in range(SCAN_GROUPS):
            lre, lim = lams[gi]
            hr, hi = c[2 * gi], c[2 * gi + 1]
            sr = jnp.where(fwd, sre_ref[gi, pl.ds(of, 8), :], pltpu.roll(sre_ref[gi, pl.ds(ob, 8), :], B, axis=0))
            si = jnp.where(fwd, sim_ref[gi, pl.ds(of, 8), :], pltpu.roll(sim_ref[gi, pl.ds(ob, 8), :], B, axis=0))
            h1r = lre * hr - lim * hi + sr
            h1i = lre * hi + lim * hr + si
            r1r = pltpu.roll(h1r, B, axis=0)
            r1i = pltpu.roll(h1i, B, axis=0)
            if hre_ref is not None:
                hre_ref[gi, pl.ds(of, 8), :] = jnp.where(first, hr, r1r)
                him_ref[gi, pl.ds(of, 8), :] = jnp.where(first, hi, r1i)
            h2r = lre * r1r - lim * r1i + sr
            h2i = lre * r1i + lim * r1r + si
            out.append(jnp.where(first, pltpu.roll(h2r, B, axis=0), h2r))
            out.append(jnp.where(first, pltpu.roll(h2i, B, axis=0), h2i))
        return tuple(out)

    return lax.fori_loop(0, n_tiles, body, carry)


def _unreverse_tiles(h_ref, n_tiles):
    lane = lax.broadcasted_iota(jnp.int32, (8, 128), 1)
    fwd = lane < S5_P

    def body(k, carry):
        of = pl.multiple_of(k * 8, 8)
        ob = pl.multiple_of((n_tiles - 1 - k) * 8, 8)
        for gi in range(SCAN_GROUPS):
            a = h_ref[gi, pl.ds(of, 8), :]
            b = h_ref[gi, pl.ds(ob, 8), :]
            h_ref[gi, pl.ds(of, 8), :] = jnp.where(fwd, a, pltpu.roll(b, B, axis=0))
            h_ref[gi, pl.ds(ob, 8), :] = jnp.where(fwd, b, pltpu.roll(a, B, axis=0))
        return carry

    lax.fori_loop(0, n_tiles // 2, body, 0, unroll=4)


def _gather_group(slab_ref, src):
    halves = []
    for h in range(S5_T // BLK):
        acc = slab_ref[BLK * h]
        for s in range(1, BLK):
            acc = jnp.where(src == s, slab_ref[BLK * h + s], acc)
        halves.append(acc)
    return jnp.concatenate(halves, axis=1)


def _s5core_kernel(ul_ref, uc_ref, win_ref, wout_ref, mix_ref, l16_ref, o_ref,
                   u_ref, sre_ref, sim_ref, cre_ref, cim_ref, hre_ref, him_ref, y_ref):
    nl = N_CHUNK * B
    ncx = N_CCHUNK * B
    blk_l = lax.broadcasted_iota(jnp.int32, (nl, 128), 1) >> H_SHIFT
    blk_c = lax.broadcasted_iota(jnp.int32, (ncx, 128), 1) >> H_SHIFT
    for g0 in range(0, BLK, SCAN_GROUPS):
        for gi in range(SCAN_GROUPS):
            bg = g0 + gi
            win = win_ref[bg]
            src_l = ((blk_l - bg) & (BLK - 1)).astype(F32).astype(BF16)
            src_c = ((blk_c - bg) & (BLK - 1)).astype(F32).astype(BF16)
            u = _gather_group(ul_ref, src_l)
            u_ref[gi] = u
            sl = jnp.dot(u, win, preferred_element_type=F32)
            sc = jnp.dot(_gather_group(uc_ref, src_c), win, preferred_element_type=F32)
            for b in range(B):
                sre_ref[gi, pl.ds(b, N_CHUNK, stride=B), :] = sl[N_CHUNK * b:N_CHUNK * (b + 1), 0:128]
                sim_ref[gi, pl.ds(b, N_CHUNK, stride=B), :] = sl[N_CHUNK * b:N_CHUNK * (b + 1), 128:256]
                cre_ref[gi, pl.ds(b, N_CCHUNK, stride=B), :] = sc[N_CCHUNK * b:N_CCHUNK * (b + 1), 0:128]
                cim_ref[gi, pl.ds(b, N_CCHUNK, stride=B), :] = sc[N_CCHUNK * b:N_CCHUNK * (b + 1), 128:256]
        lams = [(jnp.broadcast_to(l16_ref[g0 + gi, 0:1, :], (8, 128)),
                 jnp.broadcast_to(l16_ref[g0 + gi, 1:2, :], (8, 128))) for gi in range(SCAN_GROUPS)]
        zero = tuple(jnp.zeros((8, 128), F32) for _ in range(2 * SCAN_GROUPS))
        carry = _scan_tiles(cre_ref, cim_ref, None, None, ncx // 8, zero, lams)
        _scan_tiles(sre_ref, sim_ref, hre_ref, him_ref, nl // 8, carry, lams)
        _unreverse_tiles(hre_ref, nl // 8)
        _unreverse_tiles(him_ref, nl // 8)
        for gi in range(SCAN_GROUPS):
            bg = g0 + gi
            y = jnp.dot(u_ref[gi], mix_ref[bg], preferred_element_type=F32)
            hs = []
            for b in range(B):
                hs.append(jnp.concatenate([hre_ref[gi, pl.ds(b, N_CHUNK, stride=B), :],
                                           him_ref[gi, pl.ds(b, N_CHUNK, stride=B), :]], axis=1))
            hcat = jnp.concatenate(hs, axis=0).astype(BF16)
            y = y + jnp.dot(hcat, wout_ref[bg], preferred_element_type=F32)
            y_ref[bg] = y.astype(BF16)

    blk = blk_l.astype(F32).astype(BF16)
    for s in range(S5_T):
        h, r = s // BLK, s % BLK
        acc = None
        for j in range(BLK):
            piece = y_ref[(j - r) % BLK, :, 128 * h:128 * (h + 1)]
            acc = piece if acc is None else jnp.where(blk == j, piece, acc)
        o_ref[s] = acc


def _s5core(ul, uc, win, wout, mix, l16):
    TH = S5_T * S5_H
    nl = N_CHUNK * B
    ncx = N_CCHUNK * B
    g4 = lambda r, c: pl.BlockSpec((BLK, r, c), lambda q: (q, 0, 0))
    col = lambda n: pl.BlockSpec((S5_T, n, 128), lambda q: (0, 0, q))
    f32s = lambda n: pltpu.VMEM((SCAN_GROUPS, n, 128), F32)
    return pl.pallas_call(
        _s5core_kernel,
        out_shape=jax.ShapeDtypeStruct((S5_T, nl, S5_W), BF16),
        grid=(S5_G // BLK,),
        in_specs=[col(nl), col(ncx), g4(TH, TH), g4(TH, TH), g4(TH, TH), g4(8, 128)],
        out_specs=col(nl),
        scratch_shapes=[pltpu.VMEM((SCAN_GROUPS, nl, TH), BF16),
                        f32s(nl), f32s(nl), f32s(ncx), f32s(ncx), f32s(nl), f32s(nl),
                        pltpu.VMEM((BLK, nl, TH), BF16)],
        compiler_params=_params("arbitrary"),
        name="s5core",
    )(ul, uc, win, wout, mix, l16)


def _s5tail_kernel(slat_ref, sza_ref, gluw_ref, glub_ref, wtop_ref, y_ref):
    unrot = (BLK - pl.program_id(0)) & (BLK - 1)
    for b in range(B):
        rows = slice(N_CHUNK * b, N_CHUNK * (b + 1))
        g = _gelu(_rot_blocks(slat_ref[rows, :].astype(F32), unrot))
        gate = _sigmoid(jnp.dot(g.astype(BF16), gluw_ref[...], preferred_element_type=F32) + glub_ref[...])
        a = (g * gate * sza_ref[rows, :].astype(F32)).astype(BF16)
        y_ref[rows, :] = jnp.dot(a, wtop_ref[...], preferred_element_type=F32).astype(BF16)


def _s5tail(slat, sza, glu_w, glu_b, w_top):
    slab = lambda r, h: r + BLK * h
    sspec = lambda w: pl.BlockSpec((None, N_CHUNK * B, w), lambda r, h: (slab(r, h), 0, 0))
    full = lambda *s: pl.BlockSpec(s, lambda r, h: (0,) * len(s))
    return pl.pallas_call(
        _s5tail_kernel,
        out_shape=jax.ShapeDtypeStruct((S5_T, N_CHUNK * B, D), BF16),
        grid=(BLK, S5_T // BLK),
        in_specs=[sspec(512), sspec(512), full(512, 512), full(1, 512), full(512, D)],
        out_specs=sspec(D),
        compiler_params=_params("arbitrary", "arbitrary"),
        name="s5tail",
    )(slat, sza, glu_w, glu_b, w_top)


PERM_ROWS = S5_T * S5_T


def _tail0_kernel(x_ref, ys5_ref, guz_ref, vln_ref, mod_ref, sguw_ref, sgub_ref, wbot_ref, ng_ref, nb_ref, o_ref):
    tm = x_ref.shape[0]
    lane = lax.broadcasted_iota(jnp.int32, (SGU_CHUNK, 128), 1)
    lo = lane < SGU_HD
    zero = jnp.zeros((SGU_CHUNK, 128), BF16)
    chunks = []
    for ci in range(tm // SGU_CHUNK):
        v = vln_ref[ci * SGU_CHUNK:(ci + 1) * SGU_CHUNK, :]
        cols = []
        for pi in range(SGU_HEADS // 2):
            vp = v[:, 128 * pi:128 * (pi + 1)]
            bm = jnp.concatenate([jnp.where(lo, vp, zero), jnp.where(lo, zero, vp)], axis=0)
            cols.append(jnp.dot(sguw_ref[pi], bm, preferred_element_type=F32))
        chunks.append(jnp.concatenate(cols, axis=1) + sgub_ref[...])
    s = jnp.concatenate(chunks, axis=0)
    bsg = (guz_ref[...].astype(F32) * s).astype(BF16)
    ri = lax.broadcasted_iota(jnp.int32, (PERM_ROWS, PERM_ROWS), 0)
    ci = lax.broadcasted_iota(jnp.int32, (PERM_ROWS, PERM_ROWS), 1)
    perm = jnp.where(((ri >> H_SHIFT) == (ci & (S5_T - 1))) & ((ri & (S5_T - 1)) == (ci >> H_SHIFT)), 1.0, 0.0)
    perm = perm.astype(BF16)
    ys5 = jnp.concatenate(
        [jnp.dot(perm, ys5_ref[:, S5_T * j:S5_T * (j + 1), :].reshape(PERM_ROWS, D), preferred_element_type=F32)
         for j in range(tm // PERM_ROWS)], axis=0)
    y = ys5 + jnp.dot(bsg, wbot_ref[...], preferred_element_type=F32)
    gmod = mod_ref[:, 2 * D:3 * D]
    o_ref[...] = _layer_norm(DN_ALPHA * x_ref[...] + gmod * y, ng_ref[...], nb_ref[...])


def _tail0(x, ys5, guz, vln, mod, sguw, sgub, w_bot, ng, nb, tm=512):
    nct = tm // S5_T
    t512 = pl.BlockSpec((None, tm, 512), lambda b, i: (b, i, 0))
    tD = pl.BlockSpec((None, tm, D), lambda b, i: (b, i, 0))
    full = lambda *s: pl.BlockSpec(s, lambda b, i: (0,) * len(s))
    return pl.pallas_call(
        _tail0_kernel,
        out_shape=jax.ShapeDtypeStruct((B, L, D), F32),
        grid=(B, L // tm),
        in_specs=[tD, pl.BlockSpec((S5_T, nct, D), lambda b, i: (0, b * (N_CHUNK // nct) + i, 0)), t512, t512,
                  pl.BlockSpec((None, 1, 3 * D), lambda b, i: (b, 0, 0)),
                  full(SGU_HEADS // 2, SGU_CHUNK, 256), full(SGU_CHUNK, 512),
                  full(512, D), full(1, D), full(1, D)],
        out_specs=tD,
        compiler_params=_params("arbitrary", "arbitrary"),
        name="tail0",
    )(x, ys5, guz, vln, mod, sguw, sgub, w_bot, ng, nb)


def _inproj1_kernel(x_ref, mod_ref, w_ref, hg_ref, sz_ref):
    shift = mod_ref[:, 0:D]
    scale = mod_ref[:, D:2 * D]
    h = (x_ref[...] * (1.0 + scale) + shift).astype(BF16)
    dot = lambda lo: jnp.dot(h, w_ref[:, lo:lo + D], preferred_element_type=F32)
    hg_ref[...] = (dot(0) * _sigmoid(dot(D))).astype(BF16)
    sz_ref[...] = _silu(dot(2 * D)).astype(BF16)


def _inproj1(x, mod, w_in, tm=512):
    o = jax.ShapeDtypeStruct((B, L, D), BF16)
    ospec = pl.BlockSpec((None, tm, D), lambda b, i: (b, i, 0))
    return pl.pallas_call(
        _inproj1_kernel,
        out_shape=(o, o),
        grid=(B, L // tm),
        in_specs=[pl.BlockSpec((None, tm, D), lambda b, i: (b, i, 0)),
                  pl.BlockSpec((None, 1, 3 * D), lambda b, i: (b, 0, 0)),
                  pl.BlockSpec((D, ODD_IN), lambda b, i: (0, 0))],
        out_specs=(ospec, ospec),
        compiler_params=_params("arbitrary", "arbitrary"),
        name="inproj1",
    )(x, mod, w_in)


ROW_PAD = 16
COL_PAD = CONV_HALF * GRID_W
CONV_ROWS = 128


def _conv_kernel(h_ref, w_ref, b_ref, o_ref, prow_ref, pcol_ref):
    j = pl.program_id(1)
    bias = b_ref[...]

    @pl.when(j < (D // 2) // 128)
    def _():
        zpad = jnp.zeros((GRID_W, ROW_PAD, 128), F32)
        prow_ref[:, 0:ROW_PAD, :] = zpad
        prow_ref[:, ROW_PAD + GRID_W:, :] = zpad
        prow_ref[:, ROW_PAD:ROW_PAD + GRID_W, :] = h_ref[...].astype(F32).reshape(GRID_W, GRID_W, 128)

        def body(r, carry):
            acc = jnp.zeros((GRID_W, 128), F32) + bias
            for k in range(CONV_K):
                off = ROW_PAD - CONV_HALF + k
                acc = acc + w_ref[k:k + 1, :] * prow_ref[r, off:off + GRID_W, :]
            o_ref[pl.ds(pl.multiple_of(r * GRID_W, GRID_W), GRID_W), :] = acc.astype(BF16)
            return carry

        lax.fori_loop(0, GRID_W, body, 0)

    @pl.when(j >= (D // 2) // 128)
    def _():
        zpad = jnp.zeros((COL_PAD, 128), F32)
        pcol_ref[0:COL_PAD, :] = zpad
        pcol_ref[COL_PAD + L:, :] = zpad
        pcol_ref[COL_PAD:COL_PAD + L, :] = h_ref[...].astype(F32)

        def body(i, carry):
            base = pl.multiple_of(i * CONV_ROWS, CONV_ROWS)
            acc = jnp.zeros((CONV_ROWS, 128), F32) + bias
            for k in range(CONV_K):
                acc = acc + w_ref[k:k + 1, :] * pcol_ref[pl.ds(base + k * GRID_W, CONV_ROWS), :]
            o_ref[pl.ds(base, CONV_ROWS), :] = acc.astype(BF16)
            return carry

        lax.fori_loop(0, L // CONV_ROWS, body, 0)


def _conv(hg, dw_w, dw_b):
    return pl.pallas_call(
        _conv_kernel,
        out_shape=jax.ShapeDtypeStruct((B, L, D), BF16),
        grid=(B, D // 128),
        in_specs=[pl.BlockSpec((None, L, 128), lambda b, j: (b, 0, j)),
                  pl.BlockSpec((CONV_K, 128), lambda b, j: (0, j)),
                  pl.BlockSpec((1, 128), lambda b, j: (0, j))],
        out_specs=pl.BlockSpec((None, L, 128), lambda b, j: (b, 0, j)),
        scratch_shapes=[pltpu.VMEM((GRID_W, GRID_W + 2 * ROW_PAD, 128), F32),
                        pltpu.VMEM((L + 2 * COL_PAD, 128), F32)],
        compiler_params=_params("arbitrary", "arbitrary"),
        name="dwconv",
    )(hg, dw_w, dw_b)


def _tail1_kernel(x_ref, hc_ref, sz_ref, mod_ref, lng_ref, lnb_ref, wout_ref, ng_ref, nb_ref, o_ref):
    m = _silu(_layer_norm(hc_ref[...].astype(F32), lng_ref[...], lnb_ref[...])) * sz_ref[...].astype(F32)
    y = jnp.dot(m.astype(BF16), wout_ref[...], preferred_element_type=F32)
    gmod = mod_ref[:, 2 * D:3 * D]
    o_ref[...] = _layer_norm(DN_ALPHA * x_ref[...] + gmod * y, ng_ref[...], nb_ref[...])


def _tail1(x, hc, sz, mod, ln_g, ln_b, w_out, ng, nb, tm=512):
    tD = pl.BlockSpec((None, tm, D), lambda b, i: (b, i, 0))
    full = lambda *s: pl.BlockSpec(s, lambda b, i: (0,) * len(s))
    return pl.pallas_call(
        _tail1_kernel,
        out_shape=jax.ShapeDtypeStruct((B, L, D), F32),
        grid=(B, L // tm),
        in_specs=[tD, tD, tD, pl.BlockSpec((None, 1, 3 * D), lambda b, i: (b, 0, 0)),
                  full(1, D), full(1, D), full(D, D), full(1, D), full(1, D)],
        out_specs=tD,
        compiler_params=_params("arbitrary", "arbitrary"),
        name="tail1",
    )(x, hc, sz, mod, ln_g, ln_b, w_out, ng, nb)


L1_TM = 512
L1_ROWS = L1_TM // GRID_W
L1_HALO = L1_TM + 2 * COL_PAD
CONV_SUB = 64


def _layer1_kernel(x_ref, mod_ref, wag_ref, wz_ref, wout_ref, dww_ref, dwb_ref, lng_ref, lnb_ref, ng_ref, nb_ref,
                   o_ref, hg_ref, pcol_ref, prow_ref, hc_ref):
    ph = pl.program_id(1)
    i = pl.program_id(2)
    half = D // 2
    shift = mod_ref[:, 0:D]
    scale = mod_ref[:, D:2 * D]
    base = pl.multiple_of(i * L1_TM, L1_TM)

    @pl.when((ph == 0) & (i == 0))
    def _():
        zrow = jnp.zeros((COL_PAD, D), BF16)
        hg_ref[0:COL_PAD, :] = zrow
        hg_ref[COL_PAD + L:, :] = zrow
        zhalo = jnp.zeros((half // 128, L1_ROWS, ROW_PAD, 128), F32)
        prow_ref[:, :, 0:ROW_PAD, :] = zhalo
        prow_ref[:, :, ROW_PAD + GRID_W:, :] = zhalo

    @pl.when(ph == 0)
    def _():
        h = (x_ref[...] * (1.0 + scale) + shift).astype(BF16)
        a = jnp.dot(h, wag_ref[:, 0:D], preferred_element_type=F32)
        g = jnp.dot(h, wag_ref[:, D:2 * D], preferred_element_type=F32)
        ha = 0.5 * a
        hg_ref[pl.ds(COL_PAD + base, L1_TM), :] = (ha * jnp.tanh(0.5 * g) + ha).astype(BF16)

    @pl.when(ph == 1)
    def _():
        x = x_ref[...]
        h = (x * (1.0 + scale) + shift).astype(BF16)
        sz = _silu(jnp.dot(h, wz_ref[...], preferred_element_type=F32))
        pcol_ref[...] = hg_ref[pl.ds(base, L1_HALO), half:D].astype(F32)
        for lb in range(half // 128):
            prow_ref[lb, :, ROW_PAD:ROW_PAD + GRID_W, :] = (
                hg_ref[pl.ds(COL_PAD + base, L1_TM), 128 * lb:128 * (lb + 1)].astype(F32)
                .reshape(L1_ROWS, GRID_W, 128))
        def row_body(r, carry):
            for lb in range(half // 128):
                lo = 128 * lb
                acc = jnp.zeros((GRID_W, 128), F32) + dwb_ref[:, lo:lo + 128]
                for k in range(CONV_K):
                    off = ROW_PAD - CONV_HALF + k
                    acc = acc + dww_ref[k:k + 1, lo:lo + 128] * prow_ref[lb, r, off:off + GRID_W, :]
                hc_ref[pl.ds(pl.multiple_of(r * GRID_W, GRID_W), GRID_W), lo:lo + 128] = acc
            return carry

        lax.fori_loop(0, L1_ROWS, row_body, 0)
        for lb in range(half // 128):
            lo = 128 * lb
            hi = half + lo
            for rb in range(L1_TM // CONV_SUB):
                acc = jnp.zeros((CONV_SUB, 128), F32) + dwb_ref[:, hi:hi + 128]
                for k in range(CONV_K):
                    r0 = CONV_SUB * rb + GRID_W * k
                    acc = acc + dww_ref[k:k + 1, hi:hi + 128] * pcol_ref[r0:r0 + CONV_SUB, lo:lo + 128]
                hc_ref[CONV_SUB * rb:CONV_SUB * (rb + 1), hi:hi + 128] = acc
        m = _silu(_layer_norm(hc_ref[...], lng_ref[...], lnb_ref[...])) * sz
        y = jnp.dot(m.astype(BF16), wout_ref[...], preferred_element_type=F32)
        gmod = mod_ref[:, 2 * D:3 * D]
        o_ref[...] = _layer_norm(DN_ALPHA * x + gmod * y, ng_ref[...], nb_ref[...])


def _layer1(x, mod, w_in, w_out, dw_w, dw_b, ln_g, ln_b, ng, nb):
    nt = L // L1_TM
    full = lambda *s: pl.BlockSpec(s, lambda b, p, i: (0,) * len(s))
    return pl.pallas_call(
        _layer1_kernel,
        out_shape=jax.ShapeDtypeStruct((B, L, D), F32),
        grid=(B, 2, nt),
        in_specs=[pl.BlockSpec((None, L1_TM, D), lambda b, p, i: (b, i, 0)),
                  pl.BlockSpec((None, 1, 3 * D), lambda b, p, i: (b, 0, 0)),
                  full(D, 2 * D), full(D, D), full(D, D), full(CONV_K, D), full(1, D),
                  full(1, D), full(1, D), full(1, D), full(1, D)],
        out_specs=pl.BlockSpec((None, L1_TM, D), lambda b, p, i: (b, i * p, 0)),
        scratch_shapes=[pltpu.VMEM((L + 2 * COL_PAD, D), BF16),
                        pltpu.VMEM((L1_HALO, D // 2), F32),
                        pltpu.VMEM((D // 256, L1_ROWS, GRID_W + 2 * ROW_PAD, 128), F32),
                        pltpu.VMEM((L1_TM, D), F32)],
        compiler_params=_params("arbitrary", "arbitrary", "arbitrary"),
        name="layer1",
    )(x, mod, w_in[:, :2 * D], w_in[:, 2 * D:], w_out, dw_w, dw_b, ln_g, ln_b, ng, nb)


def kernel(x, c, ctx, c_ctx, mod_w, mod_b, norm_g, norm_b, ev_w_in, ev_w_out, s5_lam_re, s5_lam_im, s5_log_dt, s5_b_re, s5_b_im, s5_c_re, s5_c_im, s5_d, glu_w, glu_b, sgu_ln_g, sgu_ln_b, sgu_w, sgu_b, od_w_in, od_w_out, dw_w, dw_b, conv_ln_g, conv_ln_b):
    TH = S5_T * S5_H
    row = lambda v: v.reshape(1, -1)

    cond8 = jnp.concatenate([c, c_ctx[None], jnp.zeros((3, D), F32)], axis=0)
    mods = _adaln(cond8, mod_w, mod_b)
    mod0 = mods[0, :B].reshape(B, 1, 3 * D)
    mod0c = mods[0, B:B + 1]
    mod1 = mods[1, :B].reshape(B, 1, 3 * D)

    lbr, lbi, cfr, cfi = _discretise(s5_lam_re[0], s5_lam_im[0], s5_log_dt[0])
    rowcat = lambda a: jnp.concatenate([a[0], a[1]], axis=-1).reshape(S5_G, 1, 2 * S5_P)
    colcat = lambda a: jnp.broadcast_to(jnp.concatenate([a[0], a[1]], axis=-1)[:, :, None], (S5_G, 2 * S5_P, TH))
    bt = lambda a: jnp.concatenate([jnp.swapaxes(a[0], 1, 2), jnp.swapaxes(a[1], 1, 2)], axis=-1)
    ct = lambda a: jnp.tile(jnp.concatenate([jnp.swapaxes(a[0], 1, 2), jnp.swapaxes(a[1], 1, 2)], axis=1), (1, 1, S5_T))
    d_row = jnp.tile(s5_d[0].reshape(S5_G, 1, S5_H), (1, 1, S5_T))
    win, wout, mix, l16 = _s5_weights(rowcat(lbr), rowcat(lbi), rowcat(cfr), rowcat(cfi),
                                      bt(s5_b_re[0]), bt(s5_b_im[0]), colcat(lbr), colcat(lbi),
                                      ct(s5_c_re[0]), ct(s5_c_im[0]), d_row)

    w_in0 = ev_w_in[0].astype(BF16)
    w_out0 = ev_w_out[0].astype(BF16)
    glu_w0 = glu_w[0].astype(BF16)
    guz, vln, hs = _inproj0n(x, mod0, w_in0[:, 2 * S5_W:], row(sgu_ln_g[0]), row(sgu_ln_b[0]))
    ua, sza, ua_c = _inproj0a(hs, _ctx_slabs(ctx, mod0c), w_in0[:, :2 * S5_W])
    s_lat = _s5core(ua, ua_c, win, wout, mix, l16)
    y_s5 = _s5tail(s_lat, sza, glu_w0, row(glu_b[0]), w_out0[:S5_W])
    sguw = sgu_w[0].reshape(SGU_HEADS // 2, 2, SGU_CHUNK, SGU_CHUNK)
    sguw = jnp.transpose(sguw, (0, 2, 1, 3)).reshape(SGU_HEADS // 2, SGU_CHUNK, 2 * SGU_CHUNK).astype(BF16)
    sgub = jnp.repeat(sgu_b[0].T, SGU_HD, axis=1)
    x1 = _tail0(x, y_s5, guz, vln, mod0, sguw, sgub, w_out0[S5_W:], row(norm_g[0]), row(norm_b[0]))

    return _layer1(x1, mod1, od_w_in[0].astype(BF16), od_w_out[0].astype(BF16), dw_w[0], row(dw_b[0]),
                   row(conv_ln_g[0]), row(conv_ln_b[0]), row(norm_g[1]), row(norm_b[1]))
```

```python
import functools
import math

import jax
import jax.numpy as jnp
from jax import lax
from jax.experimental import pallas as pl
from jax.experimental.pallas import tpu as pltpu

D = 1024
B = 4
L = 4096
CTX = 256
GRID_W = 64
S5_W = 512
S5_G = 32
S5_H = 16
H_SHIFT = 4
BLK = 128 // S5_H
S5_P = 64
S5_T = 16
SGU_W = 512
SGU_HEADS = 8
SGU_HD = 64
SGU_CHUNK = 128
CONV_K = 31
CONV_HALF = CONV_K // 2
EVEN_IN = 2560
SGU_COL0 = 2 * S5_W
ODD_IN = 3072
DEPTH = 2
DN_ALPHA = (2 * DEPTH) ** 0.25
LN_EPS = 1e-5
N_CHUNK = L // S5_T
N_CCHUNK = CTX // S5_T
VMEM_LIMIT_V7X = 56 * 1024 * 1024

F32 = jnp.float32
BF16 = jnp.bfloat16


GELU_C = math.sqrt(2.0 / math.pi)


def _gelu(x):
    hx = 0.5 * x
    return hx * jnp.tanh(x * ((x * x) * (0.044715 * GELU_C) + GELU_C)) + hx


def _sigmoid(x):
    return 0.5 * jnp.tanh(0.5 * x) + 0.5


def _silu(x):
    hx = 0.5 * x
    return hx * jnp.tanh(hx) + hx


def _layer_norm(x, g, b):
    mu = jnp.mean(x, axis=-1, keepdims=True)
    xc = x - mu
    var = jnp.mean(xc * xc, axis=-1, keepdims=True)
    return xc * lax.rsqrt(var + LN_EPS) * g + b


def _params(*sem):
    return pltpu.CompilerParams(dimension_semantics=sem, vmem_limit_bytes=VMEM_LIMIT_V7X)


def _adaln_kernel(c_ref, w_ref, b_ref, o_ref):
    def split(v):
        hi = v.astype(BF16)
        return hi, (v - hi.astype(F32)).astype(BF16)

    a_hi, a_lo = split(_silu(c_ref[...]))
    w_hi, w_lo = split(w_ref[...])
    dot = functools.partial(jnp.dot, preferred_element_type=F32)
    o_ref[...] = dot(a_hi, w_hi) + dot(a_lo, w_hi) + dot(a_hi, w_lo) + b_ref[...]


def _adaln(cond8, mod_w, mod_b):
    tn = 512
    return pl.pallas_call(
        _adaln_kernel,
        out_shape=jax.ShapeDtypeStruct((DEPTH, 8, 3 * D), F32),
        grid=(DEPTH, 3 * D // tn),
        in_specs=[pl.BlockSpec((8, D), lambda l, j: (0, 0)),
                  pl.BlockSpec((None, D, tn), lambda l, j: (l, 0, j)),
                  pl.BlockSpec((None, 1, tn), lambda l, j: (l, 0, j))],
        out_specs=pl.BlockSpec((None, 8, tn), lambda l, j: (l, 0, j)),
        compiler_params=_params("arbitrary", "arbitrary"),
        name="adaln",
    )(cond8, mod_w, mod_b.reshape(DEPTH, 1, 3 * D))


def _disc_kernel(lr_ref, li_ref, ldt_ref, obr_ref, obi_ref, ocr_ref, oci_ref):
    lr = lr_ref[...]
    li = li_ref[...]
    dt = jnp.exp(ldt_ref[...])
    mag = jnp.exp(lr * dt)
    br = mag * jnp.cos(li * dt)
    bi = mag * jnp.sin(li * dt)
    inv = 1.0 / (lr * lr + li * li)
    nr = br - 1.0
    obr_ref[...] = br
    obi_ref[...] = bi
    ocr_ref[...] = (nr * lr + bi * li) * inv
    oci_ref[...] = (bi * lr - nr * li) * inv


def _discretise(lam_re, lam_im, log_dt):
    shp = jax.ShapeDtypeStruct((2 * S5_G, S5_P), F32)
    ldt = jnp.broadcast_to(log_dt.reshape(2 * S5_G, 1), (2 * S5_G, S5_P))
    outs = pl.pallas_call(
        _disc_kernel, out_shape=(shp, shp, shp, shp), name="s5_discretise",
    )(lam_re.reshape(2 * S5_G, S5_P), lam_im.reshape(2 * S5_G, S5_P), ldt)
    return [o.reshape(2, S5_G, S5_P) for o in outs]


def _cpow(base_pows, j):
    re = None
    im = None
    for k, (pr, pi) in enumerate(base_pows):
        bit = ((j >> k) & 1) == 1
        mr = jnp.where(bit, pr, 1.0)
        mi = jnp.where(bit, pi, 0.0)
        if re is None:
            re, im = mr, mi
        else:
            re, im = re * mr - im * mi, re * mi + im * mr
    return re, im


def _squarings(pr, pi, n):
    out = [(pr, pi)]
    for _ in range(n - 1):
        pr, pi = pr * pr - pi * pi, 2.0 * pr * pi
        out.append((pr, pi))
    return out


def _shift_lanes(x, n):
    lane = lax.broadcasted_iota(jnp.int32, (S5_H, 128), 1)
    lo, hi = x[:, :128], x[:, 128:]
    if n == 0:
        return x
    if n < 128:
        rlo = pltpu.roll(lo, n, axis=1)
        rhi = pltpu.roll(hi, n, axis=1)
        return jnp.concatenate([jnp.where(lane >= n, rlo, 0.0), jnp.where(lane >= n, rhi, rlo)], axis=1)
    m = n - 128
    rlo = lo if m == 0 else pltpu.roll(lo, m, axis=1)
    return jnp.concatenate([jnp.zeros_like(lo), jnp.where(lane >= m, rlo, 0.0)], axis=1)


def _unshift_lanes(x, n):
    lane = lax.broadcasted_iota(jnp.int32, (S5_H, 128), 1)
    lo, hi = x[:, :128], x[:, 128:]
    if n == 0:
        return x
    if n < 128:
        rlo = pltpu.roll(lo, 128 - n, axis=1)
        rhi = pltpu.roll(hi, 128 - n, axis=1)
        keep = lane < 128 - n
        return jnp.concatenate([jnp.where(keep, rlo, rhi), jnp.where(keep, rhi, 0.0)], axis=1)
    m = n - 128
    rhi = hi if m == 0 else pltpu.roll(hi, 128 - m, axis=1)
    return jnp.concatenate([jnp.where(lane < 128 - m, rhi, 0.0), jnp.zeros_like(lo)], axis=1)


def _s5w_kernel(lrow_re, lrow_im, crow_re, crow_im, bt_re, bt_im,
                lcol_re, lcol_im, ct_re, ct_im, d_ref,
                win_ref, wout_ref, mix_ref, l16_ref):
    TH = S5_T * S5_H
    bg = pl.program_id(0) & (BLK - 1)

    def chunk_pos(idx):
        return (((idx >> H_SHIFT) - bg) & (BLK - 1)) + ((idx >> 7) << 3)

    lr = lrow_re[...]
    li = lrow_im[...]
    pows_row = _squarings(lr, li, 5)
    l16_ref[0:1, :] = pows_row[4][0]
    l16_ref[1:2, :] = pows_row[4][1]
    l16_ref[2:8, :] = jnp.zeros((6, 128), F32)
    cr = crow_re[...]
    ci = crow_im[...]
    btr = bt_re[...]
    bti = bt_im[...]
    bbr = cr * btr - ci * bti
    bbi = cr * bti + ci * btr
    s_idx = chunk_pos(lax.broadcasted_iota(jnp.int32, (TH, 128), 0))
    lane = lax.broadcasted_iota(jnp.int32, (TH, 128), 1)
    jw = jnp.where(lane < S5_P, S5_T - 1 - s_idx, s_idx)
    pr, pi = _cpow(pows_row[:4], jw)
    tbr = jnp.broadcast_to(bbr[None], (S5_T, S5_H, 128)).reshape(TH, 128)
    tbi = jnp.broadcast_to(bbi[None], (S5_T, S5_H, 128)).reshape(TH, 128)
    win_ref[:, 0:128] = (pr * tbr - pi * tbi).astype(BF16)
    win_ref[:, 128:256] = (pr * tbi + pi * tbr).astype(BF16)

    cpows = _squarings(lcol_re[...], lcol_im[...], 5)
    row = lax.broadcasted_iota(jnp.int32, (2 * S5_P, TH), 0)
    lane_w = lax.broadcasted_iota(jnp.int32, (2 * S5_P, TH), 1)
    t_idx = chunk_pos(lane_w)
    j_idx = lane_w >> H_SHIFT
    is_f = row < S5_P
    ctr = ct_re[...]
    cti = ct_im[...]
    er, ei = _cpow(cpows[:4], jnp.where(is_f, t_idx, S5_T - 1 - t_idx))
    er, ei = er * cpows[0][0] - ei * cpows[0][1], er * cpows[0][1] + ei * cpows[0][0]
    wr = ctr * er - cti * ei
    wi = ctr * ei + cti * er
    wout_ref[0:128, :] = wr.astype(BF16)
    wout_ref[128:256, :] = (-wi).astype(BF16)
    kr, ki = _cpow(cpows[:4], jnp.where(is_f, j_idx, S5_T - 1 - j_idx))
    ekr = ctr * kr - cti * ki
    eki = ctr * ki + cti * kr
    lane16 = lax.broadcasted_iota(jnp.int32, (S5_H, 128), 1)
    mf = lane16 < S5_P
    hp = lax.Precision.HIGHEST
    dot = functools.partial(jnp.dot, preferred_element_type=F32, precision=hp)
    kkf = dot(jnp.where(mf, bbr, 0.0), ekr) - dot(jnp.where(mf, bbi, 0.0), eki)
    kkb = dot(jnp.where(mf, 0.0, bbr), ekr) - dot(jnp.where(mf, 0.0, bbi), eki)
    dl = d_ref[...]
    r16 = lax.broadcasted_iota(jnp.int32, (S5_H, TH), 0)
    l256 = lax.broadcasted_iota(jnp.int32, (S5_H, TH), 1)
    rot = bg * S5_H
    for s in range(S5_T):
        blk = _shift_lanes(kkf, S5_H * s) + _unshift_lanes(kkb, S5_H * (S5_T - 1 - s))
        blk = blk + jnp.where(l256 == r16 + S5_H * s, dl, 0.0)
        blk = jnp.concatenate([pltpu.roll(blk[:, :128], rot, axis=1), pltpu.roll(blk[:, 128:], rot, axis=1)], axis=1)
        rho = ((s + bg) & (BLK - 1)) + (s & BLK)
        mix_ref[pl.ds(pl.multiple_of(rho * S5_H, S5_H), S5_H), :] = blk.astype(BF16)


def _s5_weights(lrow_re, lrow_im, crow_re, crow_im, bt_re, bt_im, lcol_re, lcol_im, ct_re, ct_im, d_row):
    TH = S5_T * S5_H
    g3 = lambda r, c: pl.BlockSpec((None, r, c), lambda g: (g, 0, 0))
    wshape = jax.ShapeDtypeStruct((S5_G, TH, TH), BF16)
    return pl.pallas_call(
        _s5w_kernel,
        out_shape=(wshape, wshape, wshape, jax.ShapeDtypeStruct((S5_G, 8, 128), F32)),
        grid=(S5_G,),
        in_specs=[g3(1, 128)] * 4 + [g3(S5_H, 128)] * 2 + [g3(128, TH)] * 4 + [g3(1, TH)],
        out_specs=(g3(TH, TH), g3(TH, TH), g3(TH, TH), g3(8, 128)),
        compiler_params=_params("arbitrary"),
        name="s5_weights",
    )(lrow_re, lrow_im, crow_re, crow_im, bt_re, bt_im, lcol_re, lcol_im, ct_re, ct_im, d_row)


def _rot_blocks(v, r):
    cols = [pltpu.roll(v[:, 128 * q:128 * (q + 1)], S5_H * r, axis=1) for q in range(v.shape[1] // 128)]
    return jnp.concatenate(cols, axis=1)


def _slabs_of(h, hs_ref):
    h3 = h.reshape(h.shape[0] // S5_T, S5_T, h.shape[1])
    for s in range(S5_T):
        hs_ref[s] = h3[:, s, :].astype(BF16)


def _inproj0n_kernel(x_ref, mod_ref, w_ref, perm_ref, lng_ref, lnb_ref, guz_ref, vln_ref, hs_ref):
    shift = mod_ref[:, 0:D]
    scale = mod_ref[:, D:2 * D]
    hb = (x_ref[...] * (1.0 + scale) + shift).astype(BF16)
    nct = hb.shape[0] // S5_T
    hs = jnp.dot(perm_ref[...], hb, preferred_element_type=F32).astype(BF16)
    for s in range(S5_T):
        hs_ref[s] = hs[nct * s:nct * (s + 1), :]
    dot = lambda lo: jnp.dot(hb, w_ref[:, SGU_COL0 + lo:SGU_COL0 + lo + 512], preferred_element_type=F32)
    guz_ref[...] = (_gelu(dot(0)) * _silu(dot(1024))).astype(BF16)
    vln_ref[...] = _layer_norm(_gelu(dot(512)), lng_ref[...], lnb_ref[...]).astype(BF16)


def _inproj0n(x, mod, w_in, ln_g, ln_b, tm=512):
    nct = tm // S5_T
    ri = lax.broadcasted_iota(jnp.int32, (tm, tm), 0)
    ci = lax.broadcasted_iota(jnp.int32, (tm, tm), 1)
    perm = ((ri // nct == ci % S5_T) & (ri % nct == ci // S5_T)).astype(BF16)
    o = jax.ShapeDtypeStruct((B, L, 512), BF16)
    ospec = pl.BlockSpec((None, tm, 512), lambda b, i: (b, i, 0))
    full = lambda *s: pl.BlockSpec(s, lambda b, i: (0,) * len(s))
    return pl.pallas_call(
        _inproj0n_kernel,
        out_shape=(o, o, jax.ShapeDtypeStruct((S5_T, B * N_CHUNK, D), BF16)),
        grid=(B, L // tm),
        in_specs=[pl.BlockSpec((None, tm, D), lambda b, i: (b, i, 0)),
                  pl.BlockSpec((None, 1, 3 * D), lambda b, i: (b, 0, 0)),
                  full(D, EVEN_IN), full(tm, tm), full(1, 512), full(1, 512)],
        out_specs=(ospec, ospec,
                   pl.BlockSpec((S5_T, nct, D), lambda b, i: (0, b * (N_CHUNK // nct) + i, 0))),
        compiler_params=_params("arbitrary", "arbitrary"),
        name="inproj0n",
    )(x, mod, w_in, perm, ln_g, ln_b)


def _ctx_slabs_kernel(x_ref, mod_ref, hs_ref):
    h = x_ref[...] * (1.0 + mod_ref[:, D:2 * D]) + mod_ref[:, 0:D]
    _slabs_of(h, hs_ref)


def _ctx_slabs(ctx, mod_c):
    return pl.pallas_call(
        _ctx_slabs_kernel,
        out_shape=jax.ShapeDtypeStruct((S5_T, B * N_CCHUNK, D), BF16),
        grid=(B,),
        in_specs=[pl.BlockSpec((None, CTX, D), lambda b: (b, 0, 0)),
                  pl.BlockSpec((1, 3 * D), lambda b: (0, 0))],
        out_specs=pl.BlockSpec((S5_T, N_CCHUNK, D), lambda b: (0, b, 0)),
        compiler_params=_params("arbitrary"),
        name="ctx_slabs",
    )(ctx, mod_c)


def _inproj0a_kernel(hs_ref, hc_ref, w_ref, ua_ref, sza_ref, uc_ref):
    r = pl.program_id(0)
    h = hs_ref[...]
    ua_ref[...] = _rot_blocks(jnp.dot(h, w_ref[:, 0:512], preferred_element_type=F32), r).astype(BF16)
    sza_ref[...] = _silu(jnp.dot(h, w_ref[:, 512:1024], preferred_element_type=F32)).astype(BF16)
    uc_ref[...] = _rot_blocks(jnp.dot(hc_ref[...], w_ref[:, 0:512], preferred_element_type=F32), r).astype(BF16)


def _inproj0a(hs, hcs, w_s5):
    slab = lambda r, h: r + BLK * h
    sspec = lambda n, w: pl.BlockSpec((None, n, w), lambda r, h: (slab(r, h), 0, 0))
    so = lambda n: jax.ShapeDtypeStruct((S5_T, n, 512), BF16)
    nl, ncx = B * N_CHUNK, B * N_CCHUNK
    return pl.pallas_call(
        _inproj0a_kernel,
        out_shape=(so(nl), so(nl), so(ncx)),
        grid=(BLK, S5_T // BLK),
        in_specs=[sspec(nl, D), sspec(ncx, D), pl.BlockSpec((D, 1024), lambda r, h: (0, 0))],
        out_specs=(sspec(nl, 512), sspec(nl, 512), sspec(ncx, 512)),
        compiler_params=_params("arbitrary", "arbitrary"),
        name="inproj0a",
    )(hs, hcs, w_s5)


SCAN_GROUPS = 4


def _scan_tiles(sre_ref, sim_ref, hre_ref, him_ref, n_tiles, carry, lams):
    row = lax.broadcasted_iota(jnp.int32, (8, 128), 0)
    lane = lax.broadcasted_iota(jnp.int32, (8, 128), 1)
    first = row < B
    fwd = lane < S5_P

    def body(k, c):
        of = pl.multiple_of(k * 8, 8)
        ob = pl.multiple_of((n_tiles - 1 - k) * 8, 8)
        out = []
        for gi in range(SCAN_GROUPS):
            lre, lim = lams[gi]
            hr, hi = c[2 * gi], c[2 * gi + 1]
            sr = jnp.where(fwd, sre_ref[gi, pl.ds(of, 8), :], pltpu.roll(sre_ref[gi, pl.ds(ob, 8), :], B, axis=0))
            si = jnp.where(fwd, sim_ref[gi, pl.ds(of, 8), :], pltpu.roll(sim_ref[gi, pl.ds(ob, 8), :], B, axis=0))
            h1r = lre * hr - lim * hi + sr
            h1i = lre * hi + lim * hr + si
            r1r = pltpu.roll(h1r, B, axis=0)
            r1i = pltpu.roll(h1i, B, axis=0)
            if hre_ref is not None:
                hre_ref[gi, pl.ds(of, 8), :] = jnp.where(first, hr, r1r)
                him_ref[gi, pl.ds(of, 8), :] = jnp.where(first, hi, r1i)
            h2r = lre * r1r - lim * r1i + sr
            h2i = lre * r1i + lim * r1r + si
            out.append(jnp.where(first, pltpu.roll(h2r, B, axis=0), h2r))
            out.append(jnp.where(first, pltpu.roll(h2i, B, axis=0), h2i))
        return tuple(out)

    return lax.fori_loop(0, n_tiles, body, carry)


def _unreverse_tiles(h_ref, n_tiles):
    lane = lax.broadcasted_iota(jnp.int32, (8, 128), 1)
    fwd = lane < S5_P

    def body(k, carry):
        of = pl.multiple_of(k * 8, 8)
        ob = pl.multiple_of((n_tiles - 1 - k) * 8, 8)
        for gi in range(SCAN_GROUPS):
            a = h_ref[gi, pl.ds(of, 8), :]
            b = h_ref[gi, pl.ds(ob, 8), :]
            h_ref[gi, pl.ds(of, 8), :] = jnp.where(fwd, a, pltpu.roll(b, B, axis=0))
            h_ref[gi, pl.ds(ob, 8), :] = jnp.where(fwd, b, pltpu.roll(a, B, axis=0))
        return carry

    lax.fori_loop(0, n_tiles // 2, body, 0, unroll=4)


def _gather_group(slab_ref, src):
    halves = []
    for h in range(S5_T // BLK):
        acc = slab_ref[BLK * h]
        for s in range(1, BLK):
            acc = jnp.where(src == s, slab_ref[BLK * h + s], acc)
        halves.append(acc)
    return jnp.concatenate(halves, axis=1)


def _s5core_kernel(ul_ref, uc_ref, win_ref, wout_ref, mix_ref, l16_ref, o_ref,
                   u_ref, sre_ref, sim_ref, cre_ref, cim_ref, hre_ref, him_ref, y_ref):
    nl = N_CHUNK * B
    ncx = N_CCHUNK * B
    blk_l = lax.broadcasted_iota(jnp.int32, (nl, 128), 1) >> H_SHIFT
    blk_c = lax.broadcasted_iota(jnp.int32, (ncx, 128), 1) >> H_SHIFT
    for g0 in range(0, BLK, SCAN_GROUPS):
        for gi in range(SCAN_GROUPS):
            bg = g0 + gi
            win = win_ref[bg]
            src_l = ((blk_l - bg) & (BLK - 1)).astype(F32).astype(BF16)
            src_c = ((blk_c - bg) & (BLK - 1)).astype(F32).astype(BF16)
            u = _gather_group(ul_ref, src_l)
            u_ref[gi] = u
            sl = jnp.dot(u, win, preferred_element_type=F32)
            sc = jnp.dot(_gather_group(uc_ref, src_c), win, preferred_element_type=F32)
            for b in range(B):
                sre_ref[gi, pl.ds(b, N_CHUNK, stride=B), :] = sl[N_CHUNK * b:N_CHUNK * (b + 1), 0:128]
                sim_ref[gi, pl.ds(b, N_CHUNK, stride=B), :] = sl[N_CHUNK * b:N_CHUNK * (b + 1), 128:256]
                cre_ref[gi, pl.ds(b, N_CCHUNK, stride=B), :] = sc[N_CCHUNK * b:N_CCHUNK * (b + 1), 0:128]
                cim_ref[gi, pl.ds(b, N_CCHUNK, stride=B), :] = sc[N_CCHUNK * b:N_CCHUNK * (b + 1), 128:256]
        lams = [(jnp.broadcast_to(l16_ref[g0 + gi, 0:1, :], (8, 128)),
                 jnp.broadcast_to(l16_ref[g0 + gi, 1:2, :], (8, 128))) for gi in range(SCAN_GROUPS)]
        zero = tuple(jnp.zeros((8, 128), F32) for _ in range(2 * SCAN_GROUPS))
        carry = _scan_tiles(cre_ref, cim_ref, None, None, ncx // 8, zero, lams)
        _scan_tiles(sre_ref, sim_ref, hre_ref, him_ref, nl // 8, carry, lams)
        _unreverse_tiles(hre_ref, nl // 8)
        _unreverse_tiles(him_ref, nl // 8)
        for gi in range(SCAN_GROUPS):
            bg = g0 + gi
            y = jnp.dot(u_ref[gi], mix_ref[bg], preferred_element_type=F32)
            hs = []
            for b in range(B):
                hs.append(jnp.concatenate([hre_ref[gi, pl.ds(b, N_CHUNK, stride=B), :],
                                           him_ref[gi, pl.ds(b, N_CHUNK, stride=B), :]], axis=1))
            hcat = jnp.concatenate(hs, axis=0).astype(BF16)
            y = y + jnp.dot(hcat, wout_ref[bg], preferred_element_type=F32)
            y_ref[bg] = y.astype(BF16)

    blk = blk_l.astype(F32).astype(BF16)
    for s in range(S5_T):
        h, r = s // BLK, s % BLK
        acc = None
        for j in range(BLK):
            piece = y_ref[(j - r) % BLK, :, 128 * h:128 * (h + 1)]
            acc = piece if acc is None else jnp.where(blk == j, piece, acc)
        o_ref[s] = acc


def _s5core(ul, uc, win, wout, mix, l16):
    TH = S5_T * S5_H
    nl = N_CHUNK * B
    ncx = N_CCHUNK * B
    g4 = lambda r, c: pl.BlockSpec((BLK, r, c), lambda q: (q, 0, 0))
    col = lambda n: pl.BlockSpec((S5_T, n, 128), lambda q: (0, 0, q))
    f32s = lambda n: pltpu.VMEM((SCAN_GROUPS, n, 128), F32)
    return pl.pallas_call(
        _s5core_kernel,
        out_shape=jax.ShapeDtypeStruct((S5_T, nl, S5_W), BF16),
        grid=(S5_G // BLK,),
        in_specs=[col(nl), col(ncx), g4(TH, TH), g4(TH, TH), g4(TH, TH), g4(8, 128)],
        out_specs=col(nl),
        scratch_shapes=[pltpu.VMEM((SCAN_GROUPS, nl, TH), BF16),
                        f32s(nl), f32s(nl), f32s(ncx), f32s(ncx), f32s(nl), f32s(nl),
                        pltpu.VMEM((BLK, nl, TH), BF16)],
        compiler_params=_params("arbitrary"),
        name="s5core",
    )(ul, uc, win, wout, mix, l16)


def _s5tail_kernel(slat_ref, sza_ref, gluw_ref, glub_ref, wtop_ref, y_ref):
    unrot = (BLK - pl.program_id(0)) & (BLK - 1)
    for b in range(B):
        rows = slice(N_CHUNK * b, N_CHUNK * (b + 1))
        g = _gelu(_rot_blocks(slat_ref[rows, :].astype(F32), unrot))
        gate = _sigmoid(jnp.dot(g.astype(BF16), gluw_ref[...], preferred_element_type=F32) + glub_ref[...])
        a = (g * gate * sza_ref[rows, :].astype(F32)).astype(BF16)
        y_ref[rows, :] = jnp.dot(a, wtop_ref[...], preferred_element_type=F32).astype(BF16)


def _s5tail(slat, sza, glu_w, glu_b, w_top):
    slab = lambda r, h: r + BLK * h
    sspec = lambda w: pl.BlockSpec((None, N_CHUNK * B, w), lambda r, h: (slab(r, h), 0, 0))
    full = lambda *s: pl.BlockSpec(s, lambda r, h: (0,) * len(s))
    return pl.pallas_call(
        _s5tail_kernel,
        out_shape=jax.ShapeDtypeStruct((S5_T, N_CHUNK * B, D), BF16),
        grid=(BLK, S5_T // BLK),
        in_specs=[sspec(512), sspec(512), full(512, 512), full(1, 512), full(S5_W, D)],
        out_specs=sspec(D),
        compiler_params=_params("arbitrary", "arbitrary"),
        name="s5tail",
    )(slat, sza, glu_w, glu_b, w_top)


PERM_ROWS = S5_T * S5_T


def _tail0_kernel(x_ref, ys5_ref, guz_ref, vln_ref, mod_ref, sguw_ref, sgub_ref, wbot_ref, ng_ref, nb_ref, o_ref):
    tm = x_ref.shape[0]
    lane = lax.broadcasted_iota(jnp.int32, (SGU_CHUNK, 128), 1)
    lo = lane < SGU_HD
    zero = jnp.zeros((SGU_CHUNK, 128), BF16)
    chunks = []
    for ci in range(tm // SGU_CHUNK):
        v = vln_ref[ci * SGU_CHUNK:(ci + 1) * SGU_CHUNK, :]
        cols = []
        for pi in range(SGU_HEADS // 2):
            vp = v[:, 128 * pi:128 * (pi + 1)]
            bm = jnp.concatenate([jnp.where(lo, vp, zero), jnp.where(lo, zero, vp)], axis=0)
            cols.append(jnp.dot(sguw_ref[pi], bm, preferred_element_type=F32))
        chunks.append(jnp.concatenate(cols, axis=1) + sgub_ref[...])
    s = jnp.concatenate(chunks, axis=0)
    bsg = (guz_ref[...].astype(F32) * s).astype(BF16)
    ri = lax.broadcasted_iota(jnp.int32, (PERM_ROWS, PERM_ROWS), 0)
    ci = lax.broadcasted_iota(jnp.int32, (PERM_ROWS, PERM_ROWS), 1)
    perm = jnp.where(((ri >> H_SHIFT) == (ci & (S5_T - 1))) & ((ri & (S5_T - 1)) == (ci >> H_SHIFT)), 1.0, 0.0)
    perm = perm.astype(BF16)
    ys5 = jnp.concatenate(
        [jnp.dot(perm, ys5_ref[:, S5_T * j:S5_T * (j + 1), :].reshape(PERM_ROWS, D), preferred_element_type=F32)
         for j in range(tm // PERM_ROWS)], axis=0)
    y = ys5 + jnp.dot(bsg, wbot_ref[...], preferred_element_type=F32)
    gmod = mod_ref[:, 2 * D:3 * D]
    o_ref[...] = _layer_norm(DN_ALPHA * x_ref[...] + gmod * y, ng_ref[...], nb_ref[...])


def _tail0(x, ys5, guz, vln, mod, sguw, sgub, w_bot, ng, nb, tm=512):
    nct = tm // S5_T
    t512 = pl.BlockSpec((None, tm, 512), lambda b, i: (b, i, 0))
    tD = pl.BlockSpec((None, tm, D), lambda b, i: (b, i, 0))
    full = lambda *s: pl.BlockSpec(s, lambda b, i: (0,) * len(s))
    return pl.pallas_call(
        _tail0_kernel,
        out_shape=jax.ShapeDtypeStruct((B, L, D), F32),
        grid=(B, L // tm),
        in_specs=[tD, pl.BlockSpec((S5_T, nct, D), lambda b, i: (0, b * (N_CHUNK // nct) + i, 0)), t512, t512,
                  pl.BlockSpec((None, 1, 3 * D), lambda b, i: (b, 0, 0)),
                  full(SGU_HEADS // 2, SGU_CHUNK, 256), full(SGU_CHUNK, 512),
                  pl.BlockSpec((SGU_W, D), lambda b, i: (1, 0)), full(1, D), full(1, D)],
        out_specs=tD,
        compiler_params=_params("arbitrary", "arbitrary"),
        name="tail0",
    )(x, ys5, guz, vln, mod, sguw, sgub, w_bot, ng, nb)


def _inproj1_kernel(x_ref, mod_ref, w_ref, hg_ref, sz_ref):
    shift = mod_ref[:, 0:D]
    scale = mod_ref[:, D:2 * D]
    h = (x_ref[...] * (1.0 + scale) + shift).astype(BF16)
    dot = lambda lo: jnp.dot(h, w_ref[:, lo:lo + D], preferred_element_type=F32)
    hg_ref[...] = (dot(0) * _sigmoid(dot(D))).astype(BF16)
    sz_ref[...] = _silu(dot(2 * D)).astype(BF16)


def _inproj1(x, mod, w_in, tm=512):
    o = jax.ShapeDtypeStruct((B, L, D), BF16)
    ospec = pl.BlockSpec((None, tm, D), lambda b, i: (b, i, 0))
    return pl.pallas_call(
        _inproj1_kernel,
        out_shape=(o, o),
        grid=(B, L // tm),
        in_specs=[pl.BlockSpec((None, tm, D), lambda b, i: (b, i, 0)),
                  pl.BlockSpec((None, 1, 3 * D), lambda b, i: (b, 0, 0)),
                  pl.BlockSpec((D, ODD_IN), lambda b, i: (0, 0))],
        out_specs=(ospec, ospec),
        compiler_params=_params("arbitrary", "arbitrary"),
        name="inproj1",
    )(x, mod, w_in)


ROW_PAD = 16
COL_PAD = CONV_HALF * GRID_W
CONV_ROWS = 128


def _conv_kernel(h_ref, w_ref, b_ref, o_ref, prow_ref, pcol_ref):
    j = pl.program_id(1)
    bias = b_ref[...]

    @pl.when(j < (D // 2) // 128)
    def _():
        zpad = jnp.zeros((GRID_W, ROW_PAD, 128), F32)
        prow_ref[:, 0:ROW_PAD, :] = zpad
        prow_ref[:, ROW_PAD + GRID_W:, :] = zpad
        prow_ref[:, ROW_PAD:ROW_PAD + GRID_W, :] = h_ref[...].astype(F32).reshape(GRID_W, GRID_W, 128)

        def body(r, carry):
            acc = jnp.zeros((GRID_W, 128), F32) + bias
            for k in range(CONV_K):
                off = ROW_PAD - CONV_HALF + k
                acc = acc + w_ref[k:k + 1, :] * prow_ref[r, off:off + GRID_W, :]
            o_ref[pl.ds(pl.multiple_of(r * GRID_W, GRID_W), GRID_W), :] = acc.astype(BF16)
            return carry

        lax.fori_loop(0, GRID_W, body, 0, unroll=2)

    @pl.when(j >= (D // 2) // 128)
    def _():
        zpad = jnp.zeros((COL_PAD, 128), F32)
        pcol_ref[0:COL_PAD, :] = zpad
        pcol_ref[COL_PAD + L:, :] = zpad
        pcol_ref[COL_PAD:COL_PAD + L, :] = h_ref[...].astype(F32)

        def body(i, carry):
            base = pl.multiple_of(i * CONV_ROWS, CONV_ROWS)
            acc = jnp.zeros((CONV_ROWS, 128), F32) + bias
            for k in range(CONV_K):
                acc = acc + w_ref[k:k + 1, :] * pcol_ref[pl.ds(base + k * GRID_W, CONV_ROWS), :]
            o_ref[pl.ds(base, CONV_ROWS), :] = acc.astype(BF16)
            return carry

        lax.fori_loop(0, L // CONV_ROWS, body, 0)


def _conv(hg, dw_w, dw_b):
    return pl.pallas_call(
        _conv_kernel,
        out_shape=jax.ShapeDtypeStruct((B, L, D), BF16),
        grid=(B, D // 128),
        in_specs=[pl.BlockSpec((None, L, 128), lambda b, j: (b, 0, j)),
                  pl.BlockSpec((CONV_K, 128), lambda b, j: (0, j)),
                  pl.BlockSpec((1, 128), lambda b, j: (0, j))],
        out_specs=pl.BlockSpec((None, L, 128), lambda b, j: (b, 0, j)),
        scratch_shapes=[pltpu.VMEM((GRID_W, GRID_W + 2 * ROW_PAD, 128), F32),
                        pltpu.VMEM((L + 2 * COL_PAD, 128), F32)],
        compiler_params=_params("arbitrary", "arbitrary"),
        name="dwconv",
    )(hg, dw_w, dw_b)


def _tail1_kernel(x_ref, hc_ref, sz_ref, mod_ref, lng_ref, lnb_ref, wout_ref, ng_ref, nb_ref, o_ref):
    m = _silu(_layer_norm(hc_ref[...].astype(F32), lng_ref[...], lnb_ref[...])) * sz_ref[...].astype(F32)
    y = jnp.dot(m.astype(BF16), wout_ref[...], preferred_element_type=F32)
    gmod = mod_ref[:, 2 * D:3 * D]
    o_ref[...] = _layer_norm(DN_ALPHA * x_ref[...] + gmod * y, ng_ref[...], nb_ref[...])


def _tail1(x, hc, sz, mod, ln_g, ln_b, w_out, ng, nb, tm=512):
    tD = pl.BlockSpec((None, tm, D), lambda b, i: (b, i, 0))
    full = lambda *s: pl.BlockSpec(s, lambda b, i: (0,) * len(s))
    return pl.pallas_call(
        _tail1_kernel,
        out_shape=jax.ShapeDtypeStruct((B, L, D), F32),
        grid=(B, L // tm),
        in_specs=[tD, tD, tD, pl.BlockSpec((None, 1, 3 * D), lambda b, i: (b, 0, 0)),
                  full(1, D), full(1, D), full(D, D), full(1, D), full(1, D)],
        out_specs=tD,
        compiler_params=_params("arbitrary", "arbitrary"),
        name="tail1",
    )(x, hc, sz, mod, ln_g, ln_b, w_out, ng, nb)


def kernel(x, c, ctx, c_ctx, mod_w, mod_b, norm_g, norm_b, ev_w_in, ev_w_out, s5_lam_re, s5_lam_im, s5_log_dt, s5_b_re, s5_b_im, s5_c_re, s5_c_im, s5_d, glu_w, glu_b, sgu_ln_g, sgu_ln_b, sgu_w, sgu_b, od_w_in, od_w_out, dw_w, dw_b, conv_ln_g, conv_ln_b):
    TH = S5_T * S5_H
    row = lambda v: v.reshape(1, -1)

    cond8 = jnp.concatenate([c, c_ctx[None], jnp.zeros((3, D), F32)], axis=0)
    mods = _adaln(cond8, mod_w, mod_b)
    mod0 = mods[0, :B].reshape(B, 1, 3 * D)
    mod0c = mods[0, B:B + 1]
    mod1 = mods[1, :B].reshape(B, 1, 3 * D)

    lbr, lbi, cfr, cfi = _discretise(s5_lam_re[0], s5_lam_im[0], s5_log_dt[0])
    rowcat = lambda a: jnp.concatenate([a[0], a[1]], axis=-1).reshape(S5_G, 1, 2 * S5_P)
    colcat = lambda a: jnp.broadcast_to(jnp.concatenate([a[0], a[1]], axis=-1)[:, :, None], (S5_G, 2 * S5_P, TH))
    bt = lambda a: jnp.concatenate([jnp.swapaxes(a[0], 1, 2), jnp.swapaxes(a[1], 1, 2)], axis=-1)
    ct = lambda a: jnp.tile(jnp.concatenate([jnp.swapaxes(a[0], 1, 2), jnp.swapaxes(a[1], 1, 2)], axis=1), (1, 1, S5_T))
    d_row = jnp.tile(s5_d[0].reshape(S5_G, 1, S5_H), (1, 1, S5_T))
    win, wout, mix, l16 = _s5_weights(rowcat(lbr), rowcat(lbi), rowcat(cfr), rowcat(cfi),
                                      bt(s5_b_re[0]), bt(s5_b_im[0]), colcat(lbr), colcat(lbi),
                                      ct(s5_c_re[0]), ct(s5_c_im[0]), d_row)

    w_in0 = ev_w_in[0].astype(BF16)
    w_out0 = ev_w_out[0].astype(BF16)
    glu_w0 = glu_w[0].astype(BF16)
    guz, vln, hs = _inproj0n(x, mod0, w_in0, row(sgu_ln_g[0]), row(sgu_ln_b[0]))
    ua, sza, ua_c = _inproj0a(hs, _ctx_slabs(ctx, mod0c), w_in0)
    s_lat = _s5core(ua, ua_c, win, wout, mix, l16)
    y_s5 = _s5tail(s_lat, sza, glu_w0, row(glu_b[0]), w_out0)
    sguw = sgu_w[0].reshape(SGU_HEADS // 2, 2, SGU_CHUNK, SGU_CHUNK)
    sguw = jnp.transpose(sguw, (0, 2, 1, 3)).reshape(SGU_HEADS // 2, SGU_CHUNK, 2 * SGU_CHUNK).astype(BF16)
    sgub = jnp.repeat(sgu_b[0].T, SGU_HD, axis=1)
    x1 = _tail0(x, y_s5, guz, vln, mod0, sguw, sgub, w_out0, row(norm_g[0]), row(norm_b[0]))

    hg, sz = _inproj1(x1, mod1, od_w_in[0].astype(BF16))
    hc = _conv(hg, dw_w[0], row(dw_b[0]))
    return _tail1(x1, hc, sz, mod1, row(conv_ln_g[0]), row(conv_ln_b[0]), od_w_out[0].astype(BF16),
                  row(norm_g[1]), row(norm_b[1]))
```

```python
import functools
import math

import jax
import jax.numpy as jnp
from jax import lax
from jax.experimental import pallas as pl
from jax.experimental.pallas import tpu as pltpu

D = 1024
B = 4
L = 4096
CTX = 256
GRID_W = 64
S5_W = 512
S5_G = 32
S5_H = 16
H_SHIFT = 4
BLK = 128 // S5_H
S5_P = 64
S5_T = 16
SGU_W = 512
SGU_HEADS = 8
SGU_HD = 64
SGU_CHUNK = 128
CONV_K = 31
CONV_HALF = CONV_K // 2
EVEN_IN = 2560
SGU_COL0 = 2 * S5_W
ODD_IN = 3072
DEPTH = 2
DN_ALPHA = (2 * DEPTH) ** 0.25
LN_EPS = 1e-5
N_CHUNK = L // S5_T
N_CCHUNK = CTX // S5_T
VMEM_LIMIT_V7X = 56 * 1024 * 1024

F32 = jnp.float32
BF16 = jnp.bfloat16


GELU_C = math.sqrt(2.0 / math.pi)


def _gelu(x):
    hx = 0.5 * x
    return hx * jnp.tanh(x * ((x * x) * (0.044715 * GELU_C) + GELU_C)) + hx


def _sigmoid(x):
    return 0.5 * jnp.tanh(0.5 * x) + 0.5


def _silu(x):
    hx = 0.5 * x
    return hx * jnp.tanh(hx) + hx


def _layer_norm(x, g, b):
    mu = jnp.mean(x, axis=-1, keepdims=True)
    xc = x - mu
    var = jnp.mean(xc * xc, axis=-1, keepdims=True)
    return xc * lax.rsqrt(var + LN_EPS) * g + b


def _params(*sem):
    return pltpu.CompilerParams(dimension_semantics=sem, vmem_limit_bytes=VMEM_LIMIT_V7X)


def _adaln_kernel(c_ref, w_ref, b_ref, o_ref):
    def split(v):
        hi = v.astype(BF16)
        return hi, (v - hi.astype(F32)).astype(BF16)

    a_hi, a_lo = split(_silu(c_ref[...]))
    w_hi, w_lo = split(w_ref[...])
    dot = functools.partial(jnp.dot, preferred_element_type=F32)
    o_ref[...] = dot(a_hi, w_hi) + dot(a_lo, w_hi) + dot(a_hi, w_lo) + b_ref[...]


def _adaln(cond8, mod_w, mod_b):
    tn = 512
    return pl.pallas_call(
        _adaln_kernel,
        out_shape=jax.ShapeDtypeStruct((DEPTH, 8, 3 * D), F32),
        grid=(DEPTH, 3 * D // tn),
        in_specs=[pl.BlockSpec((8, D), lambda l, j: (0, 0)),
                  pl.BlockSpec((None, D, tn), lambda l, j: (l, 0, j)),
                  pl.BlockSpec((None, 1, tn), lambda l, j: (l, 0, j))],
        out_specs=pl.BlockSpec((None, 8, tn), lambda l, j: (l, 0, j)),
        compiler_params=_params("arbitrary", "arbitrary"),
        name="adaln",
    )(cond8, mod_w, mod_b.reshape(DEPTH, 1, 3 * D))


def _disc_kernel(lr_ref, li_ref, ldt_ref, obr_ref, obi_ref, ocr_ref, oci_ref):
    lr = lr_ref[...]
    li = li_ref[...]
    dt = jnp.exp(ldt_ref[...])
    mag = jnp.exp(lr * dt)
    br = mag * jnp.cos(li * dt)
    bi = mag * jnp.sin(li * dt)
    inv = 1.0 / (lr * lr + li * li)
    nr = br - 1.0
    obr_ref[...] = br
    obi_ref[...] = bi
    ocr_ref[...] = (nr * lr + bi * li) * inv
    oci_ref[...] = (bi * lr - nr * li) * inv


def _discretise(lam_re, lam_im, log_dt):
    shp = jax.ShapeDtypeStruct((2 * S5_G, S5_P), F32)
    ldt = jnp.broadcast_to(log_dt.reshape(2 * S5_G, 1), (2 * S5_G, S5_P))
    outs = pl.pallas_call(
        _disc_kernel, out_shape=(shp, shp, shp, shp), name="s5_discretise",
    )(lam_re.reshape(2 * S5_G, S5_P), lam_im.reshape(2 * S5_G, S5_P), ldt)
    return [o.reshape(2, S5_G, S5_P) for o in outs]


S5W_GROUPS = 4


def _cpow(base_pows, j):
    re = None
    im = None
    for k, (pr, pi) in enumerate(base_pows):
        bit = ((j >> k) & 1) == 1
        mr = jnp.where(bit, pr, 1.0)
        mi = jnp.where(bit, pi, 0.0)
        if re is None:
            re, im = mr, mi
        else:
            re, im = re * mr - im * mi, re * mi + im * mr
    return re, im


def _squarings(pr, pi, n):
    out = [(pr, pi)]
    for _ in range(n - 1):
        pr, pi = pr * pr - pi * pi, 2.0 * pr * pi
        out.append((pr, pi))
    return out


def _shift_lanes(x, n):
    lane = lax.broadcasted_iota(jnp.int32, (S5_H, 128), 1)
    lo, hi = x[:, :128], x[:, 128:]
    if n == 0:
        return x
    if n < 128:
        rlo = pltpu.roll(lo, n, axis=1)
        rhi = pltpu.roll(hi, n, axis=1)
        return jnp.concatenate([jnp.where(lane >= n, rlo, 0.0), jnp.where(lane >= n, rhi, rlo)], axis=1)
    m = n - 128
    rlo = lo if m == 0 else pltpu.roll(lo, m, axis=1)
    return jnp.concatenate([jnp.zeros_like(lo), jnp.where(lane >= m, rlo, 0.0)], axis=1)


def _unshift_lanes(x, n):
    lane = lax.broadcasted_iota(jnp.int32, (S5_H, 128), 1)
    lo, hi = x[:, :128], x[:, 128:]
    if n == 0:
        return x
    if n < 128:
        rlo = pltpu.roll(lo, 128 - n, axis=1)
        rhi = pltpu.roll(hi, 128 - n, axis=1)
        keep = lane < 128 - n
        return jnp.concatenate([jnp.where(keep, rlo, rhi), jnp.where(keep, rhi, 0.0)], axis=1)
    m = n - 128
    rhi = hi if m == 0 else pltpu.roll(hi, 128 - m, axis=1)
    return jnp.concatenate([jnp.where(lane < 128 - m, rhi, 0.0), jnp.zeros_like(lo)], axis=1)


def _s5w_group(gi, bg, lrow_re, lrow_im, crow_re, crow_im, bt_re, bt_im,
               lcol_re, lcol_im, ct_re, ct_im, d_ref, win_ref, wout_ref, mix_ref, l16_ref):
    TH = S5_T * S5_H

    def chunk_pos(idx):
        return (((idx >> H_SHIFT) - bg) & (BLK - 1)) + ((idx >> 7) << 3)

    lr = lrow_re[gi]
    li = lrow_im[gi]
    pows_row = _squarings(lr, li, 5)
    l16_ref[gi, 0:1, :] = pows_row[4][0]
    l16_ref[gi, 1:2, :] = pows_row[4][1]
    l16_ref[gi, 2:8, :] = jnp.zeros((6, 128), F32)
    cr = crow_re[gi]
    ci = crow_im[gi]
    btr = bt_re[gi]
    bti = bt_im[gi]
    bbr = cr * btr - ci * bti
    bbi = cr * bti + ci * btr
    s_idx = chunk_pos(lax.broadcasted_iota(jnp.int32, (TH, 128), 0))
    lane = lax.broadcasted_iota(jnp.int32, (TH, 128), 1)
    jw = jnp.where(lane < S5_P, S5_T - 1 - s_idx, s_idx)
    pr, pi = _cpow(pows_row[:4], jw)
    tbr = jnp.broadcast_to(bbr[None], (S5_T, S5_H, 128)).reshape(TH, 128)
    tbi = jnp.broadcast_to(bbi[None], (S5_T, S5_H, 128)).reshape(TH, 128)
    win_ref[gi, :, 0:128] = (pr * tbr - pi * tbi).astype(BF16)
    win_ref[gi, :, 128:256] = (pr * tbi + pi * tbr).astype(BF16)

    cpows = _squarings(lcol_re[gi], lcol_im[gi], 4)
    row = lax.broadcasted_iota(jnp.int32, (2 * S5_P, TH), 0)
    lane_w = lax.broadcasted_iota(jnp.int32, (2 * S5_P, TH), 1)
    t_idx = chunk_pos(lane_w)
    j_idx = lane_w >> H_SHIFT
    is_f = row < S5_P
    ctr = ct_re[gi]
    cti = ct_im[gi]
    er, ei = _cpow(cpows, jnp.where(is_f, t_idx, S5_T - 1 - t_idx))
    er, ei = er * cpows[0][0] - ei * cpows[0][1], er * cpows[0][1] + ei * cpows[0][0]
    wr = ctr * er - cti * ei
    wi = ctr * ei + cti * er
    wout_ref[gi, 0:128, :] = wr.astype(BF16)
    wout_ref[gi, 128:256, :] = (-wi).astype(BF16)
    kr, ki = _cpow(cpows, jnp.where(is_f, j_idx, S5_T - 1 - j_idx))
    ekr = ctr * kr - cti * ki
    eki = ctr * ki + cti * kr
    lane16 = lax.broadcasted_iota(jnp.int32, (S5_H, 128), 1)
    mf = lane16 < S5_P
    hp = lax.Precision.HIGHEST
    dot = functools.partial(jnp.dot, preferred_element_type=F32, precision=hp)
    kkf = dot(jnp.where(mf, bbr, 0.0), ekr) - dot(jnp.where(mf, bbi, 0.0), eki)
    kkb = dot(jnp.where(mf, 0.0, bbr), ekr) - dot(jnp.where(mf, 0.0, bbi), eki)
    dl = d_ref[gi]
    r16 = lax.broadcasted_iota(jnp.int32, (S5_H, TH), 0)
    l256 = lax.broadcasted_iota(jnp.int32, (S5_H, TH), 1)
    rot = bg * S5_H
    for s in range(S5_T):
        blk = _shift_lanes(kkf, S5_H * s) + _unshift_lanes(kkb, S5_H * (S5_T - 1 - s))
        blk = blk + jnp.where(l256 == r16 + S5_H * s, dl, 0.0)
        blk = jnp.concatenate([pltpu.roll(blk[:, :128], rot, axis=1), pltpu.roll(blk[:, 128:], rot, axis=1)], axis=1)
        rho = ((s + bg) & (BLK - 1)) + (s & BLK)
        mix_ref[gi, pl.ds(pl.multiple_of(rho * S5_H, S5_H), S5_H), :] = blk.astype(BF16)


def _s5w_kernel(*refs):
    for gi in range(S5W_GROUPS):
        bg = (pl.program_id(0) * S5W_GROUPS + gi) & (BLK - 1)
        _s5w_group(gi, bg, *refs)


def _s5_weights(lrow_re, lrow_im, crow_re, crow_im, bt_re, bt_im, lcol_re, lcol_im, ct_re, ct_im, d_row):
    TH = S5_T * S5_H
    g3 = lambda r, c: pl.BlockSpec((S5W_GROUPS, r, c), lambda g: (g, 0, 0))
    wshape = jax.ShapeDtypeStruct((S5_G, TH, TH), BF16)
    return pl.pallas_call(
        _s5w_kernel,
        out_shape=(wshape, wshape, wshape, jax.ShapeDtypeStruct((S5_G, 8, 128), F32)),
        grid=(S5_G // S5W_GROUPS,),
        in_specs=[g3(1, 128)] * 4 + [g3(S5_H, 128)] * 2 + [g3(128, TH)] * 4 + [g3(1, TH)],
        out_specs=(g3(TH, TH), g3(TH, TH), g3(TH, TH), g3(8, 128)),
        compiler_params=_params("arbitrary"),
        name="s5_weights",
    )(lrow_re, lrow_im, crow_re, crow_im, bt_re, bt_im, lcol_re, lcol_im, ct_re, ct_im, d_row)


def _rot_blocks(v, r):
    cols = [pltpu.roll(v[:, 128 * q:128 * (q + 1)], S5_H * r, axis=1) for q in range(v.shape[1] // 128)]
    return jnp.concatenate(cols, axis=1)


def _slabs_of(h, hs_ref):
    h3 = h.reshape(h.shape[0] // S5_T, S5_T, h.shape[1])
    for s in range(S5_T):
        hs_ref[s] = h3[:, s, :].astype(BF16)


def _inproj0n_kernel(x_ref, mod_ref, w_ref, perm_ref, lng_ref, lnb_ref, guz_ref, vln_ref, hs_ref):
    shift = mod_ref[:, 0:D]
    scale = mod_ref[:, D:2 * D]
    hb = (x_ref[...] * (1.0 + scale) + shift).astype(BF16)
    nct = hb.shape[0] // S5_T
    hs = jnp.dot(perm_ref[...], hb, preferred_element_type=F32).astype(BF16)
    for s in range(S5_T):
        hs_ref[s] = hs[nct * s:nct * (s + 1), :]
    dot = lambda lo: jnp.dot(hb, w_ref[:, lo:lo + 512], preferred_element_type=F32)
    guz_ref[...] = (_gelu(dot(0)) * _silu(dot(1024))).astype(BF16)
    vln_ref[...] = _layer_norm(_gelu(dot(512)), lng_ref[...], lnb_ref[...]).astype(BF16)


def _inproj0n(x, mod, w_sgu, ln_g, ln_b, tm=512):
    nct = tm // S5_T
    ri = lax.broadcasted_iota(jnp.int32, (tm, tm), 0)
    ci = lax.broadcasted_iota(jnp.int32, (tm, tm), 1)
    perm = ((ri // nct == ci % S5_T) & (ri % nct == ci // S5_T)).astype(BF16)
    o = jax.ShapeDtypeStruct((B, L, 512), BF16)
    ospec = pl.BlockSpec((None, tm, 512), lambda b, i: (b, i, 0))
    full = lambda *s: pl.BlockSpec(s, lambda b, i: (0,) * len(s))
    return pl.pallas_call(
        _inproj0n_kernel,
        out_shape=(o, o, jax.ShapeDtypeStruct((S5_T, B * N_CHUNK, D), BF16)),
        grid=(B, L // tm),
        in_specs=[pl.BlockSpec((None, tm, D), lambda b, i: (b, i, 0)),
                  pl.BlockSpec((None, 1, 3 * D), lambda b, i: (b, 0, 0)),
                  full(D, EVEN_IN - SGU_COL0), full(tm, tm), full(1, 512), full(1, 512)],
        out_specs=(ospec, ospec,
                   pl.BlockSpec((S5_T, nct, D), lambda b, i: (0, b * (N_CHUNK // nct) + i, 0))),
        compiler_params=_params("arbitrary", "arbitrary"),
        name="inproj0n",
    )(x, mod, w_sgu, perm, ln_g, ln_b)


def _ctx_slabs_kernel(x_ref, mod_ref, hs_ref):
    h = x_ref[...] * (1.0 + mod_ref[:, D:2 * D]) + mod_ref[:, 0:D]
    _slabs_of(h, hs_ref)


def _ctx_slabs(ctx, mod_c):
    return pl.pallas_call(
        _ctx_slabs_kernel,
        out_shape=jax.ShapeDtypeStruct((S5_T, B * N_CCHUNK, D), BF16),
        grid=(B,),
        in_specs=[pl.BlockSpec((None, CTX, D), lambda b: (b, 0, 0)),
                  pl.BlockSpec((1, 3 * D), lambda b: (0, 0))],
        out_specs=pl.BlockSpec((S5_T, N_CCHUNK, D), lambda b: (0, b, 0)),
        compiler_params=_params("arbitrary"),
        name="ctx_slabs",
    )(ctx, mod_c)


def _inproj0a_kernel(hs_ref, hc_ref, w_ref, ua_ref, sza_ref, uc_ref):
    r = pl.program_id(0)
    h = hs_ref[...]
    w_ua = w_ref[:, 0:512].astype(BF16)
    ua_ref[...] = _rot_blocks(jnp.dot(h, w_ua, preferred_element_type=F32), r).astype(BF16)
    sza_ref[...] = _silu(jnp.dot(h, w_ref[:, 512:1024].astype(BF16), preferred_element_type=F32)).astype(BF16)
    uc_ref[...] = _rot_blocks(jnp.dot(hc_ref[...], w_ua, preferred_element_type=F32), r).astype(BF16)


def _inproj0a(hs, hcs, w_in_f32):
    slab = lambda r, h: r + BLK * h
    sspec = lambda n, w: pl.BlockSpec((None, n, w), lambda r, h: (slab(r, h), 0, 0))
    so = lambda n: jax.ShapeDtypeStruct((S5_T, n, 512), BF16)
    nl, ncx = B * N_CHUNK, B * N_CCHUNK
    return pl.pallas_call(
        _inproj0a_kernel,
        out_shape=(so(nl), so(nl), so(ncx)),
        grid=(BLK, S5_T // BLK),
        in_specs=[sspec(nl, D), sspec(ncx, D), pl.BlockSpec((D, SGU_COL0), lambda r, h: (0, 0))],
        out_specs=(sspec(nl, 512), sspec(nl, 512), sspec(ncx, 512)),
        compiler_params=_params("arbitrary", "arbitrary"),
        name="inproj0a",
    )(hs, hcs, w_in_f32)


SCAN_GROUPS = 4


def _scan_tiles(sre_ref, sim_ref, hre_ref, him_ref, n_tiles, carry, lams):
    row = lax.broadcasted_iota(jnp.int32, (8, 128), 0)
    lane = lax.broadcasted_iota(jnp.int32, (8, 128), 1)
    first = row < B
    fwd = lane < S5_P

    def body(k, c):
        of = pl.multiple_of(k * 8, 8)
        ob = pl.multiple_of((n_tiles - 1 - k) * 8, 8)
        out = []
        for gi in range(SCAN_GROUPS):
            lre, lim = lams[gi]
            hr, hi = c[2 * gi], c[2 * gi + 1]
            sr = jnp.where(fwd, sre_ref[gi, pl.ds(of, 8), :], pltpu.roll(sre_ref[gi, pl.ds(ob, 8), :], B, axis=0))
            si = jnp.where(fwd, sim_ref[gi, pl.ds(of, 8), :], pltpu.roll(sim_ref[gi, pl.ds(ob, 8), :], B, axis=0))
            h1r = lre * hr - lim * hi + sr
            h1i = lre * hi + lim * hr + si
            r1r = pltpu.roll(h1r, B, axis=0)
            r1i = pltpu.roll(h1i, B, axis=0)
            if hre_ref is not None:
                hre_ref[gi, pl.ds(of, 8), :] = jnp.where(first, hr, r1r)
                him_ref[gi, pl.ds(of, 8), :] = jnp.where(first, hi, r1i)
            h2r = lre * r1r - lim * r1i + sr
            h2i = lre * r1i + lim * r1r + si
            out.append(jnp.where(first, pltpu.roll(h2r, B, axis=0), h2r))
            out.append(jnp.where(first, pltpu.roll(h2i, B, axis=0), h2i))
        return tuple(out)

    return lax.fori_loop(0, n_tiles, body, carry)


def _unreverse_tiles(h_ref, n_tiles):
    lane = lax.broadcasted_iota(jnp.int32, (8, 128), 1)
    fwd = lane < S5_P

    def body(k, carry):
        of = pl.multiple_of(k * 8, 8)
        ob = pl.multiple_of((n_tiles - 1 - k) * 8, 8)
        for gi in range(SCAN_GROUPS):
            a = h_ref[gi, pl.ds(of, 8), :]
            b = h_ref[gi, pl.ds(ob, 8), :]
            h_ref[gi, pl.ds(of, 8), :] = jnp.where(fwd, a, pltpu.roll(b, B, axis=0))
            h_ref[gi, pl.ds(ob, 8), :] = jnp.where(fwd, b, pltpu.roll(a, B, axis=0))
        return carry

    lax.fori_loop(0, n_tiles // 2, body, 0, unroll=4)


def _gather_group(slab_ref, src):
    halves = []
    for h in range(S5_T // BLK):
        acc = slab_ref[BLK * h]
        for s in range(1, BLK):
            acc = jnp.where(src == s, slab_ref[BLK * h + s], acc)
        halves.append(acc)
    return jnp.concatenate(halves, axis=1)


def _s5core_kernel(ul_ref, uc_ref, win_ref, wout_ref, mix_ref, l16_ref, o_ref,
                   u_ref, sre_ref, sim_ref, cre_ref, cim_ref, hre_ref, him_ref, y_ref):
    nl = N_CHUNK * B
    ncx = N_CCHUNK * B
    blk_l = lax.broadcasted_iota(jnp.int32, (nl, 128), 1) >> H_SHIFT
    blk_c = lax.broadcasted_iota(jnp.int32, (ncx, 128), 1) >> H_SHIFT
    for g0 in range(0, BLK, SCAN_GROUPS):
        for gi in range(SCAN_GROUPS):
            bg = g0 + gi
            win = win_ref[bg]
            src_l = ((blk_l - bg) & (BLK - 1)).astype(F32).astype(BF16)
            src_c = ((blk_c - bg) & (BLK - 1)).astype(F32).astype(BF16)
            u = _gather_group(ul_ref, src_l)
            u_ref[gi] = u
            sl = jnp.dot(u, win, preferred_element_type=F32)
            sc = jnp.dot(_gather_group(uc_ref, src_c), win, preferred_element_type=F32)
            for b in range(B):
                sre_ref[gi, pl.ds(b, N_CHUNK, stride=B), :] = sl[N_CHUNK * b:N_CHUNK * (b + 1), 0:128]
                sim_ref[gi, pl.ds(b, N_CHUNK, stride=B), :] = sl[N_CHUNK * b:N_CHUNK * (b + 1), 128:256]
                cre_ref[gi, pl.ds(b, N_CCHUNK, stride=B), :] = sc[N_CCHUNK * b:N_CCHUNK * (b + 1), 0:128]
                cim_ref[gi, pl.ds(b, N_CCHUNK, stride=B), :] = sc[N_CCHUNK * b:N_CCHUNK * (b + 1), 128:256]
        lams = [(jnp.broadcast_to(l16_ref[g0 + gi, 0:1, :], (8, 128)),
                 jnp.broadcast_to(l16_ref[g0 + gi, 1:2, :], (8, 128))) for gi in range(SCAN_GROUPS)]
        zero = tuple(jnp.zeros((8, 128), F32) for _ in range(2 * SCAN_GROUPS))
        carry = _scan_tiles(cre_ref, cim_ref, None, None, ncx // 8, zero, lams)
        _scan_tiles(sre_ref, sim_ref, hre_ref, him_ref, nl // 8, carry, lams)
        _unreverse_tiles(hre_ref, nl // 8)
        _unreverse_tiles(him_ref, nl // 8)
        for gi in range(SCAN_GROUPS):
            bg = g0 + gi
            y = jnp.dot(u_ref[gi], mix_ref[bg], preferred_element_type=F32)
            hs = []
            for b in range(B):
                hs.append(jnp.concatenate([hre_ref[gi, pl.ds(b, N_CHUNK, stride=B), :],
                                           him_ref[gi, pl.ds(b, N_CHUNK, stride=B), :]], axis=1))
            hcat = jnp.concatenate(hs, axis=0).astype(BF16)
            y = y + jnp.dot(hcat, wout_ref[bg], preferred_element_type=F32)
            y_ref[bg] = y.astype(BF16)

    blk = blk_l.astype(F32).astype(BF16)
    for s in range(S5_T):
        h, r = s // BLK, s % BLK
        acc = None
        for j in range(BLK):
            piece = y_ref[(j - r) % BLK, :, 128 * h:128 * (h + 1)]
            acc = piece if acc is None else jnp.where(blk == j, piece, acc)
        o_ref[s] = acc


def _s5core(ul, uc, win, wout, mix, l16):
    TH = S5_T * S5_H
    nl = N_CHUNK * B
    ncx = N_CCHUNK * B
    g4 = lambda r, c: pl.BlockSpec((BLK, r, c), lambda q: (q, 0, 0))
    col = lambda n: pl.BlockSpec((S5_T, n, 128), lambda q: (0, 0, q))
    f32s = lambda n: pltpu.VMEM((SCAN_GROUPS, n, 128), F32)
    return pl.pallas_call(
        _s5core_kernel,
        out_shape=jax.ShapeDtypeStruct((S5_T, nl, S5_W), BF16),
        grid=(S5_G // BLK,),
        in_specs=[col(nl), col(ncx), g4(TH, TH), g4(TH, TH), g4(TH, TH), g4(8, 128)],
        out_specs=col(nl),
        scratch_shapes=[pltpu.VMEM((SCAN_GROUPS, nl, TH), BF16),
                        f32s(nl), f32s(nl), f32s(ncx), f32s(ncx), f32s(nl), f32s(nl),
                        pltpu.VMEM((BLK, nl, TH), BF16)],
        compiler_params=_params("arbitrary"),
        name="s5core",
    )(ul, uc, win, wout, mix, l16)


def _s5tail_kernel(slat_ref, sza_ref, gluw_ref, glub_ref, wtop_ref, y_ref):
    unrot = (BLK - pl.program_id(0)) & (BLK - 1)
    for b in range(B):
        rows = slice(N_CHUNK * b, N_CHUNK * (b + 1))
        g = _gelu(_rot_blocks(slat_ref[rows, :].astype(F32), unrot))
        gate = _sigmoid(jnp.dot(g.astype(BF16), gluw_ref[...], preferred_element_type=F32) + glub_ref[...])
        a = (g * gate * sza_ref[rows, :].astype(F32)).astype(BF16)
        y_ref[rows, :] = jnp.dot(a, wtop_ref[...], preferred_element_type=F32).astype(BF16)


def _s5tail(slat, sza, glu_w, glu_b, w_top):
    slab = lambda r, h: r + BLK * h
    sspec = lambda w: pl.BlockSpec((None, N_CHUNK * B, w), lambda r, h: (slab(r, h), 0, 0))
    full = lambda *s: pl.BlockSpec(s, lambda r, h: (0,) * len(s))
    return pl.pallas_call(
        _s5tail_kernel,
        out_shape=jax.ShapeDtypeStruct((S5_T, N_CHUNK * B, D), BF16),
        grid=(BLK, S5_T // BLK),
        in_specs=[sspec(512), sspec(512), full(512, 512), full(1, 512), full(S5_W, D)],
        out_specs=sspec(D),
        compiler_params=_params("arbitrary", "arbitrary"),
        name="s5tail",
    )(slat, sza, glu_w, glu_b, w_top)


PERM_ROWS = S5_T * S5_T


def _tail0_kernel(x_ref, ys5_ref, guz_ref, vln_ref, mod_ref, sguw_ref, sgub_ref, wbot_ref, ng_ref, nb_ref, o_ref):
    tm = x_ref.shape[0]
    lane = lax.broadcasted_iota(jnp.int32, (SGU_CHUNK, 128), 1)
    lo = lane < SGU_HD
    zero = jnp.zeros((SGU_CHUNK, 128), BF16)
    chunks = []
    for ci in range(tm // SGU_CHUNK):
        v = vln_ref[ci * SGU_CHUNK:(ci + 1) * SGU_CHUNK, :]
        cols = []
        for pi in range(SGU_HEADS // 2):
            vp = v[:, 128 * pi:128 * (pi + 1)]
            bm = jnp.concatenate([jnp.where(lo, vp, zero), jnp.where(lo, zero, vp)], axis=0)
            cols.append(jnp.dot(sguw_ref[pi], bm, preferred_element_type=F32))
        chunks.append(jnp.concatenate(cols, axis=1) + sgub_ref[...])
    s = jnp.concatenate(chunks, axis=0)
    bsg = (guz_ref[...].astype(F32) * s).astype(BF16)
    ri = lax.broadcasted_iota(jnp.int32, (PERM_ROWS, PERM_ROWS), 0)
    ci = lax.broadcasted_iota(jnp.int32, (PERM_ROWS, PERM_ROWS), 1)
    perm = jnp.where(((ri >> H_SHIFT) == (ci & (S5_T - 1))) & ((ri & (S5_T - 1)) == (ci >> H_SHIFT)), 1.0, 0.0)
    perm = perm.astype(BF16)
    ys5 = jnp.concatenate(
        [jnp.dot(perm, ys5_ref[:, S5_T * j:S5_T * (j + 1), :].reshape(PERM_ROWS, D), preferred_element_type=F32)
         for j in range(tm // PERM_ROWS)], axis=0)
    y = ys5 + jnp.dot(bsg, wbot_ref[...], preferred_element_type=F32)
    gmod = mod_ref[:, 2 * D:3 * D]
    o_ref[...] = _layer_norm(DN_ALPHA * x_ref[...] + gmod * y, ng_ref[...], nb_ref[...])


def _tail0(x, ys5, guz, vln, mod, sguw, sgub, w_bot, ng, nb, tm=512):
    nct = tm // S5_T
    t512 = pl.BlockSpec((None, tm, 512), lambda b, i: (b, i, 0))
    tD = pl.BlockSpec((None, tm, D), lambda b, i: (b, i, 0))
    full = lambda *s: pl.BlockSpec(s, lambda b, i: (0,) * len(s))
    return pl.pallas_call(
        _tail0_kernel,
        out_shape=jax.ShapeDtypeStruct((B, L, D), F32),
        grid=(B, L // tm),
        in_specs=[tD, pl.BlockSpec((S5_T, nct, D), lambda b, i: (0, b * (N_CHUNK // nct) + i, 0)), t512, t512,
                  pl.BlockSpec((None, 1, 3 * D), lambda b, i: (b, 0, 0)),
                  full(SGU_HEADS // 2, SGU_CHUNK, 256), full(SGU_CHUNK, 512),
                  pl.BlockSpec((SGU_W, D), lambda b, i: (1, 0)), full(1, D), full(1, D)],
        out_specs=tD,
        compiler_params=_params("arbitrary", "arbitrary"),
        name="tail0",
    )(x, ys5, guz, vln, mod, sguw, sgub, w_bot, ng, nb)


def _inproj1_kernel(x_ref, mod_ref, w_ref, hg_ref, sz_ref):
    shift = mod_ref[:, 0:D]
    scale = mod_ref[:, D:2 * D]
    h = (x_ref[...] * (1.0 + scale) + shift).astype(BF16)
    dot = lambda lo: jnp.dot(h, w_ref[:, lo:lo + D].astype(BF16), preferred_element_type=F32)
    hg_ref[...] = (dot(0) * _sigmoid(dot(D))).astype(BF16)
    sz_ref[...] = _silu(dot(2 * D)).astype(BF16)


def _inproj1(x, mod, w_in_f32, tm=512):
    o = jax.ShapeDtypeStruct((B, L, D), BF16)
    ospec = pl.BlockSpec((None, tm, D), lambda b, i: (b, i, 0))
    return pl.pallas_call(
        _inproj1_kernel,
        out_shape=(o, o),
        grid=(B, L // tm),
        in_specs=[pl.BlockSpec((None, tm, D), lambda b, i: (b, i, 0)),
                  pl.BlockSpec((None, 1, 3 * D), lambda b, i: (b, 0, 0)),
                  pl.BlockSpec((D, ODD_IN), lambda b, i: (0, 0))],
        out_specs=(ospec, ospec),
        compiler_params=_params("arbitrary", "arbitrary"),
        name="inproj1",
    )(x, mod, w_in_f32)


ROW_PAD = 16
COL_PAD = CONV_HALF * GRID_W
CONV_ROWS = 128


def _conv_kernel(h_ref, w_ref, b_ref, o_ref, prow_ref, pcol_ref):
    j = pl.program_id(1)
    bias = b_ref[...]

    @pl.when(j < (D // 2) // 128)
    def _():
        zpad = jnp.zeros((GRID_W, ROW_PAD, 128), F32)
        prow_ref[:, 0:ROW_PAD, :] = zpad
        prow_ref[:, ROW_PAD + GRID_W:, :] = zpad
        prow_ref[:, ROW_PAD:ROW_PAD + GRID_W, :] = h_ref[...].astype(F32).reshape(GRID_W, GRID_W, 128)

        def body(r, carry):
            acc = jnp.zeros((GRID_W, 128), F32) + bias
            for k in range(CONV_K):
                off = ROW_PAD - CONV_HALF + k
                acc = acc + w_ref[k:k + 1, :] * prow_ref[r, off:off + GRID_W, :]
            o_ref[pl.ds(pl.multiple_of(r * GRID_W, GRID_W), GRID_W), :] = acc.astype(BF16)
            return carry

        lax.fori_loop(0, GRID_W, body, 0, unroll=2)

    @pl.when(j >= (D // 2) // 128)
    def _():
        zpad = jnp.zeros((COL_PAD, 128), F32)
        pcol_ref[0:COL_PAD, :] = zpad
        pcol_ref[COL_PAD + L:, :] = zpad
        pcol_ref[COL_PAD:COL_PAD + L, :] = h_ref[...].astype(F32)

        def body(i, carry):
            base = pl.multiple_of(i * CONV_ROWS, CONV_ROWS)
            acc = jnp.zeros((CONV_ROWS, 128), F32) + bias
            for k in range(CONV_K):
                acc = acc + w_ref[k:k + 1, :] * pcol_ref[pl.ds(base + k * GRID_W, CONV_ROWS), :]
            o_ref[pl.ds(base, CONV_ROWS), :] = acc.astype(BF16)
            return carry

        lax.fori_loop(0, L // CONV_ROWS, body, 0)


def _conv(hg, dw_w, dw_b):
    return pl.pallas_call(
        _conv_kernel,
        out_shape=jax.ShapeDtypeStruct((B, L, D), BF16),
        grid=(B, D // 128),
        in_specs=[pl.BlockSpec((None, L, 128), lambda b, j: (b, 0, j)),
                  pl.BlockSpec((CONV_K, 128), lambda b, j: (0, j)),
                  pl.BlockSpec((1, 128), lambda b, j: (0, j))],
        out_specs=pl.BlockSpec((None, L, 128), lambda b, j: (b, 0, j)),
        scratch_shapes=[pltpu.VMEM((GRID_W, GRID_W + 2 * ROW_PAD, 128), F32),
                        pltpu.VMEM((L + 2 * COL_PAD, 128), F32)],
        compiler_params=_params("arbitrary", "arbitrary"),
        name="dwconv",
    )(hg, dw_w, dw_b)


def _tail1_kernel(x_ref, hc_ref, sz_ref, mod_ref, lng_ref, lnb_ref, wout_ref, ng_ref, nb_ref, o_ref):
    m = _silu(_layer_norm(hc_ref[...].astype(F32), lng_ref[...], lnb_ref[...])) * sz_ref[...].astype(F32)
    y = jnp.dot(m.astype(BF16), wout_ref[...], preferred_element_type=F32)
    gmod = mod_ref[:, 2 * D:3 * D]
    o_ref[...] = _layer_norm(DN_ALPHA * x_ref[...] + gmod * y, ng_ref[...], nb_ref[...])


def _tail1(x, hc, sz, mod, ln_g, ln_b, w_out, ng, nb, tm=512):
    tD = pl.BlockSpec((None, tm, D), lambda b, i: (b, i, 0))
    full = lambda *s: pl.BlockSpec(s, lambda b, i: (0,) * len(s))
    return pl.pallas_call(
        _tail1_kernel,
        out_shape=jax.ShapeDtypeStruct((B, L, D), F32),
        grid=(B, L // tm),
        in_specs=[tD, tD, tD, pl.BlockSpec((None, 1, 3 * D), lambda b, i: (b, 0, 0)),
                  full(1, D), full(1, D), full(D, D), full(1, D), full(1, D)],
        out_specs=tD,
        compiler_params=_params("arbitrary", "arbitrary"),
        name="tail1",
    )(x, hc, sz, mod, ln_g, ln_b, w_out, ng, nb)


def kernel(x, c, ctx, c_ctx, mod_w, mod_b, norm_g, norm_b, ev_w_in, ev_w_out, s5_lam_re, s5_lam_im, s5_log_dt, s5_b_re, s5_b_im, s5_c_re, s5_c_im, s5_d, glu_w, glu_b, sgu_ln_g, sgu_ln_b, sgu_w, sgu_b, od_w_in, od_w_out, dw_w, dw_b, conv_ln_g, conv_ln_b):
    TH = S5_T * S5_H
    row = lambda v: v.reshape(1, -1)

    cond8 = jnp.concatenate([c, c_ctx[None], jnp.zeros((3, D), F32)], axis=0)
    mods = _adaln(cond8, mod_w, mod_b)
    mod0 = mods[0, :B].reshape(B, 1, 3 * D)
    mod0c = mods[0, B:B + 1]
    mod1 = mods[1, :B].reshape(B, 1, 3 * D)

    lbr, lbi, cfr, cfi = _discretise(s5_lam_re[0], s5_lam_im[0], s5_log_dt[0])
    rowcat = lambda a: jnp.concatenate([a[0], a[1]], axis=-1).reshape(S5_G, 1, 2 * S5_P)
    colcat = lambda a: jnp.broadcast_to(jnp.concatenate([a[0], a[1]], axis=-1)[:, :, None], (S5_G, 2 * S5_P, TH))
    bt = lambda a: jnp.concatenate([jnp.swapaxes(a[0], 1, 2), jnp.swapaxes(a[1], 1, 2)], axis=-1)
    ct = lambda a: jnp.tile(jnp.concatenate([jnp.swapaxes(a[0], 1, 2), jnp.swapaxes(a[1], 1, 2)], axis=1), (1, 1, S5_T))
    d_row = jnp.tile(s5_d[0].reshape(S5_G, 1, S5_H), (1, 1, S5_T))
    win, wout, mix, l16 = _s5_weights(rowcat(lbr), rowcat(lbi), rowcat(cfr), rowcat(cfi),
                                      bt(s5_b_re[0]), bt(s5_b_im[0]), colcat(lbr), colcat(lbi),
                                      ct(s5_c_re[0]), ct(s5_c_im[0]), d_row)

    w_sgu = ev_w_in[0][:, SGU_COL0:].astype(BF16)
    w_out0 = ev_w_out[0].astype(BF16)
    glu_w0 = glu_w[0].astype(BF16)
    guz, vln, hs = _inproj0n(x, mod0, w_sgu, row(sgu_ln_g[0]), row(sgu_ln_b[0]))
    ua, sza, ua_c = _inproj0a(hs, _ctx_slabs(ctx, mod0c), ev_w_in[0])
    s_lat = _s5core(ua, ua_c, win, wout, mix, l16)
    y_s5 = _s5tail(s_lat, sza, glu_w0, row(glu_b[0]), w_out0)
    sguw = sgu_w[0].reshape(SGU_HEADS // 2, 2, SGU_CHUNK, SGU_CHUNK)
    sguw = jnp.transpose(sguw, (0, 2, 1, 3)).reshape(SGU_HEADS // 2, SGU_CHUNK, 2 * SGU_CHUNK).astype(BF16)
    sgub = jnp.repeat(sgu_b[0].T, SGU_HD, axis=1)
    x1 = _tail0(x, y_s5, guz, vln, mod0, sguw, sgub, w_out0, row(norm_g[0]), row(norm_b[0]))

    hg, sz = _inproj1(x1, mod1, od_w_in[0])
    hc = _conv(hg, dw_w[0], row(dw_b[0]))
    return _tail1(x1, hc, sz, mod1, row(conv_ln_g[0]), row(conv_ln_b[0]), od_w_out[0].astype(BF16),
                  row(norm_g[1]), row(norm_b[1]))
```

```python
import functools
import math

import jax
import jax.numpy as jnp
from jax import lax
from jax.experimental import pallas as pl
from jax.experimental.pallas import tpu as pltpu

D = 1024
B = 4
L = 4096
CTX = 256
GRID_W = 64
S5_W = 512
S5_G = 32
S5_H = 16
H_SHIFT = 4
BLK = 128 // S5_H
S5_P = 64
S5_T = 16
SGU_W = 512
SGU_HEADS = 8
SGU_HD = 64
SGU_CHUNK = 128
CONV_K = 31
CONV_HALF = CONV_K // 2
EVEN_IN = 2560
SGU_COL0 = 2 * S5_W
ODD_IN = 3072
DEPTH = 2
DN_ALPHA = (2 * DEPTH) ** 0.25
LN_EPS = 1e-5
N_CHUNK = L // S5_T
N_CCHUNK = CTX // S5_T
VMEM_LIMIT_V7X = 56 * 1024 * 1024

F32 = jnp.float32
BF16 = jnp.bfloat16


GELU_C = math.sqrt(2.0 / math.pi)


def _gelu(x):
    hx = 0.5 * x
    return hx * jnp.tanh(x * ((x * x) * (0.044715 * GELU_C) + GELU_C)) + hx


def _sigmoid(x):
    return 0.5 * jnp.tanh(0.5 * x) + 0.5


def _silu(x):
    hx = 0.5 * x
    return hx * jnp.tanh(hx) + hx


def _layer_norm(x, g, b):
    mu = jnp.mean(x, axis=-1, keepdims=True)
    xc = x - mu
    var = jnp.mean(xc * xc, axis=-1, keepdims=True)
    return xc * lax.rsqrt(var + LN_EPS) * g + b


def _params(*sem):
    return pltpu.CompilerParams(dimension_semantics=sem, vmem_limit_bytes=VMEM_LIMIT_V7X)


def _adaln_kernel(c_ref, w_ref, b_ref, o_ref):
    def split(v):
        hi = v.astype(BF16)
        return hi, (v - hi.astype(F32)).astype(BF16)

    a_hi, a_lo = split(_silu(c_ref[...]))
    w_hi, w_lo = split(w_ref[...])
    dot = functools.partial(jnp.dot, preferred_element_type=F32)
    o_ref[...] = dot(a_hi, w_hi) + dot(a_lo, w_hi) + dot(a_hi, w_lo) + b_ref[...]


def _adaln(cond8, mod_w, mod_b):
    tn = 512
    return pl.pallas_call(
        _adaln_kernel,
        out_shape=jax.ShapeDtypeStruct((DEPTH, 8, 3 * D), F32),
        grid=(DEPTH, 3 * D // tn),
        in_specs=[pl.BlockSpec((8, D), lambda l, j: (0, 0)),
                  pl.BlockSpec((None, D, tn), lambda l, j: (l, 0, j)),
                  pl.BlockSpec((None, 1, tn), lambda l, j: (l, 0, j))],
        out_specs=pl.BlockSpec((None, 8, tn), lambda l, j: (l, 0, j)),
        compiler_params=_params("arbitrary", "arbitrary"),
        name="adaln",
    )(cond8, mod_w, mod_b.reshape(DEPTH, 1, 3 * D))


def _disc_kernel(lr_ref, li_ref, ldt_ref, obr_ref, obi_ref, ocr_ref, oci_ref):
    lr = lr_ref[...]
    li = li_ref[...]
    dt = jnp.exp(ldt_ref[...])
    mag = jnp.exp(lr * dt)
    br = mag * jnp.cos(li * dt)
    bi = mag * jnp.sin(li * dt)
    inv = 1.0 / (lr * lr + li * li)
    nr = br - 1.0
    obr_ref[...] = br
    obi_ref[...] = bi
    ocr_ref[...] = (nr * lr + bi * li) * inv
    oci_ref[...] = (bi * lr - nr * li) * inv


def _discretise(lam_re, lam_im, log_dt):
    shp = jax.ShapeDtypeStruct((2 * S5_G, S5_P), F32)
    ldt = jnp.broadcast_to(log_dt.reshape(2 * S5_G, 1), (2 * S5_G, S5_P))
    outs = pl.pallas_call(
        _disc_kernel, out_shape=(shp, shp, shp, shp), name="s5_discretise",
    )(lam_re.reshape(2 * S5_G, S5_P), lam_im.reshape(2 * S5_G, S5_P), ldt)
    return [o.reshape(2, S5_G, S5_P) for o in outs]


S5W_GROUPS = 4


def _cpow(base_pows, j):
    re = None
    im = None
    for k, (pr, pi) in enumerate(base_pows):
        bit = ((j >> k) & 1) == 1
        mr = jnp.where(bit, pr, 1.0)
        mi = jnp.where(bit, pi, 0.0)
        if re is None:
            re, im = mr, mi
        else:
            re, im = re * mr - im * mi, re * mi + im * mr
    return re, im


def _squarings(pr, pi, n):
    out = [(pr, pi)]
    for _ in range(n - 1):
        pr, pi = pr * pr - pi * pi, 2.0 * pr * pi
        out.append((pr, pi))
    return out


def _shift_lanes(x, n):
    lane = lax.broadcasted_iota(jnp.int32, (S5_H, 128), 1)
    lo, hi = x[:, :128], x[:, 128:]
    if n == 0:
        return x
    if n < 128:
        rlo = pltpu.roll(lo, n, axis=1)
        rhi = pltpu.roll(hi, n, axis=1)
        return jnp.concatenate([jnp.where(lane >= n, rlo, 0.0), jnp.where(lane >= n, rhi, rlo)], axis=1)
    m = n - 128
    rlo = lo if m == 0 else pltpu.roll(lo, m, axis=1)
    return jnp.concatenate([jnp.zeros_like(lo), jnp.where(lane >= m, rlo, 0.0)], axis=1)


def _unshift_lanes(x, n):
    lane = lax.broadcasted_iota(jnp.int32, (S5_H, 128), 1)
    lo, hi = x[:, :128], x[:, 128:]
    if n == 0:
        return x
    if n < 128:
        rlo = pltpu.roll(lo, 128 - n, axis=1)
        rhi = pltpu.roll(hi, 128 - n, axis=1)
        keep = lane < 128 - n
        return jnp.concatenate([jnp.where(keep, rlo, rhi), jnp.where(keep, rhi, 0.0)], axis=1)
    m = n - 128
    rhi = hi if m == 0 else pltpu.roll(hi, 128 - m, axis=1)
    return jnp.concatenate([jnp.where(lane < 128 - m, rhi, 0.0), jnp.zeros_like(lo)], axis=1)


def _s5w_group(gi, bg, lrow_re, lrow_im, crow_re, crow_im, bt_re, bt_im,
               lcol_re, lcol_im, ct_re, ct_im, d_ref, win_ref, wout_ref, mix_ref, l16_ref):
    TH = S5_T * S5_H

    def chunk_pos(idx):
        return (((idx >> H_SHIFT) - bg) & (BLK - 1)) + ((idx >> 7) << 3)

    lr = lrow_re[gi]
    li = lrow_im[gi]
    pows_row = _squarings(lr, li, 5)
    l16_ref[gi, 0:1, :] = pows_row[4][0]
    l16_ref[gi, 1:2, :] = pows_row[4][1]
    l16_ref[gi, 2:8, :] = jnp.zeros((6, 128), F32)
    cr = crow_re[gi]
    ci = crow_im[gi]
    btr = bt_re[gi]
    bti = bt_im[gi]
    bbr = cr * btr - ci * bti
    bbi = cr * bti + ci * btr
    s_idx = chunk_pos(lax.broadcasted_iota(jnp.int32, (TH, 128), 0))
    lane = lax.broadcasted_iota(jnp.int32, (TH, 128), 1)
    jw = jnp.where(lane < S5_P, S5_T - 1 - s_idx, s_idx)
    pr, pi = _cpow(pows_row[:4], jw)
    tbr = jnp.broadcast_to(bbr[None], (S5_T, S5_H, 128)).reshape(TH, 128)
    tbi = jnp.broadcast_to(bbi[None], (S5_T, S5_H, 128)).reshape(TH, 128)
    win_ref[gi, :, 0:128] = (pr * tbr - pi * tbi).astype(BF16)
    win_ref[gi, :, 128:256] = (pr * tbi + pi * tbr).astype(BF16)

    cpows = _squarings(lcol_re[gi], lcol_im[gi], 4)
    row = lax.broadcasted_iota(jnp.int32, (2 * S5_P, TH), 0)
    lane_w = lax.broadcasted_iota(jnp.int32, (2 * S5_P, TH), 1)
    t_idx = chunk_pos(lane_w)
    j_idx = lane_w >> H_SHIFT
    is_f = row < S5_P
    ctr = ct_re[gi]
    cti = ct_im[gi]
    er, ei = _cpow(cpows, jnp.where(is_f, t_idx, S5_T - 1 - t_idx))
    er, ei = er * cpows[0][0] - ei * cpows[0][1], er * cpows[0][1] + ei * cpows[0][0]
    wr = ctr * er - cti * ei
    wi = ctr * ei + cti * er
    wout_ref[gi, 0:128, :] = wr.astype(BF16)
    wout_ref[gi, 128:256, :] = (-wi).astype(BF16)
    kr, ki = _cpow(cpows, jnp.where(is_f, j_idx, S5_T - 1 - j_idx))
    ekr = ctr * kr - cti * ki
    eki = ctr * ki + cti * kr
    lane16 = lax.broadcasted_iota(jnp.int32, (S5_H, 128), 1)
    mf = lane16 < S5_P
    hp = lax.Precision.HIGHEST
    dot = functools.partial(jnp.dot, preferred_element_type=F32, precision=hp)
    kkf = dot(jnp.where(mf, bbr, 0.0), ekr) - dot(jnp.where(mf, bbi, 0.0), eki)
    kkb = dot(jnp.where(mf, 0.0, bbr), ekr) - dot(jnp.where(mf, 0.0, bbi), eki)
    dl = d_ref[gi]
    r16 = lax.broadcasted_iota(jnp.int32, (S5_H, TH), 0)
    l256 = lax.broadcasted_iota(jnp.int32, (S5_H, TH), 1)
    rot = bg * S5_H
    for s in range(S5_T):
        blk = _shift_lanes(kkf, S5_H * s) + _unshift_lanes(kkb, S5_H * (S5_T - 1 - s))
        blk = blk + jnp.where(l256 == r16 + S5_H * s, dl, 0.0)
        blk = jnp.concatenate([pltpu.roll(blk[:, :128], rot, axis=1), pltpu.roll(blk[:, 128:], rot, axis=1)], axis=1)
        rho = ((s + bg) & (BLK - 1)) + (s & BLK)
        mix_ref[gi, pl.ds(pl.multiple_of(rho * S5_H, S5_H), S5_H), :] = blk.astype(BF16)


def _s5w_kernel(*refs):
    for gi in range(S5W_GROUPS):
        bg = (pl.program_id(0) * S5W_GROUPS + gi) & (BLK - 1)
        _s5w_group(gi, bg, *refs)


def _s5_weights(lrow_re, lrow_im, crow_re, crow_im, bt_re, bt_im, lcol_re, lcol_im, ct_re, ct_im, d_row):
    TH = S5_T * S5_H
    g3 = lambda r, c: pl.BlockSpec((S5W_GROUPS, r, c), lambda g: (g, 0, 0))
    wshape = jax.ShapeDtypeStruct((S5_G, TH, TH), BF16)
    return pl.pallas_call(
        _s5w_kernel,
        out_shape=(wshape, wshape, wshape, jax.ShapeDtypeStruct((S5_G, 8, 128), F32)),
        grid=(S5_G // S5W_GROUPS,),
        in_specs=[g3(1, 128)] * 4 + [g3(S5_H, 128)] * 2 + [g3(128, TH)] * 4 + [g3(1, TH)],
        out_specs=(g3(TH, TH), g3(TH, TH), g3(TH, TH), g3(8, 128)),
        compiler_params=_params("arbitrary"),
        name="s5_weights",
    )(lrow_re, lrow_im, crow_re, crow_im, bt_re, bt_im, lcol_re, lcol_im, ct_re, ct_im, d_row)


def _rot_blocks(v, r):
    cols = [pltpu.roll(v[:, 128 * q:128 * (q + 1)], S5_H * r, axis=1) for q in range(v.shape[1] // 128)]
    return jnp.concatenate(cols, axis=1)


def _slabs_of(h, hs_ref):
    h3 = h.reshape(h.shape[0] // S5_T, S5_T, h.shape[1])
    for s in range(S5_T):
        hs_ref[s] = h3[:, s, :].astype(BF16)


def _inproj0n_kernel(x_ref, mod_ref, w_ref, perm_ref, lng_ref, lnb_ref, guz_ref, vln_ref, hs_ref):
    shift = mod_ref[:, 0:D]
    scale = mod_ref[:, D:2 * D]
    hb = (x_ref[...] * (1.0 + scale) + shift).astype(BF16)
    nct = hb.shape[0] // S5_T
    hs = jnp.dot(perm_ref[...], hb, preferred_element_type=F32).astype(BF16)
    for s in range(S5_T):
        hs_ref[s] = hs[nct * s:nct * (s + 1), :]
    dot = lambda lo: jnp.dot(hb, w_ref[:, lo:lo + 512], preferred_element_type=F32)
    guz_ref[...] = (_gelu(dot(0)) * _silu(dot(1024))).astype(BF16)
    vln_ref[...] = _layer_norm(_gelu(dot(512)), lng_ref[...], lnb_ref[...]).astype(BF16)


def _inproj0n(x, mod, w_sgu, ln_g, ln_b, tm=512):
    nct = tm // S5_T
    ri = lax.broadcasted_iota(jnp.int32, (tm, tm), 0)
    ci = lax.broadcasted_iota(jnp.int32, (tm, tm), 1)
    perm = ((ri // nct == ci % S5_T) & (ri % nct == ci // S5_T)).astype(BF16)
    o = jax.ShapeDtypeStruct((B, L, 512), BF16)
    ospec = pl.BlockSpec((None, tm, 512), lambda b, i: (b, i, 0))
    full = lambda *s: pl.BlockSpec(s, lambda b, i: (0,) * len(s))
    return pl.pallas_call(
        _inproj0n_kernel,
        out_shape=(o, o, jax.ShapeDtypeStruct((S5_T, B * N_CHUNK, D), BF16)),
        grid=(B, L // tm),
        in_specs=[pl.BlockSpec((None, tm, D), lambda b, i: (b, i, 0)),
                  pl.BlockSpec((None, 1, 3 * D), lambda b, i: (b, 0, 0)),
                  full(D, EVEN_IN - SGU_COL0), full(tm, tm), full(1, 512), full(1, 512)],
        out_specs=(ospec, ospec,
                   pl.BlockSpec((S5_T, nct, D), lambda b, i: (0, b * (N_CHUNK // nct) + i, 0))),
        compiler_params=_params("arbitrary", "arbitrary"),
        name="inproj0n",
    )(x, mod, w_sgu, perm, ln_g, ln_b)


def _ctx_slabs_kernel(x_ref, mod_ref, hs_ref):
    h = x_ref[...] * (1.0 + mod_ref[:, D:2 * D]) + mod_ref[:, 0:D]
    _slabs_of(h, hs_ref)


def _ctx_slabs(ctx, mod_c):
    return pl.pallas_call(
        _ctx_slabs_kernel,
        out_shape=jax.ShapeDtypeStruct((S5_T, B * N_CCHUNK, D), BF16),
        grid=(B,),
        in_specs=[pl.BlockSpec((None, CTX, D), lambda b: (b, 0, 0)),
                  pl.BlockSpec((1, 3 * D), lambda b: (0, 0))],
        out_specs=pl.BlockSpec((S5_T, N_CCHUNK, D), lambda b: (0, b, 0)),
        compiler_params=_params("arbitrary"),
        name="ctx_slabs",
    )(ctx, mod_c)


def _inproj0a_kernel(hs_ref, hc_ref, w_ref, ua_ref, sza_ref, uc_ref):
    r = pl.program_id(0)
    h = hs_ref[...]
    w_ua = w_ref[:, 0:512].astype(BF16)
    ua_ref[...] = _rot_blocks(jnp.dot(h, w_ua, preferred_element_type=F32), r).astype(BF16)
    sza_ref[...] = _silu(jnp.dot(h, w_ref[:, 512:1024].astype(BF16), preferred_element_type=F32)).astype(BF16)
    uc_ref[...] = _rot_blocks(jnp.dot(hc_ref[...], w_ua, preferred_element_type=F32), r).astype(BF16)


def _inproj0a(hs, hcs, w_in_f32):
    slab = lambda r, h: r + BLK * h
    sspec = lambda n, w: pl.BlockSpec((None, n, w), lambda r, h: (slab(r, h), 0, 0))
    so = lambda n: jax.ShapeDtypeStruct((S5_T, n, 512), BF16)
    nl, ncx = B * N_CHUNK, B * N_CCHUNK
    return pl.pallas_call(
        _inproj0a_kernel,
        out_shape=(so(nl), so(nl), so(ncx)),
        grid=(BLK, S5_T // BLK),
        in_specs=[sspec(nl, D), sspec(ncx, D), pl.BlockSpec((D, SGU_COL0), lambda r, h: (0, 0))],
        out_specs=(sspec(nl, 512), sspec(nl, 512), sspec(ncx, 512)),
        compiler_params=_params("arbitrary", "arbitrary"),
        name="inproj0a",
    )(hs, hcs, w_in_f32)


SCAN_GROUPS = 4


def _scan_tiles(sre_ref, sim_ref, h_refs, n_tiles, carry, lams):
    row = lax.broadcasted_iota(jnp.int32, (8, 128), 0)
    lane = lax.broadcasted_iota(jnp.int32, (8, 128), 1)
    first = row < B
    fwd = lane < S5_P

    def body(k, c):
        of = pl.multiple_of(k * 8, 8)
        ob = pl.multiple_of((n_tiles - 1 - k) * 8, 8)
        out = []
        for gi in range(SCAN_GROUPS):
            lre, lim = lams[gi]
            hr, hi = c[2 * gi], c[2 * gi + 1]
            sr = jnp.where(fwd, sre_ref[gi, pl.ds(of, 8), :], pltpu.roll(sre_ref[gi, pl.ds(ob, 8), :], B, axis=0))
            si = jnp.where(fwd, sim_ref[gi, pl.ds(of, 8), :], pltpu.roll(sim_ref[gi, pl.ds(ob, 8), :], B, axis=0))
            h1r = lre * hr - lim * hi + sr
            h1i = lre * hi + lim * hr + si
            r1r = pltpu.roll(h1r, B, axis=0)
            r1i = pltpu.roll(h1i, B, axis=0)
            if h_refs is not None:
                fre_ref, fim_ref, bre_ref, bim_ref = h_refs
                er = jnp.where(first, hr, r1r)
                ei = jnp.where(first, hi, r1i)
                fre_ref[gi, pl.ds(of, 8), :] = er
                fim_ref[gi, pl.ds(of, 8), :] = ei
                bre_ref[gi, pl.ds(ob, 8), :] = pltpu.roll(er, B, axis=0)
                bim_ref[gi, pl.ds(ob, 8), :] = pltpu.roll(ei, B, axis=0)
            h2r = lre * r1r - lim * r1i + sr
            h2i = lre * r1i + lim * r1r + si
            out.append(jnp.where(first, pltpu.roll(h2r, B, axis=0), h2r))
            out.append(jnp.where(first, pltpu.roll(h2i, B, axis=0), h2i))
        return tuple(out)

    return lax.fori_loop(0, n_tiles, body, carry)


def _gather_group(slab_ref, src):
    halves = []
    for h in range(S5_T // BLK):
        acc = slab_ref[BLK * h]
        for s in range(1, BLK):
            acc = jnp.where(src == s, slab_ref[BLK * h + s], acc)
        halves.append(acc)
    return jnp.concatenate(halves, axis=1)


def _s5core_kernel(ul_ref, uc_ref, win_ref, wout_ref, mix_ref, l16_ref, o_ref,
                   u_ref, sre_ref, sim_ref, cre_ref, cim_ref, fre_ref, fim_ref, bre_ref, bim_ref, y_ref):
    nl = N_CHUNK * B
    ncx = N_CCHUNK * B
    blk_l = lax.broadcasted_iota(jnp.int32, (nl, 128), 1) >> H_SHIFT
    blk_c = lax.broadcasted_iota(jnp.int32, (ncx, 128), 1) >> H_SHIFT
    fwd = lax.broadcasted_iota(jnp.int32, (N_CHUNK, 128), 1) < S5_P
    for g0 in range(0, BLK, SCAN_GROUPS):
        for gi in range(SCAN_GROUPS):
            bg = g0 + gi
            win = win_ref[bg]
            src_l = ((blk_l - bg) & (BLK - 1)).astype(F32).astype(BF16)
            src_c = ((blk_c - bg) & (BLK - 1)).astype(F32).astype(BF16)
            u = _gather_group(ul_ref, src_l)
            u_ref[gi] = u
            sl = jnp.dot(u, win, preferred_element_type=F32)
            sc = jnp.dot(_gather_group(uc_ref, src_c), win, preferred_element_type=F32)
            for b in range(B):
                sre_ref[gi, pl.ds(b, N_CHUNK, stride=B), :] = sl[N_CHUNK * b:N_CHUNK * (b + 1), 0:128]
                sim_ref[gi, pl.ds(b, N_CHUNK, stride=B), :] = sl[N_CHUNK * b:N_CHUNK * (b + 1), 128:256]
                cre_ref[gi, pl.ds(b, N_CCHUNK, stride=B), :] = sc[N_CCHUNK * b:N_CCHUNK * (b + 1), 0:128]
                cim_ref[gi, pl.ds(b, N_CCHUNK, stride=B), :] = sc[N_CCHUNK * b:N_CCHUNK * (b + 1), 128:256]
        lams = [(jnp.broadcast_to(l16_ref[g0 + gi, 0:1, :], (8, 128)),
                 jnp.broadcast_to(l16_ref[g0 + gi, 1:2, :], (8, 128))) for gi in range(SCAN_GROUPS)]
        zero = tuple(jnp.zeros((8, 128), F32) for _ in range(2 * SCAN_GROUPS))
        carry = _scan_tiles(cre_ref, cim_ref, None, ncx // 8, zero, lams)
        _scan_tiles(sre_ref, sim_ref, (fre_ref, fim_ref, bre_ref, bim_ref), nl // 8, carry, lams)
        for gi in range(SCAN_GROUPS):
            bg = g0 + gi
            y = jnp.dot(u_ref[gi], mix_ref[bg], preferred_element_type=F32)
            hs = []
            for b in range(B):
                rows = pl.ds(b, N_CHUNK, stride=B)
                hs.append(jnp.concatenate([jnp.where(fwd, fre_ref[gi, rows, :], bre_ref[gi, rows, :]),
                                           jnp.where(fwd, fim_ref[gi, rows, :], bim_ref[gi, rows, :])], axis=1))
            hcat = jnp.concatenate(hs, axis=0).astype(BF16)
            y = y + jnp.dot(hcat, wout_ref[bg], preferred_element_type=F32)
            y_ref[bg] = y.astype(BF16)

    blk = blk_l.astype(F32).astype(BF16)
    for s in range(S5_T):
        h, r = s // BLK, s % BLK
        acc = None
        for j in range(BLK):
            piece = y_ref[(j - r) % BLK, :, 128 * h:128 * (h + 1)]
            acc = piece if acc is None else jnp.where(blk == j, piece, acc)
        o_ref[s] = acc


def _s5core(ul, uc, win, wout, mix, l16):
    TH = S5_T * S5_H
    nl = N_CHUNK * B
    ncx = N_CCHUNK * B
    g4 = lambda r, c: pl.BlockSpec((BLK, r, c), lambda q: (q, 0, 0))
    col = lambda n: pl.BlockSpec((S5_T, n, 128), lambda q: (0, 0, q))
    f32s = lambda n: pltpu.VMEM((SCAN_GROUPS, n, 128), F32)
    return pl.pallas_call(
        _s5core_kernel,
        out_shape=jax.ShapeDtypeStruct((S5_T, nl, S5_W), BF16),
        grid=(S5_G // BLK,),
        in_specs=[col(nl), col(ncx), g4(TH, TH), g4(TH, TH), g4(TH, TH), g4(8, 128)],
        out_specs=col(nl),
        scratch_shapes=[pltpu.VMEM((SCAN_GROUPS, nl, TH), BF16),
                        f32s(nl), f32s(nl), f32s(ncx), f32s(ncx), f32s(nl), f32s(nl), f32s(nl), f32s(nl),
                        pltpu.VMEM((BLK, nl, TH), BF16)],
        compiler_params=_params("arbitrary"),
        name="s5core",
    )(ul, uc, win, wout, mix, l16)


def _s5tail_kernel(slat_ref, sza_ref, gluw_ref, glub_ref, wtop_ref, y_ref):
    unrot = (BLK - pl.program_id(0)) & (BLK - 1)
    for b in range(B):
        rows = slice(N_CHUNK * b, N_CHUNK * (b + 1))
        g = _gelu(_rot_blocks(slat_ref[rows, :].astype(F32), unrot))
        gate = _sigmoid(jnp.dot(g.astype(BF16), gluw_ref[...], preferred_element_type=F32) + glub_ref[...])
        a = (g * gate * sza_ref[rows, :].astype(F32)).astype(BF16)
        y_ref[rows, :] = jnp.dot(a, wtop_ref[...], preferred_element_type=F32).astype(BF16)


def _s5tail(slat, sza, glu_w, glu_b, w_top):
    slab = lambda r, h: r + BLK * h
    sspec = lambda w: pl.BlockSpec((None, N_CHUNK * B, w), lambda r, h: (slab(r, h), 0, 0))
    full = lambda *s: pl.BlockSpec(s, lambda r, h: (0,) * len(s))
    return pl.pallas_call(
        _s5tail_kernel,
        out_shape=jax.ShapeDtypeStruct((S5_T, N_CHUNK * B, D), BF16),
        grid=(BLK, S5_T // BLK),
        in_specs=[sspec(512), sspec(512), full(512, 512), full(1, 512), full(S5_W, D)],
        out_specs=sspec(D),
        compiler_params=_params("arbitrary", "arbitrary"),
        name="s5tail",
    )(slat, sza, glu_w, glu_b, w_top)


PERM_ROWS = S5_T * S5_T


def _tail0_kernel(x_ref, ys5_ref, guz_ref, vln_ref, mod_ref, sguw_ref, sgub_ref, wbot_ref, ng_ref, nb_ref, o_ref):
    tm = x_ref.shape[0]
    lane = lax.broadcasted_iota(jnp.int32, (SGU_CHUNK, 128), 1)
    lo = lane < SGU_HD
    zero = jnp.zeros((SGU_CHUNK, 128), BF16)
    chunks = []
    for ci in range(tm // SGU_CHUNK):
        v = vln_ref[ci * SGU_CHUNK:(ci + 1) * SGU_CHUNK, :]
        cols = []
        for pi in range(SGU_HEADS // 2):
            vp = v[:, 128 * pi:128 * (pi + 1)]
            bm = jnp.concatenate([jnp.where(lo, vp, zero), jnp.where(lo, zero, vp)], axis=0)
            cols.append(jnp.dot(sguw_ref[pi], bm, preferred_element_type=F32))
        chunks.append(jnp.concatenate(cols, axis=1) + sgub_ref[...])
    s = jnp.concatenate(chunks, axis=0)
    bsg = (guz_ref[...].astype(F32) * s).astype(BF16)
    ri = lax.broadcasted_iota(jnp.int32, (PERM_ROWS, PERM_ROWS), 0)
    ci = lax.broadcasted_iota(jnp.int32, (PERM_ROWS, PERM_ROWS), 1)
    perm = jnp.where(((ri >> H_SHIFT) == (ci & (S5_T - 1))) & ((ri & (S5_T - 1)) == (ci >> H_SHIFT)), 1.0, 0.0)
    perm = perm.astype(BF16)
    ys5 = jnp.concatenate(
        [jnp.dot(perm, ys5_ref[:, S5_T * j:S5_T * (j + 1), :].reshape(PERM_ROWS, D), preferred_element_type=F32)
         for j in range(tm // PERM_ROWS)], axis=0)
    y = ys5 + jnp.dot(bsg, wbot_ref[...], preferred_element_type=F32)
    gmod = mod_ref[:, 2 * D:3 * D]
    o_ref[...] = _layer_norm(DN_ALPHA * x_ref[...] + gmod * y, ng_ref[...], nb_ref[...])


def _tail0(x, ys5, guz, vln, mod, sguw, sgub, w_bot, ng, nb, tm=512):
    nct = tm // S5_T
    t512 = pl.BlockSpec((None, tm, 512), lambda b, i: (b, i, 0))
    tD = pl.BlockSpec((None, tm, D), lambda b, i: (b, i, 0))
    full = lambda *s: pl.BlockSpec(s, lambda b, i: (0,) * len(s))
    return pl.pallas_call(
        _tail0_kernel,
        out_shape=jax.ShapeDtypeStruct((B, L, D), F32),
        grid=(B, L // tm),
        in_specs=[tD, pl.BlockSpec((S5_T, nct, D), lambda b, i: (0, b * (N_CHUNK // nct) + i, 0)), t512, t512,
                  pl.BlockSpec((None, 1, 3 * D), lambda b, i: (b, 0, 0)),
                  full(SGU_HEADS // 2, SGU_CHUNK, 256), full(SGU_CHUNK, 512),
                  pl.BlockSpec((SGU_W, D), lambda b, i: (1, 0)), full(1, D), full(1, D)],
        out_specs=tD,
        compiler_params=_params("arbitrary", "arbitrary"),
        name="tail0",
    )(x, ys5, guz, vln, mod, sguw, sgub, w_bot, ng, nb)


def _inproj1_kernel(x_ref, mod_ref, w_ref, hg_ref, sz_ref):
    shift = mod_ref[:, 0:D]
    scale = mod_ref[:, D:2 * D]
    h = (x_ref[...] * (1.0 + scale) + shift).astype(BF16)
    dot = lambda lo: jnp.dot(h, w_ref[:, lo:lo + D].astype(BF16), preferred_element_type=F32)
    hg_ref[...] = (dot(0) * _sigmoid(dot(D))).astype(BF16)
    sz_ref[...] = _silu(dot(2 * D)).astype(BF16)


def _inproj1(x, mod, w_in_f32, tm=512):
    o = jax.ShapeDtypeStruct((B, L, D), BF16)
    ospec = pl.BlockSpec((None, tm, D), lambda b, i: (b, i, 0))
    return pl.pallas_call(
        _inproj1_kernel,
        out_shape=(o, o),
        grid=(B, L // tm),
        in_specs=[pl.BlockSpec((None, tm, D), lambda b, i: (b, i, 0)),
                  pl.BlockSpec((None, 1, 3 * D), lambda b, i: (b, 0, 0)),
                  pl.BlockSpec((D, ODD_IN), lambda b, i: (0, 0))],
        out_specs=(ospec, ospec),
        compiler_params=_params("arbitrary", "arbitrary"),
        name="inproj1",
    )(x, mod, w_in_f32)


ROW_PAD = 16
COL_PAD = CONV_HALF * GRID_W
CONV_ROWS = 128


def _conv_kernel(h_ref, w_ref, b_ref, o_ref, prow_ref, pcol_ref):
    j = pl.program_id(1)
    bias = b_ref[...]

    @pl.when(j < (D // 2) // 128)
    def _():
        zpad = jnp.zeros((GRID_W, ROW_PAD, 128), F32)
        prow_ref[:, 0:ROW_PAD, :] = zpad
        prow_ref[:, ROW_PAD + GRID_W:, :] = zpad
        prow_ref[:, ROW_PAD:ROW_PAD + GRID_W, :] = h_ref[...].astype(F32).reshape(GRID_W, GRID_W, 128)

        def body(r, carry):
            acc = jnp.zeros((GRID_W, 128), F32) + bias
            for k in range(CONV_K):
                off = ROW_PAD - CONV_HALF + k
                acc = acc + w_ref[k:k + 1, :] * prow_ref[r, off:off + GRID_W, :]
            o_ref[pl.ds(pl.multiple_of(r * GRID_W, GRID_W), GRID_W), :] = acc.astype(BF16)
            return carry

        lax.fori_loop(0, GRID_W, body, 0, unroll=2)

    @pl.when(j >= (D // 2) // 128)
    def _():
        zpad = jnp.zeros((COL_PAD, 128), F32)
        pcol_ref[0:COL_PAD, :] = zpad
        pcol_ref[COL_PAD + L:, :] = zpad
        pcol_ref[COL_PAD:COL_PAD + L, :] = h_ref[...].astype(F32)

        def body(i, carry):
            base = pl.multiple_of(i * CONV_ROWS, CONV_ROWS)
            acc = jnp.zeros((CONV_ROWS, 128), F32) + bias
            for k in range(CONV_K):
                acc = acc + w_ref[k:k + 1, :] * pcol_ref[pl.ds(base + k * GRID_W, CONV_ROWS), :]
            o_ref[pl.ds(base, CONV_ROWS), :] = acc.astype(BF16)
            return carry

        lax.fori_loop(0, L // CONV_ROWS, body, 0)


def _conv(hg, dw_w, dw_b):
    return pl.pallas_call(
        _conv_kernel,
        out_shape=jax.ShapeDtypeStruct((B, L, D), BF16),
        grid=(B, D // 128),
        in_specs=[pl.BlockSpec((None, L, 128), lambda b, j: (b, 0, j)),
                  pl.BlockSpec((CONV_K, 128), lambda b, j: (0, j)),
                  pl.BlockSpec((1, 128), lambda b, j: (0, j))],
        out_specs=pl.BlockSpec((None, L, 128), lambda b, j: (b, 0, j)),
        scratch_shapes=[pltpu.VMEM((GRID_W, GRID_W + 2 * ROW_PAD, 128), F32),
                        pltpu.VMEM((L + 2 * COL_PAD, 128), F32)],
        compiler_params=_params("arbitrary", "arbitrary"),
        name="dwconv",
    )(hg, dw_w, dw_b)


def _tail1_kernel(x_ref, hc_ref, sz_ref, mod_ref, lng_ref, lnb_ref, wout_ref, ng_ref, nb_ref, o_ref):
    m = _silu(_layer_norm(hc_ref[...].astype(F32), lng_ref[...], lnb_ref[...])) * sz_ref[...].astype(F32)
    y = jnp.dot(m.astype(BF16), wout_ref[...], preferred_element_type=F32)
    gmod = mod_ref[:, 2 * D:3 * D]
    o_ref[...] = _layer_norm(DN_ALPHA * x_ref[...] + gmod * y, ng_ref[...], nb_ref[...])


def _tail1(x, hc, sz, mod, ln_g, ln_b, w_out, ng, nb, tm=512):
    tD = pl.BlockSpec((None, tm, D), lambda b, i: (b, i, 0))
    full = lambda *s: pl.BlockSpec(s, lambda b, i: (0,) * len(s))
    return pl.pallas_call(
        _tail1_kernel,
        out_shape=jax.ShapeDtypeStruct((B, L, D), F32),
        grid=(B, L // tm),
        in_specs=[tD, tD, tD, pl.BlockSpec((None, 1, 3 * D), lambda b, i: (b, 0, 0)),
                  full(1, D), full(1, D), full(D, D), full(1, D), full(1, D)],
        out_specs=tD,
        compiler_params=_params("arbitrary", "arbitrary"),
        name="tail1",
    )(x, hc, sz, mod, ln_g, ln_b, w_out, ng, nb)


def kernel(x, c, ctx, c_ctx, mod_w, mod_b, norm_g, norm_b, ev_w_in, ev_w_out, s5_lam_re, s5_lam_im, s5_log_dt, s5_b_re, s5_b_im, s5_c_re, s5_c_im, s5_d, glu_w, glu_b, sgu_ln_g, sgu_ln_b, sgu_w, sgu_b, od_w_in, od_w_out, dw_w, dw_b, conv_ln_g, conv_ln_b):
    TH = S5_T * S5_H
    row = lambda v: v.reshape(1, -1)

    cond8 = jnp.concatenate([c, c_ctx[None], jnp.zeros((3, D), F32)], axis=0)
    mods = _adaln(cond8, mod_w, mod_b)
    mod0 = mods[0, :B].reshape(B, 1, 3 * D)
    mod0c = mods[0, B:B + 1]
    mod1 = mods[1, :B].reshape(B, 1, 3 * D)

    lbr, lbi, cfr, cfi = _discretise(s5_lam_re[0], s5_lam_im[0], s5_log_dt[0])
    rowcat = lambda a: jnp.concatenate([a[0], a[1]], axis=-1).reshape(S5_G, 1, 2 * S5_P)
    colcat = lambda a: jnp.broadcast_to(jnp.concatenate([a[0], a[1]], axis=-1)[:, :, None], (S5_G, 2 * S5_P, TH))
    bt = lambda a: jnp.concatenate([jnp.swapaxes(a[0], 1, 2), jnp.swapaxes(a[1], 1, 2)], axis=-1)
    ct = lambda a: jnp.tile(jnp.concatenate([jnp.swapaxes(a[0], 1, 2), jnp.swapaxes(a[1], 1, 2)], axis=1), (1, 1, S5_T))
    d_row = jnp.tile(s5_d[0].reshape(S5_G, 1, S5_H), (1, 1, S5_T))
    win, wout, mix, l16 = _s5_weights(rowcat(lbr), rowcat(lbi), rowcat(cfr), rowcat(cfi),
                                      bt(s5_b_re[0]), bt(s5_b_im[0]), colcat(lbr), colcat(lbi),
                                      ct(s5_c_re[0]), ct(s5_c_im[0]), d_row)

    w_sgu = ev_w_in[0][:, SGU_COL0:].astype(BF16)
    w_out0 = ev_w_out[0].astype(BF16)
    glu_w0 = glu_w[0].astype(BF16)
    guz, vln, hs = _inproj0n(x, mod0, w_sgu, row(sgu_ln_g[0]), row(sgu_ln_b[0]))
    ua, sza, ua_c = _inproj0a(hs, _ctx_slabs(ctx, mod0c), ev_w_in[0])
    s_lat = _s5core(ua, ua_c, win, wout, mix, l16)
    y_s5 = _s5tail(s_lat, sza, glu_w0, row(glu_b[0]), w_out0)
    sguw = sgu_w[0].reshape(SGU_HEADS // 2, 2, SGU_CHUNK, SGU_CHUNK)
    sguw = jnp.transpose(sguw, (0, 2, 1, 3)).reshape(SGU_HEADS // 2, SGU_CHUNK, 2 * SGU_CHUNK).astype(BF16)
    sgub = jnp.repeat(sgu_b[0].T, SGU_HD, axis=1)
    x1 = _tail0(x, y_s5, guz, vln, mod0, sguw, sgub, w_out0, row(norm_g[0]), row(norm_b[0]))

    hg, sz = _inproj1(x1, mod1, od_w_in[0])
    hc = _conv(hg, dw_w[0], row(dw_b[0]))
    return _tail1(x1, hc, sz, mod1, row(conv_ln_g[0]), row(conv_ln_b[0]), od_w_out[0].astype(BF16),
                  row(norm_g[1]), row(norm_b[1]))
```

```python
import functools
import math

import jax
import jax.numpy as jnp
from jax import lax
from jax.experimental import pallas as pl
from jax.experimental.pallas import tpu as pltpu

D = 1024
B = 4
L = 4096
CTX = 256
GRID_W = 64
S5_W = 512
S5_G = 32
S5_H = 16
H_SHIFT = 4
BLK = 128 // S5_H
S5_P = 64
S5_T = 16
SGU_W = 512
SGU_HEADS = 8
SGU_HD = 64
SGU_CHUNK = 128
CONV_K = 31
CONV_HALF = CONV_K // 2
EVEN_IN = 2560
SGU_COL0 = 2 * S5_W
ODD_IN = 3072
DEPTH = 2
DN_ALPHA = (2 * DEPTH) ** 0.25
LN_EPS = 1e-5
N_CHUNK = L // S5_T
N_CCHUNK = CTX // S5_T
VMEM_LIMIT_V7X = 56 * 1024 * 1024
TOKEN_TILE = 1024

F32 = jnp.float32
BF16 = jnp.bfloat16


GELU_C = math.sqrt(2.0 / math.pi)


def _gelu(x):
    hx = 0.5 * x
    return hx * jnp.tanh(x * ((x * x) * (0.044715 * GELU_C) + GELU_C)) + hx


def _sigmoid(x):
    return 0.5 * jnp.tanh(0.5 * x) + 0.5


def _silu(x):
    hx = 0.5 * x
    return hx * jnp.tanh(hx) + hx


def _layer_norm(x, g, b):
    mu = jnp.mean(x, axis=-1, keepdims=True)
    xc = x - mu
    var = jnp.mean(xc * xc, axis=-1, keepdims=True)
    return xc * lax.rsqrt(var + LN_EPS) * g + b


def _params(*sem):
    return pltpu.CompilerParams(dimension_semantics=sem, vmem_limit_bytes=VMEM_LIMIT_V7X)


def _adaln_kernel(c_ref, w_ref, b_ref, o_ref):
    def split(v):
        hi = v.astype(BF16)
        return hi, (v - hi.astype(F32)).astype(BF16)

    a_hi, a_lo = split(_silu(c_ref[...]))
    w_hi, w_lo = split(w_ref[...])
    dot = functools.partial(jnp.dot, preferred_element_type=F32)
    o_ref[...] = dot(a_hi, w_hi) + dot(a_lo, w_hi) + dot(a_hi, w_lo) + b_ref[...]


def _adaln(cond8, mod_w, mod_b):
    tn = 512
    return pl.pallas_call(
        _adaln_kernel,
        out_shape=jax.ShapeDtypeStruct((DEPTH, 8, 3 * D), F32),
        grid=(DEPTH, 3 * D // tn),
        in_specs=[pl.BlockSpec((8, D), lambda l, j: (0, 0)),
                  pl.BlockSpec((None, D, tn), lambda l, j: (l, 0, j)),
                  pl.BlockSpec((None, 1, tn), lambda l, j: (l, 0, j))],
        out_specs=pl.BlockSpec((None, 8, tn), lambda l, j: (l, 0, j)),
        compiler_params=_params("arbitrary", "arbitrary"),
        name="adaln",
    )(cond8, mod_w, mod_b.reshape(DEPTH, 1, 3 * D))


def _disc_kernel(lr_ref, li_ref, ldt_ref, obr_ref, obi_ref, ocr_ref, oci_ref):
    lr = lr_ref[...]
    li = li_ref[...]
    dt = jnp.exp(ldt_ref[...])
    mag = jnp.exp(lr * dt)
    br = mag * jnp.cos(li * dt)
    bi = mag * jnp.sin(li * dt)
    inv = 1.0 / (lr * lr + li * li)
    nr = br - 1.0
    obr_ref[...] = br
    obi_ref[...] = bi
    ocr_ref[...] = (nr * lr + bi * li) * inv
    oci_ref[...] = (bi * lr - nr * li) * inv


def _discretise(lam_re, lam_im, log_dt):
    shp = jax.ShapeDtypeStruct((2 * S5_G, S5_P), F32)
    ldt = jnp.broadcast_to(log_dt.reshape(2 * S5_G, 1), (2 * S5_G, S5_P))
    outs = pl.pallas_call(
        _disc_kernel, out_shape=(shp, shp, shp, shp), name="s5_discretise",
    )(lam_re.reshape(2 * S5_G, S5_P), lam_im.reshape(2 * S5_G, S5_P), ldt)
    return [o.reshape(2, S5_G, S5_P) for o in outs]


S5W_GROUPS = 4


def _cpow(base_pows, j):
    re = None
    im = None
    for k, (pr, pi) in enumerate(base_pows):
        bit = ((j >> k) & 1) == 1
        mr = jnp.where(bit, pr, 1.0)
        mi = jnp.where(bit, pi, 0.0)
        if re is None:
            re, im = mr, mi
        else:
            re, im = re * mr - im * mi, re * mi + im * mr
    return re, im


def _squarings(pr, pi, n):
    out = [(pr, pi)]
    for _ in range(n - 1):
        pr, pi = pr * pr - pi * pi, 2.0 * pr * pi
        out.append((pr, pi))
    return out


def _shift_lanes(x, n):
    lane = lax.broadcasted_iota(jnp.int32, (S5_H, 128), 1)
    lo, hi = x[:, :128], x[:, 128:]
    if n == 0:
        return x
    if n < 128:
        rlo = pltpu.roll(lo, n, axis=1)
        rhi = pltpu.roll(hi, n, axis=1)
        return jnp.concatenate([jnp.where(lane >= n, rlo, 0.0), jnp.where(lane >= n, rhi, rlo)], axis=1)
    m = n - 128
    rlo = lo if m == 0 else pltpu.roll(lo, m, axis=1)
    return jnp.concatenate([jnp.zeros_like(lo), jnp.where(lane >= m, rlo, 0.0)], axis=1)


def _unshift_lanes(x, n):
    lane = lax.broadcasted_iota(jnp.int32, (S5_H, 128), 1)
    lo, hi = x[:, :128], x[:, 128:]
    if n == 0:
        return x
    if n < 128:
        rlo = pltpu.roll(lo, 128 - n, axis=1)
        rhi = pltpu.roll(hi, 128 - n, axis=1)
        keep = lane < 128 - n
        return jnp.concatenate([jnp.where(keep, rlo, rhi), jnp.where(keep, rhi, 0.0)], axis=1)
    m = n - 128
    rhi = hi if m == 0 else pltpu.roll(hi, 128 - m, axis=1)
    return jnp.concatenate([jnp.where(lane < 128 - m, rhi, 0.0), jnp.zeros_like(lo)], axis=1)


def _s5w_group(gi, bg, lrow_re, lrow_im, crow_re, crow_im, bt_re, bt_im,
               lcol_re, lcol_im, ct_re, ct_im, d_ref, win_ref, wout_ref, mix_ref, l16_ref):
    TH = S5_T * S5_H

    def chunk_pos(idx):
        return (((idx >> H_SHIFT) - bg) & (BLK - 1)) + ((idx >> 7) << 3)

    lr = lrow_re[gi]
    li = lrow_im[gi]
    pows_row = _squarings(lr, li, 5)
    l16_ref[gi, 0:1, :] = pows_row[4][0]
    l16_ref[gi, 1:2, :] = pows_row[4][1]
    l16_ref[gi, 2:8, :] = jnp.zeros((6, 128), F32)
    cr = crow_re[gi]
    ci = crow_im[gi]
    btr = bt_re[gi]
    bti = bt_im[gi]
    bbr = cr * btr - ci * bti
    bbi = cr * bti + ci * btr
    s_idx = chunk_pos(lax.broadcasted_iota(jnp.int32, (TH, 128), 0))
    lane = lax.broadcasted_iota(jnp.int32, (TH, 128), 1)
    jw = jnp.where(lane < S5_P, S5_T - 1 - s_idx, s_idx)
    pr, pi = _cpow(pows_row[:4], jw)
    tbr = jnp.broadcast_to(bbr[None], (S5_T, S5_H, 128)).reshape(TH, 128)
    tbi = jnp.broadcast_to(bbi[None], (S5_T, S5_H, 128)).reshape(TH, 128)
    win_ref[gi, :, 0:128] = (pr * tbr - pi * tbi).astype(BF16)
    win_ref[gi, :, 128:256] = (pr * tbi + pi * tbr).astype(BF16)

    cpows = _squarings(lcol_re[gi], lcol_im[gi], 4)
    row = lax.broadcasted_iota(jnp.int32, (2 * S5_P, TH), 0)
    lane_w = lax.broadcasted_iota(jnp.int32, (2 * S5_P, TH), 1)
    t_idx = chunk_pos(lane_w)
    j_idx = lane_w >> H_SHIFT
    is_f = row < S5_P
    ctr = ct_re[gi]
    cti = ct_im[gi]
    er, ei = _cpow(cpows, jnp.where(is_f, t_idx, S5_T - 1 - t_idx))
    er, ei = er * cpows[0][0] - ei * cpows[0][1], er * cpows[0][1] + ei * cpows[0][0]
    wr = ctr * er - cti * ei
    wi = ctr * ei + cti * er
    wout_ref[gi, 0:128, :] = wr.astype(BF16)
    wout_ref[gi, 128:256, :] = (-wi).astype(BF16)
    kr, ki = _cpow(cpows, jnp.where(is_f, j_idx, S5_T - 1 - j_idx))
    ekr = ctr * kr - cti * ki
    eki = ctr * ki + cti * kr
    lane16 = lax.broadcasted_iota(jnp.int32, (S5_H, 128), 1)
    mf = lane16 < S5_P
    hp = lax.Precision.HIGHEST
    dot = functools.partial(jnp.dot, preferred_element_type=F32, precision=hp)
    kkf = dot(jnp.where(mf, bbr, 0.0), ekr) - dot(jnp.where(mf, bbi, 0.0), eki)
    kkb = dot(jnp.where(mf, 0.0, bbr), ekr) - dot(jnp.where(mf, 0.0, bbi), eki)
    dl = d_ref[gi]
    r16 = lax.broadcasted_iota(jnp.int32, (S5_H, TH), 0)
    l256 = lax.broadcasted_iota(jnp.int32, (S5_H, TH), 1)
    rot = bg * S5_H
    for s in range(S5_T):
        blk = _shift_lanes(kkf, S5_H * s) + _unshift_lanes(kkb, S5_H * (S5_T - 1 - s))
        blk = blk + jnp.where(l256 == r16 + S5_H * s, dl, 0.0)
        blk = jnp.concatenate([pltpu.roll(blk[:, :128], rot, axis=1), pltpu.roll(blk[:, 128:], rot, axis=1)], axis=1)
        rho = ((s + bg) & (BLK - 1)) + (s & BLK)
        mix_ref[gi, pl.ds(pl.multiple_of(rho * S5_H, S5_H), S5_H), :] = blk.astype(BF16)


def _s5w_kernel(*refs):
    for gi in range(S5W_GROUPS):
        bg = (pl.program_id(0) * S5W_GROUPS + gi) & (BLK - 1)
        _s5w_group(gi, bg, *refs)


def _s5_weights(lrow_re, lrow_im, crow_re, crow_im, bt_re, bt_im, lcol_re, lcol_im, ct_re, ct_im, d_row):
    TH = S5_T * S5_H
    g3 = lambda r, c: pl.BlockSpec((S5W_GROUPS, r, c), lambda g: (g, 0, 0))
    wshape = jax.ShapeDtypeStruct((S5_G, TH, TH), BF16)
    return pl.pallas_call(
        _s5w_kernel,
        out_shape=(wshape, wshape, wshape, jax.ShapeDtypeStruct((S5_G, 8, 128), F32)),
        grid=(S5_G // S5W_GROUPS,),
        in_specs=[g3(1, 128)] * 4 + [g3(S5_H, 128)] * 2 + [g3(128, TH)] * 4 + [g3(1, TH)],
        out_specs=(g3(TH, TH), g3(TH, TH), g3(TH, TH), g3(8, 128)),
        compiler_params=_params("arbitrary"),
        name="s5_weights",
    )(lrow_re, lrow_im, crow_re, crow_im, bt_re, bt_im, lcol_re, lcol_im, ct_re, ct_im, d_row)


def _rot_blocks(v, r):
    cols = [pltpu.roll(v[:, 128 * q:128 * (q + 1)], S5_H * r, axis=1) for q in range(v.shape[1] // 128)]
    return jnp.concatenate(cols, axis=1)


def _slabs_of(h, hs_ref):
    h3 = h.reshape(h.shape[0] // S5_T, S5_T, h.shape[1])
    for s in range(S5_T):
        hs_ref[s] = h3[:, s, :].astype(BF16)


PERM_ROWS = S5_T * S5_T


def _chunk_transpose_perm():
    ri = lax.broadcasted_iota(jnp.int32, (PERM_ROWS, PERM_ROWS), 0)
    ci = lax.broadcasted_iota(jnp.int32, (PERM_ROWS, PERM_ROWS), 1)
    hit = ((ri >> H_SHIFT) == (ci & (S5_T - 1))) & ((ri & (S5_T - 1)) == (ci >> H_SHIFT))
    return jnp.where(hit, 1.0, 0.0).astype(BF16)


def _inproj0n_kernel(x_ref, mod_ref, w_ref, lng_ref, lnb_ref, guz_ref, vln_ref, hs_ref):
    shift = mod_ref[:, 0:D]
    scale = mod_ref[:, D:2 * D]
    hb = (x_ref[...] * (1.0 + scale) + shift).astype(BF16)
    perm = _chunk_transpose_perm()
    for j in range(hb.shape[0] // PERM_ROWS):
        blk = jnp.dot(perm, hb[PERM_ROWS * j:PERM_ROWS * (j + 1), :], preferred_element_type=F32).astype(BF16)
        for s in range(S5_T):
            hs_ref[s, S5_T * j:S5_T * (j + 1), :] = blk[S5_T * s:S5_T * (s + 1), :]
    dot = lambda lo: jnp.dot(hb, w_ref[:, lo:lo + 512], preferred_element_type=F32)
    guz_ref[...] = (_gelu(dot(0)) * _silu(dot(1024))).astype(BF16)
    vln_ref[...] = _layer_norm(_gelu(dot(512)), lng_ref[...], lnb_ref[...]).astype(BF16)


def _inproj0n(x, mod, w_sgu, ln_g, ln_b, tm=TOKEN_TILE):
    nct = tm // S5_T
    o = jax.ShapeDtypeStruct((B, L, 512), BF16)
    ospec = pl.BlockSpec((None, tm, 512), lambda b, i: (b, i, 0))
    full = lambda *s: pl.BlockSpec(s, lambda b, i: (0,) * len(s))
    return pl.pallas_call(
        _inproj0n_kernel,
        out_shape=(o, o, jax.ShapeDtypeStruct((S5_T, B * N_CHUNK, D), BF16)),
        grid=(B, L // tm),
        in_specs=[pl.BlockSpec((None, tm, D), lambda b, i: (b, i, 0)),
                  pl.BlockSpec((None, 1, 3 * D), lambda b, i: (b, 0, 0)),
                  full(D, EVEN_IN - SGU_COL0), full(1, 512), full(1, 512)],
        out_specs=(ospec, ospec,
                   pl.BlockSpec((S5_T, nct, D), lambda b, i: (0, b * (N_CHUNK // nct) + i, 0))),
        compiler_params=_params("arbitrary", "arbitrary"),
        name="inproj0n",
    )(x, mod, w_sgu, ln_g, ln_b)


def _ctx_slabs_kernel(x_ref, mod_ref, hs_ref):
    h = x_ref[...] * (1.0 + mod_ref[:, D:2 * D]) + mod_ref[:, 0:D]
    _slabs_of(h, hs_ref)


def _ctx_slabs(ctx, mod_c):
    return pl.pallas_call(
        _ctx_slabs_kernel,
        out_shape=jax.ShapeDtypeStruct((S5_T, B * N_CCHUNK, D), BF16),
        grid=(B,),
        in_specs=[pl.BlockSpec((None, CTX, D), lambda b: (b, 0, 0)),
                  pl.BlockSpec((1, 3 * D), lambda b: (0, 0))],
        out_specs=pl.BlockSpec((S5_T, N_CCHUNK, D), lambda b: (0, b, 0)),
        compiler_params=_params("arbitrary"),
        name="ctx_slabs",
    )(ctx, mod_c)


def _inproj0a_kernel(hs_ref, hc_ref, w_ref, ua_ref, sza_ref, uc_ref):
    r = pl.program_id(0)
    h = hs_ref[...]
    w_ua = w_ref[:, 0:512].astype(BF16)
    ua_ref[...] = _rot_blocks(jnp.dot(h, w_ua, preferred_element_type=F32), r).astype(BF16)
    sza_ref[...] = _silu(jnp.dot(h, w_ref[:, 512:1024].astype(BF16), preferred_element_type=F32)).astype(BF16)
    uc_ref[...] = _rot_blocks(jnp.dot(hc_ref[...], w_ua, preferred_element_type=F32), r).astype(BF16)


def _inproj0a(hs, hcs, w_in_f32):
    slab = lambda r, h: r + BLK * h
    sspec = lambda n, w: pl.BlockSpec((None, n, w), lambda r, h: (slab(r, h), 0, 0))
    so = lambda n: jax.ShapeDtypeStruct((S5_T, n, 512), BF16)
    nl, ncx = B * N_CHUNK, B * N_CCHUNK
    return pl.pallas_call(
        _inproj0a_kernel,
        out_shape=(so(nl), so(nl), so(ncx)),
        grid=(BLK, S5_T // BLK),
        in_specs=[sspec(nl, D), sspec(ncx, D), pl.BlockSpec((D, SGU_COL0), lambda r, h: (0, 0))],
        out_specs=(sspec(nl, 512), sspec(nl, 512), sspec(ncx, 512)),
        compiler_params=_params("arbitrary", "arbitrary"),
        name="inproj0a",
    )(hs, hcs, w_in_f32)


SCAN_GROUPS = 4


def _scan_tiles(sre_ref, sim_ref, h_refs, n_tiles, carry, lams):
    row = lax.broadcasted_iota(jnp.int32, (8, 128), 0)
    lane = lax.broadcasted_iota(jnp.int32, (8, 128), 1)
    first = row < B
    fwd = lane < S5_P

    def body(k, c):
        of = pl.multiple_of(k * 8, 8)
        ob = pl.multiple_of((n_tiles - 1 - k) * 8, 8)
        out = []
        for gi in range(SCAN_GROUPS):
            lre, lim = lams[gi]
            hr, hi = c[2 * gi], c[2 * gi + 1]
            sr = jnp.where(fwd, sre_ref[gi, pl.ds(of, 8), :], pltpu.roll(sre_ref[gi, pl.ds(ob, 8), :], B, axis=0))
            si = jnp.where(fwd, sim_ref[gi, pl.ds(of, 8), :], pltpu.roll(sim_ref[gi, pl.ds(ob, 8), :], B, axis=0))
            h1r = lre * hr - lim * hi + sr
            h1i = lre * hi + lim * hr + si
            r1r = pltpu.roll(h1r, B, axis=0)
            r1i = pltpu.roll(h1i, B, axis=0)
            if h_refs is not None:
                fre_ref, fim_ref, bre_ref, bim_ref = h_refs
                er = jnp.where(first, hr, r1r)
                ei = jnp.where(first, hi, r1i)
                fre_ref[gi, pl.ds(of, 8), :] = er
                fim_ref[gi, pl.ds(of, 8), :] = ei
                bre_ref[gi, pl.ds(ob, 8), :] = pltpu.roll(er, B, axis=0)
                bim_ref[gi, pl.ds(ob, 8), :] = pltpu.roll(ei, B, axis=0)
            h2r = lre * r1r - lim * r1i + sr
            h2i = lre * r1i + lim * r1r + si
            out.append(jnp.where(first, pltpu.roll(h2r, B, axis=0), h2r))
            out.append(jnp.where(first, pltpu.roll(h2i, B, axis=0), h2i))
        return tuple(out)

    return lax.fori_loop(0, n_tiles, body, carry)


def _gather_group(slab_ref, src):
    halves = []
    for h in range(S5_T // BLK):
        acc = slab_ref[BLK * h]
        for s in range(1, BLK):
            acc = jnp.where(src == s, slab_ref[BLK * h + s], acc)
        halves.append(acc)
    return jnp.concatenate(halves, axis=1)


def _s5core_kernel(ul_ref, uc_ref, win_ref, wout_ref, mix_ref, l16_ref, o_ref,
                   u_ref, sre_ref, sim_ref, cre_ref, cim_ref, fre_ref, fim_ref, bre_ref, bim_ref, y_ref):
    nl = N_CHUNK * B
    ncx = N_CCHUNK * B
    blk_l = lax.broadcasted_iota(jnp.int32, (nl, 128), 1) >> H_SHIFT
    blk_c = lax.broadcasted_iota(jnp.int32, (ncx, 128), 1) >> H_SHIFT
    fwd = lax.broadcasted_iota(jnp.int32, (N_CHUNK, 128), 1) < S5_P
    for g0 in range(0, BLK, SCAN_GROUPS):
        for gi in range(SCAN_GROUPS):
            bg = g0 + gi
            win = win_ref[bg]
            src_l = ((blk_l - bg) & (BLK - 1)).astype(F32).astype(BF16)
            src_c = ((blk_c - bg) & (BLK - 1)).astype(F32).astype(BF16)
            u = _gather_group(ul_ref, src_l)
            u_ref[gi] = u
            sl = jnp.dot(u, win, preferred_element_type=F32)
            sc = jnp.dot(_gather_group(uc_ref, src_c), win, preferred_element_type=F32)
            for b in range(B):
                sre_ref[gi, pl.ds(b, N_CHUNK, stride=B), :] = sl[N_CHUNK * b:N_CHUNK * (b + 1), 0:128]
                sim_ref[gi, pl.ds(b, N_CHUNK, stride=B), :] = sl[N_CHUNK * b:N_CHUNK * (b + 1), 128:256]
                cre_ref[gi, pl.ds(b, N_CCHUNK, stride=B), :] = sc[N_CCHUNK * b:N_CCHUNK * (b + 1), 0:128]
                cim_ref[gi, pl.ds(b, N_CCHUNK, stride=B), :] = sc[N_CCHUNK * b:N_CCHUNK * (b + 1), 128:256]
        lams = [(jnp.broadcast_to(l16_ref[g0 + gi, 0:1, :], (8, 128)),
                 jnp.broadcast_to(l16_ref[g0 + gi, 1:2, :], (8, 128))) for gi in range(SCAN_GROUPS)]
        zero = tuple(jnp.zeros((8, 128), F32) for _ in range(2 * SCAN_GROUPS))
        carry = _scan_tiles(cre_ref, cim_ref, None, ncx // 8, zero, lams)
        _scan_tiles(sre_ref, sim_ref, (fre_ref, fim_ref, bre_ref, bim_ref), nl // 8, carry, lams)
        for gi in range(SCAN_GROUPS):
            bg = g0 + gi
            y = jnp.dot(u_ref[gi], mix_ref[bg], preferred_element_type=F32)
            hs = []
            for b in range(B):
                rows = pl.ds(b, N_CHUNK, stride=B)
                hs.append(jnp.concatenate([jnp.where(fwd, fre_ref[gi, rows, :], bre_ref[gi, rows, :]),
                                           jnp.where(fwd, fim_ref[gi, rows, :], bim_ref[gi, rows, :])], axis=1))
            hcat = jnp.concatenate(hs, axis=0).astype(BF16)
            y = y + jnp.dot(hcat, wout_ref[bg], preferred_element_type=F32)
            y_ref[bg] = y.astype(BF16)

    blk = blk_l.astype(F32).astype(BF16)
    for s in range(S5_T):
        h, r = s // BLK, s % BLK
        acc = None
        for j in range(BLK):
            piece = y_ref[(j - r) % BLK, :, 128 * h:128 * (h + 1)]
            acc = piece if acc is None else jnp.where(blk == j, piece, acc)
        o_ref[s] = acc


def _s5core(ul, uc, win, wout, mix, l16):
    TH = S5_T * S5_H
    nl = N_CHUNK * B
    ncx = N_CCHUNK * B
    g4 = lambda r, c: pl.BlockSpec((BLK, r, c), lambda q: (q, 0, 0))
    col = lambda n: pl.BlockSpec((S5_T, n, 128), lambda q: (0, 0, q))
    f32s = lambda n: pltpu.VMEM((SCAN_GROUPS, n, 128), F32)
    return pl.pallas_call(
        _s5core_kernel,
        out_shape=jax.ShapeDtypeStruct((S5_T, nl, S5_W), BF16),
        grid=(S5_G // BLK,),
        in_specs=[col(nl), col(ncx), g4(TH, TH), g4(TH, TH), g4(TH, TH), g4(8, 128)],
        out_specs=col(nl),
        scratch_shapes=[pltpu.VMEM((SCAN_GROUPS, nl, TH), BF16),
                        f32s(nl), f32s(nl), f32s(ncx), f32s(ncx), f32s(nl), f32s(nl), f32s(nl), f32s(nl),
                        pltpu.VMEM((BLK, nl, TH), BF16)],
        compiler_params=_params("arbitrary"),
        name="s5core",
    )(ul, uc, win, wout, mix, l16)


def _s5tail_kernel(slat_ref, sza_ref, gluw_ref, glub_ref, wtop_ref, y_ref):
    unrot = (BLK - pl.program_id(0)) & (BLK - 1)
    for b in range(B):
        rows = slice(N_CHUNK * b, N_CHUNK * (b + 1))
        g = _gelu(_rot_blocks(slat_ref[rows, :].astype(F32), unrot))
        gate = _sigmoid(jnp.dot(g.astype(BF16), gluw_ref[...], preferred_element_type=F32) + glub_ref[...])
        a = (g * gate * sza_ref[rows, :].astype(F32)).astype(BF16)
        y_ref[rows, :] = jnp.dot(a, wtop_ref[...], preferred_element_type=F32).astype(BF16)


def _s5tail(slat, sza, glu_w, glu_b, w_top):
    slab = lambda r, h: r + BLK * h
    sspec = lambda w: pl.BlockSpec((None, N_CHUNK * B, w), lambda r, h: (slab(r, h), 0, 0))
    full = lambda *s: pl.BlockSpec(s, lambda r, h: (0,) * len(s))
    return pl.pallas_call(
        _s5tail_kernel,
        out_shape=jax.ShapeDtypeStruct((S5_T, N_CHUNK * B, D), BF16),
        grid=(BLK, S5_T // BLK),
        in_specs=[sspec(512), sspec(512), full(512, 512), full(1, 512), full(S5_W, D)],
        out_specs=sspec(D),
        compiler_params=_params("arbitrary", "arbitrary"),
        name="s5tail",
    )(slat, sza, glu_w, glu_b, w_top)


def _tail0_kernel(x_ref, ys5_ref, guz_ref, vln_ref, mod_ref, sguw_ref, sgub_ref, wbot_ref, ng_ref, nb_ref, o_ref):
    tm = x_ref.shape[0]
    lane = lax.broadcasted_iota(jnp.int32, (SGU_CHUNK, 128), 1)
    lo = lane < SGU_HD
    zero = jnp.zeros((SGU_CHUNK, 128), BF16)
    chunks = []
    for ci in range(tm // SGU_CHUNK):
        v = vln_ref[ci * SGU_CHUNK:(ci + 1) * SGU_CHUNK, :]
        cols = []
        for pi in range(SGU_HEADS // 2):
            vp = v[:, 128 * pi:128 * (pi + 1)]
            bm = jnp.concatenate([jnp.where(lo, vp, zero), jnp.where(lo, zero, vp)], axis=0)
            cols.append(jnp.dot(sguw_ref[pi], bm, preferred_element_type=F32))
        chunks.append(jnp.concatenate(cols, axis=1) + sgub_ref[...])
    s = jnp.concatenate(chunks, axis=0)
    bsg = (guz_ref[...].astype(F32) * s).astype(BF16)
    perm = _chunk_transpose_perm()
    ys5 = jnp.concatenate(
        [jnp.dot(perm, ys5_ref[:, S5_T * j:S5_T * (j + 1), :].reshape(PERM_ROWS, D), preferred_element_type=F32)
         for j in range(tm // PERM_ROWS)], axis=0)
    y = ys5 + jnp.dot(bsg, wbot_ref[...], preferred_element_type=F32)
    gmod = mod_ref[:, 2 * D:3 * D]
    o_ref[...] = _layer_norm(DN_ALPHA * x_ref[...] + gmod * y, ng_ref[...], nb_ref[...])


def _tail0(x, ys5, guz, vln, mod, sguw, sgub, w_bot, ng, nb, tm=TOKEN_TILE):
    nct = tm // S5_T
    t512 = pl.BlockSpec((None, tm, 512), lambda b, i: (b, i, 0))
    tD = pl.BlockSpec((None, tm, D), lambda b, i: (b, i, 0))
    full = lambda *s: pl.BlockSpec(s, lambda b, i: (0,) * len(s))
    return pl.pallas_call(
        _tail0_kernel,
        out_shape=jax.ShapeDtypeStruct((B, L, D), F32),
        grid=(B, L // tm),
        in_specs=[tD, pl.BlockSpec((S5_T, nct, D), lambda b, i: (0, b * (N_CHUNK // nct) + i, 0)), t512, t512,
                  pl.BlockSpec((None, 1, 3 * D), lambda b, i: (b, 0, 0)),
                  full(SGU_HEADS // 2, SGU_CHUNK, 256), full(SGU_CHUNK, 512),
                  pl.BlockSpec((SGU_W, D), lambda b, i: (1, 0)), full(1, D), full(1, D)],
        out_specs=tD,
        compiler_params=_params("arbitrary", "arbitrary"),
        name="tail0",
    )(x, ys5, guz, vln, mod, sguw, sgub, w_bot, ng, nb)


def _inproj1_kernel(x_ref, mod_ref, w_ref, hg_ref, sz_ref):
    shift = mod_ref[:, 0:D]
    scale = mod_ref[:, D:2 * D]
    h = (x_ref[...] * (1.0 + scale) + shift).astype(BF16)
    dot = lambda lo: jnp.dot(h, w_ref[:, lo:lo + D].astype(BF16), preferred_element_type=F32)
    hg_ref[...] = (dot(0) * _sigmoid(dot(D))).astype(BF16)
    sz_ref[...] = _silu(dot(2 * D)).astype(BF16)


def _inproj1(x, mod, w_in_f32, tm=TOKEN_TILE // 2):
    o = jax.ShapeDtypeStruct((B, L, D), BF16)
    ospec = pl.BlockSpec((None, tm, D), lambda b, i: (b, i, 0))
    return pl.pallas_call(
        _inproj1_kernel,
        out_shape=(o, o),
        grid=(B, L // tm),
        in_specs=[pl.BlockSpec((None, tm, D), lambda b, i: (b, i, 0)),
                  pl.BlockSpec((None, 1, 3 * D), lambda b, i: (b, 0, 0)),
                  pl.BlockSpec((D, ODD_IN), lambda b, i: (0, 0))],
        out_specs=(ospec, ospec),
        compiler_params=_params("arbitrary", "arbitrary"),
        name="inproj1",
    )(x, mod, w_in_f32)


ROW_PAD = 16
COL_PAD = CONV_HALF * GRID_W
CONV_ROWS = 128


def _conv_kernel(h_ref, w_ref, b_ref, o_ref, prow_ref, pcol_ref):
    j = pl.program_id(1)
    bias = b_ref[...]

    @pl.when(j < (D // 2) // 128)
    def _():
        zpad = jnp.zeros((GRID_W, ROW_PAD, 128), F32)
        prow_ref[:, 0:ROW_PAD, :] = zpad
        prow_ref[:, ROW_PAD + GRID_W:, :] = zpad
        prow_ref[:, ROW_PAD:ROW_PAD + GRID_W, :] = h_ref[...].astype(F32).reshape(GRID_W, GRID_W, 128)

        def body(r, carry):
            acc = jnp.zeros((GRID_W, 128), F32) + bias
            for k in range(CONV_K):
                off = ROW_PAD - CONV_HALF + k
                acc = acc + w_ref[k:k + 1, :] * prow_ref[r, off:off + GRID_W, :]
            o_ref[pl.ds(pl.multiple_of(r * GRID_W, GRID_W), GRID_W), :] = acc.astype(BF16)
            return carry

        lax.fori_loop(0, GRID_W, body, 0, unroll=2)

    @pl.when(j >= (D // 2) // 128)
    def _():
        zpad = jnp.zeros((COL_PAD, 128), F32)
        pcol_ref[0:COL_PAD, :] = zpad
        pcol_ref[COL_PAD + L:, :] = zpad
        pcol_ref[COL_PAD:COL_PAD + L, :] = h_ref[...].astype(F32)

        def body(i, carry):
            base = pl.multiple_of(i * CONV_ROWS, CONV_ROWS)
            acc = jnp.zeros((CONV_ROWS, 128), F32) + bias
            for k in range(CONV_K):
                acc = acc + w_ref[k:k + 1, :] * pcol_ref[pl.ds(base + k * GRID_W, CONV_ROWS), :]
            o_ref[pl.ds(base, CONV_ROWS), :] = acc.astype(BF16)
            return carry

        lax.fori_loop(0, L // CONV_ROWS, body, 0)


def _conv(hg, dw_w, dw_b):
    return pl.pallas_call(
        _conv_kernel,
        out_shape=jax.ShapeDtypeStruct((B, L, D), BF16),
        grid=(B, D // 128),
        in_specs=[pl.BlockSpec((None, L, 128), lambda b, j: (b, 0, j)),
                  pl.BlockSpec((CONV_K, 128), lambda b, j: (0, j)),
                  pl.BlockSpec((1, 128), lambda b, j: (0, j))],
        out_specs=pl.BlockSpec((None, L, 128), lambda b, j: (b, 0, j)),
        scratch_shapes=[pltpu.VMEM((GRID_W, GRID_W + 2 * ROW_PAD, 128), F32),
                        pltpu.VMEM((L + 2 * COL_PAD, 128), F32)],
        compiler_params=_params("arbitrary", "arbitrary"),
        name="dwconv",
    )(hg, dw_w, dw_b)


def _tail1_kernel(x_ref, hc_ref, sz_ref, mod_ref, lng_ref, lnb_ref, wout_ref, ng_ref, nb_ref, o_ref):
    m = _silu(_layer_norm(hc_ref[...].astype(F32), lng_ref[...], lnb_ref[...])) * sz_ref[...].astype(F32)
    y = jnp.dot(m.astype(BF16), wout_ref[...], preferred_element_type=F32)
    gmod = mod_ref[:, 2 * D:3 * D]
    o_ref[...] = _layer_norm(DN_ALPHA * x_ref[...] + gmod * y, ng_ref[...], nb_ref[...])


def _tail1(x, hc, sz, mod, ln_g, ln_b, w_out, ng, nb, tm=TOKEN_TILE):
    tD = pl.BlockSpec((None, tm, D), lambda b, i: (b, i, 0))
    full = lambda *s: pl.BlockSpec(s, lambda b, i: (0,) * len(s))
    return pl.pallas_call(
        _tail1_kernel,
        out_shape=jax.ShapeDtypeStruct((B, L, D), F32),
        grid=(B, L // tm),
        in_specs=[tD, tD, tD, pl.BlockSpec((None, 1, 3 * D), lambda b, i: (b, 0, 0)),
                  full(1, D), full(1, D), full(D, D), full(1, D), full(1, D)],
        out_specs=tD,
        compiler_params=_params("arbitrary", "arbitrary"),
        name="tail1",
    )(x, hc, sz, mod, ln_g, ln_b, w_out, ng, nb)


def kernel(x, c, ctx, c_ctx, mod_w, mod_b, norm_g, norm_b, ev_w_in, ev_w_out, s5_lam_re, s5_lam_im, s5_log_dt, s5_b_re, s5_b_im, s5_c_re, s5_c_im, s5_d, glu_w, glu_b, sgu_ln_g, sgu_ln_b, sgu_w, sgu_b, od_w_in, od_w_out, dw_w, dw_b, conv_ln_g, conv_ln_b):
    TH = S5_T * S5_H
    row = lambda v: v.reshape(1, -1)

    cond8 = jnp.concatenate([c, c_ctx[None], jnp.zeros((3, D), F32)], axis=0)
    mods = _adaln(cond8, mod_w, mod_b)
    mod0 = mods[0, :B].reshape(B, 1, 3 * D)
    mod0c = mods[0, B:B + 1]
    mod1 = mods[1, :B].reshape(B, 1, 3 * D)

    lbr, lbi, cfr, cfi = _discretise(s5_lam_re[0], s5_lam_im[0], s5_log_dt[0])
    rowcat = lambda a: jnp.concatenate([a[0], a[1]], axis=-1).reshape(S5_G, 1, 2 * S5_P)
    colcat = lambda a: jnp.broadcast_to(jnp.concatenate([a[0], a[1]], axis=-1)[:, :, None], (S5_G, 2 * S5_P, TH))
    bt = lambda a: jnp.concatenate([jnp.swapaxes(a[0], 1, 2), jnp.swapaxes(a[1], 1, 2)], axis=-1)
    ct = lambda a: jnp.tile(jnp.concatenate([jnp.swapaxes(a[0], 1, 2), jnp.swapaxes(a[1], 1, 2)], axis=1), (1, 1, S5_T))
    d_row = jnp.tile(s5_d[0].reshape(S5_G, 1, S5_H), (1, 1, S5_T))
    win, wout, mix, l16 = _s5_weights(rowcat(lbr), rowcat(lbi), rowcat(cfr), rowcat(cfi),
                                      bt(s5_b_re[0]), bt(s5_b_im[0]), colcat(lbr), colcat(lbi),
                                      ct(s5_c_re[0]), ct(s5_c_im[0]), d_row)

    w_sgu = ev_w_in[0][:, SGU_COL0:].astype(BF16)
    w_out0 = ev_w_out[0].astype(BF16)
    glu_w0 = glu_w[0].astype(BF16)
    guz, vln, hs = _inproj0n(x, mod0, w_sgu, row(sgu_ln_g[0]), row(sgu_ln_b[0]))
    ua, sza, ua_c = _inproj0a(hs, _ctx_slabs(ctx, mod0c), ev_w_in[0])
    s_lat = _s5core(ua, ua_c, win, wout, mix, l16)
    y_s5 = _s5tail(s_lat, sza, glu_w0, row(glu_b[0]), w_out0)
    sguw = sgu_w[0].reshape(SGU_HEADS // 2, 2, SGU_CHUNK, SGU_CHUNK)
    sguw = jnp.transpose(sguw, (0, 2, 1, 3)).reshape(SGU_HEADS // 2, SGU_CHUNK, 2 * SGU_CHUNK).astype(BF16)
    sgub = jnp.repeat(sgu_b[0].T, SGU_HD, axis=1)
    x1 = _tail0(x, y_s5, guz, vln, mod0, sguw, sgub, w_out0, row(norm_g[0]), row(norm_b[0]))

    hg, sz = _inproj1(x1, mod1, od_w_in[0])
    hc = _conv(hg, dw_w[0], row(dw_b[0]))
    return _tail1(x1, hc, sz, mod1, row(conv_ln_g[0]), row(conv_ln_b[0]), od_w_out[0].astype(BF16),
                  row(norm_g[1]), row(norm_b[1]))
```

```python
import functools
import math

import jax
import jax.numpy as jnp
from jax import lax
from jax.experimental import pallas as pl
from jax.experimental.pallas import tpu as pltpu

D = 1024
B = 4
L = 4096
CTX = 256
GRID_W = 64
S5_W = 512
S5_G = 32
S5_H = 16
H_SHIFT = 4
BLK = 128 // S5_H
S5_P = 64
S5_T = 16
SGU_W = 512
SGU_HEADS = 8
SGU_HD = 64
SGU_CHUNK = 128
CONV_K = 31
CONV_HALF = CONV_K // 2
EVEN_IN = 2560
SGU_COL0 = 2 * S5_W
ODD_IN = 3072
DEPTH = 2
DN_ALPHA = (2 * DEPTH) ** 0.25
LN_EPS = 1e-5
N_CHUNK = L // S5_T
N_CCHUNK = CTX // S5_T
VMEM_LIMIT_V7X = 56 * 1024 * 1024
TOKEN_TILE = 1024

F32 = jnp.float32
BF16 = jnp.bfloat16


GELU_C = math.sqrt(2.0 / math.pi)


def _gelu(x):
    hx = 0.5 * x
    return hx * jnp.tanh(x * ((x * x) * (0.044715 * GELU_C) + GELU_C)) + hx


def _sigmoid(x):
    return 0.5 * jnp.tanh(0.5 * x) + 0.5


def _silu(x):
    hx = 0.5 * x
    return hx * jnp.tanh(hx) + hx


def _layer_norm(x, g, b):
    mu = jnp.mean(x, axis=-1, keepdims=True)
    xc = x - mu
    var = jnp.mean(xc * xc, axis=-1, keepdims=True)
    return xc * lax.rsqrt(var + LN_EPS) * g + b


def _params(*sem):
    return pltpu.CompilerParams(dimension_semantics=sem, vmem_limit_bytes=VMEM_LIMIT_V7X)


def _adaln_kernel(c_ref, w_ref, b_ref, o_ref):
    def split(v):
        hi = v.astype(BF16)
        return hi, (v - hi.astype(F32)).astype(BF16)

    a_hi, a_lo = split(_silu(c_ref[...]))
    w_hi, w_lo = split(w_ref[...])
    dot = functools.partial(jnp.dot, preferred_element_type=F32)
    o_ref[...] = dot(a_hi, w_hi) + dot(a_lo, w_hi) + dot(a_hi, w_lo) + b_ref[...]


def _adaln(cond8, mod_w, mod_b):
    tn = 512
    return pl.pallas_call(
        _adaln_kernel,
        out_shape=jax.ShapeDtypeStruct((DEPTH, 8, 3 * D), F32),
        grid=(DEPTH, 3 * D // tn),
        in_specs=[pl.BlockSpec((8, D), lambda l, j: (0, 0)),
                  pl.BlockSpec((None, D, tn), lambda l, j: (l, 0, j)),
                  pl.BlockSpec((None, 1, tn), lambda l, j: (l, 0, j))],
        out_specs=pl.BlockSpec((None, 8, tn), lambda l, j: (l, 0, j)),
        compiler_params=_params("arbitrary", "arbitrary"),
        name="adaln",
    )(cond8, mod_w, mod_b.reshape(DEPTH, 1, 3 * D))


def _disc_kernel(lr_ref, li_ref, ldt_ref, obr_ref, obi_ref, ocr_ref, oci_ref):
    lr = lr_ref[...]
    li = li_ref[...]
    dt = jnp.exp(ldt_ref[...])
    mag = jnp.exp(lr * dt)
    br = mag * jnp.cos(li * dt)
    bi = mag * jnp.sin(li * dt)
    inv = 1.0 / (lr * lr + li * li)
    nr = br - 1.0
    obr_ref[...] = br
    obi_ref[...] = bi
    ocr_ref[...] = (nr * lr + bi * li) * inv
    oci_ref[...] = (bi * lr - nr * li) * inv


def _discretise(lam_re, lam_im, log_dt):
    shp = jax.ShapeDtypeStruct((2 * S5_G, S5_P), F32)
    ldt = jnp.broadcast_to(log_dt.reshape(2 * S5_G, 1), (2 * S5_G, S5_P))
    outs = pl.pallas_call(
        _disc_kernel, out_shape=(shp, shp, shp, shp), name="s5_discretise",
    )(lam_re.reshape(2 * S5_G, S5_P), lam_im.reshape(2 * S5_G, S5_P), ldt)
    return [o.reshape(2, S5_G, S5_P) for o in outs]


S5W_GROUPS = 4


def _cpow(base_pows, j):
    re = None
    im = None
    for k, (pr, pi) in enumerate(base_pows):
        bit = ((j >> k) & 1) == 1
        mr = jnp.where(bit, pr, 1.0)
        mi = jnp.where(bit, pi, 0.0)
        if re is None:
            re, im = mr, mi
        else:
            re, im = re * mr - im * mi, re * mi + im * mr
    return re, im


def _squarings(pr, pi, n):
    out = [(pr, pi)]
    for _ in range(n - 1):
        pr, pi = pr * pr - pi * pi, 2.0 * pr * pi
        out.append((pr, pi))
    return out


def _shift_lanes(x, n):
    lane = lax.broadcasted_iota(jnp.int32, (S5_H, 128), 1)
    lo, hi = x[:, :128], x[:, 128:]
    if n == 0:
        return x
    if n < 128:
        rlo = pltpu.roll(lo, n, axis=1)
        rhi = pltpu.roll(hi, n, axis=1)
        return jnp.concatenate([jnp.where(lane >= n, rlo, 0.0), jnp.where(lane >= n, rhi, rlo)], axis=1)
    m = n - 128
    rlo = lo if m == 0 else pltpu.roll(lo, m, axis=1)
    return jnp.concatenate([jnp.zeros_like(lo), jnp.where(lane >= m, rlo, 0.0)], axis=1)


def _unshift_lanes(x, n):
    lane = lax.broadcasted_iota(jnp.int32, (S5_H, 128), 1)
    lo, hi = x[:, :128], x[:, 128:]
    if n == 0:
        return x
    if n < 128:
        rlo = pltpu.roll(lo, 128 - n, axis=1)
        rhi = pltpu.roll(hi, 128 - n, axis=1)
        keep = lane < 128 - n
        return jnp.concatenate([jnp.where(keep, rlo, rhi), jnp.where(keep, rhi, 0.0)], axis=1)
    m = n - 128
    rhi = hi if m == 0 else pltpu.roll(hi, 128 - m, axis=1)
    return jnp.concatenate([jnp.where(lane < 128 - m, rhi, 0.0), jnp.zeros_like(lo)], axis=1)


def _s5w_group(gi, bg, lrow_re, lrow_im, crow_re, crow_im, bt_re, bt_im,
               cn_re, cn_im, d_ref, win_ref, wout_ref, mix_ref, l16_ref):
    TH = S5_T * S5_H

    def chunk_pos(idx):
        return (((idx >> H_SHIFT) - bg) & (BLK - 1)) + ((idx >> 7) << 3)

    lr = lrow_re[gi]
    li = lrow_im[gi]
    pows_row = _squarings(lr, li, 5)
    l16_ref[gi, 0:1, :] = pows_row[4][0]
    l16_ref[gi, 1:2, :] = pows_row[4][1]
    l16_ref[gi, 2:8, :] = jnp.zeros((6, 128), F32)
    cr = crow_re[gi]
    ci = crow_im[gi]
    btr = bt_re[gi]
    bti = bt_im[gi]
    bbr = cr * btr - ci * bti
    bbi = cr * bti + ci * btr
    s_idx = chunk_pos(lax.broadcasted_iota(jnp.int32, (TH, 128), 0))
    lane = lax.broadcasted_iota(jnp.int32, (TH, 128), 1)
    jw = jnp.where(lane < S5_P, S5_T - 1 - s_idx, s_idx)
    pr, pi = _cpow(pows_row[:4], jw)
    tbr = jnp.broadcast_to(bbr[None], (S5_T, S5_H, 128)).reshape(TH, 128)
    tbi = jnp.broadcast_to(bbi[None], (S5_T, S5_H, 128)).reshape(TH, 128)
    win_ref[gi, :, 0:128] = (pr * tbr - pi * tbi).astype(BF16)
    win_ref[gi, :, 128:256] = (pr * tbi + pi * tbr).astype(BF16)

    def col256(r):
        col = jnp.broadcast_to(r, (2 * S5_P, 2 * S5_P)).T
        return jnp.concatenate([col, col], axis=1)

    def tiled_t(cn):
        t8 = jnp.broadcast_to(cn[None], (BLK, S5_H, 2 * S5_P)).reshape(2 * S5_P, 2 * S5_P).T
        return jnp.concatenate([t8, t8], axis=1)

    cpows = _squarings(col256(lr), col256(li), 4)
    row = lax.broadcasted_iota(jnp.int32, (2 * S5_P, TH), 0)
    lane_w = lax.broadcasted_iota(jnp.int32, (2 * S5_P, TH), 1)
    t_idx = chunk_pos(lane_w)
    j_idx = lane_w >> H_SHIFT
    is_f = row < S5_P
    ctr = tiled_t(cn_re[gi])
    cti = tiled_t(cn_im[gi])
    er, ei = _cpow(cpows, jnp.where(is_f, t_idx, S5_T - 1 - t_idx))
    er, ei = er * cpows[0][0] - ei * cpows[0][1], er * cpows[0][1] + ei * cpows[0][0]
    wr = ctr * er - cti * ei
    wi = ctr * ei + cti * er
    wout_ref[gi, 0:128, :] = wr.astype(BF16)
    wout_ref[gi, 128:256, :] = (-wi).astype(BF16)
    kr, ki = _cpow(cpows, jnp.where(is_f, j_idx, S5_T - 1 - j_idx))
    ekr = ctr * kr - cti * ki
    eki = ctr * ki + cti * kr
    lane16 = lax.broadcasted_iota(jnp.int32, (S5_H, 128), 1)
    mf = lane16 < S5_P
    hp = lax.Precision.HIGHEST
    dot = functools.partial(jnp.dot, preferred_element_type=F32, precision=hp)
    kkf = dot(jnp.where(mf, bbr, 0.0), ekr) - dot(jnp.where(mf, bbi, 0.0), eki)
    kkb = dot(jnp.where(mf, 0.0, bbr), ekr) - dot(jnp.where(mf, 0.0, bbi), eki)
    dl = d_ref[gi]
    r16 = lax.broadcasted_iota(jnp.int32, (S5_H, TH), 0)
    l256 = lax.broadcasted_iota(jnp.int32, (S5_H, TH), 1)
    rot = bg * S5_H
    for s in range(S5_T):
        blk = _shift_lanes(kkf, S5_H * s) + _unshift_lanes(kkb, S5_H * (S5_T - 1 - s))
        blk = blk + jnp.where(l256 == r16 + S5_H * s, dl, 0.0)
        blk = jnp.concatenate([pltpu.roll(blk[:, :128], rot, axis=1), pltpu.roll(blk[:, 128:], rot, axis=1)], axis=1)
        rho = ((s + bg) & (BLK - 1)) + (s & BLK)
        mix_ref[gi, pl.ds(pl.multiple_of(rho * S5_H, S5_H), S5_H), :] = blk.astype(BF16)


def _s5w_kernel(*refs):
    for gi in range(S5W_GROUPS):
        bg = (pl.program_id(0) * S5W_GROUPS + gi) & (BLK - 1)
        _s5w_group(gi, bg, *refs)


def _s5_weights(lrow_re, lrow_im, crow_re, crow_im, bt_re, bt_im, cn_re, cn_im, d_row):
    TH = S5_T * S5_H
    g3 = lambda r, c: pl.BlockSpec((S5W_GROUPS, r, c), lambda g: (g, 0, 0))
    wshape = jax.ShapeDtypeStruct((S5_G, TH, TH), BF16)
    return pl.pallas_call(
        _s5w_kernel,
        out_shape=(wshape, wshape, wshape, jax.ShapeDtypeStruct((S5_G, 8, 128), F32)),
        grid=(S5_G // S5W_GROUPS,),
        in_specs=[g3(1, 128)] * 4 + [g3(S5_H, 128)] * 4 + [g3(1, TH)],
        out_specs=(g3(TH, TH), g3(TH, TH), g3(TH, TH), g3(8, 128)),
        compiler_params=_params("arbitrary"),
        name="s5_weights",
    )(lrow_re, lrow_im, crow_re, crow_im, bt_re, bt_im, cn_re, cn_im, d_row)


def _rot_blocks(v, r):
    cols = [pltpu.roll(v[:, 128 * q:128 * (q + 1)], S5_H * r, axis=1) for q in range(v.shape[1] // 128)]
    return jnp.concatenate(cols, axis=1)


def _slabs_of(h, hs_ref):
    h3 = h.reshape(h.shape[0] // S5_T, S5_T, h.shape[1])
    for s in range(S5_T):
        hs_ref[s] = h3[:, s, :].astype(BF16)


PERM_ROWS = S5_T * S5_T


def _chunk_transpose_perm():
    ri = lax.broadcasted_iota(jnp.int32, (PERM_ROWS, PERM_ROWS), 0)
    ci = lax.broadcasted_iota(jnp.int32, (PERM_ROWS, PERM_ROWS), 1)
    hit = ((ri >> H_SHIFT) == (ci & (S5_T - 1))) & ((ri & (S5_T - 1)) == (ci >> H_SHIFT))
    return jnp.where(hit, 1.0, 0.0).astype(BF16)


def _inproj0n_kernel(x_ref, mod_ref, w_ref, lng_ref, lnb_ref, guz_ref, vln_ref, hs_ref):
    shift = mod_ref[:, 0:D]
    scale = mod_ref[:, D:2 * D]
    hb = (x_ref[...] * (1.0 + scale) + shift).astype(BF16)
    perm = _chunk_transpose_perm()
    for j in range(hb.shape[0] // PERM_ROWS):
        blk = jnp.dot(perm, hb[PERM_ROWS * j:PERM_ROWS * (j + 1), :], preferred_element_type=F32).astype(BF16)
        for s in range(S5_T):
            hs_ref[s, S5_T * j:S5_T * (j + 1), :] = blk[S5_T * s:S5_T * (s + 1), :]
    dot = lambda lo: jnp.dot(hb, w_ref[:, lo:lo + 512], preferred_element_type=F32)
    guz_ref[...] = (_gelu(dot(0)) * _silu(dot(1024))).astype(BF16)
    vln_ref[...] = _layer_norm(_gelu(dot(512)), lng_ref[...], lnb_ref[...]).astype(BF16)


def _inproj0n(x, mod, w_sgu, ln_g, ln_b, tm=TOKEN_TILE):
    nct = tm // S5_T
    o = jax.ShapeDtypeStruct((B, L, 512), BF16)
    ospec = pl.BlockSpec((None, tm, 512), lambda b, i: (b, i, 0))
    full = lambda *s: pl.BlockSpec(s, lambda b, i: (0,) * len(s))
    return pl.pallas_call(
        _inproj0n_kernel,
        out_shape=(o, o, jax.ShapeDtypeStruct((S5_T, B * N_CHUNK, D), BF16)),
        grid=(B, L // tm),
        in_specs=[pl.BlockSpec((None, tm, D), lambda b, i: (b, i, 0)),
                  pl.BlockSpec((None, 1, 3 * D), lambda b, i: (b, 0, 0)),
                  full(D, EVEN_IN - SGU_COL0), full(1, 512), full(1, 512)],
        out_specs=(ospec, ospec,
                   pl.BlockSpec((S5_T, nct, D), lambda b, i: (0, b * (N_CHUNK // nct) + i, 0))),
        compiler_params=_params("arbitrary", "arbitrary"),
        name="inproj0n",
    )(x, mod, w_sgu, ln_g, ln_b)


def _ctx_slabs_kernel(x_ref, mod_ref, hs_ref):
    h = x_ref[...] * (1.0 + mod_ref[:, D:2 * D]) + mod_ref[:, 0:D]
    _slabs_of(h, hs_ref)


def _ctx_slabs(ctx, mod_c):
    return pl.pallas_call(
        _ctx_slabs_kernel,
        out_shape=jax.ShapeDtypeStruct((S5_T, B * N_CCHUNK, D), BF16),
        grid=(B,),
        in_specs=[pl.BlockSpec((None, CTX, D), lambda b: (b, 0, 0)),
                  pl.BlockSpec((1, 3 * D), lambda b: (0, 0))],
        out_specs=pl.BlockSpec((S5_T, N_CCHUNK, D), lambda b: (0, b, 0)),
        compiler_params=_params("arbitrary"),
        name="ctx_slabs",
    )(ctx, mod_c)


def _inproj0a_kernel(hs_ref, hc_ref, w_ref, ua_ref, sza_ref, uc_ref):
    r = pl.program_id(0)
    h = hs_ref[...]
    w_ua = w_ref[:, 0:512].astype(BF16)
    ua_ref[...] = _rot_blocks(jnp.dot(h, w_ua, preferred_element_type=F32), r).astype(BF16)
    sza_ref[...] = _silu(jnp.dot(h, w_ref[:, 512:1024].astype(BF16), preferred_element_type=F32)).astype(BF16)
    uc_ref[...] = _rot_blocks(jnp.dot(hc_ref[...], w_ua, preferred_element_type=F32), r).astype(BF16)


def _inproj0a(hs, hcs, w_in_f32):
    slab = lambda r, h: r + BLK * h
    sspec = lambda n, w: pl.BlockSpec((None, n, w), lambda r, h: (slab(r, h), 0, 0))
    so = lambda n: jax.ShapeDtypeStruct((S5_T, n, 512), BF16)
    nl, ncx = B * N_CHUNK, B * N_CCHUNK
    return pl.pallas_call(
        _inproj0a_kernel,
        out_shape=(so(nl), so(nl), so(ncx)),
        grid=(BLK, S5_T // BLK),
        in_specs=[sspec(nl, D), sspec(ncx, D), pl.BlockSpec((D, SGU_COL0), lambda r, h: (0, 0))],
        out_specs=(sspec(nl, 512), sspec(nl, 512), sspec(ncx, 512)),
        compiler_params=_params("arbitrary", "arbitrary"),
        name="inproj0a",
    )(hs, hcs, w_in_f32)


SCAN_GROUPS = 4


def _scan_tiles(sre_ref, sim_ref, h_refs, n_tiles, carry, lams):
    row = lax.broadcasted_iota(jnp.int32, (8, 128), 0)
    lane = lax.broadcasted_iota(jnp.int32, (8, 128), 1)
    first = row < B
    fwd = lane < S5_P

    def body(k, c):
        of = pl.multiple_of(k * 8, 8)
        ob = pl.multiple_of((n_tiles - 1 - k) * 8, 8)
        out = []
        for gi in range(SCAN_GROUPS):
            lre, lim = lams[gi]
            hr, hi = c[2 * gi], c[2 * gi + 1]
            sr = jnp.where(fwd, sre_ref[gi, pl.ds(of, 8), :], pltpu.roll(sre_ref[gi, pl.ds(ob, 8), :], B, axis=0))
            si = jnp.where(fwd, sim_ref[gi, pl.ds(of, 8), :], pltpu.roll(sim_ref[gi, pl.ds(ob, 8), :], B, axis=0))
            h1r = lre * hr - lim * hi + sr
            h1i = lre * hi + lim * hr + si
            r1r = pltpu.roll(h1r, B, axis=0)
            r1i = pltpu.roll(h1i, B, axis=0)
            if h_refs is not None:
                fre_ref, fim_ref, bre_ref, bim_ref = h_refs
                er = jnp.where(first, hr, r1r)
                ei = jnp.where(first, hi, r1i)
                fre_ref[gi, pl.ds(of, 8), :] = er
                fim_ref[gi, pl.ds(of, 8), :] = ei
                bre_ref[gi, pl.ds(ob, 8), :] = pltpu.roll(er, B, axis=0)
                bim_ref[gi, pl.ds(ob, 8), :] = pltpu.roll(ei, B, axis=0)
            h2r = lre * r1r - lim * r1i + sr
            h2i = lre * r1i + lim * r1r + si
            out.append(jnp.where(first, pltpu.roll(h2r, B, axis=0), h2r))
            out.append(jnp.where(first, pltpu.roll(h2i, B, axis=0), h2i))
        return tuple(out)

    return lax.fori_loop(0, n_tiles, body, carry)


def _gather_group(slab_ref, src):
    halves = []
    for h in range(S5_T // BLK):
        acc = slab_ref[BLK * h]
        for s in range(1, BLK):
            acc = jnp.where(src == s, slab_ref[BLK * h + s], acc)
        halves.append(acc)
    return jnp.concatenate(halves, axis=1)


def _s5core_kernel(ul_ref, uc_ref, win_ref, wout_ref, mix_ref, l16_ref, o_ref,
                   u_ref, sre_ref, sim_ref, cre_ref, cim_ref, fre_ref, fim_ref, bre_ref, bim_ref, y_ref):
    nl = N_CHUNK * B
    ncx = N_CCHUNK * B
    blk_l = lax.broadcasted_iota(jnp.int32, (nl, 128), 1) >> H_SHIFT
    blk_c = lax.broadcasted_iota(jnp.int32, (ncx, 128), 1) >> H_SHIFT
    fwd = lax.broadcasted_iota(jnp.int32, (N_CHUNK, 128), 1) < S5_P
    for g0 in range(0, BLK, SCAN_GROUPS):
        for gi in range(SCAN_GROUPS):
            bg = g0 + gi
            win = win_ref[bg]
            src_l = ((blk_l - bg) & (BLK - 1)).astype(F32).astype(BF16)
            src_c = ((blk_c - bg) & (BLK - 1)).astype(F32).astype(BF16)
            u = _gather_group(ul_ref, src_l)
            u_ref[gi] = u
            sl = jnp.dot(u, win, preferred_element_type=F32)
            sc = jnp.dot(_gather_group(uc_ref, src_c), win, preferred_element_type=F32)
            for b in range(B):
                sre_ref[gi, pl.ds(b, N_CHUNK, stride=B), :] = sl[N_CHUNK * b:N_CHUNK * (b + 1), 0:128]
                sim_ref[gi, pl.ds(b, N_CHUNK, stride=B), :] = sl[N_CHUNK * b:N_CHUNK * (b + 1), 128:256]
                cre_ref[gi, pl.ds(b, N_CCHUNK, stride=B), :] = sc[N_CCHUNK * b:N_CCHUNK * (b + 1), 0:128]
                cim_ref[gi, pl.ds(b, N_CCHUNK, stride=B), :] = sc[N_CCHUNK * b:N_CCHUNK * (b + 1), 128:256]
        lams = [(jnp.broadcast_to(l16_ref[g0 + gi, 0:1, :], (8, 128)),
                 jnp.broadcast_to(l16_ref[g0 + gi, 1:2, :], (8, 128))) for gi in range(SCAN_GROUPS)]
        zero = tuple(jnp.zeros((8, 128), F32) for _ in range(2 * SCAN_GROUPS))
        carry = _scan_tiles(cre_ref, cim_ref, None, ncx // 8, zero, lams)
        _scan_tiles(sre_ref, sim_ref, (fre_ref, fim_ref, bre_ref, bim_ref), nl // 8, carry, lams)
        for gi in range(SCAN_GROUPS):
            bg = g0 + gi
            y = jnp.dot(u_ref[gi], mix_ref[bg], preferred_element_type=F32)
            hs = []
            for b in range(B):
                rows = pl.ds(b, N_CHUNK, stride=B)
                hs.append(jnp.concatenate([jnp.where(fwd, fre_ref[gi, rows, :], bre_ref[gi, rows, :]),
                                           jnp.where(fwd, fim_ref[gi, rows, :], bim_ref[gi, rows, :])], axis=1))
            hcat = jnp.concatenate(hs, axis=0).astype(BF16)
            y = y + jnp.dot(hcat, wout_ref[bg], preferred_element_type=F32)
            y_ref[bg] = y.astype(BF16)

    blk = blk_l.astype(F32).astype(BF16)
    for s in range(S5_T):
        h, r = s // BLK, s % BLK
        acc = None
        for j in range(BLK):
            piece = y_ref[(j - r) % BLK, :, 128 * h:128 * (h + 1)]
            acc = piece if acc is None else jnp.where(blk == j, piece, acc)
        o_ref[s] = acc


def _s5core(ul, uc, win, wout, mix, l16):
    TH = S5_T * S5_H
    nl = N_CHUNK * B
    ncx = N_CCHUNK * B
    g4 = lambda r, c: pl.BlockSpec((BLK, r, c), lambda q: (q, 0, 0))
    col = lambda n: pl.BlockSpec((S5_T, n, 128), lambda q: (0, 0, q))
    f32s = lambda n: pltpu.VMEM((SCAN_GROUPS, n, 128), F32)
    return pl.pallas_call(
        _s5core_kernel,
        out_shape=jax.ShapeDtypeStruct((S5_T, nl, S5_W), BF16),
        grid=(S5_G // BLK,),
        in_specs=[col(nl), col(ncx), g4(TH, TH), g4(TH, TH), g4(TH, TH), g4(8, 128)],
        out_specs=col(nl),
        scratch_shapes=[pltpu.VMEM((SCAN_GROUPS, nl, TH), BF16),
                        f32s(nl), f32s(nl), f32s(ncx), f32s(ncx), f32s(nl), f32s(nl), f32s(nl), f32s(nl),
                        pltpu.VMEM((BLK, nl, TH), BF16)],
        compiler_params=_params("arbitrary"),
        name="s5core",
    )(ul, uc, win, wout, mix, l16)


def _s5tail_kernel(slat_ref, sza_ref, gluw_ref, glub_ref, wtop_ref, y_ref):
    unrot = (BLK - pl.program_id(0)) & (BLK - 1)
    for b in range(B):
        rows = slice(N_CHUNK * b, N_CHUNK * (b + 1))
        g = _gelu(_rot_blocks(slat_ref[rows, :].astype(F32), unrot))
        gate = _sigmoid(jnp.dot(g.astype(BF16), gluw_ref[...], preferred_element_type=F32) + glub_ref[...])
        a = (g * gate * sza_ref[rows, :].astype(F32)).astype(BF16)
        y_ref[rows, :] = jnp.dot(a, wtop_ref[...], preferred_element_type=F32).astype(BF16)


def _s5tail(slat, sza, glu_w, glu_b, w_top):
    slab = lambda r, h: r + BLK * h
    sspec = lambda w: pl.BlockSpec((None, N_CHUNK * B, w), lambda r, h: (slab(r, h), 0, 0))
    full = lambda *s: pl.BlockSpec(s, lambda r, h: (0,) * len(s))
    return pl.pallas_call(
        _s5tail_kernel,
        out_shape=jax.ShapeDtypeStruct((S5_T, N_CHUNK * B, D), BF16),
        grid=(BLK, S5_T // BLK),
        in_specs=[sspec(512), sspec(512), full(512, 512), full(1, 512), full(S5_W, D)],
        out_specs=sspec(D),
        compiler_params=_params("arbitrary", "arbitrary"),
        name="s5tail",
    )(slat, sza, glu_w, glu_b, w_top)


def _tail0_kernel(x_ref, ys5_ref, guz_ref, vln_ref, mod_ref, sguw_ref, sgub_ref, wbot_ref, ng_ref, nb_ref, o_ref):
    tm = x_ref.shape[0]
    lane = lax.broadcasted_iota(jnp.int32, (SGU_CHUNK, 128), 1)
    lo = lane < SGU_HD
    zero = jnp.zeros((SGU_CHUNK, 128), BF16)
    chunks = []
    for ci in range(tm // SGU_CHUNK):
        v = vln_ref[ci * SGU_CHUNK:(ci + 1) * SGU_CHUNK, :]
        cols = []
        for pi in range(SGU_HEADS // 2):
            vp = v[:, 128 * pi:128 * (pi + 1)]
            bm = jnp.concatenate([jnp.where(lo, vp, zero), jnp.where(lo, zero, vp)], axis=0)
            cols.append(jnp.dot(sguw_ref[pi], bm, preferred_element_type=F32))
        chunks.append(jnp.concatenate(cols, axis=1) + sgub_ref[...])
    s = jnp.concatenate(chunks, axis=0)
    bsg = (guz_ref[...].astype(F32) * s).astype(BF16)
    perm = _chunk_transpose_perm()
    ys5 = jnp.concatenate(
        [jnp.dot(perm, ys5_ref[:, S5_T * j:S5_T * (j + 1), :].reshape(PERM_ROWS, D), preferred_element_type=F32)
         for j in range(tm // PERM_ROWS)], axis=0)
    y = ys5 + jnp.dot(bsg, wbot_ref[...], preferred_element_type=F32)
    gmod = mod_ref[:, 2 * D:3 * D]
    o_ref[...] = _layer_norm(DN_ALPHA * x_ref[...] + gmod * y, ng_ref[...], nb_ref[...])


def _tail0(x, ys5, guz, vln, mod, sguw, sgub, w_bot, ng, nb, tm=TOKEN_TILE):
    nct = tm // S5_T
    t512 = pl.BlockSpec((None, tm, 512), lambda b, i: (b, i, 0))
    tD = pl.BlockSpec((None, tm, D), lambda b, i: (b, i, 0))
    full = lambda *s: pl.BlockSpec(s, lambda b, i: (0,) * len(s))
    return pl.pallas_call(
        _tail0_kernel,
        out_shape=jax.ShapeDtypeStruct((B, L, D), F32),
        grid=(B, L // tm),
        in_specs=[tD, pl.BlockSpec((S5_T, nct, D), lambda b, i: (0, b * (N_CHUNK // nct) + i, 0)), t512, t512,
                  pl.BlockSpec((None, 1, 3 * D), lambda b, i: (b, 0, 0)),
                  full(SGU_HEADS // 2, SGU_CHUNK, 256), full(SGU_CHUNK, 512),
                  pl.BlockSpec((SGU_W, D), lambda b, i: (1, 0)), full(1, D), full(1, D)],
        out_specs=tD,
        compiler_params=_params("arbitrary", "arbitrary"),
        name="tail0",
    )(x, ys5, guz, vln, mod, sguw, sgub, w_bot, ng, nb)


def _inproj1_kernel(x_ref, mod_ref, w_ref, hg_ref, sz_ref):
    shift = mod_ref[:, 0:D]
    scale = mod_ref[:, D:2 * D]
    h = (x_ref[...] * (1.0 + scale) + shift).astype(BF16)
    dot = lambda lo: jnp.dot(h, w_ref[:, lo:lo + D].astype(BF16), preferred_element_type=F32)
    hg_ref[...] = (dot(0) * _sigmoid(dot(D))).astype(BF16)
    sz_ref[...] = _silu(dot(2 * D)).astype(BF16)


def _inproj1(x, mod, w_in_f32, tm=TOKEN_TILE):
    o = jax.ShapeDtypeStruct((B, L, D), BF16)
    ospec = pl.BlockSpec((None, tm, D), lambda b, i: (b, i, 0))
    return pl.pallas_call(
        _inproj1_kernel,
        out_shape=(o, o),
        grid=(B, L // tm),
        in_specs=[pl.BlockSpec((None, tm, D), lambda b, i: (b, i, 0)),
                  pl.BlockSpec((None, 1, 3 * D), lambda b, i: (b, 0, 0)),
                  pl.BlockSpec((D, ODD_IN), lambda b, i: (0, 0), pipeline_mode=pl.Buffered(1))],
        out_specs=(ospec, ospec),
        compiler_params=_params("arbitrary", "arbitrary"),
        name="inproj1",
    )(x, mod, w_in_f32)


ROW_PAD = 16
COL_PAD = CONV_HALF * GRID_W
CONV_ROWS = 128


def _conv_kernel(h_ref, w_ref, b_ref, o_ref, prow_ref, pcol_ref):
    j = pl.program_id(1)
    bias = b_ref[...]

    @pl.when(j < (D // 2) // 128)
    def _():
        zpad = jnp.zeros((GRID_W, ROW_PAD, 128), F32)
        prow_ref[:, 0:ROW_PAD, :] = zpad
        prow_ref[:, ROW_PAD + GRID_W:, :] = zpad
        prow_ref[:, ROW_PAD:ROW_PAD + GRID_W, :] = h_ref[...].astype(F32).reshape(GRID_W, GRID_W, 128)

        def body(r, carry):
            acc = jnp.zeros((GRID_W, 128), F32) + bias
            for k in range(CONV_K):
                off = ROW_PAD - CONV_HALF + k
                acc = acc + w_ref[k:k + 1, :] * prow_ref[r, off:off + GRID_W, :]
            o_ref[pl.ds(pl.multiple_of(r * GRID_W, GRID_W), GRID_W), :] = acc.astype(BF16)
            return carry

        lax.fori_loop(0, GRID_W, body, 0, unroll=2)

    @pl.when(j >= (D // 2) // 128)
    def _():
        zpad = jnp.zeros((COL_PAD, 128), F32)
        pcol_ref[0:COL_PAD, :] = zpad
        pcol_ref[COL_PAD + L:, :] = zpad
        pcol_ref[COL_PAD:COL_PAD + L, :] = h_ref[...].astype(F32)

        def body(i, carry):
            base = pl.multiple_of(i * CONV_ROWS, CONV_ROWS)
            acc = jnp.zeros((CONV_ROWS, 128), F32) + bias
            for k in range(CONV_K):
                acc = acc + w_ref[k:k + 1, :] * pcol_ref[pl.ds(base + k * GRID_W, CONV_ROWS), :]
            o_ref[pl.ds(base, CONV_ROWS), :] = acc.astype(BF16)
            return carry

        lax.fori_loop(0, L // CONV_ROWS, body, 0)


def _conv(hg, dw_w, dw_b):
    return pl.pallas_call(
        _conv_kernel,
        out_shape=jax.ShapeDtypeStruct((B, L, D), BF16),
        grid=(B, D // 128),
        in_specs=[pl.BlockSpec((None, L, 128), lambda b, j: (b, 0, j)),
                  pl.BlockSpec((CONV_K, 128), lambda b, j: (0, j)),
                  pl.BlockSpec((1, 128), lambda b, j: (0, j))],
        out_specs=pl.BlockSpec((None, L, 128), lambda b, j: (b, 0, j)),
        scratch_shapes=[pltpu.VMEM((GRID_W, GRID_W + 2 * ROW_PAD, 128), F32),
                        pltpu.VMEM((L + 2 * COL_PAD, 128), F32)],
        compiler_params=_params("arbitrary", "arbitrary"),
        name="dwconv",
    )(hg, dw_w, dw_b)


def _tail1_kernel(x_ref, hc_ref, sz_ref, mod_ref, lng_ref, lnb_ref, wout_ref, ng_ref, nb_ref, o_ref):
    m = _silu(_layer_norm(hc_ref[...].astype(F32), lng_ref[...], lnb_ref[...])) * sz_ref[...].astype(F32)
    y = jnp.dot(m.astype(BF16), wout_ref[...], preferred_element_type=F32)
    gmod = mod_ref[:, 2 * D:3 * D]
    o_ref[...] = _layer_norm(DN_ALPHA * x_ref[...] + gmod * y, ng_ref[...], nb_ref[...])


def _tail1(x, hc, sz, mod, ln_g, ln_b, w_out, ng, nb, tm=TOKEN_TILE):
    tD = pl.BlockSpec((None, tm, D), lambda b, i: (b, i, 0))
    full = lambda *s: pl.BlockSpec(s, lambda b, i: (0,) * len(s))
    return pl.pallas_call(
        _tail1_kernel,
        out_shape=jax.ShapeDtypeStruct((B, L, D), F32),
        grid=(B, L // tm),
        in_specs=[tD, tD, tD, pl.BlockSpec((None, 1, 3 * D), lambda b, i: (b, 0, 0)),
                  full(1, D), full(1, D), full(D, D), full(1, D), full(1, D)],
        out_specs=tD,
        compiler_params=_params("arbitrary", "arbitrary"),
        name="tail1",
    )(x, hc, sz, mod, ln_g, ln_b, w_out, ng, nb)


def kernel(x, c, ctx, c_ctx, mod_w, mod_b, norm_g, norm_b, ev_w_in, ev_w_out, s5_lam_re, s5_lam_im, s5_log_dt, s5_b_re, s5_b_im, s5_c_re, s5_c_im, s5_d, glu_w, glu_b, sgu_ln_g, sgu_ln_b, sgu_w, sgu_b, od_w_in, od_w_out, dw_w, dw_b, conv_ln_g, conv_ln_b):
    TH = S5_T * S5_H
    row = lambda v: v.reshape(1, -1)

    cond8 = jnp.concatenate([c, c_ctx[None], jnp.zeros((3, D), F32)], axis=0)
    mods = _adaln(cond8, mod_w, mod_b)
    mod0 = mods[0, :B].reshape(B, 1, 3 * D)
    mod0c = mods[0, B:B + 1]
    mod1 = mods[1, :B].reshape(B, 1, 3 * D)

    lbr, lbi, cfr, cfi = _discretise(s5_lam_re[0], s5_lam_im[0], s5_log_dt[0])
    rowcat = lambda a: jnp.concatenate([a[0], a[1]], axis=-1).reshape(S5_G, 1, 2 * S5_P)
    bt = lambda a: jnp.concatenate([jnp.swapaxes(a[0], 1, 2), jnp.swapaxes(a[1], 1, 2)], axis=-1)
    cn = lambda a: jnp.concatenate([a[0], a[1]], axis=-1)
    d_row = jnp.tile(s5_d[0].reshape(S5_G, 1, S5_H), (1, 1, S5_T))
    win, wout, mix, l16 = _s5_weights(rowcat(lbr), rowcat(lbi), rowcat(cfr), rowcat(cfi),
                                      bt(s5_b_re[0]), bt(s5_b_im[0]), cn(s5_c_re[0]), cn(s5_c_im[0]), d_row)

    w_sgu = ev_w_in[0][:, SGU_COL0:].astype(BF16)
    w_out0 = ev_w_out[0].astype(BF16)
    glu_w0 = glu_w[0].astype(BF16)
    guz, vln, hs = _inproj0n(x, mod0, w_sgu, row(sgu_ln_g[0]), row(sgu_ln_b[0]))
    ua, sza, ua_c = _inproj0a(hs, _ctx_slabs(ctx, mod0c), ev_w_in[0])
    s_lat = _s5core(ua, ua_c, win, wout, mix, l16)
    y_s5 = _s5tail(s_lat, sza, glu_w0, row(glu_b[0]), w_out0)
    sguw = sgu_w[0].reshape(SGU_HEADS // 2, 2, SGU_CHUNK, SGU_CHUNK)
    sguw = jnp.transpose(sguw, (0, 2, 1, 3)).reshape(SGU_HEADS // 2, SGU_CHUNK, 2 * SGU_CHUNK).astype(BF16)
    sgub = jnp.repeat(sgu_b[0].T, SGU_HD, axis=1)
    x1 = _tail0(x, y_s5, guz, vln, mod0, sguw, sgub, w_out0, row(norm_g[0]), row(norm_b[0]))

    hg, sz = _inproj1(x1, mod1, od_w_in[0])
    hc = _conv(hg, dw_w[0], row(dw_b[0]))
    return _tail1(x1, hc, sz, mod1, row(conv_ln_g[0]), row(conv_ln_b[0]), od_w_out[0].astype(BF16),
                  row(norm_g[1]), row(norm_b[1]))
```

```python
import functools
import math

import jax
import jax.numpy as jnp
from jax import lax
from jax.experimental import pallas as pl
from jax.experimental.pallas import tpu as pltpu

D = 1024
B = 4
L = 4096
CTX = 256
GRID_W = 64
S5_W = 512
S5_G = 32
S5_H = 16
H_SHIFT = 4
BLK = 128 // S5_H
S5_P = 64
S5_T = 16
SGU_W = 512
SGU_HEADS = 8
SGU_HD = 64
SGU_CHUNK = 128
CONV_K = 31
CONV_HALF = CONV_K // 2
EVEN_IN = 2560
SGU_COL0 = 2 * S5_W
ODD_IN = 3072
DEPTH = 2
DN_ALPHA = (2 * DEPTH) ** 0.25
LN_EPS = 1e-5
N_CHUNK = L // S5_T
N_CCHUNK = CTX // S5_T
VMEM_LIMIT_V7X = 56 * 1024 * 1024
TOKEN_TILE = 1024

F32 = jnp.float32
BF16 = jnp.bfloat16


GELU_C = math.sqrt(2.0 / math.pi)


def _gelu(x):
    hx = 0.5 * x
    return hx * jnp.tanh(x * ((x * x) * (0.044715 * GELU_C) + GELU_C)) + hx


def _sigmoid(x):
    return 0.5 * jnp.tanh(0.5 * x) + 0.5


def _silu(x):
    hx = 0.5 * x
    return hx * jnp.tanh(hx) + hx


def _layer_norm(x, g, b):
    mu = jnp.mean(x, axis=-1, keepdims=True)
    xc = x - mu
    var = jnp.mean(xc * xc, axis=-1, keepdims=True)
    return xc * lax.rsqrt(var + LN_EPS) * g + b


def _params(*sem):
    return pltpu.CompilerParams(dimension_semantics=sem, vmem_limit_bytes=VMEM_LIMIT_V7X)


def _adaln_kernel(c_ref, w_ref, b_ref, o_ref):
    def split(v):
        hi = v.astype(BF16)
        return hi, (v - hi.astype(F32)).astype(BF16)

    a_hi, a_lo = split(_silu(c_ref[...]))
    w_hi, w_lo = split(w_ref[...])
    dot = functools.partial(jnp.dot, preferred_element_type=F32)
    o_ref[...] = dot(a_hi, w_hi) + dot(a_lo, w_hi) + dot(a_hi, w_lo) + b_ref[...]


def _adaln(cond8, mod_w, mod_b):
    tn = 512
    return pl.pallas_call(
        _adaln_kernel,
        out_shape=jax.ShapeDtypeStruct((DEPTH, 8, 3 * D), F32),
        grid=(DEPTH, 3 * D // tn),
        in_specs=[pl.BlockSpec((8, D), lambda l, j: (0, 0)),
                  pl.BlockSpec((None, D, tn), lambda l, j: (l, 0, j)),
                  pl.BlockSpec((None, 1, tn), lambda l, j: (l, 0, j))],
        out_specs=pl.BlockSpec((None, 8, tn), lambda l, j: (l, 0, j)),
        compiler_params=_params("arbitrary", "arbitrary"),
        name="adaln",
    )(cond8, mod_w, mod_b.reshape(DEPTH, 1, 3 * D))


def _disc_kernel(lr_ref, li_ref, ldt_ref, obr_ref, obi_ref, ocr_ref, oci_ref):
    lr = lr_ref[...]
    li = li_ref[...]
    dt = jnp.exp(ldt_ref[...])
    mag = jnp.exp(lr * dt)
    br = mag * jnp.cos(li * dt)
    bi = mag * jnp.sin(li * dt)
    inv = 1.0 / (lr * lr + li * li)
    nr = br - 1.0
    obr_ref[...] = br
    obi_ref[...] = bi
    ocr_ref[...] = (nr * lr + bi * li) * inv
    oci_ref[...] = (bi * lr - nr * li) * inv


def _discretise(lam_re, lam_im, log_dt):
    shp = jax.ShapeDtypeStruct((2 * S5_G, S5_P), F32)
    ldt = jnp.broadcast_to(log_dt.reshape(2 * S5_G, 1), (2 * S5_G, S5_P))
    outs = pl.pallas_call(
        _disc_kernel, out_shape=(shp, shp, shp, shp), name="s5_discretise",
    )(lam_re.reshape(2 * S5_G, S5_P), lam_im.reshape(2 * S5_G, S5_P), ldt)
    return [o.reshape(2, S5_G, S5_P) for o in outs]


S5W_GROUPS = 4


def _cpow(base_pows, j):
    re = None
    im = None
    for k, (pr, pi) in enumerate(base_pows):
        bit = ((j >> k) & 1) == 1
        mr = jnp.where(bit, pr, 1.0)
        mi = jnp.where(bit, pi, 0.0)
        if re is None:
            re, im = mr, mi
        else:
            re, im = re * mr - im * mi, re * mi + im * mr
    return re, im


def _squarings(pr, pi, n):
    out = [(pr, pi)]
    for _ in range(n - 1):
        pr, pi = pr * pr - pi * pi, 2.0 * pr * pi
        out.append((pr, pi))
    return out


def _shift_lanes(x, n):
    lane = lax.broadcasted_iota(jnp.int32, (S5_H, 128), 1)
    lo, hi = x[:, :128], x[:, 128:]
    if n == 0:
        return x
    if n < 128:
        rlo = pltpu.roll(lo, n, axis=1)
        rhi = pltpu.roll(hi, n, axis=1)
        return jnp.concatenate([jnp.where(lane >= n, rlo, 0.0), jnp.where(lane >= n, rhi, rlo)], axis=1)
    m = n - 128
    rlo = lo if m == 0 else pltpu.roll(lo, m, axis=1)
    return jnp.concatenate([jnp.zeros_like(lo), jnp.where(lane >= m, rlo, 0.0)], axis=1)


def _unshift_lanes(x, n):
    lane = lax.broadcasted_iota(jnp.int32, (S5_H, 128), 1)
    lo, hi = x[:, :128], x[:, 128:]
    if n == 0:
        return x
    if n < 128:
        rlo = pltpu.roll(lo, 128 - n, axis=1)
        rhi = pltpu.roll(hi, 128 - n, axis=1)
        keep = lane < 128 - n
        return jnp.concatenate([jnp.where(keep, rlo, rhi), jnp.where(keep, rhi, 0.0)], axis=1)
    m = n - 128
    rhi = hi if m == 0 else pltpu.roll(hi, 128 - m, axis=1)
    return jnp.concatenate([jnp.where(lane < 128 - m, rhi, 0.0), jnp.zeros_like(lo)], axis=1)


def _s5w_group(gi, bg, lrow_re, lrow_im, crow_re, crow_im, bt_re, bt_im,
               cn_re, cn_im, d_ref, win_ref, wout_ref, mix_ref, l16_ref):
    TH = S5_T * S5_H

    def chunk_pos(idx):
        return (((idx >> H_SHIFT) - bg) & (BLK - 1)) + ((idx >> 7) << 3)

    lr = lrow_re[gi]
    li = lrow_im[gi]
    pows_row = _squarings(lr, li, 5)
    l16_ref[gi, 0:1, :] = pows_row[4][0]
    l16_ref[gi, 1:2, :] = pows_row[4][1]
    l16_ref[gi, 2:8, :] = jnp.zeros((6, 128), F32)
    cr = crow_re[gi]
    ci = crow_im[gi]
    btr = bt_re[gi]
    bti = bt_im[gi]
    bbr = cr * btr - ci * bti
    bbi = cr * bti + ci * btr
    s_idx = chunk_pos(lax.broadcasted_iota(jnp.int32, (TH, 128), 0))
    lane = lax.broadcasted_iota(jnp.int32, (TH, 128), 1)
    jw = jnp.where(lane < S5_P, S5_T - 1 - s_idx, s_idx)
    pr, pi = _cpow(pows_row[:4], jw)
    tbr = jnp.broadcast_to(bbr[None], (S5_T, S5_H, 128)).reshape(TH, 128)
    tbi = jnp.broadcast_to(bbi[None], (S5_T, S5_H, 128)).reshape(TH, 128)
    win_ref[gi, :, 0:128] = (pr * tbr - pi * tbi).astype(BF16)
    win_ref[gi, :, 128:256] = (pr * tbi + pi * tbr).astype(BF16)

    def col256(r):
        col = jnp.broadcast_to(r, (2 * S5_P, 2 * S5_P)).T
        return jnp.concatenate([col, col], axis=1)

    def tiled_t(cn):
        t8 = jnp.broadcast_to(cn[None], (BLK, S5_H, 2 * S5_P)).reshape(2 * S5_P, 2 * S5_P).T
        return jnp.concatenate([t8, t8], axis=1)

    cpows = _squarings(col256(lr), col256(li), 4)
    row = lax.broadcasted_iota(jnp.int32, (2 * S5_P, TH), 0)
    lane_w = lax.broadcasted_iota(jnp.int32, (2 * S5_P, TH), 1)
    t_idx = chunk_pos(lane_w)
    j_idx = lane_w >> H_SHIFT
    is_f = row < S5_P
    ctr = tiled_t(cn_re[gi])
    cti = tiled_t(cn_im[gi])
    er, ei = _cpow(cpows, jnp.where(is_f, t_idx, S5_T - 1 - t_idx))
    er, ei = er * cpows[0][0] - ei * cpows[0][1], er * cpows[0][1] + ei * cpows[0][0]
    wr = ctr * er - cti * ei
    wi = ctr * ei + cti * er
    wout_ref[gi, 0:128, :] = wr.astype(BF16)
    wout_ref[gi, 128:256, :] = (-wi).astype(BF16)
    kr, ki = _cpow(cpows, jnp.where(is_f, j_idx, S5_T - 1 - j_idx))
    ekr = ctr * kr - cti * ki
    eki = ctr * ki + cti * kr
    lane16 = lax.broadcasted_iota(jnp.int32, (S5_H, 128), 1)
    mf = lane16 < S5_P
    hp = lax.Precision.HIGHEST
    dot = functools.partial(jnp.dot, preferred_element_type=F32, precision=hp)
    kkf = dot(jnp.where(mf, bbr, 0.0), ekr) - dot(jnp.where(mf, bbi, 0.0), eki)
    kkb = dot(jnp.where(mf, 0.0, bbr), ekr) - dot(jnp.where(mf, 0.0, bbi), eki)
    dl = d_ref[gi]
    r16 = lax.broadcasted_iota(jnp.int32, (S5_H, TH), 0)
    l256 = lax.broadcasted_iota(jnp.int32, (S5_H, TH), 1)
    rot = bg * S5_H
    for s in range(S5_T):
        blk = _shift_lanes(kkf, S5_H * s) + _unshift_lanes(kkb, S5_H * (S5_T - 1 - s))
        blk = blk + jnp.where(l256 == r16 + S5_H * s, dl, 0.0)
        blk = jnp.concatenate([pltpu.roll(blk[:, :128], rot, axis=1), pltpu.roll(blk[:, 128:], rot, axis=1)], axis=1)
        rho = ((s + bg) & (BLK - 1)) + (s & BLK)
        mix_ref[gi, pl.ds(pl.multiple_of(rho * S5_H, S5_H), S5_H), :] = blk.astype(BF16)


def _s5w_kernel(*refs):
    for gi in range(S5W_GROUPS):
        bg = (pl.program_id(0) * S5W_GROUPS + gi) & (BLK - 1)
        _s5w_group(gi, bg, *refs)


def _s5_weights(lrow_re, lrow_im, crow_re, crow_im, bt_re, bt_im, cn_re, cn_im, d_row):
    TH = S5_T * S5_H
    g3 = lambda r, c: pl.BlockSpec((S5W_GROUPS, r, c), lambda g: (g, 0, 0))
    wshape = jax.ShapeDtypeStruct((S5_G, TH, TH), BF16)
    return pl.pallas_call(
        _s5w_kernel,
        out_shape=(wshape, wshape, wshape, jax.ShapeDtypeStruct((S5_G, 8, 128), F32)),
        grid=(S5_G // S5W_GROUPS,),
        in_specs=[g3(1, 128)] * 4 + [g3(S5_H, 128)] * 4 + [g3(1, TH)],
        out_specs=(g3(TH, TH), g3(TH, TH), g3(TH, TH), g3(8, 128)),
        compiler_params=_params("arbitrary"),
        name="s5_weights",
    )(lrow_re, lrow_im, crow_re, crow_im, bt_re, bt_im, cn_re, cn_im, d_row)


def _rot_blocks(v, r):
    cols = [pltpu.roll(v[:, 128 * q:128 * (q + 1)], S5_H * r, axis=1) for q in range(v.shape[1] // 128)]
    return jnp.concatenate(cols, axis=1)


def _slabs_of(h, hs_ref):
    h3 = h.reshape(h.shape[0] // S5_T, S5_T, h.shape[1])
    for s in range(S5_T):
        hs_ref[s] = h3[:, s, :].astype(BF16)


PERM_ROWS = S5_T * S5_T


def _chunk_transpose_perm():
    ri = lax.broadcasted_iota(jnp.int32, (PERM_ROWS, PERM_ROWS), 0)
    ci = lax.broadcasted_iota(jnp.int32, (PERM_ROWS, PERM_ROWS), 1)
    hit = ((ri >> H_SHIFT) == (ci & (S5_T - 1))) & ((ri & (S5_T - 1)) == (ci >> H_SHIFT))
    return jnp.where(hit, 1.0, 0.0).astype(BF16)


def _inproj0n_kernel(x_ref, mod_ref, w_ref, lng_ref, lnb_ref, guz_ref, vln_ref, hs_ref):
    shift = mod_ref[:, 0:D]
    scale = mod_ref[:, D:2 * D]
    hb = (x_ref[...] * (1.0 + scale) + shift).astype(BF16)
    perm = _chunk_transpose_perm()
    for j in range(hb.shape[0] // PERM_ROWS):
        blk = jnp.dot(perm, hb[PERM_ROWS * j:PERM_ROWS * (j + 1), :], preferred_element_type=F32).astype(BF16)
        for s in range(S5_T):
            hs_ref[s, S5_T * j:S5_T * (j + 1), :] = blk[S5_T * s:S5_T * (s + 1), :]
    dot = lambda lo: jnp.dot(hb, w_ref[:, lo:lo + 512], preferred_element_type=F32)
    guz_ref[...] = (_gelu(dot(0)) * _silu(dot(1024))).astype(BF16)
    vln_ref[...] = _layer_norm(_gelu(dot(512)), lng_ref[...], lnb_ref[...]).astype(BF16)


def _inproj0n(x, mod, w_sgu, ln_g, ln_b, tm=TOKEN_TILE):
    nct = tm // S5_T
    o = jax.ShapeDtypeStruct((B, L, 512), BF16)
    ospec = pl.BlockSpec((None, tm, 512), lambda b, i: (b, i, 0))
    full = lambda *s: pl.BlockSpec(s, lambda b, i: (0,) * len(s))
    return pl.pallas_call(
        _inproj0n_kernel,
        out_shape=(o, o, jax.ShapeDtypeStruct((S5_T, B * N_CHUNK, D), BF16)),
        grid=(B, L // tm),
        in_specs=[pl.BlockSpec((None, tm, D), lambda b, i: (b, i, 0)),
                  pl.BlockSpec((None, 1, 3 * D), lambda b, i: (b, 0, 0)),
                  full(D, EVEN_IN - SGU_COL0), full(1, 512), full(1, 512)],
        out_specs=(ospec, ospec,
                   pl.BlockSpec((S5_T, nct, D), lambda b, i: (0, b * (N_CHUNK // nct) + i, 0))),
        compiler_params=_params("arbitrary", "arbitrary"),
        name="inproj0n",
    )(x, mod, w_sgu, ln_g, ln_b)


def _ctx_slabs_kernel(x_ref, mod_ref, hs_ref):
    h = x_ref[...] * (1.0 + mod_ref[:, D:2 * D]) + mod_ref[:, 0:D]
    _slabs_of(h, hs_ref)


def _ctx_slabs(ctx, mod_c):
    return pl.pallas_call(
        _ctx_slabs_kernel,
        out_shape=jax.ShapeDtypeStruct((S5_T, B * N_CCHUNK, D), BF16),
        grid=(B,),
        in_specs=[pl.BlockSpec((None, CTX, D), lambda b: (b, 0, 0)),
                  pl.BlockSpec((1, 3 * D), lambda b: (0, 0))],
        out_specs=pl.BlockSpec((S5_T, N_CCHUNK, D), lambda b: (0, b, 0)),
        compiler_params=_params("arbitrary"),
        name="ctx_slabs",
    )(ctx, mod_c)


def _inproj0a_kernel(hs_ref, hc_ref, w_ref, ua_ref, sza_ref, uc_ref):
    r = pl.program_id(0)
    h = hs_ref[...]
    w_ua = w_ref[:, 0:512].astype(BF16)
    ua_ref[...] = _rot_blocks(jnp.dot(h, w_ua, preferred_element_type=F32), r).astype(BF16)
    sza_ref[...] = _silu(jnp.dot(h, w_ref[:, 512:1024].astype(BF16), preferred_element_type=F32)).astype(BF16)
    uc_ref[...] = _rot_blocks(jnp.dot(hc_ref[...], w_ua, preferred_element_type=F32), r).astype(BF16)


def _inproj0a(hs, hcs, w_in_f32):
    slab = lambda r, h: r + BLK * h
    sspec = lambda n, w: pl.BlockSpec((None, n, w), lambda r, h: (slab(r, h), 0, 0))
    so = lambda n: jax.ShapeDtypeStruct((S5_T, n, 512), BF16)
    nl, ncx = B * N_CHUNK, B * N_CCHUNK
    return pl.pallas_call(
        _inproj0a_kernel,
        out_shape=(so(nl), so(nl), so(ncx)),
        grid=(BLK, S5_T // BLK),
        in_specs=[sspec(nl, D), sspec(ncx, D), pl.BlockSpec((D, SGU_COL0), lambda r, h: (0, 0))],
        out_specs=(sspec(nl, 512), sspec(nl, 512), sspec(ncx, 512)),
        compiler_params=_params("arbitrary", "arbitrary"),
        name="inproj0a",
    )(hs, hcs, w_in_f32)


SCAN_GROUPS = 4


def _scan_tiles(sre_ref, sim_ref, h_refs, n_tiles, carry, lams):
    row = lax.broadcasted_iota(jnp.int32, (8, 128), 0)
    lane = lax.broadcasted_iota(jnp.int32, (8, 128), 1)
    first = row < B
    fwd = lane < S5_P

    def body(k, c):
        of = pl.multiple_of(k * 8, 8)
        ob = pl.multiple_of((n_tiles - 1 - k) * 8, 8)
        out = []
        for gi in range(SCAN_GROUPS):
            lre, lim = lams[gi]
            hr, hi = c[2 * gi], c[2 * gi + 1]
            sr = jnp.where(fwd, sre_ref[gi, pl.ds(of, 8), :], pltpu.roll(sre_ref[gi, pl.ds(ob, 8), :], B, axis=0))
            si = jnp.where(fwd, sim_ref[gi, pl.ds(of, 8), :], pltpu.roll(sim_ref[gi, pl.ds(ob, 8), :], B, axis=0))
            h1r = lre * hr - lim * hi + sr
            h1i = lre * hi + lim * hr + si
            r1r = pltpu.roll(h1r, B, axis=0)
            r1i = pltpu.roll(h1i, B, axis=0)
            if h_refs is not None:
                fre_ref, fim_ref, bre_ref, bim_ref = h_refs
                er = jnp.where(first, hr, r1r)
                ei = jnp.where(first, hi, r1i)
                fre_ref[gi, pl.ds(of, 8), :] = er
                fim_ref[gi, pl.ds(of, 8), :] = ei
                bre_ref[gi, pl.ds(ob, 8), :] = pltpu.roll(er, B, axis=0)
                bim_ref[gi, pl.ds(ob, 8), :] = pltpu.roll(ei, B, axis=0)
            h2r = lre * r1r - lim * r1i + sr
            h2i = lre * r1i + lim * r1r + si
            out.append(jnp.where(first, pltpu.roll(h2r, B, axis=0), h2r))
            out.append(jnp.where(first, pltpu.roll(h2i, B, axis=0), h2i))
        return tuple(out)

    return lax.fori_loop(0, n_tiles, body, carry)


def _gather_group(slab_ref, src):
    halves = []
    for h in range(S5_T // BLK):
        acc = slab_ref[BLK * h]
        for s in range(1, BLK):
            acc = jnp.where(src == s, slab_ref[BLK * h + s], acc)
        halves.append(acc)
    return jnp.concatenate(halves, axis=1)


def _s5core_kernel(ul_ref, uc_ref, win_ref, wout_ref, mix_ref, l16_ref, o_ref,
                   u_ref, sre_ref, sim_ref, cre_ref, cim_ref, fre_ref, fim_ref, bre_ref, bim_ref, y_ref):
    nl = N_CHUNK * B
    ncx = N_CCHUNK * B
    blk_l = lax.broadcasted_iota(jnp.int32, (nl, 128), 1) >> H_SHIFT
    blk_c = lax.broadcasted_iota(jnp.int32, (ncx, 128), 1) >> H_SHIFT
    fwd = lax.broadcasted_iota(jnp.int32, (N_CHUNK, 128), 1) < S5_P
    for g0 in range(0, BLK, SCAN_GROUPS):
        for gi in range(SCAN_GROUPS):
            bg = g0 + gi
            win = win_ref[bg]
            src_l = ((blk_l - bg) & (BLK - 1)).astype(F32).astype(BF16)
            src_c = ((blk_c - bg) & (BLK - 1)).astype(F32).astype(BF16)
            u = _gather_group(ul_ref, src_l)
            u_ref[gi] = u
            sl = jnp.dot(u, win, preferred_element_type=F32)
            sc = jnp.dot(_gather_group(uc_ref, src_c), win, preferred_element_type=F32)
            for b in range(B):
                sre_ref[gi, pl.ds(b, N_CHUNK, stride=B), :] = sl[N_CHUNK * b:N_CHUNK * (b + 1), 0:128]
                sim_ref[gi, pl.ds(b, N_CHUNK, stride=B), :] = sl[N_CHUNK * b:N_CHUNK * (b + 1), 128:256]
                cre_ref[gi, pl.ds(b, N_CCHUNK, stride=B), :] = sc[N_CCHUNK * b:N_CCHUNK * (b + 1), 0:128]
                cim_ref[gi, pl.ds(b, N_CCHUNK, stride=B), :] = sc[N_CCHUNK * b:N_CCHUNK * (b + 1), 128:256]
        lams = [(jnp.broadcast_to(l16_ref[g0 + gi, 0:1, :], (8, 128)),
                 jnp.broadcast_to(l16_ref[g0 + gi, 1:2, :], (8, 128))) for gi in range(SCAN_GROUPS)]
        zero = tuple(jnp.zeros((8, 128), F32) for _ in range(2 * SCAN_GROUPS))
        carry = _scan_tiles(cre_ref, cim_ref, None, ncx // 8, zero, lams)
        _scan_tiles(sre_ref, sim_ref, (fre_ref, fim_ref, bre_ref, bim_ref), nl // 8, carry, lams)
        for gi in range(SCAN_GROUPS):
            bg = g0 + gi
            y = jnp.dot(u_ref[gi], mix_ref[bg], preferred_element_type=F32)
            hs = []
            for b in range(B):
                rows = pl.ds(b, N_CHUNK, stride=B)
                hs.append(jnp.concatenate([jnp.where(fwd, fre_ref[gi, rows, :], bre_ref[gi, rows, :]),
                                           jnp.where(fwd, fim_ref[gi, rows, :], bim_ref[gi, rows, :])], axis=1))
            hcat = jnp.concatenate(hs, axis=0).astype(BF16)
            y = y + jnp.dot(hcat, wout_ref[bg], preferred_element_type=F32)
            y_ref[bg] = y.astype(BF16)

    blk = blk_l.astype(F32).astype(BF16)
    for s in range(S5_T):
        h, r = s // BLK, s % BLK
        acc = None
        for j in range(BLK):
            piece = y_ref[(j - r) % BLK, :, 128 * h:128 * (h + 1)]
            acc = piece if acc is None else jnp.where(blk == j, piece, acc)
        o_ref[s] = acc


def _s5core(ul, uc, win, wout, mix, l16):
    TH = S5_T * S5_H
    nl = N_CHUNK * B
    ncx = N_CCHUNK * B
    g4 = lambda r, c: pl.BlockSpec((BLK, r, c), lambda q: (q, 0, 0))
    col = lambda n: pl.BlockSpec((S5_T, n, 128), lambda q: (0, 0, q))
    f32s = lambda n: pltpu.VMEM((SCAN_GROUPS, n, 128), F32)
    return pl.pallas_call(
        _s5core_kernel,
        out_shape=jax.ShapeDtypeStruct((S5_T, nl, S5_W), BF16),
        grid=(S5_G // BLK,),
        in_specs=[col(nl), col(ncx), g4(TH, TH), g4(TH, TH), g4(TH, TH), g4(8, 128)],
        out_specs=col(nl),
        scratch_shapes=[pltpu.VMEM((SCAN_GROUPS, nl, TH), BF16),
                        f32s(nl), f32s(nl), f32s(ncx), f32s(ncx), f32s(nl), f32s(nl), f32s(nl), f32s(nl),
                        pltpu.VMEM((BLK, nl, TH), BF16)],
        compiler_params=_params("arbitrary"),
        name="s5core",
    )(ul, uc, win, wout, mix, l16)


def _s5tail_kernel(slat_ref, sza_ref, gluw_ref, glub_ref, wtop_ref, y_ref):
    unrot = (BLK - pl.program_id(0)) & (BLK - 1)
    for b in range(B):
        rows = slice(N_CHUNK * b, N_CHUNK * (b + 1))
        g = _gelu(_rot_blocks(slat_ref[rows, :].astype(F32), unrot))
        gate = _sigmoid(jnp.dot(g.astype(BF16), gluw_ref[...], preferred_element_type=F32) + glub_ref[...])
        a = (g * gate * sza_ref[rows, :].astype(F32)).astype(BF16)
        y_ref[rows, :] = jnp.dot(a, wtop_ref[...], preferred_element_type=F32).astype(BF16)


def _s5tail(slat, sza, glu_w, glu_b, w_top):
    slab = lambda r, h: r + BLK * h
    sspec = lambda w: pl.BlockSpec((None, N_CHUNK * B, w), lambda r, h: (slab(r, h), 0, 0))
    full = lambda *s: pl.BlockSpec(s, lambda r, h: (0,) * len(s))
    return pl.pallas_call(
        _s5tail_kernel,
        out_shape=jax.ShapeDtypeStruct((S5_T, N_CHUNK * B, D), BF16),
        grid=(BLK, S5_T // BLK),
        in_specs=[sspec(512), sspec(512), full(512, 512), full(1, 512), full(S5_W, D)],
        out_specs=sspec(D),
        compiler_params=_params("arbitrary", "arbitrary"),
        name="s5tail",
    )(slat, sza, glu_w, glu_b, w_top)


def _tail0_kernel(x_ref, ys5_ref, guz_ref, vln_ref, mod_ref, sguw_ref, sgub_ref, wbot_ref, ng_ref, nb_ref, o_ref):
    tm = x_ref.shape[0]
    lane = lax.broadcasted_iota(jnp.int32, (SGU_CHUNK, 128), 1)
    lo = lane < SGU_HD
    zero = jnp.zeros((SGU_CHUNK, 128), BF16)
    chunks = []
    for ci in range(tm // SGU_CHUNK):
        v = vln_ref[ci * SGU_CHUNK:(ci + 1) * SGU_CHUNK, :]
        cols = []
        for pi in range(SGU_HEADS // 2):
            vp = v[:, 128 * pi:128 * (pi + 1)]
            bm = jnp.concatenate([jnp.where(lo, vp, zero), jnp.where(lo, zero, vp)], axis=0)
            cols.append(jnp.dot(sguw_ref[pi], bm, preferred_element_type=F32))
        chunks.append(jnp.concatenate(cols, axis=1) + sgub_ref[...])
    s = jnp.concatenate(chunks, axis=0)
    bsg = (guz_ref[...].astype(F32) * s).astype(BF16)
    perm = _chunk_transpose_perm()
    ys5 = jnp.concatenate(
        [jnp.dot(perm, ys5_ref[:, S5_T * j:S5_T * (j + 1), :].reshape(PERM_ROWS, D), preferred_element_type=F32)
         for j in range(tm // PERM_ROWS)], axis=0)
    y = ys5 + jnp.dot(bsg, wbot_ref[...], preferred_element_type=F32)
    gmod = mod_ref[:, 2 * D:3 * D]
    o_ref[...] = _layer_norm(DN_ALPHA * x_ref[...] + gmod * y, ng_ref[...], nb_ref[...])


def _tail0(x, ys5, guz, vln, mod, sguw, sgub, w_bot, ng, nb, tm=TOKEN_TILE):
    nct = tm // S5_T
    t512 = pl.BlockSpec((None, tm, 512), lambda b, i: (b, i, 0))
    tD = pl.BlockSpec((None, tm, D), lambda b, i: (b, i, 0))
    full = lambda *s: pl.BlockSpec(s, lambda b, i: (0,) * len(s))
    return pl.pallas_call(
        _tail0_kernel,
        out_shape=jax.ShapeDtypeStruct((B, L, D), F32),
        grid=(B, L // tm),
        in_specs=[tD, pl.BlockSpec((S5_T, nct, D), lambda b, i: (0, b * (N_CHUNK // nct) + i, 0)), t512, t512,
                  pl.BlockSpec((None, 1, 3 * D), lambda b, i: (b, 0, 0)),
                  full(SGU_HEADS // 2, SGU_CHUNK, 256), full(SGU_CHUNK, 512),
                  pl.BlockSpec((SGU_W, D), lambda b, i: (1, 0)), full(1, D), full(1, D)],
        out_specs=tD,
        compiler_params=_params("arbitrary", "arbitrary"),
        name="tail0",
    )(x, ys5, guz, vln, mod, sguw, sgub, w_bot, ng, nb)


CONV_C = D // 2
TILE_ROWS = TOKEN_TILE // GRID_W


def _grid_transpose_in(v, o_ref):
    perm = _chunk_transpose_perm()
    for q in range(GRID_W // S5_T):
        seg = jnp.concatenate([v[GRID_W * r + S5_T * q:GRID_W * r + S5_T * (q + 1), :] for r in range(TILE_ROWS)],
                              axis=0)
        t = jnp.dot(perm, seg, preferred_element_type=F32).astype(BF16)
        o_ref[S5_T * q:S5_T * (q + 1), :, :] = t.reshape(S5_T, TILE_ROWS, v.shape[1])


def _inproj1_kernel(x_ref, mod_ref, w_ref, hgr_ref, hgc_ref, sz_ref):
    shift = mod_ref[:, 0:D]
    scale = mod_ref[:, D:2 * D]
    h = (x_ref[...] * (1.0 + scale) + shift).astype(BF16)
    dot = lambda lo, w: jnp.dot(h, w_ref[:, lo:lo + w].astype(BF16), preferred_element_type=F32)
    hgr_ref[...] = (dot(0, CONV_C) * _sigmoid(dot(D, CONV_C))).astype(BF16)
    _grid_transpose_in((dot(CONV_C, CONV_C) * _sigmoid(dot(D + CONV_C, CONV_C))).astype(BF16), hgc_ref)
    sz_ref[...] = _silu(dot(2 * D, D)).astype(BF16)


def _inproj1(x, mod, w_in_f32, tm=TOKEN_TILE):
    tile = lambda w: pl.BlockSpec((None, tm, w), lambda b, i: (b, i, 0))
    return pl.pallas_call(
        _inproj1_kernel,
        out_shape=(jax.ShapeDtypeStruct((B, L, CONV_C), BF16),
                   jax.ShapeDtypeStruct((B, GRID_W, GRID_W, CONV_C), BF16),
                   jax.ShapeDtypeStruct((B, L, D), BF16)),
        grid=(B, L // tm),
        in_specs=[tile(D),
                  pl.BlockSpec((None, 1, 3 * D), lambda b, i: (b, 0, 0)),
                  pl.BlockSpec((D, ODD_IN), lambda b, i: (0, 0), pipeline_mode=pl.Buffered(1))],
        out_specs=(tile(CONV_C), pl.BlockSpec((None, GRID_W, TILE_ROWS, CONV_C), lambda b, i: (b, 0, i, 0)),
                   tile(D)),
        compiler_params=_params("arbitrary", "arbitrary"),
        name="inproj1",
    )(x, mod, w_in_f32)


DFT_N = 2 * GRID_W
TAPS_PAD = CONV_K + 1


def _dft_constants():
    th = 2.0 * math.pi / DFT_N
    f = jnp.arange(GRID_W, dtype=F32)[:, None]
    p = jnp.arange(GRID_W, dtype=F32)[None, :]
    cosm = jnp.cos(th * f * p)
    sinm = jnp.sin(th * f * p)
    alt = jnp.where(jnp.arange(GRID_W) % 2 == 0, 1.0, -1.0).astype(F32)
    fwd = jnp.concatenate([cosm, alt[None, :], sinm[1:]], axis=0)
    cf = jnp.where(jnp.arange(GRID_W) == 0, 1.0, 2.0).astype(F32) / DFT_N
    inv = jnp.concatenate([cosm.T * cf[None, :], (alt / DFT_N)[:, None], sinm.T[:, 1:] * (2.0 / DFT_N)], axis=1)
    sft = (CONV_HALF - jnp.arange(TAPS_PAD, dtype=F32))[None, :]
    live = (jnp.arange(TAPS_PAD) < CONV_K).astype(F32)[None, :]
    f64 = jnp.where(f == 0, float(GRID_W), f)
    return (fwd.astype(BF16), inv.astype(BF16),
            jnp.cos(th * f * sft) * live, jnp.sin(th * f * sft) * live, jnp.cos(th * f64 * sft) * live)


def _fconv_kernel(h_ref, w_ref, b_ref, fwd_ref, inv_ref, c1_ref, s3_ref, c4_ref, o_ref):
    hp = lax.Precision.HIGHEST
    taps = w_ref[...]
    g_re = jnp.dot(c1_ref[...], taps, preferred_element_type=F32, precision=hp)
    g_im = jnp.dot(s3_ref[...], taps, preferred_element_type=F32, precision=hp)
    g_r2 = jnp.dot(c4_ref[...], taps, preferred_element_type=F32, precision=hp)
    fwd = fwd_ref[...]
    inv = inv_ref[...]
    bias = b_ref[...]
    n_runs = h_ref.shape[0] // GRID_W
    rows = lambda r: slice(GRID_W * r, GRID_W * (r + 1))
    forward = lambda r: jnp.dot(fwd, h_ref[rows(r), :], preferred_element_type=F32)
    ahead = 2
    specs = [forward(r) for r in range(ahead)]
    for r in range(n_runs):
        if r + ahead < n_runs:
            specs.append(forward(r + ahead))
        spec = specs[r]
        a, bm = spec[0:GRID_W], spec[GRID_W:DFT_N]
        prod = jnp.concatenate([a * g_re - bm * g_im, a * g_im + bm * g_r2], axis=0).astype(BF16)
        o_ref[rows(r), :] = (jnp.dot(inv, prod, preferred_element_type=F32) + bias).astype(BF16)


def _fconv(h, taps, bias, consts, tm=TOKEN_TILE):
    fwd, inv, c1, s3, c4 = consts
    c = h.shape[-1]
    tile = pl.BlockSpec((None, tm, c), lambda b, i: (b, i, 0))
    full = lambda *s: pl.BlockSpec(s, lambda b, i: (0,) * len(s))
    return pl.pallas_call(
        _fconv_kernel,
        out_shape=jax.ShapeDtypeStruct(h.shape, BF16),
        grid=(B, L // tm),
        in_specs=[tile, full(TAPS_PAD, c), full(1, c), full(DFT_N, GRID_W), full(GRID_W, DFT_N),
                  full(GRID_W, TAPS_PAD), full(GRID_W, TAPS_PAD), full(GRID_W, TAPS_PAD)],
        out_specs=tile,
        compiler_params=_params("arbitrary", "arbitrary"),
        name="fconv",
    )(h, taps, bias, fwd, inv, c1, s3, c4)


def _tail1_kernel(x_ref, hcr_ref, hcc_ref, sz_ref, mod_ref, lng_ref, lnb_ref, wout_ref, ng_ref, nb_ref, o_ref,
                  col_ref):
    perm = _chunk_transpose_perm()
    for q in range(GRID_W // S5_T):
        blk = hcc_ref[S5_T * q:S5_T * (q + 1), :, :].reshape(PERM_ROWS, CONV_C)
        t = jnp.dot(perm, blk, preferred_element_type=F32)
        for r in range(TILE_ROWS):
            col_ref[GRID_W * r + S5_T * q:GRID_W * r + S5_T * (q + 1), :] = t[S5_T * r:S5_T * (r + 1), :]
    hc = jnp.concatenate([hcr_ref[...].astype(F32), col_ref[...]], axis=1)
    m = _silu(_layer_norm(hc, lng_ref[...], lnb_ref[...])) * sz_ref[...].astype(F32)
    y = jnp.dot(m.astype(BF16), wout_ref[...], preferred_element_type=F32)
    gmod = mod_ref[:, 2 * D:3 * D]
    o_ref[...] = _layer_norm(DN_ALPHA * x_ref[...] + gmod * y, ng_ref[...], nb_ref[...])


def _tail1(x, hc_row, hc_col, sz, mod, ln_g, ln_b, w_out, ng, nb, tm=TOKEN_TILE):
    tile = lambda w: pl.BlockSpec((None, tm, w), lambda b, i: (b, i, 0))
    full = lambda *s: pl.BlockSpec(s, lambda b, i: (0,) * len(s))
    return pl.pallas_call(
        _tail1_kernel,
        out_shape=jax.ShapeDtypeStruct((B, L, D), F32),
        grid=(B, L // tm),
        in_specs=[tile(D), tile(CONV_C),
                  pl.BlockSpec((None, GRID_W, TILE_ROWS, CONV_C), lambda b, i: (b, 0, i, 0)),
                  tile(D), pl.BlockSpec((None, 1, 3 * D), lambda b, i: (b, 0, 0)),
                  full(1, D), full(1, D), full(D, D), full(1, D), full(1, D)],
        out_specs=tile(D),
        scratch_shapes=[pltpu.VMEM((tm, CONV_C), F32)],
        compiler_params=_params("arbitrary", "arbitrary"),
        name="tail1",
    )(x, hc_row, hc_col, sz, mod, ln_g, ln_b, w_out, ng, nb)


def kernel(x, c, ctx, c_ctx, mod_w, mod_b, norm_g, norm_b, ev_w_in, ev_w_out, s5_lam_re, s5_lam_im, s5_log_dt, s5_b_re, s5_b_im, s5_c_re, s5_c_im, s5_d, glu_w, glu_b, sgu_ln_g, sgu_ln_b, sgu_w, sgu_b, od_w_in, od_w_out, dw_w, dw_b, conv_ln_g, conv_ln_b):
    TH = S5_T * S5_H
    row = lambda v: v.reshape(1, -1)

    cond8 = jnp.concatenate([c, c_ctx[None], jnp.zeros((3, D), F32)], axis=0)
    mods = _adaln(cond8, mod_w, mod_b)
    mod0 = mods[0, :B].reshape(B, 1, 3 * D)
    mod0c = mods[0, B:B + 1]
    mod1 = mods[1, :B].reshape(B, 1, 3 * D)

    lbr, lbi, cfr, cfi = _discretise(s5_lam_re[0], s5_lam_im[0], s5_log_dt[0])
    rowcat = lambda a: jnp.concatenate([a[0], a[1]], axis=-1).reshape(S5_G, 1, 2 * S5_P)
    bt = lambda a: jnp.concatenate([jnp.swapaxes(a[0], 1, 2), jnp.swapaxes(a[1], 1, 2)], axis=-1)
    cn = lambda a: jnp.concatenate([a[0], a[1]], axis=-1)
    d_row = jnp.tile(s5_d[0].reshape(S5_G, 1, S5_H), (1, 1, S5_T))
    win, wout, mix, l16 = _s5_weights(rowcat(lbr), rowcat(lbi), rowcat(cfr), rowcat(cfi),
                                      bt(s5_b_re[0]), bt(s5_b_im[0]), cn(s5_c_re[0]), cn(s5_c_im[0]), d_row)

    w_sgu = ev_w_in[0][:, SGU_COL0:].astype(BF16)
    w_out0 = ev_w_out[0].astype(BF16)
    glu_w0 = glu_w[0].astype(BF16)
    guz, vln, hs = _inproj0n(x, mod0, w_sgu, row(sgu_ln_g[0]), row(sgu_ln_b[0]))
    ua, sza, ua_c = _inproj0a(hs, _ctx_slabs(ctx, mod0c), ev_w_in[0])
    s_lat = _s5core(ua, ua_c, win, wout, mix, l16)
    y_s5 = _s5tail(s_lat, sza, glu_w0, row(glu_b[0]), w_out0)
    sguw = sgu_w[0].reshape(SGU_HEADS // 2, 2, SGU_CHUNK, SGU_CHUNK)
    sguw = jnp.transpose(sguw, (0, 2, 1, 3)).reshape(SGU_HEADS // 2, SGU_CHUNK, 2 * SGU_CHUNK).astype(BF16)
    sgub = jnp.repeat(sgu_b[0].T, SGU_HD, axis=1)
    x1 = _tail0(x, y_s5, guz, vln, mod0, sguw, sgub, w_out0, row(norm_g[0]), row(norm_b[0]))

    hg_row, hg_col, sz = _inproj1(x1, mod1, od_w_in[0])
    consts = _dft_constants()
    taps = jnp.pad(dw_w[0], ((0, TAPS_PAD - CONV_K), (0, 0)))
    bias = row(dw_b[0])
    hc_row = _fconv(hg_row, taps[:, :CONV_C], bias[:, :CONV_C], consts)
    hc_col = _fconv(hg_col.reshape(B, L, CONV_C), taps[:, CONV_C:], bias[:, CONV_C:], consts)
    return _tail1(x1, hc_row, hc_col.reshape(B, GRID_W, GRID_W, CONV_C), sz, mod1,
                  row(conv_ln_g[0]), row(conv_ln_b[0]), od_w_out[0].astype(BF16), row(norm_g[1]), row(norm_b[1]))
```

```python
import functools
import math

import jax
import jax.numpy as jnp
from jax import lax
from jax.experimental import pallas as pl
from jax.experimental.pallas import tpu as pltpu

D = 1024
B = 4
L = 4096
CTX = 256
GRID_W = 64
S5_W = 512
S5_G = 32
S5_H = 16
H_SHIFT = 4
BLK = 128 // S5_H
S5_P = 64
S5_T = 16
SGU_W = 512
SGU_HEADS = 8
SGU_HD = 64
SGU_CHUNK = 128
CONV_K = 31
CONV_HALF = CONV_K // 2
EVEN_IN = 2560
SGU_COL0 = 2 * S5_W
ODD_IN = 3072
DEPTH = 2
DN_ALPHA = (2 * DEPTH) ** 0.25
LN_EPS = 1e-5
N_CHUNK = L // S5_T
N_CCHUNK = CTX // S5_T
VMEM_LIMIT_V7X = 56 * 1024 * 1024
TOKEN_TILE = 1024

F32 = jnp.float32
BF16 = jnp.bfloat16


GELU_C = math.sqrt(2.0 / math.pi)


def _gelu(x):
    hx = 0.5 * x
    return hx * jnp.tanh(x * ((x * x) * (0.044715 * GELU_C) + GELU_C)) + hx


def _sigmoid(x):
    return 0.5 * jnp.tanh(0.5 * x) + 0.5


def _silu(x):
    hx = 0.5 * x
    return hx * jnp.tanh(hx) + hx


def _layer_norm(x, g, b):
    mu = jnp.mean(x, axis=-1, keepdims=True)
    xc = x - mu
    var = jnp.mean(xc * xc, axis=-1, keepdims=True)
    return xc * lax.rsqrt(var + LN_EPS) * g + b


def _params(*sem):
    return pltpu.CompilerParams(dimension_semantics=sem, vmem_limit_bytes=VMEM_LIMIT_V7X)


def _adaln_kernel(c_ref, w_ref, b_ref, o_ref):
    def split(v):
        hi = v.astype(BF16)
        return hi, (v - hi.astype(F32)).astype(BF16)

    a_hi, a_lo = split(_silu(c_ref[...]))
    w_hi, w_lo = split(w_ref[...])
    dot = functools.partial(jnp.dot, preferred_element_type=F32)
    o_ref[...] = dot(a_hi, w_hi) + dot(a_lo, w_hi) + dot(a_hi, w_lo) + b_ref[...]


def _adaln(cond8, mod_w, mod_b):
    tn = 512
    return pl.pallas_call(
        _adaln_kernel,
        out_shape=jax.ShapeDtypeStruct((DEPTH, 8, 3 * D), F32),
        grid=(DEPTH, 3 * D // tn),
        in_specs=[pl.BlockSpec((8, D), lambda l, j: (0, 0)),
                  pl.BlockSpec((None, D, tn), lambda l, j: (l, 0, j)),
                  pl.BlockSpec((None, 1, tn), lambda l, j: (l, 0, j))],
        out_specs=pl.BlockSpec((None, 8, tn), lambda l, j: (l, 0, j)),
        compiler_params=_params("arbitrary", "arbitrary"),
        name="adaln",
    )(cond8, mod_w, mod_b.reshape(DEPTH, 1, 3 * D))


def _disc_kernel(lr_ref, li_ref, ldt_ref, obr_ref, obi_ref, ocr_ref, oci_ref):
    lr = lr_ref[...]
    li = li_ref[...]
    dt = jnp.exp(ldt_ref[...])
    mag = jnp.exp(lr * dt)
    br = mag * jnp.cos(li * dt)
    bi = mag * jnp.sin(li * dt)
    inv = 1.0 / (lr * lr + li * li)
    nr = br - 1.0
    obr_ref[...] = br
    obi_ref[...] = bi
    ocr_ref[...] = (nr * lr + bi * li) * inv
    oci_ref[...] = (bi * lr - nr * li) * inv


def _discretise(lam_re, lam_im, log_dt):
    shp = jax.ShapeDtypeStruct((2 * S5_G, S5_P), F32)
    ldt = jnp.broadcast_to(log_dt.reshape(2 * S5_G, 1), (2 * S5_G, S5_P))
    outs = pl.pallas_call(
        _disc_kernel, out_shape=(shp, shp, shp, shp), name="s5_discretise",
    )(lam_re.reshape(2 * S5_G, S5_P), lam_im.reshape(2 * S5_G, S5_P), ldt)
    return [o.reshape(2, S5_G, S5_P) for o in outs]


S5W_GROUPS = 4


def _cpow(base_pows, j):
    re = None
    im = None
    for k, (pr, pi) in enumerate(base_pows):
        bit = ((j >> k) & 1) == 1
        mr = jnp.where(bit, pr, 1.0)
        mi = jnp.where(bit, pi, 0.0)
        if re is None:
            re, im = mr, mi
        else:
            re, im = re * mr - im * mi, re * mi + im * mr
    return re, im


def _squarings(pr, pi, n):
    out = [(pr, pi)]
    for _ in range(n - 1):
        pr, pi = pr * pr - pi * pi, 2.0 * pr * pi
        out.append((pr, pi))
    return out


def _shift_lanes(x, n):
    lane = lax.broadcasted_iota(jnp.int32, (S5_H, 128), 1)
    lo, hi = x[:, :128], x[:, 128:]
    if n == 0:
        return x
    if n < 128:
        rlo = pltpu.roll(lo, n, axis=1)
        rhi = pltpu.roll(hi, n, axis=1)
        return jnp.concatenate([jnp.where(lane >= n, rlo, 0.0), jnp.where(lane >= n, rhi, rlo)], axis=1)
    m = n - 128
    rlo = lo if m == 0 else pltpu.roll(lo, m, axis=1)
    return jnp.concatenate([jnp.zeros_like(lo), jnp.where(lane >= m, rlo, 0.0)], axis=1)


def _unshift_lanes(x, n):
    lane = lax.broadcasted_iota(jnp.int32, (S5_H, 128), 1)
    lo, hi = x[:, :128], x[:, 128:]
    if n == 0:
        return x
    if n < 128:
        rlo = pltpu.roll(lo, 128 - n, axis=1)
        rhi = pltpu.roll(hi, 128 - n, axis=1)
        keep = lane < 128 - n
        return jnp.concatenate([jnp.where(keep, rlo, rhi), jnp.where(keep, rhi, 0.0)], axis=1)
    m = n - 128
    rhi = hi if m == 0 else pltpu.roll(hi, 128 - m, axis=1)
    return jnp.concatenate([jnp.where(lane < 128 - m, rhi, 0.0), jnp.zeros_like(lo)], axis=1)


def _s5w_group(gi, bg, lrow_re, lrow_im, crow_re, crow_im, bt_re, bt_im,
               cn_re, cn_im, d_ref, win_ref, wout_ref, mix_ref, l16_ref):
    TH = S5_T * S5_H

    def chunk_pos(idx):
        return (((idx >> H_SHIFT) - bg) & (BLK - 1)) + ((idx >> 7) << 3)

    lr = lrow_re[gi]
    li = lrow_im[gi]
    pows_row = _squarings(lr, li, 5)
    l16_ref[gi, 0:1, :] = pows_row[4][0]
    l16_ref[gi, 1:2, :] = pows_row[4][1]
    l16_ref[gi, 2:8, :] = jnp.zeros((6, 128), F32)
    cr = crow_re[gi]
    ci = crow_im[gi]
    btr = bt_re[gi]
    bti = bt_im[gi]
    bbr = cr * btr - ci * bti
    bbi = cr * bti + ci * btr
    blk16 = lax.broadcasted_iota(jnp.int32, (S5_T, 128), 0)
    is_f16 = lax.broadcasted_iota(jnp.int32, (S5_T, 128), 1) < S5_P
    pos16 = chunk_pos(blk16 << H_SHIFT)
    pr16, pi16 = _cpow(pows_row[:4], jnp.where(is_f16, S5_T - 1 - pos16, pos16))
    rep_rows = lambda v: jnp.broadcast_to(v[:, None, :], (S5_T, S5_H, 128)).reshape(TH, 128)
    pr, pi = rep_rows(pr16), rep_rows(pi16)
    tbr = jnp.broadcast_to(bbr[None], (S5_T, S5_H, 128)).reshape(TH, 128)
    tbi = jnp.broadcast_to(bbi[None], (S5_T, S5_H, 128)).reshape(TH, 128)
    win_ref[gi, :, 0:128] = (pr * tbr - pi * tbi).astype(BF16)
    win_ref[gi, :, 128:256] = (pr * tbi + pi * tbr).astype(BF16)

    hp = lax.Precision.HIGHEST
    dot = functools.partial(jnp.dot, preferred_element_type=F32, precision=hp)
    def col256(r):
        col = jnp.broadcast_to(r, (2 * S5_P, 2 * S5_P)).T
        return jnp.concatenate([col, col], axis=1)

    def tiled_t(cn):
        t8 = jnp.broadcast_to(cn[None], (BLK, S5_H, 2 * S5_P)).reshape(2 * S5_P, 2 * S5_P).T
        return jnp.concatenate([t8, t8], axis=1)

    cpows = _squarings(col256(lr), col256(li), 4)
    row = lax.broadcasted_iota(jnp.int32, (2 * S5_P, TH), 0)
    lane_w = lax.broadcasted_iota(jnp.int32, (2 * S5_P, TH), 1)
    t_idx = chunk_pos(lane_w)
    j_idx = lane_w >> H_SHIFT
    is_f = row < S5_P
    ctr = tiled_t(cn_re[gi])
    cti = tiled_t(cn_im[gi])
    er, ei = _cpow(cpows, jnp.where(is_f, t_idx, S5_T - 1 - t_idx))
    er, ei = er * cpows[0][0] - ei * cpows[0][1], er * cpows[0][1] + ei * cpows[0][0]
    wr = ctr * er - cti * ei
    wi = ctr * ei + cti * er
    wout_ref[gi, 0:128, :] = wr.astype(BF16)
    wout_ref[gi, 128:256, :] = (-wi).astype(BF16)
    kr, ki = _cpow(cpows, jnp.where(is_f, j_idx, S5_T - 1 - j_idx))
    ekr = ctr * kr - cti * ki
    eki = ctr * ki + cti * kr
    lane16 = lax.broadcasted_iota(jnp.int32, (S5_H, 128), 1)
    mf = lane16 < S5_P
    kkf = dot(jnp.where(mf, bbr, 0.0), ekr) - dot(jnp.where(mf, bbi, 0.0), eki)
    kkb = dot(jnp.where(mf, 0.0, bbr), ekr) - dot(jnp.where(mf, 0.0, bbi), eki)
    dl = d_ref[gi]
    r16 = lax.broadcasted_iota(jnp.int32, (S5_H, TH), 0)
    l256 = lax.broadcasted_iota(jnp.int32, (S5_H, TH), 1)
    rot = bg * S5_H
    for s in range(S5_T):
        blk = _shift_lanes(kkf, S5_H * s) + _unshift_lanes(kkb, S5_H * (S5_T - 1 - s))
        blk = blk + jnp.where(l256 == r16 + S5_H * s, dl, 0.0)
        blk = jnp.concatenate([pltpu.roll(blk[:, :128], rot, axis=1), pltpu.roll(blk[:, 128:], rot, axis=1)], axis=1)
        rho = ((s + bg) & (BLK - 1)) + (s & BLK)
        mix_ref[gi, pl.ds(pl.multiple_of(rho * S5_H, S5_H), S5_H), :] = blk.astype(BF16)


def _s5w_kernel(*refs):
    for gi in range(S5W_GROUPS):
        bg = (pl.program_id(0) * S5W_GROUPS + gi) & (BLK - 1)
        _s5w_group(gi, bg, *refs)


def _s5_weights(lrow_re, lrow_im, crow_re, crow_im, bt_re, bt_im, cn_re, cn_im, d_row):
    TH = S5_T * S5_H
    g3 = lambda r, c: pl.BlockSpec((S5W_GROUPS, r, c), lambda g: (g, 0, 0))
    wshape = jax.ShapeDtypeStruct((S5_G, TH, TH), BF16)
    return pl.pallas_call(
        _s5w_kernel,
        out_shape=(wshape, wshape, wshape, jax.ShapeDtypeStruct((S5_G, 8, 128), F32)),
        grid=(S5_G // S5W_GROUPS,),
        in_specs=[g3(1, 128)] * 4 + [g3(S5_H, 128)] * 4 + [g3(1, TH)],
        out_specs=(g3(TH, TH), g3(TH, TH), g3(TH, TH), g3(8, 128)),
        compiler_params=_params("arbitrary"),
        name="s5_weights",
    )(lrow_re, lrow_im, crow_re, crow_im, bt_re, bt_im, cn_re, cn_im, d_row)


def _rot_blocks(v, r):
    cols = [pltpu.roll(v[:, 128 * q:128 * (q + 1)], S5_H * r, axis=1) for q in range(v.shape[1] // 128)]
    return jnp.concatenate(cols, axis=1)


def _slabs_of(h, hs_ref):
    h3 = h.reshape(h.shape[0] // S5_T, S5_T, h.shape[1])
    for s in range(S5_T):
        hs_ref[s] = h3[:, s, :].astype(BF16)


PERM_ROWS = S5_T * S5_T


def _chunk_transpose_perm():
    ri = lax.broadcasted_iota(jnp.int32, (PERM_ROWS, PERM_ROWS), 0)
    ci = lax.broadcasted_iota(jnp.int32, (PERM_ROWS, PERM_ROWS), 1)
    hit = ((ri >> H_SHIFT) == (ci & (S5_T - 1))) & ((ri & (S5_T - 1)) == (ci >> H_SHIFT))
    return jnp.where(hit, 1.0, 0.0).astype(BF16)


def _inproj0n_kernel(x_ref, mod_ref, w_ref, lng_ref, lnb_ref, guz_ref, vln_ref, hs_ref):
    shift = mod_ref[:, 0:D]
    scale = mod_ref[:, D:2 * D]
    hb = (x_ref[...] * (1.0 + scale) + shift).astype(BF16)
    perm = _chunk_transpose_perm()
    for j in range(hb.shape[0] // PERM_ROWS):
        blk = jnp.dot(perm, hb[PERM_ROWS * j:PERM_ROWS * (j + 1), :], preferred_element_type=F32).astype(BF16)
        for s in range(S5_T):
            hs_ref[s, S5_T * j:S5_T * (j + 1), :] = blk[S5_T * s:S5_T * (s + 1), :]
    dot = lambda lo: jnp.dot(hb, w_ref[:, SGU_COL0 + lo:SGU_COL0 + lo + 512].astype(BF16),
                             preferred_element_type=F32)
    guz_ref[...] = (_gelu(dot(0)) * _silu(dot(1024))).astype(BF16)
    vln_ref[...] = _layer_norm(_gelu(dot(512)), lng_ref[...], lnb_ref[...]).astype(BF16)


def _inproj0n(x, mod, w_in_f32, ln_g, ln_b, tm=TOKEN_TILE):
    nct = tm // S5_T
    o = jax.ShapeDtypeStruct((B, L, 512), BF16)
    ospec = pl.BlockSpec((None, tm, 512), lambda b, i: (b, i, 0))
    full = lambda *s: pl.BlockSpec(s, lambda b, i: (0,) * len(s))
    return pl.pallas_call(
        _inproj0n_kernel,
        out_shape=(o, o, jax.ShapeDtypeStruct((S5_T, B * N_CHUNK, D), BF16)),
        grid=(B, L // tm),
        in_specs=[pl.BlockSpec((None, tm, D), lambda b, i: (b, i, 0)),
                  pl.BlockSpec((None, 1, 3 * D), lambda b, i: (b, 0, 0)),
                  pl.BlockSpec((D, EVEN_IN), lambda b, i: (0, 0), pipeline_mode=pl.Buffered(1)),
                  full(1, 512), full(1, 512)],
        out_specs=(ospec, ospec,
                   pl.BlockSpec((S5_T, nct, D), lambda b, i: (0, b * (N_CHUNK // nct) + i, 0))),
        compiler_params=_params("arbitrary", "arbitrary"),
        name="inproj0n",
    )(x, mod, w_in_f32, ln_g, ln_b)


def _ctx_slabs_kernel(x_ref, mod_ref, hs_ref):
    h = x_ref[...] * (1.0 + mod_ref[:, D:2 * D]) + mod_ref[:, 0:D]
    _slabs_of(h, hs_ref)


def _ctx_slabs(ctx, mod_c):
    return pl.pallas_call(
        _ctx_slabs_kernel,
        out_shape=jax.ShapeDtypeStruct((S5_T, B * N_CCHUNK, D), BF16),
        grid=(B,),
        in_specs=[pl.BlockSpec((None, CTX, D), lambda b: (b, 0, 0)),
                  pl.BlockSpec((1, 3 * D), lambda b: (0, 0))],
        out_specs=pl.BlockSpec((S5_T, N_CCHUNK, D), lambda b: (0, b, 0)),
        compiler_params=_params("arbitrary"),
        name="ctx_slabs",
    )(ctx, mod_c)


def _inproj0a_kernel(hs_ref, hc_ref, w_ref, ua_ref, sza_ref, uc_ref):
    r = pl.program_id(0)
    h = hs_ref[...]
    w_ua = w_ref[:, 0:512].astype(BF16)
    ua_ref[...] = _rot_blocks(jnp.dot(h, w_ua, preferred_element_type=F32), r).astype(BF16)
    sza_ref[...] = _silu(jnp.dot(h, w_ref[:, 512:1024].astype(BF16), preferred_element_type=F32)).astype(BF16)
    uc_ref[...] = _rot_blocks(jnp.dot(hc_ref[...], w_ua, preferred_element_type=F32), r).astype(BF16)


def _inproj0a(hs, hcs, w_in_f32):
    slab = lambda r, h: r + BLK * h
    sspec = lambda n, w: pl.BlockSpec((None, n, w), lambda r, h: (slab(r, h), 0, 0))
    so = lambda n: jax.ShapeDtypeStruct((S5_T, n, 512), BF16)
    nl, ncx = B * N_CHUNK, B * N_CCHUNK
    return pl.pallas_call(
        _inproj0a_kernel,
        out_shape=(so(nl), so(nl), so(ncx)),
        grid=(BLK, S5_T // BLK),
        in_specs=[sspec(nl, D), sspec(ncx, D), pl.BlockSpec((D, SGU_COL0), lambda r, h: (0, 0))],
        out_specs=(sspec(nl, 512), sspec(nl, 512), sspec(ncx, 512)),
        compiler_params=_params("arbitrary", "arbitrary"),
        name="inproj0a",
    )(hs, hcs, w_in_f32)


SCAN_GROUPS = 4


def _scan_tiles(sre_ref, sim_ref, h_refs, n_tiles, carry, lams):
    row = lax.broadcasted_iota(jnp.int32, (8, 128), 0)
    lane = lax.broadcasted_iota(jnp.int32, (8, 128), 1)
    first = row < B
    fwd = lane < S5_P

    def body(k, c):
        of = pl.multiple_of(k * 8, 8)
        ob = pl.multiple_of((n_tiles - 1 - k) * 8, 8)
        out = []
        for gi in range(SCAN_GROUPS):
            lre, lim = lams[gi]
            hr, hi = c[2 * gi], c[2 * gi + 1]
            sr = jnp.where(fwd, sre_ref[gi, pl.ds(of, 8), :], pltpu.roll(sre_ref[gi, pl.ds(ob, 8), :], B, axis=0))
            si = jnp.where(fwd, sim_ref[gi, pl.ds(of, 8), :], pltpu.roll(sim_ref[gi, pl.ds(ob, 8), :], B, axis=0))
            h1r = lre * hr - lim * hi + sr
            h1i = lre * hi + lim * hr + si
            r1r = pltpu.roll(h1r, B, axis=0)
            r1i = pltpu.roll(h1i, B, axis=0)
            if h_refs is not None:
                fre_ref, fim_ref, bre_ref, bim_ref = h_refs
                er = jnp.where(first, hr, r1r)
                ei = jnp.where(first, hi, r1i)
                fre_ref[gi, pl.ds(of, 8), :] = er
                fim_ref[gi, pl.ds(of, 8), :] = ei
                bre_ref[gi, pl.ds(ob, 8), :] = pltpu.roll(er, B, axis=0)
                bim_ref[gi, pl.ds(ob, 8), :] = pltpu.roll(ei, B, axis=0)
            h2r = lre * r1r - lim * r1i + sr
            h2i = lre * r1i + lim * r1r + si
            out.append(jnp.where(first, pltpu.roll(h2r, B, axis=0), h2r))
            out.append(jnp.where(first, pltpu.roll(h2i, B, axis=0), h2i))
        return tuple(out)

    return lax.fori_loop(0, n_tiles, body, carry)


def _gather_group(slab_ref, src):
    halves = []
    for h in range(S5_T // BLK):
        acc = slab_ref[BLK * h]
        for s in range(1, BLK):
            acc = jnp.where(src == s, slab_ref[BLK * h + s], acc)
        halves.append(acc)
    return jnp.concatenate(halves, axis=1)


def _s5core_kernel(ul_ref, uc_ref, win_ref, wout_ref, mix_ref, l16_ref, o_ref,
                   u_ref, sre_ref, sim_ref, cre_ref, cim_ref, fre_ref, fim_ref, bre_ref, bim_ref, y_ref):
    nl = N_CHUNK * B
    ncx = N_CCHUNK * B
    blk_l = lax.broadcasted_iota(jnp.int32, (nl, 128), 1) >> H_SHIFT
    blk_c = lax.broadcasted_iota(jnp.int32, (ncx, 128), 1) >> H_SHIFT
    fwd = lax.broadcasted_iota(jnp.int32, (N_CHUNK, 128), 1) < S5_P
    for g0 in range(0, BLK, SCAN_GROUPS):
        for gi in range(SCAN_GROUPS):
            bg = g0 + gi
            win = win_ref[bg]
            src_l = ((blk_l - bg) & (BLK - 1)).astype(F32).astype(BF16)
            src_c = ((blk_c - bg) & (BLK - 1)).astype(F32).astype(BF16)
            u = _gather_group(ul_ref, src_l)
            u_ref[gi] = u
            sl = jnp.dot(u, win, preferred_element_type=F32)
            sc = jnp.dot(_gather_group(uc_ref, src_c), win, preferred_element_type=F32)
            for b in range(B):
                sre_ref[gi, pl.ds(b, N_CHUNK, stride=B), :] = sl[N_CHUNK * b:N_CHUNK * (b + 1), 0:128]
                sim_ref[gi, pl.ds(b, N_CHUNK, stride=B), :] = sl[N_CHUNK * b:N_CHUNK * (b + 1), 128:256]
                cre_ref[gi, pl.ds(b, N_CCHUNK, stride=B), :] = sc[N_CCHUNK * b:N_CCHUNK * (b + 1), 0:128]
                cim_ref[gi, pl.ds(b, N_CCHUNK, stride=B), :] = sc[N_CCHUNK * b:N_CCHUNK * (b + 1), 128:256]
        lams = [(jnp.broadcast_to(l16_ref[g0 + gi, 0:1, :], (8, 128)),
                 jnp.broadcast_to(l16_ref[g0 + gi, 1:2, :], (8, 128))) for gi in range(SCAN_GROUPS)]
        zero = tuple(jnp.zeros((8, 128), F32) for _ in range(2 * SCAN_GROUPS))
        carry = _scan_tiles(cre_ref, cim_ref, None, ncx // 8, zero, lams)
        _scan_tiles(sre_ref, sim_ref, (fre_ref, fim_ref, bre_ref, bim_ref), nl // 8, carry, lams)
        for gi in range(SCAN_GROUPS):
            bg = g0 + gi
            y = jnp.dot(u_ref[gi], mix_ref[bg], preferred_element_type=F32)
            hs = []
            for b in range(B):
                rows = pl.ds(b, N_CHUNK, stride=B)
                hs.append(jnp.concatenate([jnp.where(fwd, fre_ref[gi, rows, :], bre_ref[gi, rows, :]),
                                           jnp.where(fwd, fim_ref[gi, rows, :], bim_ref[gi, rows, :])], axis=1))
            hcat = jnp.concatenate(hs, axis=0).astype(BF16)
            y = y + jnp.dot(hcat, wout_ref[bg], preferred_element_type=F32)
            y_ref[bg] = y.astype(BF16)

    blk = blk_l.astype(F32).astype(BF16)
    for s in range(S5_T):
        h, r = s // BLK, s % BLK
        acc = None
        for j in range(BLK):
            piece = y_ref[(j - r) % BLK, :, 128 * h:128 * (h + 1)]
            acc = piece if acc is None else jnp.where(blk == j, piece, acc)
        o_ref[s] = acc


def _s5core(ul, uc, win, wout, mix, l16):
    TH = S5_T * S5_H
    nl = N_CHUNK * B
    ncx = N_CCHUNK * B
    g4 = lambda r, c: pl.BlockSpec((BLK, r, c), lambda q: (q, 0, 0))
    col = lambda n: pl.BlockSpec((S5_T, n, 128), lambda q: (0, 0, q))
    f32s = lambda n: pltpu.VMEM((SCAN_GROUPS, n, 128), F32)
    return pl.pallas_call(
        _s5core_kernel,
        out_shape=jax.ShapeDtypeStruct((S5_T, nl, S5_W), BF16),
        grid=(S5_G // BLK,),
        in_specs=[col(nl), col(ncx), g4(TH, TH), g4(TH, TH), g4(TH, TH), g4(8, 128)],
        out_specs=col(nl),
        scratch_shapes=[pltpu.VMEM((SCAN_GROUPS, nl, TH), BF16),
                        f32s(nl), f32s(nl), f32s(ncx), f32s(ncx), f32s(nl), f32s(nl), f32s(nl), f32s(nl),
                        pltpu.VMEM((BLK, nl, TH), BF16)],
        compiler_params=_params("arbitrary"),
        name="s5core",
    )(ul, uc, win, wout, mix, l16)


def _s5tail_kernel(slat_ref, sza_ref, gluw_ref, glub_ref, wtop_ref, y_ref):
    unrot = (BLK - pl.program_id(0)) & (BLK - 1)
    for b in range(B):
        rows = slice(N_CHUNK * b, N_CHUNK * (b + 1))
        g = _gelu(_rot_blocks(slat_ref[rows, :].astype(F32), unrot))
        gate = _sigmoid(jnp.dot(g.astype(BF16), gluw_ref[...], preferred_element_type=F32) + glub_ref[...])
        a = (g * gate * sza_ref[rows, :].astype(F32)).astype(BF16)
        y_ref[rows, :] = jnp.dot(a, wtop_ref[...], preferred_element_type=F32).astype(BF16)


def _s5tail(slat, sza, glu_w, glu_b, w_top):
    slab = lambda r, h: r + BLK * h
    sspec = lambda w: pl.BlockSpec((None, N_CHUNK * B, w), lambda r, h: (slab(r, h), 0, 0))
    full = lambda *s: pl.BlockSpec(s, lambda r, h: (0,) * len(s))
    return pl.pallas_call(
        _s5tail_kernel,
        out_shape=jax.ShapeDtypeStruct((S5_T, N_CHUNK * B, D), BF16),
        grid=(BLK, S5_T // BLK),
        in_specs=[sspec(512), sspec(512), full(512, 512), full(1, 512), full(S5_W, D)],
        out_specs=sspec(D),
        compiler_params=_params("arbitrary", "arbitrary"),
        name="s5tail",
    )(slat, sza, glu_w, glu_b, w_top)


def _tail0_kernel(x_ref, ys5_ref, guz_ref, vln_ref, mod_ref, sguw_ref, sgub_ref, wbot_ref, ng_ref, nb_ref, o_ref):
    tm = x_ref.shape[0]
    lane = lax.broadcasted_iota(jnp.int32, (SGU_CHUNK, 128), 1)
    lo = lane < SGU_HD
    zero = jnp.zeros((SGU_CHUNK, 128), BF16)
    chunks = []
    for ci in range(tm // SGU_CHUNK):
        v = vln_ref[ci * SGU_CHUNK:(ci + 1) * SGU_CHUNK, :]
        cols = []
        for pi in range(SGU_HEADS // 2):
            vp = v[:, 128 * pi:128 * (pi + 1)]
            bm = jnp.concatenate([jnp.where(lo, vp, zero), jnp.where(lo, zero, vp)], axis=0)
            cols.append(jnp.dot(sguw_ref[pi], bm, preferred_element_type=F32))
        chunks.append(jnp.concatenate(cols, axis=1) + sgub_ref[...])
    s = jnp.concatenate(chunks, axis=0)
    bsg = (guz_ref[...].astype(F32) * s).astype(BF16)
    perm = _chunk_transpose_perm()
    ys5 = jnp.concatenate(
        [jnp.dot(perm, ys5_ref[:, S5_T * j:S5_T * (j + 1), :].reshape(PERM_ROWS, D), preferred_element_type=F32)
         for j in range(tm // PERM_ROWS)], axis=0)
    y = ys5 + jnp.dot(bsg, wbot_ref[...], preferred_element_type=F32)
    gmod = mod_ref[:, 2 * D:3 * D]
    o_ref[...] = _layer_norm(DN_ALPHA * x_ref[...] + gmod * y, ng_ref[...], nb_ref[...])


def _tail0(x, ys5, guz, vln, mod, sguw, sgub, w_bot, ng, nb, tm=TOKEN_TILE):
    nct = tm // S5_T
    t512 = pl.BlockSpec((None, tm, 512), lambda b, i: (b, i, 0))
    tD = pl.BlockSpec((None, tm, D), lambda b, i: (b, i, 0))
    full = lambda *s: pl.BlockSpec(s, lambda b, i: (0,) * len(s))
    return pl.pallas_call(
        _tail0_kernel,
        out_shape=jax.ShapeDtypeStruct((B, L, D), F32),
        grid=(B, L // tm),
        in_specs=[tD, pl.BlockSpec((S5_T, nct, D), lambda b, i: (0, b * (N_CHUNK // nct) + i, 0)), t512, t512,
                  pl.BlockSpec((None, 1, 3 * D), lambda b, i: (b, 0, 0)),
                  full(SGU_HEADS // 2, SGU_CHUNK, 256), full(SGU_CHUNK, 512),
                  pl.BlockSpec((SGU_W, D), lambda b, i: (1, 0)), full(1, D), full(1, D)],
        out_specs=tD,
        compiler_params=_params("arbitrary", "arbitrary"),
        name="tail0",
    )(x, ys5, guz, vln, mod, sguw, sgub, w_bot, ng, nb)


CONV_C = D // 2
TILE_ROWS = TOKEN_TILE // GRID_W


def _grid_transpose_in(v, o_ref):
    perm = _chunk_transpose_perm()
    for q in range(GRID_W // S5_T):
        seg = jnp.concatenate([v[GRID_W * r + S5_T * q:GRID_W * r + S5_T * (q + 1), :] for r in range(TILE_ROWS)],
                              axis=0)
        t = jnp.dot(perm, seg, preferred_element_type=F32).astype(BF16)
        o_ref[S5_T * q:S5_T * (q + 1), :, :] = t.reshape(S5_T, TILE_ROWS, v.shape[1])


def _inproj1_kernel(x_ref, mod_ref, w_ref, hgr_ref, hgc_ref, sz_ref):
    shift = mod_ref[:, 0:D]
    scale = mod_ref[:, D:2 * D]
    h = (x_ref[...] * (1.0 + scale) + shift).astype(BF16)
    dot = lambda lo, w: jnp.dot(h, w_ref[:, lo:lo + w].astype(BF16), preferred_element_type=F32)
    hgr_ref[...] = (dot(0, CONV_C) * _sigmoid(dot(D, CONV_C))).astype(BF16)
    _grid_transpose_in((dot(CONV_C, CONV_C) * _sigmoid(dot(D + CONV_C, CONV_C))).astype(BF16), hgc_ref)
    sz_ref[...] = _silu(dot(2 * D, D)).astype(BF16)


def _inproj1(x, mod, w_in_f32, tm=TOKEN_TILE):
    tile = lambda w: pl.BlockSpec((None, tm, w), lambda b, i: (b, i, 0))
    return pl.pallas_call(
        _inproj1_kernel,
        out_shape=(jax.ShapeDtypeStruct((B, L, CONV_C), BF16),
                   jax.ShapeDtypeStruct((B, GRID_W, GRID_W, CONV_C), BF16),
                   jax.ShapeDtypeStruct((B, L, D), BF16)),
        grid=(B, L // tm),
        in_specs=[tile(D),
                  pl.BlockSpec((None, 1, 3 * D), lambda b, i: (b, 0, 0)),
                  pl.BlockSpec((D, ODD_IN), lambda b, i: (0, 0), pipeline_mode=pl.Buffered(1))],
        out_specs=(tile(CONV_C), pl.BlockSpec((None, GRID_W, TILE_ROWS, CONV_C), lambda b, i: (b, 0, i, 0)),
                   tile(D)),
        compiler_params=_params("arbitrary", "arbitrary"),
        name="inproj1",
    )(x, mod, w_in_f32)


DFT_N = 2 * GRID_W
TAPS_PAD = CONV_K + 1


def _dft_constants():
    th = 2.0 * math.pi / DFT_N
    f = jnp.arange(GRID_W, dtype=F32)[:, None]
    p = jnp.arange(GRID_W, dtype=F32)[None, :]
    cosm = jnp.cos(th * f * p)
    sinm = jnp.sin(th * f * p)
    alt = jnp.where(jnp.arange(GRID_W) % 2 == 0, 1.0, -1.0).astype(F32)
    fwd = jnp.concatenate([cosm, alt[None, :], sinm[1:]], axis=0)
    cf = jnp.where(jnp.arange(GRID_W) == 0, 1.0, 2.0).astype(F32) / DFT_N
    inv = jnp.concatenate([cosm.T * cf[None, :], (alt / DFT_N)[:, None], sinm.T[:, 1:] * (2.0 / DFT_N)], axis=1)
    sft = (CONV_HALF - jnp.arange(TAPS_PAD, dtype=F32))[None, :]
    live = (jnp.arange(TAPS_PAD) < CONV_K).astype(F32)[None, :]
    f64 = jnp.where(f == 0, float(GRID_W), f)
    return (fwd.astype(BF16), inv.astype(BF16),
            jnp.cos(th * f * sft) * live, jnp.sin(th * f * sft) * live, jnp.cos(th * f64 * sft) * live)


def _fconv_kernel(h_ref, w_ref, b_ref, fwd_ref, inv_ref, c1_ref, s3_ref, c4_ref, o_ref):
    hp = lax.Precision.HIGHEST
    taps = w_ref[...]
    g_re = jnp.dot(c1_ref[...], taps, preferred_element_type=F32, precision=hp)
    g_im = jnp.dot(s3_ref[...], taps, preferred_element_type=F32, precision=hp)
    g_r2 = jnp.dot(c4_ref[...], taps, preferred_element_type=F32, precision=hp)
    fwd = fwd_ref[...]
    inv = inv_ref[...]
    bias = b_ref[...]
    n_runs = h_ref.shape[0] // GRID_W
    rows = lambda r: slice(GRID_W * r, GRID_W * (r + 1))
    forward = lambda r: jnp.dot(fwd, h_ref[rows(r), :], preferred_element_type=F32)
    ahead = 2
    specs = [forward(r) for r in range(ahead)]
    for r in range(n_runs):
        if r + ahead < n_runs:
            specs.append(forward(r + ahead))
        spec = specs[r]
        a, bm = spec[0:GRID_W], spec[GRID_W:DFT_N]
        prod = jnp.concatenate([a * g_re - bm * g_im, a * g_im + bm * g_r2], axis=0).astype(BF16)
        o_ref[rows(r), :] = (jnp.dot(inv, prod, preferred_element_type=F32) + bias).astype(BF16)


def _fconv(h, taps, bias, consts, tm=TOKEN_TILE):
    fwd, inv, c1, s3, c4 = consts
    c = h.shape[-1]
    tile = pl.BlockSpec((None, tm, c), lambda b, i: (b, i, 0))
    full = lambda *s: pl.BlockSpec(s, lambda b, i: (0,) * len(s))
    return pl.pallas_call(
        _fconv_kernel,
        out_shape=jax.ShapeDtypeStruct(h.shape, BF16),
        grid=(B, L // tm),
        in_specs=[tile, full(TAPS_PAD, c), full(1, c), full(DFT_N, GRID_W), full(GRID_W, DFT_N),
                  full(GRID_W, TAPS_PAD), full(GRID_W, TAPS_PAD), full(GRID_W, TAPS_PAD)],
        out_specs=tile,
        compiler_params=_params("arbitrary", "arbitrary"),
        name="fconv",
    )(h, taps, bias, fwd, inv, c1, s3, c4)


def _tail1_kernel(x_ref, hcr_ref, hcc_ref, sz_ref, mod_ref, lng_ref, lnb_ref, wout_ref, ng_ref, nb_ref, o_ref,
                  col_ref):
    perm = _chunk_transpose_perm()
    for q in range(GRID_W // S5_T):
        blk = hcc_ref[S5_T * q:S5_T * (q + 1), :, :].reshape(PERM_ROWS, CONV_C)
        t = jnp.dot(perm, blk, preferred_element_type=F32)
        for r in range(TILE_ROWS):
            col_ref[GRID_W * r + S5_T * q:GRID_W * r + S5_T * (q + 1), :] = t[S5_T * r:S5_T * (r + 1), :]
    hc = jnp.concatenate([hcr_ref[...].astype(F32), col_ref[...]], axis=1)
    m = _silu(_layer_norm(hc, lng_ref[...], lnb_ref[...])) * sz_ref[...].astype(F32)
    y = jnp.dot(m.astype(BF16), wout_ref[...], preferred_element_type=F32)
    gmod = mod_ref[:, 2 * D:3 * D]
    o_ref[...] = _layer_norm(DN_ALPHA * x_ref[...] + gmod * y, ng_ref[...], nb_ref[...])


def _tail1(x, hc_row, hc_col, sz, mod, ln_g, ln_b, w_out, ng, nb, tm=TOKEN_TILE):
    tile = lambda w: pl.BlockSpec((None, tm, w), lambda b, i: (b, i, 0))
    full = lambda *s: pl.BlockSpec(s, lambda b, i: (0,) * len(s))
    return pl.pallas_call(
        _tail1_kernel,
        out_shape=jax.ShapeDtypeStruct((B, L, D), F32),
        grid=(B, L // tm),
        in_specs=[tile(D), tile(CONV_C),
                  pl.BlockSpec((None, GRID_W, TILE_ROWS, CONV_C), lambda b, i: (b, 0, i, 0)),
                  tile(D), pl.BlockSpec((None, 1, 3 * D), lambda b, i: (b, 0, 0)),
                  full(1, D), full(1, D), full(D, D), full(1, D), full(1, D)],
        out_specs=tile(D),
        scratch_shapes=[pltpu.VMEM((tm, CONV_C), F32)],
        compiler_params=_params("arbitrary", "arbitrary"),
        name="tail1",
    )(x, hc_row, hc_col, sz, mod, ln_g, ln_b, w_out, ng, nb)


def kernel(x, c, ctx, c_ctx, mod_w, mod_b, norm_g, norm_b, ev_w_in, ev_w_out, s5_lam_re, s5_lam_im, s5_log_dt, s5_b_re, s5_b_im, s5_c_re, s5_c_im, s5_d, glu_w, glu_b, sgu_ln_g, sgu_ln_b, sgu_w, sgu_b, od_w_in, od_w_out, dw_w, dw_b, conv_ln_g, conv_ln_b):
    TH = S5_T * S5_H
    row = lambda v: v.reshape(1, -1)

    cond8 = jnp.concatenate([c, c_ctx[None], jnp.zeros((3, D), F32)], axis=0)
    mods = _adaln(cond8, mod_w, mod_b)
    mod0 = mods[0, :B].reshape(B, 1, 3 * D)
    mod0c = mods[0, B:B + 1]
    mod1 = mods[1, :B].reshape(B, 1, 3 * D)

    lbr, lbi, cfr, cfi = _discretise(s5_lam_re[0], s5_lam_im[0], s5_log_dt[0])
    rowcat = lambda a: jnp.concatenate([a[0], a[1]], axis=-1).reshape(S5_G, 1, 2 * S5_P)
    bt = lambda a: jnp.concatenate([jnp.swapaxes(a[0], 1, 2), jnp.swapaxes(a[1], 1, 2)], axis=-1)
    cn = lambda a: jnp.concatenate([a[0], a[1]], axis=-1)
    d_row = jnp.tile(s5_d[0].reshape(S5_G, 1, S5_H), (1, 1, S5_T))
    win, wout, mix, l16 = _s5_weights(rowcat(lbr), rowcat(lbi), rowcat(cfr), rowcat(cfi),
                                      bt(s5_b_re[0]), bt(s5_b_im[0]), cn(s5_c_re[0]), cn(s5_c_im[0]), d_row)

    w_out0 = ev_w_out[0].astype(BF16)
    glu_w0 = glu_w[0].astype(BF16)
    guz, vln, hs = _inproj0n(x, mod0, ev_w_in[0], row(sgu_ln_g[0]), row(sgu_ln_b[0]))
    ua, sza, ua_c = _inproj0a(hs, _ctx_slabs(ctx, mod0c), ev_w_in[0])
    s_lat = _s5core(ua, ua_c, win, wout, mix, l16)
    y_s5 = _s5tail(s_lat, sza, glu_w0, row(glu_b[0]), w_out0)
    sguw = sgu_w[0].reshape(SGU_HEADS // 2, 2, SGU_CHUNK, SGU_CHUNK)
    sguw = jnp.transpose(sguw, (0, 2, 1, 3)).reshape(SGU_HEADS // 2, SGU_CHUNK, 2 * SGU_CHUNK).astype(BF16)
    sgub = jnp.repeat(sgu_b[0].T, SGU_HD, axis=1)
    x1 = _tail0(x, y_s5, guz, vln, mod0, sguw, sgub, w_out0, row(norm_g[0]), row(norm_b[0]))

    hg_row, hg_col, sz = _inproj1(x1, mod1, od_w_in[0])
    consts = _dft_constants()
    taps = jnp.pad(dw_w[0], ((0, TAPS_PAD - CONV_K), (0, 0)))
    bias = row(dw_b[0])
    hc_row = _fconv(hg_row, taps[:, :CONV_C], bias[:, :CONV_C], consts)
    hc_col = _fconv(hg_col.reshape(B, L, CONV_C), taps[:, CONV_C:], bias[:, CONV_C:], consts)
    return _tail1(x1, hc_row, hc_col.reshape(B, GRID_W, GRID_W, CONV_C), sz, mod1,
                  row(conv_ln_g[0]), row(conv_ln_b[0]), od_w_out[0].astype(BF16), row(norm_g[1]), row(norm_b[1]))
```

```python
import functools
import math

import jax
import jax.numpy as jnp
from jax import lax
from jax.experimental import pallas as pl
from jax.experimental.pallas import tpu as pltpu

D = 1024
B = 4
L = 4096
CTX = 256
GRID_W = 64
S5_W = 512
S5_G = 32
S5_H = 16
H_SHIFT = 4
BLK = 128 // S5_H
S5_P = 64
S5_T = 16
SGU_W = 512
SGU_HEADS = 8
SGU_HD = 64
SGU_CHUNK = 128
CONV_K = 31
CONV_HALF = CONV_K // 2
EVEN_IN = 2560
SGU_COL0 = 2 * S5_W
ODD_IN = 3072
DEPTH = 2
DN_ALPHA = (2 * DEPTH) ** 0.25
LN_EPS = 1e-5
N_CHUNK = L // S5_T
N_CCHUNK = CTX // S5_T
VMEM_LIMIT_V7X = 56 * 1024 * 1024
TOKEN_TILE = 1024

F32 = jnp.float32
BF16 = jnp.bfloat16


GELU_C = math.sqrt(2.0 / math.pi)


def _gelu(x):
    hx = 0.5 * x
    return hx * jnp.tanh(x * ((x * x) * (0.044715 * GELU_C) + GELU_C)) + hx


def _sigmoid(x):
    return 0.5 * jnp.tanh(0.5 * x) + 0.5


def _silu(x):
    hx = 0.5 * x
    return hx * jnp.tanh(hx) + hx


def _layer_norm(x, g, b):
    mu = jnp.mean(x, axis=-1, keepdims=True)
    xc = x - mu
    var = jnp.mean(xc * xc, axis=-1, keepdims=True)
    return xc * lax.rsqrt(var + LN_EPS) * g + b


def _params(*sem):
    return pltpu.CompilerParams(dimension_semantics=sem, vmem_limit_bytes=VMEM_LIMIT_V7X)


def _adaln_kernel(c_ref, w_ref, b_ref, o_ref):
    def split(v):
        hi = v.astype(BF16)
        return hi, (v - hi.astype(F32)).astype(BF16)

    a_hi, a_lo = split(_silu(c_ref[...]))
    w_hi, w_lo = split(w_ref[...])
    dot = functools.partial(jnp.dot, preferred_element_type=F32)
    o_ref[...] = dot(a_hi, w_hi) + dot(a_lo, w_hi) + dot(a_hi, w_lo) + b_ref[...]


def _adaln(cond8, mod_w, mod_b):
    tn = 512
    return pl.pallas_call(
        _adaln_kernel,
        out_shape=jax.ShapeDtypeStruct((DEPTH, 8, 3 * D), F32),
        grid=(DEPTH, 3 * D // tn),
        in_specs=[pl.BlockSpec((8, D), lambda l, j: (0, 0)),
                  pl.BlockSpec((None, D, tn), lambda l, j: (l, 0, j)),
                  pl.BlockSpec((None, 1, tn), lambda l, j: (l, 0, j))],
        out_specs=pl.BlockSpec((None, 8, tn), lambda l, j: (l, 0, j)),
        compiler_params=_params("arbitrary", "arbitrary"),
        name="adaln",
    )(cond8, mod_w, mod_b.reshape(DEPTH, 1, 3 * D))


def _disc_kernel(lr_ref, li_ref, ldt_ref, obr_ref, obi_ref, ocr_ref, oci_ref):
    lr = lr_ref[...]
    li = li_ref[...]
    dt = jnp.exp(ldt_ref[...])
    mag = jnp.exp(lr * dt)
    br = mag * jnp.cos(li * dt)
    bi = mag * jnp.sin(li * dt)
    inv = 1.0 / (lr * lr + li * li)
    nr = br - 1.0
    obr_ref[...] = br
    obi_ref[...] = bi
    ocr_ref[...] = (nr * lr + bi * li) * inv
    oci_ref[...] = (bi * lr - nr * li) * inv


def _discretise(lam_re, lam_im, log_dt):
    shp = jax.ShapeDtypeStruct((2 * S5_G, S5_P), F32)
    ldt = jnp.broadcast_to(log_dt.reshape(2 * S5_G, 1), (2 * S5_G, S5_P))
    outs = pl.pallas_call(
        _disc_kernel, out_shape=(shp, shp, shp, shp), name="s5_discretise",
    )(lam_re.reshape(2 * S5_G, S5_P), lam_im.reshape(2 * S5_G, S5_P), ldt)
    return [o.reshape(2, S5_G, S5_P) for o in outs]


S5W_GROUPS = 4


def _cpow(base_pows, j):
    re = None
    im = None
    for k, (pr, pi) in enumerate(base_pows):
        bit = ((j >> k) & 1) == 1
        mr = jnp.where(bit, pr, 1.0)
        mi = jnp.where(bit, pi, 0.0)
        if re is None:
            re, im = mr, mi
        else:
            re, im = re * mr - im * mi, re * mi + im * mr
    return re, im


def _squarings(pr, pi, n):
    out = [(pr, pi)]
    for _ in range(n - 1):
        pr, pi = pr * pr - pi * pi, 2.0 * pr * pi
        out.append((pr, pi))
    return out


def _shift_lanes(x, n):
    lane = lax.broadcasted_iota(jnp.int32, (S5_H, 128), 1)
    lo, hi = x[:, :128], x[:, 128:]
    if n == 0:
        return x
    if n < 128:
        rlo = pltpu.roll(lo, n, axis=1)
        rhi = pltpu.roll(hi, n, axis=1)
        return jnp.concatenate([jnp.where(lane >= n, rlo, 0.0), jnp.where(lane >= n, rhi, rlo)], axis=1)
    m = n - 128
    rlo = lo if m == 0 else pltpu.roll(lo, m, axis=1)
    return jnp.concatenate([jnp.zeros_like(lo), jnp.where(lane >= m, rlo, 0.0)], axis=1)


def _unshift_lanes(x, n):
    lane = lax.broadcasted_iota(jnp.int32, (S5_H, 128), 1)
    lo, hi = x[:, :128], x[:, 128:]
    if n == 0:
        return x
    if n < 128:
        rlo = pltpu.roll(lo, 128 - n, axis=1)
        rhi = pltpu.roll(hi, 128 - n, axis=1)
        keep = lane < 128 - n
        return jnp.concatenate([jnp.where(keep, rlo, rhi), jnp.where(keep, rhi, 0.0)], axis=1)
    m = n - 128
    rhi = hi if m == 0 else pltpu.roll(hi, 128 - m, axis=1)
    return jnp.concatenate([jnp.where(lane < 128 - m, rhi, 0.0), jnp.zeros_like(lo)], axis=1)


def _s5w_group(gi, bg, lrow_re, lrow_im, crow_re, crow_im, bt_re, bt_im,
               cn_re, cn_im, d_ref, win_ref, wout_ref, mix_ref, l16_ref):
    TH = S5_T * S5_H

    def chunk_pos(idx):
        return (((idx >> H_SHIFT) - bg) & (BLK - 1)) + ((idx >> 7) << 3)

    lr = lrow_re[gi]
    li = lrow_im[gi]
    pows_row = _squarings(lr, li, 5)
    l16_ref[gi, 0:1, :] = pows_row[4][0]
    l16_ref[gi, 1:2, :] = pows_row[4][1]
    l16_ref[gi, 2:8, :] = jnp.zeros((6, 128), F32)
    cr = crow_re[gi]
    ci = crow_im[gi]
    btr = bt_re[gi]
    bti = bt_im[gi]
    bbr = cr * btr - ci * bti
    bbi = cr * bti + ci * btr
    blk16 = lax.broadcasted_iota(jnp.int32, (S5_T, 128), 0)
    is_f16 = lax.broadcasted_iota(jnp.int32, (S5_T, 128), 1) < S5_P
    pos16 = chunk_pos(blk16 << H_SHIFT)
    pr16, pi16 = _cpow(pows_row[:4], jnp.where(is_f16, S5_T - 1 - pos16, pos16))
    rep_rows = lambda v: jnp.broadcast_to(v[:, None, :], (S5_T, S5_H, 128)).reshape(TH, 128)
    pr, pi = rep_rows(pr16), rep_rows(pi16)
    tbr = jnp.broadcast_to(bbr[None], (S5_T, S5_H, 128)).reshape(TH, 128)
    tbi = jnp.broadcast_to(bbi[None], (S5_T, S5_H, 128)).reshape(TH, 128)
    win_ref[gi, :, 0:128] = (pr * tbr - pi * tbi).astype(BF16)
    win_ref[gi, :, 128:256] = (pr * tbi + pi * tbr).astype(BF16)

    hp = lax.Precision.HIGHEST
    dot = functools.partial(jnp.dot, preferred_element_type=F32, precision=hp)
    def col256(r):
        col = jnp.broadcast_to(r, (2 * S5_P, 2 * S5_P)).T
        return jnp.concatenate([col, col], axis=1)

    def tiled_t(cn):
        t8 = jnp.broadcast_to(cn[None], (BLK, S5_H, 2 * S5_P)).reshape(2 * S5_P, 2 * S5_P).T
        return jnp.concatenate([t8, t8], axis=1)

    cpows = _squarings(col256(lr), col256(li), 4)
    row = lax.broadcasted_iota(jnp.int32, (2 * S5_P, TH), 0)
    lane_w = lax.broadcasted_iota(jnp.int32, (2 * S5_P, TH), 1)
    t_idx = chunk_pos(lane_w)
    j_idx = lane_w >> H_SHIFT
    is_f = row < S5_P
    ctr = tiled_t(cn_re[gi])
    cti = tiled_t(cn_im[gi])
    er, ei = _cpow(cpows, jnp.where(is_f, t_idx, S5_T - 1 - t_idx))
    er, ei = er * cpows[0][0] - ei * cpows[0][1], er * cpows[0][1] + ei * cpows[0][0]
    wr = ctr * er - cti * ei
    wi = ctr * ei + cti * er
    wout_ref[gi, 0:128, :] = wr.astype(BF16)
    wout_ref[gi, 128:256, :] = (-wi).astype(BF16)
    kr, ki = _cpow(cpows, jnp.where(is_f, j_idx, S5_T - 1 - j_idx))
    ekr = ctr * kr - cti * ki
    eki = ctr * ki + cti * kr
    lane16 = lax.broadcasted_iota(jnp.int32, (S5_H, 128), 1)
    mf = lane16 < S5_P
    kkf = dot(jnp.where(mf, bbr, 0.0), ekr) - dot(jnp.where(mf, bbi, 0.0), eki)
    kkb = dot(jnp.where(mf, 0.0, bbr), ekr) - dot(jnp.where(mf, 0.0, bbi), eki)
    dl = d_ref[gi]
    r16 = lax.broadcasted_iota(jnp.int32, (S5_H, TH), 0)
    l256 = lax.broadcasted_iota(jnp.int32, (S5_H, TH), 1)
    rot = bg * S5_H
    for s in range(S5_T):
        blk = _shift_lanes(kkf, S5_H * s) + _unshift_lanes(kkb, S5_H * (S5_T - 1 - s))
        blk = blk + jnp.where(l256 == r16 + S5_H * s, dl, 0.0)
        blk = jnp.concatenate([pltpu.roll(blk[:, :128], rot, axis=1), pltpu.roll(blk[:, 128:], rot, axis=1)], axis=1)
        rho = ((s + bg) & (BLK - 1)) + (s & BLK)
        mix_ref[gi, pl.ds(pl.multiple_of(rho * S5_H, S5_H), S5_H), :] = blk.astype(BF16)


def _s5w_kernel(*refs):
    for gi in range(S5W_GROUPS):
        bg = (pl.program_id(0) * S5W_GROUPS + gi) & (BLK - 1)
        _s5w_group(gi, bg, *refs)


def _s5_weights(lrow_re, lrow_im, crow_re, crow_im, bt_re, bt_im, cn_re, cn_im, d_row):
    TH = S5_T * S5_H
    g3 = lambda r, c: pl.BlockSpec((S5W_GROUPS, r, c), lambda g: (g, 0, 0))
    wshape = jax.ShapeDtypeStruct((S5_G, TH, TH), BF16)
    return pl.pallas_call(
        _s5w_kernel,
        out_shape=(wshape, wshape, wshape, jax.ShapeDtypeStruct((S5_G, 8, 128), F32)),
        grid=(S5_G // S5W_GROUPS,),
        in_specs=[g3(1, 128)] * 4 + [g3(S5_H, 128)] * 4 + [g3(1, TH)],
        out_specs=(g3(TH, TH), g3(TH, TH), g3(TH, TH), g3(8, 128)),
        compiler_params=_params("arbitrary"),
        name="s5_weights",
    )(lrow_re, lrow_im, crow_re, crow_im, bt_re, bt_im, cn_re, cn_im, d_row)


def _rot_blocks(v, r):
    cols = [pltpu.roll(v[:, 128 * q:128 * (q + 1)], S5_H * r, axis=1) for q in range(v.shape[1] // 128)]
    return jnp.concatenate(cols, axis=1)


def _slabs_of(h, hs_ref):
    h3 = h.reshape(h.shape[0] // S5_T, S5_T, h.shape[1])
    for s in range(S5_T):
        hs_ref[s] = h3[:, s, :].astype(BF16)


PERM_ROWS = S5_T * S5_T


def _chunk_transpose_perm():
    ri = lax.broadcasted_iota(jnp.int32, (PERM_ROWS, PERM_ROWS), 0)
    ci = lax.broadcasted_iota(jnp.int32, (PERM_ROWS, PERM_ROWS), 1)
    hit = ((ri >> H_SHIFT) == (ci & (S5_T - 1))) & ((ri & (S5_T - 1)) == (ci >> H_SHIFT))
    return jnp.where(hit, 1.0, 0.0).astype(BF16)


def _inproj0n_kernel(x_ref, mod_ref, w_ref, lng_ref, lnb_ref, guz_ref, vln_ref, hs_ref):
    shift = mod_ref[:, 0:D]
    scale = mod_ref[:, D:2 * D]
    hb = (x_ref[...] * (1.0 + scale) + shift).astype(BF16)
    perm = _chunk_transpose_perm()
    for j in range(hb.shape[0] // PERM_ROWS):
        blk = jnp.dot(perm, hb[PERM_ROWS * j:PERM_ROWS * (j + 1), :], preferred_element_type=F32).astype(BF16)
        for s in range(S5_T):
            hs_ref[s, S5_T * j:S5_T * (j + 1), :] = blk[S5_T * s:S5_T * (s + 1), :]
    dot = lambda lo: jnp.dot(hb, w_ref[:, SGU_COL0 + lo:SGU_COL0 + lo + 512].astype(BF16),
                             preferred_element_type=F32)
    guz_ref[...] = (_gelu(dot(0)) * _silu(dot(1024))).astype(BF16)
    vln_ref[...] = _layer_norm(_gelu(dot(512)), lng_ref[...], lnb_ref[...]).astype(BF16)


def _inproj0n(x, mod, w_in_f32, ln_g, ln_b, tm=TOKEN_TILE):
    nct = tm // S5_T
    o = jax.ShapeDtypeStruct((B, L, 512), BF16)
    ospec = pl.BlockSpec((None, tm, 512), lambda b, i: (b, i, 0))
    full = lambda *s: pl.BlockSpec(s, lambda b, i: (0,) * len(s))
    return pl.pallas_call(
        _inproj0n_kernel,
        out_shape=(o, o, jax.ShapeDtypeStruct((S5_T, B * N_CHUNK, D), BF16)),
        grid=(B, L // tm),
        in_specs=[pl.BlockSpec((None, tm, D), lambda b, i: (b, i, 0)),
                  pl.BlockSpec((None, 1, 3 * D), lambda b, i: (b, 0, 0)),
                  pl.BlockSpec((D, EVEN_IN), lambda b, i: (0, 0), pipeline_mode=pl.Buffered(1)),
                  full(1, 512), full(1, 512)],
        out_specs=(ospec, ospec,
                   pl.BlockSpec((S5_T, nct, D), lambda b, i: (0, b * (N_CHUNK // nct) + i, 0))),
        compiler_params=_params("arbitrary", "arbitrary"),
        name="inproj0n",
    )(x, mod, w_in_f32, ln_g, ln_b)


def _ctx_slabs_kernel(x_ref, mod_ref, hs_ref):
    h = x_ref[...] * (1.0 + mod_ref[:, D:2 * D]) + mod_ref[:, 0:D]
    _slabs_of(h, hs_ref)


def _ctx_slabs(ctx, mod_c):
    return pl.pallas_call(
        _ctx_slabs_kernel,
        out_shape=jax.ShapeDtypeStruct((S5_T, B * N_CCHUNK, D), BF16),
        grid=(B,),
        in_specs=[pl.BlockSpec((None, CTX, D), lambda b: (b, 0, 0)),
                  pl.BlockSpec((1, 3 * D), lambda b: (0, 0))],
        out_specs=pl.BlockSpec((S5_T, N_CCHUNK, D), lambda b: (0, b, 0)),
        compiler_params=_params("arbitrary"),
        name="ctx_slabs",
    )(ctx, mod_c)


def _inproj0a_kernel(hs_ref, hc_ref, w_ref, ua_ref, sza_ref, uc_ref):
    r = pl.program_id(0)
    h = hs_ref[...]
    w_ua = w_ref[:, 0:512].astype(BF16)
    ua_ref[...] = _rot_blocks(jnp.dot(h, w_ua, preferred_element_type=F32), r).astype(BF16)
    sza_ref[...] = _silu(jnp.dot(h, w_ref[:, 512:1024].astype(BF16), preferred_element_type=F32)).astype(BF16)
    uc_ref[...] = _rot_blocks(jnp.dot(hc_ref[...], w_ua, preferred_element_type=F32), r).astype(BF16)


def _inproj0a(hs, hcs, w_in_f32):
    slab = lambda r, h: r + BLK * h
    sspec = lambda n, w: pl.BlockSpec((None, n, w), lambda r, h: (slab(r, h), 0, 0))
    so = lambda n: jax.ShapeDtypeStruct((S5_T, n, 512), BF16)
    nl, ncx = B * N_CHUNK, B * N_CCHUNK
    return pl.pallas_call(
        _inproj0a_kernel,
        out_shape=(so(nl), so(nl), so(ncx)),
        grid=(BLK, S5_T // BLK),
        in_specs=[sspec(nl, D), sspec(ncx, D), pl.BlockSpec((D, SGU_COL0), lambda r, h: (0, 0))],
        out_specs=(sspec(nl, 512), sspec(nl, 512), sspec(ncx, 512)),
        compiler_params=_params("arbitrary", "arbitrary"),
        name="inproj0a",
    )(hs, hcs, w_in_f32)


SCAN_GROUPS = 4


def _scan_tiles(sre_ref, sim_ref, h_refs, n_tiles, carry, lams):
    row = lax.broadcasted_iota(jnp.int32, (8, 128), 0)
    lane = lax.broadcasted_iota(jnp.int32, (8, 128), 1)
    first = row < B
    fwd = lane < S5_P

    def body(k, c):
        of = pl.multiple_of(k * 8, 8)
        ob = pl.multiple_of((n_tiles - 1 - k) * 8, 8)
        out = []
        for gi in range(SCAN_GROUPS):
            lre, lim = lams[gi]
            hr, hi = c[2 * gi], c[2 * gi + 1]
            sr = jnp.where(fwd, sre_ref[gi, pl.ds(of, 8), :], pltpu.roll(sre_ref[gi, pl.ds(ob, 8), :], B, axis=0))
            si = jnp.where(fwd, sim_ref[gi, pl.ds(of, 8), :], pltpu.roll(sim_ref[gi, pl.ds(ob, 8), :], B, axis=0))
            h1r = lre * hr - lim * hi + sr
            h1i = lre * hi + lim * hr + si
            r1r = pltpu.roll(h1r, B, axis=0)
            r1i = pltpu.roll(h1i, B, axis=0)
            if h_refs is not None:
                fre_ref, fim_ref, bre_ref, bim_ref = h_refs
                er = jnp.where(first, hr, r1r)
                ei = jnp.where(first, hi, r1i)
                fre_ref[gi, pl.ds(of, 8), :] = er
                fim_ref[gi, pl.ds(of, 8), :] = ei
                bre_ref[gi, pl.ds(ob, 8), :] = pltpu.roll(er, B, axis=0)
                bim_ref[gi, pl.ds(ob, 8), :] = pltpu.roll(ei, B, axis=0)
            h2r = lre * r1r - lim * r1i + sr
            h2i = lre * r1i + lim * r1r + si
            out.append(jnp.where(first, pltpu.roll(h2r, B, axis=0), h2r))
            out.append(jnp.where(first, pltpu.roll(h2i, B, axis=0), h2i))
        return tuple(out)

    return lax.fori_loop(0, n_tiles, body, carry)


def _gather_group(slab_ref, src):
    halves = []
    for h in range(S5_T // BLK):
        acc = slab_ref[BLK * h]
        for s in range(1, BLK):
            acc = jnp.where(src == s, slab_ref[BLK * h + s], acc)
        halves.append(acc)
    return jnp.concatenate(halves, axis=1)


def _s5core_kernel(ul_ref, uc_ref, win_ref, wout_ref, mix_ref, l16_ref, o_ref,
                   u_ref, sre_ref, sim_ref, cre_ref, cim_ref, fre_ref, fim_ref, bre_ref, bim_ref, y_ref):
    nl = N_CHUNK * B
    ncx = N_CCHUNK * B
    blk_l = lax.broadcasted_iota(jnp.int32, (nl, 128), 1) >> H_SHIFT
    blk_c = lax.broadcasted_iota(jnp.int32, (ncx, 128), 1) >> H_SHIFT
    fwd = lax.broadcasted_iota(jnp.int32, (N_CHUNK, 128), 1) < S5_P
    for g0 in range(0, BLK, SCAN_GROUPS):
        for gi in range(SCAN_GROUPS):
            bg = g0 + gi
            win = win_ref[bg]
            src_l = ((blk_l - bg) & (BLK - 1)).astype(F32).astype(BF16)
            src_c = ((blk_c - bg) & (BLK - 1)).astype(F32).astype(BF16)
            u = _gather_group(ul_ref, src_l)
            u_ref[gi] = u
            sl = jnp.dot(u, win, preferred_element_type=F32)
            sc = jnp.dot(_gather_group(uc_ref, src_c), win, preferred_element_type=F32)
            for b in range(B):
                sre_ref[gi, pl.ds(b, N_CHUNK, stride=B), :] = sl[N_CHUNK * b:N_CHUNK * (b + 1), 0:128]
                sim_ref[gi, pl.ds(b, N_CHUNK, stride=B), :] = sl[N_CHUNK * b:N_CHUNK * (b + 1), 128:256]
                cre_ref[gi, pl.ds(b, N_CCHUNK, stride=B), :] = sc[N_CCHUNK * b:N_CCHUNK * (b + 1), 0:128]
                cim_ref[gi, pl.ds(b, N_CCHUNK, stride=B), :] = sc[N_CCHUNK * b:N_CCHUNK * (b + 1), 128:256]
        lams = [(jnp.broadcast_to(l16_ref[g0 + gi, 0:1, :], (8, 128)),
                 jnp.broadcast_to(l16_ref[g0 + gi, 1:2, :], (8, 128))) for gi in range(SCAN_GROUPS)]
        zero = tuple(jnp.zeros((8, 128), F32) for _ in range(2 * SCAN_GROUPS))
        carry = _scan_tiles(cre_ref, cim_ref, None, ncx // 8, zero, lams)
        _scan_tiles(sre_ref, sim_ref, (fre_ref, fim_ref, bre_ref, bim_ref), nl // 8, carry, lams)
        for gi in range(SCAN_GROUPS):
            bg = g0 + gi
            y = jnp.dot(u_ref[gi], mix_ref[bg], preferred_element_type=F32)
            hs = []
            for b in range(B):
                rows = pl.ds(b, N_CHUNK, stride=B)
                hs.append(jnp.concatenate([jnp.where(fwd, fre_ref[gi, rows, :], bre_ref[gi, rows, :]),
                                           jnp.where(fwd, fim_ref[gi, rows, :], bim_ref[gi, rows, :])], axis=1))
            hcat = jnp.concatenate(hs, axis=0).astype(BF16)
            y = y + jnp.dot(hcat, wout_ref[bg], preferred_element_type=F32)
            y_ref[bg] = y.astype(BF16)

    blk = blk_l.astype(F32).astype(BF16)
    for s in range(S5_T):
        h, r = s // BLK, s % BLK
        acc = None
        for j in range(BLK):
            piece = y_ref[(j - r) % BLK, :, 128 * h:128 * (h + 1)]
            acc = piece if acc is None else jnp.where(blk == j, piece, acc)
        o_ref[s] = acc


def _s5core(ul, uc, win, wout, mix, l16):
    TH = S5_T * S5_H
    nl = N_CHUNK * B
    ncx = N_CCHUNK * B
    g4 = lambda r, c: pl.BlockSpec((BLK, r, c), lambda q: (q, 0, 0))
    col = lambda n: pl.BlockSpec((S5_T, n, 128), lambda q: (0, 0, q))
    f32s = lambda n: pltpu.VMEM((SCAN_GROUPS, n, 128), F32)
    return pl.pallas_call(
        _s5core_kernel,
        out_shape=jax.ShapeDtypeStruct((S5_T, nl, S5_W), BF16),
        grid=(S5_G // BLK,),
        in_specs=[col(nl), col(ncx), g4(TH, TH), g4(TH, TH), g4(TH, TH), g4(8, 128)],
        out_specs=col(nl),
        scratch_shapes=[pltpu.VMEM((SCAN_GROUPS, nl, TH), BF16),
                        f32s(nl), f32s(nl), f32s(ncx), f32s(ncx), f32s(nl), f32s(nl), f32s(nl), f32s(nl),
                        pltpu.VMEM((BLK, nl, TH), BF16)],
        compiler_params=_params("arbitrary"),
        name="s5core",
    )(ul, uc, win, wout, mix, l16)


def _s5tail_kernel(slat_ref, sza_ref, gluw_ref, glub_ref, wtop_ref, y_ref):
    unrot = (BLK - pl.program_id(0)) & (BLK - 1)
    for b in range(B):
        rows = slice(N_CHUNK * b, N_CHUNK * (b + 1))
        g = _gelu(_rot_blocks(slat_ref[rows, :].astype(F32), unrot))
        gate = _sigmoid(jnp.dot(g.astype(BF16), gluw_ref[...], preferred_element_type=F32) + glub_ref[...])
        a = (g * gate * sza_ref[rows, :].astype(F32)).astype(BF16)
        y_ref[rows, :] = jnp.dot(a, wtop_ref[...], preferred_element_type=F32).astype(BF16)


def _s5tail(slat, sza, glu_w, glu_b, w_top):
    slab = lambda r, h: r + BLK * h
    sspec = lambda w: pl.BlockSpec((None, N_CHUNK * B, w), lambda r, h: (slab(r, h), 0, 0))
    full = lambda *s: pl.BlockSpec(s, lambda r, h: (0,) * len(s))
    return pl.pallas_call(
        _s5tail_kernel,
        out_shape=jax.ShapeDtypeStruct((S5_T, N_CHUNK * B, D), BF16),
        grid=(BLK, S5_T // BLK),
        in_specs=[sspec(512), sspec(512), full(512, 512), full(1, 512), full(S5_W, D)],
        out_specs=sspec(D),
        compiler_params=_params("arbitrary", "arbitrary"),
        name="s5tail",
    )(slat, sza, glu_w, glu_b, w_top)


def _tail0_kernel(x_ref, ys5_ref, guz_ref, vln_ref, mod_ref, sguw_ref, sgub_ref, wbot_ref, ng_ref, nb_ref, o_ref):
    tm = x_ref.shape[0]
    lane = lax.broadcasted_iota(jnp.int32, (SGU_CHUNK, 128), 1)
    lo = lane < SGU_HD
    zero = jnp.zeros((SGU_CHUNK, 128), BF16)
    chunks = []
    for ci in range(tm // SGU_CHUNK):
        v = vln_ref[ci * SGU_CHUNK:(ci + 1) * SGU_CHUNK, :]
        cols = []
        for pi in range(SGU_HEADS // 2):
            vp = v[:, 128 * pi:128 * (pi + 1)]
            bm = jnp.concatenate([jnp.where(lo, vp, zero), jnp.where(lo, zero, vp)], axis=0)
            cols.append(jnp.dot(sguw_ref[pi], bm, preferred_element_type=F32))
        chunks.append(jnp.concatenate(cols, axis=1) + sgub_ref[...])
    s = jnp.concatenate(chunks, axis=0)
    bsg = (guz_ref[...].astype(F32) * s).astype(BF16)
    perm = _chunk_transpose_perm()
    ys5 = jnp.concatenate(
        [jnp.dot(perm, ys5_ref[:, S5_T * j:S5_T * (j + 1), :].reshape(PERM_ROWS, D), preferred_element_type=F32)
         for j in range(tm // PERM_ROWS)], axis=0)
    y = ys5 + jnp.dot(bsg, wbot_ref[...], preferred_element_type=F32)
    gmod = mod_ref[:, 2 * D:3 * D]
    o_ref[...] = _layer_norm(DN_ALPHA * x_ref[...] + gmod * y, ng_ref[...], nb_ref[...])


def _tail0(x, ys5, guz, vln, mod, sguw, sgub, w_bot, ng, nb, tm=TOKEN_TILE):
    nct = tm // S5_T
    t512 = pl.BlockSpec((None, tm, 512), lambda b, i: (b, i, 0))
    tD = pl.BlockSpec((None, tm, D), lambda b, i: (b, i, 0))
    full = lambda *s: pl.BlockSpec(s, lambda b, i: (0,) * len(s))
    return pl.pallas_call(
        _tail0_kernel,
        out_shape=jax.ShapeDtypeStruct((B, L, D), F32),
        grid=(B, L // tm),
        in_specs=[tD, pl.BlockSpec((S5_T, nct, D), lambda b, i: (0, b * (N_CHUNK // nct) + i, 0)), t512, t512,
                  pl.BlockSpec((None, 1, 3 * D), lambda b, i: (b, 0, 0)),
                  full(SGU_HEADS // 2, SGU_CHUNK, 256), full(SGU_CHUNK, 512),
                  pl.BlockSpec((SGU_W, D), lambda b, i: (1, 0)), full(1, D), full(1, D)],
        out_specs=tD,
        compiler_params=_params("arbitrary", "arbitrary"),
        name="tail0",
    )(x, ys5, guz, vln, mod, sguw, sgub, w_bot, ng, nb)


CONV_C = D // 2
TILE_ROWS = TOKEN_TILE // GRID_W


def _grid_transpose_in(v, o_ref):
    perm = _chunk_transpose_perm()
    for q in range(GRID_W // S5_T):
        seg = jnp.concatenate([v[GRID_W * r + S5_T * q:GRID_W * r + S5_T * (q + 1), :] for r in range(TILE_ROWS)],
                              axis=0)
        t = jnp.dot(perm, seg, preferred_element_type=F32).astype(BF16)
        o_ref[S5_T * q:S5_T * (q + 1), :, :] = t.reshape(S5_T, TILE_ROWS, v.shape[1])


def _inproj1_kernel(x_ref, mod_ref, w_ref, hgr_ref, hgc_ref):
    shift = mod_ref[:, 0:D]
    scale = mod_ref[:, D:2 * D]
    h = (x_ref[...] * (1.0 + scale) + shift).astype(BF16)
    dot = lambda lo: jnp.dot(h, w_ref[:, lo:lo + CONV_C].astype(BF16), preferred_element_type=F32)
    hgr_ref[...] = (dot(0) * _sigmoid(dot(D))).astype(BF16)
    _grid_transpose_in((dot(CONV_C) * _sigmoid(dot(D + CONV_C))).astype(BF16), hgc_ref)


def _inproj1(x, mod, w_in_f32, tm=TOKEN_TILE):
    tile = lambda w: pl.BlockSpec((None, tm, w), lambda b, i: (b, i, 0))
    return pl.pallas_call(
        _inproj1_kernel,
        out_shape=(jax.ShapeDtypeStruct((B, L, CONV_C), BF16),
                   jax.ShapeDtypeStruct((B, GRID_W, GRID_W, CONV_C), BF16)),
        grid=(B, L // tm),
        in_specs=[tile(D),
                  pl.BlockSpec((None, 1, 3 * D), lambda b, i: (b, 0, 0)),
                  pl.BlockSpec((D, 2 * D), lambda b, i: (0, 0), pipeline_mode=pl.Buffered(1))],
        out_specs=(tile(CONV_C), pl.BlockSpec((None, GRID_W, TILE_ROWS, CONV_C), lambda b, i: (b, 0, i, 0))),
        compiler_params=_params("arbitrary", "arbitrary"),
        name="inproj1",
    )(x, mod, w_in_f32)


DFT_N = 2 * GRID_W
TAPS_PAD = CONV_K + 1


def _dft_constants():
    th = 2.0 * math.pi / DFT_N
    f = jnp.arange(GRID_W, dtype=F32)[:, None]
    p = jnp.arange(GRID_W, dtype=F32)[None, :]
    cosm = jnp.cos(th * f * p)
    sinm = jnp.sin(th * f * p)
    alt = jnp.where(jnp.arange(GRID_W) % 2 == 0, 1.0, -1.0).astype(F32)
    fwd = jnp.concatenate([cosm, alt[None, :], sinm[1:]], axis=0)
    cf = jnp.where(jnp.arange(GRID_W) == 0, 1.0, 2.0).astype(F32) / DFT_N
    inv = jnp.concatenate([cosm.T * cf[None, :], (alt / DFT_N)[:, None], sinm.T[:, 1:] * (2.0 / DFT_N)], axis=1)
    sft = (CONV_HALF - jnp.arange(TAPS_PAD, dtype=F32))[None, :]
    live = (jnp.arange(TAPS_PAD) < CONV_K).astype(F32)[None, :]
    f64 = jnp.where(f == 0, float(GRID_W), f)
    return (fwd.astype(BF16), inv.astype(BF16),
            jnp.cos(th * f * sft) * live, jnp.sin(th * f * sft) * live, jnp.cos(th * f64 * sft) * live)


def _fconv_kernel(h_ref, w_ref, b_ref, fwd_ref, inv_ref, c1_ref, s3_ref, c4_ref, o_ref):
    hp = lax.Precision.HIGHEST
    taps = w_ref[...]
    g_re = jnp.dot(c1_ref[...], taps, preferred_element_type=F32, precision=hp)
    g_im = jnp.dot(s3_ref[...], taps, preferred_element_type=F32, precision=hp)
    g_r2 = jnp.dot(c4_ref[...], taps, preferred_element_type=F32, precision=hp)
    fwd = fwd_ref[...]
    inv = inv_ref[...]
    bias = b_ref[...]
    n_runs = h_ref.shape[0] // GRID_W
    rows = lambda r: slice(GRID_W * r, GRID_W * (r + 1))
    forward = lambda r: jnp.dot(fwd, h_ref[rows(r), :], preferred_element_type=F32)
    ahead = 2
    specs = [forward(r) for r in range(ahead)]
    for r in range(n_runs):
        if r + ahead < n_runs:
            specs.append(forward(r + ahead))
        spec = specs[r]
        a, bm = spec[0:GRID_W], spec[GRID_W:DFT_N]
        prod = jnp.concatenate([a * g_re - bm * g_im, a * g_im + bm * g_r2], axis=0).astype(BF16)
        o_ref[rows(r), :] = (jnp.dot(inv, prod, preferred_element_type=F32) + bias).astype(BF16)


def _fconv(h, taps, bias, consts, tm=TOKEN_TILE):
    fwd, inv, c1, s3, c4 = consts
    c = h.shape[-1]
    tile = pl.BlockSpec((None, tm, c), lambda b, i: (b, i, 0))
    full = lambda *s: pl.BlockSpec(s, lambda b, i: (0,) * len(s))
    return pl.pallas_call(
        _fconv_kernel,
        out_shape=jax.ShapeDtypeStruct(h.shape, BF16),
        grid=(B, L // tm),
        in_specs=[tile, full(TAPS_PAD, c), full(1, c), full(DFT_N, GRID_W), full(GRID_W, DFT_N),
                  full(GRID_W, TAPS_PAD), full(GRID_W, TAPS_PAD), full(GRID_W, TAPS_PAD)],
        out_specs=tile,
        compiler_params=_params("arbitrary", "arbitrary"),
        name="fconv",
    )(h, taps, bias, fwd, inv, c1, s3, c4)


def _tail1_kernel(x_ref, hcr_ref, hcc_ref, wz_ref, mod_ref, lng_ref, lnb_ref, wout_ref, ng_ref, nb_ref, o_ref,
                  col_ref):
    x = x_ref[...]
    perm = _chunk_transpose_perm()
    for q in range(GRID_W // S5_T):
        blk = hcc_ref[S5_T * q:S5_T * (q + 1), :, :].reshape(PERM_ROWS, CONV_C)
        t = jnp.dot(perm, blk, preferred_element_type=F32)
        for r in range(TILE_ROWS):
            col_ref[GRID_W * r + S5_T * q:GRID_W * r + S5_T * (q + 1), :] = t[S5_T * r:S5_T * (r + 1), :]
    h1 = (x * (1.0 + mod_ref[:, D:2 * D]) + mod_ref[:, 0:D]).astype(BF16)
    z = jnp.dot(h1, wz_ref[...], preferred_element_type=F32)
    hc = jnp.concatenate([hcr_ref[...].astype(F32), col_ref[...]], axis=1)
    m = _silu(_layer_norm(hc, lng_ref[...], lnb_ref[...])) * _silu(z)
    y = jnp.dot(m.astype(BF16), wout_ref[...], preferred_element_type=F32)
    gmod = mod_ref[:, 2 * D:3 * D]
    o_ref[...] = _layer_norm(DN_ALPHA * x + gmod * y, ng_ref[...], nb_ref[...])


def _tail1(x, hc_row, hc_col, w_z, mod, ln_g, ln_b, w_out, ng, nb, tm=TOKEN_TILE):
    tile = lambda w: pl.BlockSpec((None, tm, w), lambda b, i: (b, i, 0))
    full = lambda *s: pl.BlockSpec(s, lambda b, i: (0,) * len(s))
    return pl.pallas_call(
        _tail1_kernel,
        out_shape=jax.ShapeDtypeStruct((B, L, D), F32),
        grid=(B, L // tm),
        in_specs=[tile(D), tile(CONV_C),
                  pl.BlockSpec((None, GRID_W, TILE_ROWS, CONV_C), lambda b, i: (b, 0, i, 0)),
                  full(D, D), pl.BlockSpec((None, 1, 3 * D), lambda b, i: (b, 0, 0)),
                  full(1, D), full(1, D), full(D, D), full(1, D), full(1, D)],
        out_specs=tile(D),
        scratch_shapes=[pltpu.VMEM((tm, CONV_C), F32)],
        compiler_params=_params("arbitrary", "arbitrary"),
        name="tail1",
    )(x, hc_row, hc_col, w_z, mod, ln_g, ln_b, w_out, ng, nb)


def kernel(x, c, ctx, c_ctx, mod_w, mod_b, norm_g, norm_b, ev_w_in, ev_w_out, s5_lam_re, s5_lam_im, s5_log_dt, s5_b_re, s5_b_im, s5_c_re, s5_c_im, s5_d, glu_w, glu_b, sgu_ln_g, sgu_ln_b, sgu_w, sgu_b, od_w_in, od_w_out, dw_w, dw_b, conv_ln_g, conv_ln_b):
    TH = S5_T * S5_H
    row = lambda v: v.reshape(1, -1)

    cond8 = jnp.concatenate([c, c_ctx[None], jnp.zeros((3, D), F32)], axis=0)
    mods = _adaln(cond8, mod_w, mod_b)
    mod0 = mods[0, :B].reshape(B, 1, 3 * D)
    mod0c = mods[0, B:B + 1]
    mod1 = mods[1, :B].reshape(B, 1, 3 * D)

    lbr, lbi, cfr, cfi = _discretise(s5_lam_re[0], s5_lam_im[0], s5_log_dt[0])
    rowcat = lambda a: jnp.concatenate([a[0], a[1]], axis=-1).reshape(S5_G, 1, 2 * S5_P)
    bt = lambda a: jnp.concatenate([jnp.swapaxes(a[0], 1, 2), jnp.swapaxes(a[1], 1, 2)], axis=-1)
    cn = lambda a: jnp.concatenate([a[0], a[1]], axis=-1)
    d_row = jnp.tile(s5_d[0].reshape(S5_G, 1, S5_H), (1, 1, S5_T))
    win, wout, mix, l16 = _s5_weights(rowcat(lbr), rowcat(lbi), rowcat(cfr), rowcat(cfi),
                                      bt(s5_b_re[0]), bt(s5_b_im[0]), cn(s5_c_re[0]), cn(s5_c_im[0]), d_row)

    w_out0 = ev_w_out[0].astype(BF16)
    glu_w0 = glu_w[0].astype(BF16)
    guz, vln, hs = _inproj0n(x, mod0, ev_w_in[0], row(sgu_ln_g[0]), row(sgu_ln_b[0]))
    ua, sza, ua_c = _inproj0a(hs, _ctx_slabs(ctx, mod0c), ev_w_in[0])
    s_lat = _s5core(ua, ua_c, win, wout, mix, l16)
    y_s5 = _s5tail(s_lat, sza, glu_w0, row(glu_b[0]), w_out0)
    sguw = sgu_w[0].reshape(SGU_HEADS // 2, 2, SGU_CHUNK, SGU_CHUNK)
    sguw = jnp.transpose(sguw, (0, 2, 1, 3)).reshape(SGU_HEADS // 2, SGU_CHUNK, 2 * SGU_CHUNK).astype(BF16)
    sgub = jnp.repeat(sgu_b[0].T, SGU_HD, axis=1)
    x1 = _tail0(x, y_s5, guz, vln, mod0, sguw, sgub, w_out0, row(norm_g[0]), row(norm_b[0]))

    hg_row, hg_col = _inproj1(x1, mod1, od_w_in[0])
    consts = _dft_constants()
    taps = jnp.pad(dw_w[0], ((0, TAPS_PAD - CONV_K), (0, 0)))
    bias = row(dw_b[0])
    hc_row = _fconv(hg_row, taps[:, :CONV_C], bias[:, :CONV_C], consts)
    hc_col = _fconv(hg_col.reshape(B, L, CONV_C), taps[:, CONV_C:], bias[:, CONV_C:], consts)
    return _tail1(x1, hc_row, hc_col.reshape(B, GRID_W, GRID_W, CONV_C), od_w_in[0][:, 2 * D:].astype(BF16), mod1,
                  row(conv_ln_g[0]), row(conv_ln_b[0]), od_w_out[0].astype(BF16), row(norm_g[1]), row(norm_b[1]))
```

```python
import functools
import math

import jax
import jax.numpy as jnp
from jax import lax
from jax.experimental import pallas as pl
from jax.experimental.pallas import tpu as pltpu

D = 1024
B = 4
L = 4096
CTX = 256
GRID_W = 64
S5_W = 512
S5_G = 32
S5_H = 16
H_SHIFT = 4
BLK = 128 // S5_H
S5_P = 64
S5_T = 16
SGU_W = 512
SGU_HEADS = 8
SGU_HD = 64
SGU_CHUNK = 128
CONV_K = 31
CONV_HALF = CONV_K // 2
EVEN_IN = 2560
SGU_COL0 = 2 * S5_W
ODD_IN = 3072
DEPTH = 2
DN_ALPHA = (2 * DEPTH) ** 0.25
LN_EPS = 1e-5
N_CHUNK = L // S5_T
N_CCHUNK = CTX // S5_T
VMEM_LIMIT_V7X = 56 * 1024 * 1024
TOKEN_TILE = 1024

F32 = jnp.float32
BF16 = jnp.bfloat16


GELU_C = math.sqrt(2.0 / math.pi)


def _gelu(x):
    hx = 0.5 * x
    return hx * jnp.tanh(x * ((x * x) * (0.044715 * GELU_C) + GELU_C)) + hx


def _sigmoid(x):
    return 0.5 * jnp.tanh(0.5 * x) + 0.5


def _silu_of_half(hx):
    return hx * jnp.tanh(hx) + hx


def _silu(x):
    return _silu_of_half(0.5 * x)


def _layer_norm(x, g, b):
    mu = jnp.mean(x, axis=-1, keepdims=True)
    xc = x - mu
    var = jnp.mean(xc * xc, axis=-1, keepdims=True)
    return xc * lax.rsqrt(var + LN_EPS) * g + b


def _params(*sem):
    return pltpu.CompilerParams(dimension_semantics=sem, vmem_limit_bytes=VMEM_LIMIT_V7X)


def _adaln_kernel(c_ref, w_ref, b_ref, o_ref):
    def split(v):
        hi = v.astype(BF16)
        return hi, (v - hi.astype(F32)).astype(BF16)

    a_hi, a_lo = split(_silu(c_ref[...]))
    w_hi, w_lo = split(w_ref[...])
    dot = functools.partial(jnp.dot, preferred_element_type=F32)
    o_ref[...] = dot(a_hi, w_hi) + dot(a_lo, w_hi) + dot(a_hi, w_lo) + b_ref[...]


def _adaln(cond8, mod_w, mod_b):
    tn = 512
    return pl.pallas_call(
        _adaln_kernel,
        out_shape=jax.ShapeDtypeStruct((DEPTH, 8, 3 * D), F32),
        grid=(DEPTH, 3 * D // tn),
        in_specs=[pl.BlockSpec((8, D), lambda l, j: (0, 0)),
                  pl.BlockSpec((None, D, tn), lambda l, j: (l, 0, j)),
                  pl.BlockSpec((None, 1, tn), lambda l, j: (l, 0, j))],
        out_specs=pl.BlockSpec((None, 8, tn), lambda l, j: (l, 0, j)),
        compiler_params=_params("arbitrary", "arbitrary"),
        name="adaln",
    )(cond8, mod_w, mod_b.reshape(DEPTH, 1, 3 * D))


def _disc_kernel(lr_ref, li_ref, ldt_ref, obr_ref, obi_ref, ocr_ref, oci_ref):
    lr = lr_ref[...]
    li = li_ref[...]
    dt = jnp.exp(ldt_ref[...])
    mag = jnp.exp(lr * dt)
    br = mag * jnp.cos(li * dt)
    bi = mag * jnp.sin(li * dt)
    inv = 1.0 / (lr * lr + li * li)
    nr = br - 1.0
    obr_ref[...] = br
    obi_ref[...] = bi
    ocr_ref[...] = (nr * lr + bi * li) * inv
    oci_ref[...] = (bi * lr - nr * li) * inv


def _discretise(lam_re, lam_im, log_dt):
    shp = jax.ShapeDtypeStruct((2 * S5_G, S5_P), F32)
    ldt = jnp.broadcast_to(log_dt.reshape(2 * S5_G, 1), (2 * S5_G, S5_P))
    outs = pl.pallas_call(
        _disc_kernel, out_shape=(shp, shp, shp, shp), name="s5_discretise",
    )(lam_re.reshape(2 * S5_G, S5_P), lam_im.reshape(2 * S5_G, S5_P), ldt)
    return [o.reshape(2, S5_G, S5_P) for o in outs]


S5W_GROUPS = 4


def _cpow(base_pows, j):
    re = None
    im = None
    for k, (pr, pi) in enumerate(base_pows):
        bit = ((j >> k) & 1) == 1
        mr = jnp.where(bit, pr, 1.0)
        mi = jnp.where(bit, pi, 0.0)
        if re is None:
            re, im = mr, mi
        else:
            re, im = re * mr - im * mi, re * mi + im * mr
    return re, im


def _squarings(pr, pi, n):
    out = [(pr, pi)]
    for _ in range(n - 1):
        pr, pi = pr * pr - pi * pi, 2.0 * pr * pi
        out.append((pr, pi))
    return out


def _shift_lanes(x, n):
    lane = lax.broadcasted_iota(jnp.int32, (S5_H, 128), 1)
    lo, hi = x[:, :128], x[:, 128:]
    if n == 0:
        return x
    if n < 128:
        rlo = pltpu.roll(lo, n, axis=1)
        rhi = pltpu.roll(hi, n, axis=1)
        return jnp.concatenate([jnp.where(lane >= n, rlo, 0.0), jnp.where(lane >= n, rhi, rlo)], axis=1)
    m = n - 128
    rlo = lo if m == 0 else pltpu.roll(lo, m, axis=1)
    return jnp.concatenate([jnp.zeros_like(lo), jnp.where(lane >= m, rlo, 0.0)], axis=1)


def _unshift_lanes(x, n):
    lane = lax.broadcasted_iota(jnp.int32, (S5_H, 128), 1)
    lo, hi = x[:, :128], x[:, 128:]
    if n == 0:
        return x
    if n < 128:
        rlo = pltpu.roll(lo, 128 - n, axis=1)
        rhi = pltpu.roll(hi, 128 - n, axis=1)
        keep = lane < 128 - n
        return jnp.concatenate([jnp.where(keep, rlo, rhi), jnp.where(keep, rhi, 0.0)], axis=1)
    m = n - 128
    rhi = hi if m == 0 else pltpu.roll(hi, 128 - m, axis=1)
    return jnp.concatenate([jnp.where(lane < 128 - m, rhi, 0.0), jnp.zeros_like(lo)], axis=1)


def _s5w_group(gi, bg, lrow_re, lrow_im, crow_re, crow_im, bt_re, bt_im,
               cn_re, cn_im, d_ref, win_ref, wout_ref, mix_ref, l16_ref):
    TH = S5_T * S5_H

    def chunk_pos(idx):
        return (((idx >> H_SHIFT) - bg) & (BLK - 1)) + ((idx >> 7) << 3)

    lr = lrow_re[gi]
    li = lrow_im[gi]
    pows_row = _squarings(lr, li, 5)
    l16_ref[gi, 0:1, :] = pows_row[4][0]
    l16_ref[gi, 1:2, :] = pows_row[4][1]
    l16_ref[gi, 2:8, :] = jnp.zeros((6, 128), F32)
    cr = crow_re[gi]
    ci = crow_im[gi]
    btr = bt_re[gi]
    bti = bt_im[gi]
    bbr = cr * btr - ci * bti
    bbi = cr * bti + ci * btr
    blk16 = lax.broadcasted_iota(jnp.int32, (S5_T, 128), 0)
    is_f16 = lax.broadcasted_iota(jnp.int32, (S5_T, 128), 1) < S5_P
    pos16 = chunk_pos(blk16 << H_SHIFT)
    pr16, pi16 = _cpow(pows_row[:4], jnp.where(is_f16, S5_T - 1 - pos16, pos16))
    rep_rows = lambda v: jnp.broadcast_to(v[:, None, :], (S5_T, S5_H, 128)).reshape(TH, 128)
    pr, pi = rep_rows(pr16), rep_rows(pi16)
    tbr = jnp.broadcast_to(bbr[None], (S5_T, S5_H, 128)).reshape(TH, 128)
    tbi = jnp.broadcast_to(bbi[None], (S5_T, S5_H, 128)).reshape(TH, 128)
    win_ref[gi, :, 0:128] = (pr * tbr - pi * tbi).astype(BF16)
    win_ref[gi, :, 128:256] = (pr * tbi + pi * tbr).astype(BF16)

    hp = lax.Precision.HIGHEST
    dot = functools.partial(jnp.dot, preferred_element_type=F32, precision=hp)
    def col256(r):
        col = jnp.broadcast_to(r, (2 * S5_P, 2 * S5_P)).T
        return jnp.concatenate([col, col], axis=1)

    def tiled_t(cn):
        t8 = jnp.broadcast_to(cn[None], (BLK, S5_H, 2 * S5_P)).reshape(2 * S5_P, 2 * S5_P).T
        return jnp.concatenate([t8, t8], axis=1)

    cpows = _squarings(col256(lr), col256(li), 4)
    row = lax.broadcasted_iota(jnp.int32, (2 * S5_P, TH), 0)
    lane_w = lax.broadcasted_iota(jnp.int32, (2 * S5_P, TH), 1)
    t_idx = chunk_pos(lane_w)
    j_idx = lane_w >> H_SHIFT
    is_f = row < S5_P
    ctr = tiled_t(cn_re[gi])
    cti = tiled_t(cn_im[gi])
    er, ei = _cpow(cpows, jnp.where(is_f, t_idx, S5_T - 1 - t_idx))
    er, ei = er * cpows[0][0] - ei * cpows[0][1], er * cpows[0][1] + ei * cpows[0][0]
    wr = ctr * er - cti * ei
    wi = ctr * ei + cti * er
    wout_ref[gi, 0:128, :] = wr.astype(BF16)
    wout_ref[gi, 128:256, :] = (-wi).astype(BF16)
    kr, ki = _cpow(cpows, jnp.where(is_f, j_idx, S5_T - 1 - j_idx))
    ekr = ctr * kr - cti * ki
    eki = ctr * ki + cti * kr
    lane16 = lax.broadcasted_iota(jnp.int32, (S5_H, 128), 1)
    mf = lane16 < S5_P
    kkf = dot(jnp.where(mf, bbr, 0.0), ekr) - dot(jnp.where(mf, bbi, 0.0), eki)
    kkb = dot(jnp.where(mf, 0.0, bbr), ekr) - dot(jnp.where(mf, 0.0, bbi), eki)
    dl = d_ref[gi]
    r16 = lax.broadcasted_iota(jnp.int32, (S5_H, TH), 0)
    l256 = lax.broadcasted_iota(jnp.int32, (S5_H, TH), 1)
    rot = bg * S5_H
    for s in range(S5_T):
        blk = _shift_lanes(kkf, S5_H * s) + _unshift_lanes(kkb, S5_H * (S5_T - 1 - s))
        blk = blk + jnp.where(l256 == r16 + S5_H * s, dl, 0.0)
        blk = jnp.concatenate([pltpu.roll(blk[:, :128], rot, axis=1), pltpu.roll(blk[:, 128:], rot, axis=1)], axis=1)
        rho = ((s + bg) & (BLK - 1)) + (s & BLK)
        mix_ref[gi, pl.ds(pl.multiple_of(rho * S5_H, S5_H), S5_H), :] = blk.astype(BF16)


def _s5w_kernel(*refs):
    for gi in range(S5W_GROUPS):
        bg = (pl.program_id(0) * S5W_GROUPS + gi) & (BLK - 1)
        _s5w_group(gi, bg, *refs)


def _s5_weights(lrow_re, lrow_im, crow_re, crow_im, bt_re, bt_im, cn_re, cn_im, d_row):
    TH = S5_T * S5_H
    g3 = lambda r, c: pl.BlockSpec((S5W_GROUPS, r, c), lambda g: (g, 0, 0))
    wshape = jax.ShapeDtypeStruct((S5_G, TH, TH), BF16)
    return pl.pallas_call(
        _s5w_kernel,
        out_shape=(wshape, wshape, wshape, jax.ShapeDtypeStruct((S5_G, 8, 128), F32)),
        grid=(S5_G // S5W_GROUPS,),
        in_specs=[g3(1, 128)] * 4 + [g3(S5_H, 128)] * 4 + [g3(1, TH)],
        out_specs=(g3(TH, TH), g3(TH, TH), g3(TH, TH), g3(8, 128)),
        compiler_params=_params("arbitrary"),
        name="s5_weights",
    )(lrow_re, lrow_im, crow_re, crow_im, bt_re, bt_im, cn_re, cn_im, d_row)


def _rot_blocks(v, r):
    cols = [pltpu.roll(v[:, 128 * q:128 * (q + 1)], S5_H * r, axis=1) for q in range(v.shape[1] // 128)]
    return jnp.concatenate(cols, axis=1)


def _slabs_of(h, hs_ref):
    h3 = h.reshape(h.shape[0] // S5_T, S5_T, h.shape[1])
    for s in range(S5_T):
        hs_ref[s] = h3[:, s, :].astype(BF16)


PERM_ROWS = S5_T * S5_T


def _chunk_transpose_perm():
    ri = lax.broadcasted_iota(jnp.int32, (PERM_ROWS, PERM_ROWS), 0)
    ci = lax.broadcasted_iota(jnp.int32, (PERM_ROWS, PERM_ROWS), 1)
    hit = ((ri >> H_SHIFT) == (ci & (S5_T - 1))) & ((ri & (S5_T - 1)) == (ci >> H_SHIFT))
    return jnp.where(hit, 1.0, 0.0).astype(BF16)


def _inproj0n_kernel(x_ref, mod_ref, w_ref, lng_ref, lnb_ref, guz_ref, vln_ref, hs_ref):
    shift = mod_ref[:, 0:D]
    scale = mod_ref[:, D:2 * D]
    hb = (x_ref[...] * (1.0 + scale) + shift).astype(BF16)
    perm = _chunk_transpose_perm()
    for j in range(hb.shape[0] // PERM_ROWS):
        blk = jnp.dot(perm, hb[PERM_ROWS * j:PERM_ROWS * (j + 1), :], preferred_element_type=F32).astype(BF16)
        for s in range(S5_T):
            hs_ref[s, S5_T * j:S5_T * (j + 1), :] = blk[S5_T * s:S5_T * (s + 1), :]
    dot = lambda lo: jnp.dot(hb, w_ref[:, SGU_COL0 + lo:SGU_COL0 + lo + 512].astype(BF16),
                             preferred_element_type=F32)
    guz_ref[...] = (_gelu(dot(0)) * _silu(dot(1024))).astype(BF16)
    vln_ref[...] = _layer_norm(_gelu(dot(512)), lng_ref[...], lnb_ref[...]).astype(BF16)


def _inproj0n(x, mod, w_in_f32, ln_g, ln_b, tm=TOKEN_TILE):
    nct = tm // S5_T
    o = jax.ShapeDtypeStruct((B, L, 512), BF16)
    ospec = pl.BlockSpec((None, tm, 512), lambda b, i: (b, i, 0))
    full = lambda *s: pl.BlockSpec(s, lambda b, i: (0,) * len(s))
    return pl.pallas_call(
        _inproj0n_kernel,
        out_shape=(o, o, jax.ShapeDtypeStruct((S5_T, B * N_CHUNK, D), BF16)),
        grid=(B, L // tm),
        in_specs=[pl.BlockSpec((None, tm, D), lambda b, i: (b, i, 0)),
                  pl.BlockSpec((None, 1, 3 * D), lambda b, i: (b, 0, 0)),
                  pl.BlockSpec((D, EVEN_IN), lambda b, i: (0, 0), pipeline_mode=pl.Buffered(1)),
                  full(1, 512), full(1, 512)],
        out_specs=(ospec, ospec,
                   pl.BlockSpec((S5_T, nct, D), lambda b, i: (0, b * (N_CHUNK // nct) + i, 0))),
        compiler_params=_params("arbitrary", "arbitrary"),
        name="inproj0n",
    )(x, mod, w_in_f32, ln_g, ln_b)


def _ctx_slabs_kernel(x_ref, mod_ref, hs_ref):
    h = x_ref[...] * (1.0 + mod_ref[:, D:2 * D]) + mod_ref[:, 0:D]
    _slabs_of(h, hs_ref)


def _ctx_slabs(ctx, mod_c):
    return pl.pallas_call(
        _ctx_slabs_kernel,
        out_shape=jax.ShapeDtypeStruct((S5_T, B * N_CCHUNK, D), BF16),
        grid=(B,),
        in_specs=[pl.BlockSpec((None, CTX, D), lambda b: (b, 0, 0)),
                  pl.BlockSpec((1, 3 * D), lambda b: (0, 0))],
        out_specs=pl.BlockSpec((S5_T, N_CCHUNK, D), lambda b: (0, b, 0)),
        compiler_params=_params("arbitrary"),
        name="ctx_slabs",
    )(ctx, mod_c)


def _inproj0a_kernel(hs_ref, hc_ref, w_ref, ua_ref, sza_ref, uc_ref):
    r = pl.program_id(0)
    h = hs_ref[...]
    w_ua = w_ref[:, 0:512].astype(BF16)
    ua_ref[...] = _rot_blocks(jnp.dot(h, w_ua, preferred_element_type=F32), r).astype(BF16)
    sza_ref[...] = _silu(jnp.dot(h, w_ref[:, 512:1024].astype(BF16), preferred_element_type=F32)).astype(BF16)
    uc_ref[...] = _rot_blocks(jnp.dot(hc_ref[...], w_ua, preferred_element_type=F32), r).astype(BF16)


def _inproj0a(hs, hcs, w_in_f32):
    slab = lambda r, h: r + BLK * h
    sspec = lambda n, w: pl.BlockSpec((None, n, w), lambda r, h: (slab(r, h), 0, 0))
    so = lambda n: jax.ShapeDtypeStruct((S5_T, n, 512), BF16)
    nl, ncx = B * N_CHUNK, B * N_CCHUNK
    return pl.pallas_call(
        _inproj0a_kernel,
        out_shape=(so(nl), so(nl), so(ncx)),
        grid=(BLK, S5_T // BLK),
        in_specs=[sspec(nl, D), sspec(ncx, D), pl.BlockSpec((D, SGU_COL0), lambda r, h: (0, 0))],
        out_specs=(sspec(nl, 512), sspec(nl, 512), sspec(ncx, 512)),
        compiler_params=_params("arbitrary", "arbitrary"),
        name="inproj0a",
    )(hs, hcs, w_in_f32)


SCAN_GROUPS = 4


def _scan_tiles(sre_ref, sim_ref, h_refs, n_tiles, carry, lams):
    row = lax.broadcasted_iota(jnp.int32, (8, 128), 0)
    lane = lax.broadcasted_iota(jnp.int32, (8, 128), 1)
    first = row < B
    fwd = lane < S5_P

    def body(k, c):
        of = pl.multiple_of(k * 8, 8)
        ob = pl.multiple_of((n_tiles - 1 - k) * 8, 8)
        out = []
        for gi in range(SCAN_GROUPS):
            lre, lim = lams[gi]
            hr, hi = c[2 * gi], c[2 * gi + 1]
            sr = jnp.where(fwd, sre_ref[gi, pl.ds(of, 8), :], pltpu.roll(sre_ref[gi, pl.ds(ob, 8), :], B, axis=0))
            si = jnp.where(fwd, sim_ref[gi, pl.ds(of, 8), :], pltpu.roll(sim_ref[gi, pl.ds(ob, 8), :], B, axis=0))
            h1r = lre * hr - lim * hi + sr
            h1i = lre * hi + lim * hr + si
            r1r = pltpu.roll(h1r, B, axis=0)
            r1i = pltpu.roll(h1i, B, axis=0)
            if h_refs is not None:
                fre_ref, fim_ref, bre_ref, bim_ref = h_refs
                er = jnp.where(first, hr, r1r)
                ei = jnp.where(first, hi, r1i)
                fre_ref[gi, pl.ds(of, 8), :] = er
                fim_ref[gi, pl.ds(of, 8), :] = ei
                bre_ref[gi, pl.ds(ob, 8), :] = pltpu.roll(er, B, axis=0)
                bim_ref[gi, pl.ds(ob, 8), :] = pltpu.roll(ei, B, axis=0)
            h2r = lre * r1r - lim * r1i + sr
            h2i = lre * r1i + lim * r1r + si
            out.append(jnp.where(first, pltpu.roll(h2r, B, axis=0), h2r))
            out.append(jnp.where(first, pltpu.roll(h2i, B, axis=0), h2i))
        return tuple(out)

    return lax.fori_loop(0, n_tiles, body, carry)


def _gather_group(slab_ref, src):
    halves = []
    for h in range(S5_T // BLK):
        acc = slab_ref[BLK * h]
        for s in range(1, BLK):
            acc = jnp.where(src == s, slab_ref[BLK * h + s], acc)
        halves.append(acc)
    return jnp.concatenate(halves, axis=1)


def _s5core_kernel(ul_ref, uc_ref, win_ref, wout_ref, mix_ref, l16_ref, o_ref,
                   u_ref, sre_ref, sim_ref, cre_ref, cim_ref, fre_ref, fim_ref, bre_ref, bim_ref, y_ref):
    nl = N_CHUNK * B
    ncx = N_CCHUNK * B
    blk_l = lax.broadcasted_iota(jnp.int32, (nl, 128), 1) >> H_SHIFT
    blk_c = lax.broadcasted_iota(jnp.int32, (ncx, 128), 1) >> H_SHIFT
    fwd = lax.broadcasted_iota(jnp.int32, (N_CHUNK, 128), 1) < S5_P
    for g0 in range(0, BLK, SCAN_GROUPS):
        for gi in range(SCAN_GROUPS):
            bg = g0 + gi
            win = win_ref[bg]
            src_l = ((blk_l - bg) & (BLK - 1)).astype(F32).astype(BF16)
            src_c = ((blk_c - bg) & (BLK - 1)).astype(F32).astype(BF16)
            u = _gather_group(ul_ref, src_l)
            u_ref[gi] = u
            sl = jnp.dot(u, win, preferred_element_type=F32)
            sc = jnp.dot(_gather_group(uc_ref, src_c), win, preferred_element_type=F32)
            for b in range(B):
                sre_ref[gi, pl.ds(b, N_CHUNK, stride=B), :] = sl[N_CHUNK * b:N_CHUNK * (b + 1), 0:128]
                sim_ref[gi, pl.ds(b, N_CHUNK, stride=B), :] = sl[N_CHUNK * b:N_CHUNK * (b + 1), 128:256]
                cre_ref[gi, pl.ds(b, N_CCHUNK, stride=B), :] = sc[N_CCHUNK * b:N_CCHUNK * (b + 1), 0:128]
                cim_ref[gi, pl.ds(b, N_CCHUNK, stride=B), :] = sc[N_CCHUNK * b:N_CCHUNK * (b + 1), 128:256]
        lams = [(jnp.broadcast_to(l16_ref[g0 + gi, 0:1, :], (8, 128)),
                 jnp.broadcast_to(l16_ref[g0 + gi, 1:2, :], (8, 128))) for gi in range(SCAN_GROUPS)]
        zero = tuple(jnp.zeros((8, 128), F32) for _ in range(2 * SCAN_GROUPS))
        carry = _scan_tiles(cre_ref, cim_ref, None, ncx // 8, zero, lams)
        _scan_tiles(sre_ref, sim_ref, (fre_ref, fim_ref, bre_ref, bim_ref), nl // 8, carry, lams)
        for gi in range(SCAN_GROUPS):
            bg = g0 + gi
            y = jnp.dot(u_ref[gi], mix_ref[bg], preferred_element_type=F32)
            hs = []
            for b in range(B):
                rows = pl.ds(b, N_CHUNK, stride=B)
                hs.append(jnp.concatenate([jnp.where(fwd, fre_ref[gi, rows, :], bre_ref[gi, rows, :]),
                                           jnp.where(fwd, fim_ref[gi, rows, :], bim_ref[gi, rows, :])], axis=1))
            hcat = jnp.concatenate(hs, axis=0).astype(BF16)
            y = y + jnp.dot(hcat, wout_ref[bg], preferred_element_type=F32)
            y_ref[bg] = y.astype(BF16)

    blk = blk_l.astype(F32).astype(BF16)
    for s in range(S5_T):
        h, r = s // BLK, s % BLK
        acc = None
        for j in range(BLK):
            piece = y_ref[(j - r) % BLK, :, 128 * h:128 * (h + 1)]
            acc = piece if acc is None else jnp.where(blk == j, piece, acc)
        o_ref[s] = acc


def _s5core(ul, uc, win, wout, mix, l16):
    TH = S5_T * S5_H
    nl = N_CHUNK * B
    ncx = N_CCHUNK * B
    g4 = lambda r, c: pl.BlockSpec((BLK, r, c), lambda q: (q, 0, 0))
    col = lambda n: pl.BlockSpec((S5_T, n, 128), lambda q: (0, 0, q))
    f32s = lambda n: pltpu.VMEM((SCAN_GROUPS, n, 128), F32)
    return pl.pallas_call(
        _s5core_kernel,
        out_shape=jax.ShapeDtypeStruct((S5_T, nl, S5_W), BF16),
        grid=(S5_G // BLK,),
        in_specs=[col(nl), col(ncx), g4(TH, TH), g4(TH, TH), g4(TH, TH), g4(8, 128)],
        out_specs=col(nl),
        scratch_shapes=[pltpu.VMEM((SCAN_GROUPS, nl, TH), BF16),
                        f32s(nl), f32s(nl), f32s(ncx), f32s(ncx), f32s(nl), f32s(nl), f32s(nl), f32s(nl),
                        pltpu.VMEM((BLK, nl, TH), BF16)],
        compiler_params=_params("arbitrary"),
        name="s5core",
    )(ul, uc, win, wout, mix, l16)


def _s5tail_kernel(slat_ref, sza_ref, gluw_ref, glub_ref, wtop_ref, y_ref):
    unrot = (BLK - pl.program_id(0)) & (BLK - 1)
    for b in range(B):
        rows = slice(N_CHUNK * b, N_CHUNK * (b + 1))
        g = _gelu(_rot_blocks(slat_ref[rows, :].astype(F32), unrot))
        gate = _sigmoid(jnp.dot(g.astype(BF16), gluw_ref[...], preferred_element_type=F32) + glub_ref[...])
        a = (g * gate * sza_ref[rows, :].astype(F32)).astype(BF16)
        y_ref[rows, :] = jnp.dot(a, wtop_ref[...], preferred_element_type=F32).astype(BF16)


def _s5tail(slat, sza, glu_w, glu_b, w_top):
    slab = lambda r, h: r + BLK * h
    sspec = lambda w: pl.BlockSpec((None, N_CHUNK * B, w), lambda r, h: (slab(r, h), 0, 0))
    full = lambda *s: pl.BlockSpec(s, lambda r, h: (0,) * len(s))
    return pl.pallas_call(
        _s5tail_kernel,
        out_shape=jax.ShapeDtypeStruct((S5_T, N_CHUNK * B, D), BF16),
        grid=(BLK, S5_T // BLK),
        in_specs=[sspec(512), sspec(512), full(512, 512), full(1, 512), full(S5_W, D)],
        out_specs=sspec(D),
        compiler_params=_params("arbitrary", "arbitrary"),
        name="s5tail",
    )(slat, sza, glu_w, glu_b, w_top)


def _tail0_kernel(x_ref, ys5_ref, guz_ref, vln_ref, mod_ref, sguw_ref, sgub_ref, wbot_ref, ng_ref, nb_ref, o_ref):
    tm = x_ref.shape[0]
    lane = lax.broadcasted_iota(jnp.int32, (SGU_CHUNK, 128), 1)
    lo = lane < SGU_HD
    zero = jnp.zeros((SGU_CHUNK, 128), BF16)
    chunks = []
    for ci in range(tm // SGU_CHUNK):
        v = vln_ref[ci * SGU_CHUNK:(ci + 1) * SGU_CHUNK, :]
        cols = []
        for pi in range(SGU_HEADS // 2):
            vp = v[:, 128 * pi:128 * (pi + 1)]
            bm = jnp.concatenate([jnp.where(lo, vp, zero), jnp.where(lo, zero, vp)], axis=0)
            cols.append(jnp.dot(sguw_ref[pi], bm, preferred_element_type=F32))
        chunks.append(jnp.concatenate(cols, axis=1) + sgub_ref[...])
    s = jnp.concatenate(chunks, axis=0)
    bsg = (guz_ref[...].astype(F32) * s).astype(BF16)
    perm = _chunk_transpose_perm()
    ys5 = jnp.concatenate(
        [jnp.dot(perm, ys5_ref[:, S5_T * j:S5_T * (j + 1), :].reshape(PERM_ROWS, D), preferred_element_type=F32)
         for j in range(tm // PERM_ROWS)], axis=0)
    y = ys5 + jnp.dot(bsg, wbot_ref[...], preferred_element_type=F32)
    gmod = mod_ref[:, 2 * D:3 * D]
    o_ref[...] = _layer_norm(DN_ALPHA * x_ref[...] + gmod * y, ng_ref[...], nb_ref[...])


def _tail0(x, ys5, guz, vln, mod, sguw, sgub, w_bot, ng, nb, tm=TOKEN_TILE):
    nct = tm // S5_T
    t512 = pl.BlockSpec((None, tm, 512), lambda b, i: (b, i, 0))
    tD = pl.BlockSpec((None, tm, D), lambda b, i: (b, i, 0))
    full = lambda *s: pl.BlockSpec(s, lambda b, i: (0,) * len(s))
    return pl.pallas_call(
        _tail0_kernel,
        out_shape=jax.ShapeDtypeStruct((B, L, D), F32),
        grid=(B, L // tm),
        in_specs=[tD, pl.BlockSpec((S5_T, nct, D), lambda b, i: (0, b * (N_CHUNK // nct) + i, 0)), t512, t512,
                  pl.BlockSpec((None, 1, 3 * D), lambda b, i: (b, 0, 0)),
                  full(SGU_HEADS // 2, SGU_CHUNK, 256), full(SGU_CHUNK, 512),
                  pl.BlockSpec((SGU_W, D), lambda b, i: (1, 0)), full(1, D), full(1, D)],
        out_specs=tD,
        compiler_params=_params("arbitrary", "arbitrary"),
        name="tail0",
    )(x, ys5, guz, vln, mod, sguw, sgub, w_bot, ng, nb)


CONV_C = D // 2
TILE_ROWS = TOKEN_TILE // GRID_W


def _grid_transpose_in(v, o_ref):
    perm = _chunk_transpose_perm()
    for q in range(GRID_W // S5_T):
        seg = jnp.concatenate([v[GRID_W * r + S5_T * q:GRID_W * r + S5_T * (q + 1), :] for r in range(TILE_ROWS)],
                              axis=0)
        t = jnp.dot(perm, seg, preferred_element_type=F32).astype(BF16)
        o_ref[S5_T * q:S5_T * (q + 1), :, :] = t.reshape(S5_T, TILE_ROWS, v.shape[1])


def _inproj1_kernel(x_ref, mod_ref, w_ref, hgr_ref, hgc_ref, h_ref):
    shift = mod_ref[:, 0:D]
    scale = mod_ref[:, D:2 * D]
    h = (x_ref[...] * (1.0 + scale) + shift).astype(BF16)
    h_ref[...] = h
    dot = lambda lo: jnp.dot(h, w_ref[:, lo:lo + CONV_C].astype(BF16), preferred_element_type=F32)
    hgr_ref[...] = (dot(0) * _sigmoid(dot(D))).astype(BF16)
    _grid_transpose_in((dot(CONV_C) * _sigmoid(dot(D + CONV_C))).astype(BF16), hgc_ref)


def _inproj1(x, mod, w_in_f32, tm=TOKEN_TILE):
    tile = lambda w: pl.BlockSpec((None, tm, w), lambda b, i: (b, i, 0))
    return pl.pallas_call(
        _inproj1_kernel,
        out_shape=(jax.ShapeDtypeStruct((B, L, CONV_C), BF16),
                   jax.ShapeDtypeStruct((B, GRID_W, GRID_W, CONV_C), BF16),
                   jax.ShapeDtypeStruct((B, L, D), BF16)),
        grid=(B, L // tm),
        in_specs=[tile(D),
                  pl.BlockSpec((None, 1, 3 * D), lambda b, i: (b, 0, 0)),
                  pl.BlockSpec((D, 2 * D), lambda b, i: (0, 0), pipeline_mode=pl.Buffered(1))],
        out_specs=(tile(CONV_C), pl.BlockSpec((None, GRID_W, TILE_ROWS, CONV_C), lambda b, i: (b, 0, i, 0)),
                   tile(D)),
        compiler_params=_params("arbitrary", "arbitrary"),
        name="inproj1",
    )(x, mod, w_in_f32)


DFT_N = 2 * GRID_W
TAPS_PAD = CONV_K + 1


def _dft_constants():
    th = 2.0 * math.pi / DFT_N
    f = jnp.arange(GRID_W, dtype=F32)[:, None]
    p = jnp.arange(GRID_W, dtype=F32)[None, :]
    cosm = jnp.cos(th * f * p)
    sinm = jnp.sin(th * f * p)
    alt = jnp.where(jnp.arange(GRID_W) % 2 == 0, 1.0, -1.0).astype(F32)
    fwd = jnp.concatenate([cosm, alt[None, :], sinm[1:]], axis=0)
    cf = jnp.where(jnp.arange(GRID_W) == 0, 1.0, 2.0).astype(F32) / DFT_N
    inv = jnp.concatenate([cosm.T * cf[None, :], (alt / DFT_N)[:, None], sinm.T[:, 1:] * (2.0 / DFT_N)], axis=1)
    sft = (CONV_HALF - jnp.arange(TAPS_PAD, dtype=F32))[None, :]
    live = (jnp.arange(TAPS_PAD) < CONV_K).astype(F32)[None, :]
    f64 = jnp.where(f == 0, float(GRID_W), f)
    return (fwd.astype(BF16), inv.astype(BF16),
            jnp.cos(th * f * sft) * live, jnp.sin(th * f * sft) * live, jnp.cos(th * f64 * sft) * live)


def _fconv_kernel(h_ref, w_ref, b_ref, fwd_ref, inv_ref, c1_ref, s3_ref, c4_ref, o_ref):
    hp = lax.Precision.HIGHEST
    taps = w_ref[...]
    g_re = jnp.dot(c1_ref[...], taps, preferred_element_type=F32, precision=hp)
    g_im = jnp.dot(s3_ref[...], taps, preferred_element_type=F32, precision=hp)
    g_r2 = jnp.dot(c4_ref[...], taps, preferred_element_type=F32, precision=hp)
    fwd = fwd_ref[...]
    inv = inv_ref[...]
    bias = b_ref[...]
    n_runs = h_ref.shape[0] // GRID_W
    rows = lambda r: slice(GRID_W * r, GRID_W * (r + 1))
    forward = lambda r: jnp.dot(fwd, h_ref[rows(r), :], preferred_element_type=F32)
    ahead = 2
    specs = [forward(r) for r in range(ahead)]
    for r in range(n_runs):
        if r + ahead < n_runs:
            specs.append(forward(r + ahead))
        spec = specs[r]
        a, bm = spec[0:GRID_W], spec[GRID_W:DFT_N]
        prod = jnp.concatenate([a * g_re - bm * g_im, a * g_im + bm * g_r2], axis=0).astype(BF16)
        o_ref[rows(r), :] = (jnp.dot(inv, prod, preferred_element_type=F32) + bias).astype(BF16)


def _fconv(h, taps, bias, consts, tm=TOKEN_TILE):
    fwd, inv, c1, s3, c4 = consts
    c = h.shape[-1]
    tile = pl.BlockSpec((None, tm, c), lambda b, i: (b, i, 0))
    full = lambda *s: pl.BlockSpec(s, lambda b, i: (0,) * len(s))
    return pl.pallas_call(
        _fconv_kernel,
        out_shape=jax.ShapeDtypeStruct(h.shape, BF16),
        grid=(B, L // tm),
        in_specs=[tile, full(TAPS_PAD, c), full(1, c), full(DFT_N, GRID_W), full(GRID_W, DFT_N),
                  full(GRID_W, TAPS_PAD), full(GRID_W, TAPS_PAD), full(GRID_W, TAPS_PAD)],
        out_specs=tile,
        compiler_params=_params("arbitrary", "arbitrary"),
        name="fconv",
    )(h, taps, bias, fwd, inv, c1, s3, c4)


ROW_BLOCK = 32


def _row_blocks(n_rows):
    return [slice(ROW_BLOCK * k, ROW_BLOCK * (k + 1)) for k in range(n_rows // ROW_BLOCK)]


def _tail1_kernel(x_ref, hcr_ref, hcc_ref, h1_ref, wz_ref, mod_ref, lng_ref, lnb_ref, wout_ref, ng_ref, nb_ref,
                  o_ref, col_ref, z_ref, m_ref, y_ref):
    tm = x_ref.shape[0]
    perm = _chunk_transpose_perm()
    for q in range(GRID_W // S5_T):
        blk = hcc_ref[S5_T * q:S5_T * (q + 1), :, :].reshape(PERM_ROWS, CONV_C)
        t = jnp.dot(perm, blk, preferred_element_type=F32)
        for r in range(TILE_ROWS):
            col_ref[GRID_W * r + S5_T * q:GRID_W * r + S5_T * (q + 1), :] = t[S5_T * r:S5_T * (r + 1), :]
    z_ref[...] = jnp.dot(h1_ref[...], wz_ref[...], preferred_element_type=F32)
    lng, lnb = lng_ref[...], lnb_ref[...]
    for rows in _row_blocks(tm):
        hc = jnp.concatenate([hcr_ref[rows, :].astype(F32), col_ref[rows, :]], axis=1)
        m_ref[rows, :] = (_silu_of_half(_layer_norm(hc, lng, lnb)) * _silu_of_half(z_ref[rows, :])).astype(BF16)
    y_ref[...] = jnp.dot(m_ref[...], wout_ref[...], preferred_element_type=F32)
    gmod = mod_ref[:, 2 * D:3 * D]
    ng, nb = ng_ref[...], nb_ref[...]
    for rows in _row_blocks(tm):
        o_ref[rows, :] = _layer_norm(DN_ALPHA * x_ref[rows, :] + gmod * y_ref[rows, :], ng, nb)


def _tail1(x, hc_row, hc_col, h1, w_z_half, mod, ln_g_half, ln_b_half, w_out, ng, nb, tm=TOKEN_TILE):
    tile = lambda w: pl.BlockSpec((None, tm, w), lambda b, i: (b, i, 0))
    full = lambda *s: pl.BlockSpec(s, lambda b, i: (0,) * len(s))
    return pl.pallas_call(
        _tail1_kernel,
        out_shape=jax.ShapeDtypeStruct((B, L, D), F32),
        grid=(B, L // tm),
        in_specs=[tile(D), tile(CONV_C),
                  pl.BlockSpec((None, GRID_W, TILE_ROWS, CONV_C), lambda b, i: (b, 0, i, 0)),
                  tile(D), full(D, D), pl.BlockSpec((None, 1, 3 * D), lambda b, i: (b, 0, 0)),
                  full(1, D), full(1, D), full(D, D), full(1, D), full(1, D)],
        out_specs=tile(D),
        scratch_shapes=[pltpu.VMEM((tm, CONV_C), F32), pltpu.VMEM((tm, D), F32), pltpu.VMEM((tm, D), BF16),
                        pltpu.VMEM((tm, D), F32)],
        compiler_params=_params("arbitrary", "arbitrary"),
        name="tail1",
    )(x, hc_row, hc_col, h1, w_z_half, mod, ln_g_half, ln_b_half, w_out, ng, nb)


def kernel(x, c, ctx, c_ctx, mod_w, mod_b, norm_g, norm_b, ev_w_in, ev_w_out, s5_lam_re, s5_lam_im, s5_log_dt, s5_b_re, s5_b_im, s5_c_re, s5_c_im, s5_d, glu_w, glu_b, sgu_ln_g, sgu_ln_b, sgu_w, sgu_b, od_w_in, od_w_out, dw_w, dw_b, conv_ln_g, conv_ln_b):
    TH = S5_T * S5_H
    row = lambda v: v.reshape(1, -1)

    cond8 = jnp.concatenate([c, c_ctx[None], jnp.zeros((3, D), F32)], axis=0)
    mods = _adaln(cond8, mod_w, mod_b)
    mod0 = mods[0, :B].reshape(B, 1, 3 * D)
    mod0c = mods[0, B:B + 1]
    mod1 = mods[1, :B].reshape(B, 1, 3 * D)

    lbr, lbi, cfr, cfi = _discretise(s5_lam_re[0], s5_lam_im[0], s5_log_dt[0])
    rowcat = lambda a: jnp.concatenate([a[0], a[1]], axis=-1).reshape(S5_G, 1, 2 * S5_P)
    bt = lambda a: jnp.concatenate([jnp.swapaxes(a[0], 1, 2), jnp.swapaxes(a[1], 1, 2)], axis=-1)
    cn = lambda a: jnp.concatenate([a[0], a[1]], axis=-1)
    d_row = jnp.tile(s5_d[0].reshape(S5_G, 1, S5_H), (1, 1, S5_T))
    win, wout, mix, l16 = _s5_weights(rowcat(lbr), rowcat(lbi), rowcat(cfr), rowcat(cfi),
                                      bt(s5_b_re[0]), bt(s5_b_im[0]), cn(s5_c_re[0]), cn(s5_c_im[0]), d_row)

    w_out0 = ev_w_out[0].astype(BF16)
    glu_w0 = glu_w[0].astype(BF16)
    guz, vln, hs = _inproj0n(x, mod0, ev_w_in[0], row(sgu_ln_g[0]), row(sgu_ln_b[0]))
    ua, sza, ua_c = _inproj0a(hs, _ctx_slabs(ctx, mod0c), ev_w_in[0])
    s_lat = _s5core(ua, ua_c, win, wout, mix, l16)
    y_s5 = _s5tail(s_lat, sza, glu_w0, row(glu_b[0]), w_out0)
    sguw = sgu_w[0].reshape(SGU_HEADS // 2, 2, SGU_CHUNK, SGU_CHUNK)
    sguw = jnp.transpose(sguw, (0, 2, 1, 3)).reshape(SGU_HEADS // 2, SGU_CHUNK, 2 * SGU_CHUNK).astype(BF16)
    sgub = jnp.repeat(sgu_b[0].T, SGU_HD, axis=1)
    x1 = _tail0(x, y_s5, guz, vln, mod0, sguw, sgub, w_out0, row(norm_g[0]), row(norm_b[0]))

    hg_row, hg_col, h1 = _inproj1(x1, mod1, od_w_in[0])
    consts = _dft_constants()
    taps = jnp.pad(dw_w[0], ((0, TAPS_PAD - CONV_K), (0, 0)))
    bias = row(dw_b[0])
    hc_row = _fconv(hg_row, taps[:, :CONV_C], bias[:, :CONV_C], consts)
    hc_col = _fconv(hg_col.reshape(B, L, CONV_C), taps[:, CONV_C:], bias[:, CONV_C:], consts)
    w_z_half = (0.5 * od_w_in[0][:, 2 * D:]).astype(BF16)
    return _tail1(x1, hc_row, hc_col.reshape(B, GRID_W, GRID_W, CONV_C), h1, w_z_half, mod1,
                  row(0.5 * conv_ln_g[0]), row(0.5 * conv_ln_b[0]), od_w_out[0].astype(BF16),
                  row(norm_g[1]), row(norm_b[1]))
```

```python
import functools
import math

import jax
import jax.numpy as jnp
from jax import lax
from jax.experimental import pallas as pl
from jax.experimental.pallas import tpu as pltpu

D = 1024
B = 4
L = 4096
CTX = 256
GRID_W = 64
S5_W = 512
S5_G = 32
S5_H = 16
H_SHIFT = 4
BLK = 128 // S5_H
S5_P = 64
S5_T = 16
SGU_W = 512
SGU_HEADS = 8
SGU_HD = 64
SGU_CHUNK = 128
CONV_K = 31
CONV_HALF = CONV_K // 2
EVEN_IN = 2560
SGU_COL0 = 2 * S5_W
ODD_IN = 3072
DEPTH = 2
DN_ALPHA = (2 * DEPTH) ** 0.25
LN_EPS = 1e-5
N_CHUNK = L // S5_T
N_CCHUNK = CTX // S5_T
VMEM_LIMIT_V7X = 56 * 1024 * 1024
TOKEN_TILE = 1024

F32 = jnp.float32
BF16 = jnp.bfloat16


GELU_C = math.sqrt(2.0 / math.pi)


def _gelu(x):
    hx = 0.5 * x
    return hx * jnp.tanh(x * ((x * x) * (0.044715 * GELU_C) + GELU_C)) + hx


def _sigmoid(x):
    return 0.5 * jnp.tanh(0.5 * x) + 0.5


def _silu_of_half(hx):
    return hx * jnp.tanh(hx) + hx


def _silu(x):
    return _silu_of_half(0.5 * x)


def _layer_norm(x, g, b):
    mu = jnp.mean(x, axis=-1, keepdims=True)
    xc = x - mu
    var = jnp.mean(xc * xc, axis=-1, keepdims=True)
    return xc * lax.rsqrt(var + LN_EPS) * g + b


def _params(*sem):
    return pltpu.CompilerParams(dimension_semantics=sem, vmem_limit_bytes=VMEM_LIMIT_V7X)


def _adaln_kernel(c_ref, w_ref, b_ref, o_ref):
    def split(v):
        hi = v.astype(BF16)
        return hi, (v - hi.astype(F32)).astype(BF16)

    a_hi, a_lo = split(_silu(c_ref[...]))
    w_hi, w_lo = split(w_ref[...])
    dot = functools.partial(jnp.dot, preferred_element_type=F32)
    o_ref[...] = dot(a_hi, w_hi) + dot(a_lo, w_hi) + dot(a_hi, w_lo) + b_ref[...]


def _adaln(cond8, mod_w, mod_b):
    tn = 1024
    return pl.pallas_call(
        _adaln_kernel,
        out_shape=jax.ShapeDtypeStruct((DEPTH, 8, 3 * D), F32),
        grid=(DEPTH, 3 * D // tn),
        in_specs=[pl.BlockSpec((8, D), lambda l, j: (0, 0)),
                  pl.BlockSpec((None, D, tn), lambda l, j: (l, 0, j)),
                  pl.BlockSpec((None, 1, tn), lambda l, j: (l, 0, j))],
        out_specs=pl.BlockSpec((None, 8, tn), lambda l, j: (l, 0, j)),
        compiler_params=_params("arbitrary", "arbitrary"),
        name="adaln",
    )(cond8, mod_w, mod_b.reshape(DEPTH, 1, 3 * D))


def _disc_kernel(lr_ref, li_ref, ldt_ref, obr_ref, obi_ref, ocr_ref, oci_ref):
    lr = lr_ref[...]
    li = li_ref[...]
    dt = jnp.exp(ldt_ref[...])
    mag = jnp.exp(lr * dt)
    br = mag * jnp.cos(li * dt)
    bi = mag * jnp.sin(li * dt)
    inv = 1.0 / (lr * lr + li * li)
    nr = br - 1.0
    obr_ref[...] = br
    obi_ref[...] = bi
    ocr_ref[...] = (nr * lr + bi * li) * inv
    oci_ref[...] = (bi * lr - nr * li) * inv


def _discretise(lam_re, lam_im, log_dt):
    shp = jax.ShapeDtypeStruct((2 * S5_G, S5_P), F32)
    ldt = jnp.broadcast_to(log_dt.reshape(2 * S5_G, 1), (2 * S5_G, S5_P))
    outs = pl.pallas_call(
        _disc_kernel, out_shape=(shp, shp, shp, shp), name="s5_discretise",
    )(lam_re.reshape(2 * S5_G, S5_P), lam_im.reshape(2 * S5_G, S5_P), ldt)
    return [o.reshape(2, S5_G, S5_P) for o in outs]


S5W_GROUPS = 4


def _cpow(base_pows, j):
    re = None
    im = None
    for k, (pr, pi) in enumerate(base_pows):
        bit = ((j >> k) & 1) == 1
        mr = jnp.where(bit, pr, 1.0)
        mi = jnp.where(bit, pi, 0.0)
        if re is None:
            re, im = mr, mi
        else:
            re, im = re * mr - im * mi, re * mi + im * mr
    return re, im


def _squarings(pr, pi, n):
    out = [(pr, pi)]
    for _ in range(n - 1):
        pr, pi = pr * pr - pi * pi, 2.0 * pr * pi
        out.append((pr, pi))
    return out


def _shift_lanes(x, n):
    lane = lax.broadcasted_iota(jnp.int32, (S5_H, 128), 1)
    lo, hi = x[:, :128], x[:, 128:]
    if n == 0:
        return x
    if n < 128:
        rlo = pltpu.roll(lo, n, axis=1)
        rhi = pltpu.roll(hi, n, axis=1)
        return jnp.concatenate([jnp.where(lane >= n, rlo, 0.0), jnp.where(lane >= n, rhi, rlo)], axis=1)
    m = n - 128
    rlo = lo if m == 0 else pltpu.roll(lo, m, axis=1)
    return jnp.concatenate([jnp.zeros_like(lo), jnp.where(lane >= m, rlo, 0.0)], axis=1)


def _unshift_lanes(x, n):
    lane = lax.broadcasted_iota(jnp.int32, (S5_H, 128), 1)
    lo, hi = x[:, :128], x[:, 128:]
    if n == 0:
        return x
    if n < 128:
        rlo = pltpu.roll(lo, 128 - n, axis=1)
        rhi = pltpu.roll(hi, 128 - n, axis=1)
        keep = lane < 128 - n
        return jnp.concatenate([jnp.where(keep, rlo, rhi), jnp.where(keep, rhi, 0.0)], axis=1)
    m = n - 128
    rhi = hi if m == 0 else pltpu.roll(hi, 128 - m, axis=1)
    return jnp.concatenate([jnp.where(lane < 128 - m, rhi, 0.0), jnp.zeros_like(lo)], axis=1)


def _s5w_group(gi, bg, lrow_re, lrow_im, crow_re, crow_im, bt_re, bt_im,
               cn_re, cn_im, d_ref, win_ref, wout_ref, mix_ref, l16_ref):
    TH = S5_T * S5_H

    def chunk_pos(idx):
        return (((idx >> H_SHIFT) - bg) & (BLK - 1)) + ((idx >> 7) << 3)

    lr = lrow_re[gi]
    li = lrow_im[gi]
    pows_row = _squarings(lr, li, 5)
    l16_ref[gi, 0:1, :] = pows_row[4][0]
    l16_ref[gi, 1:2, :] = pows_row[4][1]
    l16_ref[gi, 2:8, :] = jnp.zeros((6, 128), F32)
    cr = crow_re[gi]
    ci = crow_im[gi]
    btr = bt_re[gi]
    bti = bt_im[gi]
    bbr = cr * btr - ci * bti
    bbi = cr * bti + ci * btr
    blk16 = lax.broadcasted_iota(jnp.int32, (S5_T, 128), 0)
    is_f16 = lax.broadcasted_iota(jnp.int32, (S5_T, 128), 1) < S5_P
    pos16 = chunk_pos(blk16 << H_SHIFT)
    pr16, pi16 = _cpow(pows_row[:4], jnp.where(is_f16, S5_T - 1 - pos16, pos16))
    rep_rows = lambda v: jnp.broadcast_to(v[:, None, :], (S5_T, S5_H, 128)).reshape(TH, 128)
    pr, pi = rep_rows(pr16), rep_rows(pi16)
    tbr = jnp.broadcast_to(bbr[None], (S5_T, S5_H, 128)).reshape(TH, 128)
    tbi = jnp.broadcast_to(bbi[None], (S5_T, S5_H, 128)).reshape(TH, 128)
    win_ref[gi, :, 0:128] = (pr * tbr - pi * tbi).astype(BF16)
    win_ref[gi, :, 128:256] = (pr * tbi + pi * tbr).astype(BF16)

    hp = lax.Precision.HIGHEST
    dot = functools.partial(jnp.dot, preferred_element_type=F32, precision=hp)
    def col256(r):
        col = jnp.broadcast_to(r, (2 * S5_P, 2 * S5_P)).T
        return jnp.concatenate([col, col], axis=1)

    def tiled_t(cn):
        t8 = jnp.broadcast_to(cn[None], (BLK, S5_H, 2 * S5_P)).reshape(2 * S5_P, 2 * S5_P).T
        return jnp.concatenate([t8, t8], axis=1)

    cpows = _squarings(col256(lr), col256(li), 4)
    row = lax.broadcasted_iota(jnp.int32, (2 * S5_P, TH), 0)
    lane_w = lax.broadcasted_iota(jnp.int32, (2 * S5_P, TH), 1)
    t_idx = chunk_pos(lane_w)
    j_idx = lane_w >> H_SHIFT
    is_f = row < S5_P
    ctr = tiled_t(cn_re[gi])
    cti = tiled_t(cn_im[gi])
    er, ei = _cpow(cpows, jnp.where(is_f, t_idx, S5_T - 1 - t_idx))
    er, ei = er * cpows[0][0] - ei * cpows[0][1], er * cpows[0][1] + ei * cpows[0][0]
    wr = ctr * er - cti * ei
    wi = ctr * ei + cti * er
    wout_ref[gi, 0:128, :] = wr.astype(BF16)
    wout_ref[gi, 128:256, :] = (-wi).astype(BF16)
    kr, ki = _cpow(cpows, jnp.where(is_f, j_idx, S5_T - 1 - j_idx))
    ekr = ctr * kr - cti * ki
    eki = ctr * ki + cti * kr
    lane16 = lax.broadcasted_iota(jnp.int32, (S5_H, 128), 1)
    mf = lane16 < S5_P
    kkf = dot(jnp.where(mf, bbr, 0.0), ekr) - dot(jnp.where(mf, bbi, 0.0), eki)
    kkb = dot(jnp.where(mf, 0.0, bbr), ekr) - dot(jnp.where(mf, 0.0, bbi), eki)
    dl = d_ref[gi]
    r16 = lax.broadcasted_iota(jnp.int32, (S5_H, TH), 0)
    l256 = lax.broadcasted_iota(jnp.int32, (S5_H, TH), 1)
    rot = bg * S5_H
    for s in range(S5_T):
        blk = _shift_lanes(kkf, S5_H * s) + _unshift_lanes(kkb, S5_H * (S5_T - 1 - s))
        blk = blk + jnp.where(l256 == r16 + S5_H * s, dl, 0.0)
        blk = jnp.concatenate([pltpu.roll(blk[:, :128], rot, axis=1), pltpu.roll(blk[:, 128:], rot, axis=1)], axis=1)
        rho = ((s + bg) & (BLK - 1)) + (s & BLK)
        mix_ref[gi, pl.ds(pl.multiple_of(rho * S5_H, S5_H), S5_H), :] = blk.astype(BF16)


def _s5w_kernel(*refs):
    for gi in range(S5W_GROUPS):
        bg = (pl.program_id(0) * S5W_GROUPS + gi) & (BLK - 1)
        _s5w_group(gi, bg, *refs)


def _s5_weights(lrow_re, lrow_im, crow_re, crow_im, bt_re, bt_im, cn_re, cn_im, d_row):
    TH = S5_T * S5_H
    g3 = lambda r, c: pl.BlockSpec((S5W_GROUPS, r, c), lambda g: (g, 0, 0))
    wshape = jax.ShapeDtypeStruct((S5_G, TH, TH), BF16)
    return pl.pallas_call(
        _s5w_kernel,
        out_shape=(wshape, wshape, wshape, jax.ShapeDtypeStruct((S5_G, 8, 128), F32)),
        grid=(S5_G // S5W_GROUPS,),
        in_specs=[g3(1, 128)] * 4 + [g3(S5_H, 128)] * 4 + [g3(1, TH)],
        out_specs=(g3(TH, TH), g3(TH, TH), g3(TH, TH), g3(8, 128)),
        compiler_params=_params("arbitrary"),
        name="s5_weights",
    )(lrow_re, lrow_im, crow_re, crow_im, bt_re, bt_im, cn_re, cn_im, d_row)


def _rot_blocks(v, r):
    cols = [pltpu.roll(v[:, 128 * q:128 * (q + 1)], S5_H * r, axis=1) for q in range(v.shape[1] // 128)]
    return jnp.concatenate(cols, axis=1)


def _slabs_of(h, hs_ref):
    h3 = h.reshape(h.shape[0] // S5_T, S5_T, h.shape[1])
    for s in range(S5_T):
        hs_ref[s] = h3[:, s, :].astype(BF16)


PERM_ROWS = S5_T * S5_T


def _chunk_transpose_perm():
    ri = lax.broadcasted_iota(jnp.int32, (PERM_ROWS, PERM_ROWS), 0)
    ci = lax.broadcasted_iota(jnp.int32, (PERM_ROWS, PERM_ROWS), 1)
    hit = ((ri >> H_SHIFT) == (ci & (S5_T - 1))) & ((ri & (S5_T - 1)) == (ci >> H_SHIFT))
    return jnp.where(hit, 1.0, 0.0).astype(BF16)


def _inproj0n_kernel(x_ref, mod_ref, w_ref, lng_ref, lnb_ref, guz_ref, vln_ref, hs_ref):
    shift = mod_ref[:, 0:D]
    scale = mod_ref[:, D:2 * D]
    hb = (x_ref[...] * (1.0 + scale) + shift).astype(BF16)
    perm = _chunk_transpose_perm()
    for j in range(hb.shape[0] // PERM_ROWS):
        blk = jnp.dot(perm, hb[PERM_ROWS * j:PERM_ROWS * (j + 1), :], preferred_element_type=F32).astype(BF16)
        for s in range(S5_T):
            hs_ref[s, S5_T * j:S5_T * (j + 1), :] = blk[S5_T * s:S5_T * (s + 1), :]
    dot = lambda lo: jnp.dot(hb, w_ref[:, SGU_COL0 + lo:SGU_COL0 + lo + 512].astype(BF16),
                             preferred_element_type=F32)
    guz_ref[...] = (_gelu(dot(0)) * _silu(dot(1024))).astype(BF16)
    vln_ref[...] = _layer_norm(_gelu(dot(512)), lng_ref[...], lnb_ref[...]).astype(BF16)


def _inproj0n(x, mod, w_in_f32, ln_g, ln_b, tm=TOKEN_TILE):
    nct = tm // S5_T
    o = jax.ShapeDtypeStruct((B, L, 512), BF16)
    ospec = pl.BlockSpec((None, tm, 512), lambda b, i: (b, i, 0))
    full = lambda *s: pl.BlockSpec(s, lambda b, i: (0,) * len(s))
    return pl.pallas_call(
        _inproj0n_kernel,
        out_shape=(o, o, jax.ShapeDtypeStruct((S5_T, B * N_CHUNK, D), BF16)),
        grid=(B, L // tm),
        in_specs=[pl.BlockSpec((None, tm, D), lambda b, i: (b, i, 0)),
                  pl.BlockSpec((None, 1, 3 * D), lambda b, i: (b, 0, 0)),
                  pl.BlockSpec((D, EVEN_IN), lambda b, i: (0, 0), pipeline_mode=pl.Buffered(1)),
                  full(1, 512), full(1, 512)],
        out_specs=(ospec, ospec,
                   pl.BlockSpec((S5_T, nct, D), lambda b, i: (0, b * (N_CHUNK // nct) + i, 0))),
        compiler_params=_params("arbitrary", "arbitrary"),
        name="inproj0n",
    )(x, mod, w_in_f32, ln_g, ln_b)


def _ctx_slabs_kernel(x_ref, mod_ref, hs_ref):
    h = x_ref[...] * (1.0 + mod_ref[:, D:2 * D]) + mod_ref[:, 0:D]
    _slabs_of(h, hs_ref)


def _ctx_slabs(ctx, mod_c):
    return pl.pallas_call(
        _ctx_slabs_kernel,
        out_shape=jax.ShapeDtypeStruct((S5_T, B * N_CCHUNK, D), BF16),
        grid=(B,),
        in_specs=[pl.BlockSpec((None, CTX, D), lambda b: (b, 0, 0)),
                  pl.BlockSpec((1, 3 * D), lambda b: (0, 0))],
        out_specs=pl.BlockSpec((S5_T, N_CCHUNK, D), lambda b: (0, b, 0)),
        compiler_params=_params("arbitrary"),
        name="ctx_slabs",
    )(ctx, mod_c)


def _inproj0a_kernel(hs_ref, hc_ref, w_ref, ua_ref, sza_ref, uc_ref):
    r = pl.program_id(0)
    h = hs_ref[...]
    w_ua = w_ref[:, 0:512].astype(BF16)
    ua_ref[...] = _rot_blocks(jnp.dot(h, w_ua, preferred_element_type=F32), r).astype(BF16)
    sza_ref[...] = _silu(jnp.dot(h, w_ref[:, 512:1024].astype(BF16), preferred_element_type=F32)).astype(BF16)
    uc_ref[...] = _rot_blocks(jnp.dot(hc_ref[...], w_ua, preferred_element_type=F32), r).astype(BF16)


def _inproj0a(hs, hcs, w_in_f32):
    slab = lambda r, h: r + BLK * h
    sspec = lambda n, w: pl.BlockSpec((None, n, w), lambda r, h: (slab(r, h), 0, 0))
    so = lambda n: jax.ShapeDtypeStruct((S5_T, n, 512), BF16)
    nl, ncx = B * N_CHUNK, B * N_CCHUNK
    return pl.pallas_call(
        _inproj0a_kernel,
        out_shape=(so(nl), so(nl), so(ncx)),
        grid=(BLK, S5_T // BLK),
        in_specs=[sspec(nl, D), sspec(ncx, D), pl.BlockSpec((D, SGU_COL0), lambda r, h: (0, 0))],
        out_specs=(sspec(nl, 512), sspec(nl, 512), sspec(ncx, 512)),
        compiler_params=_params("arbitrary", "arbitrary"),
        name="inproj0a",
    )(hs, hcs, w_in_f32)


SCAN_GROUPS = 4


def _scan_tiles(sre_ref, sim_ref, h_refs, n_tiles, carry, lams):
    row = lax.broadcasted_iota(jnp.int32, (8, 128), 0)
    lane = lax.broadcasted_iota(jnp.int32, (8, 128), 1)
    first = row < B
    fwd = lane < S5_P

    def body(k, c):
        of = pl.multiple_of(k * 8, 8)
        ob = pl.multiple_of((n_tiles - 1 - k) * 8, 8)
        out = []
        for gi in range(SCAN_GROUPS):
            lre, lim = lams[gi]
            hr, hi = c[2 * gi], c[2 * gi + 1]
            sr = jnp.where(fwd, sre_ref[gi, pl.ds(of, 8), :], pltpu.roll(sre_ref[gi, pl.ds(ob, 8), :], B, axis=0))
            si = jnp.where(fwd, sim_ref[gi, pl.ds(of, 8), :], pltpu.roll(sim_ref[gi, pl.ds(ob, 8), :], B, axis=0))
            h1r = lre * hr - lim * hi + sr
            h1i = lre * hi + lim * hr + si
            r1r = pltpu.roll(h1r, B, axis=0)
            r1i = pltpu.roll(h1i, B, axis=0)
            if h_refs is not None:
                fre_ref, fim_ref, bre_ref, bim_ref = h_refs
                er = jnp.where(first, hr, r1r)
                ei = jnp.where(first, hi, r1i)
                fre_ref[gi, pl.ds(of, 8), :] = er
                fim_ref[gi, pl.ds(of, 8), :] = ei
                bre_ref[gi, pl.ds(ob, 8), :] = pltpu.roll(er, B, axis=0)
                bim_ref[gi, pl.ds(ob, 8), :] = pltpu.roll(ei, B, axis=0)
            h2r = lre * r1r - lim * r1i + sr
            h2i = lre * r1i + lim * r1r + si
            out.append(jnp.where(first, pltpu.roll(h2r, B, axis=0), h2r))
            out.append(jnp.where(first, pltpu.roll(h2i, B, axis=0), h2i))
        return tuple(out)

    return lax.fori_loop(0, n_tiles, body, carry)


def _gather_group(slab_ref, src):
    halves = []
    for h in range(S5_T // BLK):
        acc = slab_ref[BLK * h]
        for s in range(1, BLK):
            acc = jnp.where(src == s, slab_ref[BLK * h + s], acc)
        halves.append(acc)
    return jnp.concatenate(halves, axis=1)


def _s5core_kernel(ul_ref, uc_ref, win_ref, wout_ref, mix_ref, l16_ref, o_ref,
                   u_ref, sre_ref, sim_ref, cre_ref, cim_ref, fre_ref, fim_ref, bre_ref, bim_ref, y_ref):
    nl = N_CHUNK * B
    ncx = N_CCHUNK * B
    blk_l = lax.broadcasted_iota(jnp.int32, (nl, 128), 1) >> H_SHIFT
    blk_c = lax.broadcasted_iota(jnp.int32, (ncx, 128), 1) >> H_SHIFT
    fwd = lax.broadcasted_iota(jnp.int32, (N_CHUNK, 128), 1) < S5_P
    for g0 in range(0, BLK, SCAN_GROUPS):
        for gi in range(SCAN_GROUPS):
            bg = g0 + gi
            win = win_ref[bg]
            src_l = ((blk_l - bg) & (BLK - 1)).astype(F32).astype(BF16)
            src_c = ((blk_c - bg) & (BLK - 1)).astype(F32).astype(BF16)
            u = _gather_group(ul_ref, src_l)
            u_ref[gi] = u
            sl = jnp.dot(u, win, preferred_element_type=F32)
            sc = jnp.dot(_gather_group(uc_ref, src_c), win, preferred_element_type=F32)
            for b in range(B):
                sre_ref[gi, pl.ds(b, N_CHUNK, stride=B), :] = sl[N_CHUNK * b:N_CHUNK * (b + 1), 0:128]
                sim_ref[gi, pl.ds(b, N_CHUNK, stride=B), :] = sl[N_CHUNK * b:N_CHUNK * (b + 1), 128:256]
                cre_ref[gi, pl.ds(b, N_CCHUNK, stride=B), :] = sc[N_CCHUNK * b:N_CCHUNK * (b + 1), 0:128]
                cim_ref[gi, pl.ds(b, N_CCHUNK, stride=B), :] = sc[N_CCHUNK * b:N_CCHUNK * (b + 1), 128:256]
        lams = [(jnp.broadcast_to(l16_ref[g0 + gi, 0:1, :], (8, 128)),
                 jnp.broadcast_to(l16_ref[g0 + gi, 1:2, :], (8, 128))) for gi in range(SCAN_GROUPS)]
        zero = tuple(jnp.zeros((8, 128), F32) for _ in range(2 * SCAN_GROUPS))
        carry = _scan_tiles(cre_ref, cim_ref, None, ncx // 8, zero, lams)
        _scan_tiles(sre_ref, sim_ref, (fre_ref, fim_ref, bre_ref, bim_ref), nl // 8, carry, lams)
        for gi in range(SCAN_GROUPS):
            bg = g0 + gi
            y = jnp.dot(u_ref[gi], mix_ref[bg], preferred_element_type=F32)
            hs = []
            for b in range(B):
                rows = pl.ds(b, N_CHUNK, stride=B)
                hs.append(jnp.concatenate([jnp.where(fwd, fre_ref[gi, rows, :], bre_ref[gi, rows, :]),
                                           jnp.where(fwd, fim_ref[gi, rows, :], bim_ref[gi, rows, :])], axis=1))
            hcat = jnp.concatenate(hs, axis=0).astype(BF16)
            y = y + jnp.dot(hcat, wout_ref[bg], preferred_element_type=F32)
            y_ref[bg] = y.astype(BF16)

    blk = blk_l.astype(F32).astype(BF16)
    for s in range(S5_T):
        h, r = s // BLK, s % BLK
        acc = None
        for j in range(BLK):
            piece = y_ref[(j - r) % BLK, :, 128 * h:128 * (h + 1)]
            acc = piece if acc is None else jnp.where(blk == j, piece, acc)
        o_ref[s] = acc


def _s5core(ul, uc, win, wout, mix, l16):
    TH = S5_T * S5_H
    nl = N_CHUNK * B
    ncx = N_CCHUNK * B
    g4 = lambda r, c: pl.BlockSpec((BLK, r, c), lambda q: (q, 0, 0))
    col = lambda n: pl.BlockSpec((S5_T, n, 128), lambda q: (0, 0, q))
    f32s = lambda n: pltpu.VMEM((SCAN_GROUPS, n, 128), F32)
    return pl.pallas_call(
        _s5core_kernel,
        out_shape=jax.ShapeDtypeStruct((S5_T, nl, S5_W), BF16),
        grid=(S5_G // BLK,),
        in_specs=[col(nl), col(ncx), g4(TH, TH), g4(TH, TH), g4(TH, TH), g4(8, 128)],
        out_specs=col(nl),
        scratch_shapes=[pltpu.VMEM((SCAN_GROUPS, nl, TH), BF16),
                        f32s(nl), f32s(nl), f32s(ncx), f32s(ncx), f32s(nl), f32s(nl), f32s(nl), f32s(nl),
                        pltpu.VMEM((BLK, nl, TH), BF16)],
        compiler_params=_params("arbitrary"),
        name="s5core",
    )(ul, uc, win, wout, mix, l16)


def _s5tail_kernel(slat_ref, sza_ref, gluw_ref, glub_ref, wtop_ref, y_ref):
    unrot = (BLK - pl.program_id(0)) & (BLK - 1)
    for b in range(B):
        rows = slice(N_CHUNK * b, N_CHUNK * (b + 1))
        g = _gelu(_rot_blocks(slat_ref[rows, :].astype(F32), unrot))
        gate = _sigmoid(jnp.dot(g.astype(BF16), gluw_ref[...], preferred_element_type=F32) + glub_ref[...])
        a = (g * gate * sza_ref[rows, :].astype(F32)).astype(BF16)
        y_ref[rows, :] = jnp.dot(a, wtop_ref[...], preferred_element_type=F32).astype(BF16)


def _s5tail(slat, sza, glu_w, glu_b, w_top):
    slab = lambda r, h: r + BLK * h
    sspec = lambda w: pl.BlockSpec((None, N_CHUNK * B, w), lambda r, h: (slab(r, h), 0, 0))
    full = lambda *s: pl.BlockSpec(s, lambda r, h: (0,) * len(s))
    return pl.pallas_call(
        _s5tail_kernel,
        out_shape=jax.ShapeDtypeStruct((S5_T, N_CHUNK * B, D), BF16),
        grid=(BLK, S5_T // BLK),
        in_specs=[sspec(512), sspec(512), full(512, 512), full(1, 512), full(S5_W, D)],
        out_specs=sspec(D),
        compiler_params=_params("arbitrary", "arbitrary"),
        name="s5tail",
    )(slat, sza, glu_w, glu_b, w_top)


def _tail0_kernel(x_ref, ys5_ref, guz_ref, vln_ref, mod_ref, sguw_ref, sgub_ref, wbot_ref, ng_ref, nb_ref, o_ref):
    tm = x_ref.shape[0]
    lane = lax.broadcasted_iota(jnp.int32, (SGU_CHUNK, 128), 1)
    lo = lane < SGU_HD
    zero = jnp.zeros((SGU_CHUNK, 128), BF16)
    chunks = []
    for ci in range(tm // SGU_CHUNK):
        v = vln_ref[ci * SGU_CHUNK:(ci + 1) * SGU_CHUNK, :]
        cols = []
        for pi in range(SGU_HEADS // 2):
            vp = v[:, 128 * pi:128 * (pi + 1)]
            bm = jnp.concatenate([jnp.where(lo, vp, zero), jnp.where(lo, zero, vp)], axis=0)
            cols.append(jnp.dot(sguw_ref[pi], bm, preferred_element_type=F32))
        chunks.append(jnp.concatenate(cols, axis=1) + sgub_ref[...])
    s = jnp.concatenate(chunks, axis=0)
    bsg = (guz_ref[...].astype(F32) * s).astype(BF16)
    perm = _chunk_transpose_perm()
    ys5 = jnp.concatenate(
        [jnp.dot(perm, ys5_ref[:, S5_T * j:S5_T * (j + 1), :].reshape(PERM_ROWS, D), preferred_element_type=F32)
         for j in range(tm // PERM_ROWS)], axis=0)
    y = ys5 + jnp.dot(bsg, wbot_ref[...], preferred_element_type=F32)
    gmod = mod_ref[:, 2 * D:3 * D]
    o_ref[...] = _layer_norm(DN_ALPHA * x_ref[...] + gmod * y, ng_ref[...], nb_ref[...])


def _tail0(x, ys5, guz, vln, mod, sguw, sgub, w_bot, ng, nb, tm=TOKEN_TILE):
    nct = tm // S5_T
    t512 = pl.BlockSpec((None, tm, 512), lambda b, i: (b, i, 0))
    tD = pl.BlockSpec((None, tm, D), lambda b, i: (b, i, 0))
    full = lambda *s: pl.BlockSpec(s, lambda b, i: (0,) * len(s))
    return pl.pallas_call(
        _tail0_kernel,
        out_shape=jax.ShapeDtypeStruct((B, L, D), F32),
        grid=(B, L // tm),
        in_specs=[tD, pl.BlockSpec((S5_T, nct, D), lambda b, i: (0, b * (N_CHUNK // nct) + i, 0)), t512, t512,
                  pl.BlockSpec((None, 1, 3 * D), lambda b, i: (b, 0, 0)),
                  full(SGU_HEADS // 2, SGU_CHUNK, 256), full(SGU_CHUNK, 512),
                  pl.BlockSpec((SGU_W, D), lambda b, i: (1, 0)), full(1, D), full(1, D)],
        out_specs=tD,
        compiler_params=_params("arbitrary", "arbitrary"),
        name="tail0",
    )(x, ys5, guz, vln, mod, sguw, sgub, w_bot, ng, nb)


CONV_C = D // 2
TILE_ROWS = TOKEN_TILE // GRID_W


def _grid_transpose_in(v, o_ref):
    perm = _chunk_transpose_perm()
    for q in range(GRID_W // S5_T):
        seg = jnp.concatenate([v[GRID_W * r + S5_T * q:GRID_W * r + S5_T * (q + 1), :] for r in range(TILE_ROWS)],
                              axis=0)
        t = jnp.dot(perm, seg, preferred_element_type=F32).astype(BF16)
        o_ref[S5_T * q:S5_T * (q + 1), :, :] = t.reshape(S5_T, TILE_ROWS, v.shape[1])


def _inproj1_kernel(x_ref, mod_ref, w_ref, hgr_ref, hgc_ref, h_ref):
    shift = mod_ref[:, 0:D]
    scale = mod_ref[:, D:2 * D]
    h = (x_ref[...] * (1.0 + scale) + shift).astype(BF16)
    h_ref[...] = h
    dot = lambda lo: jnp.dot(h, w_ref[:, lo:lo + CONV_C].astype(BF16), preferred_element_type=F32)
    hgr_ref[...] = (dot(0) * _sigmoid(dot(D))).astype(BF16)
    _grid_transpose_in((dot(CONV_C) * _sigmoid(dot(D + CONV_C))).astype(BF16), hgc_ref)


def _inproj1(x, mod, w_in_f32, tm=TOKEN_TILE):
    tile = lambda w: pl.BlockSpec((None, tm, w), lambda b, i: (b, i, 0))
    return pl.pallas_call(
        _inproj1_kernel,
        out_shape=(jax.ShapeDtypeStruct((B, L, CONV_C), BF16),
                   jax.ShapeDtypeStruct((B, GRID_W, GRID_W, CONV_C), BF16),
                   jax.ShapeDtypeStruct((B, L, D), BF16)),
        grid=(B, L // tm),
        in_specs=[tile(D),
                  pl.BlockSpec((None, 1, 3 * D), lambda b, i: (b, 0, 0)),
                  pl.BlockSpec((D, 2 * D), lambda b, i: (0, 0), pipeline_mode=pl.Buffered(1))],
        out_specs=(tile(CONV_C), pl.BlockSpec((None, GRID_W, TILE_ROWS, CONV_C), lambda b, i: (b, 0, i, 0)),
                   tile(D)),
        compiler_params=_params("arbitrary", "arbitrary"),
        name="inproj1",
    )(x, mod, w_in_f32)


DFT_N = 2 * GRID_W
TAPS_PAD = CONV_K + 1


def _dft_constants():
    th = 2.0 * math.pi / DFT_N
    f = jnp.arange(GRID_W, dtype=F32)[:, None]
    p = jnp.arange(GRID_W, dtype=F32)[None, :]
    cosm = jnp.cos(th * f * p)
    sinm = jnp.sin(th * f * p)
    alt = jnp.where(jnp.arange(GRID_W) % 2 == 0, 1.0, -1.0).astype(F32)
    fwd = jnp.concatenate([cosm, alt[None, :], sinm[1:]], axis=0)
    cf = jnp.where(jnp.arange(GRID_W) == 0, 1.0, 2.0).astype(F32) / DFT_N
    inv = jnp.concatenate([cosm.T * cf[None, :], (alt / DFT_N)[:, None], sinm.T[:, 1:] * (2.0 / DFT_N)], axis=1)
    sft = (CONV_HALF - jnp.arange(TAPS_PAD, dtype=F32))[None, :]
    live = (jnp.arange(TAPS_PAD) < CONV_K).astype(F32)[None, :]
    f64 = jnp.where(f == 0, float(GRID_W), f)
    return (fwd.astype(BF16), inv.astype(BF16),
            jnp.cos(th * f * sft) * live, jnp.sin(th * f * sft) * live, jnp.cos(th * f64 * sft) * live)


def _fconv_kernel(h_ref, w_ref, b_ref, fwd_ref, inv_ref, c1_ref, s3_ref, c4_ref, o_ref):
    hp = lax.Precision.HIGHEST
    taps = w_ref[...]
    g_re = jnp.dot(c1_ref[...], taps, preferred_element_type=F32, precision=hp)
    g_im = jnp.dot(s3_ref[...], taps, preferred_element_type=F32, precision=hp)
    g_r2 = jnp.dot(c4_ref[...], taps, preferred_element_type=F32, precision=hp)
    fwd = fwd_ref[...]
    inv = inv_ref[...]
    bias = b_ref[...]
    n_runs = h_ref.shape[0] // GRID_W
    rows = lambda r: slice(GRID_W * r, GRID_W * (r + 1))
    forward = lambda r: jnp.dot(fwd, h_ref[rows(r), :], preferred_element_type=F32)
    ahead = 2
    specs = [forward(r) for r in range(ahead)]
    for r in range(n_runs):
        if r + ahead < n_runs:
            specs.append(forward(r + ahead))
        spec = specs[r]
        a, bm = spec[0:GRID_W], spec[GRID_W:DFT_N]
        prod = jnp.concatenate([a * g_re - bm * g_im, a * g_im + bm * g_r2], axis=0).astype(BF16)
        o_ref[rows(r), :] = (jnp.dot(inv, prod, preferred_element_type=F32) + bias).astype(BF16)


def _fconv(h, taps, bias, consts, tm=2 * TOKEN_TILE):
    fwd, inv, c1, s3, c4 = consts
    c = h.shape[-1]
    tile = pl.BlockSpec((None, tm, c), lambda b, i: (b, i, 0))
    full = lambda *s: pl.BlockSpec(s, lambda b, i: (0,) * len(s))
    return pl.pallas_call(
        _fconv_kernel,
        out_shape=jax.ShapeDtypeStruct(h.shape, BF16),
        grid=(B, L // tm),
        in_specs=[tile, full(TAPS_PAD, c), full(1, c), full(DFT_N, GRID_W), full(GRID_W, DFT_N),
                  full(GRID_W, TAPS_PAD), full(GRID_W, TAPS_PAD), full(GRID_W, TAPS_PAD)],
        out_specs=tile,
        compiler_params=_params("arbitrary", "arbitrary"),
        name="fconv",
    )(h, taps, bias, fwd, inv, c1, s3, c4)


ROW_BLOCK = 32


def _row_blocks(n_rows):
    return [slice(ROW_BLOCK * k, ROW_BLOCK * (k + 1)) for k in range(n_rows // ROW_BLOCK)]


def _tail1_kernel(x_ref, hcr_ref, hcc_ref, h1_ref, wz_ref, mod_ref, lng_ref, lnb_ref, wout_ref, ng_ref, nb_ref,
                  o_ref, col_ref, z_ref, m_ref, y_ref):
    tm = x_ref.shape[0]
    perm = _chunk_transpose_perm()
    for q in range(GRID_W // S5_T):
        blk = hcc_ref[S5_T * q:S5_T * (q + 1), :, :].reshape(PERM_ROWS, CONV_C)
        t = jnp.dot(perm, blk, preferred_element_type=F32)
        for r in range(TILE_ROWS):
            col_ref[GRID_W * r + S5_T * q:GRID_W * r + S5_T * (q + 1), :] = t[S5_T * r:S5_T * (r + 1), :]
    half = tm // 2
    halves = [slice(0, half), slice(half, tm)]
    for hs in halves:
        z_ref[hs, :] = jnp.dot(h1_ref[hs, :], wz_ref[...], preferred_element_type=F32)
    lng, lnb = lng_ref[...], lnb_ref[...]
    gmod = mod_ref[:, 2 * D:3 * D]
    ng, nb = ng_ref[...], nb_ref[...]

    def post_ln(hs):
        for rows in _row_blocks(half):
            rows = slice(hs.start + rows.start, hs.start + rows.stop)
            o_ref[rows, :] = _layer_norm(DN_ALPHA * x_ref[rows, :] + gmod * y_ref[rows, :], ng, nb)

    for k, hs in enumerate(halves):
        for rows in _row_blocks(half):
            rows = slice(hs.start + rows.start, hs.start + rows.stop)
            hc = jnp.concatenate([hcr_ref[rows, :].astype(F32), col_ref[rows, :]], axis=1)
            m_ref[rows, :] = (_silu_of_half(_layer_norm(hc, lng, lnb))
                              * _silu_of_half(z_ref[rows, :])).astype(BF16)
        y_ref[hs, :] = jnp.dot(m_ref[hs, :], wout_ref[...], preferred_element_type=F32)
        if k > 0:
            post_ln(halves[k - 1])
    post_ln(halves[-1])


def _tail1(x, hc_row, hc_col, h1, w_z_half, mod, ln_g_half, ln_b_half, w_out, ng, nb, tm=TOKEN_TILE):
    tile = lambda w: pl.BlockSpec((None, tm, w), lambda b, i: (b, i, 0))
    full = lambda *s: pl.BlockSpec(s, lambda b, i: (0,) * len(s))
    return pl.pallas_call(
        _tail1_kernel,
        out_shape=jax.ShapeDtypeStruct((B, L, D), F32),
        grid=(B, L // tm),
        in_specs=[tile(D), tile(CONV_C),
                  pl.BlockSpec((None, GRID_W, TILE_ROWS, CONV_C), lambda b, i: (b, 0, i, 0)),
                  tile(D), full(D, D), pl.BlockSpec((None, 1, 3 * D), lambda b, i: (b, 0, 0)),
                  full(1, D), full(1, D), full(D, D), full(1, D), full(1, D)],
        out_specs=tile(D),
        scratch_shapes=[pltpu.VMEM((tm, CONV_C), F32), pltpu.VMEM((tm, D), F32), pltpu.VMEM((tm, D), BF16),
                        pltpu.VMEM((tm, D), F32)],
        compiler_params=_params("arbitrary", "arbitrary"),
        name="tail1",
    )(x, hc_row, hc_col, h1, w_z_half, mod, ln_g_half, ln_b_half, w_out, ng, nb)


def kernel(x, c, ctx, c_ctx, mod_w, mod_b, norm_g, norm_b, ev_w_in, ev_w_out, s5_lam_re, s5_lam_im, s5_log_dt, s5_b_re, s5_b_im, s5_c_re, s5_c_im, s5_d, glu_w, glu_b, sgu_ln_g, sgu_ln_b, sgu_w, sgu_b, od_w_in, od_w_out, dw_w, dw_b, conv_ln_g, conv_ln_b):
    TH = S5_T * S5_H
    row = lambda v: v.reshape(1, -1)

    cond8 = jnp.concatenate([c, c_ctx[None], jnp.zeros((3, D), F32)], axis=0)
    mods = _adaln(cond8, mod_w, mod_b)
    mod0 = mods[0, :B].reshape(B, 1, 3 * D)
    mod0c = mods[0, B:B + 1]
    mod1 = mods[1, :B].reshape(B, 1, 3 * D)

    lbr, lbi, cfr, cfi = _discretise(s5_lam_re[0], s5_lam_im[0], s5_log_dt[0])
    rowcat = lambda a: jnp.concatenate([a[0], a[1]], axis=-1).reshape(S5_G, 1, 2 * S5_P)
    bt = lambda a: jnp.concatenate([jnp.swapaxes(a[0], 1, 2), jnp.swapaxes(a[1], 1, 2)], axis=-1)
    cn = lambda a: jnp.concatenate([a[0], a[1]], axis=-1)
    d_row = jnp.tile(s5_d[0].reshape(S5_G, 1, S5_H), (1, 1, S5_T))
    win, wout, mix, l16 = _s5_weights(rowcat(lbr), rowcat(lbi), rowcat(cfr), rowcat(cfi),
                                      bt(s5_b_re[0]), bt(s5_b_im[0]), cn(s5_c_re[0]), cn(s5_c_im[0]), d_row)

    w_out0 = ev_w_out[0].astype(BF16)
    glu_w0 = glu_w[0].astype(BF16)
    guz, vln, hs = _inproj0n(x, mod0, ev_w_in[0], row(sgu_ln_g[0]), row(sgu_ln_b[0]))
    ua, sza, ua_c = _inproj0a(hs, _ctx_slabs(ctx, mod0c), ev_w_in[0])
    s_lat = _s5core(ua, ua_c, win, wout, mix, l16)
    y_s5 = _s5tail(s_lat, sza, glu_w0, row(glu_b[0]), w_out0)
    sguw = sgu_w[0].reshape(SGU_HEADS // 2, 2, SGU_CHUNK, SGU_CHUNK)
    sguw = jnp.transpose(sguw, (0, 2, 1, 3)).reshape(SGU_HEADS // 2, SGU_CHUNK, 2 * SGU_CHUNK).astype(BF16)
    sgub = jnp.repeat(sgu_b[0].T, SGU_HD, axis=1)
    x1 = _tail0(x, y_s5, guz, vln, mod0, sguw, sgub, w_out0, row(norm_g[0]), row(norm_b[0]))

    hg_row, hg_col, h1 = _inproj1(x1, mod1, od_w_in[0])
    consts = _dft_constants()
    taps = jnp.pad(dw_w[0], ((0, TAPS_PAD - CONV_K), (0, 0)))
    bias = row(dw_b[0])
    hc_row = _fconv(hg_row, taps[:, :CONV_C], bias[:, :CONV_C], consts)
    hc_col = _fconv(hg_col.reshape(B, L, CONV_C), taps[:, CONV_C:], bias[:, CONV_C:], consts)
    w_z_half = (0.5 * od_w_in[0][:, 2 * D:]).astype(BF16)
    return _tail1(x1, hc_row, hc_col.reshape(B, GRID_W, GRID_W, CONV_C), h1, w_z_half, mod1,
                  row(0.5 * conv_ln_g[0]), row(0.5 * conv_ln_b[0]), od_w_out[0].astype(BF16),
                  row(norm_g[1]), row(norm_b[1]))
```

```python
import functools
import math

import jax
import jax.numpy as jnp
from jax import lax
from jax.experimental import pallas as pl
from jax.experimental.pallas import tpu as pltpu

D = 1024
B = 4
L = 4096
CTX = 256
GRID_W = 64
S5_W = 512
S5_G = 32
S5_H = 16
H_SHIFT = 4
BLK = 128 // S5_H
S5_P = 64
S5_T = 16
SGU_W = 512
SGU_HEADS = 8
SGU_HD = 64
SGU_CHUNK = 128
CONV_K = 31
CONV_HALF = CONV_K // 2
EVEN_IN = 2560
SGU_COL0 = 2 * S5_W
ODD_IN = 3072
DEPTH = 2
DN_ALPHA = (2 * DEPTH) ** 0.25
LN_EPS = 1e-5
N_CHUNK = L // S5_T
N_CCHUNK = CTX // S5_T
VMEM_LIMIT_V7X = 56 * 1024 * 1024
TOKEN_TILE = 1024

F32 = jnp.float32
BF16 = jnp.bfloat16


GELU_C = math.sqrt(2.0 / math.pi)


def _gelu(x):
    hx = 0.5 * x
    return hx * jnp.tanh(x * ((x * x) * (0.044715 * GELU_C) + GELU_C)) + hx


def _sigmoid(x):
    return 0.5 * jnp.tanh(0.5 * x) + 0.5


def _silu_of_half(hx):
    return hx * jnp.tanh(hx) + hx


def _silu(x):
    return _silu_of_half(0.5 * x)


def _layer_norm(x, g, b):
    mu = jnp.mean(x, axis=-1, keepdims=True)
    xc = x - mu
    var = jnp.mean(xc * xc, axis=-1, keepdims=True)
    return xc * lax.rsqrt(var + LN_EPS) * g + b


def _params(*sem):
    return pltpu.CompilerParams(dimension_semantics=sem, vmem_limit_bytes=VMEM_LIMIT_V7X)


def _adaln_kernel(c_ref, w_ref, b_ref, o_ref):
    def split(v):
        hi = v.astype(BF16)
        return hi, (v - hi.astype(F32)).astype(BF16)

    a_hi, a_lo = split(_silu(c_ref[...]))
    w_hi, w_lo = split(w_ref[...])
    dot = functools.partial(jnp.dot, preferred_element_type=F32)
    o_ref[...] = dot(a_hi, w_hi) + dot(a_lo, w_hi) + dot(a_hi, w_lo) + b_ref[...]


def _adaln(cond8, mod_w, mod_b):
    tn = 1024
    return pl.pallas_call(
        _adaln_kernel,
        out_shape=jax.ShapeDtypeStruct((DEPTH, 8, 3 * D), F32),
        grid=(DEPTH, 3 * D // tn),
        in_specs=[pl.BlockSpec((8, D), lambda l, j: (0, 0)),
                  pl.BlockSpec((None, D, tn), lambda l, j: (l, 0, j)),
                  pl.BlockSpec((None, 1, tn), lambda l, j: (l, 0, j))],
        out_specs=pl.BlockSpec((None, 8, tn), lambda l, j: (l, 0, j)),
        compiler_params=_params("arbitrary", "arbitrary"),
        name="adaln",
    )(cond8, mod_w, mod_b.reshape(DEPTH, 1, 3 * D))


def _disc_kernel(lr_ref, li_ref, ldt_ref, obr_ref, obi_ref, ocr_ref, oci_ref):
    lr = lr_ref[...]
    li = li_ref[...]
    dt = jnp.exp(ldt_ref[...])
    mag = jnp.exp(lr * dt)
    br = mag * jnp.cos(li * dt)
    bi = mag * jnp.sin(li * dt)
    inv = 1.0 / (lr * lr + li * li)
    nr = br - 1.0
    obr_ref[...] = br
    obi_ref[...] = bi
    ocr_ref[...] = (nr * lr + bi * li) * inv
    oci_ref[...] = (bi * lr - nr * li) * inv


def _discretise(lam_re, lam_im, log_dt):
    shp = jax.ShapeDtypeStruct((2 * S5_G, S5_P), F32)
    ldt = jnp.broadcast_to(log_dt.reshape(2 * S5_G, 1), (2 * S5_G, S5_P))
    outs = pl.pallas_call(
        _disc_kernel, out_shape=(shp, shp, shp, shp), name="s5_discretise",
    )(lam_re.reshape(2 * S5_G, S5_P), lam_im.reshape(2 * S5_G, S5_P), ldt)
    return [o.reshape(2, S5_G, S5_P) for o in outs]


S5W_GROUPS = 4


def _cpow(base_pows, j):
    re = None
    im = None
    for k, (pr, pi) in enumerate(base_pows):
        bit = ((j >> k) & 1) == 1
        mr = jnp.where(bit, pr, 1.0)
        mi = jnp.where(bit, pi, 0.0)
        if re is None:
            re, im = mr, mi
        else:
            re, im = re * mr - im * mi, re * mi + im * mr
    return re, im


def _squarings(pr, pi, n):
    out = [(pr, pi)]
    for _ in range(n - 1):
        pr, pi = pr * pr - pi * pi, 2.0 * pr * pi
        out.append((pr, pi))
    return out


def _shift_lanes(x, n):
    lane = lax.broadcasted_iota(jnp.int32, (S5_H, 128), 1)
    lo, hi = x[:, :128], x[:, 128:]
    if n == 0:
        return x
    if n < 128:
        rlo = pltpu.roll(lo, n, axis=1)
        rhi = pltpu.roll(hi, n, axis=1)
        return jnp.concatenate([jnp.where(lane >= n, rlo, 0.0), jnp.where(lane >= n, rhi, rlo)], axis=1)
    m = n - 128
    rlo = lo if m == 0 else pltpu.roll(lo, m, axis=1)
    return jnp.concatenate([jnp.zeros_like(lo), jnp.where(lane >= m, rlo, 0.0)], axis=1)


def _unshift_lanes(x, n):
    lane = lax.broadcasted_iota(jnp.int32, (S5_H, 128), 1)
    lo, hi = x[:, :128], x[:, 128:]
    if n == 0:
        return x
    if n < 128:
        rlo = pltpu.roll(lo, 128 - n, axis=1)
        rhi = pltpu.roll(hi, 128 - n, axis=1)
        keep = lane < 128 - n
        return jnp.concatenate([jnp.where(keep, rlo, rhi), jnp.where(keep, rhi, 0.0)], axis=1)
    m = n - 128
    rhi = hi if m == 0 else pltpu.roll(hi, 128 - m, axis=1)
    return jnp.concatenate([jnp.where(lane < 128 - m, rhi, 0.0), jnp.zeros_like(lo)], axis=1)


def _s5w_group(gi, bg, lrow_re, lrow_im, crow_re, crow_im, bt_re, bt_im,
               cn_re, cn_im, d_ref, win_ref, wout_ref, mix_ref, l16_ref):
    TH = S5_T * S5_H

    def chunk_pos(idx):
        return (((idx >> H_SHIFT) - bg) & (BLK - 1)) + ((idx >> 7) << 3)

    lr = lrow_re[gi]
    li = lrow_im[gi]
    pows_row = _squarings(lr, li, 5)
    l16_ref[gi, 0:1, :] = pows_row[4][0]
    l16_ref[gi, 1:2, :] = pows_row[4][1]
    l16_ref[gi, 2:8, :] = jnp.zeros((6, 128), F32)
    cr = crow_re[gi]
    ci = crow_im[gi]
    btr = bt_re[gi]
    bti = bt_im[gi]
    bbr = cr * btr - ci * bti
    bbi = cr * bti + ci * btr
    blk16 = lax.broadcasted_iota(jnp.int32, (S5_T, 128), 0)
    is_f16 = lax.broadcasted_iota(jnp.int32, (S5_T, 128), 1) < S5_P
    pos16 = chunk_pos(blk16 << H_SHIFT)
    pr16, pi16 = _cpow(pows_row[:4], jnp.where(is_f16, S5_T - 1 - pos16, pos16))
    rep_rows = lambda v: jnp.broadcast_to(v[:, None, :], (S5_T, S5_H, 128)).reshape(TH, 128)
    pr, pi = rep_rows(pr16), rep_rows(pi16)
    tbr = jnp.broadcast_to(bbr[None], (S5_T, S5_H, 128)).reshape(TH, 128)
    tbi = jnp.broadcast_to(bbi[None], (S5_T, S5_H, 128)).reshape(TH, 128)
    win_ref[gi, :, 0:128] = (pr * tbr - pi * tbi).astype(BF16)
    win_ref[gi, :, 128:256] = (pr * tbi + pi * tbr).astype(BF16)

    hp = lax.Precision.HIGHEST
    dot = functools.partial(jnp.dot, preferred_element_type=F32, precision=hp)
    def col256(r):
        col = jnp.broadcast_to(r, (2 * S5_P, 2 * S5_P)).T
        return jnp.concatenate([col, col], axis=1)

    def tiled_t(cn):
        t8 = jnp.broadcast_to(cn[None], (BLK, S5_H, 2 * S5_P)).reshape(2 * S5_P, 2 * S5_P).T
        return jnp.concatenate([t8, t8], axis=1)

    cpows = _squarings(col256(lr), col256(li), 4)
    row = lax.broadcasted_iota(jnp.int32, (2 * S5_P, TH), 0)
    lane_w = lax.broadcasted_iota(jnp.int32, (2 * S5_P, TH), 1)
    t_idx = chunk_pos(lane_w)
    j_idx = lane_w >> H_SHIFT
    is_f = row < S5_P
    ctr = tiled_t(cn_re[gi])
    cti = tiled_t(cn_im[gi])
    er, ei = _cpow(cpows, jnp.where(is_f, t_idx, S5_T - 1 - t_idx))
    er, ei = er * cpows[0][0] - ei * cpows[0][1], er * cpows[0][1] + ei * cpows[0][0]
    wr = ctr * er - cti * ei
    wi = ctr * ei + cti * er
    wout_ref[gi, 0:128, :] = wr.astype(BF16)
    wout_ref[gi, 128:256, :] = (-wi).astype(BF16)
    kr, ki = _cpow(cpows, jnp.where(is_f, j_idx, S5_T - 1 - j_idx))
    ekr = ctr * kr - cti * ki
    eki = ctr * ki + cti * kr
    lane16 = lax.broadcasted_iota(jnp.int32, (S5_H, 128), 1)
    mf = lane16 < S5_P
    kkf = dot(jnp.where(mf, bbr, 0.0), ekr) - dot(jnp.where(mf, bbi, 0.0), eki)
    kkb = dot(jnp.where(mf, 0.0, bbr), ekr) - dot(jnp.where(mf, 0.0, bbi), eki)
    dl = d_ref[gi]
    r16 = lax.broadcasted_iota(jnp.int32, (S5_H, TH), 0)
    l256 = lax.broadcasted_iota(jnp.int32, (S5_H, TH), 1)
    rot = bg * S5_H
    for s in range(S5_T):
        blk = _shift_lanes(kkf, S5_H * s) + _unshift_lanes(kkb, S5_H * (S5_T - 1 - s))
        blk = blk + jnp.where(l256 == r16 + S5_H * s, dl, 0.0)
        blk = jnp.concatenate([pltpu.roll(blk[:, :128], rot, axis=1), pltpu.roll(blk[:, 128:], rot, axis=1)], axis=1)
        rho = ((s + bg) & (BLK - 1)) + (s & BLK)
        mix_ref[gi, pl.ds(pl.multiple_of(rho * S5_H, S5_H), S5_H), :] = blk.astype(BF16)


def _s5w_kernel(*refs):
    for gi in range(S5W_GROUPS):
        bg = (pl.program_id(0) * S5W_GROUPS + gi) & (BLK - 1)
        _s5w_group(gi, bg, *refs)


def _s5_weights(lrow_re, lrow_im, crow_re, crow_im, bt_re, bt_im, cn_re, cn_im, d_row):
    TH = S5_T * S5_H
    g3 = lambda r, c: pl.BlockSpec((S5W_GROUPS, r, c), lambda g: (g, 0, 0))
    wshape = jax.ShapeDtypeStruct((S5_G, TH, TH), BF16)
    return pl.pallas_call(
        _s5w_kernel,
        out_shape=(wshape, wshape, wshape, jax.ShapeDtypeStruct((S5_G, 8, 128), F32)),
        grid=(S5_G // S5W_GROUPS,),
        in_specs=[g3(1, 128)] * 4 + [g3(S5_H, 128)] * 4 + [g3(1, TH)],
        out_specs=(g3(TH, TH), g3(TH, TH), g3(TH, TH), g3(8, 128)),
        compiler_params=_params("arbitrary"),
        name="s5_weights",
    )(lrow_re, lrow_im, crow_re, crow_im, bt_re, bt_im, cn_re, cn_im, d_row)


def _rot_blocks(v, r):
    cols = [pltpu.roll(v[:, 128 * q:128 * (q + 1)], S5_H * r, axis=1) for q in range(v.shape[1] // 128)]
    return jnp.concatenate(cols, axis=1)


def _slabs_of(h, hs_ref):
    h3 = h.reshape(h.shape[0] // S5_T, S5_T, h.shape[1])
    for s in range(S5_T):
        hs_ref[s] = h3[:, s, :].astype(BF16)


PERM_ROWS = S5_T * S5_T


def _chunk_transpose_perm():
    ri = lax.broadcasted_iota(jnp.int32, (PERM_ROWS, PERM_ROWS), 0)
    ci = lax.broadcasted_iota(jnp.int32, (PERM_ROWS, PERM_ROWS), 1)
    hit = ((ri >> H_SHIFT) == (ci & (S5_T - 1))) & ((ri & (S5_T - 1)) == (ci >> H_SHIFT))
    return jnp.where(hit, 1.0, 0.0).astype(BF16)


def _inproj0n_kernel(x_ref, mod_ref, w_ref, lng_ref, lnb_ref, guz_ref, vln_ref, hs_ref):
    shift = mod_ref[:, 0:D]
    scale = mod_ref[:, D:2 * D]
    hb = (x_ref[...] * (1.0 + scale) + shift).astype(BF16)
    perm = _chunk_transpose_perm()
    for j in range(hb.shape[0] // PERM_ROWS):
        blk = jnp.dot(perm, hb[PERM_ROWS * j:PERM_ROWS * (j + 1), :], preferred_element_type=F32).astype(BF16)
        for s in range(S5_T):
            hs_ref[s, S5_T * j:S5_T * (j + 1), :] = blk[S5_T * s:S5_T * (s + 1), :]
    dot = lambda lo: jnp.dot(hb, w_ref[:, SGU_COL0 + lo:SGU_COL0 + lo + 512].astype(BF16),
                             preferred_element_type=F32)
    guz_ref[...] = (_gelu(dot(0)) * _silu(dot(1024))).astype(BF16)
    vln_ref[...] = _layer_norm(_gelu(dot(512)), lng_ref[...], lnb_ref[...]).astype(BF16)


def _inproj0n(x, mod, w_in_f32, ln_g, ln_b, tm=TOKEN_TILE):
    nct = tm // S5_T
    o = jax.ShapeDtypeStruct((B, L, 512), BF16)
    ospec = pl.BlockSpec((None, tm, 512), lambda b, i: (b, i, 0))
    full = lambda *s: pl.BlockSpec(s, lambda b, i: (0,) * len(s))
    return pl.pallas_call(
        _inproj0n_kernel,
        out_shape=(o, o, jax.ShapeDtypeStruct((S5_T, B * N_CHUNK, D), BF16)),
        grid=(B, L // tm),
        in_specs=[pl.BlockSpec((None, tm, D), lambda b, i: (b, i, 0)),
                  pl.BlockSpec((None, 1, 3 * D), lambda b, i: (b, 0, 0)),
                  pl.BlockSpec((D, EVEN_IN), lambda b, i: (0, 0), pipeline_mode=pl.Buffered(1)),
                  full(1, 512), full(1, 512)],
        out_specs=(ospec, ospec,
                   pl.BlockSpec((S5_T, nct, D), lambda b, i: (0, b * (N_CHUNK // nct) + i, 0))),
        compiler_params=_params("arbitrary", "arbitrary"),
        name="inproj0n",
    )(x, mod, w_in_f32, ln_g, ln_b)


def _ctx_slabs_kernel(x_ref, mod_ref, hs_ref):
    h = x_ref[...] * (1.0 + mod_ref[:, D:2 * D]) + mod_ref[:, 0:D]
    _slabs_of(h, hs_ref)


def _ctx_slabs(ctx, mod_c):
    return pl.pallas_call(
        _ctx_slabs_kernel,
        out_shape=jax.ShapeDtypeStruct((S5_T, B * N_CCHUNK, D), BF16),
        grid=(B,),
        in_specs=[pl.BlockSpec((None, CTX, D), lambda b: (b, 0, 0)),
                  pl.BlockSpec((1, 3 * D), lambda b: (0, 0))],
        out_specs=pl.BlockSpec((S5_T, N_CCHUNK, D), lambda b: (0, b, 0)),
        compiler_params=_params("arbitrary"),
        name="ctx_slabs",
    )(ctx, mod_c)


def _inproj0a_kernel(hs_ref, hc_ref, w_ref, ua_ref, sza_ref, uc_ref):
    r = pl.program_id(0)
    h = hs_ref[...]
    w_ua = w_ref[:, 0:512].astype(BF16)
    ua_ref[...] = _rot_blocks(jnp.dot(h, w_ua, preferred_element_type=F32), r).astype(BF16)
    sza_ref[...] = _silu(jnp.dot(h, w_ref[:, 512:1024].astype(BF16), preferred_element_type=F32)).astype(BF16)
    uc_ref[...] = _rot_blocks(jnp.dot(hc_ref[...], w_ua, preferred_element_type=F32), r).astype(BF16)


def _inproj0a(hs, hcs, w_in_f32):
    slab = lambda r, h: r + BLK * h
    sspec = lambda n, w: pl.BlockSpec((None, n, w), lambda r, h: (slab(r, h), 0, 0))
    so = lambda n: jax.ShapeDtypeStruct((S5_T, n, 512), BF16)
    nl, ncx = B * N_CHUNK, B * N_CCHUNK
    return pl.pallas_call(
        _inproj0a_kernel,
        out_shape=(so(nl), so(nl), so(ncx)),
        grid=(BLK, S5_T // BLK),
        in_specs=[sspec(nl, D), sspec(ncx, D), pl.BlockSpec((D, SGU_COL0), lambda r, h: (0, 0))],
        out_specs=(sspec(nl, 512), sspec(nl, 512), sspec(ncx, 512)),
        compiler_params=_params("arbitrary", "arbitrary"),
        name="inproj0a",
    )(hs, hcs, w_in_f32)


SCAN_GROUPS = 4


def _scan_tiles(sre_ref, sim_ref, h_refs, n_tiles, carry, lams):
    row = lax.broadcasted_iota(jnp.int32, (8, 128), 0)
    lane = lax.broadcasted_iota(jnp.int32, (8, 128), 1)
    first = row < B
    fwd = lane < S5_P

    def body(k, c):
        of = pl.multiple_of(k * 8, 8)
        ob = pl.multiple_of((n_tiles - 1 - k) * 8, 8)
        out = []
        for gi in range(SCAN_GROUPS):
            lre, lim = lams[gi]
            hr, hi = c[2 * gi], c[2 * gi + 1]
            sr = jnp.where(fwd, sre_ref[gi, pl.ds(of, 8), :], pltpu.roll(sre_ref[gi, pl.ds(ob, 8), :], B, axis=0))
            si = jnp.where(fwd, sim_ref[gi, pl.ds(of, 8), :], pltpu.roll(sim_ref[gi, pl.ds(ob, 8), :], B, axis=0))
            h1r = lre * hr - lim * hi + sr
            h1i = lre * hi + lim * hr + si
            r1r = pltpu.roll(h1r, B, axis=0)
            r1i = pltpu.roll(h1i, B, axis=0)
            if h_refs is not None:
                fre_ref, fim_ref, bre_ref, bim_ref = h_refs
                er = jnp.where(first, hr, r1r)
                ei = jnp.where(first, hi, r1i)
                fre_ref[gi, pl.ds(of, 8), :] = er
                fim_ref[gi, pl.ds(of, 8), :] = ei
                bre_ref[gi, pl.ds(ob, 8), :] = pltpu.roll(er, B, axis=0)
                bim_ref[gi, pl.ds(ob, 8), :] = pltpu.roll(ei, B, axis=0)
            h2r = lre * r1r - lim * r1i + sr
            h2i = lre * r1i + lim * r1r + si
            out.append(jnp.where(first, pltpu.roll(h2r, B, axis=0), h2r))
            out.append(jnp.where(first, pltpu.roll(h2i, B, axis=0), h2i))
        return tuple(out)

    return lax.fori_loop(0, n_tiles, body, carry)


def _gather_group(slab_ref, src):
    halves = []
    for h in range(S5_T // BLK):
        acc = slab_ref[BLK * h]
        for s in range(1, BLK):
            acc = jnp.where(src == s, slab_ref[BLK * h + s], acc)
        halves.append(acc)
    return jnp.concatenate(halves, axis=1)


def _s5core_kernel(ul_ref, uc_ref, win_ref, wout_ref, mix_ref, l16_ref, o_ref,
                   u_ref, sre_ref, sim_ref, cre_ref, cim_ref, fre_ref, fim_ref, bre_ref, bim_ref, y_ref):
    nl = N_CHUNK * B
    ncx = N_CCHUNK * B
    blk_l = lax.broadcasted_iota(jnp.int32, (nl, 128), 1) >> H_SHIFT
    blk_c = lax.broadcasted_iota(jnp.int32, (ncx, 128), 1) >> H_SHIFT
    fwd = lax.broadcasted_iota(jnp.int32, (N_CHUNK, 128), 1) < S5_P
    for g0 in range(0, BLK, SCAN_GROUPS):
        for gi in range(SCAN_GROUPS):
            bg = g0 + gi
            win = win_ref[bg]
            src_l = ((blk_l - bg) & (BLK - 1)).astype(F32).astype(BF16)
            src_c = ((blk_c - bg) & (BLK - 1)).astype(F32).astype(BF16)
            u = _gather_group(ul_ref, src_l)
            u_ref[gi] = u
            sl = jnp.dot(u, win, preferred_element_type=F32)
            sc = jnp.dot(_gather_group(uc_ref, src_c), win, preferred_element_type=F32)
            for b in range(B):
                sre_ref[gi, pl.ds(b, N_CHUNK, stride=B), :] = sl[N_CHUNK * b:N_CHUNK * (b + 1), 0:128]
                sim_ref[gi, pl.ds(b, N_CHUNK, stride=B), :] = sl[N_CHUNK * b:N_CHUNK * (b + 1), 128:256]
                cre_ref[gi, pl.ds(b, N_CCHUNK, stride=B), :] = sc[N_CCHUNK * b:N_CCHUNK * (b + 1), 0:128]
                cim_ref[gi, pl.ds(b, N_CCHUNK, stride=B), :] = sc[N_CCHUNK * b:N_CCHUNK * (b + 1), 128:256]
        lams = [(jnp.broadcast_to(l16_ref[g0 + gi, 0:1, :], (8, 128)),
                 jnp.broadcast_to(l16_ref[g0 + gi, 1:2, :], (8, 128))) for gi in range(SCAN_GROUPS)]
        zero = tuple(jnp.zeros((8, 128), F32) for _ in range(2 * SCAN_GROUPS))
        carry = _scan_tiles(cre_ref, cim_ref, None, ncx // 8, zero, lams)
        _scan_tiles(sre_ref, sim_ref, (fre_ref, fim_ref, bre_ref, bim_ref), nl // 8, carry, lams)
        for gi in range(SCAN_GROUPS):
            bg = g0 + gi
            y = jnp.dot(u_ref[gi], mix_ref[bg], preferred_element_type=F32)
            hs = []
            for b in range(B):
                rows = pl.ds(b, N_CHUNK, stride=B)
                hs.append(jnp.concatenate([jnp.where(fwd, fre_ref[gi, rows, :], bre_ref[gi, rows, :]),
                                           jnp.where(fwd, fim_ref[gi, rows, :], bim_ref[gi, rows, :])], axis=1))
            hcat = jnp.concatenate(hs, axis=0).astype(BF16)
            y = y + jnp.dot(hcat, wout_ref[bg], preferred_element_type=F32)
            y_ref[bg] = y.astype(BF16)

    blk = blk_l.astype(F32).astype(BF16)
    for s in range(S5_T):
        h, r = s // BLK, s % BLK
        acc = None
        for j in range(BLK):
            piece = y_ref[(j - r) % BLK, :, 128 * h:128 * (h + 1)]
            acc = piece if acc is None else jnp.where(blk == j, piece, acc)
        o_ref[s] = acc


def _s5core(ul, uc, win, wout, mix, l16):
    TH = S5_T * S5_H
    nl = N_CHUNK * B
    ncx = N_CCHUNK * B
    g4 = lambda r, c: pl.BlockSpec((BLK, r, c), lambda q: (q, 0, 0))
    col = lambda n: pl.BlockSpec((S5_T, n, 128), lambda q: (0, 0, q))
    f32s = lambda n: pltpu.VMEM((SCAN_GROUPS, n, 128), F32)
    return pl.pallas_call(
        _s5core_kernel,
        out_shape=jax.ShapeDtypeStruct((S5_T, nl, S5_W), BF16),
        grid=(S5_G // BLK,),
        in_specs=[col(nl), col(ncx), g4(TH, TH), g4(TH, TH), g4(TH, TH), g4(8, 128)],
        out_specs=col(nl),
        scratch_shapes=[pltpu.VMEM((SCAN_GROUPS, nl, TH), BF16),
                        f32s(nl), f32s(nl), f32s(ncx), f32s(ncx), f32s(nl), f32s(nl), f32s(nl), f32s(nl),
                        pltpu.VMEM((BLK, nl, TH), BF16)],
        compiler_params=_params("arbitrary"),
        name="s5core",
    )(ul, uc, win, wout, mix, l16)


def _s5tail_kernel(slat_ref, sza_ref, gluw_ref, glub_ref, wtop_ref, y_ref):
    unrot = (BLK - pl.program_id(0)) & (BLK - 1)
    for b in range(B):
        rows = slice(N_CHUNK * b, N_CHUNK * (b + 1))
        g = _gelu(_rot_blocks(slat_ref[rows, :].astype(F32), unrot))
        gate = _sigmoid(jnp.dot(g.astype(BF16), gluw_ref[...], preferred_element_type=F32) + glub_ref[...])
        a = (g * gate * sza_ref[rows, :].astype(F32)).astype(BF16)
        y_ref[rows, :] = jnp.dot(a, wtop_ref[...], preferred_element_type=F32).astype(BF16)


def _s5tail(slat, sza, glu_w, glu_b, w_top):
    slab = lambda r, h: r + BLK * h
    sspec = lambda w: pl.BlockSpec((None, N_CHUNK * B, w), lambda r, h: (slab(r, h), 0, 0))
    full = lambda *s: pl.BlockSpec(s, lambda r, h: (0,) * len(s))
    return pl.pallas_call(
        _s5tail_kernel,
        out_shape=jax.ShapeDtypeStruct((S5_T, N_CHUNK * B, D), BF16),
        grid=(BLK, S5_T // BLK),
        in_specs=[sspec(512), sspec(512), full(512, 512), full(1, 512), full(S5_W, D)],
        out_specs=sspec(D),
        compiler_params=_params("arbitrary", "arbitrary"),
        name="s5tail",
    )(slat, sza, glu_w, glu_b, w_top)


def _tail0_kernel(x_ref, ys5_ref, guz_ref, vln_ref, mod_ref, sguw_ref, sgub_ref, wbot_ref, ng_ref, nb_ref, o_ref):
    tm = x_ref.shape[0]
    lane = lax.broadcasted_iota(jnp.int32, (SGU_CHUNK, 128), 1)
    lo = lane < SGU_HD
    zero = jnp.zeros((SGU_CHUNK, 128), BF16)
    chunks = []
    for ci in range(tm // SGU_CHUNK):
        v = vln_ref[ci * SGU_CHUNK:(ci + 1) * SGU_CHUNK, :]
        cols = []
        for pi in range(SGU_HEADS // 2):
            vp = v[:, 128 * pi:128 * (pi + 1)]
            bm = jnp.concatenate([jnp.where(lo, vp, zero), jnp.where(lo, zero, vp)], axis=0)
            cols.append(jnp.dot(sguw_ref[pi], bm, preferred_element_type=F32))
        chunks.append(jnp.concatenate(cols, axis=1) + sgub_ref[...])
    s = jnp.concatenate(chunks, axis=0)
    bsg = (guz_ref[...].astype(F32) * s).astype(BF16)
    perm = _chunk_transpose_perm()
    ys5 = jnp.concatenate(
        [jnp.dot(perm, ys5_ref[:, S5_T * j:S5_T * (j + 1), :].reshape(PERM_ROWS, D), preferred_element_type=F32)
         for j in range(tm // PERM_ROWS)], axis=0)
    y = ys5 + jnp.dot(bsg, wbot_ref[...], preferred_element_type=F32)
    gmod = mod_ref[:, 2 * D:3 * D]
    o_ref[...] = _layer_norm(DN_ALPHA * x_ref[...] + gmod * y, ng_ref[...], nb_ref[...])


def _tail0(x, ys5, guz, vln, mod, sguw, sgub, w_bot, ng, nb, tm=TOKEN_TILE):
    nct = tm // S5_T
    t512 = pl.BlockSpec((None, tm, 512), lambda b, i: (b, i, 0))
    tD = pl.BlockSpec((None, tm, D), lambda b, i: (b, i, 0))
    full = lambda *s: pl.BlockSpec(s, lambda b, i: (0,) * len(s))
    return pl.pallas_call(
        _tail0_kernel,
        out_shape=jax.ShapeDtypeStruct((B, L, D), F32),
        grid=(B, L // tm),
        in_specs=[tD, pl.BlockSpec((S5_T, nct, D), lambda b, i: (0, b * (N_CHUNK // nct) + i, 0)), t512, t512,
                  pl.BlockSpec((None, 1, 3 * D), lambda b, i: (b, 0, 0)),
                  full(SGU_HEADS // 2, SGU_CHUNK, 256), full(SGU_CHUNK, 512),
                  pl.BlockSpec((SGU_W, D), lambda b, i: (1, 0)), full(1, D), full(1, D)],
        out_specs=tD,
        compiler_params=_params("arbitrary", "arbitrary"),
        name="tail0",
    )(x, ys5, guz, vln, mod, sguw, sgub, w_bot, ng, nb)


CONV_C = D // 2
TILE_ROWS = TOKEN_TILE // GRID_W


def _grid_transpose_in(v, o_ref):
    perm = _chunk_transpose_perm()
    for q in range(GRID_W // S5_T):
        seg = jnp.concatenate([v[GRID_W * r + S5_T * q:GRID_W * r + S5_T * (q + 1), :] for r in range(TILE_ROWS)],
                              axis=0)
        t = jnp.dot(perm, seg, preferred_element_type=F32).astype(BF16)
        o_ref[S5_T * q:S5_T * (q + 1), :, :] = t.reshape(S5_T, TILE_ROWS, v.shape[1])


def _inproj1_kernel(x_ref, mod_ref, w_ref, hgr_ref, hgc_ref, h_ref):
    shift = mod_ref[:, 0:D]
    scale = mod_ref[:, D:2 * D]
    h = (x_ref[...] * (1.0 + scale) + shift).astype(BF16)
    h_ref[...] = h
    dot = lambda lo: jnp.dot(h, w_ref[:, lo:lo + CONV_C].astype(BF16), preferred_element_type=F32)
    hgr_ref[...] = (dot(0) * _sigmoid(dot(D))).astype(BF16)
    _grid_transpose_in((dot(CONV_C) * _sigmoid(dot(D + CONV_C))).astype(BF16), hgc_ref)


def _inproj1(x, mod, w_in_f32, tm=TOKEN_TILE):
    tile = lambda w: pl.BlockSpec((None, tm, w), lambda b, i: (b, i, 0))
    return pl.pallas_call(
        _inproj1_kernel,
        out_shape=(jax.ShapeDtypeStruct((B, L, CONV_C), BF16),
                   jax.ShapeDtypeStruct((B, GRID_W, GRID_W, CONV_C), BF16),
                   jax.ShapeDtypeStruct((B, L, D), BF16)),
        grid=(B, L // tm),
        in_specs=[tile(D),
                  pl.BlockSpec((None, 1, 3 * D), lambda b, i: (b, 0, 0)),
                  pl.BlockSpec((D, 2 * D), lambda b, i: (0, 0), pipeline_mode=pl.Buffered(1))],
        out_specs=(tile(CONV_C), pl.BlockSpec((None, GRID_W, TILE_ROWS, CONV_C), lambda b, i: (b, 0, i, 0)),
                   tile(D)),
        compiler_params=_params("arbitrary", "arbitrary"),
        name="inproj1",
    )(x, mod, w_in_f32)


DFT_N = 2 * GRID_W
TAPS_PAD = CONV_K + 1


def _dft_constants():
    th = 2.0 * math.pi / DFT_N
    f = jnp.arange(GRID_W, dtype=F32)[:, None]
    p = jnp.arange(GRID_W, dtype=F32)[None, :]
    cosm = jnp.cos(th * f * p)
    sinm = jnp.sin(th * f * p)
    alt = jnp.where(jnp.arange(GRID_W) % 2 == 0, 1.0, -1.0).astype(F32)
    fwd = jnp.concatenate([cosm, alt[None, :], sinm[1:]], axis=0)
    cf = jnp.where(jnp.arange(GRID_W) == 0, 1.0, 2.0).astype(F32) / DFT_N
    inv = jnp.concatenate([cosm.T * cf[None, :], (alt / DFT_N)[:, None], sinm.T[:, 1:] * (2.0 / DFT_N)], axis=1)
    sft = (CONV_HALF - jnp.arange(TAPS_PAD, dtype=F32))[None, :]
    live = (jnp.arange(TAPS_PAD) < CONV_K).astype(F32)[None, :]
    f64 = jnp.where(f == 0, float(GRID_W), f)
    return (fwd.astype(BF16), inv.astype(BF16),
            jnp.cos(th * f * sft) * live, jnp.sin(th * f * sft) * live, jnp.cos(th * f64 * sft) * live)


def _fconv_kernel(h_ref, w_ref, b_ref, fwd_ref, inv_ref, c1_ref, s3_ref, c4_ref, o_ref):
    hp = lax.Precision.HIGHEST
    taps = w_ref[...]
    g_re = jnp.dot(c1_ref[...], taps, preferred_element_type=F32, precision=hp)
    g_im = jnp.dot(s3_ref[...], taps, preferred_element_type=F32, precision=hp)
    g_r2 = jnp.dot(c4_ref[...], taps, preferred_element_type=F32, precision=hp)
    fwd = fwd_ref[...]
    inv = inv_ref[...]
    bias = b_ref[...]
    n_runs = h_ref.shape[0] // GRID_W
    rows = lambda r: slice(GRID_W * r, GRID_W * (r + 1))
    forward = lambda r: jnp.dot(fwd, h_ref[rows(r), :], preferred_element_type=F32)
    ahead = 2
    specs = [forward(r) for r in range(ahead)]
    for r in range(n_runs):
        if r + ahead < n_runs:
            specs.append(forward(r + ahead))
        spec = specs[r]
        a, bm = spec[0:GRID_W], spec[GRID_W:DFT_N]
        prod = jnp.concatenate([a * g_re - bm * g_im, a * g_im + bm * g_r2], axis=0).astype(BF16)
        o_ref[rows(r), :] = (jnp.dot(inv, prod, preferred_element_type=F32) + bias).astype(BF16)


def _fconv(h, taps, bias, consts, tm=L):
    fwd, inv, c1, s3, c4 = consts
    c = h.shape[-1]
    tile = pl.BlockSpec((None, tm, c), lambda b, i: (b, i, 0))
    full = lambda *s: pl.BlockSpec(s, lambda b, i: (0,) * len(s))
    return pl.pallas_call(
        _fconv_kernel,
        out_shape=jax.ShapeDtypeStruct(h.shape, BF16),
        grid=(B, L // tm),
        in_specs=[tile, full(TAPS_PAD, c), full(1, c), full(DFT_N, GRID_W), full(GRID_W, DFT_N),
                  full(GRID_W, TAPS_PAD), full(GRID_W, TAPS_PAD), full(GRID_W, TAPS_PAD)],
        out_specs=tile,
        compiler_params=_params("arbitrary", "arbitrary"),
        name="fconv",
    )(h, taps, bias, fwd, inv, c1, s3, c4)


ROW_BLOCK = 32


def _row_blocks(n_rows):
    return [slice(ROW_BLOCK * k, ROW_BLOCK * (k + 1)) for k in range(n_rows // ROW_BLOCK)]


def _tail1_kernel(x_ref, hcr_ref, hcc_ref, h1_ref, wz_ref, mod_ref, lng_ref, lnb_ref, wout_ref, ng_ref, nb_ref,
                  o_ref, col_ref, z_ref, m_ref, y_ref):
    tm = x_ref.shape[0]
    perm = _chunk_transpose_perm()
    for q in range(GRID_W // S5_T):
        blk = hcc_ref[S5_T * q:S5_T * (q + 1), :, :].reshape(PERM_ROWS, CONV_C)
        t = jnp.dot(perm, blk, preferred_element_type=F32)
        for r in range(TILE_ROWS):
            col_ref[GRID_W * r + S5_T * q:GRID_W * r + S5_T * (q + 1), :] = t[S5_T * r:S5_T * (r + 1), :]
    z_ref[...] = jnp.dot(h1_ref[...], wz_ref[...], preferred_element_type=F32)
    lng, lnb = lng_ref[...], lnb_ref[...]
    for rows in _row_blocks(tm):
        hc = jnp.concatenate([hcr_ref[rows, :].astype(F32), col_ref[rows, :]], axis=1)
        m_ref[rows, :] = (_silu_of_half(_layer_norm(hc, lng, lnb)) * _silu_of_half(z_ref[rows, :])).astype(BF16)
    y_ref[...] = jnp.dot(m_ref[...], wout_ref[...], preferred_element_type=F32)
    gmod = mod_ref[:, 2 * D:3 * D]
    ng, nb = ng_ref[...], nb_ref[...]
    for rows in _row_blocks(tm):
        o_ref[rows, :] = _layer_norm(DN_ALPHA * x_ref[rows, :] + gmod * y_ref[rows, :], ng, nb)


def _tail1(x, hc_row, hc_col, h1, w_z_half, mod, ln_g_half, ln_b_half, w_out, ng, nb, tm=TOKEN_TILE):
    tile = lambda w: pl.BlockSpec((None, tm, w), lambda b, i: (b, i, 0))
    full = lambda *s: pl.BlockSpec(s, lambda b, i: (0,) * len(s))
    return pl.pallas_call(
        _tail1_kernel,
        out_shape=jax.ShapeDtypeStruct((B, L, D), F32),
        grid=(B, L // tm),
        in_specs=[tile(D), tile(CONV_C),
                  pl.BlockSpec((None, GRID_W, TILE_ROWS, CONV_C), lambda b, i: (b, 0, i, 0)),
                  tile(D), full(D, D), pl.BlockSpec((None, 1, 3 * D), lambda b, i: (b, 0, 0)),
                  full(1, D), full(1, D), full(D, D), full(1, D), full(1, D)],
        out_specs=tile(D),
        scratch_shapes=[pltpu.VMEM((tm, CONV_C), F32), pltpu.VMEM((tm, D), F32), pltpu.VMEM((tm, D), BF16),
                        pltpu.VMEM((tm, D), F32)],
        compiler_params=_params("arbitrary", "arbitrary"),
        name="tail1",
    )(x, hc_row, hc_col, h1, w_z_half, mod, ln_g_half, ln_b_half, w_out, ng, nb)


def kernel(x, c, ctx, c_ctx, mod_w, mod_b, norm_g, norm_b, ev_w_in, ev_w_out, s5_lam_re, s5_lam_im, s5_log_dt, s5_b_re, s5_b_im, s5_c_re, s5_c_im, s5_d, glu_w, glu_b, sgu_ln_g, sgu_ln_b, sgu_w, sgu_b, od_w_in, od_w_out, dw_w, dw_b, conv_ln_g, conv_ln_b):
    TH = S5_T * S5_H
    row = lambda v: v.reshape(1, -1)

    cond8 = jnp.concatenate([c, c_ctx[None], jnp.zeros((3, D), F32)], axis=0)
    mods = _adaln(cond8, mod_w, mod_b)
    mod0 = mods[0, :B].reshape(B, 1, 3 * D)
    mod0c = mods[0, B:B + 1]
    mod1 = mods[1, :B].reshape(B, 1, 3 * D)

    lbr, lbi, cfr, cfi = _discretise(s5_lam_re[0], s5_lam_im[0], s5_log_dt[0])
    rowcat = lambda a: jnp.concatenate([a[0], a[1]], axis=-1).reshape(S5_G, 1, 2 * S5_P)
    bt = lambda a: jnp.concatenate([jnp.swapaxes(a[0], 1, 2), jnp.swapaxes(a[1], 1, 2)], axis=-1)
    cn = lambda a: jnp.concatenate([a[0], a[1]], axis=-1)
    d_row = jnp.tile(s5_d[0].reshape(S5_G, 1, S5_H), (1, 1, S5_T))
    win, wout, mix, l16 = _s5_weights(rowcat(lbr), rowcat(lbi), rowcat(cfr), rowcat(cfi),
                                      bt(s5_b_re[0]), bt(s5_b_im[0]), cn(s5_c_re[0]), cn(s5_c_im[0]), d_row)

    w_out0 = ev_w_out[0].astype(BF16)
    glu_w0 = glu_w[0].astype(BF16)
    guz, vln, hs = _inproj0n(x, mod0, ev_w_in[0], row(sgu_ln_g[0]), row(sgu_ln_b[0]))
    ua, sza, ua_c = _inproj0a(hs, _ctx_slabs(ctx, mod0c), ev_w_in[0])
    s_lat = _s5core(ua, ua_c, win, wout, mix, l16)
    y_s5 = _s5tail(s_lat, sza, glu_w0, row(glu_b[0]), w_out0)
    sguw = sgu_w[0].reshape(SGU_HEADS // 2, 2, SGU_CHUNK, SGU_CHUNK)
    sguw = jnp.transpose(sguw, (0, 2, 1, 3)).reshape(SGU_HEADS // 2, SGU_CHUNK, 2 * SGU_CHUNK).astype(BF16)
    sgub = jnp.repeat(sgu_b[0].T, SGU_HD, axis=1)
    x1 = _tail0(x, y_s5, guz, vln, mod0, sguw, sgub, w_out0, row(norm_g[0]), row(norm_b[0]))

    hg_row, hg_col, h1 = _inproj1(x1, mod1, od_w_in[0])
    consts = _dft_constants()
    taps = jnp.pad(dw_w[0], ((0, TAPS_PAD - CONV_K), (0, 0)))
    bias = row(dw_b[0])
    hc_row = _fconv(hg_row, taps[:, :CONV_C], bias[:, :CONV_C], consts)
    hc_col = _fconv(hg_col.reshape(B, L, CONV_C), taps[:, CONV_C:], bias[:, CONV_C:], consts)
    w_z_half = (0.5 * od_w_in[0][:, 2 * D:]).astype(BF16)
    return _tail1(x1, hc_row, hc_col.reshape(B, GRID_W, GRID_W, CONV_C), h1, w_z_half, mod1,
                  row(0.5 * conv_ln_g[0]), row(0.5 * conv_ln_b[0]), od_w_out[0].astype(BF16),
                  row(norm_g[1]), row(norm_b[1]))
```

```python
import functools
import math

import jax
import jax.numpy as jnp
from jax import lax
from jax.experimental import pallas as pl
from jax.experimental.pallas import tpu as pltpu

D = 1024
B = 4
L = 4096
CTX = 256
GRID_W = 64
S5_W = 512
S5_G = 32
S5_H = 16
H_SHIFT = 4
BLK = 128 // S5_H
S5_P = 64
S5_T = 16
SGU_W = 512
SGU_HEADS = 8
SGU_HD = 64
SGU_CHUNK = 128
CONV_K = 31
CONV_HALF = CONV_K // 2
EVEN_IN = 2560
SGU_COL0 = 2 * S5_W
DEPTH = 2
DN_ALPHA = (2 * DEPTH) ** 0.25
LN_EPS = 1e-5
N_CHUNK = L // S5_T
N_CCHUNK = CTX // S5_T
VMEM_LIMIT_V7X = 56 * 1024 * 1024
TOKEN_TILE = 1024

F32 = jnp.float32
BF16 = jnp.bfloat16


GELU_C = math.sqrt(2.0 / math.pi)


def _gelu(x):
    hx = 0.5 * x
    return hx * jnp.tanh(x * ((x * x) * (0.044715 * GELU_C) + GELU_C)) + hx


def _sigmoid(x):
    return 0.5 * jnp.tanh(0.5 * x) + 0.5


def _silu_of_half(hx):
    return hx * jnp.tanh(hx) + hx


def _silu(x):
    return _silu_of_half(0.5 * x)


def _layer_norm(x, g, b):
    mu = jnp.mean(x, axis=-1, keepdims=True)
    xc = x - mu
    var = jnp.mean(xc * xc, axis=-1, keepdims=True)
    return xc * lax.rsqrt(var + LN_EPS) * g + b


def _params(*sem):
    return pltpu.CompilerParams(dimension_semantics=sem, vmem_limit_bytes=VMEM_LIMIT_V7X)


def _adaln_kernel(c_ref, w_ref, b_ref, o_ref):
    def split(v):
        hi = v.astype(BF16)
        return hi, (v - hi.astype(F32)).astype(BF16)

    a_hi, a_lo = split(_silu(c_ref[...]))
    w_hi, w_lo = split(w_ref[...])
    dot = functools.partial(jnp.dot, preferred_element_type=F32)
    o_ref[...] = dot(a_hi, w_hi) + dot(a_lo, w_hi) + dot(a_hi, w_lo) + b_ref[...]


def _adaln(cond8, mod_w, mod_b):
    tn = 1024
    return pl.pallas_call(
        _adaln_kernel,
        out_shape=jax.ShapeDtypeStruct((DEPTH, 8, 3 * D), F32),
        grid=(DEPTH, 3 * D // tn),
        in_specs=[pl.BlockSpec((8, D), lambda l, j: (0, 0)),
                  pl.BlockSpec((None, D, tn), lambda l, j: (l, 0, j)),
                  pl.BlockSpec((None, 1, tn), lambda l, j: (l, 0, j))],
        out_specs=pl.BlockSpec((None, 8, tn), lambda l, j: (l, 0, j)),
        compiler_params=_params("arbitrary", "arbitrary"),
        name="adaln",
    )(cond8, mod_w, mod_b.reshape(DEPTH, 1, 3 * D))


def _disc_kernel(lr_ref, li_ref, ldt_ref, obr_ref, obi_ref, ocr_ref, oci_ref):
    lr = lr_ref[...]
    li = li_ref[...]
    dt = jnp.exp(ldt_ref[...])
    mag = jnp.exp(lr * dt)
    br = mag * jnp.cos(li * dt)
    bi = mag * jnp.sin(li * dt)
    inv = 1.0 / (lr * lr + li * li)
    nr = br - 1.0
    obr_ref[...] = br
    obi_ref[...] = bi
    ocr_ref[...] = (nr * lr + bi * li) * inv
    oci_ref[...] = (bi * lr - nr * li) * inv


def _discretise(lam_re, lam_im, log_dt):
    shp = jax.ShapeDtypeStruct((2 * S5_G, S5_P), F32)
    ldt = jnp.broadcast_to(log_dt.reshape(2 * S5_G, 1), (2 * S5_G, S5_P))
    outs = pl.pallas_call(
        _disc_kernel, out_shape=(shp, shp, shp, shp), name="s5_discretise",
    )(lam_re.reshape(2 * S5_G, S5_P), lam_im.reshape(2 * S5_G, S5_P), ldt)
    return [o.reshape(2, S5_G, S5_P) for o in outs]


S5W_GROUPS = 4


def _cpow(base_pows, j):
    re = None
    im = None
    for k, (pr, pi) in enumerate(base_pows):
        bit = ((j >> k) & 1) == 1
        mr = jnp.where(bit, pr, 1.0)
        mi = jnp.where(bit, pi, 0.0)
        if re is None:
            re, im = mr, mi
        else:
            re, im = re * mr - im * mi, re * mi + im * mr
    return re, im


def _squarings(pr, pi, n):
    out = [(pr, pi)]
    for _ in range(n - 1):
        pr, pi = pr * pr - pi * pi, 2.0 * pr * pi
        out.append((pr, pi))
    return out


def _shift_lanes(x, n):
    lane = lax.broadcasted_iota(jnp.int32, (S5_H, 128), 1)
    lo, hi = x[:, :128], x[:, 128:]
    if n == 0:
        return x
    if n < 128:
        rlo = pltpu.roll(lo, n, axis=1)
        rhi = pltpu.roll(hi, n, axis=1)
        return jnp.concatenate([jnp.where(lane >= n, rlo, 0.0), jnp.where(lane >= n, rhi, rlo)], axis=1)
    m = n - 128
    rlo = lo if m == 0 else pltpu.roll(lo, m, axis=1)
    return jnp.concatenate([jnp.zeros_like(lo), jnp.where(lane >= m, rlo, 0.0)], axis=1)


def _unshift_lanes(x, n):
    lane = lax.broadcasted_iota(jnp.int32, (S5_H, 128), 1)
    lo, hi = x[:, :128], x[:, 128:]
    if n == 0:
        return x
    if n < 128:
        rlo = pltpu.roll(lo, 128 - n, axis=1)
        rhi = pltpu.roll(hi, 128 - n, axis=1)
        keep = lane < 128 - n
        return jnp.concatenate([jnp.where(keep, rlo, rhi), jnp.where(keep, rhi, 0.0)], axis=1)
    m = n - 128
    rhi = hi if m == 0 else pltpu.roll(hi, 128 - m, axis=1)
    return jnp.concatenate([jnp.where(lane < 128 - m, rhi, 0.0), jnp.zeros_like(lo)], axis=1)


def _s5w_group(gi, bg, lrow_re, lrow_im, crow_re, crow_im, bt_re, bt_im,
               cn_re, cn_im, d_ref, win_ref, wout_ref, mix_ref, l16_ref):
    TH = S5_T * S5_H

    def chunk_pos(idx):
        return (((idx >> H_SHIFT) - bg) & (BLK - 1)) + ((idx >> 7) << 3)

    lr = lrow_re[gi]
    li = lrow_im[gi]
    pows_row = _squarings(lr, li, 5)
    l16_ref[gi, 0:1, :] = pows_row[4][0]
    l16_ref[gi, 1:2, :] = pows_row[4][1]
    l16_ref[gi, 2:8, :] = jnp.zeros((6, 128), F32)
    cr = crow_re[gi]
    ci = crow_im[gi]
    btr = bt_re[gi]
    bti = bt_im[gi]
    bbr = cr * btr - ci * bti
    bbi = cr * bti + ci * btr
    blk16 = lax.broadcasted_iota(jnp.int32, (S5_T, 128), 0)
    is_f16 = lax.broadcasted_iota(jnp.int32, (S5_T, 128), 1) < S5_P
    pos16 = chunk_pos(blk16 << H_SHIFT)
    pr16, pi16 = _cpow(pows_row[:4], jnp.where(is_f16, S5_T - 1 - pos16, pos16))
    rep_rows = lambda v: jnp.broadcast_to(v[:, None, :], (S5_T, S5_H, 128)).reshape(TH, 128)
    pr, pi = rep_rows(pr16), rep_rows(pi16)
    tbr = jnp.broadcast_to(bbr[None], (S5_T, S5_H, 128)).reshape(TH, 128)
    tbi = jnp.broadcast_to(bbi[None], (S5_T, S5_H, 128)).reshape(TH, 128)
    win_ref[gi, :, 0:128] = (pr * tbr - pi * tbi).astype(BF16)
    win_ref[gi, :, 128:256] = (pr * tbi + pi * tbr).astype(BF16)

    hp = lax.Precision.HIGHEST
    dot = functools.partial(jnp.dot, preferred_element_type=F32, precision=hp)
    def col256(r):
        col = jnp.broadcast_to(r, (2 * S5_P, 2 * S5_P)).T
        return jnp.concatenate([col, col], axis=1)

    def tiled_t(cn):
        t8 = jnp.broadcast_to(cn[None], (BLK, S5_H, 2 * S5_P)).reshape(2 * S5_P, 2 * S5_P).T
        return jnp.concatenate([t8, t8], axis=1)

    cpows = _squarings(col256(lr), col256(li), 4)
    row = lax.broadcasted_iota(jnp.int32, (2 * S5_P, TH), 0)
    lane_w = lax.broadcasted_iota(jnp.int32, (2 * S5_P, TH), 1)
    t_idx = chunk_pos(lane_w)
    j_idx = lane_w >> H_SHIFT
    is_f = row < S5_P
    ctr = tiled_t(cn_re[gi])
    cti = tiled_t(cn_im[gi])
    er, ei = _cpow(cpows, jnp.where(is_f, t_idx, S5_T - 1 - t_idx))
    er, ei = er * cpows[0][0] - ei * cpows[0][1], er * cpows[0][1] + ei * cpows[0][0]
    wr = ctr * er - cti * ei
    wi = ctr * ei + cti * er
    wout_ref[gi, 0:128, :] = wr.astype(BF16)
    wout_ref[gi, 128:256, :] = (-wi).astype(BF16)
    kr, ki = _cpow(cpows, jnp.where(is_f, j_idx, S5_T - 1 - j_idx))
    ekr = ctr * kr - cti * ki
    eki = ctr * ki + cti * kr
    lane16 = lax.broadcasted_iota(jnp.int32, (S5_H, 128), 1)
    mf = lane16 < S5_P
    kkf = dot(jnp.where(mf, bbr, 0.0), ekr) - dot(jnp.where(mf, bbi, 0.0), eki)
    kkb = dot(jnp.where(mf, 0.0, bbr), ekr) - dot(jnp.where(mf, 0.0, bbi), eki)
    dl = d_ref[gi]
    r16 = lax.broadcasted_iota(jnp.int32, (S5_H, TH), 0)
    l256 = lax.broadcasted_iota(jnp.int32, (S5_H, TH), 1)
    rot = bg * S5_H
    for s in range(S5_T):
        blk = _shift_lanes(kkf, S5_H * s) + _unshift_lanes(kkb, S5_H * (S5_T - 1 - s))
        blk = blk + jnp.where(l256 == r16 + S5_H * s, dl, 0.0)
        blk = jnp.concatenate([pltpu.roll(blk[:, :128], rot, axis=1), pltpu.roll(blk[:, 128:], rot, axis=1)], axis=1)
        rho = ((s + bg) & (BLK - 1)) + (s & BLK)
        mix_ref[gi, pl.ds(pl.multiple_of(rho * S5_H, S5_H), S5_H), :] = blk.astype(BF16)


def _s5w_kernel(*refs):
    for gi in range(S5W_GROUPS):
        bg = (pl.program_id(0) * S5W_GROUPS + gi) & (BLK - 1)
        _s5w_group(gi, bg, *refs)


def _s5_weights(lrow_re, lrow_im, crow_re, crow_im, bt_re, bt_im, cn_re, cn_im, d_row):
    TH = S5_T * S5_H
    g3 = lambda r, c: pl.BlockSpec((S5W_GROUPS, r, c), lambda g: (g, 0, 0))
    wshape = jax.ShapeDtypeStruct((S5_G, TH, TH), BF16)
    return pl.pallas_call(
        _s5w_kernel,
        out_shape=(wshape, wshape, wshape, jax.ShapeDtypeStruct((S5_G, 8, 128), F32)),
        grid=(S5_G // S5W_GROUPS,),
        in_specs=[g3(1, 128)] * 4 + [g3(S5_H, 128)] * 4 + [g3(1, TH)],
        out_specs=(g3(TH, TH), g3(TH, TH), g3(TH, TH), g3(8, 128)),
        compiler_params=_params("arbitrary"),
        name="s5_weights",
    )(lrow_re, lrow_im, crow_re, crow_im, bt_re, bt_im, cn_re, cn_im, d_row)


def _rot_blocks(v, r):
    cols = [pltpu.roll(v[:, 128 * q:128 * (q + 1)], S5_H * r, axis=1) for q in range(v.shape[1] // 128)]
    return jnp.concatenate(cols, axis=1)


def _slabs_of(h, hs_ref):
    h3 = h.reshape(h.shape[0] // S5_T, S5_T, h.shape[1])
    for s in range(S5_T):
        hs_ref[s] = h3[:, s, :].astype(BF16)


PERM_ROWS = S5_T * S5_T


def _chunk_transpose_perm():
    ri = lax.broadcasted_iota(jnp.int32, (PERM_ROWS, PERM_ROWS), 0)
    ci = lax.broadcasted_iota(jnp.int32, (PERM_ROWS, PERM_ROWS), 1)
    hit = ((ri >> H_SHIFT) == (ci & (S5_T - 1))) & ((ri & (S5_T - 1)) == (ci >> H_SHIFT))
    return jnp.where(hit, 1.0, 0.0).astype(BF16)


def _inproj0n_kernel(x_ref, mod_ref, w_ref, lng_ref, lnb_ref, guz_ref, vln_ref, hs_ref):
    shift = mod_ref[:, 0:D]
    scale = mod_ref[:, D:2 * D]
    hb = (x_ref[...] * (1.0 + scale) + shift).astype(BF16)
    perm = _chunk_transpose_perm()
    for j in range(hb.shape[0] // PERM_ROWS):
        blk = jnp.dot(perm, hb[PERM_ROWS * j:PERM_ROWS * (j + 1), :], preferred_element_type=F32).astype(BF16)
        for s in range(S5_T):
            hs_ref[s, S5_T * j:S5_T * (j + 1), :] = blk[S5_T * s:S5_T * (s + 1), :]
    dot = lambda lo: jnp.dot(hb, w_ref[:, SGU_COL0 + lo:SGU_COL0 + lo + 512].astype(BF16),
                             preferred_element_type=F32)
    guz_ref[...] = (_gelu(dot(0)) * _silu(dot(1024))).astype(BF16)
    vln_ref[...] = _layer_norm(_gelu(dot(512)), lng_ref[...], lnb_ref[...]).astype(BF16)


def _inproj0n(x, mod, w_in_f32, ln_g, ln_b, tm=TOKEN_TILE):
    nct = tm // S5_T
    o = jax.ShapeDtypeStruct((B, L, 512), BF16)
    ospec = pl.BlockSpec((None, tm, 512), lambda b, i: (b, i, 0))
    full = lambda *s: pl.BlockSpec(s, lambda b, i: (0,) * len(s))
    return pl.pallas_call(
        _inproj0n_kernel,
        out_shape=(o, o, jax.ShapeDtypeStruct((S5_T, B * N_CHUNK, D), BF16)),
        grid=(B, L // tm),
        in_specs=[pl.BlockSpec((None, tm, D), lambda b, i: (b, i, 0)),
                  pl.BlockSpec((None, 1, 3 * D), lambda b, i: (b, 0, 0)),
                  pl.BlockSpec((D, EVEN_IN), lambda b, i: (0, 0), pipeline_mode=pl.Buffered(1)),
                  full(1, 512), full(1, 512)],
        out_specs=(ospec, ospec,
                   pl.BlockSpec((S5_T, nct, D), lambda b, i: (0, b * (N_CHUNK // nct) + i, 0))),
        compiler_params=_params("arbitrary", "arbitrary"),
        name="inproj0n",
    )(x, mod, w_in_f32, ln_g, ln_b)


def _inproj0a_kernel(hs_ref, ctx_ref, modc_ref, w_ref, ua_ref, sza_ref, uc_ref, hcs_ref):
    r = pl.program_id(0)
    s = r + BLK * pl.program_id(1)

    @pl.when(s == 0)
    def _():
        for b in range(B):
            hc = ctx_ref[b] * (1.0 + modc_ref[:, D:2 * D]) + modc_ref[:, 0:D]
            _slabs_of(hc, hcs_ref.at[:, N_CCHUNK * b:N_CCHUNK * (b + 1), :])

    h = hs_ref[...]
    w_ua = w_ref[:, 0:512].astype(BF16)
    ua_ref[...] = _rot_blocks(jnp.dot(h, w_ua, preferred_element_type=F32), r).astype(BF16)
    sza_ref[...] = _silu(jnp.dot(h, w_ref[:, 512:1024].astype(BF16), preferred_element_type=F32)).astype(BF16)
    uc_ref[...] = _rot_blocks(jnp.dot(hcs_ref[s], w_ua, preferred_element_type=F32), r).astype(BF16)


def _inproj0a(hs, ctx, mod_c, w_in_f32):
    slab = lambda r, h: r + BLK * h
    sspec = lambda n, w: pl.BlockSpec((None, n, w), lambda r, h: (slab(r, h), 0, 0))
    const = lambda *s: pl.BlockSpec(s, lambda r, h: (0,) * len(s), pipeline_mode=pl.Buffered(1))
    so = lambda n: jax.ShapeDtypeStruct((S5_T, n, 512), BF16)
    nl, ncx = B * N_CHUNK, B * N_CCHUNK
    return pl.pallas_call(
        _inproj0a_kernel,
        out_shape=(so(nl), so(nl), so(ncx)),
        grid=(BLK, S5_T // BLK),
        in_specs=[sspec(nl, D), const(B, CTX, D), const(1, 3 * D),
                  pl.BlockSpec((D, SGU_COL0), lambda r, h: (0, 0))],
        out_specs=(sspec(nl, 512), sspec(nl, 512), sspec(ncx, 512)),
        scratch_shapes=[pltpu.VMEM((S5_T, ncx, D), BF16)],
        compiler_params=_params("arbitrary", "arbitrary"),
        name="inproj0a",
    )(hs, ctx, mod_c, w_in_f32)


SCAN_GROUPS = 4


def _scan_tiles(sre_ref, sim_ref, h_refs, n_tiles, carry, lams):
    row = lax.broadcasted_iota(jnp.int32, (8, 128), 0)
    lane = lax.broadcasted_iota(jnp.int32, (8, 128), 1)
    first = row < B
    fwd = lane < S5_P

    def body(k, c):
        of = pl.multiple_of(k * 8, 8)
        ob = pl.multiple_of((n_tiles - 1 - k) * 8, 8)
        out = []
        for gi in range(SCAN_GROUPS):
            lre, lim = lams[gi]
            hr, hi = c[2 * gi], c[2 * gi + 1]
            sr = jnp.where(fwd, sre_ref[gi, pl.ds(of, 8), :], pltpu.roll(sre_ref[gi, pl.ds(ob, 8), :], B, axis=0))
            si = jnp.where(fwd, sim_ref[gi, pl.ds(of, 8), :], pltpu.roll(sim_ref[gi, pl.ds(ob, 8), :], B, axis=0))
            h1r = lre * hr - lim * hi + sr
            h1i = lre * hi + lim * hr + si
            r1r = pltpu.roll(h1r, B, axis=0)
            r1i = pltpu.roll(h1i, B, axis=0)
            if h_refs is not None:
                fre_ref, fim_ref, bre_ref, bim_ref = h_refs
                er = jnp.where(first, hr, r1r)
                ei = jnp.where(first, hi, r1i)
                fre_ref[gi, pl.ds(of, 8), :] = er
                fim_ref[gi, pl.ds(of, 8), :] = ei
                bre_ref[gi, pl.ds(ob, 8), :] = pltpu.roll(er, B, axis=0)
                bim_ref[gi, pl.ds(ob, 8), :] = pltpu.roll(ei, B, axis=0)
            h2r = lre * r1r - lim * r1i + sr
            h2i = lre * r1i + lim * r1r + si
            out.append(jnp.where(first, pltpu.roll(h2r, B, axis=0), h2r))
            out.append(jnp.where(first, pltpu.roll(h2i, B, axis=0), h2i))
        return tuple(out)

    return lax.fori_loop(0, n_tiles, body, carry)


def _gather_group(slab_ref, src):
    halves = []
    for h in range(S5_T // BLK):
        acc = slab_ref[BLK * h]
        for s in range(1, BLK):
            acc = jnp.where(src == s, slab_ref[BLK * h + s], acc)
        halves.append(acc)
    return jnp.concatenate(halves, axis=1)


def _s5core_kernel(ul_ref, uc_ref, win_ref, wout_ref, mix_ref, l16_ref, o_ref,
                   u_ref, sre_ref, sim_ref, cre_ref, cim_ref, fre_ref, fim_ref, bre_ref, bim_ref, y_ref):
    nl = N_CHUNK * B
    ncx = N_CCHUNK * B
    blk_l = lax.broadcasted_iota(jnp.int32, (nl, 128), 1) >> H_SHIFT
    blk_c = lax.broadcasted_iota(jnp.int32, (ncx, 128), 1) >> H_SHIFT
    fwd = lax.broadcasted_iota(jnp.int32, (N_CHUNK, 128), 1) < S5_P
    for g0 in range(0, BLK, SCAN_GROUPS):
        for gi in range(SCAN_GROUPS):
            bg = g0 + gi
            win = win_ref[bg]
            src_l = ((blk_l - bg) & (BLK - 1)).astype(F32).astype(BF16)
            src_c = ((blk_c - bg) & (BLK - 1)).astype(F32).astype(BF16)
            u = _gather_group(ul_ref, src_l)
            u_ref[gi] = u
            sl = jnp.dot(u, win, preferred_element_type=F32)
            sc = jnp.dot(_gather_group(uc_ref, src_c), win, preferred_element_type=F32)
            for b in range(B):
                sre_ref[gi, pl.ds(b, N_CHUNK, stride=B), :] = sl[N_CHUNK * b:N_CHUNK * (b + 1), 0:128]
                sim_ref[gi, pl.ds(b, N_CHUNK, stride=B), :] = sl[N_CHUNK * b:N_CHUNK * (b + 1), 128:256]
                cre_ref[gi, pl.ds(b, N_CCHUNK, stride=B), :] = sc[N_CCHUNK * b:N_CCHUNK * (b + 1), 0:128]
                cim_ref[gi, pl.ds(b, N_CCHUNK, stride=B), :] = sc[N_CCHUNK * b:N_CCHUNK * (b + 1), 128:256]
        lams = [(jnp.broadcast_to(l16_ref[g0 + gi, 0:1, :], (8, 128)),
                 jnp.broadcast_to(l16_ref[g0 + gi, 1:2, :], (8, 128))) for gi in range(SCAN_GROUPS)]
        zero = tuple(jnp.zeros((8, 128), F32) for _ in range(2 * SCAN_GROUPS))
        carry = _scan_tiles(cre_ref, cim_ref, None, ncx // 8, zero, lams)
        _scan_tiles(sre_ref, sim_ref, (fre_ref, fim_ref, bre_ref, bim_ref), nl // 8, carry, lams)
        for gi in range(SCAN_GROUPS):
            bg = g0 + gi
            y = jnp.dot(u_ref[gi], mix_ref[bg], preferred_element_type=F32)
            hs = []
            for b in range(B):
                rows = pl.ds(b, N_CHUNK, stride=B)
                hs.append(jnp.concatenate([jnp.where(fwd, fre_ref[gi, rows, :], bre_ref[gi, rows, :]),
                                           jnp.where(fwd, fim_ref[gi, rows, :], bim_ref[gi, rows, :])], axis=1))
            hcat = jnp.concatenate(hs, axis=0).astype(BF16)
            y = y + jnp.dot(hcat, wout_ref[bg], preferred_element_type=F32)
            y_ref[bg] = y.astype(BF16)

    blk = blk_l.astype(F32).astype(BF16)
    for s in range(S5_T):
        h, r = s // BLK, s % BLK
        acc = None
        for j in range(BLK):
            piece = y_ref[(j - r) % BLK, :, 128 * h:128 * (h + 1)]
            acc = piece if acc is None else jnp.where(blk == j, piece, acc)
        o_ref[s] = acc


def _s5core(ul, uc, win, wout, mix, l16):
    TH = S5_T * S5_H
    nl = N_CHUNK * B
    ncx = N_CCHUNK * B
    g4 = lambda r, c: pl.BlockSpec((BLK, r, c), lambda q: (q, 0, 0))
    col = lambda n: pl.BlockSpec((S5_T, n, 128), lambda q: (0, 0, q))
    f32s = lambda n: pltpu.VMEM((SCAN_GROUPS, n, 128), F32)
    return pl.pallas_call(
        _s5core_kernel,
        out_shape=jax.ShapeDtypeStruct((S5_T, nl, S5_W), BF16),
        grid=(S5_G // BLK,),
        in_specs=[col(nl), col(ncx), g4(TH, TH), g4(TH, TH), g4(TH, TH), g4(8, 128)],
        out_specs=col(nl),
        scratch_shapes=[pltpu.VMEM((SCAN_GROUPS, nl, TH), BF16),
                        f32s(nl), f32s(nl), f32s(ncx), f32s(ncx), f32s(nl), f32s(nl), f32s(nl), f32s(nl),
                        pltpu.VMEM((BLK, nl, TH), BF16)],
        compiler_params=_params("arbitrary"),
        name="s5core",
    )(ul, uc, win, wout, mix, l16)


def _s5tail_kernel(slat_ref, sza_ref, gluw_ref, glub_ref, wtop_ref, y_ref):
    unrot = (BLK - pl.program_id(0)) & (BLK - 1)
    for b in range(B):
        rows = slice(N_CHUNK * b, N_CHUNK * (b + 1))
        g = _gelu(_rot_blocks(slat_ref[rows, :].astype(F32), unrot))
        gate = _sigmoid(jnp.dot(g.astype(BF16), gluw_ref[...], preferred_element_type=F32) + glub_ref[...])
        a = (g * gate * sza_ref[rows, :].astype(F32)).astype(BF16)
        y_ref[rows, :] = jnp.dot(a, wtop_ref[...], preferred_element_type=F32).astype(BF16)


def _s5tail(slat, sza, glu_w, glu_b, w_top):
    slab = lambda r, h: r + BLK * h
    sspec = lambda w: pl.BlockSpec((None, N_CHUNK * B, w), lambda r, h: (slab(r, h), 0, 0))
    full = lambda *s: pl.BlockSpec(s, lambda r, h: (0,) * len(s))
    return pl.pallas_call(
        _s5tail_kernel,
        out_shape=jax.ShapeDtypeStruct((S5_T, N_CHUNK * B, D), BF16),
        grid=(BLK, S5_T // BLK),
        in_specs=[sspec(512), sspec(512), full(512, 512), full(1, 512), full(S5_W, D)],
        out_specs=sspec(D),
        compiler_params=_params("arbitrary", "arbitrary"),
        name="s5tail",
    )(slat, sza, glu_w, glu_b, w_top)


def _tail0_kernel(x_ref, ys5_ref, guz_ref, vln_ref, mod_ref, sguw_ref, sgub_ref, wbot_ref, ng_ref, nb_ref, o_ref):
    tm = x_ref.shape[0]
    lane = lax.broadcasted_iota(jnp.int32, (SGU_CHUNK, 128), 1)
    lo = lane < SGU_HD
    zero = jnp.zeros((SGU_CHUNK, 128), BF16)
    chunks = []
    for ci in range(tm // SGU_CHUNK):
        v = vln_ref[ci * SGU_CHUNK:(ci + 1) * SGU_CHUNK, :]
        cols = []
        for pi in range(SGU_HEADS // 2):
            vp = v[:, 128 * pi:128 * (pi + 1)]
            bm = jnp.concatenate([jnp.where(lo, vp, zero), jnp.where(lo, zero, vp)], axis=0)
            cols.append(jnp.dot(sguw_ref[pi], bm, preferred_element_type=F32))
        chunks.append(jnp.concatenate(cols, axis=1) + sgub_ref[...])
    s = jnp.concatenate(chunks, axis=0)
    bsg = (guz_ref[...].astype(F32) * s).astype(BF16)
    perm = _chunk_transpose_perm()
    ys5 = jnp.concatenate(
        [jnp.dot(perm, ys5_ref[:, S5_T * j:S5_T * (j + 1), :].reshape(PERM_ROWS, D), preferred_element_type=F32)
         for j in range(tm // PERM_ROWS)], axis=0)
    y = ys5 + jnp.dot(bsg, wbot_ref[...], preferred_element_type=F32)
    gmod = mod_ref[:, 2 * D:3 * D]
    o_ref[...] = _layer_norm(DN_ALPHA * x_ref[...] + gmod * y, ng_ref[...], nb_ref[...])


def _tail0(x, ys5, guz, vln, mod, sguw, sgub, w_bot, ng, nb, tm=TOKEN_TILE):
    nct = tm // S5_T
    t512 = pl.BlockSpec((None, tm, 512), lambda b, i: (b, i, 0))
    tD = pl.BlockSpec((None, tm, D), lambda b, i: (b, i, 0))
    full = lambda *s: pl.BlockSpec(s, lambda b, i: (0,) * len(s))
    return pl.pallas_call(
        _tail0_kernel,
        out_shape=jax.ShapeDtypeStruct((B, L, D), F32),
        grid=(B, L // tm),
        in_specs=[tD, pl.BlockSpec((S5_T, nct, D), lambda b, i: (0, b * (N_CHUNK // nct) + i, 0)), t512, t512,
                  pl.BlockSpec((None, 1, 3 * D), lambda b, i: (b, 0, 0)),
                  full(SGU_HEADS // 2, SGU_CHUNK, 256), full(SGU_CHUNK, 512),
                  pl.BlockSpec((SGU_W, D), lambda b, i: (1, 0)), full(1, D), full(1, D)],
        out_specs=tD,
        compiler_params=_params("arbitrary", "arbitrary"),
        name="tail0",
    )(x, ys5, guz, vln, mod, sguw, sgub, w_bot, ng, nb)


CONV_C = D // 2
TILE_ROWS = TOKEN_TILE // GRID_W


def _grid_transpose_in(v, o_ref):
    perm = _chunk_transpose_perm()
    for q in range(GRID_W // S5_T):
        seg = jnp.concatenate([v[GRID_W * r + S5_T * q:GRID_W * r + S5_T * (q + 1), :] for r in range(TILE_ROWS)],
                              axis=0)
        t = jnp.dot(perm, seg, preferred_element_type=F32).astype(BF16)
        o_ref[S5_T * q:S5_T * (q + 1), :, :] = t.reshape(S5_T, TILE_ROWS, v.shape[1])


def _inproj1_kernel(x_ref, mod_ref, w_ref, hgr_ref, hgc_ref, h_ref):
    shift = mod_ref[:, 0:D]
    scale = mod_ref[:, D:2 * D]
    h = (x_ref[...] * (1.0 + scale) + shift).astype(BF16)
    h_ref[...] = h
    dot = lambda lo: jnp.dot(h, w_ref[:, lo:lo + CONV_C].astype(BF16), preferred_element_type=F32)
    hgr_ref[...] = (dot(0) * _sigmoid(dot(D))).astype(BF16)
    _grid_transpose_in((dot(CONV_C) * _sigmoid(dot(D + CONV_C))).astype(BF16), hgc_ref)


def _inproj1(x, mod, w_in_f32, tm=TOKEN_TILE):
    tile = lambda w: pl.BlockSpec((None, tm, w), lambda b, i: (b, i, 0))
    return pl.pallas_call(
        _inproj1_kernel,
        out_shape=(jax.ShapeDtypeStruct((B, L, CONV_C), BF16),
                   jax.ShapeDtypeStruct((B, GRID_W, GRID_W, CONV_C), BF16),
                   jax.ShapeDtypeStruct((B, L, D), BF16)),
        grid=(B, L // tm),
        in_specs=[tile(D),
                  pl.BlockSpec((None, 1, 3 * D), lambda b, i: (b, 0, 0)),
                  pl.BlockSpec((D, 2 * D), lambda b, i: (0, 0), pipeline_mode=pl.Buffered(1))],
        out_specs=(tile(CONV_C), pl.BlockSpec((None, GRID_W, TILE_ROWS, CONV_C), lambda b, i: (b, 0, i, 0)),
                   tile(D)),
        compiler_params=_params("arbitrary", "arbitrary"),
        name="inproj1",
    )(x, mod, w_in_f32)


DFT_N = 2 * GRID_W
TAPS_PAD = CONV_K + 1


def _dft_constants():
    th = 2.0 * math.pi / DFT_N
    f = jnp.arange(GRID_W, dtype=F32)[:, None]
    p = jnp.arange(GRID_W, dtype=F32)[None, :]
    cosm = jnp.cos(th * f * p)
    sinm = jnp.sin(th * f * p)
    alt = jnp.where(jnp.arange(GRID_W) % 2 == 0, 1.0, -1.0).astype(F32)
    fwd = jnp.concatenate([cosm, alt[None, :], sinm[1:]], axis=0)
    cf = jnp.where(jnp.arange(GRID_W) == 0, 1.0, 2.0).astype(F32) / DFT_N
    inv = jnp.concatenate([cosm.T * cf[None, :], (alt / DFT_N)[:, None], sinm.T[:, 1:] * (2.0 / DFT_N)], axis=1)
    sft = (CONV_HALF - jnp.arange(TAPS_PAD, dtype=F32))[None, :]
    live = (jnp.arange(TAPS_PAD) < CONV_K).astype(F32)[None, :]
    f64 = jnp.where(f == 0, float(GRID_W), f)
    return (fwd.astype(BF16), inv.astype(BF16),
            jnp.cos(th * f * sft) * live, jnp.sin(th * f * sft) * live, jnp.cos(th * f64 * sft) * live)


def _fconv_kernel(h_ref, w_ref, b_ref, fwd_ref, inv_ref, c1_ref, s3_ref, c4_ref, o_ref):
    hp = lax.Precision.HIGHEST
    taps = w_ref[...]
    g_re = jnp.dot(c1_ref[...], taps, preferred_element_type=F32, precision=hp)
    g_im = jnp.dot(s3_ref[...], taps, preferred_element_type=F32, precision=hp)
    g_r2 = jnp.dot(c4_ref[...], taps, preferred_element_type=F32, precision=hp)
    fwd = fwd_ref[...]
    inv = inv_ref[...]
    bias = b_ref[...]
    n_runs = h_ref.shape[0] // GRID_W
    rows = lambda r: slice(GRID_W * r, GRID_W * (r + 1))
    forward = lambda r: jnp.dot(fwd, h_ref[rows(r), :], preferred_element_type=F32)
    ahead = 2
    specs = [forward(r) for r in range(ahead)]
    for r in range(n_runs):
        if r + ahead < n_runs:
            specs.append(forward(r + ahead))
        spec = specs[r]
        a, bm = spec[0:GRID_W], spec[GRID_W:DFT_N]
        prod = jnp.concatenate([a * g_re - bm * g_im, a * g_im + bm * g_r2], axis=0).astype(BF16)
        o_ref[rows(r), :] = (jnp.dot(inv, prod, preferred_element_type=F32) + bias).astype(BF16)


def _fconv(h, taps, bias, consts, tm=L):
    fwd, inv, c1, s3, c4 = consts
    c = h.shape[-1]
    tile = pl.BlockSpec((None, tm, c), lambda b, i: (b, i, 0))
    full = lambda *s: pl.BlockSpec(s, lambda b, i: (0,) * len(s))
    return pl.pallas_call(
        _fconv_kernel,
        out_shape=jax.ShapeDtypeStruct(h.shape, BF16),
        grid=(B, L // tm),
        in_specs=[tile, full(TAPS_PAD, c), full(1, c), full(DFT_N, GRID_W), full(GRID_W, DFT_N),
                  full(GRID_W, TAPS_PAD), full(GRID_W, TAPS_PAD), full(GRID_W, TAPS_PAD)],
        out_specs=tile,
        compiler_params=_params("arbitrary", "arbitrary"),
        name="fconv",
    )(h, taps, bias, fwd, inv, c1, s3, c4)


ROW_BLOCK = 32


def _row_blocks(n_rows):
    return [slice(ROW_BLOCK * k, ROW_BLOCK * (k + 1)) for k in range(n_rows // ROW_BLOCK)]


def _tail1_kernel(x_ref, hcr_ref, hcc_ref, h1_ref, wz_ref, mod_ref, lng_ref, lnb_ref, wout_ref, ng_ref, nb_ref,
                  o_ref, col_ref, z_ref, m_ref, y_ref):
    tm = x_ref.shape[0]
    perm = _chunk_transpose_perm()
    for q in range(GRID_W // S5_T):
        blk = hcc_ref[S5_T * q:S5_T * (q + 1), :, :].reshape(PERM_ROWS, CONV_C)
        t = jnp.dot(perm, blk, preferred_element_type=F32)
        for r in range(TILE_ROWS):
            col_ref[GRID_W * r + S5_T * q:GRID_W * r + S5_T * (q + 1), :] = t[S5_T * r:S5_T * (r + 1), :]
    z_ref[...] = jnp.dot(h1_ref[...], wz_ref[...], preferred_element_type=F32)
    lng, lnb = lng_ref[...], lnb_ref[...]
    for rows in _row_blocks(tm):
        hc = jnp.concatenate([hcr_ref[rows, :].astype(F32), col_ref[rows, :]], axis=1)
        m_ref[rows, :] = (_silu_of_half(_layer_norm(hc, lng, lnb)) * _silu_of_half(z_ref[rows, :])).astype(BF16)
    y_ref[...] = jnp.dot(m_ref[...], wout_ref[...], preferred_element_type=F32)
    gmod = mod_ref[:, 2 * D:3 * D]
    ng, nb = ng_ref[...], nb_ref[...]
    for rows in _row_blocks(tm):
        o_ref[rows, :] = _layer_norm(DN_ALPHA * x_ref[rows, :] + gmod * y_ref[rows, :], ng, nb)


def _tail1(x, hc_row, hc_col, h1, w_z_half, mod, ln_g_half, ln_b_half, w_out, ng, nb, tm=TOKEN_TILE):
    tile = lambda w: pl.BlockSpec((None, tm, w), lambda b, i: (b, i, 0))
    full = lambda *s: pl.BlockSpec(s, lambda b, i: (0,) * len(s))
    return pl.pallas_call(
        _tail1_kernel,
        out_shape=jax.ShapeDtypeStruct((B, L, D), F32),
        grid=(B, L // tm),
        in_specs=[tile(D), tile(CONV_C),
                  pl.BlockSpec((None, GRID_W, TILE_ROWS, CONV_C), lambda b, i: (b, 0, i, 0)),
                  tile(D), full(D, D), pl.BlockSpec((None, 1, 3 * D), lambda b, i: (b, 0, 0)),
                  full(1, D), full(1, D), full(D, D), full(1, D), full(1, D)],
        out_specs=tile(D),
        scratch_shapes=[pltpu.VMEM((tm, CONV_C), F32), pltpu.VMEM((tm, D), F32), pltpu.VMEM((tm, D), BF16),
                        pltpu.VMEM((tm, D), F32)],
        compiler_params=_params("arbitrary", "arbitrary"),
        name="tail1",
    )(x, hc_row, hc_col, h1, w_z_half, mod, ln_g_half, ln_b_half, w_out, ng, nb)


def kernel(x, c, ctx, c_ctx, mod_w, mod_b, norm_g, norm_b, ev_w_in, ev_w_out, s5_lam_re, s5_lam_im, s5_log_dt, s5_b_re, s5_b_im, s5_c_re, s5_c_im, s5_d, glu_w, glu_b, sgu_ln_g, sgu_ln_b, sgu_w, sgu_b, od_w_in, od_w_out, dw_w, dw_b, conv_ln_g, conv_ln_b):
    TH = S5_T * S5_H
    row = lambda v: v.reshape(1, -1)

    cond8 = jnp.concatenate([c, c_ctx[None], jnp.zeros((3, D), F32)], axis=0)
    mods = _adaln(cond8, mod_w, mod_b)
    mod0 = mods[0, :B].reshape(B, 1, 3 * D)
    mod0c = mods[0, B:B + 1]
    mod1 = mods[1, :B].reshape(B, 1, 3 * D)

    lbr, lbi, cfr, cfi = _discretise(s5_lam_re[0], s5_lam_im[0], s5_log_dt[0])
    rowcat = lambda a: jnp.concatenate([a[0], a[1]], axis=-1).reshape(S5_G, 1, 2 * S5_P)
    bt = lambda a: jnp.concatenate([jnp.swapaxes(a[0], 1, 2), jnp.swapaxes(a[1], 1, 2)], axis=-1)
    cn = lambda a: jnp.concatenate([a[0], a[1]], axis=-1)
    d_row = jnp.tile(s5_d[0].reshape(S5_G, 1, S5_H), (1, 1, S5_T))
    win, wout, mix, l16 = _s5_weights(rowcat(lbr), rowcat(lbi), rowcat(cfr), rowcat(cfi),
                                      bt(s5_b_re[0]), bt(s5_b_im[0]), cn(s5_c_re[0]), cn(s5_c_im[0]), d_row)

    w_out0 = ev_w_out[0].astype(BF16)
    glu_w0 = glu_w[0].astype(BF16)
    guz, vln, hs = _inproj0n(x, mod0, ev_w_in[0], row(sgu_ln_g[0]), row(sgu_ln_b[0]))
    ua, sza, ua_c = _inproj0a(hs, ctx, mod0c, ev_w_in[0])
    s_lat = _s5core(ua, ua_c, win, wout, mix, l16)
    y_s5 = _s5tail(s_lat, sza, glu_w0, row(glu_b[0]), w_out0)
    sguw = sgu_w[0].reshape(SGU_HEADS // 2, 2, SGU_CHUNK, SGU_CHUNK)
    sguw = jnp.transpose(sguw, (0, 2, 1, 3)).reshape(SGU_HEADS // 2, SGU_CHUNK, 2 * SGU_CHUNK).astype(BF16)
    sgub = jnp.repeat(sgu_b[0].T, SGU_HD, axis=1)
    x1 = _tail0(x, y_s5, guz, vln, mod0, sguw, sgub, w_out0, row(norm_g[0]), row(norm_b[0]))

    hg_row, hg_col, h1 = _inproj1(x1, mod1, od_w_in[0])
    consts = _dft_constants()
    taps = jnp.pad(dw_w[0], ((0, TAPS_PAD - CONV_K), (0, 0)))
    bias = row(dw_b[0])
    hc_row = _fconv(hg_row, taps[:, :CONV_C], bias[:, :CONV_C], consts)
    hc_col = _fconv(hg_col.reshape(B, L, CONV_C), taps[:, CONV_C:], bias[:, CONV_C:], consts)
    w_z_half = (0.5 * od_w_in[0][:, 2 * D:]).astype(BF16)
    return _tail1(x1, hc_row, hc_col.reshape(B, GRID_W, GRID_W, CONV_C), h1, w_z_half, mod1,
                  row(0.5 * conv_ln_g[0]), row(0.5 * conv_ln_b[0]), od_w_out[0].astype(BF16),
                  row(norm_g[1]), row(norm_b[1]))
```

```python
import functools
import math

import jax
import jax.numpy as jnp
import numpy as np
from jax import lax
from jax.experimental import pallas as pl
from jax.experimental.pallas import tpu as pltpu

D = 1024
B = 4
L = 4096
CTX = 256
GRID_W = 64
S5_W = 512
S5_G = 32
S5_H = 16
H_SHIFT = 4
BLK = 128 // S5_H
S5_P = 64
S5_T = 16
SGU_W = 512
SGU_HEADS = 8
SGU_HD = 64
SGU_CHUNK = 128
CONV_K = 31
CONV_HALF = CONV_K // 2
EVEN_IN = 2560
SGU_COL0 = 2 * S5_W
ODD_IN = 3072
DEPTH = 2
DN_ALPHA = (2 * DEPTH) ** 0.25
LN_EPS = 1e-5
N_CHUNK = L // S5_T
N_CCHUNK = CTX // S5_T
VMEM_LIMIT_V7X = 56 * 1024 * 1024
TOKEN_TILE = 1024

F32 = jnp.float32
BF16 = jnp.bfloat16


GELU_C = math.sqrt(2.0 / math.pi)


def _gelu(x):
    hx = 0.5 * x
    return hx * jnp.tanh(x * ((x * x) * (0.044715 * GELU_C) + GELU_C)) + hx


def _sigmoid(x):
    return 0.5 * jnp.tanh(0.5 * x) + 0.5


def _silu_of_half(hx):
    return hx * jnp.tanh(hx) + hx


def _silu(x):
    return _silu_of_half(0.5 * x)


def _layer_norm(x, g, b):
    mu = jnp.mean(x, axis=-1, keepdims=True)
    xc = x - mu
    var = jnp.mean(xc * xc, axis=-1, keepdims=True)
    return xc * lax.rsqrt(var + LN_EPS) * g + b


def _params(*sem):
    return pltpu.CompilerParams(dimension_semantics=sem, vmem_limit_bytes=VMEM_LIMIT_V7X)


def _adaln_kernel(c_ref, w_ref, b_ref, o_ref):
    def split(v):
        hi = v.astype(BF16)
        return hi, (v - hi.astype(F32)).astype(BF16)

    a_hi, a_lo = split(_silu(c_ref[...]))
    w_hi, w_lo = split(w_ref[...])
    dot = functools.partial(jnp.dot, preferred_element_type=F32)
    o_ref[...] = dot(a_hi, w_hi) + dot(a_lo, w_hi) + dot(a_hi, w_lo) + b_ref[...]


def _adaln(cond8, mod_w, mod_b):
    tn = 1024
    return pl.pallas_call(
        _adaln_kernel,
        out_shape=jax.ShapeDtypeStruct((DEPTH, 8, 3 * D), F32),
        grid=(DEPTH, 3 * D // tn),
        in_specs=[pl.BlockSpec((8, D), lambda l, j: (0, 0)),
                  pl.BlockSpec((None, D, tn), lambda l, j: (l, 0, j)),
                  pl.BlockSpec((None, 1, tn), lambda l, j: (l, 0, j))],
        out_specs=pl.BlockSpec((None, 8, tn), lambda l, j: (l, 0, j)),
        compiler_params=_params("arbitrary", "arbitrary"),
        name="adaln",
    )(cond8, mod_w, mod_b.reshape(DEPTH, 1, 3 * D))


S5W_GROUPS = BLK


def _cpow(base_pows, j):
    re = None
    im = None
    for k, (pr, pi) in enumerate(base_pows):
        bit = ((j >> k) & 1) == 1
        mr = jnp.where(bit, pr, 1.0)
        mi = jnp.where(bit, pi, 0.0)
        if re is None:
            re, im = mr, mi
        else:
            re, im = re * mr - im * mi, re * mi + im * mr
    return re, im


def _squarings(pr, pi, n):
    out = [(pr, pi)]
    for _ in range(n - 1):
        pr, pi = pr * pr - pi * pi, 2.0 * pr * pi
        out.append((pr, pi))
    return out


def _shift_lanes(x, n):
    lane = lax.broadcasted_iota(jnp.int32, (S5_H, 128), 1)
    lo, hi = x[:, :128], x[:, 128:]
    if n == 0:
        return x
    if n < 128:
        rlo = pltpu.roll(lo, n, axis=1)
        rhi = pltpu.roll(hi, n, axis=1)
        return jnp.concatenate([jnp.where(lane >= n, rlo, 0.0), jnp.where(lane >= n, rhi, rlo)], axis=1)
    m = n - 128
    rlo = lo if m == 0 else pltpu.roll(lo, m, axis=1)
    return jnp.concatenate([jnp.zeros_like(lo), jnp.where(lane >= m, rlo, 0.0)], axis=1)


def _unshift_lanes(x, n):
    lane = lax.broadcasted_iota(jnp.int32, (S5_H, 128), 1)
    lo, hi = x[:, :128], x[:, 128:]
    if n == 0:
        return x
    if n < 128:
        rlo = pltpu.roll(lo, 128 - n, axis=1)
        rhi = pltpu.roll(hi, 128 - n, axis=1)
        keep = lane < 128 - n
        return jnp.concatenate([jnp.where(keep, rlo, rhi), jnp.where(keep, rhi, 0.0)], axis=1)
    m = n - 128
    rhi = hi if m == 0 else pltpu.roll(hi, 128 - m, axis=1)
    return jnp.concatenate([jnp.where(lane < 128 - m, rhi, 0.0), jnp.zeros_like(lo)], axis=1)


def _s5w_group(gi, bg, disc, bt_ref, cn_ref, d_ref, win_ref, wout_ref, mix_ref, l16_ref):
    TH = S5_T * S5_H

    def chunk_pos(idx):
        return (((idx >> H_SHIFT) - bg) & (BLK - 1)) + ((idx >> 7) << 3)

    lr, li, cr, ci = [v[gi:gi + 1] for v in disc]
    pows_row = _squarings(lr, li, 5)
    l16_ref[gi, 0:1, :] = pows_row[4][0]
    l16_ref[gi, 1:2, :] = pows_row[4][1]
    l16_ref[gi, 2:8, :] = jnp.zeros((6, 128), F32)
    btr = bt_ref[0, gi]
    bti = bt_ref[1, gi]
    bbr = cr * btr - ci * bti
    bbi = cr * bti + ci * btr
    blk16 = lax.broadcasted_iota(jnp.int32, (S5_T, 128), 0)
    is_f16 = lax.broadcasted_iota(jnp.int32, (S5_T, 128), 1) < S5_P
    pos16 = chunk_pos(blk16 << H_SHIFT)
    pr16, pi16 = _cpow(pows_row[:4], jnp.where(is_f16, S5_T - 1 - pos16, pos16))
    rep_rows = lambda v: jnp.broadcast_to(v[:, None, :], (S5_T, S5_H, 128)).reshape(TH, 128)
    pr, pi = rep_rows(pr16), rep_rows(pi16)
    tbr = jnp.broadcast_to(bbr[None], (S5_T, S5_H, 128)).reshape(TH, 128)
    tbi = jnp.broadcast_to(bbi[None], (S5_T, S5_H, 128)).reshape(TH, 128)
    win_ref[gi, :, 0:128] = (pr * tbr - pi * tbi).astype(BF16)
    win_ref[gi, :, 128:256] = (pr * tbi + pi * tbr).astype(BF16)

    hp = lax.Precision.HIGHEST
    dot = functools.partial(jnp.dot, preferred_element_type=F32, precision=hp)
    def col256(r):
        col = jnp.broadcast_to(r, (2 * S5_P, 2 * S5_P)).T
        return jnp.concatenate([col, col], axis=1)

    def tiled_t(cn):
        t8 = jnp.broadcast_to(cn[None], (BLK, S5_H, 2 * S5_P)).reshape(2 * S5_P, 2 * S5_P).T
        return jnp.concatenate([t8, t8], axis=1)

    cpows = _squarings(col256(lr), col256(li), 4)
    row = lax.broadcasted_iota(jnp.int32, (2 * S5_P, TH), 0)
    lane_w = lax.broadcasted_iota(jnp.int32, (2 * S5_P, TH), 1)
    t_idx = chunk_pos(lane_w)
    j_idx = lane_w >> H_SHIFT
    is_f = row < S5_P
    ctr = tiled_t(cn_ref[0, gi])
    cti = tiled_t(cn_ref[1, gi])
    er, ei = _cpow(cpows, jnp.where(is_f, t_idx, S5_T - 1 - t_idx))
    er, ei = er * cpows[0][0] - ei * cpows[0][1], er * cpows[0][1] + ei * cpows[0][0]
    wr = ctr * er - cti * ei
    wi = ctr * ei + cti * er
    wout_ref[gi, 0:128, :] = wr.astype(BF16)
    wout_ref[gi, 128:256, :] = (-wi).astype(BF16)
    kr, ki = _cpow(cpows, jnp.where(is_f, j_idx, S5_T - 1 - j_idx))
    ekr = ctr * kr - cti * ki
    eki = ctr * ki + cti * kr
    lane16 = lax.broadcasted_iota(jnp.int32, (S5_H, 128), 1)
    mf = lane16 < S5_P
    kkf = dot(jnp.where(mf, bbr, 0.0), ekr) - dot(jnp.where(mf, bbi, 0.0), eki)
    kkb = dot(jnp.where(mf, 0.0, bbr), ekr) - dot(jnp.where(mf, 0.0, bbi), eki)
    dl = d_ref[gi]
    r16 = lax.broadcasted_iota(jnp.int32, (S5_H, TH), 0)
    l256 = lax.broadcasted_iota(jnp.int32, (S5_H, TH), 1)
    rot = bg * S5_H
    for s in range(S5_T):
        blk = _shift_lanes(kkf, S5_H * s) + _unshift_lanes(kkb, S5_H * (S5_T - 1 - s))
        blk = blk + jnp.where(l256 == r16 + S5_H * s, dl, 0.0)
        if rot:
            blk = jnp.concatenate([pltpu.roll(blk[:, :128], rot, axis=1), pltpu.roll(blk[:, 128:], rot, axis=1)], axis=1)
        rho = ((s + bg) & (BLK - 1)) + (s & BLK)
        mix_ref[gi, S5_H * rho:S5_H * (rho + 1), :] = blk.astype(BF16)


def _s5w_kernel(lam_ref, *refs):
    lr = lam_ref[0]
    li = lam_ref[1]
    dt = jnp.exp(lam_ref[2])
    mag = jnp.exp(lr * dt)
    br = mag * jnp.cos(li * dt)
    bi = mag * jnp.sin(li * dt)
    inv = 1.0 / (lr * lr + li * li)
    nr = br - 1.0
    disc = (br, bi, (nr * lr + bi * li) * inv, (bi * lr - nr * li) * inv)
    for gi in range(S5W_GROUPS):
        _s5w_group(gi, gi % BLK, disc, *refs)


def _s5_weights(lam3, bt, cn, d_row):
    TH = S5_T * S5_H
    g3 = lambda r, c: pl.BlockSpec((S5W_GROUPS, r, c), lambda g: (g, 0, 0))
    ri = pl.BlockSpec((2, S5W_GROUPS, S5_H, 2 * S5_P), lambda g: (0, g, 0, 0))
    wshape = jax.ShapeDtypeStruct((S5_G, TH, TH), BF16)
    return pl.pallas_call(
        _s5w_kernel,
        out_shape=(wshape, wshape, wshape, jax.ShapeDtypeStruct((S5_G, 8, 128), F32)),
        grid=(S5_G // S5W_GROUPS,),
        in_specs=[pl.BlockSpec((3, S5W_GROUPS, 2 * S5_P), lambda g: (0, g, 0)), ri, ri, g3(1, TH)],
        out_specs=(g3(TH, TH), g3(TH, TH), g3(TH, TH), g3(8, 128)),
        compiler_params=_params("arbitrary"),
        name="s5_weights",
    )(lam3, bt, cn, d_row)


def _rot_blocks(v, r):
    cols = [pltpu.roll(v[:, 128 * q:128 * (q + 1)], S5_H * r, axis=1) for q in range(v.shape[1] // 128)]
    return jnp.concatenate(cols, axis=1)


def _slabs_of(h, hs_ref):
    h3 = h.reshape(h.shape[0] // S5_T, S5_T, h.shape[1])
    for s in range(S5_T):
        hs_ref[s] = h3[:, s, :].astype(BF16)


PERM_ROWS = S5_T * S5_T


def _chunk_transpose_perm():
    ri = lax.broadcasted_iota(jnp.int32, (PERM_ROWS, PERM_ROWS), 0)
    ci = lax.broadcasted_iota(jnp.int32, (PERM_ROWS, PERM_ROWS), 1)
    hit = ((ri >> H_SHIFT) == (ci & (S5_T - 1))) & ((ri & (S5_T - 1)) == (ci >> H_SHIFT))
    return jnp.where(hit, 1.0, 0.0).astype(BF16)


def _inproj0n_kernel(x_ref, mod_ref, w_ref, lng_ref, lnb_ref, guz_ref, vln_ref, hs_ref):
    shift = mod_ref[:, 0:D]
    scale = mod_ref[:, D:2 * D]
    hb = (x_ref[...] * (1.0 + scale) + shift).astype(BF16)
    perm = _chunk_transpose_perm()
    for j in range(hb.shape[0] // PERM_ROWS):
        blk = jnp.dot(perm, hb[PERM_ROWS * j:PERM_ROWS * (j + 1), :], preferred_element_type=F32).astype(BF16)
        for s in range(S5_T):
            hs_ref[s, S5_T * j:S5_T * (j + 1), :] = blk[S5_T * s:S5_T * (s + 1), :]
    dot = lambda lo: jnp.dot(hb, w_ref[:, SGU_COL0 + lo:SGU_COL0 + lo + 512].astype(BF16),
                             preferred_element_type=F32)
    guz_ref[...] = (_gelu(dot(0)) * _silu(dot(1024))).astype(BF16)
    vln_ref[...] = _layer_norm(_gelu(dot(512)), lng_ref[...], lnb_ref[...]).astype(BF16)


def _inproj0n(x, mod, w_in_f32, ln_g, ln_b, tm=TOKEN_TILE):
    nct = tm // S5_T
    o = jax.ShapeDtypeStruct((B, L, 512), BF16)
    ospec = pl.BlockSpec((None, tm, 512), lambda b, i: (b, i, 0))
    full = lambda *s: pl.BlockSpec(s, lambda b, i: (0,) * len(s))
    return pl.pallas_call(
        _inproj0n_kernel,
        out_shape=(o, o, jax.ShapeDtypeStruct((S5_T, B * N_CHUNK, D), BF16)),
        grid=(B, L // tm),
        in_specs=[pl.BlockSpec((None, tm, D), lambda b, i: (b, i, 0)),
                  pl.BlockSpec((None, 1, 3 * D), lambda b, i: (b, 0, 0)),
                  pl.BlockSpec((D, EVEN_IN), lambda b, i: (0, 0), pipeline_mode=pl.Buffered(1)),
                  full(1, 512), full(1, 512)],
        out_specs=(ospec, ospec,
                   pl.BlockSpec((S5_T, nct, D), lambda b, i: (0, b * (N_CHUNK // nct) + i, 0))),
        compiler_params=_params("arbitrary", "arbitrary"),
        name="inproj0n",
    )(x, mod, w_in_f32, ln_g, ln_b)


def _ctx_slabs_kernel(x_ref, mod_ref, hs_ref):
    h = x_ref[...] * (1.0 + mod_ref[:, D:2 * D]) + mod_ref[:, 0:D]
    _slabs_of(h, hs_ref)


def _ctx_slabs(ctx, mod_c):
    return pl.pallas_call(
        _ctx_slabs_kernel,
        out_shape=jax.ShapeDtypeStruct((S5_T, B * N_CCHUNK, D), BF16),
        grid=(B,),
        in_specs=[pl.BlockSpec((None, CTX, D), lambda b: (b, 0, 0)),
                  pl.BlockSpec((1, 3 * D), lambda b: (0, 0))],
        out_specs=pl.BlockSpec((S5_T, N_CCHUNK, D), lambda b: (0, b, 0)),
        compiler_params=_params("arbitrary"),
        name="ctx_slabs",
    )(ctx, mod_c)


def _inproj0a_kernel(hs_ref, hc_ref, w_ref, ua_ref, sza_ref, uc_ref):
    r = pl.program_id(0)
    h = hs_ref[...]
    w_ua = w_ref[:, 0:512].astype(BF16)
    ua_ref[...] = _rot_blocks(jnp.dot(h, w_ua, preferred_element_type=F32), r).astype(BF16)
    sza_ref[...] = _silu(jnp.dot(h, w_ref[:, 512:1024].astype(BF16), preferred_element_type=F32)).astype(BF16)
    uc_ref[...] = _rot_blocks(jnp.dot(hc_ref[...], w_ua, preferred_element_type=F32), r).astype(BF16)


def _inproj0a(hs, hcs, w_in_f32):
    slab = lambda r, h: r + BLK * h
    sspec = lambda n, w: pl.BlockSpec((None, n, w), lambda r, h: (slab(r, h), 0, 0))
    so = lambda n: jax.ShapeDtypeStruct((S5_T, n, 512), BF16)
    nl, ncx = B * N_CHUNK, B * N_CCHUNK
    return pl.pallas_call(
        _inproj0a_kernel,
        out_shape=(so(nl), so(nl), so(ncx)),
        grid=(BLK, S5_T // BLK),
        in_specs=[sspec(nl, D), sspec(ncx, D), pl.BlockSpec((D, SGU_COL0), lambda r, h: (0, 0))],
        out_specs=(sspec(nl, 512), sspec(nl, 512), sspec(ncx, 512)),
        compiler_params=_params("arbitrary", "arbitrary"),
        name="inproj0a",
    )(hs, hcs, w_in_f32)


SCAN_GROUPS = 4


def _scan_tiles(sre_ref, sim_ref, h_refs, n_tiles, carry, lams):
    row = lax.broadcasted_iota(jnp.int32, (8, 128), 0)
    lane = lax.broadcasted_iota(jnp.int32, (8, 128), 1)
    first = row < B
    fwd = lane < S5_P

    def body(k, c):
        of = pl.multiple_of(k * 8, 8)
        ob = pl.multiple_of((n_tiles - 1 - k) * 8, 8)
        out = []
        for gi in range(SCAN_GROUPS):
            lre, lim = lams[gi]
            hr, hi = c[2 * gi], c[2 * gi + 1]
            sr = jnp.where(fwd, sre_ref[gi, pl.ds(of, 8), :], pltpu.roll(sre_ref[gi, pl.ds(ob, 8), :], B, axis=0))
            si = jnp.where(fwd, sim_ref[gi, pl.ds(of, 8), :], pltpu.roll(sim_ref[gi, pl.ds(ob, 8), :], B, axis=0))
            h1r = lre * hr - lim * hi + sr
            h1i = lre * hi + lim * hr + si
            r1r = pltpu.roll(h1r, B, axis=0)
            r1i = pltpu.roll(h1i, B, axis=0)
            if h_refs is not None:
                fre_ref, fim_ref, bre_ref, bim_ref = h_refs
                er = jnp.where(first, hr, r1r)
                ei = jnp.where(first, hi, r1i)
                fre_ref[gi, pl.ds(of, 8), :] = er
                fim_ref[gi, pl.ds(of, 8), :] = ei
                bre_ref[gi, pl.ds(ob, 8), :] = pltpu.roll(er, B, axis=0)
                bim_ref[gi, pl.ds(ob, 8), :] = pltpu.roll(ei, B, axis=0)
            h2r = lre * r1r - lim * r1i + sr
            h2i = lre * r1i + lim * r1r + si
            out.append(jnp.where(first, pltpu.roll(h2r, B, axis=0), h2r))
            out.append(jnp.where(first, pltpu.roll(h2i, B, axis=0), h2i))
        return tuple(out)

    return lax.fori_loop(0, n_tiles, body, carry)


def _gather_group(slab_ref, src):
    halves = []
    for h in range(S5_T // BLK):
        acc = slab_ref[BLK * h]
        for s in range(1, BLK):
            acc = jnp.where(src == s, slab_ref[BLK * h + s], acc)
        halves.append(acc)
    return jnp.concatenate(halves, axis=1)


def _s5core_kernel(ul_ref, uc_ref, win_ref, wout_ref, mix_ref, l16_ref, o_ref,
                   u_ref, sre_ref, sim_ref, cre_ref, cim_ref, fre_ref, fim_ref, bre_ref, bim_ref, y_ref):
    nl = N_CHUNK * B
    ncx = N_CCHUNK * B
    blk_l = lax.broadcasted_iota(jnp.int32, (nl, 128), 1) >> H_SHIFT
    blk_c = lax.broadcasted_iota(jnp.int32, (ncx, 128), 1) >> H_SHIFT
    fwd = lax.broadcasted_iota(jnp.int32, (N_CHUNK, 128), 1) < S5_P
    for g0 in range(0, BLK, SCAN_GROUPS):
        for gi in range(SCAN_GROUPS):
            bg = g0 + gi
            win = win_ref[bg]
            src_l = ((blk_l - bg) & (BLK - 1)).astype(F32).astype(BF16)
            src_c = ((blk_c - bg) & (BLK - 1)).astype(F32).astype(BF16)
            u = _gather_group(ul_ref, src_l)
            u_ref[gi] = u
            sl = jnp.dot(u, win, preferred_element_type=F32)
            sc = jnp.dot(_gather_group(uc_ref, src_c), win, preferred_element_type=F32)
            for b in range(B):
                sre_ref[gi, pl.ds(b, N_CHUNK, stride=B), :] = sl[N_CHUNK * b:N_CHUNK * (b + 1), 0:128]
                sim_ref[gi, pl.ds(b, N_CHUNK, stride=B), :] = sl[N_CHUNK * b:N_CHUNK * (b + 1), 128:256]
                cre_ref[gi, pl.ds(b, N_CCHUNK, stride=B), :] = sc[N_CCHUNK * b:N_CCHUNK * (b + 1), 0:128]
                cim_ref[gi, pl.ds(b, N_CCHUNK, stride=B), :] = sc[N_CCHUNK * b:N_CCHUNK * (b + 1), 128:256]
        lams = [(jnp.broadcast_to(l16_ref[g0 + gi, 0:1, :], (8, 128)),
                 jnp.broadcast_to(l16_ref[g0 + gi, 1:2, :], (8, 128))) for gi in range(SCAN_GROUPS)]
        zero = tuple(jnp.zeros((8, 128), F32) for _ in range(2 * SCAN_GROUPS))
        carry = _scan_tiles(cre_ref, cim_ref, None, ncx // 8, zero, lams)
        _scan_tiles(sre_ref, sim_ref, (fre_ref, fim_ref, bre_ref, bim_ref), nl // 8, carry, lams)
        for gi in range(SCAN_GROUPS):
            bg = g0 + gi
            y = jnp.dot(u_ref[gi], mix_ref[bg], preferred_element_type=F32)
            hs = []
            for b in range(B):
                rows = pl.ds(b, N_CHUNK, stride=B)
                hs.append(jnp.concatenate([jnp.where(fwd, fre_ref[gi, rows, :], bre_ref[gi, rows, :]),
                                           jnp.where(fwd, fim_ref[gi, rows, :], bim_ref[gi, rows, :])], axis=1))
            hcat = jnp.concatenate(hs, axis=0).astype(BF16)
            y = y + jnp.dot(hcat, wout_ref[bg], preferred_element_type=F32)
            y_ref[bg] = y.astype(BF16)

    blk = blk_l.astype(F32).astype(BF16)
    for s in range(S5_T):
        h, r = s // BLK, s % BLK
        acc = None
        for j in range(BLK):
            piece = y_ref[(j - r) % BLK, :, 128 * h:128 * (h + 1)]
            acc = piece if acc is None else jnp.where(blk == j, piece, acc)
        o_ref[s] = acc


def _s5core(ul, uc, win, wout, mix, l16):
    TH = S5_T * S5_H
    nl = N_CHUNK * B
    ncx = N_CCHUNK * B
    g4 = lambda r, c: pl.BlockSpec((BLK, r, c), lambda q: (q, 0, 0))
    col = lambda n: pl.BlockSpec((S5_T, n, 128), lambda q: (0, 0, q))
    f32s = lambda n: pltpu.VMEM((SCAN_GROUPS, n, 128), F32)
    return pl.pallas_call(
        _s5core_kernel,
        out_shape=jax.ShapeDtypeStruct((S5_T, nl, S5_W), BF16),
        grid=(S5_G // BLK,),
        in_specs=[col(nl), col(ncx), g4(TH, TH), g4(TH, TH), g4(TH, TH), g4(8, 128)],
        out_specs=col(nl),
        scratch_shapes=[pltpu.VMEM((SCAN_GROUPS, nl, TH), BF16),
                        f32s(nl), f32s(nl), f32s(ncx), f32s(ncx), f32s(nl), f32s(nl), f32s(nl), f32s(nl),
                        pltpu.VMEM((BLK, nl, TH), BF16)],
        compiler_params=_params("arbitrary"),
        name="s5core",
    )(ul, uc, win, wout, mix, l16)


def _s5tail_kernel(slat_ref, sza_ref, gluw_ref, glub_ref, wtop_ref, y_ref):
    unrot = (BLK - pl.program_id(0)) & (BLK - 1)
    for b in range(B):
        rows = slice(N_CHUNK * b, N_CHUNK * (b + 1))
        g = _gelu(_rot_blocks(slat_ref[rows, :].astype(F32), unrot))
        gate = _sigmoid(jnp.dot(g.astype(BF16), gluw_ref[...], preferred_element_type=F32) + glub_ref[...])
        a = (g * gate * sza_ref[rows, :].astype(F32)).astype(BF16)
        y_ref[rows, :] = jnp.dot(a, wtop_ref[...], preferred_element_type=F32).astype(BF16)


def _s5tail(slat, sza, glu_w, glu_b, w_top):
    slab = lambda r, h: r + BLK * h
    sspec = lambda w: pl.BlockSpec((None, N_CHUNK * B, w), lambda r, h: (slab(r, h), 0, 0))
    full = lambda *s: pl.BlockSpec(s, lambda r, h: (0,) * len(s))
    return pl.pallas_call(
        _s5tail_kernel,
        out_shape=jax.ShapeDtypeStruct((S5_T, N_CHUNK * B, D), BF16),
        grid=(BLK, S5_T // BLK),
        in_specs=[sspec(512), sspec(512), full(512, 512), full(1, 512), full(S5_W, D)],
        out_specs=sspec(D),
        compiler_params=_params("arbitrary", "arbitrary"),
        name="s5tail",
    )(slat, sza, glu_w, glu_b, w_top)


def _tail0_kernel(x_ref, ys5_ref, guz_ref, vln_ref, mod_ref, sguw_ref, sgub_ref, wbot_ref, ng_ref, nb_ref, o_ref):
    tm = x_ref.shape[0]
    lane = lax.broadcasted_iota(jnp.int32, (SGU_CHUNK, 128), 1)
    lo = lane < SGU_HD
    zero = jnp.zeros((SGU_CHUNK, 128), BF16)
    chunks = []
    for ci in range(tm // SGU_CHUNK):
        v = vln_ref[ci * SGU_CHUNK:(ci + 1) * SGU_CHUNK, :]
        cols = []
        for pi in range(SGU_HEADS // 2):
            vp = v[:, 128 * pi:128 * (pi + 1)]
            bm = jnp.concatenate([jnp.where(lo, vp, zero), jnp.where(lo, zero, vp)], axis=0)
            cols.append(jnp.dot(sguw_ref[pi], bm, preferred_element_type=F32))
        chunks.append(jnp.concatenate(cols, axis=1) + sgub_ref[...])
    s = jnp.concatenate(chunks, axis=0)
    bsg = (guz_ref[...].astype(F32) * s).astype(BF16)
    perm = _chunk_transpose_perm()
    ys5 = jnp.concatenate(
        [jnp.dot(perm, ys5_ref[:, S5_T * j:S5_T * (j + 1), :].reshape(PERM_ROWS, D), preferred_element_type=F32)
         for j in range(tm // PERM_ROWS)], axis=0)
    y = ys5 + jnp.dot(bsg, wbot_ref[...], preferred_element_type=F32)
    gmod = mod_ref[:, 2 * D:3 * D]
    o_ref[...] = _layer_norm(DN_ALPHA * x_ref[...] + gmod * y, ng_ref[...], nb_ref[...])


def _tail0(x, ys5, guz, vln, mod, sguw, sgub, w_bot, ng, nb, tm=TOKEN_TILE):
    nct = tm // S5_T
    t512 = pl.BlockSpec((None, tm, 512), lambda b, i: (b, i, 0))
    tD = pl.BlockSpec((None, tm, D), lambda b, i: (b, i, 0))
    full = lambda *s: pl.BlockSpec(s, lambda b, i: (0,) * len(s))
    return pl.pallas_call(
        _tail0_kernel,
        out_shape=jax.ShapeDtypeStruct((B, L, D), F32),
        grid=(B, L // tm),
        in_specs=[tD, pl.BlockSpec((S5_T, nct, D), lambda b, i: (0, b * (N_CHUNK // nct) + i, 0)), t512, t512,
                  pl.BlockSpec((None, 1, 3 * D), lambda b, i: (b, 0, 0)),
                  full(SGU_HEADS // 2, SGU_CHUNK, 256), full(SGU_CHUNK, 512),
                  pl.BlockSpec((SGU_W, D), lambda b, i: (1, 0)), full(1, D), full(1, D)],
        out_specs=tD,
        compiler_params=_params("arbitrary", "arbitrary"),
        name="tail0",
    )(x, ys5, guz, vln, mod, sguw, sgub, w_bot, ng, nb)


CONV_C = D // 2
TILE_ROWS = TOKEN_TILE // GRID_W


def _grid_transpose_in(v, o_ref):
    perm = _chunk_transpose_perm()
    for q in range(GRID_W // S5_T):
        seg = jnp.concatenate([v[GRID_W * r + S5_T * q:GRID_W * r + S5_T * (q + 1), :] for r in range(TILE_ROWS)],
                              axis=0)
        t = jnp.dot(perm, seg, preferred_element_type=F32).astype(BF16)
        o_ref[S5_T * q:S5_T * (q + 1), :, :] = t.reshape(S5_T, TILE_ROWS, v.shape[1])


def _inproj1_kernel(x_ref, mod_ref, w_ref, hgr_ref, hgc_ref, h_ref):
    shift = mod_ref[:, 0:D]
    scale = mod_ref[:, D:2 * D]
    h = (x_ref[...] * (1.0 + scale) + shift).astype(BF16)
    h_ref[...] = h
    dot = lambda lo: jnp.dot(h, w_ref[:, lo:lo + CONV_C].astype(BF16), preferred_element_type=F32)
    hgr_ref[...] = (dot(0) * _sigmoid(dot(D))).astype(BF16)
    _grid_transpose_in((dot(CONV_C) * _sigmoid(dot(D + CONV_C))).astype(BF16), hgc_ref)


def _inproj1(x, mod, w_in_f32, tm=TOKEN_TILE):
    tile = lambda w: pl.BlockSpec((None, tm, w), lambda b, i: (b, i, 0))
    return pl.pallas_call(
        _inproj1_kernel,
        out_shape=(jax.ShapeDtypeStruct((B, L, CONV_C), BF16),
                   jax.ShapeDtypeStruct((B, GRID_W, GRID_W, CONV_C), BF16),
                   jax.ShapeDtypeStruct((B, L, D), BF16)),
        grid=(B, L // tm),
        in_specs=[tile(D),
                  pl.BlockSpec((None, 1, 3 * D), lambda b, i: (b, 0, 0)),
                  pl.BlockSpec((D, 2 * D), lambda b, i: (0, 0), pipeline_mode=pl.Buffered(1))],
        out_specs=(tile(CONV_C), pl.BlockSpec((None, GRID_W, TILE_ROWS, CONV_C), lambda b, i: (b, 0, i, 0)),
                   tile(D)),
        compiler_params=_params("arbitrary", "arbitrary"),
        name="inproj1",
    )(x, mod, w_in_f32)


DFT_N = 2 * GRID_W
TAPS_PAD = CONV_K + 1


def _dft_constants():
    th = 2.0 * math.pi / DFT_N
    f = np.arange(GRID_W, dtype=np.float64)[:, None]
    p = np.arange(GRID_W, dtype=np.float64)[None, :]
    cosm = np.cos(th * f * p)
    sinm = np.sin(th * f * p)
    alt = np.where(np.arange(GRID_W) % 2 == 0, 1.0, -1.0)
    fwd = np.concatenate([cosm, alt[None, :], sinm[1:]], axis=0)
    cf = np.where(np.arange(GRID_W) == 0, 1.0, 2.0) / DFT_N
    inv = np.concatenate([cosm.T * cf[None, :], (alt / DFT_N)[:, None], sinm.T[:, 1:] * (2.0 / DFT_N)], axis=1)
    sft = (CONV_HALF - np.arange(TAPS_PAD, dtype=np.float64))[None, :]
    live = (np.arange(TAPS_PAD) < CONV_K).astype(np.float64)[None, :]
    f64 = np.where(f == 0, float(GRID_W), f)
    f32 = lambda a: jnp.asarray(a.astype(np.float32))
    return (f32(fwd).astype(BF16), f32(inv).astype(BF16),
            f32(np.cos(th * f * sft) * live), f32(np.sin(th * f * sft) * live), f32(np.cos(th * f64 * sft) * live))


def _fconv_kernel(h_ref, w_ref, b_ref, fwd_ref, inv_ref, c1_ref, s3_ref, c4_ref, o_ref):
    hp = lax.Precision.HIGHEST
    taps = w_ref[...]
    g_re = jnp.dot(c1_ref[...], taps, preferred_element_type=F32, precision=hp)
    g_im = jnp.dot(s3_ref[...], taps, preferred_element_type=F32, precision=hp)
    g_r2 = jnp.dot(c4_ref[...], taps, preferred_element_type=F32, precision=hp)
    fwd = fwd_ref[...]
    inv = inv_ref[...]
    bias = b_ref[...]
    n_runs = h_ref.shape[0] // GRID_W
    rows = lambda r: slice(GRID_W * r, GRID_W * (r + 1))
    forward = lambda r: jnp.dot(fwd, h_ref[rows(r), :], preferred_element_type=F32)
    ahead = 2
    specs = [forward(r) for r in range(ahead)]
    for r in range(n_runs):
        if r + ahead < n_runs:
            specs.append(forward(r + ahead))
        spec = specs[r]
        a, bm = spec[0:GRID_W], spec[GRID_W:DFT_N]
        prod = jnp.concatenate([a * g_re - bm * g_im, a * g_im + bm * g_r2], axis=0).astype(BF16)
        o_ref[rows(r), :] = (jnp.dot(inv, prod, preferred_element_type=F32) + bias).astype(BF16)


def _fconv(h, taps, bias, half, consts, tm=L):
    fwd, inv, c1, s3, c4 = consts
    c = h.shape[-1]
    tile = pl.BlockSpec((None, tm, c), lambda b, i: (b, i, 0))
    full = lambda *s: pl.BlockSpec(s, lambda b, i: (0,) * len(s))
    cols = lambda r: pl.BlockSpec((r, c), lambda b, i: (0, half))
    return pl.pallas_call(
        _fconv_kernel,
        out_shape=jax.ShapeDtypeStruct(h.shape, BF16),
        grid=(B, L // tm),
        in_specs=[tile, cols(TAPS_PAD), cols(1), full(DFT_N, GRID_W), full(GRID_W, DFT_N),
                  full(GRID_W, TAPS_PAD), full(GRID_W, TAPS_PAD), full(GRID_W, TAPS_PAD)],
        out_specs=tile,
        compiler_params=_params("arbitrary", "arbitrary"),
        name="fconv",
    )(h, taps, bias, fwd, inv, c1, s3, c4)


ROW_BLOCK = 32


def _row_blocks(n_rows):
    return [slice(ROW_BLOCK * k, ROW_BLOCK * (k + 1)) for k in range(n_rows // ROW_BLOCK)]


def _tail1_kernel(x_ref, hcr_ref, hcc_ref, h1_ref, wz_ref, mod_ref, lng_ref, lnb_ref, wout_ref, ng_ref, nb_ref,
                  o_ref, col_ref, z_ref, m_ref, y_ref):
    tm = x_ref.shape[0]
    perm = _chunk_transpose_perm()
    for q in range(GRID_W // S5_T):
        blk = hcc_ref[S5_T * q:S5_T * (q + 1), :, :].reshape(PERM_ROWS, CONV_C)
        t = jnp.dot(perm, blk, preferred_element_type=F32)
        for r in range(TILE_ROWS):
            col_ref[GRID_W * r + S5_T * q:GRID_W * r + S5_T * (q + 1), :] = t[S5_T * r:S5_T * (r + 1), :]
    z_ref[...] = jnp.dot(h1_ref[...], wz_ref[...], preferred_element_type=F32)
    lng, lnb = lng_ref[...], lnb_ref[...]
    for rows in _row_blocks(tm):
        hc = jnp.concatenate([hcr_ref[rows, :].astype(F32), col_ref[rows, :]], axis=1)
        m_ref[rows, :] = (_silu_of_half(_layer_norm(hc, lng, lnb)) * _silu_of_half(z_ref[rows, :])).astype(BF16)
    y_ref[...] = jnp.dot(m_ref[...], wout_ref[...], preferred_element_type=F32)
    gmod = mod_ref[:, 2 * D:3 * D]
    ng, nb = ng_ref[...], nb_ref[...]
    for rows in _row_blocks(tm):
        o_ref[rows, :] = _layer_norm(DN_ALPHA * x_ref[rows, :] + gmod * y_ref[rows, :], ng, nb)


def _tail1(x, hc_row, hc_col, h1, w_z_half, mod, ln_g_half, ln_b_half, w_out, ng, nb, tm=TOKEN_TILE):
    tile = lambda w: pl.BlockSpec((None, tm, w), lambda b, i: (b, i, 0))
    full = lambda *s: pl.BlockSpec(s, lambda b, i: (0,) * len(s))
    return pl.pallas_call(
        _tail1_kernel,
        out_shape=jax.ShapeDtypeStruct((B, L, D), F32),
        grid=(B, L // tm),
        in_specs=[tile(D), tile(CONV_C),
                  pl.BlockSpec((None, GRID_W, TILE_ROWS, CONV_C), lambda b, i: (b, 0, i, 0)),
                  tile(D), full(D, D), pl.BlockSpec((None, 1, 3 * D), lambda b, i: (b, 0, 0)),
                  full(1, D), full(1, D), full(D, D), full(1, D), full(1, D)],
        out_specs=tile(D),
        scratch_shapes=[pltpu.VMEM((tm, CONV_C), F32), pltpu.VMEM((tm, D), F32), pltpu.VMEM((tm, D), BF16),
                        pltpu.VMEM((tm, D), F32)],
        compiler_params=_params("arbitrary", "arbitrary"),
        name="tail1",
    )(x, hc_row, hc_col, h1, w_z_half, mod, ln_g_half, ln_b_half, w_out, ng, nb)


def kernel(x, c, ctx, c_ctx, mod_w, mod_b, norm_g, norm_b, ev_w_in, ev_w_out, s5_lam_re, s5_lam_im, s5_log_dt, s5_b_re, s5_b_im, s5_c_re, s5_c_im, s5_d, glu_w, glu_b, sgu_ln_g, sgu_ln_b, sgu_w, sgu_b, od_w_in, od_w_out, dw_w, dw_b, conv_ln_g, conv_ln_b):
    TH = S5_T * S5_H
    row = lambda v: v.reshape(1, -1)

    cond8 = jnp.concatenate([c, c_ctx[None], jnp.zeros((3, D), F32)], axis=0)
    mods = _adaln(cond8, mod_w, mod_b)
    mod0 = mods[0, :B].reshape(B, 1, 3 * D)
    mod0c = mods[0, B:B + 1]
    mod1 = mods[1, :B].reshape(B, 1, 3 * D)

    ldt = jnp.broadcast_to(s5_log_dt[0][:, :, None], (2, S5_G, S5_P))
    lam3 = jnp.transpose(jnp.stack([s5_lam_re[0], s5_lam_im[0], ldt]), (0, 2, 1, 3)).reshape(3, S5_G, 2 * S5_P)
    bt = jnp.transpose(jnp.stack([s5_b_re[0], s5_b_im[0]]), (0, 2, 4, 1, 3)).reshape(2, S5_G, S5_H, 2 * S5_P)
    cn = jnp.transpose(jnp.stack([s5_c_re[0], s5_c_im[0]]), (0, 2, 3, 1, 4)).reshape(2, S5_G, S5_H, 2 * S5_P)
    d_row = jnp.tile(s5_d[0].reshape(S5_G, 1, S5_H), (1, 1, S5_T))
    win, wout, mix, l16 = _s5_weights(lam3, bt, cn, d_row)

    w_out0 = ev_w_out[0].astype(BF16)
    glu_w0 = glu_w[0].astype(BF16)
    guz, vln, hs = _inproj0n(x, mod0, ev_w_in[0], row(sgu_ln_g[0]), row(sgu_ln_b[0]))
    ua, sza, ua_c = _inproj0a(hs, _ctx_slabs(ctx, mod0c), ev_w_in[0])
    s_lat = _s5core(ua, ua_c, win, wout, mix, l16)
    y_s5 = _s5tail(s_lat, sza, glu_w0, row(glu_b[0]), w_out0)
    sguw = sgu_w[0].reshape(SGU_HEADS // 2, 2, SGU_CHUNK, SGU_CHUNK)
    sguw = jnp.transpose(sguw, (0, 2, 1, 3)).reshape(SGU_HEADS // 2, SGU_CHUNK, 2 * SGU_CHUNK).astype(BF16)
    sgub = jnp.repeat(sgu_b[0].T, SGU_HD, axis=1)
    x1 = _tail0(x, y_s5, guz, vln, mod0, sguw, sgub, w_out0, row(norm_g[0]), row(norm_b[0]))

    hg_row, hg_col, h1 = _inproj1(x1, mod1, od_w_in[0])
    consts = _dft_constants()
    taps = jnp.pad(dw_w[0], ((0, TAPS_PAD - CONV_K), (0, 0)))
    bias = row(dw_b[0])
    hc_row = _fconv(hg_row, taps, bias, 0, consts)
    hc_col = _fconv(hg_col.reshape(B, L, CONV_C), taps, bias, 1, consts)
    w_z_half = (0.5 * od_w_in[0][:, 2 * D:]).astype(BF16)
    return _tail1(x1, hc_row, hc_col.reshape(B, GRID_W, GRID_W, CONV_C), h1, w_z_half, mod1,
                  row(0.5 * conv_ln_g[0]), row(0.5 * conv_ln_b[0]), od_w_out[0].astype(BF16),
                  row(norm_g[1]), row(norm_b[1]))
```

```python
import functools
import math

import jax
import jax.numpy as jnp
import numpy as np
from jax import lax
from jax.experimental import pallas as pl
from jax.experimental.pallas import tpu as pltpu

D = 1024
B = 4
L = 4096
CTX = 256
GRID_W = 64
S5_W = 512
S5_G = 32
S5_H = 16
H_SHIFT = 4
BLK = 128 // S5_H
S5_P = 64
S5_T = 16
SGU_W = 512
SGU_HEADS = 8
SGU_HD = 64
SGU_CHUNK = 128
CONV_K = 31
CONV_HALF = CONV_K // 2
EVEN_IN = 2560
SGU_COL0 = 2 * S5_W
ODD_IN = 3072
DEPTH = 2
DN_ALPHA = (2 * DEPTH) ** 0.25
LN_EPS = 1e-5
N_CHUNK = L // S5_T
N_CCHUNK = CTX // S5_T
VMEM_LIMIT_V7X = 56 * 1024 * 1024
TOKEN_TILE = 1024

F32 = jnp.float32
BF16 = jnp.bfloat16


GELU_C = math.sqrt(2.0 / math.pi)


def _gelu(x):
    hx = 0.5 * x
    return hx * jnp.tanh(x * ((x * x) * (0.044715 * GELU_C) + GELU_C)) + hx


def _sigmoid(x):
    return 0.5 * jnp.tanh(0.5 * x) + 0.5


def _silu_of_half(hx):
    return hx * jnp.tanh(hx) + hx


def _silu(x):
    return _silu_of_half(0.5 * x)


def _layer_norm(x, g, b):
    mu = jnp.mean(x, axis=-1, keepdims=True)
    xc = x - mu
    var = jnp.mean(xc * xc, axis=-1, keepdims=True)
    return xc * lax.rsqrt(var + LN_EPS) * g + b


def _params(*sem):
    return pltpu.CompilerParams(dimension_semantics=sem, vmem_limit_bytes=VMEM_LIMIT_V7X)


def _adaln_kernel(c_ref, w_ref, b_ref, o_ref):
    def split(v):
        hi = v.astype(BF16)
        return hi, (v - hi.astype(F32)).astype(BF16)

    a_hi, a_lo = split(_silu(c_ref[...]))
    w_hi, w_lo = split(w_ref[...])
    dot = functools.partial(jnp.dot, preferred_element_type=F32)
    mod = dot(a_hi, w_hi) + dot(a_lo, w_hi) + dot(a_hi, w_lo) + b_ref[pl.ds(pl.program_id(0), 1), :]
    for r in range(mod.shape[0]):
        o_ref[r] = mod[r:r + 1, :]


def _adaln(cond8, mod_w, mod_b):
    tn = 1024
    return pl.pallas_call(
        _adaln_kernel,
        out_shape=jax.ShapeDtypeStruct((DEPTH, 8, 1, 3 * D), F32),
        grid=(DEPTH, 3 * D // tn),
        in_specs=[pl.BlockSpec((8, D), lambda l, j: (0, 0)),
                  pl.BlockSpec((None, D, tn), lambda l, j: (l, 0, j)),
                  pl.BlockSpec((DEPTH, tn), lambda l, j: (0, j))],
        out_specs=pl.BlockSpec((None, 8, 1, tn), lambda l, j: (l, 0, 0, j)),
        compiler_params=_params("arbitrary", "arbitrary"),
        name="adaln",
    )(cond8, mod_w, mod_b)


def _mod_spec(layer, cond=None):
    if cond is None:
        return pl.BlockSpec((None, None, 1, 3 * D), lambda b, i: (layer, b, 0, 0))
    return pl.BlockSpec((None, None, 1, 3 * D), lambda b: (layer, cond, 0, 0))


S5W_GROUPS = BLK


def _cpow(base_pows, j):
    re = None
    im = None
    for k, (pr, pi) in enumerate(base_pows):
        bit = ((j >> k) & 1) == 1
        mr = jnp.where(bit, pr, 1.0)
        mi = jnp.where(bit, pi, 0.0)
        if re is None:
            re, im = mr, mi
        else:
            re, im = re * mr - im * mi, re * mi + im * mr
    return re, im


def _squarings(pr, pi, n):
    out = [(pr, pi)]
    for _ in range(n - 1):
        pr, pi = pr * pr - pi * pi, 2.0 * pr * pi
        out.append((pr, pi))
    return out


def _shift_lanes(x, n):
    lane = lax.broadcasted_iota(jnp.int32, (S5_H, 128), 1)
    lo, hi = x[:, :128], x[:, 128:]
    if n == 0:
        return x
    if n < 128:
        rlo = pltpu.roll(lo, n, axis=1)
        rhi = pltpu.roll(hi, n, axis=1)
        return jnp.concatenate([jnp.where(lane >= n, rlo, 0.0), jnp.where(lane >= n, rhi, rlo)], axis=1)
    m = n - 128
    rlo = lo if m == 0 else pltpu.roll(lo, m, axis=1)
    return jnp.concatenate([jnp.zeros_like(lo), jnp.where(lane >= m, rlo, 0.0)], axis=1)


def _unshift_lanes(x, n):
    lane = lax.broadcasted_iota(jnp.int32, (S5_H, 128), 1)
    lo, hi = x[:, :128], x[:, 128:]
    if n == 0:
        return x
    if n < 128:
        rlo = pltpu.roll(lo, 128 - n, axis=1)
        rhi = pltpu.roll(hi, 128 - n, axis=1)
        keep = lane < 128 - n
        return jnp.concatenate([jnp.where(keep, rlo, rhi), jnp.where(keep, rhi, 0.0)], axis=1)
    m = n - 128
    rhi = hi if m == 0 else pltpu.roll(hi, 128 - m, axis=1)
    return jnp.concatenate([jnp.where(lane < 128 - m, rhi, 0.0), jnp.zeros_like(lo)], axis=1)


def _s5w_group(gi, bg, disc, d_t, bt_ref, cn_ref, win_ref, wout_ref, mix_ref, l16_ref):
    TH = S5_T * S5_H

    def chunk_pos(idx):
        return (((idx >> H_SHIFT) - bg) & (BLK - 1)) + ((idx >> 7) << 3)

    lr, li, cr, ci = [v[gi:gi + 1] for v in disc]
    pows_row = _squarings(lr, li, 5)
    l16_ref[gi, 0:1, :] = pows_row[4][0]
    l16_ref[gi, 1:2, :] = pows_row[4][1]
    l16_ref[gi, 2:8, :] = jnp.zeros((6, 128), F32)
    btr = bt_ref[0, gi]
    bti = bt_ref[1, gi]
    bbr = cr * btr - ci * bti
    bbi = cr * bti + ci * btr
    blk16 = lax.broadcasted_iota(jnp.int32, (S5_T, 128), 0)
    is_f16 = lax.broadcasted_iota(jnp.int32, (S5_T, 128), 1) < S5_P
    pos16 = chunk_pos(blk16 << H_SHIFT)
    pr16, pi16 = _cpow(pows_row[:4], jnp.where(is_f16, S5_T - 1 - pos16, pos16))
    rep_rows = lambda v: jnp.broadcast_to(v[:, None, :], (S5_T, S5_H, 128)).reshape(TH, 128)
    pr, pi = rep_rows(pr16), rep_rows(pi16)
    tbr = jnp.broadcast_to(bbr[None], (S5_T, S5_H, 128)).reshape(TH, 128)
    tbi = jnp.broadcast_to(bbi[None], (S5_T, S5_H, 128)).reshape(TH, 128)
    win_ref[gi, :, 0:128] = (pr * tbr - pi * tbi).astype(BF16)
    win_ref[gi, :, 128:256] = (pr * tbi + pi * tbr).astype(BF16)

    hp = lax.Precision.HIGHEST
    dot = functools.partial(jnp.dot, preferred_element_type=F32, precision=hp)
    def col256(r):
        col = jnp.broadcast_to(r, (2 * S5_P, 2 * S5_P)).T
        return jnp.concatenate([col, col], axis=1)

    def tiled_t(cn):
        t8 = jnp.broadcast_to(cn[None], (BLK, S5_H, 2 * S5_P)).reshape(2 * S5_P, 2 * S5_P).T
        return jnp.concatenate([t8, t8], axis=1)

    cpows = _squarings(col256(lr), col256(li), 4)
    row = lax.broadcasted_iota(jnp.int32, (2 * S5_P, TH), 0)
    lane_w = lax.broadcasted_iota(jnp.int32, (2 * S5_P, TH), 1)
    t_idx = chunk_pos(lane_w)
    j_idx = lane_w >> H_SHIFT
    is_f = row < S5_P
    ctr = tiled_t(cn_ref[0, gi])
    cti = tiled_t(cn_ref[1, gi])
    er, ei = _cpow(cpows, jnp.where(is_f, t_idx, S5_T - 1 - t_idx))
    er, ei = er * cpows[0][0] - ei * cpows[0][1], er * cpows[0][1] + ei * cpows[0][0]
    wr = ctr * er - cti * ei
    wi = ctr * ei + cti * er
    wout_ref[gi, 0:128, :] = wr.astype(BF16)
    wout_ref[gi, 128:256, :] = (-wi).astype(BF16)
    kr, ki = _cpow(cpows, jnp.where(is_f, j_idx, S5_T - 1 - j_idx))
    ekr = ctr * kr - cti * ki
    eki = ctr * ki + cti * kr
    lane16 = lax.broadcasted_iota(jnp.int32, (S5_H, 128), 1)
    mf = lane16 < S5_P
    kkf = dot(jnp.where(mf, bbr, 0.0), ekr) - dot(jnp.where(mf, bbi, 0.0), eki)
    kkb = dot(jnp.where(mf, 0.0, bbr), ekr) - dot(jnp.where(mf, 0.0, bbi), eki)
    d_rows = d_t[S5_H * gi:S5_H * (gi + 1), :]
    dl = jnp.concatenate([d_rows, d_rows], axis=1)
    r16 = lax.broadcasted_iota(jnp.int32, (S5_H, TH), 0)
    l256 = lax.broadcasted_iota(jnp.int32, (S5_H, TH), 1)
    rot = bg * S5_H
    for s in range(S5_T):
        blk = _shift_lanes(kkf, S5_H * s) + _unshift_lanes(kkb, S5_H * (S5_T - 1 - s))
        blk = blk + jnp.where(l256 == r16 + S5_H * s, dl, 0.0)
        if rot:
            blk = jnp.concatenate([pltpu.roll(blk[:, :128], rot, axis=1), pltpu.roll(blk[:, 128:], rot, axis=1)], axis=1)
        rho = ((s + bg) & (BLK - 1)) + (s & BLK)
        mix_ref[gi, S5_H * rho:S5_H * (rho + 1), :] = blk.astype(BF16)


def _s5w_kernel(lam_ref, d_ref, *refs):
    d_t = jnp.broadcast_to(d_ref[...], (S5W_GROUPS * S5_H, S5W_GROUPS * S5_H)).T
    lr = lam_ref[0]
    li = lam_ref[1]
    dt = jnp.exp(lam_ref[2])
    mag = jnp.exp(lr * dt)
    br = mag * jnp.cos(li * dt)
    bi = mag * jnp.sin(li * dt)
    inv = 1.0 / (lr * lr + li * li)
    nr = br - 1.0
    disc = (br, bi, (nr * lr + bi * li) * inv, (bi * lr - nr * li) * inv)
    for gi in range(S5W_GROUPS):
        _s5w_group(gi, gi % BLK, disc, d_t, *refs)


def _s5_weights(lam3, d, bt, cn):
    TH = S5_T * S5_H
    g3 = lambda r, c: pl.BlockSpec((S5W_GROUPS, r, c), lambda g: (g, 0, 0))
    ri = pl.BlockSpec((2, S5W_GROUPS, S5_H, 2 * S5_P), lambda g: (0, g, 0, 0))
    wshape = jax.ShapeDtypeStruct((S5_G, TH, TH), BF16)
    return pl.pallas_call(
        _s5w_kernel,
        out_shape=(wshape, wshape, wshape, jax.ShapeDtypeStruct((S5_G, 8, 128), F32)),
        grid=(S5_G // S5W_GROUPS,),
        in_specs=[pl.BlockSpec((3, S5W_GROUPS, 2 * S5_P), lambda g: (0, g, 0)),
                  pl.BlockSpec((1, S5W_GROUPS * S5_H), lambda g: (0, g)), ri, ri],
        out_specs=(g3(TH, TH), g3(TH, TH), g3(TH, TH), g3(8, 128)),
        compiler_params=_params("arbitrary"),
        name="s5_weights",
    )(lam3, d, bt, cn)


def _rot_blocks(v, r):
    cols = [pltpu.roll(v[:, 128 * q:128 * (q + 1)], S5_H * r, axis=1) for q in range(v.shape[1] // 128)]
    return jnp.concatenate(cols, axis=1)


def _slabs_of(h, hs_ref):
    h3 = h.reshape(h.shape[0] // S5_T, S5_T, h.shape[1])
    for s in range(S5_T):
        hs_ref[s] = h3[:, s, :].astype(BF16)


PERM_ROWS = S5_T * S5_T


def _chunk_transpose_perm():
    ri = lax.broadcasted_iota(jnp.int32, (PERM_ROWS, PERM_ROWS), 0)
    ci = lax.broadcasted_iota(jnp.int32, (PERM_ROWS, PERM_ROWS), 1)
    hit = ((ri >> H_SHIFT) == (ci & (S5_T - 1))) & ((ri & (S5_T - 1)) == (ci >> H_SHIFT))
    return jnp.where(hit, 1.0, 0.0).astype(BF16)


def _inproj0n_kernel(x_ref, mod_ref, w_ref, lng_ref, lnb_ref, guz_ref, vln_ref, hs_ref):
    shift = mod_ref[:, 0:D]
    scale = mod_ref[:, D:2 * D]
    hb = (x_ref[...] * (1.0 + scale) + shift).astype(BF16)
    perm = _chunk_transpose_perm()
    for j in range(hb.shape[0] // PERM_ROWS):
        blk = jnp.dot(perm, hb[PERM_ROWS * j:PERM_ROWS * (j + 1), :], preferred_element_type=F32).astype(BF16)
        for s in range(S5_T):
            hs_ref[s, S5_T * j:S5_T * (j + 1), :] = blk[S5_T * s:S5_T * (s + 1), :]
    dot = lambda lo: jnp.dot(hb, w_ref[:, SGU_COL0 + lo:SGU_COL0 + lo + 512].astype(BF16),
                             preferred_element_type=F32)
    guz_ref[...] = (_gelu(dot(0)) * _silu(dot(1024))).astype(BF16)
    vln_ref[...] = _layer_norm(_gelu(dot(512)), lng_ref[...], lnb_ref[...]).astype(BF16)


def _inproj0n(x, mod, w_in_f32, ln_g, ln_b, tm=TOKEN_TILE):
    nct = tm // S5_T
    o = jax.ShapeDtypeStruct((B, L, 512), BF16)
    ospec = pl.BlockSpec((None, tm, 512), lambda b, i: (b, i, 0))
    full = lambda *s: pl.BlockSpec(s, lambda b, i: (0,) * len(s))
    return pl.pallas_call(
        _inproj0n_kernel,
        out_shape=(o, o, jax.ShapeDtypeStruct((S5_T, B * N_CHUNK, D), BF16)),
        grid=(B, L // tm),
        in_specs=[pl.BlockSpec((None, tm, D), lambda b, i: (b, i, 0)),
                  _mod_spec(0),
                  pl.BlockSpec((D, EVEN_IN), lambda b, i: (0, 0), pipeline_mode=pl.Buffered(1)),
                  full(1, 512), full(1, 512)],
        out_specs=(ospec, ospec,
                   pl.BlockSpec((S5_T, nct, D), lambda b, i: (0, b * (N_CHUNK // nct) + i, 0))),
        compiler_params=_params("arbitrary", "arbitrary"),
        name="inproj0n",
    )(x, mod, w_in_f32, ln_g, ln_b)


def _ctx_slabs_kernel(x_ref, mod_ref, hs_ref):
    h = x_ref[...] * (1.0 + mod_ref[:, D:2 * D]) + mod_ref[:, 0:D]
    _slabs_of(h, hs_ref)


def _ctx_slabs(ctx, mod_c):
    return pl.pallas_call(
        _ctx_slabs_kernel,
        out_shape=jax.ShapeDtypeStruct((S5_T, B * N_CCHUNK, D), BF16),
        grid=(B,),
        in_specs=[pl.BlockSpec((None, CTX, D), lambda b: (b, 0, 0)),
                  _mod_spec(0, cond=B)],
        out_specs=pl.BlockSpec((S5_T, N_CCHUNK, D), lambda b: (0, b, 0)),
        compiler_params=_params("arbitrary"),
        name="ctx_slabs",
    )(ctx, mod_c)


def _inproj0a_kernel(hs_ref, hc_ref, w_ref, ua_ref, sza_ref, uc_ref):
    r = pl.program_id(0)
    h = hs_ref[...]
    w_ua = w_ref[:, 0:512].astype(BF16)
    ua_ref[...] = _rot_blocks(jnp.dot(h, w_ua, preferred_element_type=F32), r).astype(BF16)
    sza_ref[...] = _silu(jnp.dot(h, w_ref[:, 512:1024].astype(BF16), preferred_element_type=F32)).astype(BF16)
    uc_ref[...] = _rot_blocks(jnp.dot(hc_ref[...], w_ua, preferred_element_type=F32), r).astype(BF16)


def _inproj0a(hs, hcs, w_in_f32):
    slab = lambda r, h: r + BLK * h
    sspec = lambda n, w: pl.BlockSpec((None, n, w), lambda r, h: (slab(r, h), 0, 0))
    so = lambda n: jax.ShapeDtypeStruct((S5_T, n, 512), BF16)
    nl, ncx = B * N_CHUNK, B * N_CCHUNK
    return pl.pallas_call(
        _inproj0a_kernel,
        out_shape=(so(nl), so(nl), so(ncx)),
        grid=(BLK, S5_T // BLK),
        in_specs=[sspec(nl, D), sspec(ncx, D), pl.BlockSpec((D, SGU_COL0), lambda r, h: (0, 0))],
        out_specs=(sspec(nl, 512), sspec(nl, 512), sspec(ncx, 512)),
        compiler_params=_params("arbitrary", "arbitrary"),
        name="inproj0a",
    )(hs, hcs, w_in_f32)


SCAN_GROUPS = 4


def _scan_tiles(sre_ref, sim_ref, h_refs, n_tiles, carry, lams):
    row = lax.broadcasted_iota(jnp.int32, (8, 128), 0)
    lane = lax.broadcasted_iota(jnp.int32, (8, 128), 1)
    first = row < B
    fwd = lane < S5_P

    def body(k, c):
        of = pl.multiple_of(k * 8, 8)
        ob = pl.multiple_of((n_tiles - 1 - k) * 8, 8)
        out = []
        for gi in range(SCAN_GROUPS):
            lre, lim = lams[gi]
            hr, hi = c[2 * gi], c[2 * gi + 1]
            sr = jnp.where(fwd, sre_ref[gi, pl.ds(of, 8), :], pltpu.roll(sre_ref[gi, pl.ds(ob, 8), :], B, axis=0))
            si = jnp.where(fwd, sim_ref[gi, pl.ds(of, 8), :], pltpu.roll(sim_ref[gi, pl.ds(ob, 8), :], B, axis=0))
            h1r = lre * hr - lim * hi + sr
            h1i = lre * hi + lim * hr + si
            r1r = pltpu.roll(h1r, B, axis=0)
            r1i = pltpu.roll(h1i, B, axis=0)
            if h_refs is not None:
                fre_ref, fim_ref, bre_ref, bim_ref = h_refs
                er = jnp.where(first, hr, r1r)
                ei = jnp.where(first, hi, r1i)
                fre_ref[gi, pl.ds(of, 8), :] = er
                fim_ref[gi, pl.ds(of, 8), :] = ei
                bre_ref[gi, pl.ds(ob, 8), :] = pltpu.roll(er, B, axis=0)
                bim_ref[gi, pl.ds(ob, 8), :] = pltpu.roll(ei, B, axis=0)
            h2r = lre * r1r - lim * r1i + sr
            h2i = lre * r1i + lim * r1r + si
            out.append(jnp.where(first, pltpu.roll(h2r, B, axis=0), h2r))
            out.append(jnp.where(first, pltpu.roll(h2i, B, axis=0), h2i))
        return tuple(out)

    return lax.fori_loop(0, n_tiles, body, carry)


def _gather_group(slab_ref, src):
    halves = []
    for h in range(S5_T // BLK):
        acc = slab_ref[BLK * h]
        for s in range(1, BLK):
            acc = jnp.where(src == s, slab_ref[BLK * h + s], acc)
        halves.append(acc)
    return jnp.concatenate(halves, axis=1)


def _s5core_kernel(ul_ref, uc_ref, win_ref, wout_ref, mix_ref, l16_ref, o_ref,
                   u_ref, sre_ref, sim_ref, cre_ref, cim_ref, fre_ref, fim_ref, bre_ref, bim_ref, y_ref):
    nl = N_CHUNK * B
    ncx = N_CCHUNK * B
    blk_l = lax.broadcasted_iota(jnp.int32, (nl, 128), 1) >> H_SHIFT
    blk_c = lax.broadcasted_iota(jnp.int32, (ncx, 128), 1) >> H_SHIFT
    fwd = lax.broadcasted_iota(jnp.int32, (N_CHUNK, 128), 1) < S5_P
    for g0 in range(0, BLK, SCAN_GROUPS):
        for gi in range(SCAN_GROUPS):
            bg = g0 + gi
            win = win_ref[bg]
            src_l = ((blk_l - bg) & (BLK - 1)).astype(F32).astype(BF16)
            src_c = ((blk_c - bg) & (BLK - 1)).astype(F32).astype(BF16)
            u = _gather_group(ul_ref, src_l)
            u_ref[gi] = u
            sl = jnp.dot(u, win, preferred_element_type=F32)
            sc = jnp.dot(_gather_group(uc_ref, src_c), win, preferred_element_type=F32)
            for b in range(B):
                sre_ref[gi, pl.ds(b, N_CHUNK, stride=B), :] = sl[N_CHUNK * b:N_CHUNK * (b + 1), 0:128]
                sim_ref[gi, pl.ds(b, N_CHUNK, stride=B), :] = sl[N_CHUNK * b:N_CHUNK * (b + 1), 128:256]
                cre_ref[gi, pl.ds(b, N_CCHUNK, stride=B), :] = sc[N_CCHUNK * b:N_CCHUNK * (b + 1), 0:128]
                cim_ref[gi, pl.ds(b, N_CCHUNK, stride=B), :] = sc[N_CCHUNK * b:N_CCHUNK * (b + 1), 128:256]
        lams = [(jnp.broadcast_to(l16_ref[g0 + gi, 0:1, :], (8, 128)),
                 jnp.broadcast_to(l16_ref[g0 + gi, 1:2, :], (8, 128))) for gi in range(SCAN_GROUPS)]
        zero = tuple(jnp.zeros((8, 128), F32) for _ in range(2 * SCAN_GROUPS))
        carry = _scan_tiles(cre_ref, cim_ref, None, ncx // 8, zero, lams)
        _scan_tiles(sre_ref, sim_ref, (fre_ref, fim_ref, bre_ref, bim_ref), nl // 8, carry, lams)
        for gi in range(SCAN_GROUPS):
            bg = g0 + gi
            y = jnp.dot(u_ref[gi], mix_ref[bg], preferred_element_type=F32)
            hs = []
            for b in range(B):
                rows = pl.ds(b, N_CHUNK, stride=B)
                hs.append(jnp.concatenate([jnp.where(fwd, fre_ref[gi, rows, :], bre_ref[gi, rows, :]),
                                           jnp.where(fwd, fim_ref[gi, rows, :], bim_ref[gi, rows, :])], axis=1))
            hcat = jnp.concatenate(hs, axis=0).astype(BF16)
            y = y + jnp.dot(hcat, wout_ref[bg], preferred_element_type=F32)
            y_ref[bg] = y.astype(BF16)

    blk = blk_l.astype(F32).astype(BF16)
    for s in range(S5_T):
        h, r = s // BLK, s % BLK
        acc = None
        for j in range(BLK):
            piece = y_ref[(j - r) % BLK, :, 128 * h:128 * (h + 1)]
            acc = piece if acc is None else jnp.where(blk == j, piece, acc)
        o_ref[s] = acc


def _s5core(ul, uc, win, wout, mix, l16):
    TH = S5_T * S5_H
    nl = N_CHUNK * B
    ncx = N_CCHUNK * B
    g4 = lambda r, c: pl.BlockSpec((BLK, r, c), lambda q: (q, 0, 0))
    col = lambda n: pl.BlockSpec((S5_T, n, 128), lambda q: (0, 0, q))
    f32s = lambda n: pltpu.VMEM((SCAN_GROUPS, n, 128), F32)
    return pl.pallas_call(
        _s5core_kernel,
        out_shape=jax.ShapeDtypeStruct((S5_T, nl, S5_W), BF16),
        grid=(S5_G // BLK,),
        in_specs=[col(nl), col(ncx), g4(TH, TH), g4(TH, TH), g4(TH, TH), g4(8, 128)],
        out_specs=col(nl),
        scratch_shapes=[pltpu.VMEM((SCAN_GROUPS, nl, TH), BF16),
                        f32s(nl), f32s(nl), f32s(ncx), f32s(ncx), f32s(nl), f32s(nl), f32s(nl), f32s(nl),
                        pltpu.VMEM((BLK, nl, TH), BF16)],
        compiler_params=_params("arbitrary"),
        name="s5core",
    )(ul, uc, win, wout, mix, l16)


def _s5tail_kernel(slat_ref, sza_ref, gluw_ref, glub_ref, wtop_ref, y_ref):
    unrot = (BLK - pl.program_id(0)) & (BLK - 1)
    for b in range(B):
        rows = slice(N_CHUNK * b, N_CHUNK * (b + 1))
        g = _gelu(_rot_blocks(slat_ref[rows, :].astype(F32), unrot))
        gate = _sigmoid(jnp.dot(g.astype(BF16), gluw_ref[...], preferred_element_type=F32) + glub_ref[...])
        a = (g * gate * sza_ref[rows, :].astype(F32)).astype(BF16)
        y_ref[rows, :] = jnp.dot(a, wtop_ref[...], preferred_element_type=F32).astype(BF16)


def _s5tail(slat, sza, glu_w, glu_b, w_top):
    slab = lambda r, h: r + BLK * h
    sspec = lambda w: pl.BlockSpec((None, N_CHUNK * B, w), lambda r, h: (slab(r, h), 0, 0))
    full = lambda *s: pl.BlockSpec(s, lambda r, h: (0,) * len(s))
    return pl.pallas_call(
        _s5tail_kernel,
        out_shape=jax.ShapeDtypeStruct((S5_T, N_CHUNK * B, D), BF16),
        grid=(BLK, S5_T // BLK),
        in_specs=[sspec(512), sspec(512), full(512, 512), full(1, 512), full(S5_W, D)],
        out_specs=sspec(D),
        compiler_params=_params("arbitrary", "arbitrary"),
        name="s5tail",
    )(slat, sza, glu_w, glu_b, w_top)


def _tail0_kernel(x_ref, ys5_ref, guz_ref, vln_ref, mod_ref, sguw_ref, sgub_ref, wbot_ref, ng_ref, nb_ref, o_ref):
    tm = x_ref.shape[0]
    lane = lax.broadcasted_iota(jnp.int32, (SGU_CHUNK, 128), 1)
    lo = lane < SGU_HD
    zero = jnp.zeros((SGU_CHUNK, 128), BF16)
    w_pair = [jnp.concatenate([sguw_ref[2 * pi].astype(BF16), sguw_ref[2 * pi + 1].astype(BF16)], axis=1)
              for pi in range(SGU_HEADS // 2)]
    chunks = []
    for ci in range(tm // SGU_CHUNK):
        v = vln_ref[ci * SGU_CHUNK:(ci + 1) * SGU_CHUNK, :]
        cols = []
        for pi in range(SGU_HEADS // 2):
            vp = v[:, 128 * pi:128 * (pi + 1)]
            bm = jnp.concatenate([jnp.where(lo, vp, zero), jnp.where(lo, zero, vp)], axis=0)
            cols.append(jnp.dot(w_pair[pi], bm, preferred_element_type=F32))
        chunks.append(jnp.concatenate(cols, axis=1) + sgub_ref[...])
    s = jnp.concatenate(chunks, axis=0)
    bsg = (guz_ref[...].astype(F32) * s).astype(BF16)
    perm = _chunk_transpose_perm()
    ys5 = jnp.concatenate(
        [jnp.dot(perm, ys5_ref[:, S5_T * j:S5_T * (j + 1), :].reshape(PERM_ROWS, D), preferred_element_type=F32)
         for j in range(tm // PERM_ROWS)], axis=0)
    y = ys5 + jnp.dot(bsg, wbot_ref[...], preferred_element_type=F32)
    gmod = mod_ref[:, 2 * D:3 * D]
    o_ref[...] = _layer_norm(DN_ALPHA * x_ref[...] + gmod * y, ng_ref[0:1, :], nb_ref[0:1, :])


def _tail0(x, ys5, guz, vln, mod, sguw, sgub, w_bot, ng, nb, tm=TOKEN_TILE):
    nct = tm // S5_T
    t512 = pl.BlockSpec((None, tm, 512), lambda b, i: (b, i, 0))
    tD = pl.BlockSpec((None, tm, D), lambda b, i: (b, i, 0))
    full = lambda *s: pl.BlockSpec(s, lambda b, i: (0,) * len(s))
    return pl.pallas_call(
        _tail0_kernel,
        out_shape=jax.ShapeDtypeStruct((B, L, D), F32),
        grid=(B, L // tm),
        in_specs=[tD, pl.BlockSpec((S5_T, nct, D), lambda b, i: (0, b * (N_CHUNK // nct) + i, 0)), t512, t512,
                  _mod_spec(0),
                  full(SGU_HEADS, SGU_CHUNK, SGU_CHUNK), full(SGU_CHUNK, 512),
                  pl.BlockSpec((SGU_W, D), lambda b, i: (1, 0)), full(DEPTH, D), full(DEPTH, D)],
        out_specs=tD,
        compiler_params=_params("arbitrary", "arbitrary"),
        name="tail0",
    )(x, ys5, guz, vln, mod, sguw, sgub, w_bot, ng, nb)


CONV_C = D // 2
TILE_ROWS = TOKEN_TILE // GRID_W


def _grid_transpose_in(v, o_ref):
    perm = _chunk_transpose_perm()
    for q in range(GRID_W // S5_T):
        seg = jnp.concatenate([v[GRID_W * r + S5_T * q:GRID_W * r + S5_T * (q + 1), :] for r in range(TILE_ROWS)],
                              axis=0)
        t = jnp.dot(perm, seg, preferred_element_type=F32).astype(BF16)
        o_ref[S5_T * q:S5_T * (q + 1), :, :] = t.reshape(S5_T, TILE_ROWS, v.shape[1])


def _inproj1_kernel(x_ref, mod_ref, w_ref, hgr_ref, hgc_ref, h_ref):
    shift = mod_ref[:, 0:D]
    scale = mod_ref[:, D:2 * D]
    h = (x_ref[...] * (1.0 + scale) + shift).astype(BF16)
    h_ref[...] = h
    dot = lambda lo: jnp.dot(h, w_ref[:, lo:lo + CONV_C].astype(BF16), preferred_element_type=F32)
    hgr_ref[...] = (dot(0) * _sigmoid(dot(D))).astype(BF16)
    _grid_transpose_in((dot(CONV_C) * _sigmoid(dot(D + CONV_C))).astype(BF16), hgc_ref)


def _inproj1(x, mod, w_in_f32, tm=TOKEN_TILE):
    tile = lambda w: pl.BlockSpec((None, tm, w), lambda b, i: (b, i, 0))
    return pl.pallas_call(
        _inproj1_kernel,
        out_shape=(jax.ShapeDtypeStruct((B, L, CONV_C), BF16),
                   jax.ShapeDtypeStruct((B, GRID_W, GRID_W, CONV_C), BF16),
                   jax.ShapeDtypeStruct((B, L, D), BF16)),
        grid=(B, L // tm),
        in_specs=[tile(D),
                  _mod_spec(1),
                  pl.BlockSpec((D, 2 * D), lambda b, i: (0, 0), pipeline_mode=pl.Buffered(1))],
        out_specs=(tile(CONV_C), pl.BlockSpec((None, GRID_W, TILE_ROWS, CONV_C), lambda b, i: (b, 0, i, 0)),
                   tile(D)),
        compiler_params=_params("arbitrary", "arbitrary"),
        name="inproj1",
    )(x, mod, w_in_f32)


DFT_N = 2 * GRID_W
TAPS_PAD = CONV_K + 1


def _dft_constants():
    th = 2.0 * math.pi / DFT_N
    f = np.arange(GRID_W, dtype=np.float64)[:, None]
    p = np.arange(GRID_W, dtype=np.float64)[None, :]
    cosm = np.cos(th * f * p)
    sinm = np.sin(th * f * p)
    alt = np.where(np.arange(GRID_W) % 2 == 0, 1.0, -1.0)
    fwd = np.concatenate([cosm, alt[None, :], sinm[1:]], axis=0)
    cf = np.where(np.arange(GRID_W) == 0, 1.0, 2.0) / DFT_N
    inv = np.concatenate([cosm.T * cf[None, :], (alt / DFT_N)[:, None], sinm.T[:, 1:] * (2.0 / DFT_N)], axis=1)
    sft = (CONV_HALF - np.arange(TAPS_PAD, dtype=np.float64))[None, :]
    live = (np.arange(TAPS_PAD) < CONV_K).astype(np.float64)[None, :]
    f64 = np.where(f == 0, float(GRID_W), f)
    f32 = lambda a: jnp.asarray(a.astype(np.float32))
    return (f32(fwd).astype(BF16), f32(inv).astype(BF16),
            f32(np.cos(th * f * sft) * live), f32(np.sin(th * f * sft) * live), f32(np.cos(th * f64 * sft) * live))


def _fconv_kernel(h_ref, w_ref, b_ref, fwd_ref, inv_ref, c1_ref, s3_ref, c4_ref, o_ref, taps_ref):
    hp = lax.Precision.HIGHEST
    taps_ref[...] = jnp.zeros(taps_ref.shape, F32)
    taps_ref[0:CONV_K, :] = w_ref[...]
    taps = taps_ref[...]
    g_re = jnp.dot(c1_ref[...], taps, preferred_element_type=F32, precision=hp)
    g_im = jnp.dot(s3_ref[...], taps, preferred_element_type=F32, precision=hp)
    g_r2 = jnp.dot(c4_ref[...], taps, preferred_element_type=F32, precision=hp)
    fwd = fwd_ref[...]
    inv = inv_ref[...]
    bias = b_ref[...]
    n_runs = h_ref.shape[0] // GRID_W
    rows = lambda r: slice(GRID_W * r, GRID_W * (r + 1))
    forward = lambda r: jnp.dot(fwd, h_ref[rows(r), :], preferred_element_type=F32)
    ahead = 2
    specs = [forward(r) for r in range(ahead)]
    for r in range(n_runs):
        if r + ahead < n_runs:
            specs.append(forward(r + ahead))
        spec = specs[r]
        a, bm = spec[0:GRID_W], spec[GRID_W:DFT_N]
        prod = jnp.concatenate([a * g_re - bm * g_im, a * g_im + bm * g_r2], axis=0).astype(BF16)
        o_ref[rows(r), :] = (jnp.dot(inv, prod, preferred_element_type=F32) + bias).astype(BF16)


def _fconv(h, taps, bias, half, consts, tm=L):
    fwd, inv, c1, s3, c4 = consts
    c = h.shape[-1]
    tile = pl.BlockSpec((None, tm, c), lambda b, i: (b, i, 0))
    full = lambda *s: pl.BlockSpec(s, lambda b, i: (0,) * len(s))
    cols = lambda r: pl.BlockSpec((r, c), lambda b, i: (0, half))
    return pl.pallas_call(
        _fconv_kernel,
        out_shape=jax.ShapeDtypeStruct(h.shape, BF16),
        grid=(B, L // tm),
        in_specs=[tile, cols(CONV_K), cols(1), full(DFT_N, GRID_W), full(GRID_W, DFT_N),
                  full(GRID_W, TAPS_PAD), full(GRID_W, TAPS_PAD), full(GRID_W, TAPS_PAD)],
        out_specs=tile,
        scratch_shapes=[pltpu.VMEM((TAPS_PAD, c), F32)],
        compiler_params=_params("arbitrary", "arbitrary"),
        name="fconv",
    )(h, taps, bias, fwd, inv, c1, s3, c4)


ROW_BLOCK = 32


def _row_blocks(n_rows):
    return [slice(ROW_BLOCK * k, ROW_BLOCK * (k + 1)) for k in range(n_rows // ROW_BLOCK)]


def _tail1_kernel(x_ref, hcr_ref, hcc_ref, h1_ref, wz_ref, mod_ref, lng_ref, lnb_ref, wout_ref, ng_ref, nb_ref,
                  o_ref, col_ref, z_ref, m_ref, y_ref):
    tm = x_ref.shape[0]
    perm = _chunk_transpose_perm()
    for q in range(GRID_W // S5_T):
        blk = hcc_ref[S5_T * q:S5_T * (q + 1), :, :].reshape(PERM_ROWS, CONV_C)
        t = jnp.dot(perm, blk, preferred_element_type=F32)
        for r in range(TILE_ROWS):
            col_ref[GRID_W * r + S5_T * q:GRID_W * r + S5_T * (q + 1), :] = t[S5_T * r:S5_T * (r + 1), :]
    z_ref[...] = jnp.dot(h1_ref[...], wz_ref[...], preferred_element_type=F32)
    lng, lnb = 0.5 * lng_ref[...], 0.5 * lnb_ref[...]
    for rows in _row_blocks(tm):
        hc = jnp.concatenate([hcr_ref[rows, :].astype(F32), col_ref[rows, :]], axis=1)
        m_ref[rows, :] = (_silu_of_half(_layer_norm(hc, lng, lnb)) * _silu_of_half(z_ref[rows, :])).astype(BF16)
    y_ref[...] = jnp.dot(m_ref[...], wout_ref[...], preferred_element_type=F32)
    gmod = mod_ref[:, 2 * D:3 * D]
    ng, nb = ng_ref[1:2, :], nb_ref[1:2, :]
    for rows in _row_blocks(tm):
        o_ref[rows, :] = _layer_norm(DN_ALPHA * x_ref[rows, :] + gmod * y_ref[rows, :], ng, nb)


def _tail1(x, hc_row, hc_col, h1, w_z_half, mod, ln_g_half, ln_b_half, w_out, ng, nb, tm=TOKEN_TILE):
    tile = lambda w: pl.BlockSpec((None, tm, w), lambda b, i: (b, i, 0))
    full = lambda *s: pl.BlockSpec(s, lambda b, i: (0,) * len(s))
    return pl.pallas_call(
        _tail1_kernel,
        out_shape=jax.ShapeDtypeStruct((B, L, D), F32),
        grid=(B, L // tm),
        in_specs=[tile(D), tile(CONV_C),
                  pl.BlockSpec((None, GRID_W, TILE_ROWS, CONV_C), lambda b, i: (b, 0, i, 0)),
                  tile(D), full(D, D), _mod_spec(1),
                  full(1, D), full(1, D), full(D, D), full(DEPTH, D), full(DEPTH, D)],
        out_specs=tile(D),
        scratch_shapes=[pltpu.VMEM((tm, CONV_C), F32), pltpu.VMEM((tm, D), F32), pltpu.VMEM((tm, D), BF16),
                        pltpu.VMEM((tm, D), F32)],
        compiler_params=_params("arbitrary", "arbitrary"),
        name="tail1",
    )(x, hc_row, hc_col, h1, w_z_half, mod, ln_g_half, ln_b_half, w_out, ng, nb)


def kernel(x, c, ctx, c_ctx, mod_w, mod_b, norm_g, norm_b, ev_w_in, ev_w_out, s5_lam_re, s5_lam_im, s5_log_dt, s5_b_re, s5_b_im, s5_c_re, s5_c_im, s5_d, glu_w, glu_b, sgu_ln_g, sgu_ln_b, sgu_w, sgu_b, od_w_in, od_w_out, dw_w, dw_b, conv_ln_g, conv_ln_b):
    TH = S5_T * S5_H
    row = lambda v: v.reshape(1, -1)

    cond8 = jnp.concatenate([c, c_ctx[None], jnp.zeros((3, D), F32)], axis=0)
    mods = _adaln(cond8, mod_w, mod_b)

    ldt = jnp.broadcast_to(s5_log_dt[0][:, :, None], (2, S5_G, S5_P))
    fb = lambda s: jnp.concatenate([s[:, 0], s[:, 1]], axis=-1)
    lam3 = fb(jnp.stack([s5_lam_re[0], s5_lam_im[0], ldt]))
    bt = fb(jnp.swapaxes(jnp.stack([s5_b_re[0], s5_b_im[0]]), -1, -2))
    cn = fb(jnp.stack([s5_c_re[0], s5_c_im[0]]))
    win, wout, mix, l16 = _s5_weights(lam3, s5_d, bt, cn)

    w_out0 = ev_w_out[0].astype(BF16)
    glu_w0 = glu_w[0].astype(BF16)
    guz, vln, hs = _inproj0n(x, mods, ev_w_in[0], row(sgu_ln_g[0]), row(sgu_ln_b[0]))
    ua, sza, ua_c = _inproj0a(hs, _ctx_slabs(ctx, mods), ev_w_in[0])
    s_lat = _s5core(ua, ua_c, win, wout, mix, l16)
    y_s5 = _s5tail(s_lat, sza, glu_w0, row(glu_b[0]), w_out0)
    sgub = jnp.repeat(sgu_b[0].T, SGU_HD, axis=1)
    x1 = _tail0(x, y_s5, guz, vln, mods, sgu_w[0], sgub, w_out0, norm_g, norm_b)

    hg_row, hg_col, h1 = _inproj1(x1, mods, od_w_in[0])
    consts = _dft_constants()
    bias = row(dw_b[0])
    hc_row = _fconv(hg_row, dw_w[0], bias, 0, consts)
    hc_col = _fconv(hg_col.reshape(B, L, CONV_C), dw_w[0], bias, 1, consts)
    w_z_half = (0.5 * od_w_in[0][:, 2 * D:]).astype(BF16)
    return _tail1(x1, hc_row, hc_col.reshape(B, GRID_W, GRID_W, CONV_C), h1, w_z_half, mods,
                  row(conv_ln_g[0]), row(conv_ln_b[0]), od_w_out[0].astype(BF16), norm_g, norm_b)
```

```python
import functools
import math

import jax
import jax.numpy as jnp
import numpy as np
from jax import lax
from jax.experimental import pallas as pl
from jax.experimental.pallas import tpu as pltpu

D = 1024
B = 4
L = 4096
CTX = 256
GRID_W = 64
S5_W = 512
S5_G = 32
S5_H = 16
H_SHIFT = 4
BLK = 128 // S5_H
S5_P = 64
S5_T = 16
SGU_W = 512
SGU_HEADS = 8
SGU_HD = 64
SGU_CHUNK = 128
CONV_K = 31
CONV_HALF = CONV_K // 2
EVEN_IN = 2560
SGU_COL0 = 2 * S5_W
ODD_IN = 3072
DEPTH = 2
DN_ALPHA = (2 * DEPTH) ** 0.25
LN_EPS = 1e-5
N_CHUNK = L // S5_T
N_CCHUNK = CTX // S5_T
VMEM_LIMIT_V7X = 56 * 1024 * 1024
TOKEN_TILE = 1024

F32 = jnp.float32
BF16 = jnp.bfloat16


GELU_C = math.sqrt(2.0 / math.pi)


def _gelu(x):
    hx = 0.5 * x
    return hx * jnp.tanh(x * ((x * x) * (0.044715 * GELU_C) + GELU_C)) + hx


def _sigmoid(x):
    return 0.5 * jnp.tanh(0.5 * x) + 0.5


def _silu_of_half(hx):
    return hx * jnp.tanh(hx) + hx


def _silu(x):
    return _silu_of_half(0.5 * x)


def _layer_norm(x, g, b):
    mu = jnp.mean(x, axis=-1, keepdims=True)
    xc = x - mu
    var = jnp.mean(xc * xc, axis=-1, keepdims=True)
    return xc * lax.rsqrt(var + LN_EPS) * g + b


def _params(*sem):
    return pltpu.CompilerParams(dimension_semantics=sem, vmem_limit_bytes=VMEM_LIMIT_V7X)


def _adaln_kernel(c_ref, w_ref, b_ref, o_ref):
    def split(v):
        hi = v.astype(BF16)
        return hi, (v - hi.astype(F32)).astype(BF16)

    a_hi, a_lo = split(_silu(c_ref[...]))
    w_hi, w_lo = split(w_ref[...])
    dot = functools.partial(jnp.dot, preferred_element_type=F32)
    mod = dot(a_hi, w_hi) + dot(a_lo, w_hi) + dot(a_hi, w_lo) + b_ref[pl.ds(pl.program_id(0), 1), :]
    for r in range(mod.shape[0]):
        o_ref[r] = mod[r:r + 1, :]


def _adaln(cond8, mod_w, mod_b):
    tn = 1024
    return pl.pallas_call(
        _adaln_kernel,
        out_shape=jax.ShapeDtypeStruct((DEPTH, 8, 1, 3 * D), F32),
        grid=(DEPTH, 3 * D // tn),
        in_specs=[pl.BlockSpec((8, D), lambda l, j: (0, 0)),
                  pl.BlockSpec((None, D, tn), lambda l, j: (l, 0, j)),
                  pl.BlockSpec((DEPTH, tn), lambda l, j: (0, j))],
        out_specs=pl.BlockSpec((None, 8, 1, tn), lambda l, j: (l, 0, 0, j)),
        compiler_params=_params("arbitrary", "arbitrary"),
        name="adaln",
    )(cond8, mod_w, mod_b)


def _mod_spec(layer, cond=None):
    if cond is None:
        return pl.BlockSpec((None, None, 1, 3 * D), lambda b, i: (layer, b, 0, 0))
    return pl.BlockSpec((None, None, 1, 3 * D), lambda b: (layer, cond, 0, 0))


S5W_GROUPS = BLK


def _cpow(base_pows, j):
    re = None
    im = None
    for k, (pr, pi) in enumerate(base_pows):
        bit = ((j >> k) & 1) == 1
        mr = jnp.where(bit, pr, 1.0)
        mi = jnp.where(bit, pi, 0.0)
        if re is None:
            re, im = mr, mi
        else:
            re, im = re * mr - im * mi, re * mi + im * mr
    return re, im


def _squarings(pr, pi, n):
    out = [(pr, pi)]
    for _ in range(n - 1):
        pr, pi = pr * pr - pi * pi, 2.0 * pr * pi
        out.append((pr, pi))
    return out


def _shift_lanes(x, n):
    lane = lax.broadcasted_iota(jnp.int32, (S5_H, 128), 1)
    lo, hi = x[:, :128], x[:, 128:]
    if n == 0:
        return x
    if n < 128:
        rlo = pltpu.roll(lo, n, axis=1)
        rhi = pltpu.roll(hi, n, axis=1)
        return jnp.concatenate([jnp.where(lane >= n, rlo, 0.0), jnp.where(lane >= n, rhi, rlo)], axis=1)
    m = n - 128
    rlo = lo if m == 0 else pltpu.roll(lo, m, axis=1)
    return jnp.concatenate([jnp.zeros_like(lo), jnp.where(lane >= m, rlo, 0.0)], axis=1)


def _unshift_lanes(x, n):
    lane = lax.broadcasted_iota(jnp.int32, (S5_H, 128), 1)
    lo, hi = x[:, :128], x[:, 128:]
    if n == 0:
        return x
    if n < 128:
        rlo = pltpu.roll(lo, 128 - n, axis=1)
        rhi = pltpu.roll(hi, 128 - n, axis=1)
        keep = lane < 128 - n
        return jnp.concatenate([jnp.where(keep, rlo, rhi), jnp.where(keep, rhi, 0.0)], axis=1)
    m = n - 128
    rhi = hi if m == 0 else pltpu.roll(hi, 128 - m, axis=1)
    return jnp.concatenate([jnp.where(lane < 128 - m, rhi, 0.0), jnp.zeros_like(lo)], axis=1)


def _s5w_group(gi, bg, disc, d_t, bt_ref, cn_ref, win_ref, wout_ref, mix_ref, l16_ref):
    TH = S5_T * S5_H

    def chunk_pos(idx):
        return (((idx >> H_SHIFT) - bg) & (BLK - 1)) + ((idx >> 7) << 3)

    lr, li, cr, ci = [v[gi:gi + 1] for v in disc]
    pows_row = _squarings(lr, li, 5)
    l16_ref[gi, 0:1, :] = pows_row[4][0]
    l16_ref[gi, 1:2, :] = pows_row[4][1]
    l16_ref[gi, 2:8, :] = jnp.zeros((6, 128), F32)
    btr = bt_ref[0, gi]
    bti = bt_ref[1, gi]
    bbr = cr * btr - ci * bti
    bbi = cr * bti + ci * btr
    blk16 = lax.broadcasted_iota(jnp.int32, (S5_T, 128), 0)
    is_f16 = lax.broadcasted_iota(jnp.int32, (S5_T, 128), 1) < S5_P
    pos16 = chunk_pos(blk16 << H_SHIFT)
    pr16, pi16 = _cpow(pows_row[:4], jnp.where(is_f16, S5_T - 1 - pos16, pos16))
    rep_rows = lambda v: jnp.broadcast_to(v[:, None, :], (S5_T, S5_H, 128)).reshape(TH, 128)
    pr, pi = rep_rows(pr16), rep_rows(pi16)
    tbr = jnp.broadcast_to(bbr[None], (S5_T, S5_H, 128)).reshape(TH, 128)
    tbi = jnp.broadcast_to(bbi[None], (S5_T, S5_H, 128)).reshape(TH, 128)
    win_ref[gi, :, 0:128] = (pr * tbr - pi * tbi).astype(BF16)
    win_ref[gi, :, 128:256] = (pr * tbi + pi * tbr).astype(BF16)

    hp = lax.Precision.HIGHEST
    dot = functools.partial(jnp.dot, preferred_element_type=F32, precision=hp)
    def col256(r):
        col = jnp.broadcast_to(r, (2 * S5_P, 2 * S5_P)).T
        return jnp.concatenate([col, col], axis=1)

    def tiled_t(cn):
        t8 = jnp.broadcast_to(cn[None], (BLK, S5_H, 2 * S5_P)).reshape(2 * S5_P, 2 * S5_P).T
        return jnp.concatenate([t8, t8], axis=1)

    cpows = _squarings(col256(lr), col256(li), 4)
    row = lax.broadcasted_iota(jnp.int32, (2 * S5_P, TH), 0)
    lane_w = lax.broadcasted_iota(jnp.int32, (2 * S5_P, TH), 1)
    t_idx = chunk_pos(lane_w)
    j_idx = lane_w >> H_SHIFT
    is_f = row < S5_P
    ctr = tiled_t(cn_ref[0, gi])
    cti = tiled_t(cn_ref[1, gi])
    er, ei = _cpow(cpows, jnp.where(is_f, t_idx, S5_T - 1 - t_idx))
    er, ei = er * cpows[0][0] - ei * cpows[0][1], er * cpows[0][1] + ei * cpows[0][0]
    wr = ctr * er - cti * ei
    wi = ctr * ei + cti * er
    wout_ref[gi, 0:128, :] = wr.astype(BF16)
    wout_ref[gi, 128:256, :] = (-wi).astype(BF16)
    kr, ki = _cpow(cpows, jnp.where(is_f, j_idx, S5_T - 1 - j_idx))
    ekr = ctr * kr - cti * ki
    eki = ctr * ki + cti * kr
    lane16 = lax.broadcasted_iota(jnp.int32, (S5_H, 128), 1)
    mf = lane16 < S5_P
    kkf = dot(jnp.where(mf, bbr, 0.0), ekr) - dot(jnp.where(mf, bbi, 0.0), eki)
    kkb = dot(jnp.where(mf, 0.0, bbr), ekr) - dot(jnp.where(mf, 0.0, bbi), eki)
    d_rows = d_t[S5_H * gi:S5_H * (gi + 1), :]
    dl = jnp.concatenate([d_rows, d_rows], axis=1)
    r16 = lax.broadcasted_iota(jnp.int32, (S5_H, TH), 0)
    l256 = lax.broadcasted_iota(jnp.int32, (S5_H, TH), 1)
    rot = bg * S5_H
    for s in range(S5_T):
        blk = _shift_lanes(kkf, S5_H * s) + _unshift_lanes(kkb, S5_H * (S5_T - 1 - s))
        blk = blk + jnp.where(l256 == r16 + S5_H * s, dl, 0.0)
        if rot:
            blk = jnp.concatenate([pltpu.roll(blk[:, :128], rot, axis=1), pltpu.roll(blk[:, 128:], rot, axis=1)], axis=1)
        rho = ((s + bg) & (BLK - 1)) + (s & BLK)
        mix_ref[gi, S5_H * rho:S5_H * (rho + 1), :] = blk.astype(BF16)


def _s5w_kernel(lam_ref, d_ref, *refs):
    d_t = jnp.broadcast_to(d_ref[...], (S5W_GROUPS * S5_H, S5W_GROUPS * S5_H)).T
    lr = lam_ref[0]
    li = lam_ref[1]
    dt = jnp.exp(lam_ref[2])
    mag = jnp.exp(lr * dt)
    br = mag * jnp.cos(li * dt)
    bi = mag * jnp.sin(li * dt)
    inv = 1.0 / (lr * lr + li * li)
    nr = br - 1.0
    disc = (br, bi, (nr * lr + bi * li) * inv, (bi * lr - nr * li) * inv)
    for gi in range(S5W_GROUPS):
        _s5w_group(gi, gi % BLK, disc, d_t, *refs)


def _s5_weights(lam3, d, bt, cn):
    TH = S5_T * S5_H
    g3 = lambda r, c: pl.BlockSpec((S5W_GROUPS, r, c), lambda g: (g, 0, 0))
    ri = pl.BlockSpec((2, S5W_GROUPS, S5_H, 2 * S5_P), lambda g: (0, g, 0, 0))
    wshape = jax.ShapeDtypeStruct((S5_G, TH, TH), BF16)
    return pl.pallas_call(
        _s5w_kernel,
        out_shape=(wshape, wshape, wshape, jax.ShapeDtypeStruct((S5_G, 8, 128), F32)),
        grid=(S5_G // S5W_GROUPS,),
        in_specs=[pl.BlockSpec((3, S5W_GROUPS, 2 * S5_P), lambda g: (0, g, 0)),
                  pl.BlockSpec((1, S5W_GROUPS * S5_H), lambda g: (0, g)), ri, ri],
        out_specs=(g3(TH, TH), g3(TH, TH), g3(TH, TH), g3(8, 128)),
        compiler_params=_params("arbitrary"),
        name="s5_weights",
    )(lam3, d, bt, cn)


def _rot_blocks(v, r):
    cols = [pltpu.roll(v[:, 128 * q:128 * (q + 1)], S5_H * r, axis=1) for q in range(v.shape[1] // 128)]
    return jnp.concatenate(cols, axis=1)


def _slabs_of(h, hs_ref):
    h3 = h.reshape(h.shape[0] // S5_T, S5_T, h.shape[1])
    for s in range(S5_T):
        hs_ref[s] = h3[:, s, :].astype(BF16)


PERM_ROWS = S5_T * S5_T


def _chunk_transpose_perm():
    ri = lax.broadcasted_iota(jnp.int32, (PERM_ROWS, PERM_ROWS), 0)
    ci = lax.broadcasted_iota(jnp.int32, (PERM_ROWS, PERM_ROWS), 1)
    hit = ((ri >> H_SHIFT) == (ci & (S5_T - 1))) & ((ri & (S5_T - 1)) == (ci >> H_SHIFT))
    return jnp.where(hit, 1.0, 0.0).astype(BF16)


def _inproj0n_kernel(x_ref, mod_ref, w_ref, lng_ref, lnb_ref, guz_ref, vln_ref, hs_ref):
    shift = mod_ref[:, 0:D]
    scale = mod_ref[:, D:2 * D]
    hb = (x_ref[...] * (1.0 + scale) + shift).astype(BF16)
    perm = _chunk_transpose_perm()
    for j in range(hb.shape[0] // PERM_ROWS):
        blk = jnp.dot(perm, hb[PERM_ROWS * j:PERM_ROWS * (j + 1), :], preferred_element_type=F32).astype(BF16)
        for s in range(S5_T):
            hs_ref[s, S5_T * j:S5_T * (j + 1), :] = blk[S5_T * s:S5_T * (s + 1), :]
    dot = lambda lo: jnp.dot(hb, w_ref[:, SGU_COL0 + lo:SGU_COL0 + lo + 512].astype(BF16),
                             preferred_element_type=F32)
    guz_ref[...] = (_gelu(dot(0)) * _silu(dot(1024))).astype(BF16)
    vln_ref[...] = _layer_norm(_gelu(dot(512)), lng_ref[...], lnb_ref[...]).astype(BF16)


def _inproj0n(x, mod, w_in_f32, ln_g, ln_b, tm=TOKEN_TILE):
    nct = tm // S5_T
    o = jax.ShapeDtypeStruct((B, L, 512), BF16)
    ospec = pl.BlockSpec((None, tm, 512), lambda b, i: (b, i, 0))
    full = lambda *s: pl.BlockSpec(s, lambda b, i: (0,) * len(s))
    return pl.pallas_call(
        _inproj0n_kernel,
        out_shape=(o, o, jax.ShapeDtypeStruct((S5_T, B * N_CHUNK, D), BF16)),
        grid=(B, L // tm),
        in_specs=[pl.BlockSpec((None, tm, D), lambda b, i: (b, i, 0)),
                  _mod_spec(0),
                  pl.BlockSpec((D, EVEN_IN), lambda b, i: (0, 0), pipeline_mode=pl.Buffered(1)),
                  full(1, 512), full(1, 512)],
        out_specs=(ospec, ospec,
                   pl.BlockSpec((S5_T, nct, D), lambda b, i: (0, b * (N_CHUNK // nct) + i, 0))),
        compiler_params=_params("arbitrary", "arbitrary"),
        name="inproj0n",
    )(x, mod, w_in_f32, ln_g, ln_b)


def _ctx_slabs_kernel(x_ref, mod_ref, hs_ref):
    h = x_ref[...] * (1.0 + mod_ref[:, D:2 * D]) + mod_ref[:, 0:D]
    _slabs_of(h, hs_ref)


def _ctx_slabs(ctx, mod_c):
    return pl.pallas_call(
        _ctx_slabs_kernel,
        out_shape=jax.ShapeDtypeStruct((S5_T, B * N_CCHUNK, D), BF16),
        grid=(B,),
        in_specs=[pl.BlockSpec((None, CTX, D), lambda b: (b, 0, 0)),
                  _mod_spec(0, cond=B)],
        out_specs=pl.BlockSpec((S5_T, N_CCHUNK, D), lambda b: (0, b, 0)),
        compiler_params=_params("arbitrary"),
        name="ctx_slabs",
    )(ctx, mod_c)


def _inproj0a_kernel(hs_ref, hc_ref, w_ref, ua_ref, sza_ref, uc_ref):
    r = pl.program_id(0)
    h = hs_ref[...]
    w_ua = w_ref[:, 0:512].astype(BF16)
    ua_ref[...] = _rot_blocks(jnp.dot(h, w_ua, preferred_element_type=F32), r).astype(BF16)
    sza_ref[...] = _silu(jnp.dot(h, w_ref[:, 512:1024].astype(BF16), preferred_element_type=F32)).astype(BF16)
    uc_ref[...] = _rot_blocks(jnp.dot(hc_ref[...], w_ua, preferred_element_type=F32), r).astype(BF16)


def _inproj0a(hs, hcs, w_in_f32):
    slab = lambda r, h: r + BLK * h
    sspec = lambda n, w: pl.BlockSpec((None, n, w), lambda r, h: (slab(r, h), 0, 0))
    so = lambda n: jax.ShapeDtypeStruct((S5_T, n, 512), BF16)
    nl, ncx = B * N_CHUNK, B * N_CCHUNK
    return pl.pallas_call(
        _inproj0a_kernel,
        out_shape=(so(nl), so(nl), so(ncx)),
        grid=(BLK, S5_T // BLK),
        in_specs=[sspec(nl, D), sspec(ncx, D), pl.BlockSpec((D, SGU_COL0), lambda r, h: (0, 0))],
        out_specs=(sspec(nl, 512), sspec(nl, 512), sspec(ncx, 512)),
        compiler_params=_params("arbitrary", "arbitrary"),
        name="inproj0a",
    )(hs, hcs, w_in_f32)


SCAN_GROUPS = 4


def _scan_tiles(sre_ref, sim_ref, h_refs, n_tiles, carry, lams):
    row = lax.broadcasted_iota(jnp.int32, (8, 128), 0)
    lane = lax.broadcasted_iota(jnp.int32, (8, 128), 1)
    first = row < B
    fwd = lane < S5_P

    def body(k, c):
        of = pl.multiple_of(k * 8, 8)
        ob = pl.multiple_of((n_tiles - 1 - k) * 8, 8)
        out = []
        for gi in range(SCAN_GROUPS):
            lre, lim = lams[gi]
            hr, hi = c[2 * gi], c[2 * gi + 1]
            sr = jnp.where(fwd, sre_ref[gi, pl.ds(of, 8), :], pltpu.roll(sre_ref[gi, pl.ds(ob, 8), :], B, axis=0))
            si = jnp.where(fwd, sim_ref[gi, pl.ds(of, 8), :], pltpu.roll(sim_ref[gi, pl.ds(ob, 8), :], B, axis=0))
            h1r = lre * hr - lim * hi + sr
            h1i = lre * hi + lim * hr + si
            r1r = pltpu.roll(h1r, B, axis=0)
            r1i = pltpu.roll(h1i, B, axis=0)
            if h_refs is not None:
                fre_ref, fim_ref, bre_ref, bim_ref = h_refs
                er = jnp.where(first, hr, r1r)
                ei = jnp.where(first, hi, r1i)
                fre_ref[gi, pl.ds(of, 8), :] = er
                fim_ref[gi, pl.ds(of, 8), :] = ei
                bre_ref[gi, pl.ds(ob, 8), :] = pltpu.roll(er, B, axis=0)
                bim_ref[gi, pl.ds(ob, 8), :] = pltpu.roll(ei, B, axis=0)
            h2r = lre * r1r - lim * r1i + sr
            h2i = lre * r1i + lim * r1r + si
            out.append(jnp.where(first, pltpu.roll(h2r, B, axis=0), h2r))
            out.append(jnp.where(first, pltpu.roll(h2i, B, axis=0), h2i))
        return tuple(out)

    return lax.fori_loop(0, n_tiles, body, carry)


def _gather_group(slab_ref, src):
    halves = []
    for h in range(S5_T // BLK):
        acc = slab_ref[BLK * h]
        for s in range(1, BLK):
            acc = jnp.where(src == s, slab_ref[BLK * h + s], acc)
        halves.append(acc)
    return jnp.concatenate(halves, axis=1)


def _s5core_kernel(ul_ref, uc_ref, win_ref, wout_ref, mix_ref, l16_ref, o_ref,
                   u_ref, sre_ref, sim_ref, cre_ref, cim_ref, fre_ref, fim_ref, bre_ref, bim_ref, y_ref):
    nl = N_CHUNK * B
    ncx = N_CCHUNK * B
    blk_l = lax.broadcasted_iota(jnp.int32, (nl, 128), 1) >> H_SHIFT
    blk_c = lax.broadcasted_iota(jnp.int32, (ncx, 128), 1) >> H_SHIFT
    fwd = lax.broadcasted_iota(jnp.int32, (N_CHUNK, 128), 1) < S5_P
    for g0 in range(0, BLK, SCAN_GROUPS):
        for gi in range(SCAN_GROUPS):
            bg = g0 + gi
            win = win_ref[bg]
            src_l = ((blk_l - bg) & (BLK - 1)).astype(F32).astype(BF16)
            src_c = ((blk_c - bg) & (BLK - 1)).astype(F32).astype(BF16)
            u = _gather_group(ul_ref, src_l)
            u_ref[gi] = u
            sl = jnp.dot(u, win, preferred_element_type=F32)
            sc = jnp.dot(_gather_group(uc_ref, src_c), win, preferred_element_type=F32)
            for b in range(B):
                sre_ref[gi, pl.ds(b, N_CHUNK, stride=B), :] = sl[N_CHUNK * b:N_CHUNK * (b + 1), 0:128]
                sim_ref[gi, pl.ds(b, N_CHUNK, stride=B), :] = sl[N_CHUNK * b:N_CHUNK * (b + 1), 128:256]
                cre_ref[gi, pl.ds(b, N_CCHUNK, stride=B), :] = sc[N_CCHUNK * b:N_CCHUNK * (b + 1), 0:128]
                cim_ref[gi, pl.ds(b, N_CCHUNK, stride=B), :] = sc[N_CCHUNK * b:N_CCHUNK * (b + 1), 128:256]
        lams = [(jnp.broadcast_to(l16_ref[g0 + gi, 0:1, :], (8, 128)),
                 jnp.broadcast_to(l16_ref[g0 + gi, 1:2, :], (8, 128))) for gi in range(SCAN_GROUPS)]
        zero = tuple(jnp.zeros((8, 128), F32) for _ in range(2 * SCAN_GROUPS))
        carry = _scan_tiles(cre_ref, cim_ref, None, ncx // 8, zero, lams)
        _scan_tiles(sre_ref, sim_ref, (fre_ref, fim_ref, bre_ref, bim_ref), nl // 8, carry, lams)
        for gi in range(SCAN_GROUPS):
            bg = g0 + gi
            y = jnp.dot(u_ref[gi], mix_ref[bg], preferred_element_type=F32)
            hs = []
            for b in range(B):
                rows = pl.ds(b, N_CHUNK, stride=B)
                hs.append(jnp.concatenate([jnp.where(fwd, fre_ref[gi, rows, :], bre_ref[gi, rows, :]),
                                           jnp.where(fwd, fim_ref[gi, rows, :], bim_ref[gi, rows, :])], axis=1))
            hcat = jnp.concatenate(hs, axis=0).astype(BF16)
            y = y + jnp.dot(hcat, wout_ref[bg], preferred_element_type=F32)
            y_ref[bg] = y.astype(BF16)

    blk = blk_l.astype(F32).astype(BF16)
    for s in range(S5_T):
        h, r = s // BLK, s % BLK
        acc = None
        for j in range(BLK):
            piece = y_ref[(j - r) % BLK, :, 128 * h:128 * (h + 1)]
            acc = piece if acc is None else jnp.where(blk == j, piece, acc)
        o_ref[s] = acc


def _s5core(ul, uc, win, wout, mix, l16):
    TH = S5_T * S5_H
    nl = N_CHUNK * B
    ncx = N_CCHUNK * B
    g4 = lambda r, c: pl.BlockSpec((BLK, r, c), lambda q: (q, 0, 0))
    col = lambda n: pl.BlockSpec((S5_T, n, 128), lambda q: (0, 0, q))
    f32s = lambda n: pltpu.VMEM((SCAN_GROUPS, n, 128), F32)
    return pl.pallas_call(
        _s5core_kernel,
        out_shape=jax.ShapeDtypeStruct((S5_T, nl, S5_W), BF16),
        grid=(S5_G // BLK,),
        in_specs=[col(nl), col(ncx), g4(TH, TH), g4(TH, TH), g4(TH, TH), g4(8, 128)],
        out_specs=col(nl),
        scratch_shapes=[pltpu.VMEM((SCAN_GROUPS, nl, TH), BF16),
                        f32s(nl), f32s(nl), f32s(ncx), f32s(ncx), f32s(nl), f32s(nl), f32s(nl), f32s(nl),
                        pltpu.VMEM((BLK, nl, TH), BF16)],
        compiler_params=_params("arbitrary"),
        name="s5core",
    )(ul, uc, win, wout, mix, l16)


def _cast_on_first_step(src_ref, dst_ref, scale=None):
    @pl.when((pl.program_id(0) == 0) & (pl.program_id(1) == 0))
    def _():
        w = src_ref[...]
        dst_ref[...] = (w if scale is None else scale * w).astype(BF16)


def _const_spec(shape, index):
    return pl.BlockSpec(shape, lambda r, h: index, pipeline_mode=pl.Buffered(1))


def _s5tail_kernel(slat_ref, sza_ref, gluw32_ref, glub_ref, wtop32_ref, y_ref, gluw_ref, wtop_ref):
    _cast_on_first_step(gluw32_ref, gluw_ref)
    _cast_on_first_step(wtop32_ref, wtop_ref)
    unrot = (BLK - pl.program_id(0)) & (BLK - 1)
    for b in range(B):
        rows = slice(N_CHUNK * b, N_CHUNK * (b + 1))
        g = _gelu(_rot_blocks(slat_ref[rows, :].astype(F32), unrot))
        gate = _sigmoid(jnp.dot(g.astype(BF16), gluw_ref[...], preferred_element_type=F32) + glub_ref[...])
        a = (g * gate * sza_ref[rows, :].astype(F32)).astype(BF16)
        y_ref[rows, :] = jnp.dot(a, wtop_ref[...], preferred_element_type=F32).astype(BF16)


def _s5tail(slat, sza, glu_w, glu_b, w_out):
    slab = lambda r, h: r + BLK * h
    sspec = lambda w: pl.BlockSpec((None, N_CHUNK * B, w), lambda r, h: (slab(r, h), 0, 0))
    full = lambda *s: pl.BlockSpec(s, lambda r, h: (0,) * len(s))
    return pl.pallas_call(
        _s5tail_kernel,
        out_shape=jax.ShapeDtypeStruct((S5_T, N_CHUNK * B, D), BF16),
        grid=(BLK, S5_T // BLK),
        in_specs=[sspec(512), sspec(512), _const_spec((S5_W, S5_W), (0, 0)), full(1, 512),
                  _const_spec((S5_W, D), (0, 0))],
        out_specs=sspec(D),
        scratch_shapes=[pltpu.VMEM((S5_W, S5_W), BF16), pltpu.VMEM((S5_W, D), BF16)],
        compiler_params=_params("arbitrary", "arbitrary"),
        name="s5tail",
    )(slat, sza, glu_w, glu_b, w_out)


def _tail0_kernel(x_ref, ys5_ref, guz_ref, vln_ref, mod_ref, sguw_ref, sgub_ref, wbot32_ref, ng_ref, nb_ref, o_ref,
                  wbot_ref):
    _cast_on_first_step(wbot32_ref, wbot_ref)
    tm = x_ref.shape[0]
    lane = lax.broadcasted_iota(jnp.int32, (SGU_CHUNK, 128), 1)
    lo = lane < SGU_HD
    zero = jnp.zeros((SGU_CHUNK, 128), BF16)
    w_pair = [jnp.concatenate([sguw_ref[2 * pi].astype(BF16), sguw_ref[2 * pi + 1].astype(BF16)], axis=1)
              for pi in range(SGU_HEADS // 2)]
    chunks = []
    for ci in range(tm // SGU_CHUNK):
        v = vln_ref[ci * SGU_CHUNK:(ci + 1) * SGU_CHUNK, :]
        cols = []
        for pi in range(SGU_HEADS // 2):
            vp = v[:, 128 * pi:128 * (pi + 1)]
            bm = jnp.concatenate([jnp.where(lo, vp, zero), jnp.where(lo, zero, vp)], axis=0)
            cols.append(jnp.dot(w_pair[pi], bm, preferred_element_type=F32))
        chunks.append(jnp.concatenate(cols, axis=1) + sgub_ref[...])
    s = jnp.concatenate(chunks, axis=0)
    bsg = (guz_ref[...].astype(F32) * s).astype(BF16)
    perm = _chunk_transpose_perm()
    ys5 = jnp.concatenate(
        [jnp.dot(perm, ys5_ref[:, S5_T * j:S5_T * (j + 1), :].reshape(PERM_ROWS, D), preferred_element_type=F32)
         for j in range(tm // PERM_ROWS)], axis=0)
    y = ys5 + jnp.dot(bsg, wbot_ref[...], preferred_element_type=F32)
    gmod = mod_ref[:, 2 * D:3 * D]
    o_ref[...] = _layer_norm(DN_ALPHA * x_ref[...] + gmod * y, ng_ref[0:1, :], nb_ref[0:1, :])


def _tail0(x, ys5, guz, vln, mod, sguw, sgub, w_bot, ng, nb, tm=TOKEN_TILE):
    nct = tm // S5_T
    t512 = pl.BlockSpec((None, tm, 512), lambda b, i: (b, i, 0))
    tD = pl.BlockSpec((None, tm, D), lambda b, i: (b, i, 0))
    full = lambda *s: pl.BlockSpec(s, lambda b, i: (0,) * len(s))
    return pl.pallas_call(
        _tail0_kernel,
        out_shape=jax.ShapeDtypeStruct((B, L, D), F32),
        grid=(B, L // tm),
        in_specs=[tD, pl.BlockSpec((S5_T, nct, D), lambda b, i: (0, b * (N_CHUNK // nct) + i, 0)), t512, t512,
                  _mod_spec(0),
                  full(SGU_HEADS, SGU_CHUNK, SGU_CHUNK), full(SGU_CHUNK, 512),
                  _const_spec((SGU_W, D), (1, 0)), full(DEPTH, D), full(DEPTH, D)],
        out_specs=tD,
        scratch_shapes=[pltpu.VMEM((SGU_W, D), BF16)],
        compiler_params=_params("arbitrary", "arbitrary"),
        name="tail0",
    )(x, ys5, guz, vln, mod, sguw, sgub, w_bot, ng, nb)


CONV_C = D // 2
TILE_ROWS = TOKEN_TILE // GRID_W


def _grid_transpose_in(v, o_ref):
    perm = _chunk_transpose_perm()
    for q in range(GRID_W // S5_T):
        seg = jnp.concatenate([v[GRID_W * r + S5_T * q:GRID_W * r + S5_T * (q + 1), :] for r in range(TILE_ROWS)],
                              axis=0)
        t = jnp.dot(perm, seg, preferred_element_type=F32).astype(BF16)
        o_ref[S5_T * q:S5_T * (q + 1), :, :] = t.reshape(S5_T, TILE_ROWS, v.shape[1])


def _inproj1_kernel(x_ref, mod_ref, w_ref, hgr_ref, hgc_ref, h_ref):
    shift = mod_ref[:, 0:D]
    scale = mod_ref[:, D:2 * D]
    h = (x_ref[...] * (1.0 + scale) + shift).astype(BF16)
    h_ref[...] = h
    dot = lambda lo: jnp.dot(h, w_ref[:, lo:lo + CONV_C].astype(BF16), preferred_element_type=F32)
    hgr_ref[...] = (dot(0) * _sigmoid(dot(D))).astype(BF16)
    _grid_transpose_in((dot(CONV_C) * _sigmoid(dot(D + CONV_C))).astype(BF16), hgc_ref)


def _inproj1(x, mod, w_in_f32, tm=TOKEN_TILE):
    tile = lambda w: pl.BlockSpec((None, tm, w), lambda b, i: (b, i, 0))
    return pl.pallas_call(
        _inproj1_kernel,
        out_shape=(jax.ShapeDtypeStruct((B, L, CONV_C), BF16),
                   jax.ShapeDtypeStruct((B, GRID_W, GRID_W, CONV_C), BF16),
                   jax.ShapeDtypeStruct((B, L, D), BF16)),
        grid=(B, L // tm),
        in_specs=[tile(D),
                  _mod_spec(1),
                  pl.BlockSpec((D, 2 * D), lambda b, i: (0, 0), pipeline_mode=pl.Buffered(1))],
        out_specs=(tile(CONV_C), pl.BlockSpec((None, GRID_W, TILE_ROWS, CONV_C), lambda b, i: (b, 0, i, 0)),
                   tile(D)),
        compiler_params=_params("arbitrary", "arbitrary"),
        name="inproj1",
    )(x, mod, w_in_f32)


DFT_N = 2 * GRID_W
TAPS_PAD = CONV_K + 1


def _dft_constants():
    th = 2.0 * math.pi / DFT_N
    f = np.arange(GRID_W, dtype=np.float64)[:, None]
    p = np.arange(GRID_W, dtype=np.float64)[None, :]
    cosm = np.cos(th * f * p)
    sinm = np.sin(th * f * p)
    alt = np.where(np.arange(GRID_W) % 2 == 0, 1.0, -1.0)
    fwd = np.concatenate([cosm, alt[None, :], sinm[1:]], axis=0)
    cf = np.where(np.arange(GRID_W) == 0, 1.0, 2.0) / DFT_N
    inv = np.concatenate([cosm.T * cf[None, :], (alt / DFT_N)[:, None], sinm.T[:, 1:] * (2.0 / DFT_N)], axis=1)
    sft = (CONV_HALF - np.arange(TAPS_PAD, dtype=np.float64))[None, :]
    live = (np.arange(TAPS_PAD) < CONV_K).astype(np.float64)[None, :]
    f64 = np.where(f == 0, float(GRID_W), f)
    f32 = lambda a: jnp.asarray(a.astype(np.float32))
    return (f32(fwd).astype(BF16), f32(inv).astype(BF16),
            f32(np.cos(th * f * sft) * live), f32(np.sin(th * f * sft) * live), f32(np.cos(th * f64 * sft) * live))


def _fconv_kernel(h_ref, w_ref, b_ref, fwd_ref, inv_ref, c1_ref, s3_ref, c4_ref, o_ref, taps_ref):
    hp = lax.Precision.HIGHEST
    taps_ref[...] = jnp.zeros(taps_ref.shape, F32)
    taps_ref[0:CONV_K, :] = w_ref[...]
    taps = taps_ref[...]
    g_re = jnp.dot(c1_ref[...], taps, preferred_element_type=F32, precision=hp)
    g_im = jnp.dot(s3_ref[...], taps, preferred_element_type=F32, precision=hp)
    g_r2 = jnp.dot(c4_ref[...], taps, preferred_element_type=F32, precision=hp)
    fwd = fwd_ref[...]
    inv = inv_ref[...]
    bias = b_ref[...]
    n_runs = h_ref.shape[0] // GRID_W
    rows = lambda r: slice(GRID_W * r, GRID_W * (r + 1))
    forward = lambda r: jnp.dot(fwd, h_ref[rows(r), :], preferred_element_type=F32)
    ahead = 2
    specs = [forward(r) for r in range(ahead)]
    for r in range(n_runs):
        if r + ahead < n_runs:
            specs.append(forward(r + ahead))
        spec = specs[r]
        a, bm = spec[0:GRID_W], spec[GRID_W:DFT_N]
        prod = jnp.concatenate([a * g_re - bm * g_im, a * g_im + bm * g_r2], axis=0).astype(BF16)
        o_ref[rows(r), :] = (jnp.dot(inv, prod, preferred_element_type=F32) + bias).astype(BF16)


def _fconv(h, taps, bias, half, consts, tm=L):
    fwd, inv, c1, s3, c4 = consts
    c = h.shape[-1]
    tile = pl.BlockSpec((None, tm, c), lambda b, i: (b, i, 0))
    full = lambda *s: pl.BlockSpec(s, lambda b, i: (0,) * len(s))
    cols = lambda r: pl.BlockSpec((None, r, c), lambda b, i: (0, 0, half))
    return pl.pallas_call(
        _fconv_kernel,
        out_shape=jax.ShapeDtypeStruct(h.shape, BF16),
        grid=(B, L // tm),
        in_specs=[tile, cols(CONV_K), cols(1), full(DFT_N, GRID_W), full(GRID_W, DFT_N),
                  full(GRID_W, TAPS_PAD), full(GRID_W, TAPS_PAD), full(GRID_W, TAPS_PAD)],
        out_specs=tile,
        scratch_shapes=[pltpu.VMEM((TAPS_PAD, c), F32)],
        compiler_params=_params("arbitrary", "arbitrary"),
        name="fconv",
    )(h, taps, bias, fwd, inv, c1, s3, c4)


ROW_BLOCK = 32


def _row_blocks(n_rows):
    return [slice(ROW_BLOCK * k, ROW_BLOCK * (k + 1)) for k in range(n_rows // ROW_BLOCK)]


def _tail1_kernel(x_ref, hcr_ref, hcc_ref, h1_ref, wz32_ref, mod_ref, lng_ref, lnb_ref, wout_ref, ng_ref, nb_ref,
                  o_ref, col_ref, z_ref, m_ref, y_ref, wz_ref):
    _cast_on_first_step(wz32_ref, wz_ref, scale=0.5)
    tm = x_ref.shape[0]
    perm = _chunk_transpose_perm()
    for q in range(GRID_W // S5_T):
        blk = hcc_ref[S5_T * q:S5_T * (q + 1), :, :].reshape(PERM_ROWS, CONV_C)
        t = jnp.dot(perm, blk, preferred_element_type=F32)
        for r in range(TILE_ROWS):
            col_ref[GRID_W * r + S5_T * q:GRID_W * r + S5_T * (q + 1), :] = t[S5_T * r:S5_T * (r + 1), :]
    z_ref[...] = jnp.dot(h1_ref[...], wz_ref[...], preferred_element_type=F32)
    lng, lnb = 0.5 * lng_ref[...], 0.5 * lnb_ref[...]
    for rows in _row_blocks(tm):
        hc = jnp.concatenate([hcr_ref[rows, :].astype(F32), col_ref[rows, :]], axis=1)
        m_ref[rows, :] = (_silu_of_half(_layer_norm(hc, lng, lnb)) * _silu_of_half(z_ref[rows, :])).astype(BF16)
    y_ref[...] = jnp.dot(m_ref[...], wout_ref[...], preferred_element_type=F32)
    gmod = mod_ref[:, 2 * D:3 * D]
    ng, nb = ng_ref[1:2, :], nb_ref[1:2, :]
    for rows in _row_blocks(tm):
        o_ref[rows, :] = _layer_norm(DN_ALPHA * x_ref[rows, :] + gmod * y_ref[rows, :], ng, nb)


def _tail1(x, hc_row, hc_col, h1, w_in_f32, mod, ln_g, ln_b, w_out, ng, nb, tm=TOKEN_TILE):
    tile = lambda w: pl.BlockSpec((None, tm, w), lambda b, i: (b, i, 0))
    full = lambda *s: pl.BlockSpec(s, lambda b, i: (0,) * len(s))
    return pl.pallas_call(
        _tail1_kernel,
        out_shape=jax.ShapeDtypeStruct((B, L, D), F32),
        grid=(B, L // tm),
        in_specs=[tile(D), tile(CONV_C),
                  pl.BlockSpec((None, GRID_W, TILE_ROWS, CONV_C), lambda b, i: (b, 0, i, 0)),
                  tile(D), _const_spec((D, D), (0, 2)), _mod_spec(1),
                  full(1, D), full(1, D), full(D, D), full(DEPTH, D), full(DEPTH, D)],
        out_specs=tile(D),
        scratch_shapes=[pltpu.VMEM((tm, CONV_C), F32), pltpu.VMEM((tm, D), F32), pltpu.VMEM((tm, D), BF16),
                        pltpu.VMEM((tm, D), F32), pltpu.VMEM((D, D), BF16)],
        compiler_params=_params("arbitrary", "arbitrary"),
        name="tail1",
    )(x, hc_row, hc_col, h1, w_in_f32, mod, ln_g, ln_b, w_out, ng, nb)


def kernel(x, c, ctx, c_ctx, mod_w, mod_b, norm_g, norm_b, ev_w_in, ev_w_out, s5_lam_re, s5_lam_im, s5_log_dt, s5_b_re, s5_b_im, s5_c_re, s5_c_im, s5_d, glu_w, glu_b, sgu_ln_g, sgu_ln_b, sgu_w, sgu_b, od_w_in, od_w_out, dw_w, dw_b, conv_ln_g, conv_ln_b):
    TH = S5_T * S5_H
    row = lambda v: v.reshape(1, -1)

    cond8 = jnp.concatenate([c, c_ctx[None], jnp.zeros((3, D), F32)], axis=0)
    mods = _adaln(cond8, mod_w, mod_b)

    ldt = jnp.broadcast_to(s5_log_dt[0][:, :, None], (2, S5_G, S5_P))
    fb = lambda s: jnp.concatenate([s[:, 0], s[:, 1]], axis=-1)
    lam3 = fb(jnp.stack([s5_lam_re[0], s5_lam_im[0], ldt]))
    bt = fb(jnp.swapaxes(jnp.stack([s5_b_re[0], s5_b_im[0]]), -1, -2))
    cn = fb(jnp.stack([s5_c_re[0], s5_c_im[0]]))
    win, wout, mix, l16 = _s5_weights(lam3, s5_d, bt, cn)

    guz, vln, hs = _inproj0n(x, mods, ev_w_in[0], row(sgu_ln_g[0]), row(sgu_ln_b[0]))
    ua, sza, ua_c = _inproj0a(hs, _ctx_slabs(ctx, mods), ev_w_in[0])
    s_lat = _s5core(ua, ua_c, win, wout, mix, l16)
    y_s5 = _s5tail(s_lat, sza, glu_w[0], row(glu_b[0]), ev_w_out[0])
    sgub = jnp.repeat(sgu_b[0].T, SGU_HD, axis=1)
    x1 = _tail0(x, y_s5, guz, vln, mods, sgu_w[0], sgub, ev_w_out[0], norm_g, norm_b)

    hg_row, hg_col, h1 = _inproj1(x1, mods, od_w_in[0])
    consts = _dft_constants()
    bias = dw_b.reshape(1, 1, 2 * CONV_C)
    hc_row = _fconv(hg_row, dw_w, bias, 0, consts)
    hc_col = _fconv(hg_col.reshape(B, L, CONV_C), dw_w, bias, 1, consts)
    return _tail1(x1, hc_row, hc_col.reshape(B, GRID_W, GRID_W, CONV_C), h1, od_w_in[0], mods,
                  row(conv_ln_g[0]), row(conv_ln_b[0]), od_w_out[0].astype(BF16), norm_g, norm_b)
```

```python
import functools
import math

import jax
import jax.numpy as jnp
import numpy as np
from jax import lax
from jax.experimental import pallas as pl
from jax.experimental.pallas import tpu as pltpu

D = 1024
B = 4
L = 4096
CTX = 256
GRID_W = 64
S5_W = 512
S5_G = 32
S5_H = 16
H_SHIFT = 4
BLK = 128 // S5_H
S5_P = 64
S5_T = 16
SGU_W = 512
SGU_HEADS = 8
SGU_HD = 64
SGU_CHUNK = 128
CONV_K = 31
CONV_HALF = CONV_K // 2
EVEN_IN = 2560
SGU_COL0 = 2 * S5_W
ODD_IN = 3072
DEPTH = 2
DN_ALPHA = (2 * DEPTH) ** 0.25
LN_EPS = 1e-5
N_CHUNK = L // S5_T
N_CCHUNK = CTX // S5_T
VMEM_LIMIT_V7X = 56 * 1024 * 1024
TOKEN_TILE = 1024

F32 = jnp.float32
BF16 = jnp.bfloat16


GELU_C = math.sqrt(2.0 / math.pi)


def _gelu(x):
    hx = 0.5 * x
    return hx * jnp.tanh(x * ((x * x) * (0.044715 * GELU_C) + GELU_C)) + hx


def _sigmoid(x):
    return 0.5 * jnp.tanh(0.5 * x) + 0.5


def _silu_of_half(hx):
    return hx * jnp.tanh(hx) + hx


def _silu(x):
    return _silu_of_half(0.5 * x)


def _layer_norm(x, g, b):
    mu = jnp.mean(x, axis=-1, keepdims=True)
    xc = x - mu
    var = jnp.mean(xc * xc, axis=-1, keepdims=True)
    return xc * lax.rsqrt(var + LN_EPS) * g + b


def _params(*sem):
    return pltpu.CompilerParams(dimension_semantics=sem, vmem_limit_bytes=VMEM_LIMIT_V7X)


def _adaln_kernel(c_ref, w_ref, b_ref, o_ref):
    def split(v):
        hi = v.astype(BF16)
        return hi, (v - hi.astype(F32)).astype(BF16)

    a_hi, a_lo = split(_silu(c_ref[...]))
    w_hi, w_lo = split(w_ref[...])
    dot = functools.partial(jnp.dot, preferred_element_type=F32)
    n = a_hi.shape[0]
    both = dot(jnp.concatenate([a_hi.astype(F32), a_lo.astype(F32)], axis=0).astype(BF16), w_hi)
    mod = both[:n] + both[n:] + dot(a_hi, w_lo) + b_ref[pl.ds(pl.program_id(0), 1), :]
    for r in range(mod.shape[0]):
        o_ref[r] = mod[r:r + 1, :]


def _adaln(cond8, mod_w, mod_b):
    tn = 1024
    return pl.pallas_call(
        _adaln_kernel,
        out_shape=jax.ShapeDtypeStruct((DEPTH, 8, 1, 3 * D), F32),
        grid=(DEPTH, 3 * D // tn),
        in_specs=[pl.BlockSpec((8, D), lambda l, j: (0, 0)),
                  pl.BlockSpec((None, D, tn), lambda l, j: (l, 0, j)),
                  pl.BlockSpec((DEPTH, tn), lambda l, j: (0, j))],
        out_specs=pl.BlockSpec((None, 8, 1, tn), lambda l, j: (l, 0, 0, j)),
        compiler_params=_params("arbitrary", "arbitrary"),
        name="adaln",
    )(cond8, mod_w, mod_b)


def _mod_spec(layer, cond=None):
    if cond is None:
        return pl.BlockSpec((None, None, 1, 3 * D), lambda b, i: (layer, b, 0, 0))
    return pl.BlockSpec((None, None, 1, 3 * D), lambda b: (layer, cond, 0, 0))


S5W_GROUPS = BLK


def _cpow(base_pows, j):
    re = None
    im = None
    for k, (pr, pi) in enumerate(base_pows):
        bit = ((j >> k) & 1) == 1
        mr = jnp.where(bit, pr, 1.0)
        mi = jnp.where(bit, pi, 0.0)
        if re is None:
            re, im = mr, mi
        else:
            re, im = re * mr - im * mi, re * mi + im * mr
    return re, im


def _squarings(pr, pi, n):
    out = [(pr, pi)]
    for _ in range(n - 1):
        pr, pi = pr * pr - pi * pi, 2.0 * pr * pi
        out.append((pr, pi))
    return out


def _shift_lanes(x, n):
    lane = lax.broadcasted_iota(jnp.int32, (S5_H, 128), 1)
    lo, hi = x[:, :128], x[:, 128:]
    if n == 0:
        return x
    if n < 128:
        rlo = pltpu.roll(lo, n, axis=1)
        rhi = pltpu.roll(hi, n, axis=1)
        return jnp.concatenate([jnp.where(lane >= n, rlo, 0.0), jnp.where(lane >= n, rhi, rlo)], axis=1)
    m = n - 128
    rlo = lo if m == 0 else pltpu.roll(lo, m, axis=1)
    return jnp.concatenate([jnp.zeros_like(lo), jnp.where(lane >= m, rlo, 0.0)], axis=1)


def _unshift_lanes(x, n):
    lane = lax.broadcasted_iota(jnp.int32, (S5_H, 128), 1)
    lo, hi = x[:, :128], x[:, 128:]
    if n == 0:
        return x
    if n < 128:
        rlo = pltpu.roll(lo, 128 - n, axis=1)
        rhi = pltpu.roll(hi, 128 - n, axis=1)
        keep = lane < 128 - n
        return jnp.concatenate([jnp.where(keep, rlo, rhi), jnp.where(keep, rhi, 0.0)], axis=1)
    m = n - 128
    rhi = hi if m == 0 else pltpu.roll(hi, 128 - m, axis=1)
    return jnp.concatenate([jnp.where(lane < 128 - m, rhi, 0.0), jnp.zeros_like(lo)], axis=1)


def _s5w_group(gi, bg, disc, d_t, bt_ref, cn_ref, win_ref, wout_ref, mix_ref, l16_ref):
    TH = S5_T * S5_H

    def chunk_pos(idx):
        return (((idx >> H_SHIFT) - bg) & (BLK - 1)) + ((idx >> 7) << 3)

    lr, li, cr, ci = [v[gi:gi + 1] for v in disc]
    pows_row = _squarings(lr, li, 5)
    l16_ref[gi, 0:1, :] = pows_row[4][0]
    l16_ref[gi, 1:2, :] = pows_row[4][1]
    l16_ref[gi, 2:8, :] = jnp.zeros((6, 128), F32)
    btr = bt_ref[0, gi]
    bti = bt_ref[1, gi]
    bbr = cr * btr - ci * bti
    bbi = cr * bti + ci * btr
    blk16 = lax.broadcasted_iota(jnp.int32, (S5_T, 128), 0)
    is_f16 = lax.broadcasted_iota(jnp.int32, (S5_T, 128), 1) < S5_P
    pos16 = chunk_pos(blk16 << H_SHIFT)
    pr16, pi16 = _cpow(pows_row[:4], jnp.where(is_f16, S5_T - 1 - pos16, pos16))
    rep_rows = lambda v: jnp.broadcast_to(v[:, None, :], (S5_T, S5_H, 128)).reshape(TH, 128)
    pr, pi = rep_rows(pr16), rep_rows(pi16)
    tbr = jnp.broadcast_to(bbr[None], (S5_T, S5_H, 128)).reshape(TH, 128)
    tbi = jnp.broadcast_to(bbi[None], (S5_T, S5_H, 128)).reshape(TH, 128)
    win_ref[gi, :, 0:128] = (pr * tbr - pi * tbi).astype(BF16)
    win_ref[gi, :, 128:256] = (pr * tbi + pi * tbr).astype(BF16)

    hp = lax.Precision.HIGHEST
    dot = functools.partial(jnp.dot, preferred_element_type=F32, precision=hp)
    def col256(r):
        col = jnp.broadcast_to(r, (2 * S5_P, 2 * S5_P)).T
        return jnp.concatenate([col, col], axis=1)

    def tiled_t(cn):
        t8 = jnp.broadcast_to(cn[None], (BLK, S5_H, 2 * S5_P)).reshape(2 * S5_P, 2 * S5_P).T
        return jnp.concatenate([t8, t8], axis=1)

    cpows = _squarings(col256(lr), col256(li), 4)
    row = lax.broadcasted_iota(jnp.int32, (2 * S5_P, TH), 0)
    lane_w = lax.broadcasted_iota(jnp.int32, (2 * S5_P, TH), 1)
    t_idx = chunk_pos(lane_w)
    j_idx = lane_w >> H_SHIFT
    is_f = row < S5_P
    ctr = tiled_t(cn_ref[0, gi])
    cti = tiled_t(cn_ref[1, gi])
    er, ei = _cpow(cpows, jnp.where(is_f, t_idx, S5_T - 1 - t_idx))
    er, ei = er * cpows[0][0] - ei * cpows[0][1], er * cpows[0][1] + ei * cpows[0][0]
    wr = ctr * er - cti * ei
    wi = ctr * ei + cti * er
    wout_ref[gi, 0:128, :] = wr.astype(BF16)
    wout_ref[gi, 128:256, :] = (-wi).astype(BF16)
    kr, ki = _cpow(cpows, jnp.where(is_f, j_idx, S5_T - 1 - j_idx))
    ekr = ctr * kr - cti * ki
    eki = ctr * ki + cti * kr
    lane16 = lax.broadcasted_iota(jnp.int32, (S5_H, 128), 1)
    mf = lane16 < S5_P
    kkf = dot(jnp.where(mf, bbr, 0.0), ekr) - dot(jnp.where(mf, bbi, 0.0), eki)
    kkb = dot(jnp.where(mf, 0.0, bbr), ekr) - dot(jnp.where(mf, 0.0, bbi), eki)
    d_rows = d_t[S5_H * gi:S5_H * (gi + 1), :]
    dl = jnp.concatenate([d_rows, d_rows], axis=1)
    r16 = lax.broadcasted_iota(jnp.int32, (S5_H, TH), 0)
    l256 = lax.broadcasted_iota(jnp.int32, (S5_H, TH), 1)
    rot = bg * S5_H
    for s in range(S5_T):
        blk = _shift_lanes(kkf, S5_H * s) + _unshift_lanes(kkb, S5_H * (S5_T - 1 - s))
        blk = blk + jnp.where(l256 == r16 + S5_H * s, dl, 0.0)
        if rot:
            blk = jnp.concatenate([pltpu.roll(blk[:, :128], rot, axis=1), pltpu.roll(blk[:, 128:], rot, axis=1)], axis=1)
        rho = ((s + bg) & (BLK - 1)) + (s & BLK)
        mix_ref[gi, S5_H * rho:S5_H * (rho + 1), :] = blk.astype(BF16)


def _s5w_kernel(lam_ref, d_ref, *refs):
    d_t = jnp.broadcast_to(d_ref[...], (S5W_GROUPS * S5_H, S5W_GROUPS * S5_H)).T
    lr = lam_ref[0]
    li = lam_ref[1]
    dt = jnp.exp(lam_ref[2])
    mag = jnp.exp(lr * dt)
    br = mag * jnp.cos(li * dt)
    bi = mag * jnp.sin(li * dt)
    inv = 1.0 / (lr * lr + li * li)
    nr = br - 1.0
    disc = (br, bi, (nr * lr + bi * li) * inv, (bi * lr - nr * li) * inv)
    for gi in range(S5W_GROUPS):
        _s5w_group(gi, gi % BLK, disc, d_t, *refs)


def _s5_weights(lam3, d, bt, cn):
    TH = S5_T * S5_H
    g3 = lambda r, c: pl.BlockSpec((S5W_GROUPS, r, c), lambda g: (g, 0, 0))
    ri = pl.BlockSpec((2, S5W_GROUPS, S5_H, 2 * S5_P), lambda g: (0, g, 0, 0))
    wshape = jax.ShapeDtypeStruct((S5_G, TH, TH), BF16)
    return pl.pallas_call(
        _s5w_kernel,
        out_shape=(wshape, wshape, wshape, jax.ShapeDtypeStruct((S5_G, 8, 128), F32)),
        grid=(S5_G // S5W_GROUPS,),
        in_specs=[pl.BlockSpec((3, S5W_GROUPS, 2 * S5_P), lambda g: (0, g, 0)),
                  pl.BlockSpec((1, S5W_GROUPS * S5_H), lambda g: (0, g)), ri, ri],
        out_specs=(g3(TH, TH), g3(TH, TH), g3(TH, TH), g3(8, 128)),
        compiler_params=_params("arbitrary"),
        name="s5_weights",
    )(lam3, d, bt, cn)


def _rot_blocks(v, r):
    cols = [pltpu.roll(v[:, 128 * q:128 * (q + 1)], S5_H * r, axis=1) for q in range(v.shape[1] // 128)]
    return jnp.concatenate(cols, axis=1)


PERM_ROWS = S5_T * S5_T
assert CTX == PERM_ROWS


def _chunk_transpose_perm():
    ri = lax.broadcasted_iota(jnp.int32, (PERM_ROWS, PERM_ROWS), 0)
    ci = lax.broadcasted_iota(jnp.int32, (PERM_ROWS, PERM_ROWS), 1)
    hit = ((ri >> H_SHIFT) == (ci & (S5_T - 1))) & ((ri & (S5_T - 1)) == (ci >> H_SHIFT))
    return jnp.where(hit, 1.0, 0.0).astype(BF16)


def _inproj0n_kernel(x_ref, mod_ref, w_ref, lng_ref, lnb_ref, guz_ref, vln_ref, hs_ref):
    shift = mod_ref[:, 0:D]
    scale = mod_ref[:, D:2 * D]
    hb = (x_ref[...] * (1.0 + scale) + shift).astype(BF16)
    perm = _chunk_transpose_perm()
    for j in range(hb.shape[0] // PERM_ROWS):
        blk = jnp.dot(perm, hb[PERM_ROWS * j:PERM_ROWS * (j + 1), :], preferred_element_type=F32).astype(BF16)
        for s in range(S5_T):
            hs_ref[s, S5_T * j:S5_T * (j + 1), :] = blk[S5_T * s:S5_T * (s + 1), :]
    dot = lambda lo: jnp.dot(hb, w_ref[:, SGU_COL0 + lo:SGU_COL0 + lo + 512].astype(BF16),
                             preferred_element_type=F32)
    guz_ref[...] = (_gelu(dot(0)) * _silu(dot(1024))).astype(BF16)
    vln_ref[...] = _layer_norm(_gelu(dot(512)), lng_ref[...], lnb_ref[...]).astype(BF16)


def _inproj0n(x, mod, w_in_f32, ln_g, ln_b, tm=TOKEN_TILE):
    nct = tm // S5_T
    o = jax.ShapeDtypeStruct((B, L, 512), BF16)
    ospec = pl.BlockSpec((None, tm, 512), lambda b, i: (b, i, 0))
    full = lambda *s: pl.BlockSpec(s, lambda b, i: (0,) * len(s))
    return pl.pallas_call(
        _inproj0n_kernel,
        out_shape=(o, o, jax.ShapeDtypeStruct((S5_T, B * N_CHUNK, D), BF16)),
        grid=(B, L // tm),
        in_specs=[pl.BlockSpec((None, tm, D), lambda b, i: (b, i, 0)),
                  _mod_spec(0),
                  pl.BlockSpec((D, EVEN_IN), lambda b, i: (0, 0), pipeline_mode=pl.Buffered(1)),
                  full(1, 512), full(1, 512)],
        out_specs=(ospec, ospec,
                   pl.BlockSpec((S5_T, nct, D), lambda b, i: (0, b * (N_CHUNK // nct) + i, 0))),
        compiler_params=_params("arbitrary", "arbitrary"),
        name="inproj0n",
    )(x, mod, w_in_f32, ln_g, ln_b)


def _ctx_slabs_kernel(x_ref, mod_ref, hs_ref):
    hb = (x_ref[...] * (1.0 + mod_ref[:, D:2 * D]) + mod_ref[:, 0:D]).astype(BF16)
    blk = jnp.dot(_chunk_transpose_perm(), hb, preferred_element_type=F32).astype(BF16)
    for s in range(S5_T):
        hs_ref[s] = blk[N_CCHUNK * s:N_CCHUNK * (s + 1), :]


def _ctx_slabs(ctx, mod_c):
    return pl.pallas_call(
        _ctx_slabs_kernel,
        out_shape=jax.ShapeDtypeStruct((S5_T, B * N_CCHUNK, D), BF16),
        grid=(B,),
        in_specs=[pl.BlockSpec((None, CTX, D), lambda b: (b, 0, 0)),
                  _mod_spec(0, cond=B)],
        out_specs=pl.BlockSpec((S5_T, N_CCHUNK, D), lambda b: (0, b, 0)),
        compiler_params=_params("arbitrary"),
        name="ctx_slabs",
    )(ctx, mod_c)


def _inproj0a_kernel(hs_ref, hc_ref, w_ref, ua_ref, sza_ref, uc_ref):
    r = pl.program_id(0)
    h = hs_ref[...]
    w_ua = w_ref[:, 0:512].astype(BF16)
    ua_ref[...] = _rot_blocks(jnp.dot(h, w_ua, preferred_element_type=F32), r).astype(BF16)
    sza_ref[...] = _silu(jnp.dot(h, w_ref[:, 512:1024].astype(BF16), preferred_element_type=F32)).astype(BF16)
    uc_ref[...] = _rot_blocks(jnp.dot(hc_ref[...], w_ua, preferred_element_type=F32), r).astype(BF16)


def _inproj0a(hs, hcs, w_in_f32):
    slab = lambda r, h: r + BLK * h
    sspec = lambda n, w: pl.BlockSpec((None, n, w), lambda r, h: (slab(r, h), 0, 0))
    so = lambda n: jax.ShapeDtypeStruct((S5_T, n, 512), BF16)
    nl, ncx = B * N_CHUNK, B * N_CCHUNK
    return pl.pallas_call(
        _inproj0a_kernel,
        out_shape=(so(nl), so(nl), so(ncx)),
        grid=(BLK, S5_T // BLK),
        in_specs=[sspec(nl, D), sspec(ncx, D), pl.BlockSpec((D, SGU_COL0), lambda r, h: (0, 0))],
        out_specs=(sspec(nl, 512), sspec(nl, 512), sspec(ncx, 512)),
        compiler_params=_params("arbitrary", "arbitrary"),
        name="inproj0a",
    )(hs, hcs, w_in_f32)


SCAN_GROUPS = 4


def _scan_tiles(sre_ref, sim_ref, h_refs, n_tiles, carry, lams):
    row = lax.broadcasted_iota(jnp.int32, (8, 128), 0)
    lane = lax.broadcasted_iota(jnp.int32, (8, 128), 1)
    first = row < B
    fwd = lane < S5_P

    def body(k, c):
        of = pl.multiple_of(k * 8, 8)
        ob = pl.multiple_of((n_tiles - 1 - k) * 8, 8)
        out = []
        for gi in range(SCAN_GROUPS):
            lre, lim = lams[gi]
            hr, hi = c[2 * gi], c[2 * gi + 1]
            sr = jnp.where(fwd, sre_ref[gi, pl.ds(of, 8), :], pltpu.roll(sre_ref[gi, pl.ds(ob, 8), :], B, axis=0))
            si = jnp.where(fwd, sim_ref[gi, pl.ds(of, 8), :], pltpu.roll(sim_ref[gi, pl.ds(ob, 8), :], B, axis=0))
            h1r = lre * hr - lim * hi + sr
            h1i = lre * hi + lim * hr + si
            r1r = pltpu.roll(h1r, B, axis=0)
            r1i = pltpu.roll(h1i, B, axis=0)
            if h_refs is not None:
                fre_ref, fim_ref, bre_ref, bim_ref = h_refs
                er = jnp.where(first, hr, r1r)
                ei = jnp.where(first, hi, r1i)
                fre_ref[gi, pl.ds(of, 8), :] = er
                fim_ref[gi, pl.ds(of, 8), :] = ei
                bre_ref[gi, pl.ds(ob, 8), :] = pltpu.roll(er, B, axis=0)
                bim_ref[gi, pl.ds(ob, 8), :] = pltpu.roll(ei, B, axis=0)
            h2r = lre * r1r - lim * r1i + sr
            h2i = lre * r1i + lim * r1r + si
            out.append(jnp.where(first, pltpu.roll(h2r, B, axis=0), h2r))
            out.append(jnp.where(first, pltpu.roll(h2i, B, axis=0), h2i))
        return tuple(out)

    return lax.fori_loop(0, n_tiles, body, carry)


def _gather_group(slab_ref, src):
    halves = []
    for h in range(S5_T // BLK):
        acc = slab_ref[BLK * h]
        for s in range(1, BLK):
            acc = jnp.where(src == s, slab_ref[BLK * h + s], acc)
        halves.append(acc)
    return jnp.concatenate(halves, axis=1)


def _s5core_kernel(ul_ref, uc_ref, win_ref, wout_ref, mix_ref, l16_ref, o_ref,
                   u_ref, sre_ref, sim_ref, cre_ref, cim_ref, fre_ref, fim_ref, bre_ref, bim_ref, y_ref):
    nl = N_CHUNK * B
    ncx = N_CCHUNK * B
    blk_l = lax.broadcasted_iota(jnp.int32, (nl, 128), 1) >> H_SHIFT
    blk_c = lax.broadcasted_iota(jnp.int32, (ncx, 128), 1) >> H_SHIFT
    fwd = lax.broadcasted_iota(jnp.int32, (N_CHUNK, 128), 1) < S5_P
    for g0 in range(0, BLK, SCAN_GROUPS):
        for gi in range(SCAN_GROUPS):
            bg = g0 + gi
            win = win_ref[bg]
            src_l = ((blk_l - bg) & (BLK - 1)).astype(F32).astype(BF16)
            src_c = ((blk_c - bg) & (BLK - 1)).astype(F32).astype(BF16)
            u = _gather_group(ul_ref, src_l)
            u_ref[gi] = u
            sl = jnp.dot(u, win, preferred_element_type=F32)
            sc = jnp.dot(_gather_group(uc_ref, src_c), win, preferred_element_type=F32)
            for b in range(B):
                sre_ref[gi, pl.ds(b, N_CHUNK, stride=B), :] = sl[N_CHUNK * b:N_CHUNK * (b + 1), 0:128]
                sim_ref[gi, pl.ds(b, N_CHUNK, stride=B), :] = sl[N_CHUNK * b:N_CHUNK * (b + 1), 128:256]
                cre_ref[gi, pl.ds(b, N_CCHUNK, stride=B), :] = sc[N_CCHUNK * b:N_CCHUNK * (b + 1), 0:128]
                cim_ref[gi, pl.ds(b, N_CCHUNK, stride=B), :] = sc[N_CCHUNK * b:N_CCHUNK * (b + 1), 128:256]
        lams = [(jnp.broadcast_to(l16_ref[g0 + gi, 0:1, :], (8, 128)),
                 jnp.broadcast_to(l16_ref[g0 + gi, 1:2, :], (8, 128))) for gi in range(SCAN_GROUPS)]
        zero = tuple(jnp.zeros((8, 128), F32) for _ in range(2 * SCAN_GROUPS))
        carry = _scan_tiles(cre_ref, cim_ref, None, ncx // 8, zero, lams)
        _scan_tiles(sre_ref, sim_ref, (fre_ref, fim_ref, bre_ref, bim_ref), nl // 8, carry, lams)
        for gi in range(SCAN_GROUPS):
            bg = g0 + gi
            y = jnp.dot(u_ref[gi], mix_ref[bg], preferred_element_type=F32)
            hs = []
            for b in range(B):
                rows = pl.ds(b, N_CHUNK, stride=B)
                hs.append(jnp.concatenate([jnp.where(fwd, fre_ref[gi, rows, :], bre_ref[gi, rows, :]),
                                           jnp.where(fwd, fim_ref[gi, rows, :], bim_ref[gi, rows, :])], axis=1))
            hcat = jnp.concatenate(hs, axis=0).astype(BF16)
            y = y + jnp.dot(hcat, wout_ref[bg], preferred_element_type=F32)
            y_ref[bg] = y.astype(BF16)

    blk = blk_l.astype(F32).astype(BF16)
    for s in range(S5_T):
        h, r = s // BLK, s % BLK
        acc = None
        for j in range(BLK):
            piece = y_ref[(j - r) % BLK, :, 128 * h:128 * (h + 1)]
            acc = piece if acc is None else jnp.where(blk == j, piece, acc)
        o_ref[s] = acc


def _s5core(ul, uc, win, wout, mix, l16):
    TH = S5_T * S5_H
    nl = N_CHUNK * B
    ncx = N_CCHUNK * B
    g4 = lambda r, c: pl.BlockSpec((BLK, r, c), lambda q: (q, 0, 0))
    col = lambda n: pl.BlockSpec((S5_T, n, 128), lambda q: (0, 0, q))
    f32s = lambda n: pltpu.VMEM((SCAN_GROUPS, n, 128), F32)
    return pl.pallas_call(
        _s5core_kernel,
        out_shape=jax.ShapeDtypeStruct((S5_T, nl, S5_W), BF16),
        grid=(S5_G // BLK,),
        in_specs=[col(nl), col(ncx), g4(TH, TH), g4(TH, TH), g4(TH, TH), g4(8, 128)],
        out_specs=col(nl),
        scratch_shapes=[pltpu.VMEM((SCAN_GROUPS, nl, TH), BF16),
                        f32s(nl), f32s(nl), f32s(ncx), f32s(ncx), f32s(nl), f32s(nl), f32s(nl), f32s(nl),
                        pltpu.VMEM((BLK, nl, TH), BF16)],
        compiler_params=_params("arbitrary"),
        name="s5core",
    )(ul, uc, win, wout, mix, l16)


def _cast_on_first_step(src_ref, dst_ref, scale=None):
    @pl.when((pl.program_id(0) == 0) & (pl.program_id(1) == 0))
    def _():
        w = src_ref[...]
        dst_ref[...] = (w if scale is None else scale * w).astype(BF16)


def _const_spec(shape, index):
    return pl.BlockSpec(shape, lambda r, h: index, pipeline_mode=pl.Buffered(1))


def _s5tail_kernel(slat_ref, sza_ref, gluw32_ref, glub_ref, wtop32_ref, y_ref, gluw_ref, wtop_ref):
    _cast_on_first_step(gluw32_ref, gluw_ref)
    _cast_on_first_step(wtop32_ref, wtop_ref)
    unrot = (BLK - pl.program_id(0)) & (BLK - 1)
    for b in range(B):
        rows = slice(N_CHUNK * b, N_CHUNK * (b + 1))
        g = _gelu(_rot_blocks(slat_ref[rows, :].astype(F32), unrot))
        gate = _sigmoid(jnp.dot(g.astype(BF16), gluw_ref[...], preferred_element_type=F32) + glub_ref[...])
        a = (g * gate * sza_ref[rows, :].astype(F32)).astype(BF16)
        y_ref[rows, :] = jnp.dot(a, wtop_ref[...], preferred_element_type=F32).astype(BF16)


def _s5tail(slat, sza, glu_w, glu_b, w_out):
    slab = lambda r, h: r + BLK * h
    sspec = lambda w: pl.BlockSpec((None, N_CHUNK * B, w), lambda r, h: (slab(r, h), 0, 0))
    full = lambda *s: pl.BlockSpec(s, lambda r, h: (0,) * len(s))
    return pl.pallas_call(
        _s5tail_kernel,
        out_shape=jax.ShapeDtypeStruct((S5_T, N_CHUNK * B, D), BF16),
        grid=(BLK, S5_T // BLK),
        in_specs=[sspec(512), sspec(512), _const_spec((S5_W, S5_W), (0, 0)), full(1, 512),
                  _const_spec((S5_W, D), (0, 0))],
        out_specs=sspec(D),
        scratch_shapes=[pltpu.VMEM((S5_W, S5_W), BF16), pltpu.VMEM((S5_W, D), BF16)],
        compiler_params=_params("arbitrary", "arbitrary"),
        name="s5tail",
    )(slat, sza, glu_w, glu_b, w_out)


def _tail0_kernel(x_ref, ys5_ref, guz_ref, vln_ref, mod_ref, sguw_ref, sgub_ref, wbot32_ref, ng_ref, nb_ref, o_ref,
                  wbot_ref):
    _cast_on_first_step(wbot32_ref, wbot_ref)
    tm = x_ref.shape[0]
    lane = lax.broadcasted_iota(jnp.int32, (SGU_CHUNK, 128), 1)
    lo = lane < SGU_HD
    zero = jnp.zeros((SGU_CHUNK, 128), BF16)
    w_pair = [jnp.concatenate([sguw_ref[2 * pi].astype(BF16), sguw_ref[2 * pi + 1].astype(BF16)], axis=1)
              for pi in range(SGU_HEADS // 2)]
    chunks = []
    for ci in range(tm // SGU_CHUNK):
        v = vln_ref[ci * SGU_CHUNK:(ci + 1) * SGU_CHUNK, :]
        cols = []
        for pi in range(SGU_HEADS // 2):
            vp = v[:, 128 * pi:128 * (pi + 1)]
            bm = jnp.concatenate([jnp.where(lo, vp, zero), jnp.where(lo, zero, vp)], axis=0)
            cols.append(jnp.dot(w_pair[pi], bm, preferred_element_type=F32))
        chunks.append(jnp.concatenate(cols, axis=1) + sgub_ref[...])
    s = jnp.concatenate(chunks, axis=0)
    bsg = (guz_ref[...].astype(F32) * s).astype(BF16)
    perm = _chunk_transpose_perm()
    ys5 = jnp.concatenate(
        [jnp.dot(perm, ys5_ref[:, S5_T * j:S5_T * (j + 1), :].reshape(PERM_ROWS, D), preferred_element_type=F32)
         for j in range(tm // PERM_ROWS)], axis=0)
    y = ys5 + jnp.dot(bsg, wbot_ref[...], preferred_element_type=F32)
    gmod = mod_ref[:, 2 * D:3 * D]
    o_ref[...] = _layer_norm(DN_ALPHA * x_ref[...] + gmod * y, ng_ref[0:1, :], nb_ref[0:1, :])


def _tail0(x, ys5, guz, vln, mod, sguw, sgub, w_bot, ng, nb, tm=TOKEN_TILE):
    nct = tm // S5_T
    t512 = pl.BlockSpec((None, tm, 512), lambda b, i: (b, i, 0))
    tD = pl.BlockSpec((None, tm, D), lambda b, i: (b, i, 0))
    full = lambda *s: pl.BlockSpec(s, lambda b, i: (0,) * len(s))
    return pl.pallas_call(
        _tail0_kernel,
        out_shape=jax.ShapeDtypeStruct((B, L, D), F32),
        grid=(B, L // tm),
        in_specs=[tD, pl.BlockSpec((S5_T, nct, D), lambda b, i: (0, b * (N_CHUNK // nct) + i, 0)), t512, t512,
                  _mod_spec(0),
                  full(SGU_HEADS, SGU_CHUNK, SGU_CHUNK), full(SGU_CHUNK, 512),
                  _const_spec((SGU_W, D), (1, 0)), full(DEPTH, D), full(DEPTH, D)],
        out_specs=tD,
        scratch_shapes=[pltpu.VMEM((SGU_W, D), BF16)],
        compiler_params=_params("arbitrary", "arbitrary"),
        name="tail0",
    )(x, ys5, guz, vln, mod, sguw, sgub, w_bot, ng, nb)


CONV_C = D // 2
TILE_ROWS = TOKEN_TILE // GRID_W


def _grid_transpose_in(v, o_ref):
    perm = _chunk_transpose_perm()
    for q in range(GRID_W // S5_T):
        seg = jnp.concatenate([v[GRID_W * r + S5_T * q:GRID_W * r + S5_T * (q + 1), :] for r in range(TILE_ROWS)],
                              axis=0)
        t = jnp.dot(perm, seg, preferred_element_type=F32).astype(BF16)
        o_ref[S5_T * q:S5_T * (q + 1), :, :] = t.reshape(S5_T, TILE_ROWS, v.shape[1])


def _inproj1_kernel(x_ref, mod_ref, w_ref, hgr_ref, hgc_ref, h_ref):
    shift = mod_ref[:, 0:D]
    scale = mod_ref[:, D:2 * D]
    h = (x_ref[...] * (1.0 + scale) + shift).astype(BF16)
    h_ref[...] = h
    dot = lambda lo: jnp.dot(h, w_ref[:, lo:lo + CONV_C].astype(BF16), preferred_element_type=F32)
    hgr_ref[...] = (dot(0) * _sigmoid(dot(D))).astype(BF16)
    _grid_transpose_in((dot(CONV_C) * _sigmoid(dot(D + CONV_C))).astype(BF16), hgc_ref)


def _inproj1(x, mod, w_in_f32, tm=TOKEN_TILE):
    tile = lambda w: pl.BlockSpec((None, tm, w), lambda b, i: (b, i, 0))
    return pl.pallas_call(
        _inproj1_kernel,
        out_shape=(jax.ShapeDtypeStruct((B, L, CONV_C), BF16),
                   jax.ShapeDtypeStruct((B, GRID_W, GRID_W, CONV_C), BF16),
                   jax.ShapeDtypeStruct((B, L, D), BF16)),
        grid=(B, L // tm),
        in_specs=[tile(D),
                  _mod_spec(1),
                  pl.BlockSpec((D, 2 * D), lambda b, i: (0, 0), pipeline_mode=pl.Buffered(1))],
        out_specs=(tile(CONV_C), pl.BlockSpec((None, GRID_W, TILE_ROWS, CONV_C), lambda b, i: (b, 0, i, 0)),
                   tile(D)),
        compiler_params=_params("arbitrary", "arbitrary"),
        name="inproj1",
    )(x, mod, w_in_f32)


DFT_N = 2 * GRID_W
TAPS_PAD = CONV_K + 1


def _dft_constants():
    th = 2.0 * math.pi / DFT_N
    f = np.arange(GRID_W, dtype=np.float64)[:, None]
    p = np.arange(GRID_W, dtype=np.float64)[None, :]
    cosm = np.cos(th * f * p)
    sinm = np.sin(th * f * p)
    alt = np.where(np.arange(GRID_W) % 2 == 0, 1.0, -1.0)
    fwd = np.concatenate([cosm, alt[None, :], sinm[1:]], axis=0)
    cf = np.where(np.arange(GRID_W) == 0, 1.0, 2.0) / DFT_N
    inv = np.concatenate([cosm.T * cf[None, :], (alt / DFT_N)[:, None], sinm.T[:, 1:] * (2.0 / DFT_N)], axis=1)
    sft = (CONV_HALF - np.arange(TAPS_PAD, dtype=np.float64))[None, :]
    live = (np.arange(TAPS_PAD) < CONV_K).astype(np.float64)[None, :]
    f64 = np.where(f == 0, float(GRID_W), f)
    f32 = lambda a: jnp.asarray(a.astype(np.float32))
    return (f32(fwd).astype(BF16), f32(inv).astype(BF16),
            f32(np.cos(th * f * sft) * live), f32(np.sin(th * f * sft) * live), f32(np.cos(th * f64 * sft) * live))


def _fconv_kernel(h_ref, w_ref, b_ref, fwd_ref, inv_ref, c1_ref, s3_ref, c4_ref, o_ref, taps_ref):
    hp = lax.Precision.HIGHEST
    taps_ref[...] = jnp.zeros(taps_ref.shape, F32)
    taps_ref[0:CONV_K, :] = w_ref[...]
    taps = taps_ref[...]
    g_re = jnp.dot(c1_ref[...], taps, preferred_element_type=F32, precision=hp)
    g_im = jnp.dot(s3_ref[...], taps, preferred_element_type=F32, precision=hp)
    g_r2 = jnp.dot(c4_ref[...], taps, preferred_element_type=F32, precision=hp)
    fwd = fwd_ref[...]
    inv = inv_ref[...]
    bias = b_ref[...]
    n_runs = h_ref.shape[0] // GRID_W
    rows = lambda r: slice(GRID_W * r, GRID_W * (r + 1))
    forward = lambda r: jnp.dot(fwd, h_ref[rows(r), :], preferred_element_type=F32)
    ahead = 2
    specs = [forward(r) for r in range(ahead)]
    for r in range(n_runs):
        if r + ahead < n_runs:
            specs.append(forward(r + ahead))
        spec = specs[r]
        a, bm = spec[0:GRID_W], spec[GRID_W:DFT_N]
        prod = jnp.concatenate([a * g_re - bm * g_im, a * g_im + bm * g_r2], axis=0).astype(BF16)
        o_ref[rows(r), :] = (jnp.dot(inv, prod, preferred_element_type=F32) + bias).astype(BF16)


def _fconv(h, taps, bias, half, consts, tm=L):
    fwd, inv, c1, s3, c4 = consts
    c = h.shape[-1]
    tile = pl.BlockSpec((None, tm, c), lambda b, i: (b, i, 0))
    full = lambda *s: pl.BlockSpec(s, lambda b, i: (0,) * len(s))
    cols = lambda r: pl.BlockSpec((None, r, c), lambda b, i: (0, 0, half))
    return pl.pallas_call(
        _fconv_kernel,
        out_shape=jax.ShapeDtypeStruct(h.shape, BF16),
        grid=(B, L // tm),
        in_specs=[tile, cols(CONV_K), cols(1), full(DFT_N, GRID_W), full(GRID_W, DFT_N),
                  full(GRID_W, TAPS_PAD), full(GRID_W, TAPS_PAD), full(GRID_W, TAPS_PAD)],
        out_specs=tile,
        scratch_shapes=[pltpu.VMEM((TAPS_PAD, c), F32)],
        compiler_params=_params("arbitrary", "arbitrary"),
        name="fconv",
    )(h, taps, bias, fwd, inv, c1, s3, c4)


ROW_BLOCK = 32


def _row_blocks(n_rows):
    return [slice(ROW_BLOCK * k, ROW_BLOCK * (k + 1)) for k in range(n_rows // ROW_BLOCK)]


def _tail1_kernel(x_ref, hcr_ref, hcc_ref, h1_ref, wz32_ref, mod_ref, lng_ref, lnb_ref, wout_ref, ng_ref, nb_ref,
                  o_ref, col_ref, z_ref, m_ref, y_ref, wz_ref):
    _cast_on_first_step(wz32_ref, wz_ref, scale=0.5)
    tm = x_ref.shape[0]
    perm = _chunk_transpose_perm()
    for q in range(GRID_W // S5_T):
        blk = hcc_ref[S5_T * q:S5_T * (q + 1), :, :].reshape(PERM_ROWS, CONV_C)
        t = jnp.dot(perm, blk, preferred_element_type=F32)
        for r in range(TILE_ROWS):
            col_ref[GRID_W * r + S5_T * q:GRID_W * r + S5_T * (q + 1), :] = t[S5_T * r:S5_T * (r + 1), :]
    z_ref[...] = jnp.dot(h1_ref[...], wz_ref[...], preferred_element_type=F32)
    lng, lnb = 0.5 * lng_ref[...], 0.5 * lnb_ref[...]
    for rows in _row_blocks(tm):
        hc = jnp.concatenate([hcr_ref[rows, :].astype(F32), col_ref[rows, :]], axis=1)
        m_ref[rows, :] = (_silu_of_half(_layer_norm(hc, lng, lnb)) * _silu_of_half(z_ref[rows, :])).astype(BF16)
    y_ref[...] = jnp.dot(m_ref[...], wout_ref[...], preferred_element_type=F32)
    gmod = mod_ref[:, 2 * D:3 * D]
    ng, nb = ng_ref[1:2, :], nb_ref[1:2, :]
    for rows in _row_blocks(tm):
        o_ref[rows, :] = _layer_norm(DN_ALPHA * x_ref[rows, :] + gmod * y_ref[rows, :], ng, nb)


def _tail1(x, hc_row, hc_col, h1, w_in_f32, mod, ln_g, ln_b, w_out, ng, nb, tm=TOKEN_TILE):
    tile = lambda w: pl.BlockSpec((None, tm, w), lambda b, i: (b, i, 0))
    full = lambda *s: pl.BlockSpec(s, lambda b, i: (0,) * len(s))
    return pl.pallas_call(
        _tail1_kernel,
        out_shape=jax.ShapeDtypeStruct((B, L, D), F32),
        grid=(B, L // tm),
        in_specs=[tile(D), tile(CONV_C),
                  pl.BlockSpec((None, GRID_W, TILE_ROWS, CONV_C), lambda b, i: (b, 0, i, 0)),
                  tile(D), _const_spec((D, D), (0, 2)), _mod_spec(1),
                  full(1, D), full(1, D), full(D, D), full(DEPTH, D), full(DEPTH, D)],
        out_specs=tile(D),
        scratch_shapes=[pltpu.VMEM((tm, CONV_C), F32), pltpu.VMEM((tm, D), F32), pltpu.VMEM((tm, D), BF16),
                        pltpu.VMEM((tm, D), F32), pltpu.VMEM((D, D), BF16)],
        compiler_params=_params("arbitrary", "arbitrary"),
        name="tail1",
    )(x, hc_row, hc_col, h1, w_in_f32, mod, ln_g, ln_b, w_out, ng, nb)


def kernel(x, c, ctx, c_ctx, mod_w, mod_b, norm_g, norm_b, ev_w_in, ev_w_out, s5_lam_re, s5_lam_im, s5_log_dt, s5_b_re, s5_b_im, s5_c_re, s5_c_im, s5_d, glu_w, glu_b, sgu_ln_g, sgu_ln_b, sgu_w, sgu_b, od_w_in, od_w_out, dw_w, dw_b, conv_ln_g, conv_ln_b):
    TH = S5_T * S5_H
    row = lambda v: v.reshape(1, -1)

    cond8 = jnp.concatenate([c, c_ctx[None], jnp.zeros((3, D), F32)], axis=0)
    mods = _adaln(cond8, mod_w, mod_b)

    ldt = jnp.broadcast_to(s5_log_dt[0][:, :, None], (2, S5_G, S5_P))
    fb = lambda s: jnp.concatenate([s[:, 0], s[:, 1]], axis=-1)
    lam3 = fb(jnp.stack([s5_lam_re[0], s5_lam_im[0], ldt]))
    bt = fb(jnp.swapaxes(jnp.stack([s5_b_re[0], s5_b_im[0]]), -1, -2))
    cn = fb(jnp.stack([s5_c_re[0], s5_c_im[0]]))
    win, wout, mix, l16 = _s5_weights(lam3, s5_d, bt, cn)

    guz, vln, hs = _inproj0n(x, mods, ev_w_in[0], row(sgu_ln_g[0]), row(sgu_ln_b[0]))
    ua, sza, ua_c = _inproj0a(hs, _ctx_slabs(ctx, mods), ev_w_in[0])
    s_lat = _s5core(ua, ua_c, win, wout, mix, l16)
    y_s5 = _s5tail(s_lat, sza, glu_w[0], row(glu_b[0]), ev_w_out[0])
    sgub = jnp.repeat(sgu_b[0].T, SGU_HD, axis=1)
    x1 = _tail0(x, y_s5, guz, vln, mods, sgu_w[0], sgub, ev_w_out[0], norm_g, norm_b)

    hg_row, hg_col, h1 = _inproj1(x1, mods, od_w_in[0])
    consts = _dft_constants()
    bias = dw_b.reshape(1, 1, 2 * CONV_C)
    hc_row = _fconv(hg_row, dw_w, bias, 0, consts)
    hc_col = _fconv(hg_col.reshape(B, L, CONV_C), dw_w, bias, 1, consts)
    return _tail1(x1, hc_row, hc_col.reshape(B, GRID_W, GRID_W, CONV_C), h1, od_w_in[0], mods,
                  row(conv_ln_g[0]), row(conv_ln_b[0]), od_w_out[0].astype(BF16), norm_g, norm_b)
```

```python
import functools
import math

import jax
import jax.numpy as jnp
import numpy as np
from jax import lax
from jax.experimental import pallas as pl
from jax.experimental.pallas import tpu as pltpu

D = 1024
B = 4
L = 4096
CTX = 256
GRID_W = 64
S5_W = 512
S5_G = 32
S5_H = 16
H_SHIFT = 4
BLK = 128 // S5_H
S5_P = 64
S5_T = 16
SGU_W = 512
SGU_HEADS = 8
SGU_HD = 64
SGU_CHUNK = 128
CONV_K = 31
CONV_HALF = CONV_K // 2
EVEN_IN = 2560
SGU_COL0 = 2 * S5_W
ODD_IN = 3072
DEPTH = 2
DN_ALPHA = (2 * DEPTH) ** 0.25
LN_EPS = 1e-5
N_CHUNK = L // S5_T
N_CCHUNK = CTX // S5_T
VMEM_LIMIT_V7X = 56 * 1024 * 1024
TOKEN_TILE = 1024

F32 = jnp.float32
BF16 = jnp.bfloat16


GELU_C = math.sqrt(2.0 / math.pi)


def _gelu(x):
    hx = 0.5 * x
    return hx * jnp.tanh(x * ((x * x) * (0.044715 * GELU_C) + GELU_C)) + hx


def _sigmoid(x):
    return 0.5 * jnp.tanh(0.5 * x) + 0.5


def _silu_of_half(hx):
    return hx * jnp.tanh(hx) + hx


def _silu(x):
    return _silu_of_half(0.5 * x)


def _layer_norm(x, g, b):
    mu = jnp.mean(x, axis=-1, keepdims=True)
    xc = x - mu
    var = jnp.mean(xc * xc, axis=-1, keepdims=True)
    return xc * lax.rsqrt(var + LN_EPS) * g + b


def _params(*sem):
    return pltpu.CompilerParams(dimension_semantics=sem, vmem_limit_bytes=VMEM_LIMIT_V7X)


ADALN_ROWS = 8
ADALN_TK = 256


def _adaln_kernel(c_ref, cctx_ref, w_ref, b_ref, o_ref, cond_ref, acc_ref):
    layer, k = pl.program_id(0), pl.program_id(1)
    cond_ref[...] = jnp.zeros(cond_ref.shape, F32)
    cond_ref[0:B, :] = c_ref[...]
    cond_ref[B:B + 1, :] = cctx_ref[...]

    def split(v):
        hi = v.astype(BF16)
        return hi, (v - hi.astype(F32)).astype(BF16)

    a_hi, a_lo = split(_silu(cond_ref[:, pl.ds(pl.multiple_of(k * ADALN_TK, ADALN_TK), ADALN_TK)]))
    w_hi, w_lo = split(w_ref[...])
    dot = functools.partial(jnp.dot, preferred_element_type=F32)
    both = dot(jnp.concatenate([a_hi.astype(F32), a_lo.astype(F32)], axis=0).astype(BF16), w_hi)
    part = both[:ADALN_ROWS] + both[ADALN_ROWS:] + dot(a_hi, w_lo)

    @pl.when(k == 0)
    def _():
        acc_ref[...] = part + b_ref[pl.ds(layer, 1), :]

    @pl.when(k > 0)
    def _():
        acc_ref[...] += part

    @pl.when(k == pl.num_programs(1) - 1)
    def _():
        for r in range(ADALN_ROWS):
            o_ref[r] = acc_ref[r:r + 1, :]


def _adaln(c, c_ctx, mod_w, mod_b):
    full = lambda *s: pl.BlockSpec(s, lambda l, k: (0,) * len(s))
    return pl.pallas_call(
        _adaln_kernel,
        out_shape=jax.ShapeDtypeStruct((DEPTH, ADALN_ROWS, 1, 3 * D), F32),
        grid=(DEPTH, D // ADALN_TK),
        in_specs=[full(B, D), full(1, D),
                  pl.BlockSpec((None, ADALN_TK, 3 * D), lambda l, k: (l, k, 0)),
                  full(DEPTH, 3 * D)],
        out_specs=pl.BlockSpec((None, ADALN_ROWS, 1, 3 * D), lambda l, k: (l, 0, 0, 0)),
        scratch_shapes=[pltpu.VMEM((ADALN_ROWS, D), F32), pltpu.VMEM((ADALN_ROWS, 3 * D), F32)],
        compiler_params=_params("arbitrary", "arbitrary"),
        name="adaln",
    )(c, c_ctx.reshape(1, D), mod_w, mod_b)


def _mod_spec(layer, cond=None):
    if cond is None:
        return pl.BlockSpec((None, None, 1, 3 * D), lambda b, i: (layer, b, 0, 0))
    return pl.BlockSpec((None, None, 1, 3 * D), lambda b: (layer, cond, 0, 0))


S5W_GROUPS = BLK


def _cpow(base_pows, j):
    re = None
    im = None
    for k, (pr, pi) in enumerate(base_pows):
        bit = ((j >> k) & 1) == 1
        mr = jnp.where(bit, pr, 1.0)
        mi = jnp.where(bit, pi, 0.0)
        if re is None:
            re, im = mr, mi
        else:
            re, im = re * mr - im * mi, re * mi + im * mr
    return re, im


def _squarings(pr, pi, n):
    out = [(pr, pi)]
    for _ in range(n - 1):
        pr, pi = pr * pr - pi * pi, 2.0 * pr * pi
        out.append((pr, pi))
    return out


def _shift_lanes(x, n):
    lane = lax.broadcasted_iota(jnp.int32, (S5_H, 128), 1)
    lo, hi = x[:, :128], x[:, 128:]
    if n == 0:
        return x
    if n < 128:
        rlo = pltpu.roll(lo, n, axis=1)
        rhi = pltpu.roll(hi, n, axis=1)
        return jnp.concatenate([jnp.where(lane >= n, rlo, 0.0), jnp.where(lane >= n, rhi, rlo)], axis=1)
    m = n - 128
    rlo = lo if m == 0 else pltpu.roll(lo, m, axis=1)
    return jnp.concatenate([jnp.zeros_like(lo), jnp.where(lane >= m, rlo, 0.0)], axis=1)


def _unshift_lanes(x, n):
    lane = lax.broadcasted_iota(jnp.int32, (S5_H, 128), 1)
    lo, hi = x[:, :128], x[:, 128:]
    if n == 0:
        return x
    if n < 128:
        rlo = pltpu.roll(lo, 128 - n, axis=1)
        rhi = pltpu.roll(hi, 128 - n, axis=1)
        keep = lane < 128 - n
        return jnp.concatenate([jnp.where(keep, rlo, rhi), jnp.where(keep, rhi, 0.0)], axis=1)
    m = n - 128
    rhi = hi if m == 0 else pltpu.roll(hi, 128 - m, axis=1)
    return jnp.concatenate([jnp.where(lane < 128 - m, rhi, 0.0), jnp.zeros_like(lo)], axis=1)


def _s5w_group(gi, bg, disc, d_t, bt_ref, cn_ref, win_ref, wout_ref, mix_ref, l16_ref):
    TH = S5_T * S5_H

    def chunk_pos(idx):
        return (((idx >> H_SHIFT) - bg) & (BLK - 1)) + ((idx >> 7) << 3)

    lr, li, cr, ci = [v[gi:gi + 1] for v in disc]
    pows_row = _squarings(lr, li, 5)
    l16_ref[gi, 0:1, :] = pows_row[4][0]
    l16_ref[gi, 1:2, :] = pows_row[4][1]
    l16_ref[gi, 2:8, :] = jnp.zeros((6, 128), F32)
    btr = bt_ref[0, gi]
    bti = bt_ref[1, gi]
    bbr = cr * btr - ci * bti
    bbi = cr * bti + ci * btr
    blk16 = lax.broadcasted_iota(jnp.int32, (S5_T, 128), 0)
    is_f16 = lax.broadcasted_iota(jnp.int32, (S5_T, 128), 1) < S5_P
    pos16 = chunk_pos(blk16 << H_SHIFT)
    pr16, pi16 = _cpow(pows_row[:4], jnp.where(is_f16, S5_T - 1 - pos16, pos16))
    rep_rows = lambda v: jnp.broadcast_to(v[:, None, :], (S5_T, S5_H, 128)).reshape(TH, 128)
    pr, pi = rep_rows(pr16), rep_rows(pi16)
    tbr = jnp.broadcast_to(bbr[None], (S5_T, S5_H, 128)).reshape(TH, 128)
    tbi = jnp.broadcast_to(bbi[None], (S5_T, S5_H, 128)).reshape(TH, 128)
    win_ref[gi, :, 0:128] = (pr * tbr - pi * tbi).astype(BF16)
    win_ref[gi, :, 128:256] = (pr * tbi + pi * tbr).astype(BF16)

    hp = lax.Precision.HIGHEST
    dot = functools.partial(jnp.dot, preferred_element_type=F32, precision=hp)
    def col256(r):
        col = jnp.broadcast_to(r, (2 * S5_P, 2 * S5_P)).T
        return jnp.concatenate([col, col], axis=1)

    def tiled_t(cn):
        t8 = jnp.broadcast_to(cn[None], (BLK, S5_H, 2 * S5_P)).reshape(2 * S5_P, 2 * S5_P).T
        return jnp.concatenate([t8, t8], axis=1)

    cpows = _squarings(col256(lr), col256(li), 4)
    row = lax.broadcasted_iota(jnp.int32, (2 * S5_P, TH), 0)
    lane_w = lax.broadcasted_iota(jnp.int32, (2 * S5_P, TH), 1)
    t_idx = chunk_pos(lane_w)
    j_idx = lane_w >> H_SHIFT
    is_f = row < S5_P
    ctr = tiled_t(cn_ref[0, gi])
    cti = tiled_t(cn_ref[1, gi])
    er, ei = _cpow(cpows, jnp.where(is_f, t_idx, S5_T - 1 - t_idx))
    er, ei = er * cpows[0][0] - ei * cpows[0][1], er * cpows[0][1] + ei * cpows[0][0]
    wr = ctr * er - cti * ei
    wi = ctr * ei + cti * er
    wout_ref[gi, 0:128, :] = wr.astype(BF16)
    wout_ref[gi, 128:256, :] = (-wi).astype(BF16)
    kr, ki = _cpow(cpows, jnp.where(is_f, j_idx, S5_T - 1 - j_idx))
    ekr = ctr * kr - cti * ki
    eki = ctr * ki + cti * kr
    lane16 = lax.broadcasted_iota(jnp.int32, (S5_H, 128), 1)
    mf = lane16 < S5_P
    kkf = dot(jnp.where(mf, bbr, 0.0), ekr) - dot(jnp.where(mf, bbi, 0.0), eki)
    kkb = dot(jnp.where(mf, 0.0, bbr), ekr) - dot(jnp.where(mf, 0.0, bbi), eki)
    d_rows = d_t[S5_H * gi:S5_H * (gi + 1), :]
    dl = jnp.concatenate([d_rows, d_rows], axis=1)
    r16 = lax.broadcasted_iota(jnp.int32, (S5_H, TH), 0)
    l256 = lax.broadcasted_iota(jnp.int32, (S5_H, TH), 1)
    rot = bg * S5_H
    for s in range(S5_T):
        blk = _shift_lanes(kkf, S5_H * s) + _unshift_lanes(kkb, S5_H * (S5_T - 1 - s))
        blk = blk + jnp.where(l256 == r16 + S5_H * s, dl, 0.0)
        if rot:
            blk = jnp.concatenate([pltpu.roll(blk[:, :128], rot, axis=1), pltpu.roll(blk[:, 128:], rot, axis=1)], axis=1)
        rho = ((s + bg) & (BLK - 1)) + (s & BLK)
        mix_ref[gi, S5_H * rho:S5_H * (rho + 1), :] = blk.astype(BF16)


def _s5w_kernel(lam_ref, d_ref, *refs):
    d_t = jnp.broadcast_to(d_ref[...], (S5W_GROUPS * S5_H, S5W_GROUPS * S5_H)).T
    lr = lam_ref[0]
    li = lam_ref[1]
    dt = jnp.exp(lam_ref[2])
    mag = jnp.exp(lr * dt)
    br = mag * jnp.cos(li * dt)
    bi = mag * jnp.sin(li * dt)
    inv = 1.0 / (lr * lr + li * li)
    nr = br - 1.0
    disc = (br, bi, (nr * lr + bi * li) * inv, (bi * lr - nr * li) * inv)
    for gi in range(S5W_GROUPS):
        _s5w_group(gi, gi % BLK, disc, d_t, *refs)


def _s5_weights(lam3, d, bt, cn):
    TH = S5_T * S5_H
    g3 = lambda r, c: pl.BlockSpec((S5W_GROUPS, r, c), lambda g: (g, 0, 0))
    ri = pl.BlockSpec((2, S5W_GROUPS, S5_H, 2 * S5_P), lambda g: (0, g, 0, 0))
    wshape = jax.ShapeDtypeStruct((S5_G, TH, TH), BF16)
    return pl.pallas_call(
        _s5w_kernel,
        out_shape=(wshape, wshape, wshape, jax.ShapeDtypeStruct((S5_G, 8, 128), F32)),
        grid=(S5_G // S5W_GROUPS,),
        in_specs=[pl.BlockSpec((3, S5W_GROUPS, 2 * S5_P), lambda g: (0, g, 0)),
                  pl.BlockSpec((1, S5W_GROUPS * S5_H), lambda g: (0, g)), ri, ri],
        out_specs=(g3(TH, TH), g3(TH, TH), g3(TH, TH), g3(8, 128)),
        compiler_params=_params("arbitrary"),
        name="s5_weights",
    )(lam3, d, bt, cn)


def _rot_blocks(v, r):
    cols = [pltpu.roll(v[:, 128 * q:128 * (q + 1)], S5_H * r, axis=1) for q in range(v.shape[1] // 128)]
    return jnp.concatenate(cols, axis=1)


PERM_ROWS = S5_T * S5_T
assert CTX == PERM_ROWS


def _chunk_transpose_perm():
    ri = lax.broadcasted_iota(jnp.int32, (PERM_ROWS, PERM_ROWS), 0)
    ci = lax.broadcasted_iota(jnp.int32, (PERM_ROWS, PERM_ROWS), 1)
    hit = ((ri >> H_SHIFT) == (ci & (S5_T - 1))) & ((ri & (S5_T - 1)) == (ci >> H_SHIFT))
    return jnp.where(hit, 1.0, 0.0).astype(BF16)


def _inproj0n_kernel(x_ref, mod_ref, w_ref, lng_ref, lnb_ref, guz_ref, vln_ref, hs_ref):
    shift = mod_ref[:, 0:D]
    scale = mod_ref[:, D:2 * D]
    hb = (x_ref[...] * (1.0 + scale) + shift).astype(BF16)
    perm = _chunk_transpose_perm()
    for j in range(hb.shape[0] // PERM_ROWS):
        blk = jnp.dot(perm, hb[PERM_ROWS * j:PERM_ROWS * (j + 1), :], preferred_element_type=F32).astype(BF16)
        for s in range(S5_T):
            hs_ref[s, S5_T * j:S5_T * (j + 1), :] = blk[S5_T * s:S5_T * (s + 1), :]
    dot = lambda lo: jnp.dot(hb, w_ref[:, SGU_COL0 + lo:SGU_COL0 + lo + 512].astype(BF16),
                             preferred_element_type=F32)
    guz_ref[...] = (_gelu(dot(0)) * _silu(dot(1024))).astype(BF16)
    vln_ref[...] = _layer_norm(_gelu(dot(512)), lng_ref[...], lnb_ref[...]).astype(BF16)


def _inproj0n(x, mod, w_in_f32, ln_g, ln_b, tm=TOKEN_TILE):
    nct = tm // S5_T
    o = jax.ShapeDtypeStruct((B, L, 512), BF16)
    ospec = pl.BlockSpec((None, tm, 512), lambda b, i: (b, i, 0))
    full = lambda *s: pl.BlockSpec(s, lambda b, i: (0,) * len(s))
    return pl.pallas_call(
        _inproj0n_kernel,
        out_shape=(o, o, jax.ShapeDtypeStruct((S5_T, B * N_CHUNK, D), BF16)),
        grid=(B, L // tm),
        in_specs=[pl.BlockSpec((None, tm, D), lambda b, i: (b, i, 0)),
                  _mod_spec(0),
                  pl.BlockSpec((D, EVEN_IN), lambda b, i: (0, 0), pipeline_mode=pl.Buffered(1)),
                  full(1, 512), full(1, 512)],
        out_specs=(ospec, ospec,
                   pl.BlockSpec((S5_T, nct, D), lambda b, i: (0, b * (N_CHUNK // nct) + i, 0))),
        compiler_params=_params("arbitrary", "arbitrary"),
        name="inproj0n",
    )(x, mod, w_in_f32, ln_g, ln_b)


def _ctx_slabs_kernel(x_ref, mod_ref, hs_ref):
    hb = (x_ref[...] * (1.0 + mod_ref[:, D:2 * D]) + mod_ref[:, 0:D]).astype(BF16)
    blk = jnp.dot(_chunk_transpose_perm(), hb, preferred_element_type=F32).astype(BF16)
    for s in range(S5_T):
        hs_ref[s] = blk[N_CCHUNK * s:N_CCHUNK * (s + 1), :]


def _ctx_slabs(ctx, mod_c):
    return pl.pallas_call(
        _ctx_slabs_kernel,
        out_shape=jax.ShapeDtypeStruct((S5_T, B * N_CCHUNK, D), BF16),
        grid=(B,),
        in_specs=[pl.BlockSpec((None, CTX, D), lambda b: (b, 0, 0)),
                  _mod_spec(0, cond=B)],
        out_specs=pl.BlockSpec((S5_T, N_CCHUNK, D), lambda b: (0, b, 0)),
        compiler_params=_params("arbitrary"),
        name="ctx_slabs",
    )(ctx, mod_c)


def _inproj0a_kernel(hs_ref, hc_ref, w_ref, ua_ref, sza_ref, uc_ref):
    r = pl.program_id(0)
    h = hs_ref[...]
    w_ua = w_ref[:, 0:512].astype(BF16)
    ua_ref[...] = _rot_blocks(jnp.dot(h, w_ua, preferred_element_type=F32), r).astype(BF16)
    sza_ref[...] = _silu(jnp.dot(h, w_ref[:, 512:1024].astype(BF16), preferred_element_type=F32)).astype(BF16)
    uc_ref[...] = _rot_blocks(jnp.dot(hc_ref[...], w_ua, preferred_element_type=F32), r).astype(BF16)


def _inproj0a(hs, hcs, w_in_f32):
    slab = lambda r, h: r + BLK * h
    sspec = lambda n, w: pl.BlockSpec((None, n, w), lambda r, h: (slab(r, h), 0, 0))
    so = lambda n: jax.ShapeDtypeStruct((S5_T, n, 512), BF16)
    nl, ncx = B * N_CHUNK, B * N_CCHUNK
    return pl.pallas_call(
        _inproj0a_kernel,
        out_shape=(so(nl), so(nl), so(ncx)),
        grid=(BLK, S5_T // BLK),
        in_specs=[sspec(nl, D), sspec(ncx, D), pl.BlockSpec((D, SGU_COL0), lambda r, h: (0, 0))],
        out_specs=(sspec(nl, 512), sspec(nl, 512), sspec(ncx, 512)),
        compiler_params=_params("arbitrary", "arbitrary"),
        name="inproj0a",
    )(hs, hcs, w_in_f32)


SCAN_GROUPS = 4


def _scan_tiles(sre_ref, sim_ref, h_refs, n_tiles, carry, lams):
    row = lax.broadcasted_iota(jnp.int32, (8, 128), 0)
    lane = lax.broadcasted_iota(jnp.int32, (8, 128), 1)
    first = row < B
    fwd = lane < S5_P

    def body(k, c):
        of = pl.multiple_of(k * 8, 8)
        ob = pl.multiple_of((n_tiles - 1 - k) * 8, 8)
        out = []
        for gi in range(SCAN_GROUPS):
            lre, lim = lams[gi]
            hr, hi = c[2 * gi], c[2 * gi + 1]
            sr = jnp.where(fwd, sre_ref[gi, pl.ds(of, 8), :], pltpu.roll(sre_ref[gi, pl.ds(ob, 8), :], B, axis=0))
            si = jnp.where(fwd, sim_ref[gi, pl.ds(of, 8), :], pltpu.roll(sim_ref[gi, pl.ds(ob, 8), :], B, axis=0))
            h1r = lre * hr - lim * hi + sr
            h1i = lre * hi + lim * hr + si
            r1r = pltpu.roll(h1r, B, axis=0)
            r1i = pltpu.roll(h1i, B, axis=0)
            if h_refs is not None:
                fre_ref, fim_ref, bre_ref, bim_ref = h_refs
                er = jnp.where(first, hr, r1r)
                ei = jnp.where(first, hi, r1i)
                fre_ref[gi, pl.ds(of, 8), :] = er
                fim_ref[gi, pl.ds(of, 8), :] = ei
                bre_ref[gi, pl.ds(ob, 8), :] = pltpu.roll(er, B, axis=0)
                bim_ref[gi, pl.ds(ob, 8), :] = pltpu.roll(ei, B, axis=0)
            h2r = lre * r1r - lim * r1i + sr
            h2i = lre * r1i + lim * r1r + si
            out.append(jnp.where(first, pltpu.roll(h2r, B, axis=0), h2r))
            out.append(jnp.where(first, pltpu.roll(h2i, B, axis=0), h2i))
        return tuple(out)

    return lax.fori_loop(0, n_tiles, body, carry)


def _gather_group(slab_ref, src):
    halves = []
    for h in range(S5_T // BLK):
        acc = slab_ref[BLK * h]
        for s in range(1, BLK):
            acc = jnp.where(src == s, slab_ref[BLK * h + s], acc)
        halves.append(acc)
    return jnp.concatenate(halves, axis=1)


def _s5core_kernel(ul_ref, uc_ref, win_ref, wout_ref, mix_ref, l16_ref, o_ref,
                   u_ref, sre_ref, sim_ref, cre_ref, cim_ref, fre_ref, fim_ref, bre_ref, bim_ref, y_ref):
    nl = N_CHUNK * B
    ncx = N_CCHUNK * B
    blk_l = lax.broadcasted_iota(jnp.int32, (nl, 128), 1) >> H_SHIFT
    blk_c = lax.broadcasted_iota(jnp.int32, (ncx, 128), 1) >> H_SHIFT
    fwd = lax.broadcasted_iota(jnp.int32, (N_CHUNK, 128), 1) < S5_P
    for g0 in range(0, BLK, SCAN_GROUPS):
        for gi in range(SCAN_GROUPS):
            bg = g0 + gi
            win = win_ref[bg]
            src_l = ((blk_l - bg) & (BLK - 1)).astype(F32).astype(BF16)
            src_c = ((blk_c - bg) & (BLK - 1)).astype(F32).astype(BF16)
            u = _gather_group(ul_ref, src_l)
            u_ref[gi] = u
            sl = jnp.dot(u, win, preferred_element_type=F32)
            sc = jnp.dot(_gather_group(uc_ref, src_c), win, preferred_element_type=F32)
            for b in range(B):
                sre_ref[gi, pl.ds(b, N_CHUNK, stride=B), :] = sl[N_CHUNK * b:N_CHUNK * (b + 1), 0:128]
                sim_ref[gi, pl.ds(b, N_CHUNK, stride=B), :] = sl[N_CHUNK * b:N_CHUNK * (b + 1), 128:256]
                cre_ref[gi, pl.ds(b, N_CCHUNK, stride=B), :] = sc[N_CCHUNK * b:N_CCHUNK * (b + 1), 0:128]
                cim_ref[gi, pl.ds(b, N_CCHUNK, stride=B), :] = sc[N_CCHUNK * b:N_CCHUNK * (b + 1), 128:256]
        lams = [(jnp.broadcast_to(l16_ref[g0 + gi, 0:1, :], (8, 128)),
                 jnp.broadcast_to(l16_ref[g0 + gi, 1:2, :], (8, 128))) for gi in range(SCAN_GROUPS)]
        zero = tuple(jnp.zeros((8, 128), F32) for _ in range(2 * SCAN_GROUPS))
        carry = _scan_tiles(cre_ref, cim_ref, None, ncx // 8, zero, lams)
        _scan_tiles(sre_ref, sim_ref, (fre_ref, fim_ref, bre_ref, bim_ref), nl // 8, carry, lams)
        for gi in range(SCAN_GROUPS):
            bg = g0 + gi
            y = jnp.dot(u_ref[gi], mix_ref[bg], preferred_element_type=F32)
            hs = []
            for b in range(B):
                rows = pl.ds(b, N_CHUNK, stride=B)
                hs.append(jnp.concatenate([jnp.where(fwd, fre_ref[gi, rows, :], bre_ref[gi, rows, :]),
                                           jnp.where(fwd, fim_ref[gi, rows, :], bim_ref[gi, rows, :])], axis=1))
            hcat = jnp.concatenate(hs, axis=0).astype(BF16)
            y = y + jnp.dot(hcat, wout_ref[bg], preferred_element_type=F32)
            y_ref[bg] = y.astype(BF16)

    blk = blk_l.astype(F32).astype(BF16)
    for s in range(S5_T):
        h, r = s // BLK, s % BLK
        acc = None
        for j in range(BLK):
            piece = y_ref[(j - r) % BLK, :, 128 * h:128 * (h + 1)]
            acc = piece if acc is None else jnp.where(blk == j, piece, acc)
        o_ref[s] = acc


def _s5core(ul, uc, win, wout, mix, l16):
    TH = S5_T * S5_H
    nl = N_CHUNK * B
    ncx = N_CCHUNK * B
    g4 = lambda r, c: pl.BlockSpec((BLK, r, c), lambda q: (q, 0, 0))
    col = lambda n: pl.BlockSpec((S5_T, n, 128), lambda q: (0, 0, q))
    f32s = lambda n: pltpu.VMEM((SCAN_GROUPS, n, 128), F32)
    return pl.pallas_call(
        _s5core_kernel,
        out_shape=jax.ShapeDtypeStruct((S5_T, nl, S5_W), BF16),
        grid=(S5_G // BLK,),
        in_specs=[col(nl), col(ncx), g4(TH, TH), g4(TH, TH), g4(TH, TH), g4(8, 128)],
        out_specs=col(nl),
        scratch_shapes=[pltpu.VMEM((SCAN_GROUPS, nl, TH), BF16),
                        f32s(nl), f32s(nl), f32s(ncx), f32s(ncx), f32s(nl), f32s(nl), f32s(nl), f32s(nl),
                        pltpu.VMEM((BLK, nl, TH), BF16)],
        compiler_params=_params("arbitrary"),
        name="s5core",
    )(ul, uc, win, wout, mix, l16)


def _cast_on_first_step(src_ref, dst_ref, scale=None):
    @pl.when((pl.program_id(0) == 0) & (pl.program_id(1) == 0))
    def _():
        w = src_ref[...]
        dst_ref[...] = (w if scale is None else scale * w).astype(BF16)


def _const_spec(shape, index):
    return pl.BlockSpec(shape, lambda r, h: index, pipeline_mode=pl.Buffered(1))


def _s5tail_kernel(slat_ref, sza_ref, gluw32_ref, glub_ref, wtop32_ref, y_ref, gluw_ref, wtop_ref):
    _cast_on_first_step(gluw32_ref, gluw_ref)
    _cast_on_first_step(wtop32_ref, wtop_ref)
    unrot = (BLK - pl.program_id(0)) & (BLK - 1)
    for b in range(B):
        rows = slice(N_CHUNK * b, N_CHUNK * (b + 1))
        g = _gelu(_rot_blocks(slat_ref[rows, :].astype(F32), unrot))
        gate = _sigmoid(jnp.dot(g.astype(BF16), gluw_ref[...], preferred_element_type=F32) + glub_ref[...])
        a = (g * gate * sza_ref[rows, :].astype(F32)).astype(BF16)
        y_ref[rows, :] = jnp.dot(a, wtop_ref[...], preferred_element_type=F32).astype(BF16)


def _s5tail(slat, sza, glu_w, glu_b, w_out):
    slab = lambda r, h: r + BLK * h
    sspec = lambda w: pl.BlockSpec((None, N_CHUNK * B, w), lambda r, h: (slab(r, h), 0, 0))
    full = lambda *s: pl.BlockSpec(s, lambda r, h: (0,) * len(s))
    return pl.pallas_call(
        _s5tail_kernel,
        out_shape=jax.ShapeDtypeStruct((S5_T, N_CHUNK * B, D), BF16),
        grid=(BLK, S5_T // BLK),
        in_specs=[sspec(512), sspec(512), _const_spec((S5_W, S5_W), (0, 0)), full(1, 512),
                  _const_spec((S5_W, D), (0, 0))],
        out_specs=sspec(D),
        scratch_shapes=[pltpu.VMEM((S5_W, S5_W), BF16), pltpu.VMEM((S5_W, D), BF16)],
        compiler_params=_params("arbitrary", "arbitrary"),
        name="s5tail",
    )(slat, sza, glu_w, glu_b, w_out)


def _tail0_kernel(x_ref, ys5_ref, guz_ref, vln_ref, mod_ref, sguw_ref, sgub_ref, wbot32_ref, ng_ref, nb_ref, o_ref,
                  wbot_ref):
    _cast_on_first_step(wbot32_ref, wbot_ref)
    tm = x_ref.shape[0]
    lane = lax.broadcasted_iota(jnp.int32, (SGU_CHUNK, 128), 1)
    lo = lane < SGU_HD
    zero = jnp.zeros((SGU_CHUNK, 128), BF16)
    w_pair = [jnp.concatenate([sguw_ref[2 * pi].astype(BF16), sguw_ref[2 * pi + 1].astype(BF16)], axis=1)
              for pi in range(SGU_HEADS // 2)]
    chunks = []
    for ci in range(tm // SGU_CHUNK):
        v = vln_ref[ci * SGU_CHUNK:(ci + 1) * SGU_CHUNK, :]
        cols = []
        for pi in range(SGU_HEADS // 2):
            vp = v[:, 128 * pi:128 * (pi + 1)]
            bm = jnp.concatenate([jnp.where(lo, vp, zero), jnp.where(lo, zero, vp)], axis=0)
            cols.append(jnp.dot(w_pair[pi], bm, preferred_element_type=F32))
        chunks.append(jnp.concatenate(cols, axis=1) + sgub_ref[...])
    s = jnp.concatenate(chunks, axis=0)
    bsg = (guz_ref[...].astype(F32) * s).astype(BF16)
    perm = _chunk_transpose_perm()
    ys5 = jnp.concatenate(
        [jnp.dot(perm, ys5_ref[:, S5_T * j:S5_T * (j + 1), :].reshape(PERM_ROWS, D), preferred_element_type=F32)
         for j in range(tm // PERM_ROWS)], axis=0)
    y = ys5 + jnp.dot(bsg, wbot_ref[...], preferred_element_type=F32)
    gmod = mod_ref[:, 2 * D:3 * D]
    o_ref[...] = _layer_norm(DN_ALPHA * x_ref[...] + gmod * y, ng_ref[0:1, :], nb_ref[0:1, :])


def _tail0(x, ys5, guz, vln, mod, sguw, sgub, w_bot, ng, nb, tm=TOKEN_TILE):
    nct = tm // S5_T
    t512 = pl.BlockSpec((None, tm, 512), lambda b, i: (b, i, 0))
    tD = pl.BlockSpec((None, tm, D), lambda b, i: (b, i, 0))
    full = lambda *s: pl.BlockSpec(s, lambda b, i: (0,) * len(s))
    return pl.pallas_call(
        _tail0_kernel,
        out_shape=jax.ShapeDtypeStruct((B, L, D), F32),
        grid=(B, L // tm),
        in_specs=[tD, pl.BlockSpec((S5_T, nct, D), lambda b, i: (0, b * (N_CHUNK // nct) + i, 0)), t512, t512,
                  _mod_spec(0),
                  full(SGU_HEADS, SGU_CHUNK, SGU_CHUNK), full(SGU_CHUNK, 512),
                  _const_spec((SGU_W, D), (1, 0)), full(DEPTH, D), full(DEPTH, D)],
        out_specs=tD,
        scratch_shapes=[pltpu.VMEM((SGU_W, D), BF16)],
        compiler_params=_params("arbitrary", "arbitrary"),
        name="tail0",
    )(x, ys5, guz, vln, mod, sguw, sgub, w_bot, ng, nb)


CONV_C = D // 2
TILE_ROWS = TOKEN_TILE // GRID_W


def _grid_transpose_in(v, o_ref):
    perm = _chunk_transpose_perm()
    for q in range(GRID_W // S5_T):
        seg = jnp.concatenate([v[GRID_W * r + S5_T * q:GRID_W * r + S5_T * (q + 1), :] for r in range(TILE_ROWS)],
                              axis=0)
        t = jnp.dot(perm, seg, preferred_element_type=F32).astype(BF16)
        o_ref[S5_T * q:S5_T * (q + 1), :, :] = t.reshape(S5_T, TILE_ROWS, v.shape[1])


def _inproj1_kernel(x_ref, mod_ref, w_ref, hgr_ref, hgc_ref, h_ref):
    shift = mod_ref[:, 0:D]
    scale = mod_ref[:, D:2 * D]
    h = (x_ref[...] * (1.0 + scale) + shift).astype(BF16)
    h_ref[...] = h
    dot = lambda lo: jnp.dot(h, w_ref[:, lo:lo + CONV_C].astype(BF16), preferred_element_type=F32)
    hgr_ref[...] = (dot(0) * _sigmoid(dot(D))).astype(BF16)
    _grid_transpose_in((dot(CONV_C) * _sigmoid(dot(D + CONV_C))).astype(BF16), hgc_ref)


def _inproj1(x, mod, w_in_f32, tm=TOKEN_TILE):
    tile = lambda w: pl.BlockSpec((None, tm, w), lambda b, i: (b, i, 0))
    return pl.pallas_call(
        _inproj1_kernel,
        out_shape=(jax.ShapeDtypeStruct((B, L, CONV_C), BF16),
                   jax.ShapeDtypeStruct((B, GRID_W, GRID_W, CONV_C), BF16),
                   jax.ShapeDtypeStruct((B, L, D), BF16)),
        grid=(B, L // tm),
        in_specs=[tile(D),
                  _mod_spec(1),
                  pl.BlockSpec((D, 2 * D), lambda b, i: (0, 0), pipeline_mode=pl.Buffered(1))],
        out_specs=(tile(CONV_C), pl.BlockSpec((None, GRID_W, TILE_ROWS, CONV_C), lambda b, i: (b, 0, i, 0)),
                   tile(D)),
        compiler_params=_params("arbitrary", "arbitrary"),
        name="inproj1",
    )(x, mod, w_in_f32)


DFT_N = 2 * GRID_W
TAPS_PAD = CONV_K + 1


def _dft_constants():
    th = 2.0 * math.pi / DFT_N
    f = np.arange(GRID_W, dtype=np.float64)[:, None]
    p = np.arange(GRID_W, dtype=np.float64)[None, :]
    cosm = np.cos(th * f * p)
    sinm = np.sin(th * f * p)
    alt = np.where(np.arange(GRID_W) % 2 == 0, 1.0, -1.0)
    fwd = np.concatenate([cosm, alt[None, :], sinm[1:]], axis=0)
    cf = np.where(np.arange(GRID_W) == 0, 1.0, 2.0) / DFT_N
    inv = np.concatenate([cosm.T * cf[None, :], (alt / DFT_N)[:, None], sinm.T[:, 1:] * (2.0 / DFT_N)], axis=1)
    sft = (CONV_HALF - np.arange(TAPS_PAD, dtype=np.float64))[None, :]
    live = (np.arange(TAPS_PAD) < CONV_K).astype(np.float64)[None, :]
    f64 = np.where(f == 0, float(GRID_W), f)
    f32 = lambda a: jnp.asarray(a.astype(np.float32))
    return (f32(fwd).astype(BF16), f32(inv).astype(BF16),
            f32(np.cos(th * f * sft) * live), f32(np.sin(th * f * sft) * live), f32(np.cos(th * f64 * sft) * live))


def _fconv_kernel(h_ref, w_ref, b_ref, fwd_ref, inv_ref, c1_ref, s3_ref, c4_ref, o_ref, taps_ref):
    hp = lax.Precision.HIGHEST
    taps_ref[...] = jnp.zeros(taps_ref.shape, F32)
    taps_ref[0:CONV_K, :] = w_ref[...]
    taps = taps_ref[...]
    g_re = jnp.dot(c1_ref[...], taps, preferred_element_type=F32, precision=hp)
    g_im = jnp.dot(s3_ref[...], taps, preferred_element_type=F32, precision=hp)
    g_r2 = jnp.dot(c4_ref[...], taps, preferred_element_type=F32, precision=hp)
    fwd = fwd_ref[...]
    inv = inv_ref[...]
    bias = b_ref[...]
    n_runs = h_ref.shape[0] // GRID_W
    rows = lambda r: slice(GRID_W * r, GRID_W * (r + 1))
    forward = lambda r: jnp.dot(fwd, h_ref[rows(r), :], preferred_element_type=F32)
    ahead = 2
    specs = [forward(r) for r in range(ahead)]
    for r in range(n_runs):
        if r + ahead < n_runs:
            specs.append(forward(r + ahead))
        spec = specs[r]
        a, bm = spec[0:GRID_W], spec[GRID_W:DFT_N]
        prod = jnp.concatenate([a * g_re - bm * g_im, a * g_im + bm * g_r2], axis=0).astype(BF16)
        o_ref[rows(r), :] = (jnp.dot(inv, prod, preferred_element_type=F32) + bias).astype(BF16)


def _fconv(h, taps, bias, half, consts, tm=L):
    fwd, inv, c1, s3, c4 = consts
    c = h.shape[-1]
    tile = pl.BlockSpec((None, tm, c), lambda b, i: (b, i, 0))
    full = lambda *s: pl.BlockSpec(s, lambda b, i: (0,) * len(s))
    cols = lambda r: pl.BlockSpec((None, r, c), lambda b, i: (0, 0, half))
    return pl.pallas_call(
        _fconv_kernel,
        out_shape=jax.ShapeDtypeStruct(h.shape, BF16),
        grid=(B, L // tm),
        in_specs=[tile, cols(CONV_K), cols(1), full(DFT_N, GRID_W), full(GRID_W, DFT_N),
                  full(GRID_W, TAPS_PAD), full(GRID_W, TAPS_PAD), full(GRID_W, TAPS_PAD)],
        out_specs=tile,
        scratch_shapes=[pltpu.VMEM((TAPS_PAD, c), F32)],
        compiler_params=_params("arbitrary", "arbitrary"),
        name="fconv",
    )(h, taps, bias, fwd, inv, c1, s3, c4)


ROW_BLOCK = 32


def _row_blocks(n_rows):
    return [slice(ROW_BLOCK * k, ROW_BLOCK * (k + 1)) for k in range(n_rows // ROW_BLOCK)]


def _tail1_kernel(x_ref, hcr_ref, hcc_ref, h1_ref, wz32_ref, mod_ref, lng_ref, lnb_ref, wout_ref, ng_ref, nb_ref,
                  o_ref, col_ref, z_ref, m_ref, y_ref, wz_ref):
    _cast_on_first_step(wz32_ref, wz_ref, scale=0.5)
    tm = x_ref.shape[0]
    perm = _chunk_transpose_perm()
    for q in range(GRID_W // S5_T):
        blk = hcc_ref[S5_T * q:S5_T * (q + 1), :, :].reshape(PERM_ROWS, CONV_C)
        t = jnp.dot(perm, blk, preferred_element_type=F32)
        for r in range(TILE_ROWS):
            col_ref[GRID_W * r + S5_T * q:GRID_W * r + S5_T * (q + 1), :] = t[S5_T * r:S5_T * (r + 1), :]
    z_ref[...] = jnp.dot(h1_ref[...], wz_ref[...], preferred_element_type=F32)
    lng, lnb = 0.5 * lng_ref[...], 0.5 * lnb_ref[...]
    for rows in _row_blocks(tm):
        hc = jnp.concatenate([hcr_ref[rows, :].astype(F32), col_ref[rows, :]], axis=1)
        m_ref[rows, :] = (_silu_of_half(_layer_norm(hc, lng, lnb)) * _silu_of_half(z_ref[rows, :])).astype(BF16)
    y_ref[...] = jnp.dot(m_ref[...], wout_ref[...], preferred_element_type=F32)
    gmod = mod_ref[:, 2 * D:3 * D]
    ng, nb = ng_ref[1:2, :], nb_ref[1:2, :]
    for rows in _row_blocks(tm):
        o_ref[rows, :] = _layer_norm(DN_ALPHA * x_ref[rows, :] + gmod * y_ref[rows, :], ng, nb)


def _tail1(x, hc_row, hc_col, h1, w_in_f32, mod, ln_g, ln_b, w_out, ng, nb, tm=TOKEN_TILE):
    tile = lambda w: pl.BlockSpec((None, tm, w), lambda b, i: (b, i, 0))
    full = lambda *s: pl.BlockSpec(s, lambda b, i: (0,) * len(s))
    return pl.pallas_call(
        _tail1_kernel,
        out_shape=jax.ShapeDtypeStruct((B, L, D), F32),
        grid=(B, L // tm),
        in_specs=[tile(D), tile(CONV_C),
                  pl.BlockSpec((None, GRID_W, TILE_ROWS, CONV_C), lambda b, i: (b, 0, i, 0)),
                  tile(D), _const_spec((D, D), (0, 2)), _mod_spec(1),
                  full(1, D), full(1, D), full(D, D), full(DEPTH, D), full(DEPTH, D)],
        out_specs=tile(D),
        scratch_shapes=[pltpu.VMEM((tm, CONV_C), F32), pltpu.VMEM((tm, D), F32), pltpu.VMEM((tm, D), BF16),
                        pltpu.VMEM((tm, D), F32), pltpu.VMEM((D, D), BF16)],
        compiler_params=_params("arbitrary", "arbitrary"),
        name="tail1",
    )(x, hc_row, hc_col, h1, w_in_f32, mod, ln_g, ln_b, w_out, ng, nb)


def kernel(x, c, ctx, c_ctx, mod_w, mod_b, norm_g, norm_b, ev_w_in, ev_w_out, s5_lam_re, s5_lam_im, s5_log_dt, s5_b_re, s5_b_im, s5_c_re, s5_c_im, s5_d, glu_w, glu_b, sgu_ln_g, sgu_ln_b, sgu_w, sgu_b, od_w_in, od_w_out, dw_w, dw_b, conv_ln_g, conv_ln_b):
    TH = S5_T * S5_H
    row = lambda v: v.reshape(1, -1)

    mods = _adaln(c, c_ctx, mod_w, mod_b)

    ldt = jnp.broadcast_to(s5_log_dt[0][:, :, None], (2, S5_G, S5_P))
    fb = lambda s: jnp.concatenate([s[:, 0], s[:, 1]], axis=-1)
    lam3 = fb(jnp.stack([s5_lam_re[0], s5_lam_im[0], ldt]))
    bt = fb(jnp.swapaxes(jnp.stack([s5_b_re[0], s5_b_im[0]]), -1, -2))
    cn = fb(jnp.stack([s5_c_re[0], s5_c_im[0]]))
    win, wout, mix, l16 = _s5_weights(lam3, s5_d, bt, cn)

    guz, vln, hs = _inproj0n(x, mods, ev_w_in[0], row(sgu_ln_g[0]), row(sgu_ln_b[0]))
    ua, sza, ua_c = _inproj0a(hs, _ctx_slabs(ctx, mods), ev_w_in[0])
    s_lat = _s5core(ua, ua_c, win, wout, mix, l16)
    y_s5 = _s5tail(s_lat, sza, glu_w[0], row(glu_b[0]), ev_w_out[0])
    sgub = jnp.repeat(sgu_b[0].T, SGU_HD, axis=1)
    x1 = _tail0(x, y_s5, guz, vln, mods, sgu_w[0], sgub, ev_w_out[0], norm_g, norm_b)

    hg_row, hg_col, h1 = _inproj1(x1, mods, od_w_in[0])
    consts = _dft_constants()
    bias = dw_b.reshape(1, 1, 2 * CONV_C)
    hc_row = _fconv(hg_row, dw_w, bias, 0, consts)
    hc_col = _fconv(hg_col.reshape(B, L, CONV_C), dw_w, bias, 1, consts)
    return _tail1(x1, hc_row, hc_col.reshape(B, GRID_W, GRID_W, CONV_C), h1, od_w_in[0], mods,
                  row(conv_ln_g[0]), row(conv_ln_b[0]), od_w_out[0].astype(BF16), norm_g, norm_b)
```

```python
import functools
import math

import jax
import jax.numpy as jnp
import numpy as np
from jax import lax
from jax.experimental import pallas as pl
from jax.experimental.pallas import tpu as pltpu

D = 1024
B = 4
L = 4096
CTX = 256
GRID_W = 64
S5_W = 512
S5_G = 32
S5_H = 16
H_SHIFT = 4
BLK = 128 // S5_H
S5_P = 64
S5_T = 16
SGU_W = 512
SGU_HEADS = 8
SGU_HD = 64
SGU_CHUNK = 128
CONV_K = 31
CONV_HALF = CONV_K // 2
EVEN_IN = 2560
SGU_COL0 = 2 * S5_W
ODD_IN = 3072
DEPTH = 2
DN_ALPHA = (2 * DEPTH) ** 0.25
LN_EPS = 1e-5
N_CHUNK = L // S5_T
N_CCHUNK = CTX // S5_T
VMEM_LIMIT_V7X = 56 * 1024 * 1024
TOKEN_TILE = 1024

F32 = jnp.float32
BF16 = jnp.bfloat16


GELU_C = math.sqrt(2.0 / math.pi)


def _gelu(x):
    hx = 0.5 * x
    return hx * jnp.tanh(x * ((x * x) * (0.044715 * GELU_C) + GELU_C)) + hx


def _sigmoid(x):
    return 0.5 * jnp.tanh(0.5 * x) + 0.5


def _silu_of_half(hx):
    return hx * jnp.tanh(hx) + hx


def _silu(x):
    return _silu_of_half(0.5 * x)


def _layer_norm(x, g, b):
    mu = jnp.mean(x, axis=-1, keepdims=True)
    xc = x - mu
    var = jnp.mean(xc * xc, axis=-1, keepdims=True)
    return xc * lax.rsqrt(var + LN_EPS) * g + b


def _params(*sem):
    return pltpu.CompilerParams(dimension_semantics=sem, vmem_limit_bytes=VMEM_LIMIT_V7X)


ADALN_ROWS = 8
ADALN_TK = 256


def _adaln_kernel(c_ref, cctx_ref, w0_ref, w1_ref, b_ref, o_ref, cond_ref, acc_ref):
    layer, k = pl.program_id(0), pl.program_id(1)
    w_refs = (w0_ref, w1_ref)
    cond_ref[...] = jnp.zeros(cond_ref.shape, F32)
    cond_ref[0:B, :] = c_ref[...]
    cond_ref[B:B + 1, :] = cctx_ref[...]

    def split(v):
        hi = v.astype(BF16)
        return hi, (v - hi.astype(F32)).astype(BF16)

    a_hi, a_lo = split(_silu(cond_ref[:, pl.ds(pl.multiple_of(k * ADALN_TK, ADALN_TK), ADALN_TK)]))
    dot = functools.partial(jnp.dot, preferred_element_type=F32)
    a_both = jnp.concatenate([a_hi.astype(F32), a_lo.astype(F32)], axis=0).astype(BF16)
    halves = []
    for w_ref in w_refs:
        w_hi, w_lo = split(w_ref[...])
        both = dot(a_both, w_hi)
        halves.append(both[:ADALN_ROWS] + both[ADALN_ROWS:] + dot(a_hi, w_lo))
    part = jnp.concatenate(halves, axis=1)

    @pl.when(k == 0)
    def _():
        acc_ref[...] = part + b_ref[pl.ds(layer, 1), :]

    @pl.when(k > 0)
    def _():
        acc_ref[...] += part

    @pl.when(k == pl.num_programs(1) - 1)
    def _():
        for r in range(ADALN_ROWS):
            o_ref[r] = acc_ref[r:r + 1, :]


def _adaln(c, c_ctx, mod_w, mod_b):
    full = lambda *s: pl.BlockSpec(s, lambda l, k: (0,) * len(s))
    w_half = lambda h: pl.BlockSpec((None, ADALN_TK, 3 * D // 2), lambda l, k: (l, k, h))
    return pl.pallas_call(
        _adaln_kernel,
        out_shape=jax.ShapeDtypeStruct((DEPTH, ADALN_ROWS, 1, 3 * D), F32),
        grid=(DEPTH, D // ADALN_TK),
        in_specs=[full(B, D), full(1, D), w_half(0), w_half(1), full(DEPTH, 3 * D)],
        out_specs=pl.BlockSpec((None, ADALN_ROWS, 1, 3 * D), lambda l, k: (l, 0, 0, 0)),
        scratch_shapes=[pltpu.VMEM((ADALN_ROWS, D), F32), pltpu.VMEM((ADALN_ROWS, 3 * D), F32)],
        compiler_params=_params("arbitrary", "arbitrary"),
        name="adaln",
    )(c, c_ctx.reshape(1, D), mod_w, mod_w, mod_b)


def _mod_spec(layer, cond=None):
    if cond is None:
        return pl.BlockSpec((None, None, 1, 3 * D), lambda b, i: (layer, b, 0, 0))
    return pl.BlockSpec((None, None, 1, 3 * D), lambda b: (layer, cond, 0, 0))


S5W_GROUPS = BLK


def _cpow(base_pows, j):
    re = None
    im = None
    for k, (pr, pi) in enumerate(base_pows):
        bit = ((j >> k) & 1) == 1
        mr = jnp.where(bit, pr, 1.0)
        mi = jnp.where(bit, pi, 0.0)
        if re is None:
            re, im = mr, mi
        else:
            re, im = re * mr - im * mi, re * mi + im * mr
    return re, im


def _squarings(pr, pi, n):
    out = [(pr, pi)]
    for _ in range(n - 1):
        pr, pi = pr * pr - pi * pi, 2.0 * pr * pi
        out.append((pr, pi))
    return out


def _shift_lanes(x, n):
    lane = lax.broadcasted_iota(jnp.int32, (S5_H, 128), 1)
    lo, hi = x[:, :128], x[:, 128:]
    if n == 0:
        return x
    if n < 128:
        rlo = pltpu.roll(lo, n, axis=1)
        rhi = pltpu.roll(hi, n, axis=1)
        return jnp.concatenate([jnp.where(lane >= n, rlo, 0.0), jnp.where(lane >= n, rhi, rlo)], axis=1)
    m = n - 128
    rlo = lo if m == 0 else pltpu.roll(lo, m, axis=1)
    return jnp.concatenate([jnp.zeros_like(lo), jnp.where(lane >= m, rlo, 0.0)], axis=1)


def _unshift_lanes(x, n):
    lane = lax.broadcasted_iota(jnp.int32, (S5_H, 128), 1)
    lo, hi = x[:, :128], x[:, 128:]
    if n == 0:
        return x
    if n < 128:
        rlo = pltpu.roll(lo, 128 - n, axis=1)
        rhi = pltpu.roll(hi, 128 - n, axis=1)
        keep = lane < 128 - n
        return jnp.concatenate([jnp.where(keep, rlo, rhi), jnp.where(keep, rhi, 0.0)], axis=1)
    m = n - 128
    rhi = hi if m == 0 else pltpu.roll(hi, 128 - m, axis=1)
    return jnp.concatenate([jnp.where(lane < 128 - m, rhi, 0.0), jnp.zeros_like(lo)], axis=1)


def _s5w_group(gi, bg, disc, d_t, bt_ref, cn_ref, win_ref, wout_ref, mix_ref, l16_ref):
    TH = S5_T * S5_H

    def chunk_pos(idx):
        return (((idx >> H_SHIFT) - bg) & (BLK - 1)) + ((idx >> 7) << 3)

    lr, li, cr, ci = [v[gi:gi + 1] for v in disc]
    pows_row = _squarings(lr, li, 5)
    l16_ref[gi, 0:1, :] = pows_row[4][0]
    l16_ref[gi, 1:2, :] = pows_row[4][1]
    l16_ref[gi, 2:8, :] = jnp.zeros((6, 128), F32)
    btr = bt_ref[0, gi]
    bti = bt_ref[1, gi]
    bbr = cr * btr - ci * bti
    bbi = cr * bti + ci * btr
    blk16 = lax.broadcasted_iota(jnp.int32, (S5_T, 128), 0)
    is_f16 = lax.broadcasted_iota(jnp.int32, (S5_T, 128), 1) < S5_P
    pos16 = chunk_pos(blk16 << H_SHIFT)
    pr16, pi16 = _cpow(pows_row[:4], jnp.where(is_f16, S5_T - 1 - pos16, pos16))
    rep_rows = lambda v: jnp.broadcast_to(v[:, None, :], (S5_T, S5_H, 128)).reshape(TH, 128)
    pr, pi = rep_rows(pr16), rep_rows(pi16)
    tbr = jnp.broadcast_to(bbr[None], (S5_T, S5_H, 128)).reshape(TH, 128)
    tbi = jnp.broadcast_to(bbi[None], (S5_T, S5_H, 128)).reshape(TH, 128)
    win_ref[gi, :, 0:128] = (pr * tbr - pi * tbi).astype(BF16)
    win_ref[gi, :, 128:256] = (pr * tbi + pi * tbr).astype(BF16)

    hp = lax.Precision.HIGHEST
    dot = functools.partial(jnp.dot, preferred_element_type=F32, precision=hp)
    def col256(r):
        col = jnp.broadcast_to(r, (2 * S5_P, 2 * S5_P)).T
        return jnp.concatenate([col, col], axis=1)

    def tiled_t(cn):
        t8 = jnp.broadcast_to(cn[None], (BLK, S5_H, 2 * S5_P)).reshape(2 * S5_P, 2 * S5_P).T
        return jnp.concatenate([t8, t8], axis=1)

    cpows = _squarings(col256(lr), col256(li), 4)
    row = lax.broadcasted_iota(jnp.int32, (2 * S5_P, TH), 0)
    lane_w = lax.broadcasted_iota(jnp.int32, (2 * S5_P, TH), 1)
    t_idx = chunk_pos(lane_w)
    j_idx = lane_w >> H_SHIFT
    is_f = row < S5_P
    ctr = tiled_t(cn_ref[0, gi])
    cti = tiled_t(cn_ref[1, gi])
    er, ei = _cpow(cpows, jnp.where(is_f, t_idx, S5_T - 1 - t_idx))
    er, ei = er * cpows[0][0] - ei * cpows[0][1], er * cpows[0][1] + ei * cpows[0][0]
    wr = ctr * er - cti * ei
    wi = ctr * ei + cti * er
    wout_ref[gi, 0:128, :] = wr.astype(BF16)
    wout_ref[gi, 128:256, :] = (-wi).astype(BF16)
    kr, ki = _cpow(cpows, jnp.where(is_f, j_idx, S5_T - 1 - j_idx))
    ekr = ctr * kr - cti * ki
    eki = ctr * ki + cti * kr
    lane16 = lax.broadcasted_iota(jnp.int32, (S5_H, 128), 1)
    mf = lane16 < S5_P
    kkf = dot(jnp.where(mf, bbr, 0.0), ekr) - dot(jnp.where(mf, bbi, 0.0), eki)
    kkb = dot(jnp.where(mf, 0.0, bbr), ekr) - dot(jnp.where(mf, 0.0, bbi), eki)
    d_rows = d_t[S5_H * gi:S5_H * (gi + 1), :]
    dl = jnp.concatenate([d_rows, d_rows], axis=1)
    r16 = lax.broadcasted_iota(jnp.int32, (S5_H, TH), 0)
    l256 = lax.broadcasted_iota(jnp.int32, (S5_H, TH), 1)
    rot = bg * S5_H
    for s in range(S5_T):
        blk = _shift_lanes(kkf, S5_H * s) + _unshift_lanes(kkb, S5_H * (S5_T - 1 - s))
        blk = blk + jnp.where(l256 == r16 + S5_H * s, dl, 0.0)
        if rot:
            blk = jnp.concatenate([pltpu.roll(blk[:, :128], rot, axis=1), pltpu.roll(blk[:, 128:], rot, axis=1)], axis=1)
        rho = ((s + bg) & (BLK - 1)) + (s & BLK)
        mix_ref[gi, S5_H * rho:S5_H * (rho + 1), :] = blk.astype(BF16)


def _s5w_kernel(lam_ref, d_ref, *refs):
    d_t = jnp.broadcast_to(d_ref[...], (S5W_GROUPS * S5_H, S5W_GROUPS * S5_H)).T
    lr = lam_ref[0]
    li = lam_ref[1]
    dt = jnp.exp(lam_ref[2])
    mag = jnp.exp(lr * dt)
    br = mag * jnp.cos(li * dt)
    bi = mag * jnp.sin(li * dt)
    inv = 1.0 / (lr * lr + li * li)
    nr = br - 1.0
    disc = (br, bi, (nr * lr + bi * li) * inv, (bi * lr - nr * li) * inv)
    for gi in range(S5W_GROUPS):
        _s5w_group(gi, gi % BLK, disc, d_t, *refs)


def _s5_weights(lam3, d, bt, cn):
    TH = S5_T * S5_H
    g3 = lambda r, c: pl.BlockSpec((S5W_GROUPS, r, c), lambda g: (g, 0, 0))
    ri = pl.BlockSpec((2, S5W_GROUPS, S5_H, 2 * S5_P), lambda g: (0, g, 0, 0))
    wshape = jax.ShapeDtypeStruct((S5_G, TH, TH), BF16)
    return pl.pallas_call(
        _s5w_kernel,
        out_shape=(wshape, wshape, wshape, jax.ShapeDtypeStruct((S5_G, 8, 128), F32)),
        grid=(S5_G // S5W_GROUPS,),
        in_specs=[pl.BlockSpec((3, S5W_GROUPS, 2 * S5_P), lambda g: (0, g, 0)),
                  pl.BlockSpec((1, S5W_GROUPS * S5_H), lambda g: (0, g)), ri, ri],
        out_specs=(g3(TH, TH), g3(TH, TH), g3(TH, TH), g3(8, 128)),
        compiler_params=_params("arbitrary"),
        name="s5_weights",
    )(lam3, d, bt, cn)


def _rot_blocks(v, r):
    cols = [pltpu.roll(v[:, 128 * q:128 * (q + 1)], S5_H * r, axis=1) for q in range(v.shape[1] // 128)]
    return jnp.concatenate(cols, axis=1)


PERM_ROWS = S5_T * S5_T
assert CTX == PERM_ROWS


def _chunk_transpose_perm():
    ri = lax.broadcasted_iota(jnp.int32, (PERM_ROWS, PERM_ROWS), 0)
    ci = lax.broadcasted_iota(jnp.int32, (PERM_ROWS, PERM_ROWS), 1)
    hit = ((ri >> H_SHIFT) == (ci & (S5_T - 1))) & ((ri & (S5_T - 1)) == (ci >> H_SHIFT))
    return jnp.where(hit, 1.0, 0.0).astype(BF16)


def _inproj0n_kernel(x_ref, mod_ref, w_ref, lng_ref, lnb_ref, guz_ref, vln_ref, hs_ref):
    shift = mod_ref[:, 0:D]
    scale = mod_ref[:, D:2 * D]
    hb = (x_ref[...] * (1.0 + scale) + shift).astype(BF16)
    perm = _chunk_transpose_perm()
    for j in range(hb.shape[0] // PERM_ROWS):
        blk = jnp.dot(perm, hb[PERM_ROWS * j:PERM_ROWS * (j + 1), :], preferred_element_type=F32).astype(BF16)
        for s in range(S5_T):
            hs_ref[s, S5_T * j:S5_T * (j + 1), :] = blk[S5_T * s:S5_T * (s + 1), :]
    dot = lambda lo: jnp.dot(hb, w_ref[:, SGU_COL0 + lo:SGU_COL0 + lo + 512].astype(BF16),
                             preferred_element_type=F32)
    guz_ref[...] = (_gelu(dot(0)) * _silu(dot(1024))).astype(BF16)
    vln_ref[...] = _layer_norm(_gelu(dot(512)), lng_ref[...], lnb_ref[...]).astype(BF16)


def _inproj0n(x, mod, w_in_f32, ln_g, ln_b, tm=TOKEN_TILE):
    nct = tm // S5_T
    o = jax.ShapeDtypeStruct((B, L, 512), BF16)
    ospec = pl.BlockSpec((None, tm, 512), lambda b, i: (b, i, 0))
    full = lambda *s: pl.BlockSpec(s, lambda b, i: (0,) * len(s))
    return pl.pallas_call(
        _inproj0n_kernel,
        out_shape=(o, o, jax.ShapeDtypeStruct((S5_T, B * N_CHUNK, D), BF16)),
        grid=(B, L // tm),
        in_specs=[pl.BlockSpec((None, tm, D), lambda b, i: (b, i, 0)),
                  _mod_spec(0),
                  pl.BlockSpec((D, EVEN_IN), lambda b, i: (0, 0), pipeline_mode=pl.Buffered(1)),
                  full(1, 512), full(1, 512)],
        out_specs=(ospec, ospec,
                   pl.BlockSpec((S5_T, nct, D), lambda b, i: (0, b * (N_CHUNK // nct) + i, 0))),
        compiler_params=_params("arbitrary", "arbitrary"),
        name="inproj0n",
    )(x, mod, w_in_f32, ln_g, ln_b)


def _ctx_slabs_kernel(x_ref, mod_ref, hs_ref):
    hb = (x_ref[...] * (1.0 + mod_ref[:, D:2 * D]) + mod_ref[:, 0:D]).astype(BF16)
    blk = jnp.dot(_chunk_transpose_perm(), hb, preferred_element_type=F32).astype(BF16)
    for s in range(S5_T):
        hs_ref[s] = blk[N_CCHUNK * s:N_CCHUNK * (s + 1), :]


def _ctx_slabs(ctx, mod_c):
    return pl.pallas_call(
        _ctx_slabs_kernel,
        out_shape=jax.ShapeDtypeStruct((S5_T, B * N_CCHUNK, D), BF16),
        grid=(B,),
        in_specs=[pl.BlockSpec((None, CTX, D), lambda b: (b, 0, 0)),
                  _mod_spec(0, cond=B)],
        out_specs=pl.BlockSpec((S5_T, N_CCHUNK, D), lambda b: (0, b, 0)),
        compiler_params=_params("arbitrary"),
        name="ctx_slabs",
    )(ctx, mod_c)


def _inproj0a_kernel(hs_ref, hc_ref, w_ref, ua_ref, sza_ref, uc_ref):
    r = pl.program_id(0)
    h = hs_ref[...]
    w_ua = w_ref[:, 0:512].astype(BF16)
    ua_ref[...] = _rot_blocks(jnp.dot(h, w_ua, preferred_element_type=F32), r).astype(BF16)
    sza_ref[...] = _silu(jnp.dot(h, w_ref[:, 512:1024].astype(BF16), preferred_element_type=F32)).astype(BF16)
    uc_ref[...] = _rot_blocks(jnp.dot(hc_ref[...], w_ua, preferred_element_type=F32), r).astype(BF16)


def _inproj0a(hs, hcs, w_in_f32):
    slab = lambda r, h: r + BLK * h
    sspec = lambda n, w: pl.BlockSpec((None, n, w), lambda r, h: (slab(r, h), 0, 0))
    so = lambda n: jax.ShapeDtypeStruct((S5_T, n, 512), BF16)
    nl, ncx = B * N_CHUNK, B * N_CCHUNK
    return pl.pallas_call(
        _inproj0a_kernel,
        out_shape=(so(nl), so(nl), so(ncx)),
        grid=(BLK, S5_T // BLK),
        in_specs=[sspec(nl, D), sspec(ncx, D), pl.BlockSpec((D, SGU_COL0), lambda r, h: (0, 0))],
        out_specs=(sspec(nl, 512), sspec(nl, 512), sspec(ncx, 512)),
        compiler_params=_params("arbitrary", "arbitrary"),
        name="inproj0a",
    )(hs, hcs, w_in_f32)


SCAN_GROUPS = 4


def _scan_tiles(sre_ref, sim_ref, h_refs, n_tiles, carry, lams):
    row = lax.broadcasted_iota(jnp.int32, (8, 128), 0)
    lane = lax.broadcasted_iota(jnp.int32, (8, 128), 1)
    first = row < B
    fwd = lane < S5_P

    def body(k, c):
        of = pl.multiple_of(k * 8, 8)
        ob = pl.multiple_of((n_tiles - 1 - k) * 8, 8)
        out = []
        for gi in range(SCAN_GROUPS):
            lre, lim = lams[gi]
            hr, hi = c[2 * gi], c[2 * gi + 1]
            sr = jnp.where(fwd, sre_ref[gi, pl.ds(of, 8), :], pltpu.roll(sre_ref[gi, pl.ds(ob, 8), :], B, axis=0))
            si = jnp.where(fwd, sim_ref[gi, pl.ds(of, 8), :], pltpu.roll(sim_ref[gi, pl.ds(ob, 8), :], B, axis=0))
            h1r = lre * hr - lim * hi + sr
            h1i = lre * hi + lim * hr + si
            r1r = pltpu.roll(h1r, B, axis=0)
            r1i = pltpu.roll(h1i, B, axis=0)
            if h_refs is not None:
                fre_ref, fim_ref, bre_ref, bim_ref = h_refs
                er = jnp.where(first, hr, r1r)
                ei = jnp.where(first, hi, r1i)
                fre_ref[gi, pl.ds(of, 8), :] = er
                fim_ref[gi, pl.ds(of, 8), :] = ei
                bre_ref[gi, pl.ds(ob, 8), :] = pltpu.roll(er, B, axis=0)
                bim_ref[gi, pl.ds(ob, 8), :] = pltpu.roll(ei, B, axis=0)
            h2r = lre * r1r - lim * r1i + sr
            h2i = lre * r1i + lim * r1r + si
            out.append(jnp.where(first, pltpu.roll(h2r, B, axis=0), h2r))
            out.append(jnp.where(first, pltpu.roll(h2i, B, axis=0), h2i))
        return tuple(out)

    return lax.fori_loop(0, n_tiles, body, carry)


def _gather_group(slab_ref, src):
    halves = []
    for h in range(S5_T // BLK):
        acc = slab_ref[BLK * h]
        for s in range(1, BLK):
            acc = jnp.where(src == s, slab_ref[BLK * h + s], acc)
        halves.append(acc)
    return jnp.concatenate(halves, axis=1)


def _s5core_kernel(ul_ref, uc_ref, win_ref, wout_ref, mix_ref, l16_ref, o_ref,
                   u_ref, sre_ref, sim_ref, cre_ref, cim_ref, fre_ref, fim_ref, bre_ref, bim_ref, y_ref):
    nl = N_CHUNK * B
    ncx = N_CCHUNK * B
    blk_l = lax.broadcasted_iota(jnp.int32, (nl, 128), 1) >> H_SHIFT
    blk_c = lax.broadcasted_iota(jnp.int32, (ncx, 128), 1) >> H_SHIFT
    fwd = lax.broadcasted_iota(jnp.int32, (N_CHUNK, 128), 1) < S5_P
    for g0 in range(0, BLK, SCAN_GROUPS):
        for gi in range(SCAN_GROUPS):
            bg = g0 + gi
            win = win_ref[bg]
            src_l = ((blk_l - bg) & (BLK - 1)).astype(F32).astype(BF16)
            src_c = ((blk_c - bg) & (BLK - 1)).astype(F32).astype(BF16)
            u = _gather_group(ul_ref, src_l)
            u_ref[gi] = u
            sl = jnp.dot(u, win, preferred_element_type=F32)
            sc = jnp.dot(_gather_group(uc_ref, src_c), win, preferred_element_type=F32)
            for b in range(B):
                sre_ref[gi, pl.ds(b, N_CHUNK, stride=B), :] = sl[N_CHUNK * b:N_CHUNK * (b + 1), 0:128]
                sim_ref[gi, pl.ds(b, N_CHUNK, stride=B), :] = sl[N_CHUNK * b:N_CHUNK * (b + 1), 128:256]
                cre_ref[gi, pl.ds(b, N_CCHUNK, stride=B), :] = sc[N_CCHUNK * b:N_CCHUNK * (b + 1), 0:128]
                cim_ref[gi, pl.ds(b, N_CCHUNK, stride=B), :] = sc[N_CCHUNK * b:N_CCHUNK * (b + 1), 128:256]
        lams = [(jnp.broadcast_to(l16_ref[g0 + gi, 0:1, :], (8, 128)),
                 jnp.broadcast_to(l16_ref[g0 + gi, 1:2, :], (8, 128))) for gi in range(SCAN_GROUPS)]
        zero = tuple(jnp.zeros((8, 128), F32) for _ in range(2 * SCAN_GROUPS))
        carry = _scan_tiles(cre_ref, cim_ref, None, ncx // 8, zero, lams)
        _scan_tiles(sre_ref, sim_ref, (fre_ref, fim_ref, bre_ref, bim_ref), nl // 8, carry, lams)
        for gi in range(SCAN_GROUPS):
            bg = g0 + gi
            y = jnp.dot(u_ref[gi], mix_ref[bg], preferred_element_type=F32)
            hs = []
            for b in range(B):
                rows = pl.ds(b, N_CHUNK, stride=B)
                hs.append(jnp.concatenate([jnp.where(fwd, fre_ref[gi, rows, :], bre_ref[gi, rows, :]),
                                           jnp.where(fwd, fim_ref[gi, rows, :], bim_ref[gi, rows, :])], axis=1))
            hcat = jnp.concatenate(hs, axis=0).astype(BF16)
            y = y + jnp.dot(hcat, wout_ref[bg], preferred_element_type=F32)
            y_ref[bg] = y.astype(BF16)

    blk = blk_l.astype(F32).astype(BF16)
    for s in range(S5_T):
        h, r = s // BLK, s % BLK
        acc = None
        for j in range(BLK):
            piece = y_ref[(j - r) % BLK, :, 128 * h:128 * (h + 1)]
            acc = piece if acc is None else jnp.where(blk == j, piece, acc)
        o_ref[s] = acc


def _s5core(ul, uc, win, wout, mix, l16):
    TH = S5_T * S5_H
    nl = N_CHUNK * B
    ncx = N_CCHUNK * B
    g4 = lambda r, c: pl.BlockSpec((BLK, r, c), lambda q: (q, 0, 0))
    col = lambda n: pl.BlockSpec((S5_T, n, 128), lambda q: (0, 0, q))
    f32s = lambda n: pltpu.VMEM((SCAN_GROUPS, n, 128), F32)
    return pl.pallas_call(
        _s5core_kernel,
        out_shape=jax.ShapeDtypeStruct((S5_T, nl, S5_W), BF16),
        grid=(S5_G // BLK,),
        in_specs=[col(nl), col(ncx), g4(TH, TH), g4(TH, TH), g4(TH, TH), g4(8, 128)],
        out_specs=col(nl),
        scratch_shapes=[pltpu.VMEM((SCAN_GROUPS, nl, TH), BF16),
                        f32s(nl), f32s(nl), f32s(ncx), f32s(ncx), f32s(nl), f32s(nl), f32s(nl), f32s(nl),
                        pltpu.VMEM((BLK, nl, TH), BF16)],
        compiler_params=_params("arbitrary"),
        name="s5core",
    )(ul, uc, win, wout, mix, l16)


def _cast_on_first_step(src_ref, dst_ref, scale=None):
    @pl.when((pl.program_id(0) == 0) & (pl.program_id(1) == 0))
    def _():
        w = src_ref[...]
        dst_ref[...] = (w if scale is None else scale * w).astype(BF16)


def _const_spec(shape, index):
    return pl.BlockSpec(shape, lambda r, h: index, pipeline_mode=pl.Buffered(1))


def _s5tail_kernel(slat_ref, sza_ref, gluw32_ref, glub_ref, wtop32_ref, y_ref, gluw_ref, wtop_ref):
    _cast_on_first_step(gluw32_ref, gluw_ref)
    _cast_on_first_step(wtop32_ref, wtop_ref)
    unrot = (BLK - pl.program_id(0)) & (BLK - 1)
    for b in range(B):
        rows = slice(N_CHUNK * b, N_CHUNK * (b + 1))
        g = _gelu(_rot_blocks(slat_ref[rows, :].astype(F32), unrot))
        gate = _sigmoid(jnp.dot(g.astype(BF16), gluw_ref[...], preferred_element_type=F32) + glub_ref[...])
        a = (g * gate * sza_ref[rows, :].astype(F32)).astype(BF16)
        y_ref[rows, :] = jnp.dot(a, wtop_ref[...], preferred_element_type=F32).astype(BF16)


def _s5tail(slat, sza, glu_w, glu_b, w_out):
    slab = lambda r, h: r + BLK * h
    sspec = lambda w: pl.BlockSpec((None, N_CHUNK * B, w), lambda r, h: (slab(r, h), 0, 0))
    full = lambda *s: pl.BlockSpec(s, lambda r, h: (0,) * len(s))
    return pl.pallas_call(
        _s5tail_kernel,
        out_shape=jax.ShapeDtypeStruct((S5_T, N_CHUNK * B, D), BF16),
        grid=(BLK, S5_T // BLK),
        in_specs=[sspec(512), sspec(512), _const_spec((S5_W, S5_W), (0, 0)), full(1, 512),
                  _const_spec((S5_W, D), (0, 0))],
        out_specs=sspec(D),
        scratch_shapes=[pltpu.VMEM((S5_W, S5_W), BF16), pltpu.VMEM((S5_W, D), BF16)],
        compiler_params=_params("arbitrary", "arbitrary"),
        name="s5tail",
    )(slat, sza, glu_w, glu_b, w_out)


def _tail0_kernel(x_ref, ys5_ref, guz_ref, vln_ref, mod_ref, sguw_ref, sgub_ref, wbot32_ref, ng_ref, nb_ref, o_ref,
                  wbot_ref):
    _cast_on_first_step(wbot32_ref, wbot_ref)
    tm = x_ref.shape[0]
    lane = lax.broadcasted_iota(jnp.int32, (SGU_CHUNK, 128), 1)
    lo = lane < SGU_HD
    zero = jnp.zeros((SGU_CHUNK, 128), BF16)
    w_pair = [jnp.concatenate([sguw_ref[2 * pi].astype(BF16), sguw_ref[2 * pi + 1].astype(BF16)], axis=1)
              for pi in range(SGU_HEADS // 2)]
    chunks = []
    for ci in range(tm // SGU_CHUNK):
        v = vln_ref[ci * SGU_CHUNK:(ci + 1) * SGU_CHUNK, :]
        cols = []
        for pi in range(SGU_HEADS // 2):
            vp = v[:, 128 * pi:128 * (pi + 1)]
            bm = jnp.concatenate([jnp.where(lo, vp, zero), jnp.where(lo, zero, vp)], axis=0)
            cols.append(jnp.dot(w_pair[pi], bm, preferred_element_type=F32))
        chunks.append(jnp.concatenate(cols, axis=1) + sgub_ref[...])
    s = jnp.concatenate(chunks, axis=0)
    bsg = (guz_ref[...].astype(F32) * s).astype(BF16)
    perm = _chunk_transpose_perm()
    ys5 = jnp.concatenate(
        [jnp.dot(perm, ys5_ref[:, S5_T * j:S5_T * (j + 1), :].reshape(PERM_ROWS, D), preferred_element_type=F32)
         for j in range(tm // PERM_ROWS)], axis=0)
    y = ys5 + jnp.dot(bsg, wbot_ref[...], preferred_element_type=F32)
    gmod = mod_ref[:, 2 * D:3 * D]
    o_ref[...] = _layer_norm(DN_ALPHA * x_ref[...] + gmod * y, ng_ref[0:1, :], nb_ref[0:1, :])


def _tail0(x, ys5, guz, vln, mod, sguw, sgub, w_bot, ng, nb, tm=TOKEN_TILE):
    nct = tm // S5_T
    t512 = pl.BlockSpec((None, tm, 512), lambda b, i: (b, i, 0))
    tD = pl.BlockSpec((None, tm, D), lambda b, i: (b, i, 0))
    full = lambda *s: pl.BlockSpec(s, lambda b, i: (0,) * len(s))
    return pl.pallas_call(
        _tail0_kernel,
        out_shape=jax.ShapeDtypeStruct((B, L, D), F32),
        grid=(B, L // tm),
        in_specs=[tD, pl.BlockSpec((S5_T, nct, D), lambda b, i: (0, b * (N_CHUNK // nct) + i, 0)), t512, t512,
                  _mod_spec(0),
                  full(SGU_HEADS, SGU_CHUNK, SGU_CHUNK), full(SGU_CHUNK, 512),
                  _const_spec((SGU_W, D), (1, 0)), full(DEPTH, D), full(DEPTH, D)],
        out_specs=tD,
        scratch_shapes=[pltpu.VMEM((SGU_W, D), BF16)],
        compiler_params=_params("arbitrary", "arbitrary"),
        name="tail0",
    )(x, ys5, guz, vln, mod, sguw, sgub, w_bot, ng, nb)


CONV_C = D // 2
TILE_ROWS = TOKEN_TILE // GRID_W


def _grid_transpose_in(v, o_ref):
    perm = _chunk_transpose_perm()
    for q in range(GRID_W // S5_T):
        seg = jnp.concatenate([v[GRID_W * r + S5_T * q:GRID_W * r + S5_T * (q + 1), :] for r in range(TILE_ROWS)],
                              axis=0)
        t = jnp.dot(perm, seg, preferred_element_type=F32).astype(BF16)
        o_ref[S5_T * q:S5_T * (q + 1), :, :] = t.reshape(S5_T, TILE_ROWS, v.shape[1])


def _inproj1_kernel(x_ref, mod_ref, w_ref, hgr_ref, hgc_ref, h_ref):
    shift = mod_ref[:, 0:D]
    scale = mod_ref[:, D:2 * D]
    h = (x_ref[...] * (1.0 + scale) + shift).astype(BF16)
    h_ref[...] = h
    dot = lambda lo: jnp.dot(h, w_ref[:, lo:lo + CONV_C].astype(BF16), preferred_element_type=F32)
    hgr_ref[...] = (dot(0) * _sigmoid(dot(D))).astype(BF16)
    _grid_transpose_in((dot(CONV_C) * _sigmoid(dot(D + CONV_C))).astype(BF16), hgc_ref)


def _inproj1(x, mod, w_in_f32, tm=TOKEN_TILE):
    tile = lambda w: pl.BlockSpec((None, tm, w), lambda b, i: (b, i, 0))
    return pl.pallas_call(
        _inproj1_kernel,
        out_shape=(jax.ShapeDtypeStruct((B, L, CONV_C), BF16),
                   jax.ShapeDtypeStruct((B, GRID_W, GRID_W, CONV_C), BF16),
                   jax.ShapeDtypeStruct((B, L, D), BF16)),
        grid=(B, L // tm),
        in_specs=[tile(D),
                  _mod_spec(1),
                  pl.BlockSpec((D, 2 * D), lambda b, i: (0, 0), pipeline_mode=pl.Buffered(1))],
        out_specs=(tile(CONV_C), pl.BlockSpec((None, GRID_W, TILE_ROWS, CONV_C), lambda b, i: (b, 0, i, 0)),
                   tile(D)),
        compiler_params=_params("arbitrary", "arbitrary"),
        name="inproj1",
    )(x, mod, w_in_f32)


DFT_N = 2 * GRID_W
TAPS_PAD = CONV_K + 1


def _dft_constants():
    th = 2.0 * math.pi / DFT_N
    f = np.arange(GRID_W, dtype=np.float64)[:, None]
    p = np.arange(GRID_W, dtype=np.float64)[None, :]
    cosm = np.cos(th * f * p)
    sinm = np.sin(th * f * p)
    alt = np.where(np.arange(GRID_W) % 2 == 0, 1.0, -1.0)
    fwd = np.concatenate([cosm, alt[None, :], sinm[1:]], axis=0)
    cf = np.where(np.arange(GRID_W) == 0, 1.0, 2.0) / DFT_N
    inv = np.concatenate([cosm.T * cf[None, :], (alt / DFT_N)[:, None], sinm.T[:, 1:] * (2.0 / DFT_N)], axis=1)
    sft = (CONV_HALF - np.arange(TAPS_PAD, dtype=np.float64))[None, :]
    live = (np.arange(TAPS_PAD) < CONV_K).astype(np.float64)[None, :]
    f64 = np.where(f == 0, float(GRID_W), f)
    f32 = lambda a: jnp.asarray(a.astype(np.float32))
    return (f32(fwd).astype(BF16), f32(inv).astype(BF16),
            f32(np.cos(th * f * sft) * live), f32(np.sin(th * f * sft) * live), f32(np.cos(th * f64 * sft) * live))


def _fconv_kernel(h_ref, w_ref, b_ref, fwd_ref, inv_ref, c1_ref, s3_ref, c4_ref, o_ref, taps_ref):
    hp = lax.Precision.HIGHEST
    taps_ref[...] = jnp.zeros(taps_ref.shape, F32)
    taps_ref[0:CONV_K, :] = w_ref[...]
    taps = taps_ref[...]
    g_re = jnp.dot(c1_ref[...], taps, preferred_element_type=F32, precision=hp)
    g_im = jnp.dot(s3_ref[...], taps, preferred_element_type=F32, precision=hp)
    g_r2 = jnp.dot(c4_ref[...], taps, preferred_element_type=F32, precision=hp)
    fwd = fwd_ref[...]
    inv = inv_ref[...]
    bias = b_ref[...]
    n_runs = h_ref.shape[0] // GRID_W
    rows = lambda r: slice(GRID_W * r, GRID_W * (r + 1))
    forward = lambda r: jnp.dot(fwd, h_ref[rows(r), :], preferred_element_type=F32)
    ahead = 2
    specs = [forward(r) for r in range(ahead)]
    for r in range(n_runs):
        if r + ahead < n_runs:
            specs.append(forward(r + ahead))
        spec = specs[r]
        a, bm = spec[0:GRID_W], spec[GRID_W:DFT_N]
        prod = jnp.concatenate([a * g_re - bm * g_im, a * g_im + bm * g_r2], axis=0).astype(BF16)
        o_ref[rows(r), :] = (jnp.dot(inv, prod, preferred_element_type=F32) + bias).astype(BF16)


def _fconv(h, taps, bias, half, consts, tm=L):
    fwd, inv, c1, s3, c4 = consts
    c = h.shape[-1]
    tile = pl.BlockSpec((None, tm, c), lambda b, i: (b, i, 0))
    full = lambda *s: pl.BlockSpec(s, lambda b, i: (0,) * len(s))
    cols = lambda r: pl.BlockSpec((None, r, c), lambda b, i: (0, 0, half))
    return pl.pallas_call(
        _fconv_kernel,
        out_shape=jax.ShapeDtypeStruct(h.shape, BF16),
        grid=(B, L // tm),
        in_specs=[tile, cols(CONV_K), cols(1), full(DFT_N, GRID_W), full(GRID_W, DFT_N),
                  full(GRID_W, TAPS_PAD), full(GRID_W, TAPS_PAD), full(GRID_W, TAPS_PAD)],
        out_specs=tile,
        scratch_shapes=[pltpu.VMEM((TAPS_PAD, c), F32)],
        compiler_params=_params("arbitrary", "arbitrary"),
        name="fconv",
    )(h, taps, bias, fwd, inv, c1, s3, c4)


ROW_BLOCK = 32


def _row_blocks(n_rows):
    return [slice(ROW_BLOCK * k, ROW_BLOCK * (k + 1)) for k in range(n_rows // ROW_BLOCK)]


def _tail1_kernel(x_ref, hcr_ref, hcc_ref, h1_ref, wz32_ref, mod_ref, lng_ref, lnb_ref, wout_ref, ng_ref, nb_ref,
                  o_ref, col_ref, z_ref, m_ref, y_ref, wz_ref):
    _cast_on_first_step(wz32_ref, wz_ref, scale=0.5)
    tm = x_ref.shape[0]
    perm = _chunk_transpose_perm()
    for q in range(GRID_W // S5_T):
        blk = hcc_ref[S5_T * q:S5_T * (q + 1), :, :].reshape(PERM_ROWS, CONV_C)
        t = jnp.dot(perm, blk, preferred_element_type=F32)
        for r in range(TILE_ROWS):
            col_ref[GRID_W * r + S5_T * q:GRID_W * r + S5_T * (q + 1), :] = t[S5_T * r:S5_T * (r + 1), :]
    z_ref[...] = jnp.dot(h1_ref[...], wz_ref[...], preferred_element_type=F32)
    lng, lnb = 0.5 * lng_ref[...], 0.5 * lnb_ref[...]
    for rows in _row_blocks(tm):
        hc = jnp.concatenate([hcr_ref[rows, :].astype(F32), col_ref[rows, :]], axis=1)
        m_ref[rows, :] = (_silu_of_half(_layer_norm(hc, lng, lnb)) * _silu_of_half(z_ref[rows, :])).astype(BF16)
    y_ref[...] = jnp.dot(m_ref[...], wout_ref[...], preferred_element_type=F32)
    gmod = mod_ref[:, 2 * D:3 * D]
    ng, nb = ng_ref[1:2, :], nb_ref[1:2, :]
    for rows in _row_blocks(tm):
        o_ref[rows, :] = _layer_norm(DN_ALPHA * x_ref[rows, :] + gmod * y_ref[rows, :], ng, nb)


def _tail1(x, hc_row, hc_col, h1, w_in_f32, mod, ln_g, ln_b, w_out, ng, nb, tm=TOKEN_TILE):
    tile = lambda w: pl.BlockSpec((None, tm, w), lambda b, i: (b, i, 0))
    full = lambda *s: pl.BlockSpec(s, lambda b, i: (0,) * len(s))
    return pl.pallas_call(
        _tail1_kernel,
        out_shape=jax.ShapeDtypeStruct((B, L, D), F32),
        grid=(B, L // tm),
        in_specs=[tile(D), tile(CONV_C),
                  pl.BlockSpec((None, GRID_W, TILE_ROWS, CONV_C), lambda b, i: (b, 0, i, 0)),
                  tile(D), _const_spec((D, D), (0, 2)), _mod_spec(1),
                  full(1, D), full(1, D), full(D, D), full(DEPTH, D), full(DEPTH, D)],
        out_specs=tile(D),
        scratch_shapes=[pltpu.VMEM((tm, CONV_C), F32), pltpu.VMEM((tm, D), F32), pltpu.VMEM((tm, D), BF16),
                        pltpu.VMEM((tm, D), F32), pltpu.VMEM((D, D), BF16)],
        compiler_params=_params("arbitrary", "arbitrary"),
        name="tail1",
    )(x, hc_row, hc_col, h1, w_in_f32, mod, ln_g, ln_b, w_out, ng, nb)


def kernel(x, c, ctx, c_ctx, mod_w, mod_b, norm_g, norm_b, ev_w_in, ev_w_out, s5_lam_re, s5_lam_im, s5_log_dt, s5_b_re, s5_b_im, s5_c_re, s5_c_im, s5_d, glu_w, glu_b, sgu_ln_g, sgu_ln_b, sgu_w, sgu_b, od_w_in, od_w_out, dw_w, dw_b, conv_ln_g, conv_ln_b):
    TH = S5_T * S5_H
    row = lambda v: v.reshape(1, -1)

    mods = _adaln(c, c_ctx, mod_w, mod_b)

    ldt = jnp.broadcast_to(s5_log_dt[0][:, :, None], (2, S5_G, S5_P))
    fb = lambda s: jnp.concatenate([s[:, 0], s[:, 1]], axis=-1)
    lam3 = fb(jnp.stack([s5_lam_re[0], s5_lam_im[0], ldt]))
    bt = fb(jnp.swapaxes(jnp.stack([s5_b_re[0], s5_b_im[0]]), -1, -2))
    cn = fb(jnp.stack([s5_c_re[0], s5_c_im[0]]))
    win, wout, mix, l16 = _s5_weights(lam3, s5_d, bt, cn)

    guz, vln, hs = _inproj0n(x, mods, ev_w_in[0], row(sgu_ln_g[0]), row(sgu_ln_b[0]))
    ua, sza, ua_c = _inproj0a(hs, _ctx_slabs(ctx, mods), ev_w_in[0])
    s_lat = _s5core(ua, ua_c, win, wout, mix, l16)
    y_s5 = _s5tail(s_lat, sza, glu_w[0], row(glu_b[0]), ev_w_out[0])
    sgub = jnp.repeat(sgu_b[0].T, SGU_HD, axis=1)
    x1 = _tail0(x, y_s5, guz, vln, mods, sgu_w[0], sgub, ev_w_out[0], norm_g, norm_b)

    hg_row, hg_col, h1 = _inproj1(x1, mods, od_w_in[0])
    consts = _dft_constants()
    bias = dw_b.reshape(1, 1, 2 * CONV_C)
    hc_row = _fconv(hg_row, dw_w, bias, 0, consts)
    hc_col = _fconv(hg_col.reshape(B, L, CONV_C), dw_w, bias, 1, consts)
    return _tail1(x1, hc_row, hc_col.reshape(B, GRID_W, GRID_W, CONV_C), h1, od_w_in[0], mods,
                  row(conv_ln_g[0]), row(conv_ln_b[0]), od_w_out[0].astype(BF16), norm_g, norm_b)
```

```python
import functools
import math

import jax
import jax.numpy as jnp
import numpy as np
from jax import lax
from jax.experimental import pallas as pl
from jax.experimental.pallas import tpu as pltpu

D = 1024
B = 4
L = 4096
CTX = 256
GRID_W = 64
S5_W = 512
S5_G = 32
S5_H = 16
H_SHIFT = 4
BLK = 128 // S5_H
S5_P = 64
S5_T = 16
SGU_W = 512
SGU_HEADS = 8
SGU_HD = 64
SGU_CHUNK = 128
CONV_K = 31
CONV_HALF = CONV_K // 2
EVEN_IN = 2560
SGU_COL0 = 2 * S5_W
ODD_IN = 3072
DEPTH = 2
DN_ALPHA = (2 * DEPTH) ** 0.25
LN_EPS = 1e-5
N_CHUNK = L // S5_T
N_CCHUNK = CTX // S5_T
VMEM_LIMIT_V7X = 56 * 1024 * 1024
TOKEN_TILE = 1024

F32 = jnp.float32
BF16 = jnp.bfloat16


GELU_C = math.sqrt(2.0 / math.pi)


def _gelu(x):
    hx = 0.5 * x
    return hx * jnp.tanh(x * ((x * x) * (0.044715 * GELU_C) + GELU_C)) + hx


def _sigmoid(x):
    return 0.5 * jnp.tanh(0.5 * x) + 0.5


def _silu_of_half(hx):
    return hx * jnp.tanh(hx) + hx


def _silu(x):
    return _silu_of_half(0.5 * x)


def _layer_norm(x, g, b):
    mu = jnp.mean(x, axis=-1, keepdims=True)
    xc = x - mu
    var = jnp.mean(xc * xc, axis=-1, keepdims=True)
    return xc * lax.rsqrt(var + LN_EPS) * g + b


def _params(*sem):
    return pltpu.CompilerParams(dimension_semantics=sem, vmem_limit_bytes=VMEM_LIMIT_V7X)


ADALN_ROWS = 8
ADALN_TK = 256


def _adaln_kernel(c_ref, cctx_ref, w_ref, b_ref, o_ref, cond_ref, acc_ref):
    layer, k = pl.program_id(0), pl.program_id(1)
    cond_ref[...] = jnp.zeros(cond_ref.shape, F32)
    cond_ref[0:B, :] = c_ref[...]
    cond_ref[B:B + 1, :] = cctx_ref[...]

    def split(v):
        hi = v.astype(BF16)
        return hi, (v - hi.astype(F32)).astype(BF16)

    a_hi, a_lo = split(_silu(cond_ref[:, pl.ds(pl.multiple_of(k * ADALN_TK, ADALN_TK), ADALN_TK)]))
    w_hi, w_lo = split(w_ref[...])
    dot = functools.partial(jnp.dot, preferred_element_type=F32)
    both = dot(jnp.concatenate([a_hi.astype(F32), a_lo.astype(F32)], axis=0).astype(BF16), w_hi)
    part = both[:ADALN_ROWS] + both[ADALN_ROWS:] + dot(a_hi, w_lo)

    @pl.when(k == 0)
    def _():
        acc_ref[...] = part + b_ref[pl.ds(layer, 1), :]

    @pl.when(k > 0)
    def _():
        acc_ref[...] += part

    @pl.when(k == pl.num_programs(1) - 1)
    def _():
        for r in range(ADALN_ROWS):
            o_ref[r] = acc_ref[r:r + 1, :]


def _adaln(c, c_ctx, mod_w, mod_b):
    full = lambda *s: pl.BlockSpec(s, lambda l, k: (0,) * len(s))
    return pl.pallas_call(
        _adaln_kernel,
        out_shape=jax.ShapeDtypeStruct((DEPTH, ADALN_ROWS, 1, 3 * D), F32),
        grid=(DEPTH, D // ADALN_TK),
        in_specs=[full(B, D), full(1, D),
                  pl.BlockSpec((None, ADALN_TK, 3 * D), lambda l, k: (l, k, 0)),
                  full(DEPTH, 3 * D)],
        out_specs=pl.BlockSpec((None, ADALN_ROWS, 1, 3 * D), lambda l, k: (l, 0, 0, 0)),
        scratch_shapes=[pltpu.VMEM((ADALN_ROWS, D), F32), pltpu.VMEM((ADALN_ROWS, 3 * D), F32)],
        compiler_params=_params("arbitrary", "arbitrary"),
        name="adaln",
    )(c, c_ctx.reshape(1, D), mod_w, mod_b)


def _mod_spec(layer, cond=None):
    if cond is None:
        return pl.BlockSpec((None, None, 1, 3 * D), lambda b, i: (layer, b, 0, 0))
    return pl.BlockSpec((None, None, 1, 3 * D), lambda b: (layer, cond, 0, 0))


S5W_GROUPS = BLK


def _cpow(base_pows, j):
    re = None
    im = None
    for k, (pr, pi) in enumerate(base_pows):
        bit = ((j >> k) & 1) == 1
        mr = jnp.where(bit, pr, 1.0)
        mi = jnp.where(bit, pi, 0.0)
        if re is None:
            re, im = mr, mi
        else:
            re, im = re * mr - im * mi, re * mi + im * mr
    return re, im


def _squarings(pr, pi, n):
    out = [(pr, pi)]
    for _ in range(n - 1):
        pr, pi = pr * pr - pi * pi, 2.0 * pr * pi
        out.append((pr, pi))
    return out


def _shift_lanes(x, n):
    lane = lax.broadcasted_iota(jnp.int32, (S5_H, 128), 1)
    lo, hi = x[:, :128], x[:, 128:]
    if n == 0:
        return x
    if n < 128:
        rlo = pltpu.roll(lo, n, axis=1)
        rhi = pltpu.roll(hi, n, axis=1)
        return jnp.concatenate([jnp.where(lane >= n, rlo, 0.0), jnp.where(lane >= n, rhi, rlo)], axis=1)
    m = n - 128
    rlo = lo if m == 0 else pltpu.roll(lo, m, axis=1)
    return jnp.concatenate([jnp.zeros_like(lo), jnp.where(lane >= m, rlo, 0.0)], axis=1)


def _unshift_lanes(x, n):
    lane = lax.broadcasted_iota(jnp.int32, (S5_H, 128), 1)
    lo, hi = x[:, :128], x[:, 128:]
    if n == 0:
        return x
    if n < 128:
        rlo = pltpu.roll(lo, 128 - n, axis=1)
        rhi = pltpu.roll(hi, 128 - n, axis=1)
        keep = lane < 128 - n
        return jnp.concatenate([jnp.where(keep, rlo, rhi), jnp.where(keep, rhi, 0.0)], axis=1)
    m = n - 128
    rhi = hi if m == 0 else pltpu.roll(hi, 128 - m, axis=1)
    return jnp.concatenate([jnp.where(lane < 128 - m, rhi, 0.0), jnp.zeros_like(lo)], axis=1)


def _s5w_group(gi, bg, disc, d_t, bt_ref, cn_ref, win_ref, wout_ref, mix_ref, l16_ref):
    TH = S5_T * S5_H

    def chunk_pos(idx):
        return (((idx >> H_SHIFT) - bg) & (BLK - 1)) + ((idx >> 7) << 3)

    lr, li, cr, ci = [v[gi:gi + 1] for v in disc]
    pows_row = _squarings(lr, li, 5)
    l16_ref[gi, 0:1, :] = pows_row[4][0]
    l16_ref[gi, 1:2, :] = pows_row[4][1]
    l16_ref[gi, 2:8, :] = jnp.zeros((6, 128), F32)
    btr = bt_ref[0, gi]
    bti = bt_ref[1, gi]
    bbr = cr * btr - ci * bti
    bbi = cr * bti + ci * btr
    blk16 = lax.broadcasted_iota(jnp.int32, (S5_T, 128), 0)
    is_f16 = lax.broadcasted_iota(jnp.int32, (S5_T, 128), 1) < S5_P
    pos16 = chunk_pos(blk16 << H_SHIFT)
    pr16, pi16 = _cpow(pows_row[:4], jnp.where(is_f16, S5_T - 1 - pos16, pos16))
    rep_rows = lambda v: jnp.broadcast_to(v[:, None, :], (S5_T, S5_H, 128)).reshape(TH, 128)
    pr, pi = rep_rows(pr16), rep_rows(pi16)
    tbr = jnp.broadcast_to(bbr[None], (S5_T, S5_H, 128)).reshape(TH, 128)
    tbi = jnp.broadcast_to(bbi[None], (S5_T, S5_H, 128)).reshape(TH, 128)
    win_ref[gi, :, 0:128] = (pr * tbr - pi * tbi).astype(BF16)
    win_ref[gi, :, 128:256] = (pr * tbi + pi * tbr).astype(BF16)

    hp = lax.Precision.HIGHEST
    dot = functools.partial(jnp.dot, preferred_element_type=F32, precision=hp)
    def col256(r):
        col = jnp.broadcast_to(r, (2 * S5_P, 2 * S5_P)).T
        return jnp.concatenate([col, col], axis=1)

    def tiled_t(cn):
        t8 = jnp.broadcast_to(cn[None], (BLK, S5_H, 2 * S5_P)).reshape(2 * S5_P, 2 * S5_P).T
        return jnp.concatenate([t8, t8], axis=1)

    cpows = _squarings(col256(lr), col256(li), 4)
    row = lax.broadcasted_iota(jnp.int32, (2 * S5_P, TH), 0)
    lane_w = lax.broadcasted_iota(jnp.int32, (2 * S5_P, TH), 1)
    t_idx = chunk_pos(lane_w)
    j_idx = lane_w >> H_SHIFT
    is_f = row < S5_P
    ctr = tiled_t(cn_ref[0, gi])
    cti = tiled_t(cn_ref[1, gi])
    er, ei = _cpow(cpows, jnp.where(is_f, t_idx, S5_T - 1 - t_idx))
    er, ei = er * cpows[0][0] - ei * cpows[0][1], er * cpows[0][1] + ei * cpows[0][0]
    wr = ctr * er - cti * ei
    wi = ctr * ei + cti * er
    wout_ref[gi, 0:128, :] = wr.astype(BF16)
    wout_ref[gi, 128:256, :] = (-wi).astype(BF16)
    kr, ki = _cpow(cpows, jnp.where(is_f, j_idx, S5_T - 1 - j_idx))
    ekr = ctr * kr - cti * ki
    eki = ctr * ki + cti * kr
    lane16 = lax.broadcasted_iota(jnp.int32, (S5_H, 128), 1)
    mf = lane16 < S5_P
    kkf = dot(jnp.where(mf, bbr, 0.0), ekr) - dot(jnp.where(mf, bbi, 0.0), eki)
    kkb = dot(jnp.where(mf, 0.0, bbr), ekr) - dot(jnp.where(mf, 0.0, bbi), eki)
    d_rows = d_t[S5_H * gi:S5_H * (gi + 1), :]
    dl = jnp.concatenate([d_rows, d_rows], axis=1)
    r16 = lax.broadcasted_iota(jnp.int32, (S5_H, TH), 0)
    l256 = lax.broadcasted_iota(jnp.int32, (S5_H, TH), 1)
    rot = bg * S5_H
    for s in range(S5_T):
        blk = _shift_lanes(kkf, S5_H * s) + _unshift_lanes(kkb, S5_H * (S5_T - 1 - s))
        blk = blk + jnp.where(l256 == r16 + S5_H * s, dl, 0.0)
        if rot:
            blk = jnp.concatenate([pltpu.roll(blk[:, :128], rot, axis=1), pltpu.roll(blk[:, 128:], rot, axis=1)], axis=1)
        rho = ((s + bg) & (BLK - 1)) + (s & BLK)
        mix_ref[gi, S5_H * rho:S5_H * (rho + 1), :] = blk.astype(BF16)


def _s5w_kernel(lam_ref, d_ref, *refs):
    d_t = jnp.broadcast_to(d_ref[...], (S5W_GROUPS * S5_H, S5W_GROUPS * S5_H)).T
    lr = lam_ref[0]
    li = lam_ref[1]
    dt = jnp.exp(lam_ref[2])
    mag = jnp.exp(lr * dt)
    br = mag * jnp.cos(li * dt)
    bi = mag * jnp.sin(li * dt)
    inv = 1.0 / (lr * lr + li * li)
    nr = br - 1.0
    disc = (br, bi, (nr * lr + bi * li) * inv, (bi * lr - nr * li) * inv)
    for gi in range(S5W_GROUPS):
        _s5w_group(gi, gi % BLK, disc, d_t, *refs)


def _s5_weights(lam3, d, bt, cn):
    TH = S5_T * S5_H
    g3 = lambda r, c: pl.BlockSpec((S5W_GROUPS, r, c), lambda g: (g, 0, 0))
    ri = pl.BlockSpec((2, S5W_GROUPS, S5_H, 2 * S5_P), lambda g: (0, g, 0, 0))
    wshape = jax.ShapeDtypeStruct((S5_G, TH, TH), BF16)
    return pl.pallas_call(
        _s5w_kernel,
        out_shape=(wshape, wshape, wshape, jax.ShapeDtypeStruct((S5_G, 8, 128), F32)),
        grid=(S5_G // S5W_GROUPS,),
        in_specs=[pl.BlockSpec((3, S5W_GROUPS, 2 * S5_P), lambda g: (0, g, 0)),
                  pl.BlockSpec((1, S5W_GROUPS * S5_H), lambda g: (0, g)), ri, ri],
        out_specs=(g3(TH, TH), g3(TH, TH), g3(TH, TH), g3(8, 128)),
        compiler_params=_params("arbitrary"),
        name="s5_weights",
    )(lam3, d, bt, cn)


def _rot_blocks(v, r):
    cols = [pltpu.roll(v[:, 128 * q:128 * (q + 1)], S5_H * r, axis=1) for q in range(v.shape[1] // 128)]
    return jnp.concatenate(cols, axis=1)


PERM_ROWS = S5_T * S5_T
assert CTX == PERM_ROWS


def _chunk_transpose_perm():
    ri = lax.broadcasted_iota(jnp.int32, (PERM_ROWS, PERM_ROWS), 0)
    ci = lax.broadcasted_iota(jnp.int32, (PERM_ROWS, PERM_ROWS), 1)
    hit = ((ri >> H_SHIFT) == (ci & (S5_T - 1))) & ((ri & (S5_T - 1)) == (ci >> H_SHIFT))
    return jnp.where(hit, 1.0, 0.0).astype(BF16)


def _inproj0n_kernel(x_ref, mod_ref, w32_ref, lng_ref, lnb_ref, guz_ref, vln_ref, hs_ref, w_ref):
    _cast_on_first_step(w32_ref.at[:, SGU_COL0:EVEN_IN], w_ref)
    shift = mod_ref[:, 0:D]
    scale = mod_ref[:, D:2 * D]
    hb = (x_ref[...] * (1.0 + scale) + shift).astype(BF16)
    perm = _chunk_transpose_perm()
    for j in range(hb.shape[0] // PERM_ROWS):
        blk = jnp.dot(perm, hb[PERM_ROWS * j:PERM_ROWS * (j + 1), :], preferred_element_type=F32).astype(BF16)
        for s in range(S5_T):
            hs_ref[s, S5_T * j:S5_T * (j + 1), :] = blk[S5_T * s:S5_T * (s + 1), :]
    dot = lambda lo: jnp.dot(hb, w_ref[:, lo:lo + 512], preferred_element_type=F32)
    guz_ref[...] = (_gelu(dot(0)) * _silu(dot(1024))).astype(BF16)
    vln_ref[...] = _layer_norm(_gelu(dot(512)), lng_ref[...], lnb_ref[...]).astype(BF16)


def _inproj0n(x, mod, w_in_f32, ln_g, ln_b, tm=TOKEN_TILE):
    nct = tm // S5_T
    o = jax.ShapeDtypeStruct((B, L, 512), BF16)
    ospec = pl.BlockSpec((None, tm, 512), lambda b, i: (b, i, 0))
    full = lambda *s: pl.BlockSpec(s, lambda b, i: (0,) * len(s))
    return pl.pallas_call(
        _inproj0n_kernel,
        out_shape=(o, o, jax.ShapeDtypeStruct((S5_T, B * N_CHUNK, D), BF16)),
        grid=(B, L // tm),
        in_specs=[pl.BlockSpec((None, tm, D), lambda b, i: (b, i, 0)),
                  _mod_spec(0),
                  pl.BlockSpec((D, EVEN_IN), lambda b, i: (0, 0), pipeline_mode=pl.Buffered(1)),
                  full(1, 512), full(1, 512)],
        out_specs=(ospec, ospec,
                   pl.BlockSpec((S5_T, nct, D), lambda b, i: (0, b * (N_CHUNK // nct) + i, 0))),
        scratch_shapes=[pltpu.VMEM((D, EVEN_IN - SGU_COL0), BF16)],
        compiler_params=_params("arbitrary", "arbitrary"),
        name="inproj0n",
    )(x, mod, w_in_f32, ln_g, ln_b)


def _ctx_slabs_kernel(x_ref, mod_ref, hs_ref):
    hb = (x_ref[...] * (1.0 + mod_ref[:, D:2 * D]) + mod_ref[:, 0:D]).astype(BF16)
    blk = jnp.dot(_chunk_transpose_perm(), hb, preferred_element_type=F32).astype(BF16)
    for s in range(S5_T):
        hs_ref[s] = blk[N_CCHUNK * s:N_CCHUNK * (s + 1), :]


def _ctx_slabs(ctx, mod_c):
    return pl.pallas_call(
        _ctx_slabs_kernel,
        out_shape=jax.ShapeDtypeStruct((S5_T, B * N_CCHUNK, D), BF16),
        grid=(B,),
        in_specs=[pl.BlockSpec((None, CTX, D), lambda b: (b, 0, 0)),
                  _mod_spec(0, cond=B)],
        out_specs=pl.BlockSpec((S5_T, N_CCHUNK, D), lambda b: (0, b, 0)),
        compiler_params=_params("arbitrary"),
        name="ctx_slabs",
    )(ctx, mod_c)


def _inproj0a_kernel(hs_ref, hc_ref, w32_ref, ua_ref, sza_ref, uc_ref, w_ref):
    _cast_on_first_step(w32_ref, w_ref)
    r = pl.program_id(0)
    h = hs_ref[...]
    w_ua = w_ref[:, 0:512]
    ua_ref[...] = _rot_blocks(jnp.dot(h, w_ua, preferred_element_type=F32), r).astype(BF16)
    sza_ref[...] = _silu(jnp.dot(h, w_ref[:, 512:1024], preferred_element_type=F32)).astype(BF16)
    uc_ref[...] = _rot_blocks(jnp.dot(hc_ref[...], w_ua, preferred_element_type=F32), r).astype(BF16)


def _inproj0a(hs, hcs, w_in_f32):
    slab = lambda r, h: r + BLK * h
    sspec = lambda n, w: pl.BlockSpec((None, n, w), lambda r, h: (slab(r, h), 0, 0))
    so = lambda n: jax.ShapeDtypeStruct((S5_T, n, 512), BF16)
    nl, ncx = B * N_CHUNK, B * N_CCHUNK
    return pl.pallas_call(
        _inproj0a_kernel,
        out_shape=(so(nl), so(nl), so(ncx)),
        grid=(BLK, S5_T // BLK),
        in_specs=[sspec(nl, D), sspec(ncx, D), _const_spec((D, SGU_COL0), (0, 0))],
        out_specs=(sspec(nl, 512), sspec(nl, 512), sspec(ncx, 512)),
        scratch_shapes=[pltpu.VMEM((D, SGU_COL0), BF16)],
        compiler_params=_params("arbitrary", "arbitrary"),
        name="inproj0a",
    )(hs, hcs, w_in_f32)


SCAN_GROUPS = 4


def _scan_tiles(sre_ref, sim_ref, h_refs, n_tiles, carry, lams):
    row = lax.broadcasted_iota(jnp.int32, (8, 128), 0)
    lane = lax.broadcasted_iota(jnp.int32, (8, 128), 1)
    first = row < B
    fwd = lane < S5_P

    def body(k, c):
        of = pl.multiple_of(k * 8, 8)
        ob = pl.multiple_of((n_tiles - 1 - k) * 8, 8)
        out = []
        for gi in range(SCAN_GROUPS):
            lre, lim = lams[gi]
            hr, hi = c[2 * gi], c[2 * gi + 1]
            sr = jnp.where(fwd, sre_ref[gi, pl.ds(of, 8), :], pltpu.roll(sre_ref[gi, pl.ds(ob, 8), :], B, axis=0))
            si = jnp.where(fwd, sim_ref[gi, pl.ds(of, 8), :], pltpu.roll(sim_ref[gi, pl.ds(ob, 8), :], B, axis=0))
            h1r = lre * hr - lim * hi + sr
            h1i = lre * hi + lim * hr + si
            r1r = pltpu.roll(h1r, B, axis=0)
            r1i = pltpu.roll(h1i, B, axis=0)
            if h_refs is not None:
                fre_ref, fim_ref, bre_ref, bim_ref = h_refs
                er = jnp.where(first, hr, r1r)
                ei = jnp.where(first, hi, r1i)
                fre_ref[gi, pl.ds(of, 8), :] = er
                fim_ref[gi, pl.ds(of, 8), :] = ei
                bre_ref[gi, pl.ds(ob, 8), :] = pltpu.roll(er, B, axis=0)
                bim_ref[gi, pl.ds(ob, 8), :] = pltpu.roll(ei, B, axis=0)
            h2r = lre * r1r - lim * r1i + sr
            h2i = lre * r1i + lim * r1r + si
            out.append(jnp.where(first, pltpu.roll(h2r, B, axis=0), h2r))
            out.append(jnp.where(first, pltpu.roll(h2i, B, axis=0), h2i))
        return tuple(out)

    return lax.fori_loop(0, n_tiles, body, carry)


def _gather_group(slab_ref, src):
    halves = []
    for h in range(S5_T // BLK):
        acc = slab_ref[BLK * h]
        for s in range(1, BLK):
            acc = jnp.where(src == s, slab_ref[BLK * h + s], acc)
        halves.append(acc)
    return jnp.concatenate(halves, axis=1)


def _s5core_kernel(ul_ref, uc_ref, win_ref, wout_ref, mix_ref, l16_ref, o_ref,
                   u_ref, sre_ref, sim_ref, cre_ref, cim_ref, fre_ref, fim_ref, bre_ref, bim_ref, y_ref):
    nl = N_CHUNK * B
    ncx = N_CCHUNK * B
    blk_l = lax.broadcasted_iota(jnp.int32, (nl, 128), 1) >> H_SHIFT
    blk_c = lax.broadcasted_iota(jnp.int32, (ncx, 128), 1) >> H_SHIFT
    fwd = lax.broadcasted_iota(jnp.int32, (N_CHUNK, 128), 1) < S5_P
    for g0 in range(0, BLK, SCAN_GROUPS):
        for gi in range(SCAN_GROUPS):
            bg = g0 + gi
            win = win_ref[bg]
            src_l = ((blk_l - bg) & (BLK - 1)).astype(F32).astype(BF16)
            src_c = ((blk_c - bg) & (BLK - 1)).astype(F32).astype(BF16)
            u = _gather_group(ul_ref, src_l)
            u_ref[gi] = u
            sl = jnp.dot(u, win, preferred_element_type=F32)
            sc = jnp.dot(_gather_group(uc_ref, src_c), win, preferred_element_type=F32)
            for b in range(B):
                sre_ref[gi, pl.ds(b, N_CHUNK, stride=B), :] = sl[N_CHUNK * b:N_CHUNK * (b + 1), 0:128]
                sim_ref[gi, pl.ds(b, N_CHUNK, stride=B), :] = sl[N_CHUNK * b:N_CHUNK * (b + 1), 128:256]
                cre_ref[gi, pl.ds(b, N_CCHUNK, stride=B), :] = sc[N_CCHUNK * b:N_CCHUNK * (b + 1), 0:128]
                cim_ref[gi, pl.ds(b, N_CCHUNK, stride=B), :] = sc[N_CCHUNK * b:N_CCHUNK * (b + 1), 128:256]
        lams = [(jnp.broadcast_to(l16_ref[g0 + gi, 0:1, :], (8, 128)),
                 jnp.broadcast_to(l16_ref[g0 + gi, 1:2, :], (8, 128))) for gi in range(SCAN_GROUPS)]
        zero = tuple(jnp.zeros((8, 128), F32) for _ in range(2 * SCAN_GROUPS))
        carry = _scan_tiles(cre_ref, cim_ref, None, ncx // 8, zero, lams)
        _scan_tiles(sre_ref, sim_ref, (fre_ref, fim_ref, bre_ref, bim_ref), nl // 8, carry, lams)
        for gi in range(SCAN_GROUPS):
            bg = g0 + gi
            y = jnp.dot(u_ref[gi], mix_ref[bg], preferred_element_type=F32)
            hs = []
            for b in range(B):
                rows = pl.ds(b, N_CHUNK, stride=B)
                hs.append(jnp.concatenate([jnp.where(fwd, fre_ref[gi, rows, :], bre_ref[gi, rows, :]),
                                           jnp.where(fwd, fim_ref[gi, rows, :], bim_ref[gi, rows, :])], axis=1))
            hcat = jnp.concatenate(hs, axis=0).astype(BF16)
            y = y + jnp.dot(hcat, wout_ref[bg], preferred_element_type=F32)
            y_ref[bg] = y.astype(BF16)

    blk = blk_l.astype(F32).astype(BF16)
    for s in range(S5_T):
        h, r = s // BLK, s % BLK
        acc = None
        for j in range(BLK):
            piece = y_ref[(j - r) % BLK, :, 128 * h:128 * (h + 1)]
            acc = piece if acc is None else jnp.where(blk == j, piece, acc)
        o_ref[s] = acc


def _s5core(ul, uc, win, wout, mix, l16):
    TH = S5_T * S5_H
    nl = N_CHUNK * B
    ncx = N_CCHUNK * B
    g4 = lambda r, c: pl.BlockSpec((BLK, r, c), lambda q: (q, 0, 0))
    col = lambda n: pl.BlockSpec((S5_T, n, 128), lambda q: (0, 0, q))
    f32s = lambda n: pltpu.VMEM((SCAN_GROUPS, n, 128), F32)
    return pl.pallas_call(
        _s5core_kernel,
        out_shape=jax.ShapeDtypeStruct((S5_T, nl, S5_W), BF16),
        grid=(S5_G // BLK,),
        in_specs=[col(nl), col(ncx), g4(TH, TH), g4(TH, TH), g4(TH, TH), g4(8, 128)],
        out_specs=col(nl),
        scratch_shapes=[pltpu.VMEM((SCAN_GROUPS, nl, TH), BF16),
                        f32s(nl), f32s(nl), f32s(ncx), f32s(ncx), f32s(nl), f32s(nl), f32s(nl), f32s(nl),
                        pltpu.VMEM((BLK, nl, TH), BF16)],
        compiler_params=_params("arbitrary"),
        name="s5core",
    )(ul, uc, win, wout, mix, l16)


def _cast_on_first_step(src_ref, dst_ref, scale=None):
    @pl.when((pl.program_id(0) == 0) & (pl.program_id(1) == 0))
    def _():
        w = src_ref[...]
        dst_ref[...] = (w if scale is None else scale * w).astype(BF16)


def _const_spec(shape, index):
    return pl.BlockSpec(shape, lambda r, h: index, pipeline_mode=pl.Buffered(1))


def _s5tail_kernel(slat_ref, sza_ref, gluw32_ref, glub_ref, wtop32_ref, y_ref, gluw_ref, wtop_ref):
    _cast_on_first_step(gluw32_ref, gluw_ref)
    _cast_on_first_step(wtop32_ref, wtop_ref)
    unrot = (BLK - pl.program_id(0)) & (BLK - 1)
    for b in range(B):
        rows = slice(N_CHUNK * b, N_CHUNK * (b + 1))
        g = _gelu(_rot_blocks(slat_ref[rows, :].astype(F32), unrot))
        gate = _sigmoid(jnp.dot(g.astype(BF16), gluw_ref[...], preferred_element_type=F32) + glub_ref[...])
        a = (g * gate * sza_ref[rows, :].astype(F32)).astype(BF16)
        y_ref[rows, :] = jnp.dot(a, wtop_ref[...], preferred_element_type=F32).astype(BF16)


def _s5tail(slat, sza, glu_w, glu_b, w_out):
    slab = lambda r, h: r + BLK * h
    sspec = lambda w: pl.BlockSpec((None, N_CHUNK * B, w), lambda r, h: (slab(r, h), 0, 0))
    full = lambda *s: pl.BlockSpec(s, lambda r, h: (0,) * len(s))
    return pl.pallas_call(
        _s5tail_kernel,
        out_shape=jax.ShapeDtypeStruct((S5_T, N_CHUNK * B, D), BF16),
        grid=(BLK, S5_T // BLK),
        in_specs=[sspec(512), sspec(512), _const_spec((S5_W, S5_W), (0, 0)), full(1, 512),
                  _const_spec((S5_W, D), (0, 0))],
        out_specs=sspec(D),
        scratch_shapes=[pltpu.VMEM((S5_W, S5_W), BF16), pltpu.VMEM((S5_W, D), BF16)],
        compiler_params=_params("arbitrary", "arbitrary"),
        name="s5tail",
    )(slat, sza, glu_w, glu_b, w_out)


def _tail0_kernel(x_ref, ys5_ref, guz_ref, vln_ref, mod_ref, sguw_ref, sgub_ref, wbot32_ref, ng_ref, nb_ref, o_ref,
                  wbot_ref):
    _cast_on_first_step(wbot32_ref, wbot_ref)
    tm = x_ref.shape[0]
    lane = lax.broadcasted_iota(jnp.int32, (SGU_CHUNK, 128), 1)
    lo = lane < SGU_HD
    zero = jnp.zeros((SGU_CHUNK, 128), BF16)
    w_pair = [jnp.concatenate([sguw_ref[2 * pi].astype(BF16), sguw_ref[2 * pi + 1].astype(BF16)], axis=1)
              for pi in range(SGU_HEADS // 2)]
    chunks = []
    for ci in range(tm // SGU_CHUNK):
        v = vln_ref[ci * SGU_CHUNK:(ci + 1) * SGU_CHUNK, :]
        cols = []
        for pi in range(SGU_HEADS // 2):
            vp = v[:, 128 * pi:128 * (pi + 1)]
            bm = jnp.concatenate([jnp.where(lo, vp, zero), jnp.where(lo, zero, vp)], axis=0)
            cols.append(jnp.dot(w_pair[pi], bm, preferred_element_type=F32))
        chunks.append(jnp.concatenate(cols, axis=1) + sgub_ref[...])
    s = jnp.concatenate(chunks, axis=0)
    bsg = (guz_ref[...].astype(F32) * s).astype(BF16)
    perm = _chunk_transpose_perm()
    ys5 = jnp.concatenate(
        [jnp.dot(perm, ys5_ref[:, S5_T * j:S5_T * (j + 1), :].reshape(PERM_ROWS, D), preferred_element_type=F32)
         for j in range(tm // PERM_ROWS)], axis=0)
    y = ys5 + jnp.dot(bsg, wbot_ref[...], preferred_element_type=F32)
    gmod = mod_ref[:, 2 * D:3 * D]
    o_ref[...] = _layer_norm(DN_ALPHA * x_ref[...] + gmod * y, ng_ref[0:1, :], nb_ref[0:1, :])


def _tail0(x, ys5, guz, vln, mod, sguw, sgub, w_bot, ng, nb, tm=TOKEN_TILE):
    nct = tm // S5_T
    t512 = pl.BlockSpec((None, tm, 512), lambda b, i: (b, i, 0))
    tD = pl.BlockSpec((None, tm, D), lambda b, i: (b, i, 0))
    full = lambda *s: pl.BlockSpec(s, lambda b, i: (0,) * len(s))
    return pl.pallas_call(
        _tail0_kernel,
        out_shape=jax.ShapeDtypeStruct((B, L, D), F32),
        grid=(B, L // tm),
        in_specs=[tD, pl.BlockSpec((S5_T, nct, D), lambda b, i: (0, b * (N_CHUNK // nct) + i, 0)), t512, t512,
                  _mod_spec(0),
                  full(SGU_HEADS, SGU_CHUNK, SGU_CHUNK), full(SGU_CHUNK, 512),
                  _const_spec((SGU_W, D), (1, 0)), full(DEPTH, D), full(DEPTH, D)],
        out_specs=tD,
        scratch_shapes=[pltpu.VMEM((SGU_W, D), BF16)],
        compiler_params=_params("arbitrary", "arbitrary"),
        name="tail0",
    )(x, ys5, guz, vln, mod, sguw, sgub, w_bot, ng, nb)


CONV_C = D // 2
TILE_ROWS = TOKEN_TILE // GRID_W


def _grid_transpose_in(v, o_ref):
    perm = _chunk_transpose_perm()
    for q in range(GRID_W // S5_T):
        seg = jnp.concatenate([v[GRID_W * r + S5_T * q:GRID_W * r + S5_T * (q + 1), :] for r in range(TILE_ROWS)],
                              axis=0)
        t = jnp.dot(perm, seg, preferred_element_type=F32).astype(BF16)
        o_ref[S5_T * q:S5_T * (q + 1), :, :] = t.reshape(S5_T, TILE_ROWS, v.shape[1])


def _inproj1_kernel(x_ref, mod_ref, w_ref, hgr_ref, hgc_ref, h_ref):
    shift = mod_ref[:, 0:D]
    scale = mod_ref[:, D:2 * D]
    h = (x_ref[...] * (1.0 + scale) + shift).astype(BF16)
    h_ref[...] = h
    dot = lambda lo: jnp.dot(h, w_ref[:, lo:lo + CONV_C].astype(BF16), preferred_element_type=F32)
    hgr_ref[...] = (dot(0) * _sigmoid(dot(D))).astype(BF16)
    _grid_transpose_in((dot(CONV_C) * _sigmoid(dot(D + CONV_C))).astype(BF16), hgc_ref)


def _inproj1(x, mod, w_in_f32, tm=TOKEN_TILE):
    tile = lambda w: pl.BlockSpec((None, tm, w), lambda b, i: (b, i, 0))
    return pl.pallas_call(
        _inproj1_kernel,
        out_shape=(jax.ShapeDtypeStruct((B, L, CONV_C), BF16),
                   jax.ShapeDtypeStruct((B, GRID_W, GRID_W, CONV_C), BF16),
                   jax.ShapeDtypeStruct((B, L, D), BF16)),
        grid=(B, L // tm),
        in_specs=[tile(D),
                  _mod_spec(1),
                  pl.BlockSpec((D, 2 * D), lambda b, i: (0, 0), pipeline_mode=pl.Buffered(1))],
        out_specs=(tile(CONV_C), pl.BlockSpec((None, GRID_W, TILE_ROWS, CONV_C), lambda b, i: (b, 0, i, 0)),
                   tile(D)),
        compiler_params=_params("arbitrary", "arbitrary"),
        name="inproj1",
    )(x, mod, w_in_f32)


DFT_N = 2 * GRID_W
TAPS_PAD = CONV_K + 1


def _dft_constants():
    th = 2.0 * math.pi / DFT_N
    f = np.arange(GRID_W, dtype=np.float64)[:, None]
    p = np.arange(GRID_W, dtype=np.float64)[None, :]
    cosm = np.cos(th * f * p)
    sinm = np.sin(th * f * p)
    alt = np.where(np.arange(GRID_W) % 2 == 0, 1.0, -1.0)
    fwd = np.concatenate([cosm, alt[None, :], sinm[1:]], axis=0)
    cf = np.where(np.arange(GRID_W) == 0, 1.0, 2.0) / DFT_N
    inv = np.concatenate([cosm.T * cf[None, :], (alt / DFT_N)[:, None], sinm.T[:, 1:] * (2.0 / DFT_N)], axis=1)
    sft = (CONV_HALF - np.arange(TAPS_PAD, dtype=np.float64))[None, :]
    live = (np.arange(TAPS_PAD) < CONV_K).astype(np.float64)[None, :]
    f64 = np.where(f == 0, float(GRID_W), f)
    f32 = lambda a: jnp.asarray(a.astype(np.float32))
    return (f32(fwd).astype(BF16), f32(inv).astype(BF16),
            f32(np.cos(th * f * sft) * live), f32(np.sin(th * f * sft) * live), f32(np.cos(th * f64 * sft) * live))


def _fconv_kernel(h_ref, w_ref, b_ref, fwd_ref, inv_ref, c1_ref, s3_ref, c4_ref, o_ref, taps_ref):
    hp = lax.Precision.HIGHEST
    taps_ref[...] = jnp.zeros(taps_ref.shape, F32)
    taps_ref[0:CONV_K, :] = w_ref[...]
    taps = taps_ref[...]
    g_re = jnp.dot(c1_ref[...], taps, preferred_element_type=F32, precision=hp)
    g_im = jnp.dot(s3_ref[...], taps, preferred_element_type=F32, precision=hp)
    g_r2 = jnp.dot(c4_ref[...], taps, preferred_element_type=F32, precision=hp)
    fwd = fwd_ref[...]
    inv = inv_ref[...]
    bias = b_ref[...]
    n_runs = h_ref.shape[0] // GRID_W
    rows = lambda r: slice(GRID_W * r, GRID_W * (r + 1))
    forward = lambda r: jnp.dot(fwd, h_ref[rows(r), :], preferred_element_type=F32)
    ahead = 2
    specs = [forward(r) for r in range(ahead)]
    for r in range(n_runs):
        if r + ahead < n_runs:
            specs.append(forward(r + ahead))
        spec = specs[r]
        a, bm = spec[0:GRID_W], spec[GRID_W:DFT_N]
        prod = jnp.concatenate([a * g_re - bm * g_im, a * g_im + bm * g_r2], axis=0).astype(BF16)
        o_ref[rows(r), :] = (jnp.dot(inv, prod, preferred_element_type=F32) + bias).astype(BF16)


def _fconv(h, taps, bias, half, consts, tm=L):
    fwd, inv, c1, s3, c4 = consts
    c = h.shape[-1]
    tile = pl.BlockSpec((None, tm, c), lambda b, i: (b, i, 0))
    full = lambda *s: pl.BlockSpec(s, lambda b, i: (0,) * len(s))
    cols = lambda r: pl.BlockSpec((None, r, c), lambda b, i: (0, 0, half))
    return pl.pallas_call(
        _fconv_kernel,
        out_shape=jax.ShapeDtypeStruct(h.shape, BF16),
        grid=(B, L // tm),
        in_specs=[tile, cols(CONV_K), cols(1), full(DFT_N, GRID_W), full(GRID_W, DFT_N),
                  full(GRID_W, TAPS_PAD), full(GRID_W, TAPS_PAD), full(GRID_W, TAPS_PAD)],
        out_specs=tile,
        scratch_shapes=[pltpu.VMEM((TAPS_PAD, c), F32)],
        compiler_params=_params("arbitrary", "arbitrary"),
        name="fconv",
    )(h, taps, bias, fwd, inv, c1, s3, c4)


ROW_BLOCK = 32


def _row_blocks(n_rows):
    return [slice(ROW_BLOCK * k, ROW_BLOCK * (k + 1)) for k in range(n_rows // ROW_BLOCK)]


def _tail1_kernel(x_ref, hcr_ref, hcc_ref, h1_ref, wz32_ref, mod_ref, lng_ref, lnb_ref, wout_ref, ng_ref, nb_ref,
                  o_ref, col_ref, z_ref, m_ref, y_ref, wz_ref):
    _cast_on_first_step(wz32_ref, wz_ref, scale=0.5)
    tm = x_ref.shape[0]
    perm = _chunk_transpose_perm()
    for q in range(GRID_W // S5_T):
        blk = hcc_ref[S5_T * q:S5_T * (q + 1), :, :].reshape(PERM_ROWS, CONV_C)
        t = jnp.dot(perm, blk, preferred_element_type=F32)
        for r in range(TILE_ROWS):
            col_ref[GRID_W * r + S5_T * q:GRID_W * r + S5_T * (q + 1), :] = t[S5_T * r:S5_T * (r + 1), :]
    z_ref[...] = jnp.dot(h1_ref[...], wz_ref[...], preferred_element_type=F32)
    lng, lnb = 0.5 * lng_ref[...], 0.5 * lnb_ref[...]
    for rows in _row_blocks(tm):
        hc = jnp.concatenate([hcr_ref[rows, :].astype(F32), col_ref[rows, :]], axis=1)
        m_ref[rows, :] = (_silu_of_half(_layer_norm(hc, lng, lnb)) * _silu_of_half(z_ref[rows, :])).astype(BF16)
    y_ref[...] = jnp.dot(m_ref[...], wout_ref[...], preferred_element_type=F32)
    gmod = mod_ref[:, 2 * D:3 * D]
    ng, nb = ng_ref[1:2, :], nb_ref[1:2, :]
    for rows in _row_blocks(tm):
        o_ref[rows, :] = _layer_norm(DN_ALPHA * x_ref[rows, :] + gmod * y_ref[rows, :], ng, nb)


def _tail1(x, hc_row, hc_col, h1, w_in_f32, mod, ln_g, ln_b, w_out, ng, nb, tm=TOKEN_TILE):
    tile = lambda w: pl.BlockSpec((None, tm, w), lambda b, i: (b, i, 0))
    full = lambda *s: pl.BlockSpec(s, lambda b, i: (0,) * len(s))
    return pl.pallas_call(
        _tail1_kernel,
        out_shape=jax.ShapeDtypeStruct((B, L, D), F32),
        grid=(B, L // tm),
        in_specs=[tile(D), tile(CONV_C),
                  pl.BlockSpec((None, GRID_W, TILE_ROWS, CONV_C), lambda b, i: (b, 0, i, 0)),
                  tile(D), _const_spec((D, D), (0, 2)), _mod_spec(1),
                  full(1, D), full(1, D), full(D, D), full(DEPTH, D), full(DEPTH, D)],
        out_specs=tile(D),
        scratch_shapes=[pltpu.VMEM((tm, CONV_C), F32), pltpu.VMEM((tm, D), F32), pltpu.VMEM((tm, D), BF16),
                        pltpu.VMEM((tm, D), F32), pltpu.VMEM((D, D), BF16)],
        compiler_params=_params("arbitrary", "arbitrary"),
        name="tail1",
    )(x, hc_row, hc_col, h1, w_in_f32, mod, ln_g, ln_b, w_out, ng, nb)


def kernel(x, c, ctx, c_ctx, mod_w, mod_b, norm_g, norm_b, ev_w_in, ev_w_out, s5_lam_re, s5_lam_im, s5_log_dt, s5_b_re, s5_b_im, s5_c_re, s5_c_im, s5_d, glu_w, glu_b, sgu_ln_g, sgu_ln_b, sgu_w, sgu_b, od_w_in, od_w_out, dw_w, dw_b, conv_ln_g, conv_ln_b):
    TH = S5_T * S5_H
    row = lambda v: v.reshape(1, -1)

    mods = _adaln(c, c_ctx, mod_w, mod_b)

    ldt = jnp.broadcast_to(s5_log_dt[0][:, :, None], (2, S5_G, S5_P))
    fb = lambda s: jnp.concatenate([s[:, 0], s[:, 1]], axis=-1)
    lam3 = fb(jnp.stack([s5_lam_re[0], s5_lam_im[0], ldt]))
    bt = fb(jnp.swapaxes(jnp.stack([s5_b_re[0], s5_b_im[0]]), -1, -2))
    cn = fb(jnp.stack([s5_c_re[0], s5_c_im[0]]))
    win, wout, mix, l16 = _s5_weights(lam3, s5_d, bt, cn)

    guz, vln, hs = _inproj0n(x, mods, ev_w_in[0], row(sgu_ln_g[0]), row(sgu_ln_b[0]))
    ua, sza, ua_c = _inproj0a(hs, _ctx_slabs(ctx, mods), ev_w_in[0])
    s_lat = _s5core(ua, ua_c, win, wout, mix, l16)
    y_s5 = _s5tail(s_lat, sza, glu_w[0], row(glu_b[0]), ev_w_out[0])
    sgub = jnp.repeat(sgu_b[0].T, SGU_HD, axis=1)
    x1 = _tail0(x, y_s5, guz, vln, mods, sgu_w[0], sgub, ev_w_out[0], norm_g, norm_b)

    hg_row, hg_col, h1 = _inproj1(x1, mods, od_w_in[0])
    consts = _dft_constants()
    bias = dw_b.reshape(1, 1, 2 * CONV_C)
    hc_row = _fconv(hg_row, dw_w, bias, 0, consts)
    hc_col = _fconv(hg_col.reshape(B, L, CONV_C), dw_w, bias, 1, consts)
    return _tail1(x1, hc_row, hc_col.reshape(B, GRID_W, GRID_W, CONV_C), h1, od_w_in[0], mods,
                  row(conv_ln_g[0]), row(conv_ln_b[0]), od_w_out[0].astype(BF16), norm_g, norm_b)
```

```python
import functools
import math

import jax
import jax.numpy as jnp
import numpy as np
from jax import lax
from jax.experimental import pallas as pl
from jax.experimental.pallas import tpu as pltpu

D = 1024
B = 4
L = 4096
CTX = 256
GRID_W = 64
S5_W = 512
S5_G = 32
S5_H = 16
H_SHIFT = 4
BLK = 128 // S5_H
S5_P = 64
S5_T = 16
SGU_W = 512
SGU_HEADS = 8
SGU_HD = 64
SGU_CHUNK = 128
CONV_K = 31
CONV_HALF = CONV_K // 2
EVEN_IN = 2560
SGU_COL0 = 2 * S5_W
ODD_IN = 3072
DEPTH = 2
DN_ALPHA = (2 * DEPTH) ** 0.25
LN_EPS = 1e-5
N_CHUNK = L // S5_T
N_CCHUNK = CTX // S5_T
VMEM_LIMIT_V7X = 56 * 1024 * 1024
TOKEN_TILE = 1024

F32 = jnp.float32
BF16 = jnp.bfloat16


GELU_C = math.sqrt(2.0 / math.pi)


def _gelu(x):
    hx = 0.5 * x
    return hx * jnp.tanh(x * ((x * x) * (0.044715 * GELU_C) + GELU_C)) + hx


def _sigmoid(x):
    return 0.5 * jnp.tanh(0.5 * x) + 0.5


def _silu_of_half(hx):
    return hx * jnp.tanh(hx) + hx


def _silu(x):
    return _silu_of_half(0.5 * x)


def _layer_norm(x, g, b):
    mu = jnp.mean(x, axis=-1, keepdims=True)
    xc = x - mu
    var = jnp.mean(xc * xc, axis=-1, keepdims=True)
    return xc * lax.rsqrt(var + LN_EPS) * g + b


def _params(*sem):
    return pltpu.CompilerParams(dimension_semantics=sem, vmem_limit_bytes=VMEM_LIMIT_V7X)


ADALN_ROWS = 8
ADALN_TK = 256
ADALN_NK = D // ADALN_TK
ADALN_STEPS = DEPTH * ADALN_NK
ADALN_BUFS = 3


def _adaln_kernel(c_ref, cctx_ref, w_hbm, b_ref, o_ref, cond_ref, acc_ref, wbuf_ref, sem):
    layer, k = pl.program_id(0), pl.program_id(1)
    step = layer * ADALN_NK + k

    def w_copy(st):
        rows = pl.ds(pl.multiple_of((st % ADALN_NK) * ADALN_TK, ADALN_TK), ADALN_TK)
        slot = st % ADALN_BUFS
        return pltpu.make_async_copy(w_hbm.at[st // ADALN_NK, rows, :], wbuf_ref.at[slot], sem.at[slot])

    @pl.when(step == 0)
    def _():
        for ahead in range(ADALN_BUFS - 1):
            w_copy(step + ahead).start()

    @pl.when(step + ADALN_BUFS - 1 < ADALN_STEPS)
    def _():
        w_copy(step + ADALN_BUFS - 1).start()

    w_copy(step).wait()
    w_ref = wbuf_ref.at[step % ADALN_BUFS]
    cond_ref[...] = jnp.zeros(cond_ref.shape, F32)
    cond_ref[0:B, :] = c_ref[...]
    cond_ref[B:B + 1, :] = cctx_ref[...]

    def split(v):
        hi = v.astype(BF16)
        return hi, (v - hi.astype(F32)).astype(BF16)

    a_hi, a_lo = split(_silu(cond_ref[:, pl.ds(pl.multiple_of(k * ADALN_TK, ADALN_TK), ADALN_TK)]))
    w_hi, w_lo = split(w_ref[...])
    dot = functools.partial(jnp.dot, preferred_element_type=F32)
    both = dot(jnp.concatenate([a_hi.astype(F32), a_lo.astype(F32)], axis=0).astype(BF16), w_hi)
    part = both[:ADALN_ROWS] + both[ADALN_ROWS:] + dot(a_hi, w_lo)

    @pl.when(k == 0)
    def _():
        acc_ref[...] = part + b_ref[pl.ds(layer, 1), :]

    @pl.when(k > 0)
    def _():
        acc_ref[...] += part

    @pl.when(k == pl.num_programs(1) - 1)
    def _():
        for r in range(ADALN_ROWS):
            o_ref[r] = acc_ref[r:r + 1, :]


def _adaln(c, c_ctx, mod_w, mod_b):
    full = lambda *s: pl.BlockSpec(s, lambda l, k: (0,) * len(s))
    return pl.pallas_call(
        _adaln_kernel,
        out_shape=jax.ShapeDtypeStruct((DEPTH, ADALN_ROWS, 1, 3 * D), F32),
        grid=(DEPTH, ADALN_NK),
        in_specs=[full(B, D), full(1, D), pl.BlockSpec(memory_space=pl.ANY), full(DEPTH, 3 * D)],
        out_specs=pl.BlockSpec((None, ADALN_ROWS, 1, 3 * D), lambda l, k: (l, 0, 0, 0)),
        scratch_shapes=[pltpu.VMEM((ADALN_ROWS, D), F32), pltpu.VMEM((ADALN_ROWS, 3 * D), F32),
                        pltpu.VMEM((ADALN_BUFS, ADALN_TK, 3 * D), F32), pltpu.SemaphoreType.DMA((ADALN_BUFS,))],
        compiler_params=_params("arbitrary", "arbitrary"),
        name="adaln",
    )(c, c_ctx.reshape(1, D), mod_w, mod_b)


def _mod_spec(layer, cond=None):
    if cond is None:
        return pl.BlockSpec((None, None, 1, 3 * D), lambda b, i: (layer, b, 0, 0))
    return pl.BlockSpec((None, None, 1, 3 * D), lambda b: (layer, cond, 0, 0))


S5W_GROUPS = BLK


def _cpow(base_pows, j):
    re = None
    im = None
    for k, (pr, pi) in enumerate(base_pows):
        bit = ((j >> k) & 1) == 1
        mr = jnp.where(bit, pr, 1.0)
        mi = jnp.where(bit, pi, 0.0)
        if re is None:
            re, im = mr, mi
        else:
            re, im = re * mr - im * mi, re * mi + im * mr
    return re, im


def _squarings(pr, pi, n):
    out = [(pr, pi)]
    for _ in range(n - 1):
        pr, pi = pr * pr - pi * pi, 2.0 * pr * pi
        out.append((pr, pi))
    return out


def _shift_lanes(x, n):
    lane = lax.broadcasted_iota(jnp.int32, (S5_H, 128), 1)
    lo, hi = x[:, :128], x[:, 128:]
    if n == 0:
        return x
    if n < 128:
        rlo = pltpu.roll(lo, n, axis=1)
        rhi = pltpu.roll(hi, n, axis=1)
        return jnp.concatenate([jnp.where(lane >= n, rlo, 0.0), jnp.where(lane >= n, rhi, rlo)], axis=1)
    m = n - 128
    rlo = lo if m == 0 else pltpu.roll(lo, m, axis=1)
    return jnp.concatenate([jnp.zeros_like(lo), jnp.where(lane >= m, rlo, 0.0)], axis=1)


def _unshift_lanes(x, n):
    lane = lax.broadcasted_iota(jnp.int32, (S5_H, 128), 1)
    lo, hi = x[:, :128], x[:, 128:]
    if n == 0:
        return x
    if n < 128:
        rlo = pltpu.roll(lo, 128 - n, axis=1)
        rhi = pltpu.roll(hi, 128 - n, axis=1)
        keep = lane < 128 - n
        return jnp.concatenate([jnp.where(keep, rlo, rhi), jnp.where(keep, rhi, 0.0)], axis=1)
    m = n - 128
    rhi = hi if m == 0 else pltpu.roll(hi, 128 - m, axis=1)
    return jnp.concatenate([jnp.where(lane < 128 - m, rhi, 0.0), jnp.zeros_like(lo)], axis=1)


def _s5w_group(gi, bg, disc, d_t, bt_ref, cn_ref, win_ref, wout_ref, mix_ref, l16_ref):
    TH = S5_T * S5_H

    def chunk_pos(idx):
        return (((idx >> H_SHIFT) - bg) & (BLK - 1)) + ((idx >> 7) << 3)

    lr, li, cr, ci = [v[gi:gi + 1] for v in disc]
    pows_row = _squarings(lr, li, 5)
    l16_ref[gi, 0:1, :] = pows_row[4][0]
    l16_ref[gi, 1:2, :] = pows_row[4][1]
    l16_ref[gi, 2:8, :] = jnp.zeros((6, 128), F32)
    btr = bt_ref[0, gi]
    bti = bt_ref[1, gi]
    bbr = cr * btr - ci * bti
    bbi = cr * bti + ci * btr
    blk16 = lax.broadcasted_iota(jnp.int32, (S5_T, 128), 0)
    is_f16 = lax.broadcasted_iota(jnp.int32, (S5_T, 128), 1) < S5_P
    pos16 = chunk_pos(blk16 << H_SHIFT)
    pr16, pi16 = _cpow(pows_row[:4], jnp.where(is_f16, S5_T - 1 - pos16, pos16))
    rep_rows = lambda v: jnp.broadcast_to(v[:, None, :], (S5_T, S5_H, 128)).reshape(TH, 128)
    pr, pi = rep_rows(pr16), rep_rows(pi16)
    tbr = jnp.broadcast_to(bbr[None], (S5_T, S5_H, 128)).reshape(TH, 128)
    tbi = jnp.broadcast_to(bbi[None], (S5_T, S5_H, 128)).reshape(TH, 128)
    win_ref[gi, :, 0:128] = (pr * tbr - pi * tbi).astype(BF16)
    win_ref[gi, :, 128:256] = (pr * tbi + pi * tbr).astype(BF16)

    hp = lax.Precision.HIGHEST
    dot = functools.partial(jnp.dot, preferred_element_type=F32, precision=hp)
    def col256(r):
        col = jnp.broadcast_to(r, (2 * S5_P, 2 * S5_P)).T
        return jnp.concatenate([col, col], axis=1)

    def tiled_t(cn):
        t8 = jnp.broadcast_to(cn[None], (BLK, S5_H, 2 * S5_P)).reshape(2 * S5_P, 2 * S5_P).T
        return jnp.concatenate([t8, t8], axis=1)

    cpows = _squarings(col256(lr), col256(li), 4)
    row = lax.broadcasted_iota(jnp.int32, (2 * S5_P, TH), 0)
    lane_w = lax.broadcasted_iota(jnp.int32, (2 * S5_P, TH), 1)
    t_idx = chunk_pos(lane_w)
    j_idx = lane_w >> H_SHIFT
    is_f = row < S5_P
    ctr = tiled_t(cn_ref[0, gi])
    cti = tiled_t(cn_ref[1, gi])
    er, ei = _cpow(cpows, jnp.where(is_f, t_idx, S5_T - 1 - t_idx))
    er, ei = er * cpows[0][0] - ei * cpows[0][1], er * cpows[0][1] + ei * cpows[0][0]
    wr = ctr * er - cti * ei
    wi = ctr * ei + cti * er
    wout_ref[gi, 0:128, :] = wr.astype(BF16)
    wout_ref[gi, 128:256, :] = (-wi).astype(BF16)
    kr, ki = _cpow(cpows, jnp.where(is_f, j_idx, S5_T - 1 - j_idx))
    ekr = ctr * kr - cti * ki
    eki = ctr * ki + cti * kr
    lane16 = lax.broadcasted_iota(jnp.int32, (S5_H, 128), 1)
    mf = lane16 < S5_P
    kkf = dot(jnp.where(mf, bbr, 0.0), ekr) - dot(jnp.where(mf, bbi, 0.0), eki)
    kkb = dot(jnp.where(mf, 0.0, bbr), ekr) - dot(jnp.where(mf, 0.0, bbi), eki)
    d_rows = d_t[S5_H * gi:S5_H * (gi + 1), :]
    dl = jnp.concatenate([d_rows, d_rows], axis=1)
    r16 = lax.broadcasted_iota(jnp.int32, (S5_H, TH), 0)
    l256 = lax.broadcasted_iota(jnp.int32, (S5_H, TH), 1)
    rot = bg * S5_H
    for s in range(S5_T):
        blk = _shift_lanes(kkf, S5_H * s) + _unshift_lanes(kkb, S5_H * (S5_T - 1 - s))
        blk = blk + jnp.where(l256 == r16 + S5_H * s, dl, 0.0)
        if rot:
            blk = jnp.concatenate([pltpu.roll(blk[:, :128], rot, axis=1), pltpu.roll(blk[:, 128:], rot, axis=1)], axis=1)
        rho = ((s + bg) & (BLK - 1)) + (s & BLK)
        mix_ref[gi, S5_H * rho:S5_H * (rho + 1), :] = blk.astype(BF16)


def _s5w_kernel(lam_ref, d_ref, *refs):
    d_t = jnp.broadcast_to(d_ref[...], (S5W_GROUPS * S5_H, S5W_GROUPS * S5_H)).T
    lr = lam_ref[0]
    li = lam_ref[1]
    dt = jnp.exp(lam_ref[2])
    mag = jnp.exp(lr * dt)
    br = mag * jnp.cos(li * dt)
    bi = mag * jnp.sin(li * dt)
    inv = 1.0 / (lr * lr + li * li)
    nr = br - 1.0
    disc = (br, bi, (nr * lr + bi * li) * inv, (bi * lr - nr * li) * inv)
    for gi in range(S5W_GROUPS):
        _s5w_group(gi, gi % BLK, disc, d_t, *refs)


def _s5_weights(lam3, d, bt, cn):
    TH = S5_T * S5_H
    g3 = lambda r, c: pl.BlockSpec((S5W_GROUPS, r, c), lambda g: (g, 0, 0))
    ri = pl.BlockSpec((2, S5W_GROUPS, S5_H, 2 * S5_P), lambda g: (0, g, 0, 0))
    wshape = jax.ShapeDtypeStruct((S5_G, TH, TH), BF16)
    return pl.pallas_call(
        _s5w_kernel,
        out_shape=(wshape, wshape, wshape, jax.ShapeDtypeStruct((S5_G, 8, 128), F32)),
        grid=(S5_G // S5W_GROUPS,),
        in_specs=[pl.BlockSpec((3, S5W_GROUPS, 2 * S5_P), lambda g: (0, g, 0)),
                  pl.BlockSpec((1, S5W_GROUPS * S5_H), lambda g: (0, g)), ri, ri],
        out_specs=(g3(TH, TH), g3(TH, TH), g3(TH, TH), g3(8, 128)),
        compiler_params=_params("arbitrary"),
        name="s5_weights",
    )(lam3, d, bt, cn)


def _rot_blocks(v, r):
    cols = [pltpu.roll(v[:, 128 * q:128 * (q + 1)], S5_H * r, axis=1) for q in range(v.shape[1] // 128)]
    return jnp.concatenate(cols, axis=1)


PERM_ROWS = S5_T * S5_T
assert CTX == PERM_ROWS


def _chunk_transpose_perm():
    ri = lax.broadcasted_iota(jnp.int32, (PERM_ROWS, PERM_ROWS), 0)
    ci = lax.broadcasted_iota(jnp.int32, (PERM_ROWS, PERM_ROWS), 1)
    hit = ((ri >> H_SHIFT) == (ci & (S5_T - 1))) & ((ri & (S5_T - 1)) == (ci >> H_SHIFT))
    return jnp.where(hit, 1.0, 0.0).astype(BF16)


def _inproj0n_kernel(x_ref, mod_ref, w32_ref, lng_ref, lnb_ref, guz_ref, vln_ref, hs_ref, w_ref):
    _cast_on_first_step(w32_ref.at[:, SGU_COL0:EVEN_IN], w_ref)
    shift = mod_ref[:, 0:D]
    scale = mod_ref[:, D:2 * D]
    hb = (x_ref[...] * (1.0 + scale) + shift).astype(BF16)
    perm = _chunk_transpose_perm()
    for j in range(hb.shape[0] // PERM_ROWS):
        blk = jnp.dot(perm, hb[PERM_ROWS * j:PERM_ROWS * (j + 1), :], preferred_element_type=F32).astype(BF16)
        for s in range(S5_T):
            hs_ref[s, S5_T * j:S5_T * (j + 1), :] = blk[S5_T * s:S5_T * (s + 1), :]
    dot = lambda lo: jnp.dot(hb, w_ref[:, lo:lo + 512], preferred_element_type=F32)
    guz_ref[...] = (_gelu(dot(0)) * _silu(dot(1024))).astype(BF16)
    vln_ref[...] = _layer_norm(_gelu(dot(512)), lng_ref[...], lnb_ref[...]).astype(BF16)


def _inproj0n(x, mod, w_in_f32, ln_g, ln_b, tm=TOKEN_TILE):
    nct = tm // S5_T
    o = jax.ShapeDtypeStruct((B, L, 512), BF16)
    ospec = pl.BlockSpec((None, tm, 512), lambda b, i: (b, i, 0))
    full = lambda *s: pl.BlockSpec(s, lambda b, i: (0,) * len(s))
    return pl.pallas_call(
        _inproj0n_kernel,
        out_shape=(o, o, jax.ShapeDtypeStruct((S5_T, B * N_CHUNK, D), BF16)),
        grid=(B, L // tm),
        in_specs=[pl.BlockSpec((None, tm, D), lambda b, i: (b, i, 0)),
                  _mod_spec(0),
                  pl.BlockSpec((D, EVEN_IN), lambda b, i: (0, 0), pipeline_mode=pl.Buffered(1)),
                  full(1, 512), full(1, 512)],
        out_specs=(ospec, ospec,
                   pl.BlockSpec((S5_T, nct, D), lambda b, i: (0, b * (N_CHUNK // nct) + i, 0))),
        scratch_shapes=[pltpu.VMEM((D, EVEN_IN - SGU_COL0), BF16)],
        compiler_params=_params("arbitrary", "arbitrary"),
        name="inproj0n",
    )(x, mod, w_in_f32, ln_g, ln_b)


def _ctx_slabs_kernel(x_ref, mod_ref, hs_ref):
    hb = (x_ref[...] * (1.0 + mod_ref[:, D:2 * D]) + mod_ref[:, 0:D]).astype(BF16)
    blk = jnp.dot(_chunk_transpose_perm(), hb, preferred_element_type=F32).astype(BF16)
    for s in range(S5_T):
        hs_ref[s] = blk[N_CCHUNK * s:N_CCHUNK * (s + 1), :]


def _ctx_slabs(ctx, mod_c):
    return pl.pallas_call(
        _ctx_slabs_kernel,
        out_shape=jax.ShapeDtypeStruct((S5_T, B * N_CCHUNK, D), BF16),
        grid=(B,),
        in_specs=[pl.BlockSpec((None, CTX, D), lambda b: (b, 0, 0)),
                  _mod_spec(0, cond=B)],
        out_specs=pl.BlockSpec((S5_T, N_CCHUNK, D), lambda b: (0, b, 0)),
        compiler_params=_params("arbitrary"),
        name="ctx_slabs",
    )(ctx, mod_c)


def _inproj0a_kernel(hs_ref, hc_ref, w32_ref, ua_ref, sza_ref, uc_ref, w_ref):
    _cast_on_first_step(w32_ref, w_ref)
    r = pl.program_id(0)
    h = hs_ref[...]
    w_ua = w_ref[:, 0:512]
    ua_ref[...] = _rot_blocks(jnp.dot(h, w_ua, preferred_element_type=F32), r).astype(BF16)
    sza_ref[...] = _silu(jnp.dot(h, w_ref[:, 512:1024], preferred_element_type=F32)).astype(BF16)
    uc_ref[...] = _rot_blocks(jnp.dot(hc_ref[...], w_ua, preferred_element_type=F32), r).astype(BF16)


def _inproj0a(hs, hcs, w_in_f32):
    slab = lambda r, h: r + BLK * h
    sspec = lambda n, w: pl.BlockSpec((None, n, w), lambda r, h: (slab(r, h), 0, 0))
    so = lambda n: jax.ShapeDtypeStruct((S5_T, n, 512), BF16)
    nl, ncx = B * N_CHUNK, B * N_CCHUNK
    return pl.pallas_call(
        _inproj0a_kernel,
        out_shape=(so(nl), so(nl), so(ncx)),
        grid=(BLK, S5_T // BLK),
        in_specs=[sspec(nl, D), sspec(ncx, D), _const_spec((D, SGU_COL0), (0, 0))],
        out_specs=(sspec(nl, 512), sspec(nl, 512), sspec(ncx, 512)),
        scratch_shapes=[pltpu.VMEM((D, SGU_COL0), BF16)],
        compiler_params=_params("arbitrary", "arbitrary"),
        name="inproj0a",
    )(hs, hcs, w_in_f32)


SCAN_GROUPS = 4


def _scan_tiles(sre_ref, sim_ref, h_refs, n_tiles, carry, lams):
    row = lax.broadcasted_iota(jnp.int32, (8, 128), 0)
    lane = lax.broadcasted_iota(jnp.int32, (8, 128), 1)
    first = row < B
    fwd = lane < S5_P

    def body(k, c):
        of = pl.multiple_of(k * 8, 8)
        ob = pl.multiple_of((n_tiles - 1 - k) * 8, 8)
        out = []
        for gi in range(SCAN_GROUPS):
            lre, lim = lams[gi]
            hr, hi = c[2 * gi], c[2 * gi + 1]
            sr = jnp.where(fwd, sre_ref[gi, pl.ds(of, 8), :], pltpu.roll(sre_ref[gi, pl.ds(ob, 8), :], B, axis=0))
            si = jnp.where(fwd, sim_ref[gi, pl.ds(of, 8), :], pltpu.roll(sim_ref[gi, pl.ds(ob, 8), :], B, axis=0))
            h1r = lre * hr - lim * hi + sr
            h1i = lre * hi + lim * hr + si
            r1r = pltpu.roll(h1r, B, axis=0)
            r1i = pltpu.roll(h1i, B, axis=0)
            if h_refs is not None:
                fre_ref, fim_ref, bre_ref, bim_ref = h_refs
                er = jnp.where(first, hr, r1r)
                ei = jnp.where(first, hi, r1i)
                fre_ref[gi, pl.ds(of, 8), :] = er
                fim_ref[gi, pl.ds(of, 8), :] = ei
                bre_ref[gi, pl.ds(ob, 8), :] = pltpu.roll(er, B, axis=0)
                bim_ref[gi, pl.ds(ob, 8), :] = pltpu.roll(ei, B, axis=0)
            h2r = lre * r1r - lim * r1i + sr
            h2i = lre * r1i + lim * r1r + si
            out.append(jnp.where(first, pltpu.roll(h2r, B, axis=0), h2r))
            out.append(jnp.where(first, pltpu.roll(h2i, B, axis=0), h2i))
        return tuple(out)

    return lax.fori_loop(0, n_tiles, body, carry)


def _gather_group(slab_ref, src):
    halves = []
    for h in range(S5_T // BLK):
        acc = slab_ref[BLK * h]
        for s in range(1, BLK):
            acc = jnp.where(src == s, slab_ref[BLK * h + s], acc)
        halves.append(acc)
    return jnp.concatenate(halves, axis=1)


def _s5core_kernel(ul_ref, uc_ref, win_ref, wout_ref, mix_ref, l16_ref, o_ref,
                   u_ref, sre_ref, sim_ref, cre_ref, cim_ref, fre_ref, fim_ref, bre_ref, bim_ref, y_ref):
    nl = N_CHUNK * B
    ncx = N_CCHUNK * B
    blk_l = lax.broadcasted_iota(jnp.int32, (nl, 128), 1) >> H_SHIFT
    blk_c = lax.broadcasted_iota(jnp.int32, (ncx, 128), 1) >> H_SHIFT
    fwd = lax.broadcasted_iota(jnp.int32, (N_CHUNK, 128), 1) < S5_P
    for g0 in range(0, BLK, SCAN_GROUPS):
        for gi in range(SCAN_GROUPS):
            bg = g0 + gi
            win = win_ref[bg]
            src_l = ((blk_l - bg) & (BLK - 1)).astype(F32).astype(BF16)
            src_c = ((blk_c - bg) & (BLK - 1)).astype(F32).astype(BF16)
            u = _gather_group(ul_ref, src_l)
            u_ref[gi] = u
            sl = jnp.dot(u, win, preferred_element_type=F32)
            sc = jnp.dot(_gather_group(uc_ref, src_c), win, preferred_element_type=F32)
            for b in range(B):
                sre_ref[gi, pl.ds(b, N_CHUNK, stride=B), :] = sl[N_CHUNK * b:N_CHUNK * (b + 1), 0:128]
                sim_ref[gi, pl.ds(b, N_CHUNK, stride=B), :] = sl[N_CHUNK * b:N_CHUNK * (b + 1), 128:256]
                cre_ref[gi, pl.ds(b, N_CCHUNK, stride=B), :] = sc[N_CCHUNK * b:N_CCHUNK * (b + 1), 0:128]
                cim_ref[gi, pl.ds(b, N_CCHUNK, stride=B), :] = sc[N_CCHUNK * b:N_CCHUNK * (b + 1), 128:256]
        lams = [(jnp.broadcast_to(l16_ref[g0 + gi, 0:1, :], (8, 128)),
                 jnp.broadcast_to(l16_ref[g0 + gi, 1:2, :], (8, 128))) for gi in range(SCAN_GROUPS)]
        zero = tuple(jnp.zeros((8, 128), F32) for _ in range(2 * SCAN_GROUPS))
        carry = _scan_tiles(cre_ref, cim_ref, None, ncx // 8, zero, lams)
        _scan_tiles(sre_ref, sim_ref, (fre_ref, fim_ref, bre_ref, bim_ref), nl // 8, carry, lams)
        for gi in range(SCAN_GROUPS):
            bg = g0 + gi
            y = jnp.dot(u_ref[gi], mix_ref[bg], preferred_element_type=F32)
            hs = []
            for b in range(B):
                rows = pl.ds(b, N_CHUNK, stride=B)
                hs.append(jnp.concatenate([jnp.where(fwd, fre_ref[gi, rows, :], bre_ref[gi, rows, :]),
                                           jnp.where(fwd, fim_ref[gi, rows, :], bim_ref[gi, rows, :])], axis=1))
            hcat = jnp.concatenate(hs, axis=0).astype(BF16)
            y = y + jnp.dot(hcat, wout_ref[bg], preferred_element_type=F32)
            y_ref[bg] = y.astype(BF16)

    blk = blk_l.astype(F32).astype(BF16)
    for s in range(S5_T):
        h, r = s // BLK, s % BLK
        acc = None
        for j in range(BLK):
            piece = y_ref[(j - r) % BLK, :, 128 * h:128 * (h + 1)]
            acc = piece if acc is None else jnp.where(blk == j, piece, acc)
        o_ref[s] = acc


def _s5core(ul, uc, win, wout, mix, l16):
    TH = S5_T * S5_H
    nl = N_CHUNK * B
    ncx = N_CCHUNK * B
    g4 = lambda r, c: pl.BlockSpec((BLK, r, c), lambda q: (q, 0, 0))
    col = lambda n: pl.BlockSpec((S5_T, n, 128), lambda q: (0, 0, q))
    f32s = lambda n: pltpu.VMEM((SCAN_GROUPS, n, 128), F32)
    return pl.pallas_call(
        _s5core_kernel,
        out_shape=jax.ShapeDtypeStruct((S5_T, nl, S5_W), BF16),
        grid=(S5_G // BLK,),
        in_specs=[col(nl), col(ncx), g4(TH, TH), g4(TH, TH), g4(TH, TH), g4(8, 128)],
        out_specs=col(nl),
        scratch_shapes=[pltpu.VMEM((SCAN_GROUPS, nl, TH), BF16),
                        f32s(nl), f32s(nl), f32s(ncx), f32s(ncx), f32s(nl), f32s(nl), f32s(nl), f32s(nl),
                        pltpu.VMEM((BLK, nl, TH), BF16)],
        compiler_params=_params("arbitrary"),
        name="s5core",
    )(ul, uc, win, wout, mix, l16)


def _cast_on_first_step(src_ref, dst_ref, scale=None):
    @pl.when((pl.program_id(0) == 0) & (pl.program_id(1) == 0))
    def _():
        w = src_ref[...]
        dst_ref[...] = (w if scale is None else scale * w).astype(BF16)


def _const_spec(shape, index):
    return pl.BlockSpec(shape, lambda r, h: index, pipeline_mode=pl.Buffered(1))


def _s5tail_kernel(slat_ref, sza_ref, gluw32_ref, glub_ref, wtop32_ref, y_ref, gluw_ref, wtop_ref):
    _cast_on_first_step(gluw32_ref, gluw_ref)
    _cast_on_first_step(wtop32_ref, wtop_ref)
    unrot = (BLK - pl.program_id(0)) & (BLK - 1)
    for b in range(B):
        rows = slice(N_CHUNK * b, N_CHUNK * (b + 1))
        g = _gelu(_rot_blocks(slat_ref[rows, :].astype(F32), unrot))
        gate = _sigmoid(jnp.dot(g.astype(BF16), gluw_ref[...], preferred_element_type=F32) + glub_ref[...])
        a = (g * gate * sza_ref[rows, :].astype(F32)).astype(BF16)
        y_ref[rows, :] = jnp.dot(a, wtop_ref[...], preferred_element_type=F32).astype(BF16)


def _s5tail(slat, sza, glu_w, glu_b, w_out):
    slab = lambda r, h: r + BLK * h
    sspec = lambda w: pl.BlockSpec((None, N_CHUNK * B, w), lambda r, h: (slab(r, h), 0, 0))
    full = lambda *s: pl.BlockSpec(s, lambda r, h: (0,) * len(s))
    return pl.pallas_call(
        _s5tail_kernel,
        out_shape=jax.ShapeDtypeStruct((S5_T, N_CHUNK * B, D), BF16),
        grid=(BLK, S5_T // BLK),
        in_specs=[sspec(512), sspec(512), _const_spec((S5_W, S5_W), (0, 0)), full(1, 512),
                  _const_spec((S5_W, D), (0, 0))],
        out_specs=sspec(D),
        scratch_shapes=[pltpu.VMEM((S5_W, S5_W), BF16), pltpu.VMEM((S5_W, D), BF16)],
        compiler_params=_params("arbitrary", "arbitrary"),
        name="s5tail",
    )(slat, sza, glu_w, glu_b, w_out)


def _tail0_kernel(x_ref, ys5_ref, guz_ref, vln_ref, mod_ref, sguw_ref, sgub_ref, wbot32_ref, ng_ref, nb_ref, o_ref,
                  wbot_ref):
    _cast_on_first_step(wbot32_ref, wbot_ref)
    tm = x_ref.shape[0]
    lane = lax.broadcasted_iota(jnp.int32, (SGU_CHUNK, 128), 1)
    lo = lane < SGU_HD
    zero = jnp.zeros((SGU_CHUNK, 128), BF16)
    w_pair = [jnp.concatenate([sguw_ref[2 * pi].astype(BF16), sguw_ref[2 * pi + 1].astype(BF16)], axis=1)
              for pi in range(SGU_HEADS // 2)]
    chunks = []
    for ci in range(tm // SGU_CHUNK):
        v = vln_ref[ci * SGU_CHUNK:(ci + 1) * SGU_CHUNK, :]
        cols = []
        for pi in range(SGU_HEADS // 2):
            vp = v[:, 128 * pi:128 * (pi + 1)]
            bm = jnp.concatenate([jnp.where(lo, vp, zero), jnp.where(lo, zero, vp)], axis=0)
            cols.append(jnp.dot(w_pair[pi], bm, preferred_element_type=F32))
        chunks.append(jnp.concatenate(cols, axis=1) + sgub_ref[...])
    s = jnp.concatenate(chunks, axis=0)
    bsg = (guz_ref[...].astype(F32) * s).astype(BF16)
    perm = _chunk_transpose_perm()
    ys5 = jnp.concatenate(
        [jnp.dot(perm, ys5_ref[:, S5_T * j:S5_T * (j + 1), :].reshape(PERM_ROWS, D), preferred_element_type=F32)
         for j in range(tm // PERM_ROWS)], axis=0)
    y = ys5 + jnp.dot(bsg, wbot_ref[...], preferred_element_type=F32)
    gmod = mod_ref[:, 2 * D:3 * D]
    o_ref[...] = _layer_norm(DN_ALPHA * x_ref[...] + gmod * y, ng_ref[0:1, :], nb_ref[0:1, :])


def _tail0(x, ys5, guz, vln, mod, sguw, sgub, w_bot, ng, nb, tm=TOKEN_TILE):
    nct = tm // S5_T
    t512 = pl.BlockSpec((None, tm, 512), lambda b, i: (b, i, 0))
    tD = pl.BlockSpec((None, tm, D), lambda b, i: (b, i, 0))
    full = lambda *s: pl.BlockSpec(s, lambda b, i: (0,) * len(s))
    return pl.pallas_call(
        _tail0_kernel,
        out_shape=jax.ShapeDtypeStruct((B, L, D), F32),
        grid=(B, L // tm),
        in_specs=[tD, pl.BlockSpec((S5_T, nct, D), lambda b, i: (0, b * (N_CHUNK // nct) + i, 0)), t512, t512,
                  _mod_spec(0),
                  full(SGU_HEADS, SGU_CHUNK, SGU_CHUNK), full(SGU_CHUNK, 512),
                  _const_spec((SGU_W, D), (1, 0)), full(DEPTH, D), full(DEPTH, D)],
        out_specs=tD,
        scratch_shapes=[pltpu.VMEM((SGU_W, D), BF16)],
        compiler_params=_params("arbitrary", "arbitrary"),
        name="tail0",
    )(x, ys5, guz, vln, mod, sguw, sgub, w_bot, ng, nb)


CONV_C = D // 2
TILE_ROWS = TOKEN_TILE // GRID_W


def _grid_transpose_in(v, o_ref):
    perm = _chunk_transpose_perm()
    for q in range(GRID_W // S5_T):
        seg = jnp.concatenate([v[GRID_W * r + S5_T * q:GRID_W * r + S5_T * (q + 1), :] for r in range(TILE_ROWS)],
                              axis=0)
        t = jnp.dot(perm, seg, preferred_element_type=F32).astype(BF16)
        o_ref[S5_T * q:S5_T * (q + 1), :, :] = t.reshape(S5_T, TILE_ROWS, v.shape[1])


def _inproj1_kernel(x_ref, mod_ref, w_ref, hgr_ref, hgc_ref, h_ref):
    shift = mod_ref[:, 0:D]
    scale = mod_ref[:, D:2 * D]
    h = (x_ref[...] * (1.0 + scale) + shift).astype(BF16)
    h_ref[...] = h
    dot = lambda lo: jnp.dot(h, w_ref[:, lo:lo + CONV_C].astype(BF16), preferred_element_type=F32)
    hgr_ref[...] = (dot(0) * _sigmoid(dot(D))).astype(BF16)
    _grid_transpose_in((dot(CONV_C) * _sigmoid(dot(D + CONV_C))).astype(BF16), hgc_ref)


def _inproj1(x, mod, w_in_f32, tm=TOKEN_TILE):
    tile = lambda w: pl.BlockSpec((None, tm, w), lambda b, i: (b, i, 0))
    return pl.pallas_call(
        _inproj1_kernel,
        out_shape=(jax.ShapeDtypeStruct((B, L, CONV_C), BF16),
                   jax.ShapeDtypeStruct((B, GRID_W, GRID_W, CONV_C), BF16),
                   jax.ShapeDtypeStruct((B, L, D), BF16)),
        grid=(B, L // tm),
        in_specs=[tile(D),
                  _mod_spec(1),
                  pl.BlockSpec((D, 2 * D), lambda b, i: (0, 0), pipeline_mode=pl.Buffered(1))],
        out_specs=(tile(CONV_C), pl.BlockSpec((None, GRID_W, TILE_ROWS, CONV_C), lambda b, i: (b, 0, i, 0)),
                   tile(D)),
        compiler_params=_params("arbitrary", "arbitrary"),
        name="inproj1",
    )(x, mod, w_in_f32)


DFT_N = 2 * GRID_W
TAPS_PAD = CONV_K + 1


def _dft_constants():
    th = 2.0 * math.pi / DFT_N
    f = np.arange(GRID_W, dtype=np.float64)[:, None]
    p = np.arange(GRID_W, dtype=np.float64)[None, :]
    cosm = np.cos(th * f * p)
    sinm = np.sin(th * f * p)
    alt = np.where(np.arange(GRID_W) % 2 == 0, 1.0, -1.0)
    fwd = np.concatenate([cosm, alt[None, :], sinm[1:]], axis=0)
    cf = np.where(np.arange(GRID_W) == 0, 1.0, 2.0) / DFT_N
    inv = np.concatenate([cosm.T * cf[None, :], (alt / DFT_N)[:, None], sinm.T[:, 1:] * (2.0 / DFT_N)], axis=1)
    sft = (CONV_HALF - np.arange(TAPS_PAD, dtype=np.float64))[None, :]
    live = (np.arange(TAPS_PAD) < CONV_K).astype(np.float64)[None, :]
    f64 = np.where(f == 0, float(GRID_W), f)
    f32 = lambda a: jnp.asarray(a.astype(np.float32))
    return (f32(fwd).astype(BF16), f32(inv).astype(BF16),
            f32(np.cos(th * f * sft) * live), f32(np.sin(th * f * sft) * live), f32(np.cos(th * f64 * sft) * live))


def _fconv_kernel(h_ref, w_ref, b_ref, fwd_ref, inv_ref, c1_ref, s3_ref, c4_ref, o_ref, taps_ref):
    hp = lax.Precision.HIGHEST
    taps_ref[...] = jnp.zeros(taps_ref.shape, F32)
    taps_ref[0:CONV_K, :] = w_ref[...]
    taps = taps_ref[...]
    g_re = jnp.dot(c1_ref[...], taps, preferred_element_type=F32, precision=hp)
    g_im = jnp.dot(s3_ref[...], taps, preferred_element_type=F32, precision=hp)
    g_r2 = jnp.dot(c4_ref[...], taps, preferred_element_type=F32, precision=hp)
    fwd = fwd_ref[...]
    inv = inv_ref[...]
    bias = b_ref[...]
    n_runs = h_ref.shape[0] // GRID_W
    rows = lambda r: slice(GRID_W * r, GRID_W * (r + 1))
    forward = lambda r: jnp.dot(fwd, h_ref[rows(r), :], preferred_element_type=F32)
    ahead = 2
    specs = [forward(r) for r in range(ahead)]
    for r in range(n_runs):
        if r + ahead < n_runs:
            specs.append(forward(r + ahead))
        spec = specs[r]
        a, bm = spec[0:GRID_W], spec[GRID_W:DFT_N]
        prod = jnp.concatenate([a * g_re - bm * g_im, a * g_im + bm * g_r2], axis=0).astype(BF16)
        o_ref[rows(r), :] = (jnp.dot(inv, prod, preferred_element_type=F32) + bias).astype(BF16)


def _fconv(h, taps, bias, half, consts, tm=L):
    fwd, inv, c1, s3, c4 = consts
    c = h.shape[-1]
    tile = pl.BlockSpec((None, tm, c), lambda b, i: (b, i, 0))
    full = lambda *s: pl.BlockSpec(s, lambda b, i: (0,) * len(s))
    cols = lambda r: pl.BlockSpec((None, r, c), lambda b, i: (0, 0, half))
    return pl.pallas_call(
        _fconv_kernel,
        out_shape=jax.ShapeDtypeStruct(h.shape, BF16),
        grid=(B, L // tm),
        in_specs=[tile, cols(CONV_K), cols(1), full(DFT_N, GRID_W), full(GRID_W, DFT_N),
                  full(GRID_W, TAPS_PAD), full(GRID_W, TAPS_PAD), full(GRID_W, TAPS_PAD)],
        out_specs=tile,
        scratch_shapes=[pltpu.VMEM((TAPS_PAD, c), F32)],
        compiler_params=_params("arbitrary", "arbitrary"),
        name="fconv",
    )(h, taps, bias, fwd, inv, c1, s3, c4)


ROW_BLOCK = 32


def _row_blocks(n_rows):
    return [slice(ROW_BLOCK * k, ROW_BLOCK * (k + 1)) for k in range(n_rows // ROW_BLOCK)]


def _tail1_kernel(x_ref, hcr_ref, hcc_ref, h1_ref, wz32_ref, mod_ref, lng_ref, lnb_ref, wout_ref, ng_ref, nb_ref,
                  o_ref, col_ref, z_ref, m_ref, y_ref, wz_ref):
    _cast_on_first_step(wz32_ref, wz_ref, scale=0.5)
    tm = x_ref.shape[0]
    perm = _chunk_transpose_perm()
    for q in range(GRID_W // S5_T):
        blk = hcc_ref[S5_T * q:S5_T * (q + 1), :, :].reshape(PERM_ROWS, CONV_C)
        t = jnp.dot(perm, blk, preferred_element_type=F32)
        for r in range(TILE_ROWS):
            col_ref[GRID_W * r + S5_T * q:GRID_W * r + S5_T * (q + 1), :] = t[S5_T * r:S5_T * (r + 1), :]
    z_ref[...] = jnp.dot(h1_ref[...], wz_ref[...], preferred_element_type=F32)
    lng, lnb = 0.5 * lng_ref[...], 0.5 * lnb_ref[...]
    for rows in _row_blocks(tm):
        hc = jnp.concatenate([hcr_ref[rows, :].astype(F32), col_ref[rows, :]], axis=1)
        m_ref[rows, :] = (_silu_of_half(_layer_norm(hc, lng, lnb)) * _silu_of_half(z_ref[rows, :])).astype(BF16)
    y_ref[...] = jnp.dot(m_ref[...], wout_ref[...], preferred_element_type=F32)
    gmod = mod_ref[:, 2 * D:3 * D]
    ng, nb = ng_ref[1:2, :], nb_ref[1:2, :]
    for rows in _row_blocks(tm):
        o_ref[rows, :] = _layer_norm(DN_ALPHA * x_ref[rows, :] + gmod * y_ref[rows, :], ng, nb)


def _tail1(x, hc_row, hc_col, h1, w_in_f32, mod, ln_g, ln_b, w_out, ng, nb, tm=TOKEN_TILE):
    tile = lambda w: pl.BlockSpec((None, tm, w), lambda b, i: (b, i, 0))
    full = lambda *s: pl.BlockSpec(s, lambda b, i: (0,) * len(s))
    return pl.pallas_call(
        _tail1_kernel,
        out_shape=jax.ShapeDtypeStruct((B, L, D), F32),
        grid=(B, L // tm),
        in_specs=[tile(D), tile(CONV_C),
                  pl.BlockSpec((None, GRID_W, TILE_ROWS, CONV_C), lambda b, i: (b, 0, i, 0)),
                  tile(D), _const_spec((D, D), (0, 2)), _mod_spec(1),
                  full(1, D), full(1, D), full(D, D), full(DEPTH, D), full(DEPTH, D)],
        out_specs=tile(D),
        scratch_shapes=[pltpu.VMEM((tm, CONV_C), F32), pltpu.VMEM((tm, D), F32), pltpu.VMEM((tm, D), BF16),
                        pltpu.VMEM((tm, D), F32), pltpu.VMEM((D, D), BF16)],
        compiler_params=_params("arbitrary", "arbitrary"),
        name="tail1",
    )(x, hc_row, hc_col, h1, w_in_f32, mod, ln_g, ln_b, w_out, ng, nb)


def kernel(x, c, ctx, c_ctx, mod_w, mod_b, norm_g, norm_b, ev_w_in, ev_w_out, s5_lam_re, s5_lam_im, s5_log_dt, s5_b_re, s5_b_im, s5_c_re, s5_c_im, s5_d, glu_w, glu_b, sgu_ln_g, sgu_ln_b, sgu_w, sgu_b, od_w_in, od_w_out, dw_w, dw_b, conv_ln_g, conv_ln_b):
    TH = S5_T * S5_H
    row = lambda v: v.reshape(1, -1)

    mods = _adaln(c, c_ctx, mod_w, mod_b)

    ldt = jnp.broadcast_to(s5_log_dt[0][:, :, None], (2, S5_G, S5_P))
    fb = lambda s: jnp.concatenate([s[:, 0], s[:, 1]], axis=-1)
    lam3 = fb(jnp.stack([s5_lam_re[0], s5_lam_im[0], ldt]))
    bt = fb(jnp.swapaxes(jnp.stack([s5_b_re[0], s5_b_im[0]]), -1, -2))
    cn = fb(jnp.stack([s5_c_re[0], s5_c_im[0]]))
    win, wout, mix, l16 = _s5_weights(lam3, s5_d, bt, cn)

    guz, vln, hs = _inproj0n(x, mods, ev_w_in[0], row(sgu_ln_g[0]), row(sgu_ln_b[0]))
    ua, sza, ua_c = _inproj0a(hs, _ctx_slabs(ctx, mods), ev_w_in[0])
    s_lat = _s5core(ua, ua_c, win, wout, mix, l16)
    y_s5 = _s5tail(s_lat, sza, glu_w[0], row(glu_b[0]), ev_w_out[0])
    sgub = jnp.repeat(sgu_b[0].T, SGU_HD, axis=1)
    x1 = _tail0(x, y_s5, guz, vln, mods, sgu_w[0], sgub, ev_w_out[0], norm_g, norm_b)

    hg_row, hg_col, h1 = _inproj1(x1, mods, od_w_in[0])
    consts = _dft_constants()
    bias = dw_b.reshape(1, 1, 2 * CONV_C)
    hc_row = _fconv(hg_row, dw_w, bias, 0, consts)
    hc_col = _fconv(hg_col.reshape(B, L, CONV_C), dw_w, bias, 1, consts)
    return _tail1(x1, hc_row, hc_col.reshape(B, GRID_W, GRID_W, CONV_C), h1, od_w_in[0], mods,
                  row(conv_ln_g[0]), row(conv_ln_b[0]), od_w_out[0].astype(BF16), norm_g, norm_b)
```
